```python
import jax, jax.numpy as jnp
from jax import lax
import numpy as np

D_MODEL = 1024
BATCH = 8
SEQ = 4096
DEPTH = 1

GRID_W = 64
CTX_LEN = 256
HEAD_DIM = 64
N_HEADS_A = 8
N_KV_A = 2
GQA_GROUP = N_HEADS_A // N_KV_A
N_HEADS_B = 8
WIDTH_A = N_HEADS_A * HEAD_DIM
WIDTH_KV_A = N_KV_A * HEAD_DIM
WIDTH_B = N_HEADS_B * HEAD_DIM
MIX_WIDTH = WIDTH_A + WIDTH_B
IN_WIDTH = WIDTH_A + 2 * WIDTH_KV_A + 3 * WIDTH_B
WINDOW = 128
BLOCK_A = 128
NA_ROWS = 8
NA_COLS = 16
NA_Q_COLS = 16
NA_K_COLS = 32
N_EXPERTS = 32
TOP_K = 4
D_FF_EXPERT = D_MODEL
SWIGLU_LIMIT = 7.0
SWIGLU_ALPHA = 1.702
MOE_BLOCK = 128
ROPE_BASE = 10000.0
EPS = 1e-6
NEG_INF = -1e30

kernel_name = 'hybrid_window_natten_moe_dit_block'


def rmsnorm(x, g):
    xf = x.astype(jnp.float32)
    y = xf * lax.rsqrt(jnp.mean(xf * xf, axis=-1, keepdims=True) + EPS)
    return (y * g.astype(jnp.float32)).astype(x.dtype)


def modulate(x, shift, scale):
    return x * (1 + scale) + shift


def rope_1d(x, ang):
    n = x.shape[-1] // 2
    x1, x2 = x[..., :n], x[..., n:]
    cos = jnp.cos(ang)[:, None, :]
    sin = jnp.sin(ang)[:, None, :]
    return jnp.concatenate([x1 * cos - x2 * sin, x2 * cos + x1 * sin], axis=-1)


def axial_rope(x, ang_row, ang_col):
    half = HEAD_DIM // 2
    xf = x.astype(jnp.float32)
    out = jnp.concatenate([rope_1d(xf[..., :half], ang_row), rope_1d(xf[..., half:], ang_col)], axis=-1)
    return out.astype(x.dtype)


def split_projection(p):
    B, T, _ = p.shape
    cuts = [WIDTH_A, WIDTH_A + WIDTH_KV_A, WIDTH_A + 2 * WIDTH_KV_A,
            WIDTH_A + 2 * WIDTH_KV_A + WIDTH_B, WIDTH_A + 2 * WIDTH_KV_A + 2 * WIDTH_B]
    qa, ka, va, qb, kb, vb = jnp.split(p, cuts, axis=-1)
    qa = qa.reshape(B, T, N_HEADS_A, HEAD_DIM)
    ka = ka.reshape(B, T, N_KV_A, HEAD_DIM)
    va = va.reshape(B, T, N_KV_A, HEAD_DIM)
    qb = qb.reshape(B, T, N_HEADS_B, HEAD_DIM)
    kb = kb.reshape(B, T, N_HEADS_B, HEAD_DIM)
    vb = vb.reshape(B, T, N_HEADS_B, HEAD_DIM)
    return qa, ka, va, qb, kb, vb


def window_attention(q, k, v, kc, vc, sink):
    B, L = q.shape[:2]
    C = kc.shape[1]
    nb = L // BLOCK_A
    span = BLOCK_A + 2 * WINDOW
    scale = HEAD_DIM ** -0.5
    kp = jnp.pad(k, ((0, 0), (WINDOW, WINDOW), (0, 0), (0, 0)))
    vp = jnp.pad(v, ((0, 0), (WINDOW, WINDOW), (0, 0), (0, 0)))
    qb = q.reshape(B, nb, BLOCK_A, N_KV_A, GQA_GROUP, HEAD_DIM)
    r = jnp.arange(BLOCK_A)
    j = jnp.arange(span)
    band = jnp.abs(j[None, :] - WINDOW - r[:, None]) <= WINDOW
    sink_g = sink.astype(jnp.float32).reshape(N_KV_A, GQA_GROUP)[None, :, :, None, None]

    def block(i):
        qi = lax.dynamic_index_in_dim(qb, i, axis=1, keepdims=False)
        ki = lax.dynamic_slice_in_dim(kp, i * BLOCK_A, span, axis=1)
        vi = lax.dynamic_slice_in_dim(vp, i * BLOCK_A, span, axis=1)
        kpos = i * BLOCK_A - WINDOW + j
        valid = band & (kpos >= 0)[None, :] & (kpos < L)[None, :]
        s_loc = jnp.einsum('bqkgd,bskd->bkgqs', qi, ki, preferred_element_type=jnp.float32) * scale
        s_loc = jnp.where(valid, s_loc, NEG_INF)
        s_ctx = jnp.einsum('bqkgd,bckd->bkgqc', qi, kc, preferred_element_type=jnp.float32) * scale
        s_sink = jnp.broadcast_to(sink_g, s_ctx.shape[:-1] + (1,))
        p = jax.nn.softmax(jnp.concatenate([s_loc, s_ctx, s_sink], axis=-1), axis=-1)
        p_loc = p[..., :span].astype(v.dtype)
        p_ctx = p[..., span:span + C].astype(v.dtype)
        o = jnp.einsum('bkgqs,bskd->bqkgd', p_loc, vi) + jnp.einsum('bkgqc,bckd->bqkgd', p_ctx, vc)
        return o.reshape(B, BLOCK_A, WIDTH_A)

    out = lax.map(block, jnp.arange(nb))
    return jnp.moveaxis(out, 0, 1).reshape(B, L, WIDTH_A)


def neighbourhood_attention(q, k, v, kc, vc, rpb):
    B, L = q.shape[:2]
    rows_n = L // GRID_W
    kr = min(NA_ROWS, rows_n)
    scale = HEAD_DIM ** -0.5
    qg = q.reshape(B, rows_n, GRID_W, N_HEADS_B, HEAD_DIM)
    kg = k.reshape(B, rows_n, GRID_W, N_HEADS_B, HEAD_DIM)
    vg = v.reshape(B, rows_n, GRID_W, N_HEADS_B, HEAD_DIM)
    n_cb = GRID_W // NA_Q_COLS
    qcol = np.arange(GRID_W)
    col_start = np.clip(qcol - NA_COLS // 2, 0, GRID_W - NA_COLS).reshape(n_cb, NA_Q_COLS)
    blk_start = np.clip(np.arange(n_cb) * NA_Q_COLS - NA_COLS // 2, 0, GRID_W - NA_K_COLS)
    key_cols = blk_start[:, None] + np.arange(NA_K_COLS)
    qcols = qcol.reshape(n_cb, NA_Q_COLS)
    kc3 = key_cols[:, None, :]
    cs3 = col_start[:, :, None]
    col_ok = (kc3 >= cs3) & (kc3 < cs3 + NA_COLS)
    dcol_idx = np.clip(kc3 - qcols[:, :, None] + NA_COLS - 1, 0, 2 * NA_COLS - 2)
    mask = jnp.asarray(col_ok)[:, :, None, :]

    def row_block(r):
        rs = jnp.clip(r - kr // 2, 0, rows_n - kr)
        qr = lax.dynamic_index_in_dim(qg, r, axis=1, keepdims=False).reshape(B, n_cb, NA_Q_COLS, N_HEADS_B, HEAD_DIM)
        k_rows = lax.dynamic_slice_in_dim(kg, rs, kr, axis=1)
        v_rows = lax.dynamic_slice_in_dim(vg, rs, kr, axis=1)
        kblk = k_rows[:, :, key_cols]
        vblk = v_rows[:, :, key_cols]
        drow_idx = rs - r + jnp.arange(kr) + NA_ROWS - 1
        bias = rpb[:, drow_idx][:, :, dcol_idx].astype(jnp.float32)
        bias = jnp.transpose(bias, (0, 2, 3, 1, 4))
        s = jnp.einsum('bnqhd,banchd->bhnqac', qr, kblk, preferred_element_type=jnp.float32) * scale + bias[None]
        s = jnp.where(mask, s, NEG_INF).reshape(B, N_HEADS_B, n_cb, NA_Q_COLS, kr * NA_K_COLS)
        s_ctx = jnp.einsum('bnqhd,bchd->bhnqc', qr, kc, preferred_element_type=jnp.float32) * scale
        p = jax.nn.softmax(jnp.concatenate([s, s_ctx], axis=-1), axis=-1)
        n_loc = kr * NA_K_COLS
        p_loc = p[..., :n_loc].reshape(B, N_HEADS_B, n_cb, NA_Q_COLS, kr, NA_K_COLS).astype(v.dtype)
        p_ctx = p[..., n_loc:].astype(v.dtype)
        o = jnp.einsum('bhnqac,banchd->bnqhd', p_loc, vblk) + jnp.einsum('bhnqc,bchd->bnqhd', p_ctx, vc)
        return o.reshape(B, GRID_W, WIDTH_B)

    out = lax.map(row_block, jnp.arange(rows_n))
    return jnp.moveaxis(out, 0, 1).reshape(B, L, WIDTH_B)


def context_attention(q, k, v, sink):
    B, C, Hq, _ = q.shape
    Hkv = k.shape[2]
    G = Hq // Hkv
    qg = q.reshape(B, C, Hkv, G, HEAD_DIM)
    s = jnp.einsum('bqkgd,bckd->bkgqc', qg, k, preferred_element_type=jnp.float32) * HEAD_DIM ** -0.5
    if sink is not None:
        sink_g = sink.astype(jnp.float32).reshape(Hkv, G)[None, :, :, None, None]
        s = jnp.concatenate([s, jnp.broadcast_to(sink_g, s.shape[:-1] + (1,))], axis=-1)
    p = jax.nn.softmax(s, axis=-1)[..., :C].astype(v.dtype)
    o = jnp.einsum('bkgqc,bckd->bqkgd', p, v)
    return o.reshape(B, C, Hq * HEAD_DIM)


def moe_ffn(h, w_router, b_router, w1, b1, w2, b2):
    B, T, D = h.shape
    n_tok = B * T
    t = h.reshape(n_tok, D)
    logits = jnp.dot(t, w_router, preferred_element_type=jnp.float32) + b_router.astype(jnp.float32)
    top_val, top_idx = lax.top_k(logits, TOP_K)
    top_w = jax.nn.softmax(top_val, axis=-1)
    n_assign = n_tok * TOP_K
    flat_e = top_idx.reshape(-1)
    flat_tok = jnp.repeat(jnp.arange(n_tok, dtype=jnp.int32), TOP_K)
    flat_w = top_w.reshape(-1)
    order = jnp.argsort(flat_e)
    sorted_e = flat_e[order]
    counts = jnp.bincount(flat_e, length=N_EXPERTS)
    padded = (counts + MOE_BLOCK - 1) // MOE_BLOCK * MOE_BLOCK
    pad_end = jnp.cumsum(padded)
    pad_start = pad_end - padded
    grp_start = jnp.cumsum(counts) - counts
    dest = pad_start[sorted_e] + (jnp.arange(n_assign) - grp_start[sorted_e])
    n_blocks = -(-n_assign // MOE_BLOCK) + N_EXPERTS
    n_rows = n_blocks * MOE_BLOCK
    row_tok = jnp.zeros((n_rows,), jnp.int32).at[dest].set(flat_tok[order])
    row_w = jnp.zeros((n_rows,), jnp.float32).at[dest].set(flat_w[order])
    blk_expert = jnp.minimum(jnp.searchsorted(pad_end, jnp.arange(n_blocks) * MOE_BLOCK, side='right'), N_EXPERTS - 1)

    def expert_block(args):
        idx, e = args
        xb = t[idx]
        hu = xb @ w1[e] + b1[e]
        glu = jnp.minimum(hu[:, 0::2], SWIGLU_LIMIT)
        lin = jnp.clip(hu[:, 1::2], -SWIGLU_LIMIT, SWIGLU_LIMIT)
        act = glu * jax.nn.sigmoid(SWIGLU_ALPHA * glu) * (lin + 1)
        return act @ w2[e] + b2[e]

    y = lax.map(expert_block, (row_tok.reshape(n_blocks, MOE_BLOCK), blk_expert))
    y = y.reshape(n_rows, D) * row_w[:, None].astype(y.dtype)
    out = jax.ops.segment_sum(y, row_tok, num_segments=n_tok)
    return out.reshape(B, T, D)


def setup_inputs(seed: int = 0) -> dict:
    key = jax.random.key(seed)
    ks = jax.random.split(key, 22)
    f32 = jnp.float32

    def nrm(k, shape, s):
        return jax.random.normal(k, shape, f32) * s

    def gain(k, shape):
        return 1.0 + 0.05 * jax.random.normal(k, shape, f32)

    return {
        'x': nrm(ks[0], (BATCH, SEQ, D_MODEL), 1.0),
        'c': nrm(ks[1], (BATCH, D_MODEL), 1.0),
        'ctx': nrm(ks[2], (BATCH, CTX_LEN, D_MODEL), 1.0),
        'c_ctx': nrm(ks[3], (D_MODEL,), 1.0),
        'w_ada': nrm(ks[4], (DEPTH, D_MODEL, 6 * D_MODEL), 0.5 * D_MODEL ** -0.5),
        'b_ada': nrm(ks[5], (DEPTH, 6 * D_MODEL), 0.01),
        'g_pre_mix': gain(ks[6], (DEPTH, D_MODEL)),
        'g_post_mix': gain(ks[7], (DEPTH, D_MODEL)),
        'g_pre_ffn': gain(ks[8], (DEPTH, D_MODEL)),
        'g_post_ffn': gain(ks[9], (DEPTH, D_MODEL)),
        'w_in': nrm(ks[10], (DEPTH, D_MODEL, IN_WIDTH), D_MODEL ** -0.5),
        'g_grp_a': gain(ks[11], (DEPTH, WIDTH_A)),
        'g_grp_b': gain(ks[12], (DEPTH, WIDTH_B)),
        'sink_a': nrm(ks[13], (DEPTH, N_HEADS_A), 0.5),
        'rpb_b': nrm(ks[14], (DEPTH, N_HEADS_B, 2 * NA_ROWS - 1, 2 * NA_COLS - 1), 0.2),
        'w_out': nrm(ks[15], (DEPTH, MIX_WIDTH, D_MODEL), MIX_WIDTH ** -0.5),
        'w_router': nrm(ks[16], (DEPTH, D_MODEL, N_EXPERTS), D_MODEL ** -0.5),
        'b_router': nrm(ks[17], (DEPTH, N_EXPERTS), 0.01),
        'w_mlp1': nrm(ks[18], (DEPTH, N_EXPERTS, D_MODEL, 2 * D_FF_EXPERT), D_MODEL ** -0.5),
        'b_mlp1': nrm(ks[19], (DEPTH, N_EXPERTS, 2 * D_FF_EXPERT), 0.01),
        'w_mlp2': nrm(ks[20], (DEPTH, N_EXPERTS, D_FF_EXPERT, D_MODEL), D_FF_EXPERT ** -0.5),
        'b_mlp2': nrm(ks[21], (DEPTH, N_EXPERTS, D_MODEL), 0.01),
    }


def reference(x, c, ctx, c_ctx, w_ada, b_ada, g_pre_mix, g_post_mix, g_pre_ffn, g_post_ffn,
              w_in, g_grp_a, g_grp_b, sink_a, rpb_b, w_out, w_router, b_router,
              w_mlp1, b_mlp1, w_mlp2, b_mlp2):
    B, L, _ = x.shape
    pos = jnp.arange(L)
    rows = (pos // GRID_W).astype(jnp.float32)
    cols = (pos % GRID_W).astype(jnp.float32)
    n_freq = HEAD_DIM // 4
    freqs = ROPE_BASE ** (-jnp.arange(n_freq, dtype=jnp.float32) / n_freq)
    ang_row = rows[:, None] * freqs[None, :]
    ang_col = cols[:, None] * freqs[None, :]

    for layer in range(DEPTH):
        mod = (jax.nn.silu(c) @ w_ada[layer] + b_ada[layer])[:, None, :]
        mod_c = jax.nn.silu(c_ctx) @ w_ada[layer] + b_ada[layer]
        sh1, sc1, ga1, sh2, sc2, ga2 = jnp.split(mod, 6, axis=-1)
        csh1, csc1, cga1, csh2, csc2, cga2 = jnp.split(mod_c, 6, axis=-1)

        h = modulate(rmsnorm(x, g_pre_mix[layer]), sh1, sc1)
        hc = modulate(rmsnorm(ctx, g_pre_mix[layer]), csh1, csc1)
        qa, ka, va, qb, kb, vb = split_projection(h @ w_in[layer])
        qac, kac, vac, qbc, kbc, vbc = split_projection(hc @ w_in[layer])
        qa = axial_rope(qa, ang_row, ang_col)
        ka = axial_rope(ka, ang_row, ang_col)
        oa = window_attention(qa, ka, va, kac, vac, sink_a[layer])
        ob = neighbourhood_attention(qb, kb, vb, kbc, vbc, rpb_b[layer])
        mix = jnp.concatenate([rmsnorm(oa, g_grp_a[layer]), rmsnorm(ob, g_grp_b[layer])], axis=-1) @ w_out[layer]
        x = x + ga1 * rmsnorm(mix, g_post_mix[layer])

        h2 = modulate(rmsnorm(x, g_pre_ffn[layer]), sh2, sc2)
        y = moe_ffn(h2, w_router[layer], b_router[layer], w_mlp1[layer], b_mlp1[layer], w_mlp2[layer], b_mlp2[layer])
        x = x + ga2 * rmsnorm(y, g_post_ffn[layer])

        if layer < DEPTH - 1:
            oac = context_attention(qac, kac, vac, sink_a[layer])
            obc = context_attention(qbc, kbc, vbc, None)
            mixc = jnp.concatenate([rmsnorm(oac, g_grp_a[layer]), rmsnorm(obc, g_grp_b[layer])], axis=-1) @ w_out[layer]
            ctx = ctx + cga1 * rmsnorm(mixc, g_post_mix[layer])
            h2c = modulate(rmsnorm(ctx, g_pre_ffn[layer]), csh2, csc2)
            yc = moe_ffn(h2c, w_router[layer], b_router[layer], w_mlp1[layer], b_mlp1[layer], w_mlp2[layer], b_mlp2[layer])
            ctx = ctx + cga2 * rmsnorm(yc, g_post_ffn[layer])
    return x
```

```python
import functools

import numpy as np
import jax
import jax.numpy as jnp
from jax import lax
from jax.experimental import pallas as pl
from jax.experimental.pallas import tpu as pltpu

GRID_W = 64
HEAD_DIM = 64
N_HEADS_A = 8
N_KV_A = 2
GQA_GROUP = N_HEADS_A // N_KV_A
N_HEADS_B = 8
WIDTH_A = N_HEADS_A * HEAD_DIM
WIDTH_KV_A = N_KV_A * HEAD_DIM
WIDTH_B = N_HEADS_B * HEAD_DIM
WINDOW = 128
BLOCK_A = 128
NA_ROWS = 8
NA_COLS = 16
N_EXPERTS = 32
TOP_K = 4
SWIGLU_LIMIT = 7.0
SWIGLU_ALPHA = 1.702
ROPE_BASE = 10000.0
EPS = 1e-6
NEG_INF = -1e30

LANES = 128
VMEM_LIMIT = 56 * 1024 * 1024

NB_Q_ROWS = 2
NB_K_ROWS = NB_Q_ROWS + NA_ROWS
PROJ_TILE = 512
OUT_TILE = 256
FFN_TILE = 256
FFN_CHUNK = 256
DISPATCH_TILE = 256
COMBINE_TILE = 128


def _params(*sem):
    return pltpu.CompilerParams(dimension_semantics=sem, vmem_limit_bytes=VMEM_LIMIT)


def _rms(x, g):
    return x * lax.rsqrt(jnp.mean(x * x, axis=-1, keepdims=True) + EPS) * g


def _dot(a, b):
    return jnp.dot(a, b, preferred_element_type=jnp.float32)


def _dot_nt(a, b):
    return lax.dot_general(a, b, (((1,), (1,)), ((), ())), preferred_element_type=jnp.float32)


def _ada_kernel(c_ref, w_ref, b_ref, o_ref):
    c = c_ref[...]
    s = (c * jax.nn.sigmoid(c)).astype(jnp.bfloat16)
    o_ref[...] = _dot(s, w_ref[...].astype(jnp.bfloat16)) + b_ref[...]


def _ada(cc, w_ada, b_ada):
    rows, d = cc.shape
    n_out = w_ada.shape[1]
    return pl.pallas_call(
        _ada_kernel,
        grid=(n_out // d,),
        in_specs=[pl.BlockSpec((rows, d), lambda j: (0, 0)),
                  pl.BlockSpec((d, d), lambda j: (0, j)),
                  pl.BlockSpec((1, d), lambda j: (0, j))],
        out_specs=pl.BlockSpec((rows, d), lambda j: (0, j)),
        out_shape=jax.ShapeDtypeStruct((rows, n_out), jnp.float32),
        compiler_params=_params("arbitrary"),
        name="ada",
    )(cc, w_ada, b_ada.reshape(1, n_out))


def _rope(x, cos, sin):
    w = x.shape[1]
    reps = w // LANES
    if reps > 1:
        cos = jnp.concatenate([cos] * reps, axis=1)
        sin = jnp.concatenate([sin] * reps, axis=1)
    lane = lax.broadcasted_iota(jnp.int32, x.shape, 1)
    quarter = HEAD_DIM // 4
    partner = jnp.where(lane % (2 * quarter) < quarter,
                        pltpu.roll(x, w - quarter, 1), pltpu.roll(x, quarter, 1))
    return x * cos + partner * sin


def _inproj_kernel(x_ref, sh_ref, sc_ref, g_ref, w_ref, *rest, latent):
    x = x_ref[...]
    h = _rms(x, g_ref[...]) * (1.0 + sc_ref[0]) + sh_ref[0]
    p = _dot(h.astype(jnp.bfloat16), w_ref[...])
    if latent:
        cos_ref, sin_ref, qa_ref, ka_ref, va_ref, qb_ref, kb_ref, vb_ref = rest
        cos, sin = cos_ref[...], sin_ref[...]
        scale = HEAD_DIM ** -0.5
        o = 0
        qa_ref[...] = (_rope(p[:, o:o + WIDTH_A], cos, sin) * scale).astype(qa_ref.dtype)
        o += WIDTH_A
        ka_ref[...] = _rope(p[:, o:o + WIDTH_KV_A], cos, sin).astype(ka_ref.dtype)
        o += WIDTH_KV_A
        va_ref[...] = p[:, o:o + WIDTH_KV_A].astype(va_ref.dtype)
        o += WIDTH_KV_A
        qb_ref[...] = (p[:, o:o + WIDTH_B] * scale).astype(qb_ref.dtype)
        o += WIDTH_B
    else:
        ka_ref, va_ref, kb_ref, vb_ref = rest
        o = 0
        ka_ref[...] = p[:, o:o + WIDTH_KV_A].astype(ka_ref.dtype)
        o += WIDTH_KV_A
        va_ref[...] = p[:, o:o + WIDTH_KV_A].astype(va_ref.dtype)
        o += WIDTH_KV_A
    kb_ref[...] = p[:, o:o + WIDTH_B].astype(kb_ref.dtype)
    o += WIDTH_B
    vb_ref[...] = p[:, o:o + WIDTH_B].astype(vb_ref.dtype)


def _inproj(x2d, mod3, g_pre, w, rope_tabs, *, seq, mod_row0, latent):
    t, d = x2d.shape
    tile = min(PROJ_TILE, seq)
    per_seq = seq // tile
    if latent:
        mod_row = lambda i: i // per_seq
    else:
        mod_row = lambda i: mod_row0
    in_specs = [pl.BlockSpec((tile, d), lambda i: (i, 0)),
                pl.BlockSpec((1, 1, d), lambda i: (mod_row(i), 0, 0)),
                pl.BlockSpec((1, 1, d), lambda i: (mod_row(i), 0, 1)),
                pl.BlockSpec((1, d), lambda i: (0, 0)),
                pl.BlockSpec(w.shape, lambda i: (0, 0))]
    args = [x2d, mod3, mod3, g_pre, w]
    widths = [WIDTH_KV_A, WIDTH_KV_A, WIDTH_B, WIDTH_B]
    if latent:
        in_specs += [pl.BlockSpec((tile, LANES), lambda i: (i % per_seq, 0))] * 2
        args += list(rope_tabs)
        widths = [WIDTH_A, WIDTH_KV_A, WIDTH_KV_A, WIDTH_B, WIDTH_B, WIDTH_B]
    return pl.pallas_call(
        functools.partial(_inproj_kernel, latent=latent),
        grid=(t // tile,),
        in_specs=in_specs,
        out_specs=[pl.BlockSpec((tile, wd), lambda i: (i, 0)) for wd in widths],
        out_shape=[jax.ShapeDtypeStruct((t, wd), jnp.bfloat16) for wd in widths],
        compiler_params=_params("parallel"),
        name="inproj_latent" if latent else "inproj_ctx",
    )(*args)


def _pair_attention(q_ref, heads, o_ref):
    tq = q_ref.shape[0]
    lane = lax.broadcasted_iota(jnp.int32, (tq, LANES), 1)
    low = lane < HEAD_DIM
    for j in range(q_ref.shape[1] // LANES):
        q = q_ref[:, j * LANES:(j + 1) * LANES]
        pair = None
        for half in range(2):
            k_loc, v_loc, k_ctx, v_ctx, bias, sink = heads(j, half)
            qm = jnp.where(low if half == 0 else ~low, q, jnp.zeros_like(q))
            s_loc = _dot_nt(qm, k_loc) + bias
            s_ctx = _dot_nt(qm, k_ctx)
            m = jnp.maximum(jnp.max(s_loc, axis=1, keepdims=True), jnp.max(s_ctx, axis=1, keepdims=True))
            if sink is not None:
                m = jnp.maximum(m, sink)
            p_loc = jnp.exp(s_loc - m)
            p_ctx = jnp.exp(s_ctx - m)
            den = jnp.sum(p_loc, axis=1, keepdims=True) + jnp.sum(p_ctx, axis=1, keepdims=True)
            if sink is not None:
                den = den + jnp.exp(sink - m)
            o = _dot(p_loc.astype(v_loc.dtype), v_loc) + _dot(p_ctx.astype(v_ctx.dtype), v_ctx)
            o = o / den
            pair = o if half == 0 else jnp.where(low, pair, o)
        o_ref[:, j * LANES:(j + 1) * LANES] = pair.astype(o_ref.dtype)


def _win_kernel(sink_ref, q_ref, k_ref, v_ref, kc_ref, vc_ref, o_ref, *, seq):
    i = pl.program_id(1)
    span = BLOCK_A + 2 * WINDOW
    start = pl.multiple_of(jnp.clip(i * BLOCK_A - WINDOW, 0, seq - span), BLOCK_A)
    k_loc = k_ref[0, pl.ds(start, span), :]
    v_loc = v_ref[0, pl.ds(start, span), :]
    k_ctx = kc_ref[0]
    v_ctx = vc_ref[0]
    qpos = i * BLOCK_A + lax.broadcasted_iota(jnp.int32, (BLOCK_A, span), 0)
    kpos = start + lax.broadcasted_iota(jnp.int32, (BLOCK_A, span), 1)
    bias = jnp.where(jnp.abs(kpos - qpos) <= WINDOW, 0.0, NEG_INF).astype(jnp.float32)

    def heads(j, half):
        return k_loc, v_loc, k_ctx, v_ctx, bias, sink_ref[half * GQA_GROUP + j]

    _pair_attention(q_ref, heads, o_ref)


def _window_attention(qa, ka, va, kac, vac, sink, *, batch, seq):
    nb = seq // BLOCK_A
    n_ctx = kac.shape[1]
    return pl.pallas_call(
        functools.partial(_win_kernel, seq=seq),
        grid=(batch, nb),
        in_specs=[pl.BlockSpec(memory_space=pltpu.SMEM),
                  pl.BlockSpec((BLOCK_A, WIDTH_A), lambda b, i: (b * nb + i, 0)),
                  pl.BlockSpec((1, seq, WIDTH_KV_A), lambda b, i: (b, 0, 0)),
                  pl.BlockSpec((1, seq, WIDTH_KV_A), lambda b, i: (b, 0, 0)),
                  pl.BlockSpec((1, n_ctx, WIDTH_KV_A), lambda b, i: (b, 0, 0)),
                  pl.BlockSpec((1, n_ctx, WIDTH_KV_A), lambda b, i: (b, 0, 0))],
        out_specs=pl.BlockSpec((BLOCK_A, WIDTH_A), lambda b, i: (b * nb + i, 0)),
        out_shape=jax.ShapeDtypeStruct((batch * seq, WIDTH_A), jnp.bfloat16),
        compiler_params=_params("parallel", "arbitrary"),
        name="window_attention",
    )(sink, qa, ka, va, kac, vac)


def _nb_kernel(q_ref, k_ref, v_ref, kc_ref, vc_ref, tab_ref, o_ref, *, rows_n):
    m = pl.program_id(1)
    n_keys = NB_K_ROWS * GRID_W
    start_row = jnp.clip(NB_Q_ROWS * m - NA_ROWS // 2, 0, rows_n - NB_K_ROWS)
    start = pl.multiple_of(start_row * GRID_W, LANES)

    def heads(j, half):
        cols = slice(j * LANES, (j + 1) * LANES)
        return (k_ref[0, pl.ds(start, n_keys), cols], v_ref[0, pl.ds(start, n_keys), cols],
                kc_ref[0, :, cols], vc_ref[0, :, cols],
                tab_ref[0, 2 * j + half].astype(jnp.float32), None)

    _pair_attention(q_ref, heads, o_ref)


def _nb_classes(rows_n):
    n_blocks = rows_n // NB_Q_ROWS
    sig = {}
    cls_of_block = []
    reps = []
    for m in range(n_blocks):
        start_row = int(np.clip(NB_Q_ROWS * m - NA_ROWS // 2, 0, rows_n - NB_K_ROWS))
        key = tuple((start_row - r, int(np.clip(r - NA_ROWS // 2, 0, rows_n - NA_ROWS)) - r)
                    for r in range(NB_Q_ROWS * m, NB_Q_ROWS * (m + 1)))
        if key not in sig:
            sig[key] = len(reps)
            reps.append(m)
        cls_of_block.append(sig[key])
    return np.asarray(cls_of_block, np.int32), reps


def _nb_bias_tables(rpb, rows_n):
    cls_of_block, reps = _nb_classes(rows_n)
    tabs = []
    for m in reps:
        start_row = int(np.clip(NB_Q_ROWS * m - NA_ROWS // 2, 0, rows_n - NB_K_ROWS))
        qi = np.arange(NB_Q_ROWS * GRID_W)
        ki = np.arange(NB_K_ROWS * GRID_W)
        r = NB_Q_ROWS * m + qi // GRID_W
        cq = qi % GRID_W
        krow = start_row + ki // GRID_W
        ck = ki % GRID_W
        rs = np.clip(r - NA_ROWS // 2, 0, rows_n - NA_ROWS)
        cs = np.clip(cq - NA_COLS // 2, 0, GRID_W - NA_COLS)
        ok = ((krow[None, :] >= rs[:, None]) & (krow[None, :] < rs[:, None] + NA_ROWS)
              & (ck[None, :] >= cs[:, None]) & (ck[None, :] < cs[:, None] + NA_COLS))
        drow = np.clip(krow[None, :] - r[:, None] + NA_ROWS - 1, 0, 2 * NA_ROWS - 2)
        dcol = np.clip(ck[None, :] - cq[:, None] + NA_COLS - 1, 0, 2 * NA_COLS - 2)
        tabs.append(jnp.where(jnp.asarray(ok)[None], rpb[:, drow, dcol], NEG_INF))
    return jnp.stack(tabs).astype(jnp.bfloat16), cls_of_block


def _neighbourhood_attention(qb, kb, vb, kbc, vbc, tabs, cls_of_block, *, batch, seq):
    rows_n = seq // GRID_W
    nblk = rows_n // NB_Q_ROWS
    tq = NB_Q_ROWS * GRID_W
    n_ctx = kbc.shape[1]
    grid_spec = pltpu.PrefetchScalarGridSpec(
        num_scalar_prefetch=1,
        grid=(batch, nblk),
        in_specs=[pl.BlockSpec((tq, WIDTH_B), lambda b, m, c: (b * nblk + m, 0)),
                  pl.BlockSpec((1, seq, WIDTH_B), lambda b, m, c: (b, 0, 0)),
                  pl.BlockSpec((1, seq, WIDTH_B), lambda b, m, c: (b, 0, 0)),
                  pl.BlockSpec((1, n_ctx, WIDTH_B), lambda b, m, c: (b, 0, 0)),
                  pl.BlockSpec((1, n_ctx, WIDTH_B), lambda b, m, c: (b, 0, 0)),
                  pl.BlockSpec((1,) + tabs.shape[1:], lambda b, m, c: (c[m], 0, 0, 0))],
        out_specs=pl.BlockSpec((tq, WIDTH_B), lambda b, m, c: (b * nblk + m, 0)),
    )

    def body(c_ref, *refs):
        _nb_kernel(*refs, rows_n=rows_n)

    return pl.pallas_call(
        body,
        grid_spec=grid_spec,
        out_shape=jax.ShapeDtypeStruct((batch * seq, WIDTH_B), jnp.bfloat16),
        compiler_params=_params("parallel", "arbitrary"),
        name="neighbourhood_attention",
    )(jnp.asarray(cls_of_block), qb, kb, vb, kbc, vbc, tabs)


def _outproj_kernel(oa_ref, ob_ref, x_ref, ga_ref, sh_ref, sc_ref, gga_ref, ggb_ref, gpost_ref, gpre_ref,
                    wo_ref, wr_ref, br_ref,
                    x1_ref, h2_ref, idx_ref, wts_ref, rank_ref, cnt_ref, run_ref):
    i = pl.program_id(0)

    @pl.when(i == 0)
    def _():
        run_ref[...] = jnp.zeros_like(run_ref)

    na = _rms(oa_ref[...].astype(jnp.float32), gga_ref[...]).astype(jnp.bfloat16)
    nb = _rms(ob_ref[...].astype(jnp.float32), ggb_ref[...]).astype(jnp.bfloat16)
    mix = _dot(na, wo_ref[:WIDTH_A, :]) + _dot(nb, wo_ref[WIDTH_A:, :])
    x1 = x_ref[...] + ga_ref[0] * _rms(mix, gpost_ref[...])
    x1_ref[...] = x1
    h2 = _rms(x1, gpre_ref[...]) * (1.0 + sc_ref[0]) + sh_ref[0]
    h2_ref[...] = h2

    logits = jnp.dot(h2, wr_ref[...], preferred_element_type=jnp.float32,
                     precision=lax.Precision.HIGHEST) + br_ref[...]
    t, e = logits.shape
    lane = lax.broadcasted_iota(jnp.int32, (t, e), 1)
    work = logits
    chosen = jnp.zeros((t, e), jnp.float32)
    vals, sels, hots = [], [], []
    for _k in range(TOP_K):
        mx = jnp.max(work, axis=1, keepdims=True)
        sel = jnp.min(jnp.where(work == mx, lane, e), axis=1, keepdims=True)
        hot = lane == sel
        vals.append(mx)
        sels.append(sel)
        hots.append(hot)
        work = jnp.where(hot, -jnp.inf, work)
        chosen = chosen + hot.astype(jnp.float32)
    ex = [jnp.exp(v - vals[0]) for v in vals]
    den = ex[0] + ex[1] + ex[2] + ex[3]

    r_i = lax.broadcasted_iota(jnp.int32, (t, t), 0)
    c_i = lax.broadcasted_iota(jnp.int32, (t, t), 1)
    before = (c_i < r_i).astype(jnp.bfloat16)
    prefix = _dot(before, chosen.astype(jnp.bfloat16)) + run_ref[...]
    ranks = [jnp.sum(jnp.where(h, prefix, 0.0), axis=1, keepdims=True) for h in hots]
    run_ref[...] = run_ref[...] + jnp.sum(chosen, axis=0, keepdims=True)
    cnt_ref[...] = run_ref[...].astype(jnp.int32)

    l4 = lax.broadcasted_iota(jnp.int32, (t, TOP_K), 1)

    def cols(parts):
        out = parts[TOP_K - 1]
        for k in range(TOP_K - 2, -1, -1):
            out = jnp.where(l4 == k, parts[k], out)
        return out

    idx_ref[...] = cols(sels)
    wts_ref[...] = cols([v / den for v in ex])
    rank_ref[...] = cols(ranks).astype(jnp.int32)


def _outproj(oa, ob, x2d, mod3, gga, ggb, gpost, gpre, w_out, w_router, b_router, *, seq):
    t, d = x2d.shape
    tile = OUT_TILE
    per_seq = seq // tile
    row = lambda i: (i, 0)
    const = lambda i: (0, 0)
    modspec = lambda part: pl.BlockSpec((1, 1, d), lambda i: (i // per_seq, 0, part))
    outs = pl.pallas_call(
        _outproj_kernel,
        grid=(t // tile,),
        in_specs=[pl.BlockSpec((tile, WIDTH_A), row), pl.BlockSpec((tile, WIDTH_B), row),
                  pl.BlockSpec((tile, d), row),
                  modspec(2), modspec(3), modspec(4),
                  pl.BlockSpec((1, WIDTH_A), const), pl.BlockSpec((1, WIDTH_B), const),
                  pl.BlockSpec((1, d), const), pl.BlockSpec((1, d), const),
                  pl.BlockSpec(w_out.shape, const), pl.BlockSpec(w_router.shape, const),
                  pl.BlockSpec((1, N_EXPERTS), const)],
        out_specs=[pl.BlockSpec((tile, d), row), pl.BlockSpec((tile, d), row),
                   pl.BlockSpec((tile, TOP_K), row), pl.BlockSpec((tile, TOP_K), row),
                   pl.BlockSpec((tile, TOP_K), row), pl.BlockSpec((1, N_EXPERTS), const)],
        out_shape=[jax.ShapeDtypeStruct((t, d), jnp.float32), jax.ShapeDtypeStruct((t, d), jnp.float32),
                   jax.ShapeDtypeStruct((t, TOP_K), jnp.int32), jax.ShapeDtypeStruct((t, TOP_K), jnp.float32),
                   jax.ShapeDtypeStruct((t, TOP_K), jnp.int32), jax.ShapeDtypeStruct((1, N_EXPERTS), jnp.int32)],
        scratch_shapes=[pltpu.VMEM((1, N_EXPERTS), jnp.float32)],
        compiler_params=_params("arbitrary"),
        name="outproj_router",
    )(oa, ob, x2d, mod3, mod3, mod3, gga, ggb, gpost, gpre, w_out, w_router, b_router)
    return outs


def _dispatch_kernel(pad_end_ref, n_used_ref, dest_ref, h_ref, xs_ref, zero_ref, sem, zsem, *, n_tiles):
    i = pl.program_id(0)
    tile = h_ref.shape[0]

    def fill(row):
        return pltpu.make_async_copy(zero_ref, xs_ref.at[pl.ds(pl.multiple_of(row, FFN_TILE), FFN_TILE)], zsem)

    @pl.when(i == 0)
    def _():
        zero_ref[...] = jnp.zeros_like(zero_ref)
        n_tail = n_tiles - n_used_ref[0]

        def start_e(e, c):
            fill(jnp.maximum(pad_end_ref[e] - FFN_TILE, 0)).start()
            return c
        lax.fori_loop(0, N_EXPERTS, start_e, 0)

        def start_t(j, c):
            fill((n_used_ref[0] + j) * FFN_TILE).start()
            return c
        lax.fori_loop(0, n_tail, start_t, 0)

        def wait_all(j, c):
            fill(0).wait()
            return c
        lax.fori_loop(0, N_EXPERTS + n_tail, wait_all, 0)

    def row_copy(a):
        return pltpu.make_async_copy(h_ref.at[pl.ds(a // TOP_K, 1)], xs_ref.at[pl.ds(dest_ref[a], 1)], sem)

    def start(a, c):
        row_copy(a).start()
        return c
    lax.fori_loop(0, tile * TOP_K, start, 0, unroll=8)

    def wait(a, c):
        row_copy(0).wait()
        return c
    lax.fori_loop(0, tile * TOP_K, wait, 0, unroll=8)


def _dispatch(h2, dest_flat, pad_end, n_used, *, n_tiles):
    t, d = h2.shape
    tile = DISPATCH_TILE
    grid_spec = pltpu.PrefetchScalarGridSpec(
        num_scalar_prefetch=2,
        grid=(t // tile,),
        in_specs=[pl.BlockSpec((tile * TOP_K,), lambda i, *_: (i,), memory_space=pltpu.SMEM),
                  pl.BlockSpec((tile, d), lambda i, *_: (i, 0))],
        out_specs=pl.BlockSpec(memory_space=pl.ANY),
        scratch_shapes=[pltpu.VMEM((FFN_TILE, d), h2.dtype),
                        pltpu.SemaphoreType.DMA(()), pltpu.SemaphoreType.DMA(())],
    )
    return pl.pallas_call(
        functools.partial(_dispatch_kernel, n_tiles=n_tiles),
        grid_spec=grid_spec,
        out_shape=jax.ShapeDtypeStruct((n_tiles * FFN_TILE, d), h2.dtype),
        compiler_params=_params("arbitrary"),
        name="dispatch",
    )(pad_end, n_used, dest_flat, h2)


def _ffn_kernel(te_ref, n_used_ref, x_ref, w1g_ref, w1l_ref, b1g_ref, b1l_ref, w2_ref, b2_ref, y_ref):
    i = pl.program_id(0)

    @pl.when(i < n_used_ref[0])
    def _():
        x = x_ref[...].astype(jnp.bfloat16)
        f = w1g_ref.shape[2]
        acc = jnp.zeros(y_ref.shape, jnp.float32) + b2_ref[0]
        for c in range(0, f, FFN_CHUNK):
            cs = slice(c, c + FFN_CHUNK)
            g = _dot(x, w1g_ref[0, :, cs]) + b1g_ref[0, :, cs]
            u = _dot(x, w1l_ref[0, :, cs]) + b1l_ref[0, :, cs]
            g = jnp.minimum(g, SWIGLU_LIMIT)
            u = jnp.clip(u, -SWIGLU_LIMIT, SWIGLU_LIMIT)
            act = g * jax.nn.sigmoid(SWIGLU_ALPHA * g) * (u + 1.0)
            acc = acc + _dot(act.astype(jnp.bfloat16), w2_ref[0, cs, :])
        y_ref[...] = acc

    @pl.when(i >= n_used_ref[0])
    def _():
        y_ref[...] = jnp.zeros_like(y_ref)


def _ffn(xs, tile_expert, n_used, w1g, w1l, b1g, b1l, w2, b2):
    rows, d = xs.shape
    f = w1g.shape[2]
    n_tiles = rows // FFN_TILE
    live = lambda i, te, nu: jnp.minimum(i, nu[0] - 1)
    wspec = lambda shape: pl.BlockSpec((1,) + shape, lambda i, te, nu: (te[live(i, te, nu)], 0, 0))
    grid_spec = pltpu.PrefetchScalarGridSpec(
        num_scalar_prefetch=2,
        grid=(n_tiles,),
        in_specs=[pl.BlockSpec((FFN_TILE, d), lambda i, te, nu: (live(i, te, nu), 0)),
                  wspec((d, f)), wspec((d, f)), wspec((1, f)), wspec((1, f)), wspec((f, d)), wspec((1, d))],
        out_specs=pl.BlockSpec((FFN_TILE, d), lambda i, te, nu: (i, 0)),
    )
    return pl.pallas_call(
        _ffn_kernel,
        grid_spec=grid_spec,
        out_shape=jax.ShapeDtypeStruct((rows, d), jnp.float32),
        compiler_params=_params("arbitrary"),
        name="expert_ffn",
    )(tile_expert, n_used, xs, w1g, w1l, b1g, b1l, w2, b2)


def _combine_kernel(dest_ref, wts_ref, x1_ref, ga_ref, g_ref, y_ref, o_ref, buf_ref, sem):
    tile = x1_ref.shape[0]

    def row_copy(a):
        return pltpu.make_async_copy(y_ref.at[pl.ds(dest_ref[a], 1)],
                                     buf_ref.at[a % TOP_K, pl.ds(a // TOP_K, 1)], sem)

    def start(a, c):
        row_copy(a).start()
        return c
    lax.fori_loop(0, tile * TOP_K, start, 0, unroll=8)

    def wait(a, c):
        row_copy(0).wait()
        return c
    lax.fori_loop(0, tile * TOP_K, wait, 0, unroll=8)

    w = wts_ref[...]
    y = buf_ref[0] * w[:, 0:1]
    for k in range(1, TOP_K):
        y = y + buf_ref[k] * w[:, k:k + 1]
    o_ref[...] = x1_ref[...] + ga_ref[0] * _rms(y, g_ref[...])


def _combine(ys, dest_flat, wts, x1, mod3, g_post, *, seq):
    t, d = x1.shape
    tile = COMBINE_TILE
    per_seq = seq // tile
    return pl.pallas_call(
        _combine_kernel,
        grid=(t // tile,),
        in_specs=[pl.BlockSpec((tile * TOP_K,), lambda i: (i,), memory_space=pltpu.SMEM),
                  pl.BlockSpec((tile, TOP_K), lambda i: (i, 0)),
                  pl.BlockSpec((tile, d), lambda i: (i, 0)),
                  pl.BlockSpec((1, 1, d), lambda i: (i // per_seq, 0, 5)),
                  pl.BlockSpec((1, d), lambda i: (0, 0)),
                  pl.BlockSpec(memory_space=pl.ANY)],
        out_specs=pl.BlockSpec((tile, d), lambda i: (i, 0)),
        out_shape=jax.ShapeDtypeStruct((t, d), jnp.float32),
        scratch_shapes=[pltpu.VMEM((TOP_K, tile, d), ys.dtype), pltpu.SemaphoreType.DMA(())],
        compiler_params=_params("arbitrary"),
        name="combine",
    )(dest_flat, wts, x1, mod3, g_post, ys)


def _rope_tables(seq):
    pos = np.arange(seq)
    n_freq = HEAD_DIM // 4
    freqs = ROPE_BASE ** (-jnp.arange(n_freq, dtype=jnp.float32) / n_freq)
    rows = jnp.asarray(pos // GRID_W, jnp.float32)[:, None] * freqs[None, :]
    cols = jnp.asarray(pos % GRID_W, jnp.float32)[:, None] * freqs[None, :]
    ang = jnp.concatenate([rows, rows, cols, cols], axis=1)
    sign = np.tile(np.repeat([-1.0, 1.0], n_freq), 2).astype(np.float32)
    cos = jnp.cos(ang)
    sin = jnp.sin(ang) * sign[None, :]
    reps = LANES // HEAD_DIM
    return jnp.tile(cos, (1, reps)), jnp.tile(sin, (1, reps))


def kernel(x, c, ctx, c_ctx, w_ada, b_ada, g_pre_mix, g_post_mix, g_pre_ffn, g_post_ffn, w_in, g_grp_a, g_grp_b,
           sink_a, rpb_b, w_out, w_router, b_router, w_mlp1, b_mlp1, w_mlp2, b_mlp2):
    batch, seq, d = x.shape
    n_ctx = ctx.shape[1]
    assert w_ada.shape[0] == 1, "single layer"
    assert seq % (GRID_W * NB_Q_ROWS) == 0 and seq // GRID_W >= NB_K_ROWS and seq >= BLOCK_A + 2 * WINDOW
    n_tok = batch * seq
    bf16 = jnp.bfloat16

    head_order = np.asarray([k * GQA_GROUP + j for j in range(GQA_GROUP) for k in range(N_KV_A)])
    col_order = (head_order[:, None] * HEAD_DIM + np.arange(HEAD_DIM)[None, :]).reshape(-1)

    mod_rows = -(-(batch + 1) // 8) * 8
    cc = jnp.zeros((mod_rows, d), jnp.float32).at[:batch].set(c).at[batch].set(c_ctx)
    mod3 = _ada(cc, w_ada[0], b_ada[0]).reshape(mod_rows, 1, 6 * d)

    w_in0 = w_in[0]
    w_lat = jnp.concatenate([w_in0[:, :WIDTH_A][:, col_order], w_in0[:, WIDTH_A:]], axis=1).astype(bf16)
    kv_cols = np.concatenate([np.arange(WIDTH_A, WIDTH_A + 2 * WIDTH_KV_A),
                              np.arange(WIDTH_A + 2 * WIDTH_KV_A + WIDTH_B, w_in0.shape[1])])
    w_ctx = w_in0[:, kv_cols].astype(bf16)
    g_pre = g_pre_mix[0].reshape(1, d)
    qa, ka, va, qb, kb, vb = _inproj(x.reshape(n_tok, d), mod3, g_pre, w_lat, _rope_tables(seq),
                                     seq=seq, mod_row0=0, latent=True)
    kac, vac, kbc, vbc = _inproj(ctx.reshape(batch * n_ctx, d), mod3, g_pre, w_ctx, None,
                                 seq=n_ctx, mod_row0=batch, latent=False)

    oa = _window_attention(qa, ka.reshape(batch, seq, -1), va.reshape(batch, seq, -1),
                           kac.reshape(batch, n_ctx, -1), vac.reshape(batch, n_ctx, -1),
                           sink_a[0].astype(jnp.float32), batch=batch, seq=seq)
    tabs, cls_of_block = _nb_bias_tables(rpb_b[0], seq // GRID_W)
    ob = _neighbourhood_attention(qb, kb.reshape(batch, seq, -1), vb.reshape(batch, seq, -1),
                                  kbc.reshape(batch, n_ctx, -1), vbc.reshape(batch, n_ctx, -1),
                                  tabs, cls_of_block, batch=batch, seq=seq)

    w_out0 = w_out[0]
    w_o = jnp.concatenate([w_out0[:WIDTH_A][col_order], w_out0[WIDTH_A:]], axis=0).astype(bf16)
    x1, h2, idx, wts, rank, counts = _outproj(
        oa, ob, x.reshape(n_tok, d), mod3,
        g_grp_a[0][col_order].reshape(1, -1), g_grp_b[0].reshape(1, -1),
        g_post_mix[0].reshape(1, d), g_pre_ffn[0].reshape(1, d),
        w_o, w_router[0], b_router[0].reshape(1, -1), seq=seq)

    counts = counts.reshape(-1)
    padded = (counts + FFN_TILE - 1) // FFN_TILE * FFN_TILE
    pad_end = jnp.cumsum(padded).astype(jnp.int32)
    pad_start = pad_end - padded
    n_tiles = n_tok * TOP_K // FFN_TILE + N_EXPERTS
    n_used = (pad_end[-1:] // FFN_TILE).astype(jnp.int32)
    hot = idx[..., None] == jnp.arange(N_EXPERTS, dtype=jnp.int32)
    dest = (rank + jnp.sum(jnp.where(hot, pad_start, 0), axis=-1)).reshape(-1).astype(jnp.int32)
    tile_row0 = jnp.arange(n_tiles, dtype=jnp.int32) * FFN_TILE
    tile_expert = jnp.minimum(jnp.sum(tile_row0[:, None] >= pad_end[None, :], axis=1),
                              N_EXPERTS - 1).astype(jnp.int32)

    xs = _dispatch(h2, dest, pad_end, n_used, n_tiles=n_tiles)
    w1 = w_mlp1[0]
    b1 = b_mlp1[0]
    ys = _ffn(xs, tile_expert, n_used,
              w1[:, :, 0::2].astype(bf16), w1[:, :, 1::2].astype(bf16),
              b1[:, None, 0::2], b1[:, None, 1::2],
              w_mlp2[0].astype(bf16), b_mlp2[0][:, None, :])
    out = _combine(ys, dest, wts, x1, mod3, g_post_ffn[0].reshape(1, d), seq=seq)
    return out.reshape(batch, seq, d)
```

```python
import functools

import numpy as np
import jax
import jax.numpy as jnp
from jax import lax
from jax.experimental import pallas as pl
from jax.experimental.pallas import tpu as pltpu

GRID_W = 64
HEAD_DIM = 64
N_HEADS_A = 8
N_KV_A = 2
GQA_GROUP = N_HEADS_A // N_KV_A
N_HEADS_B = 8
WIDTH_A = N_HEADS_A * HEAD_DIM
WIDTH_KV_A = N_KV_A * HEAD_DIM
WIDTH_B = N_HEADS_B * HEAD_DIM
WINDOW = 128
BLOCK_A = 128
NA_ROWS = 8
NA_COLS = 16
N_EXPERTS = 32
TOP_K = 4
SWIGLU_LIMIT = 7.0
SWIGLU_ALPHA = 1.702
ROPE_BASE = 10000.0
EPS = 1e-6
NEG_INF = -1e30

LANES = 128
VMEM_LIMIT = 56 * 1024 * 1024

NB_Q_ROWS = 2
NB_K_ROWS = NB_Q_ROWS + NA_ROWS
PROJ_TILE = 512
OUT_TILE = 256
FFN_TILE = 256
FFN_CHUNK = 256
DISPATCH_TILE = 256
COMBINE_TILE = 128


def _params(*sem):
    return pltpu.CompilerParams(dimension_semantics=sem, vmem_limit_bytes=VMEM_LIMIT)


def _rms(x, g):
    return x * lax.rsqrt(jnp.mean(x * x, axis=-1, keepdims=True) + EPS) * g


def _dot(a, b):
    return jnp.dot(a, b, preferred_element_type=jnp.float32)


def _dot_nt(a, b):
    return lax.dot_general(a, b, (((1,), (1,)), ((), ())), preferred_element_type=jnp.float32)


def _ada_kernel(c_ref, w_ref, b_ref, o_ref):
    c = c_ref[...]
    s = (c * jax.nn.sigmoid(c)).astype(jnp.bfloat16)
    o_ref[...] = _dot(s, w_ref[...].astype(jnp.bfloat16)) + b_ref[...]


def _ada(cc, w_ada, b_ada):
    rows, d = cc.shape
    n_out = w_ada.shape[1]
    return pl.pallas_call(
        _ada_kernel,
        grid=(n_out // d,),
        in_specs=[pl.BlockSpec((rows, d), lambda j: (0, 0)),
                  pl.BlockSpec((d, d), lambda j: (0, j)),
                  pl.BlockSpec((1, d), lambda j: (0, j))],
        out_specs=pl.BlockSpec((rows, d), lambda j: (0, j)),
        out_shape=jax.ShapeDtypeStruct((rows, n_out), jnp.float32),
        compiler_params=_params("arbitrary"),
        name="ada",
    )(cc, w_ada, b_ada.reshape(1, n_out))


def _rope(x, cos, sin):
    w = x.shape[1]
    reps = w // LANES
    if reps > 1:
        cos = jnp.concatenate([cos] * reps, axis=1)
        sin = jnp.concatenate([sin] * reps, axis=1)
    lane = lax.broadcasted_iota(jnp.int32, x.shape, 1)
    quarter = HEAD_DIM // 4
    partner = jnp.where(lane % (2 * quarter) < quarter,
                        pltpu.roll(x, w - quarter, 1), pltpu.roll(x, quarter, 1))
    return x * cos + partner * sin


def _inproj_kernel(x_ref, sh_ref, sc_ref, g_ref, w_ref, *rest, latent):
    x = x_ref[...]
    h = _rms(x, g_ref[...]) * (1.0 + sc_ref[0]) + sh_ref[0]
    p = _dot(h.astype(jnp.bfloat16), w_ref[...])
    if latent:
        cos_ref, sin_ref, qa_ref, ka_ref, va_ref, qb_ref, kb_ref, vb_ref = rest
        cos, sin = cos_ref[...], sin_ref[...]
        scale = HEAD_DIM ** -0.5
        o = 0
        qa_ref[...] = (_rope(p[:, o:o + WIDTH_A], cos, sin) * scale).astype(qa_ref.dtype)
        o += WIDTH_A
        ka_ref[...] = _rope(p[:, o:o + WIDTH_KV_A], cos, sin).astype(ka_ref.dtype)
        o += WIDTH_KV_A
        va_ref[...] = p[:, o:o + WIDTH_KV_A].astype(va_ref.dtype)
        o += WIDTH_KV_A
        qb_ref[...] = (p[:, o:o + WIDTH_B] * scale).astype(qb_ref.dtype)
        o += WIDTH_B
    else:
        ka_ref, va_ref, kb_ref, vb_ref = rest
        o = 0
        ka_ref[...] = p[:, o:o + WIDTH_KV_A].astype(ka_ref.dtype)
        o += WIDTH_KV_A
        va_ref[...] = p[:, o:o + WIDTH_KV_A].astype(va_ref.dtype)
        o += WIDTH_KV_A
    kb_ref[...] = p[:, o:o + WIDTH_B].astype(kb_ref.dtype)
    o += WIDTH_B
    vb_ref[...] = p[:, o:o + WIDTH_B].astype(vb_ref.dtype)


def _inproj(x2d, mod3, g_pre, w, rope_tabs, *, seq, mod_row0, latent):
    t, d = x2d.shape
    tile = min(PROJ_TILE, seq)
    per_seq = seq // tile
    if latent:
        mod_row = lambda i: i // per_seq
    else:
        mod_row = lambda i: mod_row0
    in_specs = [pl.BlockSpec((tile, d), lambda i: (i, 0)),
                pl.BlockSpec((1, 1, d), lambda i: (mod_row(i), 0, 0)),
                pl.BlockSpec((1, 1, d), lambda i: (mod_row(i), 0, 1)),
                pl.BlockSpec((1, d), lambda i: (0, 0)),
                pl.BlockSpec(w.shape, lambda i: (0, 0))]
    args = [x2d, mod3, mod3, g_pre, w]
    widths = [WIDTH_KV_A, WIDTH_KV_A, WIDTH_B, WIDTH_B]
    if latent:
        in_specs += [pl.BlockSpec((tile, LANES), lambda i: (i % per_seq, 0))] * 2
        args += list(rope_tabs)
        widths = [WIDTH_A, WIDTH_KV_A, WIDTH_KV_A, WIDTH_B, WIDTH_B, WIDTH_B]
    return pl.pallas_call(
        functools.partial(_inproj_kernel, latent=latent),
        grid=(t // tile,),
        in_specs=in_specs,
        out_specs=[pl.BlockSpec((tile, wd), lambda i: (i, 0)) for wd in widths],
        out_shape=[jax.ShapeDtypeStruct((t, wd), jnp.bfloat16) for wd in widths],
        compiler_params=_params("parallel"),
        name="inproj_latent" if latent else "inproj_ctx",
    )(*args)


def _pair_attention(q_ref, heads, o_ref):
    tq = q_ref.shape[0]
    lane = lax.broadcasted_iota(jnp.int32, (tq, LANES), 1)
    low = lane < HEAD_DIM
    for j in range(q_ref.shape[1] // LANES):
        q = q_ref[:, j * LANES:(j + 1) * LANES]
        pair = None
        for half in range(2):
            k_loc, v_loc, k_ctx, v_ctx, bias, sink = heads(j, half)
            qm = jnp.where(low if half == 0 else ~low, q, jnp.zeros_like(q))
            s_loc = _dot_nt(qm, k_loc) + bias
            s_ctx = _dot_nt(qm, k_ctx)
            m = jnp.maximum(jnp.max(s_loc, axis=1, keepdims=True), jnp.max(s_ctx, axis=1, keepdims=True))
            if sink is not None:
                m = jnp.maximum(m, sink)
            p_loc = jnp.exp(s_loc - m)
            p_ctx = jnp.exp(s_ctx - m)
            den = jnp.sum(p_loc, axis=1, keepdims=True) + jnp.sum(p_ctx, axis=1, keepdims=True)
            if sink is not None:
                den = den + jnp.exp(sink - m)
            o = _dot(p_loc.astype(v_loc.dtype), v_loc) + _dot(p_ctx.astype(v_ctx.dtype), v_ctx)
            o = o / den
            pair = o if half == 0 else jnp.where(low, pair, o)
        o_ref[:, j * LANES:(j + 1) * LANES] = pair.astype(o_ref.dtype)


def _win_kernel(sink_ref, q_ref, k_ref, v_ref, kc_ref, vc_ref, o_ref, *, seq):
    i = pl.program_id(1)
    span = BLOCK_A + 2 * WINDOW
    start = pl.multiple_of(jnp.clip(i * BLOCK_A - WINDOW, 0, seq - span), BLOCK_A)
    k_loc = k_ref[0, pl.ds(start, span), :]
    v_loc = v_ref[0, pl.ds(start, span), :]
    k_ctx = kc_ref[0]
    v_ctx = vc_ref[0]
    qpos = i * BLOCK_A + lax.broadcasted_iota(jnp.int32, (BLOCK_A, span), 0)
    kpos = start + lax.broadcasted_iota(jnp.int32, (BLOCK_A, span), 1)
    bias = jnp.where(jnp.abs(kpos - qpos) <= WINDOW, 0.0, NEG_INF).astype(jnp.float32)

    def heads(j, half):
        return k_loc, v_loc, k_ctx, v_ctx, bias, sink_ref[half * GQA_GROUP + j]

    _pair_attention(q_ref, heads, o_ref)


def _window_attention(qa, ka, va, kac, vac, sink, *, batch, seq):
    nb = seq // BLOCK_A
    n_ctx = kac.shape[1]
    return pl.pallas_call(
        functools.partial(_win_kernel, seq=seq),
        grid=(batch, nb),
        in_specs=[pl.BlockSpec(memory_space=pltpu.SMEM),
                  pl.BlockSpec((BLOCK_A, WIDTH_A), lambda b, i: (b * nb + i, 0)),
                  pl.BlockSpec((1, seq, WIDTH_KV_A), lambda b, i: (b, 0, 0)),
                  pl.BlockSpec((1, seq, WIDTH_KV_A), lambda b, i: (b, 0, 0)),
                  pl.BlockSpec((1, n_ctx, WIDTH_KV_A), lambda b, i: (b, 0, 0)),
                  pl.BlockSpec((1, n_ctx, WIDTH_KV_A), lambda b, i: (b, 0, 0))],
        out_specs=pl.BlockSpec((BLOCK_A, WIDTH_A), lambda b, i: (b * nb + i, 0)),
        out_shape=jax.ShapeDtypeStruct((batch * seq, WIDTH_A), jnp.bfloat16),
        compiler_params=_params("parallel", "arbitrary"),
        name="window_attention",
    )(sink, qa, ka, va, kac, vac)


def _nb_kernel(q_ref, k_ref, v_ref, kc_ref, vc_ref, tab_ref, o_ref, *, rows_n):
    m = pl.program_id(1)
    n_keys = NB_K_ROWS * GRID_W
    start_row = jnp.clip(NB_Q_ROWS * m - NA_ROWS // 2, 0, rows_n - NB_K_ROWS)
    start = pl.multiple_of(start_row * GRID_W, LANES)

    def heads(j, half):
        cols = slice(j * LANES, (j + 1) * LANES)
        return (k_ref[0, pl.ds(start, n_keys), cols], v_ref[0, pl.ds(start, n_keys), cols],
                kc_ref[0, :, cols], vc_ref[0, :, cols],
                tab_ref[0, 2 * j + half].astype(jnp.float32), None)

    _pair_attention(q_ref, heads, o_ref)


def _nb_classes(rows_n):
    n_blocks = rows_n // NB_Q_ROWS
    sig = {}
    cls_of_block = []
    reps = []
    for m in range(n_blocks):
        start_row = int(np.clip(NB_Q_ROWS * m - NA_ROWS // 2, 0, rows_n - NB_K_ROWS))
        key = tuple((start_row - r, int(np.clip(r - NA_ROWS // 2, 0, rows_n - NA_ROWS)) - r)
                    for r in range(NB_Q_ROWS * m, NB_Q_ROWS * (m + 1)))
        if key not in sig:
            sig[key] = len(reps)
            reps.append(m)
        cls_of_block.append(sig[key])
    return np.asarray(cls_of_block, np.int32), reps


def _nb_bias_tables(rpb, rows_n):
    cls_of_block, reps = _nb_classes(rows_n)
    n_heads = rpb.shape[0]
    cq = np.arange(GRID_W)[:, None]
    ck = np.arange(GRID_W)[None, :]
    cs = np.clip(cq - NA_COLS // 2, 0, GRID_W - NA_COLS)
    col_ok = (ck >= cs) & (ck < cs + NA_COLS)
    pick = ((ck - cq + NA_COLS - 1)[None] == np.arange(2 * NA_COLS - 1)[:, None, None]) & col_ok[None]
    tiles = jnp.einsum('hrd,dqk->hrqk', rpb.astype(jnp.float32), jnp.asarray(pick, jnp.float32),
                       precision=lax.Precision.HIGHEST)
    tiles = jnp.where(jnp.asarray(col_ok)[None, None], tiles, NEG_INF).astype(jnp.bfloat16)
    blocked = jnp.full((n_heads, GRID_W, GRID_W), NEG_INF, jnp.bfloat16)
    tabs = []
    for m in reps:
        start_row = int(np.clip(NB_Q_ROWS * m - NA_ROWS // 2, 0, rows_n - NB_K_ROWS))
        q_rows = []
        for r in range(NB_Q_ROWS * m, NB_Q_ROWS * (m + 1)):
            rs = int(np.clip(r - NA_ROWS // 2, 0, rows_n - NA_ROWS))
            q_rows.append(jnp.concatenate(
                [tiles[:, krow - r + NA_ROWS - 1] if rs <= krow < rs + NA_ROWS else blocked
                 for krow in range(start_row, start_row + NB_K_ROWS)], axis=2))
        tabs.append(jnp.concatenate(q_rows, axis=1))
    return jnp.stack(tabs), cls_of_block


def _neighbourhood_attention(qb, kb, vb, kbc, vbc, tabs, cls_of_block, *, batch, seq):
    rows_n = seq // GRID_W
    nblk = rows_n // NB_Q_ROWS
    tq = NB_Q_ROWS * GRID_W
    n_ctx = kbc.shape[1]
    grid_spec = pltpu.PrefetchScalarGridSpec(
        num_scalar_prefetch=1,
        grid=(batch, nblk),
        in_specs=[pl.BlockSpec((tq, WIDTH_B), lambda b, m, c: (b * nblk + m, 0)),
                  pl.BlockSpec((1, seq, WIDTH_B), lambda b, m, c: (b, 0, 0)),
                  pl.BlockSpec((1, seq, WIDTH_B), lambda b, m, c: (b, 0, 0)),
                  pl.BlockSpec((1, n_ctx, WIDTH_B), lambda b, m, c: (b, 0, 0)),
                  pl.BlockSpec((1, n_ctx, WIDTH_B), lambda b, m, c: (b, 0, 0)),
                  pl.BlockSpec((1,) + tabs.shape[1:], lambda b, m, c: (c[m], 0, 0, 0))],
        out_specs=pl.BlockSpec((tq, WIDTH_B), lambda b, m, c: (b * nblk + m, 0)),
    )

    def body(c_ref, *refs):
        _nb_kernel(*refs, rows_n=rows_n)

    return pl.pallas_call(
        body,
        grid_spec=grid_spec,
        out_shape=jax.ShapeDtypeStruct((batch * seq, WIDTH_B), jnp.bfloat16),
        compiler_params=_params("parallel", "arbitrary"),
        name="neighbourhood_attention",
    )(jnp.asarray(cls_of_block), qb, kb, vb, kbc, vbc, tabs)


def _outproj_kernel(oa_ref, ob_ref, x_ref, ga_ref, sh_ref, sc_ref, gga_ref, ggb_ref, gpost_ref, gpre_ref,
                    wo_ref, wr_ref, br_ref,
                    x1_ref, h2_ref, idx_ref, wts_ref, rank_ref, cnt_ref, run_ref):
    i = pl.program_id(0)

    @pl.when(i == 0)
    def _():
        run_ref[...] = jnp.zeros_like(run_ref)

    na = _rms(oa_ref[...].astype(jnp.float32), gga_ref[...]).astype(jnp.bfloat16)
    nb = _rms(ob_ref[...].astype(jnp.float32), ggb_ref[...]).astype(jnp.bfloat16)
    mix = _dot(na, wo_ref[:WIDTH_A, :]) + _dot(nb, wo_ref[WIDTH_A:, :])
    x1 = x_ref[...] + ga_ref[0] * _rms(mix, gpost_ref[...])
    x1_ref[...] = x1
    h2 = _rms(x1, gpre_ref[...]) * (1.0 + sc_ref[0]) + sh_ref[0]
    h2_ref[...] = h2

    logits = jnp.dot(h2, wr_ref[...], preferred_element_type=jnp.float32,
                     precision=lax.Precision.HIGHEST) + br_ref[...]
    t, e = logits.shape
    lane = lax.broadcasted_iota(jnp.int32, (t, e), 1)
    work = logits
    chosen = jnp.zeros((t, e), jnp.float32)
    vals, sels, hots = [], [], []
    for _k in range(TOP_K):
        mx = jnp.max(work, axis=1, keepdims=True)
        sel = jnp.min(jnp.where(work == mx, lane, e), axis=1, keepdims=True)
        hot = lane == sel
        vals.append(mx)
        sels.append(sel)
        hots.append(hot)
        work = jnp.where(hot, -jnp.inf, work)
        chosen = chosen + hot.astype(jnp.float32)
    ex = [jnp.exp(v - vals[0]) for v in vals]
    den = ex[0] + ex[1] + ex[2] + ex[3]

    r_i = lax.broadcasted_iota(jnp.int32, (t, t), 0)
    c_i = lax.broadcasted_iota(jnp.int32, (t, t), 1)
    before = (c_i < r_i).astype(jnp.bfloat16)
    prefix = _dot(before, chosen.astype(jnp.bfloat16)) + run_ref[...]
    ranks = [jnp.sum(jnp.where(h, prefix, 0.0), axis=1, keepdims=True) for h in hots]
    run_ref[...] = run_ref[...] + jnp.sum(chosen, axis=0, keepdims=True)
    cnt_ref[...] = run_ref[...].astype(jnp.int32)

    l4 = lax.broadcasted_iota(jnp.int32, (t, TOP_K), 1)

    def cols(parts):
        out = parts[TOP_K - 1]
        for k in range(TOP_K - 2, -1, -1):
            out = jnp.where(l4 == k, parts[k], out)
        return out

    idx_ref[...] = cols(sels)
    wts_ref[...] = cols([v / den for v in ex])
    rank_ref[...] = cols(ranks).astype(jnp.int32)


def _outproj(oa, ob, x2d, mod3, gga, ggb, gpost, gpre, w_out, w_router, b_router, *, seq):
    t, d = x2d.shape
    tile = OUT_TILE
    per_seq = seq // tile
    row = lambda i: (i, 0)
    const = lambda i: (0, 0)
    modspec = lambda part: pl.BlockSpec((1, 1, d), lambda i: (i // per_seq, 0, part))
    outs = pl.pallas_call(
        _outproj_kernel,
        grid=(t // tile,),
        in_specs=[pl.BlockSpec((tile, WIDTH_A), row), pl.BlockSpec((tile, WIDTH_B), row),
                  pl.BlockSpec((tile, d), row),
                  modspec(2), modspec(3), modspec(4),
                  pl.BlockSpec((1, WIDTH_A), const), pl.BlockSpec((1, WIDTH_B), const),
                  pl.BlockSpec((1, d), const), pl.BlockSpec((1, d), const),
                  pl.BlockSpec(w_out.shape, const), pl.BlockSpec(w_router.shape, const),
                  pl.BlockSpec((1, N_EXPERTS), const)],
        out_specs=[pl.BlockSpec((tile, d), row), pl.BlockSpec((tile, d), row),
                   pl.BlockSpec((tile, TOP_K), row), pl.BlockSpec((tile, TOP_K), row),
                   pl.BlockSpec((tile, TOP_K), row), pl.BlockSpec((1, N_EXPERTS), const)],
        out_shape=[jax.ShapeDtypeStruct((t, d), jnp.float32), jax.ShapeDtypeStruct((t, d), jnp.float32),
                   jax.ShapeDtypeStruct((t, TOP_K), jnp.int32), jax.ShapeDtypeStruct((t, TOP_K), jnp.float32),
                   jax.ShapeDtypeStruct((t, TOP_K), jnp.int32), jax.ShapeDtypeStruct((1, N_EXPERTS), jnp.int32)],
        scratch_shapes=[pltpu.VMEM((1, N_EXPERTS), jnp.float32)],
        compiler_params=_params("arbitrary"),
        name="outproj_router",
    )(oa, ob, x2d, mod3, mod3, mod3, gga, ggb, gpost, gpre, w_out, w_router, b_router)
    return outs


def _dispatch_kernel(pad_end_ref, n_used_ref, dest_ref, h_ref, xs_ref, zero_ref, sem, zsem, *, n_tiles):
    i = pl.program_id(0)
    tile = h_ref.shape[0]

    def fill(row):
        return pltpu.make_async_copy(zero_ref, xs_ref.at[pl.ds(pl.multiple_of(row, FFN_TILE), FFN_TILE)], zsem)

    @pl.when(i == 0)
    def _():
        zero_ref[...] = jnp.zeros_like(zero_ref)
        n_tail = n_tiles - n_used_ref[0]

        def start_e(e, c):
            fill(jnp.maximum(pad_end_ref[e] - FFN_TILE, 0)).start()
            return c
        lax.fori_loop(0, N_EXPERTS, start_e, 0)

        def start_t(j, c):
            fill((n_used_ref[0] + j) * FFN_TILE).start()
            return c
        lax.fori_loop(0, n_tail, start_t, 0)

        def wait_all(j, c):
            fill(0).wait()
            return c
        lax.fori_loop(0, N_EXPERTS + n_tail, wait_all, 0)

    def row_copy(a):
        return pltpu.make_async_copy(h_ref.at[pl.ds(a // TOP_K, 1)], xs_ref.at[pl.ds(dest_ref[a], 1)], sem)

    def start(a, c):
        row_copy(a).start()
        return c
    lax.fori_loop(0, tile * TOP_K, start, 0, unroll=8)

    for _k in range(TOP_K):
        pltpu.make_async_copy(h_ref, xs_ref.at[pl.ds(0, tile)], sem).wait()


def _dispatch(h2, dest_flat, pad_end, n_used, *, n_tiles):
    t, d = h2.shape
    tile = DISPATCH_TILE
    grid_spec = pltpu.PrefetchScalarGridSpec(
        num_scalar_prefetch=2,
        grid=(t // tile,),
        in_specs=[pl.BlockSpec((tile * TOP_K,), lambda i, *_: (i,), memory_space=pltpu.SMEM),
                  pl.BlockSpec((tile, d), lambda i, *_: (i, 0))],
        out_specs=pl.BlockSpec(memory_space=pl.ANY),
        scratch_shapes=[pltpu.VMEM((FFN_TILE, d), h2.dtype),
                        pltpu.SemaphoreType.DMA(()), pltpu.SemaphoreType.DMA(())],
    )
    return pl.pallas_call(
        functools.partial(_dispatch_kernel, n_tiles=n_tiles),
        grid_spec=grid_spec,
        out_shape=jax.ShapeDtypeStruct((n_tiles * FFN_TILE, d), h2.dtype),
        compiler_params=_params("arbitrary"),
        name="dispatch",
    )(pad_end, n_used, dest_flat, h2)


def _ffn_kernel(te_ref, n_used_ref, x_ref, w1_ref, b1_ref, w2_ref, b2_ref, y_ref, w1p_ref, w2b_ref):
    i = pl.program_id(0)
    live = i < n_used_ref[0]
    n_pair = w1_ref.shape[2] // FFN_CHUNK
    half = FFN_CHUNK // 2

    @pl.when(live & ((i == 0) | (te_ref[i] != te_ref[jnp.maximum(i - 1, 0)])))
    def _():
        src = lax.broadcasted_iota(jnp.int32, (FFN_CHUNK, FFN_CHUNK), 0)
        dst = lax.broadcasted_iota(jnp.int32, (FFN_CHUNK, FFN_CHUNK), 1)
        unzip = (src == jnp.where(dst < half, 2 * dst, 2 * (dst - half) + 1)).astype(jnp.bfloat16)
        for c in range(n_pair):
            cs = slice(c * FFN_CHUNK, (c + 1) * FFN_CHUNK)
            w1p_ref[:, cs] = _dot(w1_ref[0, :, cs].astype(jnp.bfloat16), unzip).astype(jnp.bfloat16)
        w2b_ref[...] = w2_ref[0].astype(jnp.bfloat16)

    @pl.when(live)
    def _():
        x = x_ref[...].astype(jnp.bfloat16)
        acc = jnp.zeros(y_ref.shape, jnp.float32) + b2_ref[0]
        for c in range(0, n_pair, 2):
            cs = slice(c * FFN_CHUNK, (c + 2) * FFN_CHUNK)
            h = _dot(x, w1p_ref[:, cs]) + b1_ref[0, :, cs]
            g = jnp.concatenate([h[:, :half], h[:, FFN_CHUNK:FFN_CHUNK + half]], axis=1)
            u = jnp.concatenate([h[:, half:FFN_CHUNK], h[:, FFN_CHUNK + half:]], axis=1)
            g = jnp.minimum(g, SWIGLU_LIMIT)
            u = jnp.clip(u, -SWIGLU_LIMIT, SWIGLU_LIMIT)
            act = g * jax.nn.sigmoid(SWIGLU_ALPHA * g) * (u + 1.0)
            acc = acc + _dot(act.astype(jnp.bfloat16), w2b_ref[c * half:(c + 2) * half, :])
        y_ref[...] = acc

    @pl.when(jnp.logical_not(live))
    def _():
        y_ref[...] = jnp.zeros_like(y_ref)


def _unzip_bias(b1):
    e, f2 = b1.shape
    half = FFN_CHUNK // 2
    return b1.reshape(e, f2 // FFN_CHUNK, half, 2).transpose(0, 1, 3, 2).reshape(e, 1, f2)


def _ffn(xs, tile_expert, n_used, w1, b1, w2, b2):
    rows, d = xs.shape
    f2 = w1.shape[2]
    n_tiles = rows // FFN_TILE
    live = lambda i, te, nu: jnp.minimum(i, nu[0] - 1)
    wspec = lambda shape: pl.BlockSpec((1,) + shape, lambda i, te, nu: (te[live(i, te, nu)], 0, 0))
    grid_spec = pltpu.PrefetchScalarGridSpec(
        num_scalar_prefetch=2,
        grid=(n_tiles,),
        in_specs=[pl.BlockSpec((FFN_TILE, d), lambda i, te, nu: (live(i, te, nu), 0)),
                  wspec((d, f2)), wspec((1, f2)), wspec((f2 // 2, d)), wspec((1, d))],
        out_specs=pl.BlockSpec((FFN_TILE, d), lambda i, te, nu: (i, 0)),
        scratch_shapes=[pltpu.VMEM((d, f2), jnp.bfloat16), pltpu.VMEM((f2 // 2, d), jnp.bfloat16)],
    )
    return pl.pallas_call(
        _ffn_kernel,
        grid_spec=grid_spec,
        out_shape=jax.ShapeDtypeStruct((rows, d), jnp.float32),
        compiler_params=_params("arbitrary"),
        name="expert_ffn",
    )(tile_expert, n_used, xs, w1, _unzip_bias(b1), w2, b2[:, None, :])


def _combine_kernel(dest_ref, wts_ref, x1_ref, ga_ref, g_ref, y_ref, o_ref, buf_ref, sem):
    tile = x1_ref.shape[0]

    def row_copy(a):
        return pltpu.make_async_copy(y_ref.at[pl.ds(dest_ref[a], 1)],
                                     buf_ref.at[a % TOP_K, pl.ds(a // TOP_K, 1)], sem)

    def start(a, c):
        row_copy(a).start()
        return c
    lax.fori_loop(0, tile * TOP_K, start, 0, unroll=8)

    for k in range(TOP_K):
        pltpu.make_async_copy(y_ref.at[pl.ds(0, tile)], buf_ref.at[k], sem).wait()

    w = wts_ref[...]
    y = buf_ref[0] * w[:, 0:1]
    for k in range(1, TOP_K):
        y = y + buf_ref[k] * w[:, k:k + 1]
    o_ref[...] = x1_ref[...] + ga_ref[0] * _rms(y, g_ref[...])


def _combine(ys, dest_flat, wts, x1, mod3, g_post, *, seq):
    t, d = x1.shape
    tile = COMBINE_TILE
    per_seq = seq // tile
    return pl.pallas_call(
        _combine_kernel,
        grid=(t // tile,),
        in_specs=[pl.BlockSpec((tile * TOP_K,), lambda i: (i,), memory_space=pltpu.SMEM),
                  pl.BlockSpec((tile, TOP_K), lambda i: (i, 0)),
                  pl.BlockSpec((tile, d), lambda i: (i, 0)),
                  pl.BlockSpec((1, 1, d), lambda i: (i // per_seq, 0, 5)),
                  pl.BlockSpec((1, d), lambda i: (0, 0)),
                  pl.BlockSpec(memory_space=pl.ANY)],
        out_specs=pl.BlockSpec((tile, d), lambda i: (i, 0)),
        out_shape=jax.ShapeDtypeStruct((t, d), jnp.float32),
        scratch_shapes=[pltpu.VMEM((TOP_K, tile, d), ys.dtype), pltpu.SemaphoreType.DMA(())],
        compiler_params=_params("arbitrary"),
        name="combine",
    )(dest_flat, wts, x1, mod3, g_post, ys)


def _rope_tables(seq):
    pos = np.arange(seq)
    n_freq = HEAD_DIM // 4
    freqs = ROPE_BASE ** (-jnp.arange(n_freq, dtype=jnp.float32) / n_freq)
    rows = jnp.asarray(pos // GRID_W, jnp.float32)[:, None] * freqs[None, :]
    cols = jnp.asarray(pos % GRID_W, jnp.float32)[:, None] * freqs[None, :]
    ang = jnp.concatenate([rows, rows, cols, cols], axis=1)
    sign = np.tile(np.repeat([-1.0, 1.0], n_freq), 2).astype(np.float32)
    cos = jnp.cos(ang)
    sin = jnp.sin(ang) * sign[None, :]
    reps = LANES // HEAD_DIM
    return jnp.tile(cos, (1, reps)), jnp.tile(sin, (1, reps))


def kernel(x, c, ctx, c_ctx, w_ada, b_ada, g_pre_mix, g_post_mix, g_pre_ffn, g_post_ffn, w_in, g_grp_a, g_grp_b,
           sink_a, rpb_b, w_out, w_router, b_router, w_mlp1, b_mlp1, w_mlp2, b_mlp2):
    batch, seq, d = x.shape
    n_ctx = ctx.shape[1]
    assert w_ada.shape[0] == 1, "single layer"
    assert seq % (GRID_W * NB_Q_ROWS) == 0 and seq // GRID_W >= NB_K_ROWS and seq >= BLOCK_A + 2 * WINDOW
    n_tok = batch * seq
    bf16 = jnp.bfloat16

    head_order = np.asarray([k * GQA_GROUP + j for j in range(GQA_GROUP) for k in range(N_KV_A)])
    col_order = (head_order[:, None] * HEAD_DIM + np.arange(HEAD_DIM)[None, :]).reshape(-1)

    mod_rows = -(-(batch + 1) // 8) * 8
    cc = jnp.zeros((mod_rows, d), jnp.float32).at[:batch].set(c).at[batch].set(c_ctx)
    mod3 = _ada(cc, w_ada[0], b_ada[0]).reshape(mod_rows, 1, 6 * d)

    w_in0 = w_in[0]
    w_lat = jnp.concatenate([w_in0[:, :WIDTH_A][:, col_order], w_in0[:, WIDTH_A:]], axis=1).astype(bf16)
    kv_cols = np.concatenate([np.arange(WIDTH_A, WIDTH_A + 2 * WIDTH_KV_A),
                              np.arange(WIDTH_A + 2 * WIDTH_KV_A + WIDTH_B, w_in0.shape[1])])
    w_ctx = w_in0[:, kv_cols].astype(bf16)
    g_pre = g_pre_mix[0].reshape(1, d)
    qa, ka, va, qb, kb, vb = _inproj(x.reshape(n_tok, d), mod3, g_pre, w_lat, _rope_tables(seq),
                                     seq=seq, mod_row0=0, latent=True)
    kac, vac, kbc, vbc = _inproj(ctx.reshape(batch * n_ctx, d), mod3, g_pre, w_ctx, None,
                                 seq=n_ctx, mod_row0=batch, latent=False)

    oa = _window_attention(qa, ka.reshape(batch, seq, -1), va.reshape(batch, seq, -1),
                           kac.reshape(batch, n_ctx, -1), vac.reshape(batch, n_ctx, -1),
                           sink_a[0].astype(jnp.float32), batch=batch, seq=seq)
    tabs, cls_of_block = _nb_bias_tables(rpb_b[0], seq // GRID_W)
    ob = _neighbourhood_attention(qb, kb.reshape(batch, seq, -1), vb.reshape(batch, seq, -1),
                                  kbc.reshape(batch, n_ctx, -1), vbc.reshape(batch, n_ctx, -1),
                                  tabs, cls_of_block, batch=batch, seq=seq)

    w_out0 = w_out[0]
    w_o = jnp.concatenate([w_out0[:WIDTH_A][col_order], w_out0[WIDTH_A:]], axis=0).astype(bf16)
    x1, h2, idx, wts, rank, counts = _outproj(
        oa, ob, x.reshape(n_tok, d), mod3,
        g_grp_a[0][col_order].reshape(1, -1), g_grp_b[0].reshape(1, -1),
        g_post_mix[0].reshape(1, d), g_pre_ffn[0].reshape(1, d),
        w_o, w_router[0], b_router[0].reshape(1, -1), seq=seq)

    counts = counts.reshape(-1)
    padded = (counts + FFN_TILE - 1) // FFN_TILE * FFN_TILE
    pad_end = jnp.cumsum(padded).astype(jnp.int32)
    pad_start = pad_end - padded
    n_tiles = n_tok * TOP_K // FFN_TILE + N_EXPERTS
    n_used = (pad_end[-1:] // FFN_TILE).astype(jnp.int32)
    hot = idx[..., None] == jnp.arange(N_EXPERTS, dtype=jnp.int32)
    dest = (rank + jnp.sum(jnp.where(hot, pad_start, 0), axis=-1)).reshape(-1).astype(jnp.int32)
    tile_row0 = jnp.arange(n_tiles, dtype=jnp.int32) * FFN_TILE
    tile_expert = jnp.minimum(jnp.sum(tile_row0[:, None] >= pad_end[None, :], axis=1),
                              N_EXPERTS - 1).astype(jnp.int32)

    xs = _dispatch(h2, dest, pad_end, n_used, n_tiles=n_tiles)
    ys = _ffn(xs, tile_expert, n_used, w_mlp1[0], b_mlp1[0], w_mlp2[0], b_mlp2[0])
    out = _combine(ys, dest, wts, x1, mod3, g_post_ffn[0].reshape(1, d), seq=seq)
    return out.reshape(batch, seq, d)
```

```python
import functools

import numpy as np
import jax
import jax.numpy as jnp
from jax import lax
from jax.experimental import pallas as pl
from jax.experimental.pallas import tpu as pltpu

GRID_W = 64
HEAD_DIM = 64
N_HEADS_A = 8
N_KV_A = 2
GQA_GROUP = N_HEADS_A // N_KV_A
N_HEADS_B = 8
WIDTH_A = N_HEADS_A * HEAD_DIM
WIDTH_KV_A = N_KV_A * HEAD_DIM
WIDTH_B = N_HEADS_B * HEAD_DIM
WINDOW = 128
BLOCK_A = 128
NA_ROWS = 8
NA_COLS = 16
N_EXPERTS = 32
TOP_K = 4
SWIGLU_LIMIT = 7.0
SWIGLU_ALPHA = 1.702
ROPE_BASE = 10000.0
EPS = 1e-6
NEG_INF = -1e30

LANES = 128
VMEM_LIMIT = 56 * 1024 * 1024

NB_Q_ROWS = 2
NB_K_ROWS = NB_Q_ROWS + NA_ROWS
PROJ_TILE = 512
OUT_TILE = 256
FFN_TILE = 256
FFN_CHUNK = 256
DISPATCH_TILE = OUT_TILE
ROW_ALIGN = 8
TAB_WIDTH = 128


def _params(*sem):
    return pltpu.CompilerParams(dimension_semantics=sem, vmem_limit_bytes=VMEM_LIMIT)


def _rms(x, g):
    return x * lax.rsqrt(jnp.mean(x * x, axis=-1, keepdims=True) + EPS) * g


def _dot(a, b):
    return jnp.dot(a, b, preferred_element_type=jnp.float32)


def _dot_nt(a, b):
    return lax.dot_general(a, b, (((1,), (1,)), ((), ())), preferred_element_type=jnp.float32)


def _ada_kernel(c_ref, w_ref, b_ref, o_ref):
    c = c_ref[...]
    s = (c * jax.nn.sigmoid(c)).astype(jnp.bfloat16)
    o_ref[...] = _dot(s, w_ref[...].astype(jnp.bfloat16)) + b_ref[...]


def _ada(cc, w_ada, b_ada):
    rows, d = cc.shape
    n_out = w_ada.shape[1]
    return pl.pallas_call(
        _ada_kernel,
        grid=(n_out // d,),
        in_specs=[pl.BlockSpec((rows, d), lambda j: (0, 0)),
                  pl.BlockSpec((d, d), lambda j: (0, j)),
                  pl.BlockSpec((1, d), lambda j: (0, j))],
        out_specs=pl.BlockSpec((rows, d), lambda j: (0, j)),
        out_shape=jax.ShapeDtypeStruct((rows, n_out), jnp.float32),
        compiler_params=_params("arbitrary"),
        name="ada",
    )(cc, w_ada, b_ada.reshape(1, n_out))


def _rope(x, cos, sin):
    w = x.shape[1]
    reps = w // LANES
    if reps > 1:
        cos = jnp.concatenate([cos] * reps, axis=1)
        sin = jnp.concatenate([sin] * reps, axis=1)
    lane = lax.broadcasted_iota(jnp.int32, x.shape, 1)
    quarter = HEAD_DIM // 4
    partner = jnp.where(lane % (2 * quarter) < quarter,
                        pltpu.roll(x, w - quarter, 1), pltpu.roll(x, quarter, 1))
    return x * cos + partner * sin


def _inproj_kernel(x_ref, sh_ref, sc_ref, g_ref, w_ref, *rest, latent):
    x = x_ref[...]
    h = _rms(x, g_ref[...]) * (1.0 + sc_ref[0]) + sh_ref[0]
    p = _dot(h.astype(jnp.bfloat16), w_ref[...])
    if latent:
        cos_ref, sin_ref, qa_ref, ka_ref, va_ref, qb_ref, kb_ref, vb_ref = rest
        cos, sin = cos_ref[...], sin_ref[...]
        scale = HEAD_DIM ** -0.5
        o = 0
        qa_ref[...] = (_rope(p[:, o:o + WIDTH_A], cos, sin) * scale).astype(qa_ref.dtype)
        o += WIDTH_A
        ka_ref[...] = _rope(p[:, o:o + WIDTH_KV_A], cos, sin).astype(ka_ref.dtype)
        o += WIDTH_KV_A
        va_ref[...] = p[:, o:o + WIDTH_KV_A].astype(va_ref.dtype)
        o += WIDTH_KV_A
        qb_ref[...] = (p[:, o:o + WIDTH_B] * scale).astype(qb_ref.dtype)
        o += WIDTH_B
    else:
        ka_ref, va_ref, kb_ref, vb_ref = rest
        o = 0
        ka_ref[...] = p[:, o:o + WIDTH_KV_A].astype(ka_ref.dtype)
        o += WIDTH_KV_A
        va_ref[...] = p[:, o:o + WIDTH_KV_A].astype(va_ref.dtype)
        o += WIDTH_KV_A
    kb_ref[...] = p[:, o:o + WIDTH_B].astype(kb_ref.dtype)
    o += WIDTH_B
    vb_ref[...] = p[:, o:o + WIDTH_B].astype(vb_ref.dtype)


def _inproj(x2d, mod3, g_pre, w, rope_tabs, *, seq, mod_row0, latent):
    t, d = x2d.shape
    tile = min(PROJ_TILE, seq)
    per_seq = seq // tile
    if latent:
        mod_row = lambda i: i // per_seq
    else:
        mod_row = lambda i: mod_row0
    in_specs = [pl.BlockSpec((tile, d), lambda i: (i, 0)),
                pl.BlockSpec((1, 1, d), lambda i: (mod_row(i), 0, 0)),
                pl.BlockSpec((1, 1, d), lambda i: (mod_row(i), 0, 1)),
                pl.BlockSpec((1, d), lambda i: (0, 0)),
                pl.BlockSpec(w.shape, lambda i: (0, 0))]
    args = [x2d, mod3, mod3, g_pre, w]
    widths = [WIDTH_KV_A, WIDTH_KV_A, WIDTH_B, WIDTH_B]
    if latent:
        in_specs += [pl.BlockSpec((tile, LANES), lambda i: (i % per_seq, 0))] * 2
        args += list(rope_tabs)
        widths = [WIDTH_A, WIDTH_KV_A, WIDTH_KV_A, WIDTH_B, WIDTH_B, WIDTH_B]
    return pl.pallas_call(
        functools.partial(_inproj_kernel, latent=latent),
        grid=(t // tile,),
        in_specs=in_specs,
        out_specs=[pl.BlockSpec((tile, wd), lambda i: (i, 0)) for wd in widths],
        out_shape=[jax.ShapeDtypeStruct((t, wd), jnp.bfloat16) for wd in widths],
        compiler_params=_params("parallel"),
        name="inproj_latent" if latent else "inproj_ctx",
    )(*args)


def _pair_attention(q_ref, heads, o_ref):
    tq = q_ref.shape[0]
    lane = lax.broadcasted_iota(jnp.int32, (tq, LANES), 1)
    low = lane < HEAD_DIM
    for j in range(q_ref.shape[1] // LANES):
        q = q_ref[:, j * LANES:(j + 1) * LANES]
        pair = None
        for half in range(2):
            k_loc, v_loc, k_ctx, v_ctx, bias, sink = heads(j, half)
            qm = jnp.where(low if half == 0 else ~low, q, jnp.zeros_like(q))
            s_loc = _dot_nt(qm, k_loc) + bias
            s_ctx = _dot_nt(qm, k_ctx)
            m = jnp.maximum(jnp.max(s_loc, axis=1, keepdims=True), jnp.max(s_ctx, axis=1, keepdims=True))
            if sink is not None:
                m = jnp.maximum(m, sink)
            p_loc = jnp.exp(s_loc - m)
            p_ctx = jnp.exp(s_ctx - m)
            den = jnp.sum(p_loc, axis=1, keepdims=True) + jnp.sum(p_ctx, axis=1, keepdims=True)
            if sink is not None:
                den = den + jnp.exp(sink - m)
            o = _dot(p_loc.astype(v_loc.dtype), v_loc) + _dot(p_ctx.astype(v_ctx.dtype), v_ctx)
            o = o / den
            pair = o if half == 0 else jnp.where(low, pair, o)
        o_ref[:, j * LANES:(j + 1) * LANES] = pair.astype(o_ref.dtype)


def _win_kernel(sink_ref, q_ref, k_ref, v_ref, kc_ref, vc_ref, o_ref, *, seq):
    i = pl.program_id(1)
    span = BLOCK_A + 2 * WINDOW
    start = pl.multiple_of(jnp.clip(i * BLOCK_A - WINDOW, 0, seq - span), BLOCK_A)
    k_loc = k_ref[0, pl.ds(start, span), :]
    v_loc = v_ref[0, pl.ds(start, span), :]
    k_ctx = kc_ref[0]
    v_ctx = vc_ref[0]
    qpos = i * BLOCK_A + lax.broadcasted_iota(jnp.int32, (BLOCK_A, span), 0)
    kpos = start + lax.broadcasted_iota(jnp.int32, (BLOCK_A, span), 1)
    bias = jnp.where(jnp.abs(kpos - qpos) <= WINDOW, 0.0, NEG_INF).astype(jnp.float32)

    def heads(j, half):
        return k_loc, v_loc, k_ctx, v_ctx, bias, sink_ref[half * GQA_GROUP + j]

    _pair_attention(q_ref, heads, o_ref)


def _window_attention(qa, ka, va, kac, vac, sink, *, batch, seq):
    nb = seq // BLOCK_A
    n_ctx = kac.shape[1]
    return pl.pallas_call(
        functools.partial(_win_kernel, seq=seq),
        grid=(batch, nb),
        in_specs=[pl.BlockSpec(memory_space=pltpu.SMEM),
                  pl.BlockSpec((BLOCK_A, WIDTH_A), lambda b, i: (b * nb + i, 0)),
                  pl.BlockSpec((1, seq, WIDTH_KV_A), lambda b, i: (b, 0, 0)),
                  pl.BlockSpec((1, seq, WIDTH_KV_A), lambda b, i: (b, 0, 0)),
                  pl.BlockSpec((1, n_ctx, WIDTH_KV_A), lambda b, i: (b, 0, 0)),
                  pl.BlockSpec((1, n_ctx, WIDTH_KV_A), lambda b, i: (b, 0, 0))],
        out_specs=pl.BlockSpec((BLOCK_A, WIDTH_A), lambda b, i: (b * nb + i, 0)),
        out_shape=jax.ShapeDtypeStruct((batch * seq, WIDTH_A), jnp.bfloat16),
        compiler_params=_params("parallel", "arbitrary"),
        name="window_attention",
    )(sink, qa, ka, va, kac, vac)


def _nb_kernel(q_ref, k_ref, v_ref, kc_ref, vc_ref, tab_ref, o_ref, *, rows_n):
    m = pl.program_id(1)
    n_keys = NB_K_ROWS * GRID_W
    start_row = jnp.clip(NB_Q_ROWS * m - NA_ROWS // 2, 0, rows_n - NB_K_ROWS)
    start = pl.multiple_of(start_row * GRID_W, LANES)

    def heads(j, half):
        cols = slice(j * LANES, (j + 1) * LANES)
        return (k_ref[0, pl.ds(start, n_keys), cols], v_ref[0, pl.ds(start, n_keys), cols],
                kc_ref[0, :, cols], vc_ref[0, :, cols],
                tab_ref[0, 2 * j + half].astype(jnp.float32), None)

    _pair_attention(q_ref, heads, o_ref)


def _nb_classes(rows_n):
    n_blocks = rows_n // NB_Q_ROWS
    sig = {}
    cls_of_block = []
    reps = []
    for m in range(n_blocks):
        start_row = int(np.clip(NB_Q_ROWS * m - NA_ROWS // 2, 0, rows_n - NB_K_ROWS))
        key = tuple((start_row - r, int(np.clip(r - NA_ROWS // 2, 0, rows_n - NA_ROWS)) - r)
                    for r in range(NB_Q_ROWS * m, NB_Q_ROWS * (m + 1)))
        if key not in sig:
            sig[key] = len(reps)
            reps.append(m)
        cls_of_block.append(sig[key])
    return np.asarray(cls_of_block, np.int32), reps


def _nb_bias_tables(rpb, rows_n):
    cls_of_block, reps = _nb_classes(rows_n)
    n_heads = rpb.shape[0]
    cq = np.arange(GRID_W)[:, None]
    ck = np.arange(GRID_W)[None, :]
    cs = np.clip(cq - NA_COLS // 2, 0, GRID_W - NA_COLS)
    col_ok = (ck >= cs) & (ck < cs + NA_COLS)
    pick = ((ck - cq + NA_COLS - 1)[None] == np.arange(2 * NA_COLS - 1)[:, None, None]) & col_ok[None]
    tiles = jnp.einsum('hrd,dqk->hrqk', rpb.astype(jnp.float32), jnp.asarray(pick, jnp.float32),
                       precision=lax.Precision.HIGHEST)
    tiles = jnp.where(jnp.asarray(col_ok)[None, None], tiles, NEG_INF).astype(jnp.bfloat16)
    blocked = jnp.full((n_heads, GRID_W, GRID_W), NEG_INF, jnp.bfloat16)
    tabs = []
    for m in reps:
        start_row = int(np.clip(NB_Q_ROWS * m - NA_ROWS // 2, 0, rows_n - NB_K_ROWS))
        q_rows = []
        for r in range(NB_Q_ROWS * m, NB_Q_ROWS * (m + 1)):
            rs = int(np.clip(r - NA_ROWS // 2, 0, rows_n - NA_ROWS))
            q_rows.append(jnp.concatenate(
                [tiles[:, krow - r + NA_ROWS - 1] if rs <= krow < rs + NA_ROWS else blocked
                 for krow in range(start_row, start_row + NB_K_ROWS)], axis=2))
        tabs.append(jnp.concatenate(q_rows, axis=1))
    return jnp.stack(tabs), cls_of_block


def _neighbourhood_attention(qb, kb, vb, kbc, vbc, tabs, cls_of_block, *, batch, seq):
    rows_n = seq // GRID_W
    nblk = rows_n // NB_Q_ROWS
    tq = NB_Q_ROWS * GRID_W
    n_ctx = kbc.shape[1]
    grid_spec = pltpu.PrefetchScalarGridSpec(
        num_scalar_prefetch=1,
        grid=(batch, nblk),
        in_specs=[pl.BlockSpec((tq, WIDTH_B), lambda b, m, c: (b * nblk + m, 0)),
                  pl.BlockSpec((1, seq, WIDTH_B), lambda b, m, c: (b, 0, 0)),
                  pl.BlockSpec((1, seq, WIDTH_B), lambda b, m, c: (b, 0, 0)),
                  pl.BlockSpec((1, n_ctx, WIDTH_B), lambda b, m, c: (b, 0, 0)),
                  pl.BlockSpec((1, n_ctx, WIDTH_B), lambda b, m, c: (b, 0, 0)),
                  pl.BlockSpec((1,) + tabs.shape[1:], lambda b, m, c: (c[m], 0, 0, 0))],
        out_specs=pl.BlockSpec((tq, WIDTH_B), lambda b, m, c: (b * nblk + m, 0)),
    )

    def body(c_ref, *refs):
        _nb_kernel(*refs, rows_n=rows_n)

    return pl.pallas_call(
        body,
        grid_spec=grid_spec,
        out_shape=jax.ShapeDtypeStruct((batch * seq, WIDTH_B), jnp.bfloat16),
        compiler_params=_params("parallel", "arbitrary"),
        name="neighbourhood_attention",
    )(jnp.asarray(cls_of_block), qb, kb, vb, kbc, vbc, tabs)


def _outproj_kernel(oa_ref, ob_ref, x_ref, ga_ref, sh_ref, sc_ref, gga_ref, ggb_ref, gpost_ref, gpre_ref,
                    wo_ref, wr_ref, br_ref,
                    x1_ref, h2_ref, idx_ref, wts_ref, rank_ref, size_ref, base_ref, cnt_ref, run_ref):
    i = pl.program_id(0)

    @pl.when(i == 0)
    def _():
        run_ref[...] = jnp.zeros_like(run_ref)

    na = _rms(oa_ref[...].astype(jnp.float32), gga_ref[...]).astype(jnp.bfloat16)
    nb = _rms(ob_ref[...].astype(jnp.float32), ggb_ref[...]).astype(jnp.bfloat16)
    mix = _dot(na, wo_ref[:WIDTH_A, :]) + _dot(nb, wo_ref[WIDTH_A:, :])
    x1 = x_ref[...] + ga_ref[0] * _rms(mix, gpost_ref[...])
    x1_ref[...] = x1
    h2 = _rms(x1, gpre_ref[...]) * (1.0 + sc_ref[0]) + sh_ref[0]
    h2_ref[...] = h2.astype(h2_ref.dtype)

    logits = jnp.dot(h2, wr_ref[...], preferred_element_type=jnp.float32,
                     precision=lax.Precision.HIGHEST) + br_ref[...]
    t, e = logits.shape
    lane = lax.broadcasted_iota(jnp.int32, (t, e), 1)
    work = logits
    chosen = jnp.zeros((t, e), jnp.float32)
    vals, sels, hots = [], [], []
    for _k in range(TOP_K):
        mx = jnp.max(work, axis=1, keepdims=True)
        sel = jnp.min(jnp.where(work == mx, lane, e), axis=1, keepdims=True)
        hot = lane == sel
        vals.append(mx)
        sels.append(sel)
        hots.append(hot)
        work = jnp.where(hot, -jnp.inf, work)
        chosen = chosen + hot.astype(jnp.float32)
    ex = [jnp.exp(v - vals[0]) for v in vals]
    den = ex[0] + ex[1] + ex[2] + ex[3]

    r_i = lax.broadcasted_iota(jnp.int32, (t, t), 0)
    c_i = lax.broadcasted_iota(jnp.int32, (t, t), 1)
    before = (c_i < r_i).astype(jnp.bfloat16)
    prefix = _dot(before, chosen.astype(jnp.bfloat16))
    ranks = [jnp.sum(jnp.where(h, prefix, 0.0), axis=1, keepdims=True) for h in hots]
    size = jnp.sum(chosen, axis=0, keepdims=True).astype(jnp.int32)
    size = (size + (ROW_ALIGN - 1)) // ROW_ALIGN * ROW_ALIGN
    size_ref[0] = size
    base_ref[0] = run_ref[...]
    run_ref[...] = run_ref[...] + size
    cnt_ref[...] = run_ref[...]

    l4 = lax.broadcasted_iota(jnp.int32, (t, TOP_K), 1)

    def cols(parts):
        out = parts[TOP_K - 1]
        for k in range(TOP_K - 2, -1, -1):
            out = jnp.where(l4 == k, parts[k], out)
        return out

    idx_ref[...] = cols(sels)
    wts_ref[...] = cols([v / den for v in ex])
    rank_ref[...] = cols(ranks).astype(jnp.int32)


def _outproj(oa, ob, x2d, mod3, gga, ggb, gpost, gpre, w_out, w_router, b_router, *, seq):
    t, d = x2d.shape
    tile = OUT_TILE
    per_seq = seq // tile
    row = lambda i: (i, 0)
    const = lambda i: (0, 0)
    modspec = lambda part: pl.BlockSpec((1, 1, d), lambda i: (i // per_seq, 0, part))
    outs = pl.pallas_call(
        _outproj_kernel,
        grid=(t // tile,),
        in_specs=[pl.BlockSpec((tile, WIDTH_A), row), pl.BlockSpec((tile, WIDTH_B), row),
                  pl.BlockSpec((tile, d), row),
                  modspec(2), modspec(3), modspec(4),
                  pl.BlockSpec((1, WIDTH_A), const), pl.BlockSpec((1, WIDTH_B), const),
                  pl.BlockSpec((1, d), const), pl.BlockSpec((1, d), const),
                  pl.BlockSpec(w_out.shape, const), pl.BlockSpec(w_router.shape, const),
                  pl.BlockSpec((1, N_EXPERTS), const)],
        out_specs=[pl.BlockSpec((tile, d), row), pl.BlockSpec((tile, d), row),
                   pl.BlockSpec((tile, TOP_K), row), pl.BlockSpec((tile, TOP_K), row),
                   pl.BlockSpec((tile, TOP_K), row),
                   pl.BlockSpec((1, 1, N_EXPERTS), lambda i: (i, 0, 0)),
                   pl.BlockSpec((1, 1, N_EXPERTS), lambda i: (i, 0, 0)),
                   pl.BlockSpec((1, N_EXPERTS), const)],
        out_shape=[jax.ShapeDtypeStruct((t, d), jnp.float32), jax.ShapeDtypeStruct((t, d), jnp.bfloat16),
                   jax.ShapeDtypeStruct((t, TOP_K), jnp.int32), jax.ShapeDtypeStruct((t, TOP_K), jnp.float32),
                   jax.ShapeDtypeStruct((t, TOP_K), jnp.int32),
                   jax.ShapeDtypeStruct((t // tile, 1, N_EXPERTS), jnp.int32),
                   jax.ShapeDtypeStruct((t // tile, 1, N_EXPERTS), jnp.int32),
                   jax.ShapeDtypeStruct((1, N_EXPERTS), jnp.int32)],
        scratch_shapes=[pltpu.VMEM((1, N_EXPERTS), jnp.int32)],
        compiler_params=_params("arbitrary"),
        name="outproj_router",
    )(oa, ob, x2d, mod3, mod3, mod3, gga, ggb, gpost, gpre, w_out, w_router, b_router)
    return outs


def _piece_table(tab_ref, e):
    return (pl.multiple_of(tab_ref[e], ROW_ALIGN), pl.multiple_of(tab_ref[N_EXPERTS + e], ROW_ALIGN),
            pl.multiple_of(tab_ref[2 * N_EXPERTS + e], ROW_ALIGN))


def _dispatch_kernel(pad_end_ref, n_used_ref, tab_ref, h_ref, lpos_ref, w_ref, xs_ref,
                     stage_ref, zero_ref, pend_ref, sems, zsem, *, n_tiles):
    i = pl.program_id(0)
    n_steps = pl.num_programs(0)
    slot = i % 2
    n_local, width = stage_ref.shape[1:]
    d = h_ref.shape[1]

    def fill(row):
        return pltpu.make_async_copy(zero_ref, xs_ref.at[pl.ds(pl.multiple_of(row, FFN_TILE), FFN_TILE)], zsem)

    def drain(s):
        n = pl.multiple_of(pend_ref[s], ROW_ALIGN)

        @pl.when(n > 0)
        def _():
            pltpu.make_async_copy(stage_ref.at[s, pl.ds(0, n)], xs_ref.at[pl.ds(0, n)], sems.at[s]).wait()
        pend_ref[s] = 0

    @pl.when(i == 0)
    def _():
        zero_ref[...] = jnp.zeros_like(zero_ref)
        pend_ref[0] = 0
        pend_ref[1] = 0
        n_tail = n_tiles - n_used_ref[0]

        def start_e(e, c):
            fill(jnp.maximum(pad_end_ref[e] - FFN_TILE, 0)).start()
            return c
        lax.fori_loop(0, N_EXPERTS, start_e, 0)

        def start_t(j, c):
            fill((n_used_ref[0] + j) * FFN_TILE).start()
            return c
        lax.fori_loop(0, n_tail, start_t, 0)

        def wait_all(j, c):
            fill(0).wait()
            return c
        lax.fori_loop(0, N_EXPERTS + n_tail, wait_all, 0)

    row = lax.broadcasted_iota(jnp.int32, (n_local, h_ref.shape[0]), 0)
    place = None
    row_w = None
    for k in range(TOP_K):
        hit = row == lpos_ref[0, k:k + 1, :]
        place = hit if place is None else place | hit
        wk = jnp.sum(jnp.where(hit, w_ref[0, k:k + 1, :], 0.0), axis=1, keepdims=True)
        row_w = wk if row_w is None else row_w + wk
    rows = _dot(place.astype(jnp.bfloat16), h_ref[...])

    drain(slot)
    stage_ref[slot, :, :d] = rows
    stage_ref[slot, :, d:] = jnp.broadcast_to(row_w, (n_local, width - d))

    def piece(e, total):
        loc, glob, n = _piece_table(tab_ref, e)

        @pl.when(n > 0)
        def _():
            pltpu.make_async_copy(stage_ref.at[slot, pl.ds(loc, n)], xs_ref.at[pl.ds(glob, n)], sems.at[slot]).start()
        return total + n
    pend_ref[slot] = lax.fori_loop(0, N_EXPERTS, piece, 0)

    @pl.when(i == n_steps - 1)
    def _():
        drain(0)
        drain(1)


def _dispatch(h2, tab_flat, lpos_t, w_t, pad_end, n_used, *, n_tiles):
    t, d = h2.shape
    tile = DISPATCH_TILE
    n_local = tile * TOP_K + N_EXPERTS * ROW_ALIGN
    width = d + LANES
    grid_spec = pltpu.PrefetchScalarGridSpec(
        num_scalar_prefetch=2,
        grid=(t // tile,),
        in_specs=[pl.BlockSpec((TAB_WIDTH,), lambda i, *_: (i,), memory_space=pltpu.SMEM),
                  pl.BlockSpec((tile, d), lambda i, *_: (i, 0)),
                  pl.BlockSpec((1, TOP_K, tile), lambda i, *_: (i, 0, 0)),
                  pl.BlockSpec((1, TOP_K, tile), lambda i, *_: (i, 0, 0))],
        out_specs=pl.BlockSpec(memory_space=pl.ANY),
        scratch_shapes=[pltpu.VMEM((2, n_local, width), jnp.float32),
                        pltpu.VMEM((FFN_TILE, width), jnp.float32),
                        pltpu.SMEM((2,), jnp.int32),
                        pltpu.SemaphoreType.DMA((2,)), pltpu.SemaphoreType.DMA(())],
    )
    return pl.pallas_call(
        functools.partial(_dispatch_kernel, n_tiles=n_tiles),
        grid_spec=grid_spec,
        out_shape=jax.ShapeDtypeStruct((n_tiles * FFN_TILE, width), jnp.float32),
        compiler_params=_params("arbitrary"),
        name="dispatch",
    )(pad_end, n_used, tab_flat, h2, lpos_t, w_t)


def _ffn_kernel(te_ref, n_used_ref, x_ref, w1_ref, b1_ref, w2_ref, b2_ref, y_ref, w1p_ref, w2b_ref, act_ref):
    i = pl.program_id(0)
    live = i < n_used_ref[0]
    n_pair = w1_ref.shape[2] // FFN_CHUNK
    half = FFN_CHUNK // 2

    @pl.when(live & ((i == 0) | (te_ref[i] != te_ref[jnp.maximum(i - 1, 0)])))
    def _():
        src = lax.broadcasted_iota(jnp.int32, (FFN_CHUNK, FFN_CHUNK), 0)
        dst = lax.broadcasted_iota(jnp.int32, (FFN_CHUNK, FFN_CHUNK), 1)
        unzip = (src == jnp.where(dst < half, 2 * dst, 2 * (dst - half) + 1)).astype(jnp.bfloat16)
        for c in range(n_pair):
            cs = slice(c * FFN_CHUNK, (c + 1) * FFN_CHUNK)
            w1p_ref[:, cs] = _dot(w1_ref[0, :, cs].astype(jnp.bfloat16), unzip).astype(jnp.bfloat16)
        w2b_ref[...] = w2_ref[0].astype(jnp.bfloat16)

    @pl.when(live)
    def _():
        d = y_ref.shape[1]
        x = x_ref[:, :d].astype(jnp.bfloat16)
        for c in range(0, n_pair, 2):
            cs = slice(c * FFN_CHUNK, (c + 2) * FFN_CHUNK)
            h = _dot(x, w1p_ref[:, cs]) + b1_ref[0, :, cs]
            g = jnp.concatenate([h[:, :half], h[:, FFN_CHUNK:FFN_CHUNK + half]], axis=1)
            u = jnp.concatenate([h[:, half:FFN_CHUNK], h[:, FFN_CHUNK + half:]], axis=1)
            g = jnp.minimum(g, SWIGLU_LIMIT)
            u = jnp.clip(u, -SWIGLU_LIMIT, SWIGLU_LIMIT)
            act = g * jax.nn.sigmoid(SWIGLU_ALPHA * g) * (u + 1.0)
            act_ref[:, c * half:(c + 2) * half] = act.astype(act_ref.dtype)
        y_ref[...] = (_dot(act_ref[...], w2b_ref[...]) + b2_ref[0]) * x_ref[:, d:d + 1]

    @pl.when(jnp.logical_not(live))
    def _():
        y_ref[...] = jnp.zeros_like(y_ref)


def _unzip_bias(b1):
    e, f2 = b1.shape
    half = FFN_CHUNK // 2
    return b1.reshape(e, f2 // FFN_CHUNK, half, 2).transpose(0, 1, 3, 2).reshape(e, 1, f2)


def _ffn(xs, tile_expert, n_used, w1, b1, w2, b2):
    rows, width = xs.shape
    d, f2 = w1.shape[1:]
    n_tiles = rows // FFN_TILE
    live = lambda i, te, nu: jnp.minimum(i, nu[0] - 1)
    wspec = lambda shape: pl.BlockSpec((1,) + shape, lambda i, te, nu: (te[live(i, te, nu)], 0, 0))
    grid_spec = pltpu.PrefetchScalarGridSpec(
        num_scalar_prefetch=2,
        grid=(n_tiles,),
        in_specs=[pl.BlockSpec((FFN_TILE, width), lambda i, te, nu: (live(i, te, nu), 0)),
                  wspec((d, f2)), wspec((1, f2)), wspec((f2 // 2, d)), wspec((1, d))],
        out_specs=pl.BlockSpec((FFN_TILE, d), lambda i, te, nu: (i, 0)),
        scratch_shapes=[pltpu.VMEM((d, f2), jnp.bfloat16), pltpu.VMEM((f2 // 2, d), jnp.bfloat16),
                        pltpu.VMEM((FFN_TILE, f2 // 2), jnp.bfloat16)],
    )
    return pl.pallas_call(
        _ffn_kernel,
        grid_spec=grid_spec,
        out_shape=jax.ShapeDtypeStruct((rows, d), jnp.float32),
        compiler_params=_params("arbitrary"),
        name="expert_ffn",
    )(tile_expert, n_used, xs, w1, _unzip_bias(b1), w2, b2[:, None, :])


def _combine_kernel(tab_ref, next_tab_ref, lpos_ref, x1_ref, ga_ref, g_ref, y_ref, o_ref, buf_ref, sems):
    i = pl.program_id(0)
    n_steps = pl.num_programs(0)
    slot = i % 2
    tile = x1_ref.shape[0]
    n_local = buf_ref.shape[1]

    def fetch(tab, s):
        def piece(e, c):
            loc, glob, n = _piece_table(tab, e)

            @pl.when(n > 0)
            def _():
                pltpu.make_async_copy(y_ref.at[pl.ds(glob, n)], buf_ref.at[s, pl.ds(loc, n)], sems.at[s]).start()
            return c
        lax.fori_loop(0, N_EXPERTS, piece, 0)

    @pl.when(i == 0)
    def _():
        buf_ref[...] = jnp.zeros_like(buf_ref)
        fetch(tab_ref, 0)

    @pl.when(i + 1 < n_steps)
    def _():
        fetch(next_tab_ref, 1 - slot)

    n = pl.multiple_of(tab_ref[3 * N_EXPERTS], ROW_ALIGN)

    @pl.when(n > 0)
    def _():
        pltpu.make_async_copy(y_ref.at[pl.ds(0, n)], buf_ref.at[slot, pl.ds(0, n)], sems.at[slot]).wait()

    col = lax.broadcasted_iota(jnp.int32, (tile, n_local), 1)
    pos = lpos_ref[...]
    pick = col == pos[:, 0:1]
    for k in range(1, TOP_K):
        pick = pick | (col == pos[:, k:k + 1])
    pick = pick.astype(jnp.bfloat16)
    rows = buf_ref[slot]
    hi = rows.astype(jnp.bfloat16)
    lo = (rows - hi.astype(jnp.float32)).astype(jnp.bfloat16)
    y = _dot(pick, hi) + _dot(pick, lo)
    o_ref[...] = x1_ref[...] + ga_ref[0] * _rms(y, g_ref[...])


def _combine(ys, tab_flat, lpos, x1, mod3, g_post, *, seq):
    t, d = x1.shape
    tile = DISPATCH_TILE
    n_steps = t // tile
    per_seq = seq // tile
    n_local = tile * TOP_K + N_EXPERTS * ROW_ALIGN
    return pl.pallas_call(
        _combine_kernel,
        grid=(n_steps,),
        in_specs=[pl.BlockSpec((TAB_WIDTH,), lambda i: (i,), memory_space=pltpu.SMEM),
                  pl.BlockSpec((TAB_WIDTH,), lambda i: (jnp.minimum(i + 1, n_steps - 1),), memory_space=pltpu.SMEM),
                  pl.BlockSpec((tile, TOP_K), lambda i: (i, 0)),
                  pl.BlockSpec((tile, d), lambda i: (i, 0)),
                  pl.BlockSpec((1, 1, d), lambda i: (i // per_seq, 0, 5)),
                  pl.BlockSpec((1, d), lambda i: (0, 0)),
                  pl.BlockSpec(memory_space=pl.ANY)],
        out_specs=pl.BlockSpec((tile, d), lambda i: (i, 0)),
        out_shape=jax.ShapeDtypeStruct((t, d), jnp.float32),
        scratch_shapes=[pltpu.VMEM((2, n_local, d), ys.dtype), pltpu.SemaphoreType.DMA((2,))],
        compiler_params=_params("arbitrary"),
        name="combine",
    )(tab_flat, tab_flat, lpos, x1, mod3, g_post, ys)


def _rope_tables(seq):
    pos = np.arange(seq)
    n_freq = HEAD_DIM // 4
    freqs = ROPE_BASE ** (-jnp.arange(n_freq, dtype=jnp.float32) / n_freq)
    rows = jnp.asarray(pos // GRID_W, jnp.float32)[:, None] * freqs[None, :]
    cols = jnp.asarray(pos % GRID_W, jnp.float32)[:, None] * freqs[None, :]
    ang = jnp.concatenate([rows, rows, cols, cols], axis=1)
    sign = np.tile(np.repeat([-1.0, 1.0], n_freq), 2).astype(np.float32)
    cos = jnp.cos(ang)
    sin = jnp.sin(ang) * sign[None, :]
    reps = LANES // HEAD_DIM
    return jnp.tile(cos, (1, reps)), jnp.tile(sin, (1, reps))


def kernel(x, c, ctx, c_ctx, w_ada, b_ada, g_pre_mix, g_post_mix, g_pre_ffn, g_post_ffn, w_in, g_grp_a, g_grp_b,
           sink_a, rpb_b, w_out, w_router, b_router, w_mlp1, b_mlp1, w_mlp2, b_mlp2):
    batch, seq, d = x.shape
    n_ctx = ctx.shape[1]
    assert w_ada.shape[0] == 1, "single layer"
    assert seq % (GRID_W * NB_Q_ROWS) == 0 and seq // GRID_W >= NB_K_ROWS and seq >= BLOCK_A + 2 * WINDOW
    n_tok = batch * seq
    bf16 = jnp.bfloat16

    head_order = np.asarray([k * GQA_GROUP + j for j in range(GQA_GROUP) for k in range(N_KV_A)])
    col_order = (head_order[:, None] * HEAD_DIM + np.arange(HEAD_DIM)[None, :]).reshape(-1)

    mod_rows = -(-(batch + 1) // 8) * 8
    cc = jnp.zeros((mod_rows, d), jnp.float32).at[:batch].set(c).at[batch].set(c_ctx)
    mod3 = _ada(cc, w_ada[0], b_ada[0]).reshape(mod_rows, 1, 6 * d)

    w_in0 = w_in[0]
    w_lat = jnp.concatenate([w_in0[:, :WIDTH_A][:, col_order], w_in0[:, WIDTH_A:]], axis=1).astype(bf16)
    kv_cols = np.concatenate([np.arange(WIDTH_A, WIDTH_A + 2 * WIDTH_KV_A),
                              np.arange(WIDTH_A + 2 * WIDTH_KV_A + WIDTH_B, w_in0.shape[1])])
    w_ctx = w_in0[:, kv_cols].astype(bf16)
    g_pre = g_pre_mix[0].reshape(1, d)
    qa, ka, va, qb, kb, vb = _inproj(x.reshape(n_tok, d), mod3, g_pre, w_lat, _rope_tables(seq),
                                     seq=seq, mod_row0=0, latent=True)
    kac, vac, kbc, vbc = _inproj(ctx.reshape(batch * n_ctx, d), mod3, g_pre, w_ctx, None,
                                 seq=n_ctx, mod_row0=batch, latent=False)

    oa = _window_attention(qa, ka.reshape(batch, seq, -1), va.reshape(batch, seq, -1),
                           kac.reshape(batch, n_ctx, -1), vac.reshape(batch, n_ctx, -1),
                           sink_a[0].astype(jnp.float32), batch=batch, seq=seq)
    tabs, cls_of_block = _nb_bias_tables(rpb_b[0], seq // GRID_W)
    ob = _neighbourhood_attention(qb, kb.reshape(batch, seq, -1), vb.reshape(batch, seq, -1),
                                  kbc.reshape(batch, n_ctx, -1), vbc.reshape(batch, n_ctx, -1),
                                  tabs, cls_of_block, batch=batch, seq=seq)

    w_out0 = w_out[0]
    w_o = jnp.concatenate([w_out0[:WIDTH_A][col_order], w_out0[WIDTH_A:]], axis=0).astype(bf16)
    x1, h2, idx, wts, rank, size, base, counts = _outproj(
        oa, ob, x.reshape(n_tok, d), mod3,
        g_grp_a[0][col_order].reshape(1, -1), g_grp_b[0].reshape(1, -1),
        g_post_mix[0].reshape(1, d), g_pre_ffn[0].reshape(1, d),
        w_o, w_router[0], b_router[0].reshape(1, -1), seq=seq)

    n_tok_tiles = n_tok // DISPATCH_TILE
    size = size.reshape(n_tok_tiles, N_EXPERTS)
    counts = counts.reshape(-1)
    padded = (counts + FFN_TILE - 1) // FFN_TILE * FFN_TILE
    pad_end = jnp.cumsum(padded).astype(jnp.int32)
    pad_start = pad_end - padded
    n_tiles = (n_tok * TOP_K + n_tok_tiles * N_EXPERTS * (ROW_ALIGN - 1)) // FFN_TILE + N_EXPERTS
    n_used = (pad_end[-1:] // FFN_TILE).astype(jnp.int32)
    tile_row0 = jnp.arange(n_tiles, dtype=jnp.int32) * FFN_TILE
    tile_expert = jnp.minimum(jnp.sum(tile_row0[:, None] >= pad_end[None, :], axis=1),
                              N_EXPERTS - 1).astype(jnp.int32)
    local0 = jnp.cumsum(size, axis=1) - size
    global0 = pad_start[None, :] + base.reshape(n_tok_tiles, N_EXPERTS)
    tab = jnp.concatenate([local0, global0, size, jnp.sum(size, axis=1, keepdims=True),
                           jnp.zeros((n_tok_tiles, TAB_WIDTH - 3 * N_EXPERTS - 1), jnp.int32)], axis=1)
    tab = tab.reshape(-1).astype(jnp.int32)
    hot = idx.reshape(n_tok_tiles, DISPATCH_TILE, TOP_K, 1) == jnp.arange(N_EXPERTS, dtype=jnp.int32)
    lpos = rank.reshape(n_tok_tiles, DISPATCH_TILE, TOP_K) + jnp.sum(
        jnp.where(hot, local0[:, None, None, :], 0), axis=-1)
    lpos_t = jnp.swapaxes(lpos, 1, 2)
    w_t = jnp.swapaxes(wts.reshape(n_tok_tiles, DISPATCH_TILE, TOP_K), 1, 2)

    xs = _dispatch(h2, tab, lpos_t, w_t, pad_end, n_used, n_tiles=n_tiles)
    ys = _ffn(xs, tile_expert, n_used, w_mlp1[0], b_mlp1[0], w_mlp2[0], b_mlp2[0])
    out = _combine(ys, tab, lpos.reshape(n_tok, TOP_K), x1, mod3, g_post_ffn[0].reshape(1, d), seq=seq)
    return out.reshape(batch, seq, d)
```

```python
import functools

import numpy as np
import jax
import jax.numpy as jnp
from jax import lax
from jax.experimental import pallas as pl
from jax.experimental.pallas import tpu as pltpu

GRID_W = 64
HEAD_DIM = 64
N_HEADS_A = 8
N_KV_A = 2
GQA_GROUP = N_HEADS_A // N_KV_A
N_HEADS_B = 8
WIDTH_A = N_HEADS_A * HEAD_DIM
WIDTH_KV_A = N_KV_A * HEAD_DIM
WIDTH_B = N_HEADS_B * HEAD_DIM
WINDOW = 128
BLOCK_A = 128
NA_ROWS = 8
NA_COLS = 16
N_EXPERTS = 32
TOP_K = 4
SWIGLU_LIMIT = 7.0
SWIGLU_ALPHA = 1.702
ROPE_BASE = 10000.0
EPS = 1e-6
NEG_INF = -1e30

LANES = 128
VMEM_LIMIT = 56 * 1024 * 1024

NB_Q_ROWS = 2
NB_K_ROWS = NB_Q_ROWS + NA_ROWS
PROJ_TILE = 512
OUT_TILE = 256
FFN_TILE = 256
FFN_STEP = 4 * FFN_TILE
FFN_CHUNK = 256
DISPATCH_TILE = OUT_TILE
ROW_ALIGN = 8
TAB_WIDTH = 128


def _params(*sem):
    return pltpu.CompilerParams(dimension_semantics=sem, vmem_limit_bytes=VMEM_LIMIT)


def _rms(x, g):
    return x * lax.rsqrt(jnp.mean(x * x, axis=-1, keepdims=True) + EPS) * g


def _dot(a, b):
    return jnp.dot(a, b, preferred_element_type=jnp.float32)


def _dot_nt(a, b):
    return lax.dot_general(a, b, (((1,), (1,)), ((), ())), preferred_element_type=jnp.float32)


def _ada_kernel(c_ref, w_ref, b_ref, o_ref):
    c = c_ref[...]
    s = (c * jax.nn.sigmoid(c)).astype(jnp.bfloat16)
    o_ref[...] = _dot(s, w_ref[...].astype(jnp.bfloat16)) + b_ref[...]


def _ada(cc, w_ada, b_ada):
    rows, d = cc.shape
    n_out = w_ada.shape[1]
    return pl.pallas_call(
        _ada_kernel,
        grid=(n_out // d,),
        in_specs=[pl.BlockSpec((rows, d), lambda j: (0, 0)),
                  pl.BlockSpec((d, d), lambda j: (0, j)),
                  pl.BlockSpec((1, d), lambda j: (0, j))],
        out_specs=pl.BlockSpec((rows, d), lambda j: (0, j)),
        out_shape=jax.ShapeDtypeStruct((rows, n_out), jnp.float32),
        compiler_params=_params("arbitrary"),
        name="ada",
    )(cc, w_ada, b_ada.reshape(1, n_out))


def _rope(x, cos, sin):
    w = x.shape[1]
    reps = w // LANES
    if reps > 1:
        cos = jnp.concatenate([cos] * reps, axis=1)
        sin = jnp.concatenate([sin] * reps, axis=1)
    lane = lax.broadcasted_iota(jnp.int32, x.shape, 1)
    quarter = HEAD_DIM // 4
    partner = jnp.where(lane % (2 * quarter) < quarter,
                        pltpu.roll(x, w - quarter, 1), pltpu.roll(x, quarter, 1))
    return x * cos + partner * sin


def _inproj_kernel(x_ref, sh_ref, sc_ref, g_ref, w_ref, *rest, latent):
    x = x_ref[...]
    h = _rms(x, g_ref[...]) * (1.0 + sc_ref[0]) + sh_ref[0]
    p = _dot(h.astype(jnp.bfloat16), w_ref[...])
    if latent:
        cos_ref, sin_ref, qa_ref, ka_ref, va_ref, qb_ref, kb_ref, vb_ref = rest
        cos, sin = cos_ref[...], sin_ref[...]
        scale = HEAD_DIM ** -0.5
        o = 0
        qa_ref[...] = (_rope(p[:, o:o + WIDTH_A], cos, sin) * scale).astype(qa_ref.dtype)
        o += WIDTH_A
        ka_ref[...] = _rope(p[:, o:o + WIDTH_KV_A], cos, sin).astype(ka_ref.dtype)
        o += WIDTH_KV_A
        va_ref[...] = p[:, o:o + WIDTH_KV_A].astype(va_ref.dtype)
        o += WIDTH_KV_A
        qb_ref[...] = (p[:, o:o + WIDTH_B] * scale).astype(qb_ref.dtype)
        o += WIDTH_B
    else:
        ka_ref, va_ref, kb_ref, vb_ref = rest
        o = 0
        ka_ref[...] = p[:, o:o + WIDTH_KV_A].astype(ka_ref.dtype)
        o += WIDTH_KV_A
        va_ref[...] = p[:, o:o + WIDTH_KV_A].astype(va_ref.dtype)
        o += WIDTH_KV_A
    kb_ref[...] = p[:, o:o + WIDTH_B].astype(kb_ref.dtype)
    o += WIDTH_B
    vb_ref[...] = p[:, o:o + WIDTH_B].astype(vb_ref.dtype)


def _inproj(x2d, mod3, g_pre, w, rope_tabs, *, seq, mod_row0, latent):
    t, d = x2d.shape
    tile = min(PROJ_TILE, seq)
    per_seq = seq // tile
    if latent:
        mod_row = lambda i: i // per_seq
    else:
        mod_row = lambda i: mod_row0
    in_specs = [pl.BlockSpec((tile, d), lambda i: (i, 0)),
                pl.BlockSpec((1, 1, d), lambda i: (mod_row(i), 0, 0)),
                pl.BlockSpec((1, 1, d), lambda i: (mod_row(i), 0, 1)),
                pl.BlockSpec((1, d), lambda i: (0, 0)),
                pl.BlockSpec(w.shape, lambda i: (0, 0))]
    args = [x2d, mod3, mod3, g_pre, w]
    widths = [WIDTH_KV_A, WIDTH_KV_A, WIDTH_B, WIDTH_B]
    if latent:
        in_specs += [pl.BlockSpec((tile, LANES), lambda i: (i % per_seq, 0))] * 2
        args += list(rope_tabs)
        widths = [WIDTH_A, WIDTH_KV_A, WIDTH_KV_A, WIDTH_B, WIDTH_B, WIDTH_B]
    return pl.pallas_call(
        functools.partial(_inproj_kernel, latent=latent),
        grid=(t // tile,),
        in_specs=in_specs,
        out_specs=[pl.BlockSpec((tile, wd), lambda i: (i, 0)) for wd in widths],
        out_shape=[jax.ShapeDtypeStruct((t, wd), jnp.bfloat16) for wd in widths],
        compiler_params=_params("parallel"),
        name="inproj_latent" if latent else "inproj_ctx",
    )(*args)


def _pair_attention(q_ref, heads, o_ref):
    tq = q_ref.shape[0]
    lane = lax.broadcasted_iota(jnp.int32, (tq, LANES), 1)
    low = lane < HEAD_DIM
    for j in range(q_ref.shape[1] // LANES):
        q = q_ref[:, j * LANES:(j + 1) * LANES]
        pair = None
        for half in range(2):
            k_loc, v_loc, k_ctx, v_ctx, bias, sink = heads(j, half)
            qm = jnp.where(low if half == 0 else ~low, q, jnp.zeros_like(q))
            s_loc = _dot_nt(qm, k_loc) + bias
            s_ctx = _dot_nt(qm, k_ctx)
            m = jnp.maximum(jnp.max(s_loc, axis=1, keepdims=True), jnp.max(s_ctx, axis=1, keepdims=True))
            if sink is not None:
                m = jnp.maximum(m, sink)
            p_loc = jnp.exp(s_loc - m)
            p_ctx = jnp.exp(s_ctx - m)
            den = jnp.sum(p_loc, axis=1, keepdims=True) + jnp.sum(p_ctx, axis=1, keepdims=True)
            if sink is not None:
                den = den + jnp.exp(sink - m)
            o = _dot(p_loc.astype(v_loc.dtype), v_loc) + _dot(p_ctx.astype(v_ctx.dtype), v_ctx)
            o = o / den
            pair = o if half == 0 else jnp.where(low, pair, o)
        o_ref[:, j * LANES:(j + 1) * LANES] = pair.astype(o_ref.dtype)


def _win_kernel(sink_ref, q_ref, k_ref, v_ref, kc_ref, vc_ref, o_ref, *, seq):
    i = pl.program_id(1)
    span = BLOCK_A + 2 * WINDOW
    start = pl.multiple_of(jnp.clip(i * BLOCK_A - WINDOW, 0, seq - span), BLOCK_A)
    k_loc = k_ref[0, pl.ds(start, span), :]
    v_loc = v_ref[0, pl.ds(start, span), :]
    k_ctx = kc_ref[0]
    v_ctx = vc_ref[0]
    qpos = i * BLOCK_A + lax.broadcasted_iota(jnp.int32, (BLOCK_A, span), 0)
    kpos = start + lax.broadcasted_iota(jnp.int32, (BLOCK_A, span), 1)
    bias = jnp.where(jnp.abs(kpos - qpos) <= WINDOW, 0.0, NEG_INF).astype(jnp.float32)

    def heads(j, half):
        return k_loc, v_loc, k_ctx, v_ctx, bias, sink_ref[half * GQA_GROUP + j]

    _pair_attention(q_ref, heads, o_ref)


def _window_attention(qa, ka, va, kac, vac, sink, *, batch, seq):
    nb = seq // BLOCK_A
    n_ctx = kac.shape[1]
    return pl.pallas_call(
        functools.partial(_win_kernel, seq=seq),
        grid=(batch, nb),
        in_specs=[pl.BlockSpec(memory_space=pltpu.SMEM),
                  pl.BlockSpec((BLOCK_A, WIDTH_A), lambda b, i: (b * nb + i, 0)),
                  pl.BlockSpec((1, seq, WIDTH_KV_A), lambda b, i: (b, 0, 0)),
                  pl.BlockSpec((1, seq, WIDTH_KV_A), lambda b, i: (b, 0, 0)),
                  pl.BlockSpec((1, n_ctx, WIDTH_KV_A), lambda b, i: (b, 0, 0)),
                  pl.BlockSpec((1, n_ctx, WIDTH_KV_A), lambda b, i: (b, 0, 0))],
        out_specs=pl.BlockSpec((BLOCK_A, WIDTH_A), lambda b, i: (b * nb + i, 0)),
        out_shape=jax.ShapeDtypeStruct((batch * seq, WIDTH_A), jnp.bfloat16),
        compiler_params=_params("parallel", "arbitrary"),
        name="window_attention",
    )(sink, qa, ka, va, kac, vac)


def _nb_kernel(q_ref, k_ref, v_ref, kc_ref, vc_ref, tab_ref, o_ref, *, rows_n):
    m = pl.program_id(1)
    n_keys = NB_K_ROWS * GRID_W
    start_row = jnp.clip(NB_Q_ROWS * m - NA_ROWS // 2, 0, rows_n - NB_K_ROWS)
    start = pl.multiple_of(start_row * GRID_W, LANES)

    def heads(j, half):
        cols = slice(j * LANES, (j + 1) * LANES)
        return (k_ref[0, pl.ds(start, n_keys), cols], v_ref[0, pl.ds(start, n_keys), cols],
                kc_ref[0, :, cols], vc_ref[0, :, cols],
                tab_ref[0, 2 * j + half].astype(jnp.float32), None)

    _pair_attention(q_ref, heads, o_ref)


def _nb_classes(rows_n):
    n_blocks = rows_n // NB_Q_ROWS
    sig = {}
    cls_of_block = []
    reps = []
    for m in range(n_blocks):
        start_row = int(np.clip(NB_Q_ROWS * m - NA_ROWS // 2, 0, rows_n - NB_K_ROWS))
        key = tuple((start_row - r, int(np.clip(r - NA_ROWS // 2, 0, rows_n - NA_ROWS)) - r)
                    for r in range(NB_Q_ROWS * m, NB_Q_ROWS * (m + 1)))
        if key not in sig:
            sig[key] = len(reps)
            reps.append(m)
        cls_of_block.append(sig[key])
    return np.asarray(cls_of_block, np.int32), reps


def _nb_bias_tables(rpb, rows_n):
    cls_of_block, reps = _nb_classes(rows_n)
    n_heads = rpb.shape[0]
    cq = np.arange(GRID_W)[:, None]
    ck = np.arange(GRID_W)[None, :]
    cs = np.clip(cq - NA_COLS // 2, 0, GRID_W - NA_COLS)
    col_ok = (ck >= cs) & (ck < cs + NA_COLS)
    pick = ((ck - cq + NA_COLS - 1)[None] == np.arange(2 * NA_COLS - 1)[:, None, None]) & col_ok[None]
    tiles = jnp.einsum('hrd,dqk->hrqk', rpb.astype(jnp.float32), jnp.asarray(pick, jnp.float32),
                       precision=lax.Precision.HIGHEST)
    tiles = jnp.where(jnp.asarray(col_ok)[None, None], tiles, NEG_INF).astype(jnp.bfloat16)
    blocked = jnp.full((n_heads, GRID_W, GRID_W), NEG_INF, jnp.bfloat16)
    tabs = []
    for m in reps:
        start_row = int(np.clip(NB_Q_ROWS * m - NA_ROWS // 2, 0, rows_n - NB_K_ROWS))
        q_rows = []
        for r in range(NB_Q_ROWS * m, NB_Q_ROWS * (m + 1)):
            rs = int(np.clip(r - NA_ROWS // 2, 0, rows_n - NA_ROWS))
            q_rows.append(jnp.concatenate(
                [tiles[:, krow - r + NA_ROWS - 1] if rs <= krow < rs + NA_ROWS else blocked
                 for krow in range(start_row, start_row + NB_K_ROWS)], axis=2))
        tabs.append(jnp.concatenate(q_rows, axis=1))
    return jnp.stack(tabs), cls_of_block


def _neighbourhood_attention(qb, kb, vb, kbc, vbc, tabs, cls_of_block, *, batch, seq):
    rows_n = seq // GRID_W
    nblk = rows_n // NB_Q_ROWS
    tq = NB_Q_ROWS * GRID_W
    n_ctx = kbc.shape[1]
    grid_spec = pltpu.PrefetchScalarGridSpec(
        num_scalar_prefetch=1,
        grid=(batch, nblk),
        in_specs=[pl.BlockSpec((tq, WIDTH_B), lambda b, m, c: (b * nblk + m, 0)),
                  pl.BlockSpec((1, seq, WIDTH_B), lambda b, m, c: (b, 0, 0)),
                  pl.BlockSpec((1, seq, WIDTH_B), lambda b, m, c: (b, 0, 0)),
                  pl.BlockSpec((1, n_ctx, WIDTH_B), lambda b, m, c: (b, 0, 0)),
                  pl.BlockSpec((1, n_ctx, WIDTH_B), lambda b, m, c: (b, 0, 0)),
                  pl.BlockSpec((1,) + tabs.shape[1:], lambda b, m, c: (c[m], 0, 0, 0))],
        out_specs=pl.BlockSpec((tq, WIDTH_B), lambda b, m, c: (b * nblk + m, 0)),
    )

    def body(c_ref, *refs):
        _nb_kernel(*refs, rows_n=rows_n)

    return pl.pallas_call(
        body,
        grid_spec=grid_spec,
        out_shape=jax.ShapeDtypeStruct((batch * seq, WIDTH_B), jnp.bfloat16),
        compiler_params=_params("parallel", "arbitrary"),
        name="neighbourhood_attention",
    )(jnp.asarray(cls_of_block), qb, kb, vb, kbc, vbc, tabs)


def _outproj_kernel(oa_ref, ob_ref, x_ref, ga_ref, sh_ref, sc_ref, gga_ref, ggb_ref, gpost_ref, gpre_ref,
                    wo_ref, wr_ref, br_ref,
                    x1_ref, h2_ref, idx_ref, wts_ref, rank_ref, size_ref, base_ref, cnt_ref, run_ref):
    i = pl.program_id(0)

    @pl.when(i == 0)
    def _():
        run_ref[...] = jnp.zeros_like(run_ref)

    na = _rms(oa_ref[...].astype(jnp.float32), gga_ref[...]).astype(jnp.bfloat16)
    nb = _rms(ob_ref[...].astype(jnp.float32), ggb_ref[...]).astype(jnp.bfloat16)
    mix = _dot(na, wo_ref[:WIDTH_A, :]) + _dot(nb, wo_ref[WIDTH_A:, :])
    x1 = x_ref[...] + ga_ref[0] * _rms(mix, gpost_ref[...])
    x1_ref[...] = x1
    h2 = _rms(x1, gpre_ref[...]) * (1.0 + sc_ref[0]) + sh_ref[0]
    h2_ref[...] = h2.astype(h2_ref.dtype)

    logits = jnp.dot(h2, wr_ref[...], preferred_element_type=jnp.float32,
                     precision=lax.Precision.HIGHEST) + br_ref[...]
    t, e = logits.shape
    lane = lax.broadcasted_iota(jnp.int32, (t, e), 1)
    work = logits
    chosen = jnp.zeros((t, e), jnp.float32)
    vals, sels, hots = [], [], []
    for _k in range(TOP_K):
        mx = jnp.max(work, axis=1, keepdims=True)
        sel = jnp.min(jnp.where(work == mx, lane, e), axis=1, keepdims=True)
        hot = lane == sel
        vals.append(mx)
        sels.append(sel)
        hots.append(hot)
        work = jnp.where(hot, -jnp.inf, work)
        chosen = chosen + hot.astype(jnp.float32)
    ex = [jnp.exp(v - vals[0]) for v in vals]
    den = ex[0] + ex[1] + ex[2] + ex[3]

    r_i = lax.broadcasted_iota(jnp.int32, (t, t), 0)
    c_i = lax.broadcasted_iota(jnp.int32, (t, t), 1)
    before = (c_i < r_i).astype(jnp.bfloat16)
    prefix = _dot(before, chosen.astype(jnp.bfloat16))
    ranks = [jnp.sum(jnp.where(h, prefix, 0.0), axis=1, keepdims=True) for h in hots]
    size = jnp.sum(chosen, axis=0, keepdims=True).astype(jnp.int32)
    size = (size + (ROW_ALIGN - 1)) // ROW_ALIGN * ROW_ALIGN
    size_ref[0] = size
    base_ref[0] = run_ref[...]
    run_ref[...] = run_ref[...] + size
    cnt_ref[...] = run_ref[...]

    l4 = lax.broadcasted_iota(jnp.int32, (t, TOP_K), 1)

    def cols(parts):
        out = parts[TOP_K - 1]
        for k in range(TOP_K - 2, -1, -1):
            out = jnp.where(l4 == k, parts[k], out)
        return out

    idx_ref[...] = cols(sels)
    wts_ref[...] = cols([v / den for v in ex])
    rank_ref[...] = cols(ranks).astype(jnp.int32)


def _outproj(oa, ob, x2d, mod3, gga, ggb, gpost, gpre, w_out, w_router, b_router, *, seq):
    t, d = x2d.shape
    tile = OUT_TILE
    per_seq = seq // tile
    row = lambda i: (i, 0)
    const = lambda i: (0, 0)
    modspec = lambda part: pl.BlockSpec((1, 1, d), lambda i: (i // per_seq, 0, part))
    outs = pl.pallas_call(
        _outproj_kernel,
        grid=(t // tile,),
        in_specs=[pl.BlockSpec((tile, WIDTH_A), row), pl.BlockSpec((tile, WIDTH_B), row),
                  pl.BlockSpec((tile, d), row),
                  modspec(2), modspec(3), modspec(4),
                  pl.BlockSpec((1, WIDTH_A), const), pl.BlockSpec((1, WIDTH_B), const),
                  pl.BlockSpec((1, d), const), pl.BlockSpec((1, d), const),
                  pl.BlockSpec(w_out.shape, const), pl.BlockSpec(w_router.shape, const),
                  pl.BlockSpec((1, N_EXPERTS), const)],
        out_specs=[pl.BlockSpec((tile, d), row), pl.BlockSpec((tile, d), row),
                   pl.BlockSpec((tile, TOP_K), row), pl.BlockSpec((tile, TOP_K), row),
                   pl.BlockSpec((tile, TOP_K), row),
                   pl.BlockSpec((1, 1, N_EXPERTS), lambda i: (i, 0, 0)),
                   pl.BlockSpec((1, 1, N_EXPERTS), lambda i: (i, 0, 0)),
                   pl.BlockSpec((1, N_EXPERTS), const)],
        out_shape=[jax.ShapeDtypeStruct((t, d), jnp.float32), jax.ShapeDtypeStruct((t, d), jnp.bfloat16),
                   jax.ShapeDtypeStruct((t, TOP_K), jnp.int32), jax.ShapeDtypeStruct((t, TOP_K), jnp.float32),
                   jax.ShapeDtypeStruct((t, TOP_K), jnp.int32),
                   jax.ShapeDtypeStruct((t // tile, 1, N_EXPERTS), jnp.int32),
                   jax.ShapeDtypeStruct((t // tile, 1, N_EXPERTS), jnp.int32),
                   jax.ShapeDtypeStruct((1, N_EXPERTS), jnp.int32)],
        scratch_shapes=[pltpu.VMEM((1, N_EXPERTS), jnp.int32)],
        compiler_params=_params("arbitrary"),
        name="outproj_router",
    )(oa, ob, x2d, mod3, mod3, mod3, gga, ggb, gpost, gpre, w_out, w_router, b_router)
    return outs


def _piece_table(tab_ref, e):
    return (pl.multiple_of(tab_ref[e], ROW_ALIGN), pl.multiple_of(tab_ref[N_EXPERTS + e], ROW_ALIGN),
            pl.multiple_of(tab_ref[2 * N_EXPERTS + e], ROW_ALIGN))


def _dispatch_kernel(fill_from_ref, pad_end_ref, tab_ref, h_ref, lpos_ref, w_ref, xs_ref,
                     stage_ref, zero_ref, pend_ref, sems, zsem):
    i = pl.program_id(0)
    n_steps = pl.num_programs(0)
    slot = i % 2
    n_local, width = stage_ref.shape[1:]
    d = h_ref.shape[1]

    def fill(row):
        return pltpu.make_async_copy(zero_ref, xs_ref.at[pl.ds(pl.multiple_of(row, FFN_TILE), FFN_TILE)], zsem)

    def fills(e):
        return (pad_end_ref[e] - fill_from_ref[e]) // FFN_TILE

    def drain(s):
        n = pl.multiple_of(pend_ref[s], ROW_ALIGN)

        @pl.when(n > 0)
        def _():
            pltpu.make_async_copy(stage_ref.at[s, pl.ds(0, n)], xs_ref.at[pl.ds(0, n)], sems.at[s]).wait()
        pend_ref[s] = 0

    @pl.when(i == 0)
    def _():
        zero_ref[...] = jnp.zeros_like(zero_ref)
        pend_ref[0] = 0
        pend_ref[1] = 0

        def start_e(e, c):
            def start_j(j, c2):
                fill(fill_from_ref[e] + j * FFN_TILE).start()
                return c2
            lax.fori_loop(0, fills(e), start_j, 0)
            return c + fills(e)
        n_fill = lax.fori_loop(0, N_EXPERTS, start_e, 0)

        def wait_all(j, c):
            fill(0).wait()
            return c
        lax.fori_loop(0, n_fill, wait_all, 0)

    row = lax.broadcasted_iota(jnp.int32, (n_local, h_ref.shape[0]), 0)
    place = None
    row_w = None
    for k in range(TOP_K):
        hit = row == lpos_ref[0, k:k + 1, :]
        place = hit if place is None else place | hit
        wk = jnp.sum(jnp.where(hit, w_ref[0, k:k + 1, :], 0.0), axis=1, keepdims=True)
        row_w = wk if row_w is None else row_w + wk
    rows = _dot(place.astype(jnp.bfloat16), h_ref[...])

    drain(slot)
    stage_ref[slot, :, :d] = rows
    stage_ref[slot, :, d:] = jnp.broadcast_to(row_w, (n_local, width - d))

    def piece(e, total):
        loc, glob, n = _piece_table(tab_ref, e)

        @pl.when(n > 0)
        def _():
            pltpu.make_async_copy(stage_ref.at[slot, pl.ds(loc, n)], xs_ref.at[pl.ds(glob, n)], sems.at[slot]).start()
        return total + n
    pend_ref[slot] = lax.fori_loop(0, N_EXPERTS, piece, 0)

    tail0 = pad_end_ref[N_EXPERTS - 1]
    n_tail = (xs_ref.shape[0] - tail0) // FFN_TILE
    per_step = (n_tail + n_steps - 1) // n_steps

    def start_tail(j, c):
        t = i * per_step + j

        @pl.when(t < n_tail)
        def _():
            fill(tail0 + t * FFN_TILE).start()
        return c
    lax.fori_loop(0, per_step, start_tail, 0)

    @pl.when(i == n_steps - 1)
    def _():
        drain(0)
        drain(1)

        def wait_tail(j, c):
            fill(0).wait()
            return c
        lax.fori_loop(0, n_tail, wait_tail, 0)


def _dispatch(h2, tab_flat, lpos_t, w_t, fill_from, pad_end, *, n_rows):
    t, d = h2.shape
    tile = DISPATCH_TILE
    n_local = tile * TOP_K + N_EXPERTS * ROW_ALIGN
    width = d + LANES
    grid_spec = pltpu.PrefetchScalarGridSpec(
        num_scalar_prefetch=2,
        grid=(t // tile,),
        in_specs=[pl.BlockSpec((TAB_WIDTH,), lambda i, *_: (i,), memory_space=pltpu.SMEM),
                  pl.BlockSpec((tile, d), lambda i, *_: (i, 0)),
                  pl.BlockSpec((1, TOP_K, tile), lambda i, *_: (i, 0, 0)),
                  pl.BlockSpec((1, TOP_K, tile), lambda i, *_: (i, 0, 0))],
        out_specs=pl.BlockSpec(memory_space=pl.ANY),
        scratch_shapes=[pltpu.VMEM((2, n_local, width), jnp.float32),
                        pltpu.VMEM((FFN_TILE, width), jnp.float32),
                        pltpu.SMEM((2,), jnp.int32),
                        pltpu.SemaphoreType.DMA((2,)), pltpu.SemaphoreType.DMA(())],
    )
    return pl.pallas_call(
        _dispatch_kernel,
        grid_spec=grid_spec,
        out_shape=jax.ShapeDtypeStruct((n_rows, width), jnp.float32),
        compiler_params=_params("arbitrary"),
        name="dispatch",
    )(fill_from, pad_end, tab_flat, h2, lpos_t, w_t)


def _ffn_kernel(te_ref, rows_ref, n_used_ref, x_ref, w1_ref, b1_ref, w2_ref, b2_ref, y_ref,
                w1p_ref, w2b_ref, act_ref):
    i = pl.program_id(0)
    live = i < n_used_ref[0]
    n_pair = w1_ref.shape[2] // FFN_CHUNK
    half = FFN_CHUNK // 2
    d = y_ref.shape[1]
    n_sub = jnp.where(live, (rows_ref[i] + FFN_TILE - 1) // FFN_TILE, 0)

    @pl.when(live & ((i == 0) | (te_ref[i] != te_ref[jnp.maximum(i - 1, 0)])))
    def _():
        src = lax.broadcasted_iota(jnp.int32, (FFN_CHUNK, FFN_CHUNK), 0)
        dst = lax.broadcasted_iota(jnp.int32, (FFN_CHUNK, FFN_CHUNK), 1)
        unzip = (src == jnp.where(dst < half, 2 * dst, 2 * (dst - half) + 1)).astype(jnp.bfloat16)
        for c in range(n_pair):
            cs = slice(c * FFN_CHUNK, (c + 1) * FFN_CHUNK)
            w1p_ref[:, cs] = _dot(w1_ref[0, :, cs].astype(jnp.bfloat16), unzip).astype(jnp.bfloat16)
        w2b_ref[...] = w2_ref[0].astype(jnp.bfloat16)

    def tile(s, carry):
        rows = pl.ds(pl.multiple_of(s * FFN_TILE, FFN_TILE), FFN_TILE)
        x = x_ref[rows, :d].astype(jnp.bfloat16)
        for c in range(0, n_pair, 2):
            cs = slice(c * FFN_CHUNK, (c + 2) * FFN_CHUNK)
            h = _dot(x, w1p_ref[:, cs]) + b1_ref[0, :, cs]
            g = jnp.concatenate([h[:, :half], h[:, FFN_CHUNK:FFN_CHUNK + half]], axis=1)
            u = jnp.concatenate([h[:, half:FFN_CHUNK], h[:, FFN_CHUNK + half:]], axis=1)
            g = jnp.minimum(g, SWIGLU_LIMIT)
            u = jnp.clip(u, -SWIGLU_LIMIT, SWIGLU_LIMIT)
            act = g * jax.nn.sigmoid(SWIGLU_ALPHA * g) * (u + 1.0)
            act_ref[:, c * half:(c + 2) * half] = act.astype(act_ref.dtype)
        y_ref[rows, :] = (_dot(act_ref[...], w2b_ref[...]) + b2_ref[0]) * x_ref[rows, d:d + 1]
        return carry
    lax.fori_loop(0, n_sub, tile, 0)

    def blank(s, carry):
        y_ref[pl.ds(pl.multiple_of(s * FFN_TILE, FFN_TILE), FFN_TILE), :] = jnp.zeros((FFN_TILE, d), y_ref.dtype)
        return carry
    lax.fori_loop(n_sub, y_ref.shape[0] // FFN_TILE, blank, 0)


def _unzip_bias(b1):
    e, f2 = b1.shape
    half = FFN_CHUNK // 2
    return b1.reshape(e, f2 // FFN_CHUNK, half, 2).transpose(0, 1, 3, 2).reshape(e, 1, f2)


def _ffn(xs, step_expert, step_rows, n_used, w1, b1, w2, b2):
    rows, width = xs.shape
    d, f2 = w1.shape[1:]
    n_steps = rows // FFN_STEP
    live = lambda i, nu: jnp.minimum(i, nu[0] - 1)
    wspec = lambda shape: pl.BlockSpec((1,) + shape, lambda i, te, nr, nu: (te[live(i, nu)], 0, 0))
    grid_spec = pltpu.PrefetchScalarGridSpec(
        num_scalar_prefetch=3,
        grid=(n_steps,),
        in_specs=[pl.BlockSpec((FFN_STEP, width), lambda i, te, nr, nu: (live(i, nu), 0)),
                  wspec((d, f2)), wspec((1, f2)), wspec((f2 // 2, d)), wspec((1, d))],
        out_specs=pl.BlockSpec((FFN_STEP, d), lambda i, te, nr, nu: (i, 0)),
        scratch_shapes=[pltpu.VMEM((d, f2), jnp.bfloat16), pltpu.VMEM((f2 // 2, d), jnp.bfloat16),
                        pltpu.VMEM((FFN_TILE, f2 // 2), jnp.bfloat16)],
    )
    return pl.pallas_call(
        _ffn_kernel,
        grid_spec=grid_spec,
        out_shape=jax.ShapeDtypeStruct((rows, d), jnp.float32),
        compiler_params=_params("arbitrary"),
        name="expert_ffn",
    )(step_expert, step_rows, n_used, xs, w1, _unzip_bias(b1), w2, b2[:, None, :])


def _combine_kernel(tab_ref, next_tab_ref, lpos_ref, x1_ref, ga_ref, g_ref, y_ref, o_ref, buf_ref, sems):
    i = pl.program_id(0)
    n_steps = pl.num_programs(0)
    slot = i % 2
    tile = x1_ref.shape[0]
    n_local = buf_ref.shape[1]

    def fetch(tab, s):
        def piece(e, c):
            loc, glob, n = _piece_table(tab, e)

            @pl.when(n > 0)
            def _():
                pltpu.make_async_copy(y_ref.at[pl.ds(glob, n)], buf_ref.at[s, pl.ds(loc, n)], sems.at[s]).start()
            return c
        lax.fori_loop(0, N_EXPERTS, piece, 0)

    @pl.when(i == 0)
    def _():
        buf_ref[...] = jnp.zeros_like(buf_ref)
        fetch(tab_ref, 0)

    @pl.when(i + 1 < n_steps)
    def _():
        fetch(next_tab_ref, 1 - slot)

    n = pl.multiple_of(tab_ref[3 * N_EXPERTS], ROW_ALIGN)

    @pl.when(n > 0)
    def _():
        pltpu.make_async_copy(y_ref.at[pl.ds(0, n)], buf_ref.at[slot, pl.ds(0, n)], sems.at[slot]).wait()

    col = lax.broadcasted_iota(jnp.int32, (tile, n_local), 1)
    pos = lpos_ref[...]
    pick = col == pos[:, 0:1]
    for k in range(1, TOP_K):
        pick = pick | (col == pos[:, k:k + 1])
    pick = pick.astype(jnp.bfloat16)
    rows = buf_ref[slot]
    hi = rows.astype(jnp.bfloat16)
    lo = (rows - hi.astype(jnp.float32)).astype(jnp.bfloat16)
    y = _dot(pick, hi) + _dot(pick, lo)
    o_ref[...] = x1_ref[...] + ga_ref[0] * _rms(y, g_ref[...])


def _combine(ys, tab_flat, lpos, x1, mod3, g_post, *, seq):
    t, d = x1.shape
    tile = DISPATCH_TILE
    n_steps = t // tile
    per_seq = seq // tile
    n_local = tile * TOP_K + N_EXPERTS * ROW_ALIGN
    return pl.pallas_call(
        _combine_kernel,
        grid=(n_steps,),
        in_specs=[pl.BlockSpec((TAB_WIDTH,), lambda i: (i,), memory_space=pltpu.SMEM),
                  pl.BlockSpec((TAB_WIDTH,), lambda i: (jnp.minimum(i + 1, n_steps - 1),), memory_space=pltpu.SMEM),
                  pl.BlockSpec((tile, TOP_K), lambda i: (i, 0)),
                  pl.BlockSpec((tile, d), lambda i: (i, 0)),
                  pl.BlockSpec((1, 1, d), lambda i: (i // per_seq, 0, 5)),
                  pl.BlockSpec((1, d), lambda i: (0, 0)),
                  pl.BlockSpec(memory_space=pl.ANY)],
        out_specs=pl.BlockSpec((tile, d), lambda i: (i, 0)),
        out_shape=jax.ShapeDtypeStruct((t, d), jnp.float32),
        scratch_shapes=[pltpu.VMEM((2, n_local, d), ys.dtype), pltpu.SemaphoreType.DMA((2,))],
        compiler_params=_params("arbitrary"),
        name="combine",
    )(tab_flat, tab_flat, lpos, x1, mod3, g_post, ys)


def _rope_tables(seq):
    pos = np.arange(seq)
    n_freq = HEAD_DIM // 4
    freqs = ROPE_BASE ** (-jnp.arange(n_freq, dtype=jnp.float32) / n_freq)
    rows = jnp.asarray(pos // GRID_W, jnp.float32)[:, None] * freqs[None, :]
    cols = jnp.asarray(pos % GRID_W, jnp.float32)[:, None] * freqs[None, :]
    ang = jnp.concatenate([rows, rows, cols, cols], axis=1)
    sign = np.tile(np.repeat([-1.0, 1.0], n_freq), 2).astype(np.float32)
    cos = jnp.cos(ang)
    sin = jnp.sin(ang) * sign[None, :]
    reps = LANES // HEAD_DIM
    return jnp.tile(cos, (1, reps)), jnp.tile(sin, (1, reps))


def kernel(x, c, ctx, c_ctx, w_ada, b_ada, g_pre_mix, g_post_mix, g_pre_ffn, g_post_ffn, w_in, g_grp_a, g_grp_b,
           sink_a, rpb_b, w_out, w_router, b_router, w_mlp1, b_mlp1, w_mlp2, b_mlp2):
    batch, seq, d = x.shape
    n_ctx = ctx.shape[1]
    assert w_ada.shape[0] == 1, "single layer"
    assert seq % (GRID_W * NB_Q_ROWS) == 0 and seq // GRID_W >= NB_K_ROWS and seq >= BLOCK_A + 2 * WINDOW
    n_tok = batch * seq
    bf16 = jnp.bfloat16

    head_order = np.asarray([k * GQA_GROUP + j for j in range(GQA_GROUP) for k in range(N_KV_A)])
    col_order = (head_order[:, None] * HEAD_DIM + np.arange(HEAD_DIM)[None, :]).reshape(-1)

    mod_rows = -(-(batch + 1) // 8) * 8
    cc = jnp.zeros((mod_rows, d), jnp.float32).at[:batch].set(c).at[batch].set(c_ctx)
    mod3 = _ada(cc, w_ada[0], b_ada[0]).reshape(mod_rows, 1, 6 * d)

    w_in0 = w_in[0]
    w_lat = jnp.concatenate([w_in0[:, :WIDTH_A][:, col_order], w_in0[:, WIDTH_A:]], axis=1).astype(bf16)
    kv_cols = np.concatenate([np.arange(WIDTH_A, WIDTH_A + 2 * WIDTH_KV_A),
                              np.arange(WIDTH_A + 2 * WIDTH_KV_A + WIDTH_B, w_in0.shape[1])])
    w_ctx = w_in0[:, kv_cols].astype(bf16)
    g_pre = g_pre_mix[0].reshape(1, d)
    qa, ka, va, qb, kb, vb = _inproj(x.reshape(n_tok, d), mod3, g_pre, w_lat, _rope_tables(seq),
                                     seq=seq, mod_row0=0, latent=True)
    kac, vac, kbc, vbc = _inproj(ctx.reshape(batch * n_ctx, d), mod3, g_pre, w_ctx, None,
                                 seq=n_ctx, mod_row0=batch, latent=False)

    oa = _window_attention(qa, ka.reshape(batch, seq, -1), va.reshape(batch, seq, -1),
                           kac.reshape(batch, n_ctx, -1), vac.reshape(batch, n_ctx, -1),
                           sink_a[0].astype(jnp.float32), batch=batch, seq=seq)
    tabs, cls_of_block = _nb_bias_tables(rpb_b[0], seq // GRID_W)
    ob = _neighbourhood_attention(qb, kb.reshape(batch, seq, -1), vb.reshape(batch, seq, -1),
                                  kbc.reshape(batch, n_ctx, -1), vbc.reshape(batch, n_ctx, -1),
                                  tabs, cls_of_block, batch=batch, seq=seq)

    w_out0 = w_out[0]
    w_o = jnp.concatenate([w_out0[:WIDTH_A][col_order], w_out0[WIDTH_A:]], axis=0).astype(bf16)
    x1, h2, idx, wts, rank, size, base, counts = _outproj(
        oa, ob, x.reshape(n_tok, d), mod3,
        g_grp_a[0][col_order].reshape(1, -1), g_grp_b[0].reshape(1, -1),
        g_post_mix[0].reshape(1, d), g_pre_ffn[0].reshape(1, d),
        w_o, w_router[0], b_router[0].reshape(1, -1), seq=seq)

    n_tok_tiles = n_tok // DISPATCH_TILE
    size = size.reshape(n_tok_tiles, N_EXPERTS)
    counts = counts.reshape(-1)
    padded = (counts + FFN_STEP - 1) // FFN_STEP * FFN_STEP
    pad_end = jnp.cumsum(padded).astype(jnp.int32)
    pad_start = pad_end - padded
    n_steps = (n_tok * TOP_K + n_tok_tiles * N_EXPERTS * (ROW_ALIGN - 1)) // FFN_STEP + N_EXPERTS
    n_used = (pad_end[-1:] // FFN_STEP).astype(jnp.int32)
    step_row0 = jnp.arange(n_steps, dtype=jnp.int32) * FFN_STEP
    step_expert = jnp.minimum(jnp.sum(step_row0[:, None] >= pad_end[None, :], axis=1),
                              N_EXPERTS - 1).astype(jnp.int32)
    own = step_expert[:, None] == jnp.arange(N_EXPERTS, dtype=jnp.int32)
    step_rows = jnp.clip(jnp.sum(jnp.where(own, (pad_start + counts)[None, :], 0), axis=1) - step_row0,
                         0, FFN_STEP).astype(jnp.int32)
    fill_from = (pad_start + jnp.maximum((counts + FFN_TILE - 1) // FFN_TILE * FFN_TILE - FFN_TILE, 0)
                 ).astype(jnp.int32)
    local0 = jnp.cumsum(size, axis=1) - size
    global0 = pad_start[None, :] + base.reshape(n_tok_tiles, N_EXPERTS)
    tab = jnp.concatenate([local0, global0, size, jnp.sum(size, axis=1, keepdims=True),
                           jnp.zeros((n_tok_tiles, TAB_WIDTH - 3 * N_EXPERTS - 1), jnp.int32)], axis=1)
    tab = tab.reshape(-1).astype(jnp.int32)
    hot = idx.reshape(n_tok_tiles, DISPATCH_TILE, TOP_K, 1) == jnp.arange(N_EXPERTS, dtype=jnp.int32)
    lpos = rank.reshape(n_tok_tiles, DISPATCH_TILE, TOP_K) + jnp.sum(
        jnp.where(hot, local0[:, None, None, :], 0), axis=-1)
    lpos_t = jnp.swapaxes(lpos, 1, 2)
    w_t = jnp.swapaxes(wts.reshape(n_tok_tiles, DISPATCH_TILE, TOP_K), 1, 2)

    xs = _dispatch(h2, tab, lpos_t, w_t, fill_from, pad_end, n_rows=n_steps * FFN_STEP)
    ys = _ffn(xs, step_expert, step_rows, n_used, w_mlp1[0], b_mlp1[0], w_mlp2[0], b_mlp2[0])
    out = _combine(ys, tab, lpos.reshape(n_tok, TOP_K), x1, mod3, g_post_ffn[0].reshape(1, d), seq=seq)
    return out.reshape(batch, seq, d)
```

```python
import functools

import numpy as np
import jax
import jax.numpy as jnp
from jax import lax
from jax.experimental import pallas as pl
from jax.experimental.pallas import tpu as pltpu

GRID_W = 64
HEAD_DIM = 64
N_HEADS_A = 8
N_KV_A = 2
GQA_GROUP = N_HEADS_A // N_KV_A
N_HEADS_B = 8
WIDTH_A = N_HEADS_A * HEAD_DIM
WIDTH_KV_A = N_KV_A * HEAD_DIM
WIDTH_B = N_HEADS_B * HEAD_DIM
WINDOW = 128
BLOCK_A = 128
NA_ROWS = 8
NA_COLS = 16
N_EXPERTS = 32
TOP_K = 4
SWIGLU_LIMIT = 7.0
SWIGLU_ALPHA = 1.702
ROPE_BASE = 10000.0
EPS = 1e-6
NEG_INF = -1e30

LANES = 128
VMEM_LIMIT = 56 * 1024 * 1024

NB_Q_ROWS = 2
NB_K_ROWS = NB_Q_ROWS + NA_ROWS
PROJ_TILE = 512
OUT_TILE = 256
FFN_TILE = 512
FFN_STEP = 2 * FFN_TILE
FFN_CHUNK = 256
DISPATCH_TILE = OUT_TILE
ROW_ALIGN = 8
TAB_WIDTH = 128


def _params(*sem):
    return pltpu.CompilerParams(dimension_semantics=sem, vmem_limit_bytes=VMEM_LIMIT)


def _rms(x, g):
    return x * lax.rsqrt(jnp.mean(x * x, axis=-1, keepdims=True) + EPS) * g


def _dot(a, b):
    return jnp.dot(a, b, preferred_element_type=jnp.float32)


def _dot_nt(a, b):
    return lax.dot_general(a, b, (((1,), (1,)), ((), ())), preferred_element_type=jnp.float32)


def _ada_kernel(c_ref, w_ref, b_ref, o_ref):
    c = c_ref[...]
    s = (c * jax.nn.sigmoid(c)).astype(jnp.bfloat16)
    o_ref[...] = _dot(s, w_ref[...].astype(jnp.bfloat16)) + b_ref[...]


def _ada(cc, w_ada, b_ada):
    rows, d = cc.shape
    n_out = w_ada.shape[1]
    return pl.pallas_call(
        _ada_kernel,
        grid=(n_out // d,),
        in_specs=[pl.BlockSpec((rows, d), lambda j: (0, 0)),
                  pl.BlockSpec((d, d), lambda j: (0, j)),
                  pl.BlockSpec((1, d), lambda j: (0, j))],
        out_specs=pl.BlockSpec((rows, d), lambda j: (0, j)),
        out_shape=jax.ShapeDtypeStruct((rows, n_out), jnp.float32),
        compiler_params=_params("arbitrary"),
        name="ada",
    )(cc, w_ada, b_ada.reshape(1, n_out))


def _rope(x, cos, sin):
    w = x.shape[1]
    reps = w // LANES
    if reps > 1:
        cos = jnp.concatenate([cos] * reps, axis=1)
        sin = jnp.concatenate([sin] * reps, axis=1)
    lane = lax.broadcasted_iota(jnp.int32, x.shape, 1)
    quarter = HEAD_DIM // 4
    partner = jnp.where(lane % (2 * quarter) < quarter,
                        pltpu.roll(x, w - quarter, 1), pltpu.roll(x, quarter, 1))
    return x * cos + partner * sin


def _inproj_kernel(x_ref, sh_ref, sc_ref, g_ref, w_ref, *rest, latent):
    x = x_ref[...]
    h = _rms(x, g_ref[...]) * (1.0 + sc_ref[0]) + sh_ref[0]
    p = _dot(h.astype(jnp.bfloat16), w_ref[...])
    if latent:
        cos_ref, sin_ref, qa_ref, ka_ref, va_ref, qb_ref, kb_ref, vb_ref = rest
        cos, sin = cos_ref[...], sin_ref[...]
        scale = HEAD_DIM ** -0.5
        o = 0
        qa_ref[...] = (_rope(p[:, o:o + WIDTH_A], cos, sin) * scale).astype(qa_ref.dtype)
        o += WIDTH_A
        ka_ref[...] = _rope(p[:, o:o + WIDTH_KV_A], cos, sin).astype(ka_ref.dtype)
        o += WIDTH_KV_A
        va_ref[...] = p[:, o:o + WIDTH_KV_A].astype(va_ref.dtype)
        o += WIDTH_KV_A
        qb_ref[...] = (p[:, o:o + WIDTH_B] * scale).astype(qb_ref.dtype)
        o += WIDTH_B
    else:
        ka_ref, va_ref, kb_ref, vb_ref = rest
        o = 0
        ka_ref[...] = p[:, o:o + WIDTH_KV_A].astype(ka_ref.dtype)
        o += WIDTH_KV_A
        va_ref[...] = p[:, o:o + WIDTH_KV_A].astype(va_ref.dtype)
        o += WIDTH_KV_A
    kb_ref[...] = p[:, o:o + WIDTH_B].astype(kb_ref.dtype)
    o += WIDTH_B
    vb_ref[...] = p[:, o:o + WIDTH_B].astype(vb_ref.dtype)


def _inproj(x2d, mod3, g_pre, w, rope_tabs, *, seq, mod_row0, latent):
    t, d = x2d.shape
    tile = min(PROJ_TILE, seq)
    per_seq = seq // tile
    if latent:
        mod_row = lambda i: i // per_seq
    else:
        mod_row = lambda i: mod_row0
    in_specs = [pl.BlockSpec((tile, d), lambda i: (i, 0)),
                pl.BlockSpec((1, 1, d), lambda i: (mod_row(i), 0, 0)),
                pl.BlockSpec((1, 1, d), lambda i: (mod_row(i), 0, 1)),
                pl.BlockSpec((1, d), lambda i: (0, 0)),
                pl.BlockSpec(w.shape, lambda i: (0, 0))]
    args = [x2d, mod3, mod3, g_pre, w]
    widths = [WIDTH_KV_A, WIDTH_KV_A, WIDTH_B, WIDTH_B]
    if latent:
        in_specs += [pl.BlockSpec((tile, LANES), lambda i: (i % per_seq, 0))] * 2
        args += list(rope_tabs)
        widths = [WIDTH_A, WIDTH_KV_A, WIDTH_KV_A, WIDTH_B, WIDTH_B, WIDTH_B]
    return pl.pallas_call(
        functools.partial(_inproj_kernel, latent=latent),
        grid=(t // tile,),
        in_specs=in_specs,
        out_specs=[pl.BlockSpec((tile, wd), lambda i: (i, 0)) for wd in widths],
        out_shape=[jax.ShapeDtypeStruct((t, wd), jnp.bfloat16) for wd in widths],
        compiler_params=_params("parallel"),
        name="inproj_latent" if latent else "inproj_ctx",
    )(*args)


def _pair_attention(q_ref, heads, o_ref):
    tq = q_ref.shape[0]
    lane = lax.broadcasted_iota(jnp.int32, (tq, LANES), 1)
    low = lane < HEAD_DIM
    for j in range(q_ref.shape[1] // LANES):
        q = q_ref[:, j * LANES:(j + 1) * LANES]
        pair = None
        for half in range(2):
            k_loc, v_loc, k_ctx, v_ctx, bias, sink = heads(j, half)
            qm = jnp.where(low if half == 0 else ~low, q, jnp.zeros_like(q))
            s_loc = _dot_nt(qm, k_loc) + bias
            s_ctx = _dot_nt(qm, k_ctx)
            m = jnp.maximum(jnp.max(s_loc, axis=1, keepdims=True), jnp.max(s_ctx, axis=1, keepdims=True))
            if sink is not None:
                m = jnp.maximum(m, sink)
            p_loc = jnp.exp(s_loc - m)
            p_ctx = jnp.exp(s_ctx - m)
            den = jnp.sum(p_loc, axis=1, keepdims=True) + jnp.sum(p_ctx, axis=1, keepdims=True)
            if sink is not None:
                den = den + jnp.exp(sink - m)
            o = _dot(p_loc.astype(v_loc.dtype), v_loc) + _dot(p_ctx.astype(v_ctx.dtype), v_ctx)
            o = o / den
            pair = o if half == 0 else jnp.where(low, pair, o)
        o_ref[:, j * LANES:(j + 1) * LANES] = pair.astype(o_ref.dtype)


def _win_kernel(sink_ref, q_ref, k_ref, v_ref, kc_ref, vc_ref, o_ref, *, seq):
    i = pl.program_id(1)
    span = BLOCK_A + 2 * WINDOW
    start = pl.multiple_of(jnp.clip(i * BLOCK_A - WINDOW, 0, seq - span), BLOCK_A)
    k_loc = k_ref[0, pl.ds(start, span), :]
    v_loc = v_ref[0, pl.ds(start, span), :]
    k_ctx = kc_ref[0]
    v_ctx = vc_ref[0]
    qpos = i * BLOCK_A + lax.broadcasted_iota(jnp.int32, (BLOCK_A, span), 0)
    kpos = start + lax.broadcasted_iota(jnp.int32, (BLOCK_A, span), 1)
    bias = jnp.where(jnp.abs(kpos - qpos) <= WINDOW, 0.0, NEG_INF).astype(jnp.float32)

    def heads(j, half):
        return k_loc, v_loc, k_ctx, v_ctx, bias, sink_ref[half * GQA_GROUP + j]

    _pair_attention(q_ref, heads, o_ref)


def _window_attention(qa, ka, va, kac, vac, sink, *, batch, seq):
    nb = seq // BLOCK_A
    n_ctx = kac.shape[1]
    return pl.pallas_call(
        functools.partial(_win_kernel, seq=seq),
        grid=(batch, nb),
        in_specs=[pl.BlockSpec(memory_space=pltpu.SMEM),
                  pl.BlockSpec((BLOCK_A, WIDTH_A), lambda b, i: (b * nb + i, 0)),
                  pl.BlockSpec((1, seq, WIDTH_KV_A), lambda b, i: (b, 0, 0)),
                  pl.BlockSpec((1, seq, WIDTH_KV_A), lambda b, i: (b, 0, 0)),
                  pl.BlockSpec((1, n_ctx, WIDTH_KV_A), lambda b, i: (b, 0, 0)),
                  pl.BlockSpec((1, n_ctx, WIDTH_KV_A), lambda b, i: (b, 0, 0))],
        out_specs=pl.BlockSpec((BLOCK_A, WIDTH_A), lambda b, i: (b * nb + i, 0)),
        out_shape=jax.ShapeDtypeStruct((batch * seq, WIDTH_A), jnp.bfloat16),
        compiler_params=_params("parallel", "arbitrary"),
        name="window_attention",
    )(sink, qa, ka, va, kac, vac)


def _nb_kernel(q_ref, k_ref, v_ref, kc_ref, vc_ref, tab_ref, o_ref, *, rows_n):
    m = pl.program_id(1)
    n_keys = NB_K_ROWS * GRID_W
    start_row = jnp.clip(NB_Q_ROWS * m - NA_ROWS // 2, 0, rows_n - NB_K_ROWS)
    start = pl.multiple_of(start_row * GRID_W, LANES)

    def heads(j, half):
        cols = slice(j * LANES, (j + 1) * LANES)
        return (k_ref[0, pl.ds(start, n_keys), cols], v_ref[0, pl.ds(start, n_keys), cols],
                kc_ref[0, :, cols], vc_ref[0, :, cols],
                tab_ref[0, 2 * j + half].astype(jnp.float32), None)

    _pair_attention(q_ref, heads, o_ref)


def _nb_classes(rows_n):
    n_blocks = rows_n // NB_Q_ROWS
    sig = {}
    cls_of_block = []
    reps = []
    for m in range(n_blocks):
        start_row = int(np.clip(NB_Q_ROWS * m - NA_ROWS // 2, 0, rows_n - NB_K_ROWS))
        key = tuple((start_row - r, int(np.clip(r - NA_ROWS // 2, 0, rows_n - NA_ROWS)) - r)
                    for r in range(NB_Q_ROWS * m, NB_Q_ROWS * (m + 1)))
        if key not in sig:
            sig[key] = len(reps)
            reps.append(m)
        cls_of_block.append(sig[key])
    return np.asarray(cls_of_block, np.int32), reps


def _nb_bias_tables(rpb, rows_n):
    cls_of_block, reps = _nb_classes(rows_n)
    n_heads = rpb.shape[0]
    cq = np.arange(GRID_W)[:, None]
    ck = np.arange(GRID_W)[None, :]
    cs = np.clip(cq - NA_COLS // 2, 0, GRID_W - NA_COLS)
    col_ok = (ck >= cs) & (ck < cs + NA_COLS)
    pick = ((ck - cq + NA_COLS - 1)[None] == np.arange(2 * NA_COLS - 1)[:, None, None]) & col_ok[None]
    tiles = jnp.einsum('hrd,dqk->hrqk', rpb.astype(jnp.float32), jnp.asarray(pick, jnp.float32),
                       precision=lax.Precision.HIGHEST)
    tiles = jnp.where(jnp.asarray(col_ok)[None, None], tiles, NEG_INF).astype(jnp.bfloat16)
    blocked = jnp.full((n_heads, GRID_W, GRID_W), NEG_INF, jnp.bfloat16)
    tabs = []
    for m in reps:
        start_row = int(np.clip(NB_Q_ROWS * m - NA_ROWS // 2, 0, rows_n - NB_K_ROWS))
        q_rows = []
        for r in range(NB_Q_ROWS * m, NB_Q_ROWS * (m + 1)):
            rs = int(np.clip(r - NA_ROWS // 2, 0, rows_n - NA_ROWS))
            q_rows.append(jnp.concatenate(
                [tiles[:, krow - r + NA_ROWS - 1] if rs <= krow < rs + NA_ROWS else blocked
                 for krow in range(start_row, start_row + NB_K_ROWS)], axis=2))
        tabs.append(jnp.concatenate(q_rows, axis=1))
    return jnp.stack(tabs), cls_of_block


def _neighbourhood_attention(qb, kb, vb, kbc, vbc, tabs, cls_of_block, *, batch, seq):
    rows_n = seq // GRID_W
    nblk = rows_n // NB_Q_ROWS
    tq = NB_Q_ROWS * GRID_W
    n_ctx = kbc.shape[1]
    grid_spec = pltpu.PrefetchScalarGridSpec(
        num_scalar_prefetch=1,
        grid=(batch, nblk),
        in_specs=[pl.BlockSpec((tq, WIDTH_B), lambda b, m, c: (b * nblk + m, 0)),
                  pl.BlockSpec((1, seq, WIDTH_B), lambda b, m, c: (b, 0, 0)),
                  pl.BlockSpec((1, seq, WIDTH_B), lambda b, m, c: (b, 0, 0)),
                  pl.BlockSpec((1, n_ctx, WIDTH_B), lambda b, m, c: (b, 0, 0)),
                  pl.BlockSpec((1, n_ctx, WIDTH_B), lambda b, m, c: (b, 0, 0)),
                  pl.BlockSpec((1,) + tabs.shape[1:], lambda b, m, c: (c[m], 0, 0, 0))],
        out_specs=pl.BlockSpec((tq, WIDTH_B), lambda b, m, c: (b * nblk + m, 0)),
    )

    def body(c_ref, *refs):
        _nb_kernel(*refs, rows_n=rows_n)

    return pl.pallas_call(
        body,
        grid_spec=grid_spec,
        out_shape=jax.ShapeDtypeStruct((batch * seq, WIDTH_B), jnp.bfloat16),
        compiler_params=_params("parallel", "arbitrary"),
        name="neighbourhood_attention",
    )(jnp.asarray(cls_of_block), qb, kb, vb, kbc, vbc, tabs)


def _outproj_kernel(oa_ref, ob_ref, x_ref, ga_ref, sh_ref, sc_ref, gga_ref, ggb_ref, gpost_ref, gpre_ref,
                    wo_ref, wr_ref, br_ref,
                    x1_ref, h2_ref, idx_ref, wts_ref, rank_ref, size_ref, base_ref, cnt_ref, run_ref):
    i = pl.program_id(0)

    @pl.when(i == 0)
    def _():
        run_ref[...] = jnp.zeros_like(run_ref)

    na = _rms(oa_ref[...].astype(jnp.float32), gga_ref[...]).astype(jnp.bfloat16)
    nb = _rms(ob_ref[...].astype(jnp.float32), ggb_ref[...]).astype(jnp.bfloat16)
    mix = _dot(na, wo_ref[:WIDTH_A, :]) + _dot(nb, wo_ref[WIDTH_A:, :])
    x1 = x_ref[...] + ga_ref[0] * _rms(mix, gpost_ref[...])
    x1_ref[...] = x1
    h2 = _rms(x1, gpre_ref[...]) * (1.0 + sc_ref[0]) + sh_ref[0]
    h2_ref[...] = h2.astype(h2_ref.dtype)

    logits = jnp.dot(h2, wr_ref[...], preferred_element_type=jnp.float32,
                     precision=lax.Precision.HIGHEST) + br_ref[...]
    t, e = logits.shape
    lane = lax.broadcasted_iota(jnp.int32, (t, e), 1)
    work = logits
    chosen = jnp.zeros((t, e), jnp.float32)
    vals, sels, hots = [], [], []
    for _k in range(TOP_K):
        mx = jnp.max(work, axis=1, keepdims=True)
        sel = jnp.min(jnp.where(work == mx, lane, e), axis=1, keepdims=True)
        hot = lane == sel
        vals.append(mx)
        sels.append(sel)
        hots.append(hot)
        work = jnp.where(hot, -jnp.inf, work)
        chosen = chosen + hot.astype(jnp.float32)
    ex = [jnp.exp(v - vals[0]) for v in vals]
    den = ex[0] + ex[1] + ex[2] + ex[3]

    r_i = lax.broadcasted_iota(jnp.int32, (t, t), 0)
    c_i = lax.broadcasted_iota(jnp.int32, (t, t), 1)
    before = (c_i < r_i).astype(jnp.bfloat16)
    prefix = _dot(before, chosen.astype(jnp.bfloat16))
    ranks = [jnp.sum(jnp.where(h, prefix, 0.0), axis=1, keepdims=True) for h in hots]
    size = jnp.sum(chosen, axis=0, keepdims=True).astype(jnp.int32)
    size = (size + (ROW_ALIGN - 1)) // ROW_ALIGN * ROW_ALIGN
    size_ref[0] = size
    base_ref[0] = run_ref[...]
    run_ref[...] = run_ref[...] + size
    cnt_ref[...] = run_ref[...]

    l4 = lax.broadcasted_iota(jnp.int32, (t, TOP_K), 1)

    def cols(parts):
        out = parts[TOP_K - 1]
        for k in range(TOP_K - 2, -1, -1):
            out = jnp.where(l4 == k, parts[k], out)
        return out

    idx_ref[...] = cols(sels)
    wts_ref[...] = cols([v / den for v in ex])
    rank_ref[...] = cols(ranks).astype(jnp.int32)


def _outproj(oa, ob, x2d, mod3, gga, ggb, gpost, gpre, w_out, w_router, b_router, *, seq):
    t, d = x2d.shape
    tile = OUT_TILE
    per_seq = seq // tile
    row = lambda i: (i, 0)
    const = lambda i: (0, 0)
    modspec = lambda part: pl.BlockSpec((1, 1, d), lambda i: (i // per_seq, 0, part))
    outs = pl.pallas_call(
        _outproj_kernel,
        grid=(t // tile,),
        in_specs=[pl.BlockSpec((tile, WIDTH_A), row), pl.BlockSpec((tile, WIDTH_B), row),
                  pl.BlockSpec((tile, d), row),
                  modspec(2), modspec(3), modspec(4),
                  pl.BlockSpec((1, WIDTH_A), const), pl.BlockSpec((1, WIDTH_B), const),
                  pl.BlockSpec((1, d), const), pl.BlockSpec((1, d), const),
                  pl.BlockSpec(w_out.shape, const), pl.BlockSpec(w_router.shape, const),
                  pl.BlockSpec((1, N_EXPERTS), const)],
        out_specs=[pl.BlockSpec((tile, d), row), pl.BlockSpec((tile, d), row),
                   pl.BlockSpec((tile, TOP_K), row), pl.BlockSpec((tile, TOP_K), row),
                   pl.BlockSpec((tile, TOP_K), row),
                   pl.BlockSpec((1, 1, N_EXPERTS), lambda i: (i, 0, 0)),
                   pl.BlockSpec((1, 1, N_EXPERTS), lambda i: (i, 0, 0)),
                   pl.BlockSpec((1, N_EXPERTS), const)],
        out_shape=[jax.ShapeDtypeStruct((t, d), jnp.float32), jax.ShapeDtypeStruct((t, d), jnp.bfloat16),
                   jax.ShapeDtypeStruct((t, TOP_K), jnp.int32), jax.ShapeDtypeStruct((t, TOP_K), jnp.float32),
                   jax.ShapeDtypeStruct((t, TOP_K), jnp.int32),
                   jax.ShapeDtypeStruct((t // tile, 1, N_EXPERTS), jnp.int32),
                   jax.ShapeDtypeStruct((t // tile, 1, N_EXPERTS), jnp.int32),
                   jax.ShapeDtypeStruct((1, N_EXPERTS), jnp.int32)],
        scratch_shapes=[pltpu.VMEM((1, N_EXPERTS), jnp.int32)],
        compiler_params=_params("arbitrary"),
        name="outproj_router",
    )(oa, ob, x2d, mod3, mod3, mod3, gga, ggb, gpost, gpre, w_out, w_router, b_router)
    return outs


def _piece_table(tab_ref, e):
    return (pl.multiple_of(tab_ref[e], ROW_ALIGN), pl.multiple_of(tab_ref[N_EXPERTS + e], ROW_ALIGN),
            pl.multiple_of(tab_ref[2 * N_EXPERTS + e], ROW_ALIGN))


def _dispatch_kernel(fill_from_ref, pad_end_ref, tab_ref, h_ref, lpos_ref, w_ref, xs_ref,
                     stage_ref, zero_ref, pend_ref, sems, zsem):
    i = pl.program_id(0)
    n_steps = pl.num_programs(0)
    slot = i % 2
    n_local, width = stage_ref.shape[1:]
    d = h_ref.shape[1]

    def fill(row):
        return pltpu.make_async_copy(zero_ref, xs_ref.at[pl.ds(pl.multiple_of(row, FFN_TILE), FFN_TILE)], zsem)

    def fills(e):
        return (pad_end_ref[e] - fill_from_ref[e]) // FFN_TILE

    def drain(s):
        n = pl.multiple_of(pend_ref[s], ROW_ALIGN)

        @pl.when(n > 0)
        def _():
            pltpu.make_async_copy(stage_ref.at[s, pl.ds(0, n)], xs_ref.at[pl.ds(0, n)], sems.at[s]).wait()
        pend_ref[s] = 0

    @pl.when(i == 0)
    def _():
        zero_ref[...] = jnp.zeros_like(zero_ref)
        pend_ref[0] = 0
        pend_ref[1] = 0

        def start_e(e, c):
            def start_j(j, c2):
                fill(fill_from_ref[e] + j * FFN_TILE).start()
                return c2
            lax.fori_loop(0, fills(e), start_j, 0)
            return c + fills(e)
        n_fill = lax.fori_loop(0, N_EXPERTS, start_e, 0)

        def wait_all(j, c):
            fill(0).wait()
            return c
        lax.fori_loop(0, n_fill, wait_all, 0)

    row = lax.broadcasted_iota(jnp.int32, (n_local, h_ref.shape[0]), 0)
    place = None
    row_w = None
    for k in range(TOP_K):
        hit = row == lpos_ref[0, k:k + 1, :]
        place = hit if place is None else place | hit
        wk = jnp.sum(jnp.where(hit, w_ref[0, k:k + 1, :], 0.0), axis=1, keepdims=True)
        row_w = wk if row_w is None else row_w + wk
    rows = _dot(place.astype(jnp.bfloat16), h_ref[...])

    drain(slot)
    stage_ref[slot, :, :d] = rows
    stage_ref[slot, :, d:] = jnp.broadcast_to(row_w, (n_local, width - d))

    def piece(e, total):
        loc, glob, n = _piece_table(tab_ref, e)

        @pl.when(n > 0)
        def _():
            pltpu.make_async_copy(stage_ref.at[slot, pl.ds(loc, n)], xs_ref.at[pl.ds(glob, n)], sems.at[slot]).start()
        return total + n
    pend_ref[slot] = lax.fori_loop(0, N_EXPERTS, piece, 0)

    tail0 = pad_end_ref[N_EXPERTS - 1]
    n_tail = (xs_ref.shape[0] - tail0) // FFN_TILE
    per_step = (n_tail + n_steps - 1) // n_steps

    def start_tail(j, c):
        t = i * per_step + j

        @pl.when(t < n_tail)
        def _():
            fill(tail0 + t * FFN_TILE).start()
        return c
    lax.fori_loop(0, per_step, start_tail, 0)

    @pl.when(i == n_steps - 1)
    def _():
        drain(0)
        drain(1)

        def wait_tail(j, c):
            fill(0).wait()
            return c
        lax.fori_loop(0, n_tail, wait_tail, 0)


def _dispatch(h2, tab_flat, lpos_t, w_t, fill_from, pad_end, *, n_rows):
    t, d = h2.shape
    tile = DISPATCH_TILE
    n_local = tile * TOP_K + N_EXPERTS * ROW_ALIGN
    width = d + LANES
    grid_spec = pltpu.PrefetchScalarGridSpec(
        num_scalar_prefetch=2,
        grid=(t // tile,),
        in_specs=[pl.BlockSpec((TAB_WIDTH,), lambda i, *_: (i,), memory_space=pltpu.SMEM),
                  pl.BlockSpec((tile, d), lambda i, *_: (i, 0)),
                  pl.BlockSpec((1, TOP_K, tile), lambda i, *_: (i, 0, 0)),
                  pl.BlockSpec((1, TOP_K, tile), lambda i, *_: (i, 0, 0))],
        out_specs=pl.BlockSpec(memory_space=pl.ANY),
        scratch_shapes=[pltpu.VMEM((2, n_local, width), jnp.float32),
                        pltpu.VMEM((FFN_TILE, width), jnp.float32),
                        pltpu.SMEM((2,), jnp.int32),
                        pltpu.SemaphoreType.DMA((2,)), pltpu.SemaphoreType.DMA(())],
    )
    return pl.pallas_call(
        _dispatch_kernel,
        grid_spec=grid_spec,
        out_shape=jax.ShapeDtypeStruct((n_rows, width), jnp.float32),
        compiler_params=_params("arbitrary"),
        name="dispatch",
    )(fill_from, pad_end, tab_flat, h2, lpos_t, w_t)


def _ffn_kernel(te_ref, rows_ref, n_used_ref, x_ref, w1_ref, b1_ref, w2_ref, b2_ref, y_ref,
                w1p_ref, w2b_ref, act_ref):
    i = pl.program_id(0)
    live = i < n_used_ref[0]
    n_pair = w1_ref.shape[2] // FFN_CHUNK
    half = FFN_CHUNK // 2
    d = y_ref.shape[1]
    n_sub = jnp.where(live, (rows_ref[i] + FFN_TILE - 1) // FFN_TILE, 0)

    @pl.when(live & ((i == 0) | (te_ref[i] != te_ref[jnp.maximum(i - 1, 0)])))
    def _():
        src = lax.broadcasted_iota(jnp.int32, (FFN_CHUNK, FFN_CHUNK), 0)
        dst = lax.broadcasted_iota(jnp.int32, (FFN_CHUNK, FFN_CHUNK), 1)
        unzip = (src == jnp.where(dst < half, 2 * dst, 2 * (dst - half) + 1)).astype(jnp.bfloat16)
        for c in range(n_pair):
            cs = slice(c * FFN_CHUNK, (c + 1) * FFN_CHUNK)
            w1p_ref[:, cs] = _dot(w1_ref[0, :, cs].astype(jnp.bfloat16), unzip).astype(jnp.bfloat16)
        w2b_ref[...] = w2_ref[0].astype(jnp.bfloat16)

    def tile(s, carry):
        rows = pl.ds(pl.multiple_of(s * FFN_TILE, FFN_TILE), FFN_TILE)
        x = x_ref[rows, :d].astype(jnp.bfloat16)
        for c in range(0, n_pair, 2):
            cs = slice(c * FFN_CHUNK, (c + 2) * FFN_CHUNK)
            h = _dot(x, w1p_ref[:, cs]) + b1_ref[0, :, cs]
            g = jnp.concatenate([h[:, :half], h[:, FFN_CHUNK:FFN_CHUNK + half]], axis=1)
            u = jnp.concatenate([h[:, half:FFN_CHUNK], h[:, FFN_CHUNK + half:]], axis=1)
            g = jnp.minimum(g, SWIGLU_LIMIT)
            u = jnp.clip(u, -SWIGLU_LIMIT, SWIGLU_LIMIT)
            act = g * jax.nn.sigmoid(SWIGLU_ALPHA * g) * (u + 1.0)
            act_ref[:, c * half:(c + 2) * half] = act.astype(act_ref.dtype)
        y_ref[rows, :] = (_dot(act_ref[...], w2b_ref[...]) + b2_ref[0]) * x_ref[rows, d:d + 1]
        return carry
    lax.fori_loop(0, n_sub, tile, 0)

    def blank(s, carry):
        y_ref[pl.ds(pl.multiple_of(s * FFN_TILE, FFN_TILE), FFN_TILE), :] = jnp.zeros((FFN_TILE, d), y_ref.dtype)
        return carry
    lax.fori_loop(n_sub, y_ref.shape[0] // FFN_TILE, blank, 0)


def _unzip_bias(b1):
    e, f2 = b1.shape
    half = FFN_CHUNK // 2
    return b1.reshape(e, f2 // FFN_CHUNK, half, 2).transpose(0, 1, 3, 2).reshape(e, 1, f2)


def _ffn(xs, step_expert, step_rows, n_used, w1, b1, w2, b2):
    rows, width = xs.shape
    d, f2 = w1.shape[1:]
    n_steps = rows // FFN_STEP
    live = lambda i, nu: jnp.minimum(i, nu[0] - 1)
    wspec = lambda shape: pl.BlockSpec((1,) + shape, lambda i, te, nr, nu: (te[live(i, nu)], 0, 0))
    grid_spec = pltpu.PrefetchScalarGridSpec(
        num_scalar_prefetch=3,
        grid=(n_steps,),
        in_specs=[pl.BlockSpec((FFN_STEP, width), lambda i, te, nr, nu: (live(i, nu), 0)),
                  wspec((d, f2)), wspec((1, f2)), wspec((f2 // 2, d)), wspec((1, d))],
        out_specs=pl.BlockSpec((FFN_STEP, d), lambda i, te, nr, nu: (i, 0)),
        scratch_shapes=[pltpu.VMEM((d, f2), jnp.bfloat16), pltpu.VMEM((f2 // 2, d), jnp.bfloat16),
                        pltpu.VMEM((FFN_TILE, f2 // 2), jnp.bfloat16)],
    )
    return pl.pallas_call(
        _ffn_kernel,
        grid_spec=grid_spec,
        out_shape=jax.ShapeDtypeStruct((rows, d), jnp.float32),
        compiler_params=_params("arbitrary"),
        name="expert_ffn",
    )(step_expert, step_rows, n_used, xs, w1, _unzip_bias(b1), w2, b2[:, None, :])


def _combine_kernel(tab_ref, next_tab_ref, lpos_ref, x1_ref, ga_ref, g_ref, y_ref, o_ref, buf_ref, sems):
    i = pl.program_id(0)
    n_steps = pl.num_programs(0)
    slot = i % 2
    tile = x1_ref.shape[0]
    n_local = buf_ref.shape[1]

    def fetch(tab, s):
        def piece(e, c):
            loc, glob, n = _piece_table(tab, e)

            @pl.when(n > 0)
            def _():
                pltpu.make_async_copy(y_ref.at[pl.ds(glob, n)], buf_ref.at[s, pl.ds(loc, n)], sems.at[s]).start()
            return c
        lax.fori_loop(0, N_EXPERTS, piece, 0)

    @pl.when(i == 0)
    def _():
        buf_ref[...] = jnp.zeros_like(buf_ref)
        fetch(tab_ref, 0)

    @pl.when(i + 1 < n_steps)
    def _():
        fetch(next_tab_ref, 1 - slot)

    n = pl.multiple_of(tab_ref[3 * N_EXPERTS], ROW_ALIGN)

    @pl.when(n > 0)
    def _():
        pltpu.make_async_copy(y_ref.at[pl.ds(0, n)], buf_ref.at[slot, pl.ds(0, n)], sems.at[slot]).wait()

    col = lax.broadcasted_iota(jnp.int32, (tile, n_local), 1)
    pos = lpos_ref[...]
    pick = col == pos[:, 0:1]
    for k in range(1, TOP_K):
        pick = pick | (col == pos[:, k:k + 1])
    pick = pick.astype(jnp.bfloat16)
    rows = buf_ref[slot]
    hi = rows.astype(jnp.bfloat16)
    lo = (rows - hi.astype(jnp.float32)).astype(jnp.bfloat16)
    y = _dot(pick, hi) + _dot(pick, lo)
    o_ref[...] = x1_ref[...] + ga_ref[0] * _rms(y, g_ref[...])


def _combine(ys, tab_flat, lpos, x1, mod3, g_post, *, seq):
    t, d = x1.shape
    tile = DISPATCH_TILE
    n_steps = t // tile
    per_seq = seq // tile
    n_local = tile * TOP_K + N_EXPERTS * ROW_ALIGN
    return pl.pallas_call(
        _combine_kernel,
        grid=(n_steps,),
        in_specs=[pl.BlockSpec((TAB_WIDTH,), lambda i: (i,), memory_space=pltpu.SMEM),
                  pl.BlockSpec((TAB_WIDTH,), lambda i: (jnp.minimum(i + 1, n_steps - 1),), memory_space=pltpu.SMEM),
                  pl.BlockSpec((tile, TOP_K), lambda i: (i, 0)),
                  pl.BlockSpec((tile, d), lambda i: (i, 0)),
                  pl.BlockSpec((1, 1, d), lambda i: (i // per_seq, 0, 5)),
                  pl.BlockSpec((1, d), lambda i: (0, 0)),
                  pl.BlockSpec(memory_space=pl.ANY)],
        out_specs=pl.BlockSpec((tile, d), lambda i: (i, 0)),
        out_shape=jax.ShapeDtypeStruct((t, d), jnp.float32),
        scratch_shapes=[pltpu.VMEM((2, n_local, d), ys.dtype), pltpu.SemaphoreType.DMA((2,))],
        compiler_params=_params("arbitrary"),
        name="combine",
    )(tab_flat, tab_flat, lpos, x1, mod3, g_post, ys)


def _rope_tables(seq):
    pos = np.arange(seq)
    n_freq = HEAD_DIM // 4
    freqs = ROPE_BASE ** (-jnp.arange(n_freq, dtype=jnp.float32) / n_freq)
    rows = jnp.asarray(pos // GRID_W, jnp.float32)[:, None] * freqs[None, :]
    cols = jnp.asarray(pos % GRID_W, jnp.float32)[:, None] * freqs[None, :]
    ang = jnp.concatenate([rows, rows, cols, cols], axis=1)
    sign = np.tile(np.repeat([-1.0, 1.0], n_freq), 2).astype(np.float32)
    cos = jnp.cos(ang)
    sin = jnp.sin(ang) * sign[None, :]
    reps = LANES // HEAD_DIM
    return jnp.tile(cos, (1, reps)), jnp.tile(sin, (1, reps))


def kernel(x, c, ctx, c_ctx, w_ada, b_ada, g_pre_mix, g_post_mix, g_pre_ffn, g_post_ffn, w_in, g_grp_a, g_grp_b,
           sink_a, rpb_b, w_out, w_router, b_router, w_mlp1, b_mlp1, w_mlp2, b_mlp2):
    batch, seq, d = x.shape
    n_ctx = ctx.shape[1]
    assert w_ada.shape[0] == 1, "single layer"
    assert seq % (GRID_W * NB_Q_ROWS) == 0 and seq // GRID_W >= NB_K_ROWS and seq >= BLOCK_A + 2 * WINDOW
    n_tok = batch * seq
    bf16 = jnp.bfloat16

    head_order = np.asarray([k * GQA_GROUP + j for j in range(GQA_GROUP) for k in range(N_KV_A)])
    col_order = (head_order[:, None] * HEAD_DIM + np.arange(HEAD_DIM)[None, :]).reshape(-1)

    mod_rows = -(-(batch + 1) // 8) * 8
    cc = jnp.zeros((mod_rows, d), jnp.float32).at[:batch].set(c).at[batch].set(c_ctx)
    mod3 = _ada(cc, w_ada[0], b_ada[0]).reshape(mod_rows, 1, 6 * d)

    w_in0 = w_in[0]
    w_lat = jnp.concatenate([w_in0[:, :WIDTH_A][:, col_order], w_in0[:, WIDTH_A:]], axis=1).astype(bf16)
    kv_cols = np.concatenate([np.arange(WIDTH_A, WIDTH_A + 2 * WIDTH_KV_A),
                              np.arange(WIDTH_A + 2 * WIDTH_KV_A + WIDTH_B, w_in0.shape[1])])
    w_ctx = w_in0[:, kv_cols].astype(bf16)
    g_pre = g_pre_mix[0].reshape(1, d)
    qa, ka, va, qb, kb, vb = _inproj(x.reshape(n_tok, d), mod3, g_pre, w_lat, _rope_tables(seq),
                                     seq=seq, mod_row0=0, latent=True)
    kac, vac, kbc, vbc = _inproj(ctx.reshape(batch * n_ctx, d), mod3, g_pre, w_ctx, None,
                                 seq=n_ctx, mod_row0=batch, latent=False)

    oa = _window_attention(qa, ka.reshape(batch, seq, -1), va.reshape(batch, seq, -1),
                           kac.reshape(batch, n_ctx, -1), vac.reshape(batch, n_ctx, -1),
                           sink_a[0].astype(jnp.float32), batch=batch, seq=seq)
    tabs, cls_of_block = _nb_bias_tables(rpb_b[0], seq // GRID_W)
    ob = _neighbourhood_attention(qb, kb.reshape(batch, seq, -1), vb.reshape(batch, seq, -1),
                                  kbc.reshape(batch, n_ctx, -1), vbc.reshape(batch, n_ctx, -1),
                                  tabs, cls_of_block, batch=batch, seq=seq)

    w_out0 = w_out[0]
    w_o = jnp.concatenate([w_out0[:WIDTH_A][col_order], w_out0[WIDTH_A:]], axis=0).astype(bf16)
    x1, h2, idx, wts, rank, size, base, counts = _outproj(
        oa, ob, x.reshape(n_tok, d), mod3,
        g_grp_a[0][col_order].reshape(1, -1), g_grp_b[0].reshape(1, -1),
        g_post_mix[0].reshape(1, d), g_pre_ffn[0].reshape(1, d),
        w_o, w_router[0], b_router[0].reshape(1, -1), seq=seq)

    n_tok_tiles = n_tok // DISPATCH_TILE
    size = size.reshape(n_tok_tiles, N_EXPERTS)
    counts = counts.reshape(-1)
    padded = (counts + FFN_STEP - 1) // FFN_STEP * FFN_STEP
    pad_end = jnp.cumsum(padded).astype(jnp.int32)
    pad_start = pad_end - padded
    n_steps = (n_tok * TOP_K + n_tok_tiles * N_EXPERTS * (ROW_ALIGN - 1)) // FFN_STEP + N_EXPERTS
    n_used = (pad_end[-1:] // FFN_STEP).astype(jnp.int32)
    step_row0 = jnp.arange(n_steps, dtype=jnp.int32) * FFN_STEP
    step_expert = jnp.minimum(jnp.sum(step_row0[:, None] >= pad_end[None, :], axis=1),
                              N_EXPERTS - 1).astype(jnp.int32)
    own = step_expert[:, None] == jnp.arange(N_EXPERTS, dtype=jnp.int32)
    step_rows = jnp.clip(jnp.sum(jnp.where(own, (pad_start + counts)[None, :], 0), axis=1) - step_row0,
                         0, FFN_STEP).astype(jnp.int32)
    fill_from = (pad_start + jnp.maximum((counts + FFN_TILE - 1) // FFN_TILE * FFN_TILE - FFN_TILE, 0)
                 ).astype(jnp.int32)
    local0 = jnp.cumsum(size, axis=1) - size
    global0 = pad_start[None, :] + base.reshape(n_tok_tiles, N_EXPERTS)
    tab = jnp.concatenate([local0, global0, size, jnp.sum(size, axis=1, keepdims=True),
                           jnp.zeros((n_tok_tiles, TAB_WIDTH - 3 * N_EXPERTS - 1), jnp.int32)], axis=1)
    tab = tab.reshape(-1).astype(jnp.int32)
    hot = idx.reshape(n_tok_tiles, DISPATCH_TILE, TOP_K, 1) == jnp.arange(N_EXPERTS, dtype=jnp.int32)
    lpos = rank.reshape(n_tok_tiles, DISPATCH_TILE, TOP_K) + jnp.sum(
        jnp.where(hot, local0[:, None, None, :], 0), axis=-1)
    lpos_t = jnp.swapaxes(lpos, 1, 2)
    w_t = jnp.swapaxes(wts.reshape(n_tok_tiles, DISPATCH_TILE, TOP_K), 1, 2)

    xs = _dispatch(h2, tab, lpos_t, w_t, fill_from, pad_end, n_rows=n_steps * FFN_STEP)
    ys = _ffn(xs, step_expert, step_rows, n_used, w_mlp1[0], b_mlp1[0], w_mlp2[0], b_mlp2[0])
    out = _combine(ys, tab, lpos.reshape(n_tok, TOP_K), x1, mod3, g_post_ffn[0].reshape(1, d), seq=seq)
    return out.reshape(batch, seq, d)
```

```python
import functools

import numpy as np
import jax
import jax.numpy as jnp
from jax import lax
from jax.experimental import pallas as pl
from jax.experimental.pallas import tpu as pltpu

GRID_W = 64
HEAD_DIM = 64
N_HEADS_A = 8
N_KV_A = 2
GQA_GROUP = N_HEADS_A // N_KV_A
N_HEADS_B = 8
WIDTH_A = N_HEADS_A * HEAD_DIM
WIDTH_KV_A = N_KV_A * HEAD_DIM
WIDTH_B = N_HEADS_B * HEAD_DIM
WINDOW = 128
BLOCK_A = 128
NA_ROWS = 8
NA_COLS = 16
N_EXPERTS = 32
TOP_K = 4
SWIGLU_LIMIT = 7.0
SWIGLU_ALPHA = 1.702
ROPE_BASE = 10000.0
EPS = 1e-6
NEG_INF = -1e30

LANES = 128
VMEM_LIMIT = 56 * 1024 * 1024

LOG2E = 1.4426950408889634
ATT_BLOCKS = 2
NB_Q_ROWS = 2
NB_K_ROWS = NB_Q_ROWS + NA_ROWS
PROJ_TILE = 512
OUT_TILE = 256
FFN_TILE = 512
FFN_STEP = 2 * FFN_TILE
FFN_CHUNK = 256
DISPATCH_TILE = OUT_TILE
ROW_ALIGN = 8
TAB_WIDTH = 128


def _params(*sem):
    return pltpu.CompilerParams(dimension_semantics=sem, vmem_limit_bytes=VMEM_LIMIT)


def _rms(x, g):
    return x * lax.rsqrt(jnp.mean(x * x, axis=-1, keepdims=True) + EPS) * g


def _dot(a, b):
    return jnp.dot(a, b, preferred_element_type=jnp.float32)


def _dot_nt(a, b):
    return lax.dot_general(a, b, (((1,), (1,)), ((), ())), preferred_element_type=jnp.float32)


def _ada_kernel(c_ref, w_ref, b_ref, o_ref):
    c = c_ref[...]
    s = (c * jax.nn.sigmoid(c)).astype(jnp.bfloat16)
    o_ref[...] = _dot(s, w_ref[...].astype(jnp.bfloat16)) + b_ref[...]


def _ada(cc, w_ada, b_ada):
    rows, d = cc.shape
    n_out = w_ada.shape[1]
    return pl.pallas_call(
        _ada_kernel,
        grid=(n_out // d,),
        in_specs=[pl.BlockSpec((rows, d), lambda j: (0, 0)),
                  pl.BlockSpec((d, d), lambda j: (0, j)),
                  pl.BlockSpec((1, d), lambda j: (0, j))],
        out_specs=pl.BlockSpec((rows, d), lambda j: (0, j)),
        out_shape=jax.ShapeDtypeStruct((rows, n_out), jnp.float32),
        compiler_params=_params("arbitrary"),
        name="ada",
    )(cc, w_ada, b_ada.reshape(1, n_out))


def _rope(x, cos, sin):
    w = x.shape[1]
    reps = w // LANES
    if reps > 1:
        cos = jnp.concatenate([cos] * reps, axis=1)
        sin = jnp.concatenate([sin] * reps, axis=1)
    lane = lax.broadcasted_iota(jnp.int32, x.shape, 1)
    quarter = HEAD_DIM // 4
    partner = jnp.where(lane % (2 * quarter) < quarter,
                        pltpu.roll(x, w - quarter, 1), pltpu.roll(x, quarter, 1))
    return x * cos + partner * sin


def _inproj_kernel(x_ref, sh_ref, sc_ref, g_ref, w_ref, *rest, latent):
    x = x_ref[...]
    h = _rms(x, g_ref[...]) * (1.0 + sc_ref[0]) + sh_ref[0]
    p = _dot(h.astype(jnp.bfloat16), w_ref[...])
    if latent:
        cos_ref, sin_ref, qa_ref, ka_ref, va_ref, qb_ref, kb_ref, vb_ref = rest
        cos, sin = cos_ref[...], sin_ref[...]
        scale = HEAD_DIM ** -0.5 * LOG2E
        o = 0
        qa_ref[...] = (_rope(p[:, o:o + WIDTH_A], cos, sin) * scale).astype(qa_ref.dtype)
        o += WIDTH_A
        ka_ref[...] = _rope(p[:, o:o + WIDTH_KV_A], cos, sin).astype(ka_ref.dtype)
        o += WIDTH_KV_A
        va_ref[...] = p[:, o:o + WIDTH_KV_A].astype(va_ref.dtype)
        o += WIDTH_KV_A
        qb_ref[...] = (p[:, o:o + WIDTH_B] * scale).astype(qb_ref.dtype)
        o += WIDTH_B
    else:
        ka_ref, va_ref, kb_ref, vb_ref = rest
        o = 0
        ka_ref[...] = p[:, o:o + WIDTH_KV_A].astype(ka_ref.dtype)
        o += WIDTH_KV_A
        va_ref[...] = p[:, o:o + WIDTH_KV_A].astype(va_ref.dtype)
        o += WIDTH_KV_A
    kb_ref[...] = p[:, o:o + WIDTH_B].astype(kb_ref.dtype)
    o += WIDTH_B
    vb_ref[...] = p[:, o:o + WIDTH_B].astype(vb_ref.dtype)


def _inproj(x2d, mod3, g_pre, w, rope_tabs, *, seq, mod_row0, latent):
    t, d = x2d.shape
    tile = min(PROJ_TILE, seq)
    per_seq = seq // tile
    if latent:
        mod_row = lambda i: i // per_seq
    else:
        mod_row = lambda i: mod_row0
    in_specs = [pl.BlockSpec((tile, d), lambda i: (i, 0)),
                pl.BlockSpec((1, 1, d), lambda i: (mod_row(i), 0, 0)),
                pl.BlockSpec((1, 1, d), lambda i: (mod_row(i), 0, 1)),
                pl.BlockSpec((1, d), lambda i: (0, 0)),
                pl.BlockSpec(w.shape, lambda i: (0, 0))]
    args = [x2d, mod3, mod3, g_pre, w]
    widths = [WIDTH_KV_A, WIDTH_KV_A, WIDTH_B, WIDTH_B]
    if latent:
        in_specs += [pl.BlockSpec((tile, LANES), lambda i: (i % per_seq, 0))] * 2
        args += list(rope_tabs)
        widths = [WIDTH_A, WIDTH_KV_A, WIDTH_KV_A, WIDTH_B, WIDTH_B, WIDTH_B]
    return pl.pallas_call(
        functools.partial(_inproj_kernel, latent=latent),
        grid=(t // tile,),
        in_specs=in_specs,
        out_specs=[pl.BlockSpec((tile, wd), lambda i: (i, 0)) for wd in widths],
        out_shape=[jax.ShapeDtypeStruct((t, wd), jnp.bfloat16) for wd in widths],
        compiler_params=_params("parallel"),
        name="inproj_latent" if latent else "inproj_ctx",
    )(*args)


def _ones_beside(v, half):
    low = lax.broadcasted_iota(jnp.int32, v.shape, 1) < HEAD_DIM
    return jnp.where(low if half == 0 else ~low, v, jnp.ones_like(v))


def _pair_attention(q_ref, o_ref, rows, keys, values, sink_of, s_ref, p_ref, m_ref):
    tq = rows.stop - rows.start
    low = lax.broadcasted_iota(jnp.int32, (tq, LANES), 1) < HEAD_DIM
    n_pair = q_ref.shape[1] // LANES
    n_loc = None
    for j in range(n_pair):
        q = q_ref[rows, j * LANES:(j + 1) * LANES]
        for half in range(2):
            k_loc, k_ctx, bias = keys(j, half)
            n_loc = k_loc.shape[0]
            qm = jnp.where(low if half == 0 else ~low, q, jnp.zeros_like(q))
            s_ref[2 * j + half, :, :n_loc] = _dot_nt(qm, k_loc) + bias
            s_ref[2 * j + half, :, n_loc:] = _dot_nt(qm, k_ctx)
    for h in range(2 * n_pair):
        s = s_ref[h]
        m = jnp.max(s, axis=1, keepdims=True)
        sink = sink_of(h // 2, h % 2)
        if sink is not None:
            m = jnp.maximum(m, sink)
        p_ref[h] = jnp.exp2((s - m).astype(jnp.bfloat16))
        m_ref[h] = m
    for j in range(n_pair):
        acc = []
        for half in range(2):
            v_loc, v_ctx = values(j, half)
            h = 2 * j + half
            acc.append(_dot(p_ref[h, :, :n_loc], v_loc) + _dot(p_ref[h, :, n_loc:], v_ctx))
        num = jnp.where(low, acc[0], acc[1])
        den = pltpu.roll(jnp.where(low, acc[1], acc[0]), HEAD_DIM, 1)
        if sink_of(j, 0) is not None:
            den = den + jnp.where(low, jnp.exp2(sink_of(j, 0) - m_ref[2 * j]), jnp.exp2(sink_of(j, 1) - m_ref[2 * j + 1]))
        o_ref[rows, j * LANES:(j + 1) * LANES] = (num / den).astype(o_ref.dtype)


def _attention_scratch(tq, n_heads, n_keys):
    return [pltpu.VMEM((n_heads, tq, n_keys), jnp.float32), pltpu.VMEM((n_heads, tq, n_keys), jnp.bfloat16),
            pltpu.VMEM((n_heads, tq, 1), jnp.float32)]


def _win_kernel(sink_ref, q_ref, k_ref, v_ref, kc_ref, vc_ref, o_ref, *scratch, seq):
    span = BLOCK_A + 2 * WINDOW
    k_ctx = kc_ref[0]
    v_ctx = [_ones_beside(vc_ref[0], half) for half in range(2)]
    for sb in range(ATT_BLOCKS):
        i = pl.program_id(1) * ATT_BLOCKS + sb
        start = pl.multiple_of(jnp.clip(i * BLOCK_A - WINDOW, 0, seq - span), BLOCK_A)
        k_loc = k_ref[0, pl.ds(start, span), :]
        v_loc = [_ones_beside(v_ref[0, pl.ds(start, span), :], half) for half in range(2)]
        qpos = i * BLOCK_A + lax.broadcasted_iota(jnp.int32, (BLOCK_A, span), 0)
        kpos = start + lax.broadcasted_iota(jnp.int32, (BLOCK_A, span), 1)
        bias = jnp.where(jnp.abs(kpos - qpos) <= WINDOW, 0.0, NEG_INF).astype(jnp.float32)

        _pair_attention(q_ref, o_ref, slice(sb * BLOCK_A, (sb + 1) * BLOCK_A),
                        lambda j, half: (k_loc, k_ctx, bias),
                        lambda j, half: (v_loc[half], v_ctx[half]),
                        lambda j, half: sink_ref[half * GQA_GROUP + j], *scratch)


def _window_attention(qa, ka, va, kac, vac, sink, *, batch, seq):
    tq = ATT_BLOCKS * BLOCK_A
    nb = seq // tq
    n_ctx = kac.shape[1]
    return pl.pallas_call(
        functools.partial(_win_kernel, seq=seq),
        grid=(batch, nb),
        in_specs=[pl.BlockSpec(memory_space=pltpu.SMEM),
                  pl.BlockSpec((tq, WIDTH_A), lambda b, i: (b * nb + i, 0)),
                  pl.BlockSpec((1, seq, WIDTH_KV_A), lambda b, i: (b, 0, 0)),
                  pl.BlockSpec((1, seq, WIDTH_KV_A), lambda b, i: (b, 0, 0)),
                  pl.BlockSpec((1, n_ctx, WIDTH_KV_A), lambda b, i: (b, 0, 0)),
                  pl.BlockSpec((1, n_ctx, WIDTH_KV_A), lambda b, i: (b, 0, 0))],
        out_specs=pl.BlockSpec((tq, WIDTH_A), lambda b, i: (b * nb + i, 0)),
        out_shape=jax.ShapeDtypeStruct((batch * seq, WIDTH_A), jnp.bfloat16),
        scratch_shapes=_attention_scratch(BLOCK_A, N_HEADS_A, BLOCK_A + 2 * WINDOW + n_ctx),
        compiler_params=_params("parallel", "arbitrary"),
        name="window_attention",
    )(sink, qa, ka, va, kac, vac)


def _nb_kernel(q_ref, k_ref, v_ref, kc_ref, vc_ref, *rest, rows_n):
    tab_refs, o_ref, scratch = rest[:ATT_BLOCKS], rest[ATT_BLOCKS], rest[ATT_BLOCKS + 1:]
    n_keys = NB_K_ROWS * GRID_W
    tq = NB_Q_ROWS * GRID_W
    for sb in range(ATT_BLOCKS):
        m = pl.program_id(1) * ATT_BLOCKS + sb
        start_row = jnp.clip(NB_Q_ROWS * m - NA_ROWS // 2, 0, rows_n - NB_K_ROWS)
        start = pl.multiple_of(start_row * GRID_W, LANES)
        tab_ref = tab_refs[sb]

        def keys(j, half):
            cols = slice(j * LANES, (j + 1) * LANES)
            return (k_ref[0, pl.ds(start, n_keys), cols], kc_ref[0, :, cols],
                    tab_ref[0, 2 * j + half].astype(jnp.float32))

        def values(j, half):
            cols = slice(j * LANES, (j + 1) * LANES)
            return (_ones_beside(v_ref[0, pl.ds(start, n_keys), cols], half), _ones_beside(vc_ref[0, :, cols], half))

        _pair_attention(q_ref, o_ref, slice(sb * tq, (sb + 1) * tq), keys, values, lambda j, half: None, *scratch)


def _nb_classes(rows_n):
    n_blocks = rows_n // NB_Q_ROWS
    sig = {}
    cls_of_block = []
    reps = []
    for m in range(n_blocks):
        start_row = int(np.clip(NB_Q_ROWS * m - NA_ROWS // 2, 0, rows_n - NB_K_ROWS))
        key = tuple((start_row - r, int(np.clip(r - NA_ROWS // 2, 0, rows_n - NA_ROWS)) - r)
                    for r in range(NB_Q_ROWS * m, NB_Q_ROWS * (m + 1)))
        if key not in sig:
            sig[key] = len(reps)
            reps.append(m)
        cls_of_block.append(sig[key])
    return np.asarray(cls_of_block, np.int32), reps


def _nb_bias_tables(rpb, rows_n):
    cls_of_block, reps = _nb_classes(rows_n)
    n_heads = rpb.shape[0]
    cq = np.arange(GRID_W)[:, None]
    ck = np.arange(GRID_W)[None, :]
    cs = np.clip(cq - NA_COLS // 2, 0, GRID_W - NA_COLS)
    col_ok = (ck >= cs) & (ck < cs + NA_COLS)
    pick = ((ck - cq + NA_COLS - 1)[None] == np.arange(2 * NA_COLS - 1)[:, None, None]) & col_ok[None]
    tiles = jnp.einsum('hrd,dqk->hrqk', rpb.astype(jnp.float32), jnp.asarray(pick, jnp.float32),
                       precision=lax.Precision.HIGHEST)
    tiles = jnp.where(jnp.asarray(col_ok)[None, None], tiles * LOG2E, NEG_INF).astype(jnp.bfloat16)
    blocked = jnp.full((n_heads, GRID_W, GRID_W), NEG_INF, jnp.bfloat16)
    tabs = []
    for m in reps:
        start_row = int(np.clip(NB_Q_ROWS * m - NA_ROWS // 2, 0, rows_n - NB_K_ROWS))
        q_rows = []
        for r in range(NB_Q_ROWS * m, NB_Q_ROWS * (m + 1)):
            rs = int(np.clip(r - NA_ROWS // 2, 0, rows_n - NA_ROWS))
            q_rows.append(jnp.concatenate(
                [tiles[:, krow - r + NA_ROWS - 1] if rs <= krow < rs + NA_ROWS else blocked
                 for krow in range(start_row, start_row + NB_K_ROWS)], axis=2))
        tabs.append(jnp.concatenate(q_rows, axis=1))
    return jnp.stack(tabs), cls_of_block


def _neighbourhood_attention(qb, kb, vb, kbc, vbc, tabs, cls_of_block, *, batch, seq):
    rows_n = seq // GRID_W
    tq = ATT_BLOCKS * NB_Q_ROWS * GRID_W
    nstep = seq // tq
    n_ctx = kbc.shape[1]

    def tab_spec(sb):
        return pl.BlockSpec((1,) + tabs.shape[1:], lambda b, m, c: (c[m * ATT_BLOCKS + sb], 0, 0, 0))

    grid_spec = pltpu.PrefetchScalarGridSpec(
        num_scalar_prefetch=1,
        grid=(batch, nstep),
        in_specs=[pl.BlockSpec((tq, WIDTH_B), lambda b, m, c: (b * nstep + m, 0)),
                  pl.BlockSpec((1, seq, WIDTH_B), lambda b, m, c: (b, 0, 0)),
                  pl.BlockSpec((1, seq, WIDTH_B), lambda b, m, c: (b, 0, 0)),
                  pl.BlockSpec((1, n_ctx, WIDTH_B), lambda b, m, c: (b, 0, 0)),
                  pl.BlockSpec((1, n_ctx, WIDTH_B), lambda b, m, c: (b, 0, 0))]
                 + [tab_spec(sb) for sb in range(ATT_BLOCKS)],
        out_specs=pl.BlockSpec((tq, WIDTH_B), lambda b, m, c: (b * nstep + m, 0)),
        scratch_shapes=_attention_scratch(NB_Q_ROWS * GRID_W, N_HEADS_B, NB_K_ROWS * GRID_W + n_ctx),
    )

    def body(c_ref, *refs):
        _nb_kernel(*refs, rows_n=rows_n)

    return pl.pallas_call(
        body,
        grid_spec=grid_spec,
        out_shape=jax.ShapeDtypeStruct((batch * seq, WIDTH_B), jnp.bfloat16),
        compiler_params=_params("parallel", "arbitrary"),
        name="neighbourhood_attention",
    )(jnp.asarray(cls_of_block), qb, kb, vb, kbc, vbc, *([tabs] * ATT_BLOCKS))


def _outproj_kernel(oa_ref, ob_ref, x_ref, ga_ref, sh_ref, sc_ref, gga_ref, ggb_ref, gpost_ref, gpre_ref,
                    wo_ref, wr_ref, br_ref,
                    x1_ref, h2_ref, idx_ref, wts_ref, rank_ref, size_ref, base_ref, cnt_ref, run_ref):
    i = pl.program_id(0)

    @pl.when(i == 0)
    def _():
        run_ref[...] = jnp.zeros_like(run_ref)

    na = _rms(oa_ref[...].astype(jnp.float32), gga_ref[...]).astype(jnp.bfloat16)
    nb = _rms(ob_ref[...].astype(jnp.float32), ggb_ref[...]).astype(jnp.bfloat16)
    mix = _dot(na, wo_ref[:WIDTH_A, :]) + _dot(nb, wo_ref[WIDTH_A:, :])
    x1 = x_ref[...] + ga_ref[0] * _rms(mix, gpost_ref[...])
    x1_ref[...] = x1
    h2 = _rms(x1, gpre_ref[...]) * (1.0 + sc_ref[0]) + sh_ref[0]
    h2_ref[...] = h2.astype(h2_ref.dtype)

    logits = jnp.dot(h2, wr_ref[...], preferred_element_type=jnp.float32,
                     precision=lax.Precision.HIGHEST) + br_ref[...]
    t, e = logits.shape
    lane = lax.broadcasted_iota(jnp.int32, (t, e), 1)
    work = logits
    chosen = jnp.zeros((t, e), jnp.float32)
    vals, sels, hots = [], [], []
    for _k in range(TOP_K):
        mx = jnp.max(work, axis=1, keepdims=True)
        sel = jnp.min(jnp.where(work == mx, lane, e), axis=1, keepdims=True)
        hot = lane == sel
        vals.append(mx)
        sels.append(sel)
        hots.append(hot)
        work = jnp.where(hot, -jnp.inf, work)
        chosen = chosen + hot.astype(jnp.float32)
    ex = [jnp.exp(v - vals[0]) for v in vals]
    den = ex[0] + ex[1] + ex[2] + ex[3]

    r_i = lax.broadcasted_iota(jnp.int32, (t, t), 0)
    c_i = lax.broadcasted_iota(jnp.int32, (t, t), 1)
    before = (c_i < r_i).astype(jnp.bfloat16)
    prefix = _dot(before, chosen.astype(jnp.bfloat16))
    ranks = [jnp.sum(jnp.where(h, prefix, 0.0), axis=1, keepdims=True) for h in hots]
    size = jnp.sum(chosen, axis=0, keepdims=True).astype(jnp.int32)
    size = (size + (ROW_ALIGN - 1)) // ROW_ALIGN * ROW_ALIGN
    size_ref[0] = size
    base_ref[0] = run_ref[...]
    run_ref[...] = run_ref[...] + size
    cnt_ref[...] = run_ref[...]

    l4 = lax.broadcasted_iota(jnp.int32, (t, TOP_K), 1)

    def cols(parts):
        out = parts[TOP_K - 1]
        for k in range(TOP_K - 2, -1, -1):
            out = jnp.where(l4 == k, parts[k], out)
        return out

    idx_ref[...] = cols(sels)
    wts_ref[...] = cols([v / den for v in ex])
    rank_ref[...] = cols(ranks).astype(jnp.int32)


def _outproj(oa, ob, x2d, mod3, gga, ggb, gpost, gpre, w_out, w_router, b_router, *, seq):
    t, d = x2d.shape
    tile = OUT_TILE
    per_seq = seq // tile
    row = lambda i: (i, 0)
    const = lambda i: (0, 0)
    modspec = lambda part: pl.BlockSpec((1, 1, d), lambda i: (i // per_seq, 0, part))
    outs = pl.pallas_call(
        _outproj_kernel,
        grid=(t // tile,),
        in_specs=[pl.BlockSpec((tile, WIDTH_A), row), pl.BlockSpec((tile, WIDTH_B), row),
                  pl.BlockSpec((tile, d), row),
                  modspec(2), modspec(3), modspec(4),
                  pl.BlockSpec((1, WIDTH_A), const), pl.BlockSpec((1, WIDTH_B), const),
                  pl.BlockSpec((1, d), const), pl.BlockSpec((1, d), const),
                  pl.BlockSpec(w_out.shape, const), pl.BlockSpec(w_router.shape, const),
                  pl.BlockSpec((1, N_EXPERTS), const)],
        out_specs=[pl.BlockSpec((tile, d), row), pl.BlockSpec((tile, d), row),
                   pl.BlockSpec((tile, TOP_K), row), pl.BlockSpec((tile, TOP_K), row),
                   pl.BlockSpec((tile, TOP_K), row),
                   pl.BlockSpec((1, 1, N_EXPERTS), lambda i: (i, 0, 0)),
                   pl.BlockSpec((1, 1, N_EXPERTS), lambda i: (i, 0, 0)),
                   pl.BlockSpec((1, N_EXPERTS), const)],
        out_shape=[jax.ShapeDtypeStruct((t, d), jnp.float32), jax.ShapeDtypeStruct((t, d), jnp.bfloat16),
                   jax.ShapeDtypeStruct((t, TOP_K), jnp.int32), jax.ShapeDtypeStruct((t, TOP_K), jnp.float32),
                   jax.ShapeDtypeStruct((t, TOP_K), jnp.int32),
                   jax.ShapeDtypeStruct((t // tile, 1, N_EXPERTS), jnp.int32),
                   jax.ShapeDtypeStruct((t // tile, 1, N_EXPERTS), jnp.int32),
                   jax.ShapeDtypeStruct((1, N_EXPERTS), jnp.int32)],
        scratch_shapes=[pltpu.VMEM((1, N_EXPERTS), jnp.int32)],
        compiler_params=_params("arbitrary"),
        name="outproj_router",
    )(oa, ob, x2d, mod3, mod3, mod3, gga, ggb, gpost, gpre, w_out, w_router, b_router)
    return outs


def _piece_table(tab_ref, e):
    return (pl.multiple_of(tab_ref[e], ROW_ALIGN), pl.multiple_of(tab_ref[N_EXPERTS + e], ROW_ALIGN),
            pl.multiple_of(tab_ref[2 * N_EXPERTS + e], ROW_ALIGN))


def _dispatch_kernel(fill_from_ref, pad_end_ref, tab_ref, h_ref, lpos_ref, w_ref, xs_ref,
                     stage_ref, zero_ref, pend_ref, sems, zsem):
    i = pl.program_id(0)
    n_steps = pl.num_programs(0)
    slot = i % 2
    n_local, width = stage_ref.shape[1:]
    d = h_ref.shape[1]

    def fill(row):
        return pltpu.make_async_copy(zero_ref, xs_ref.at[pl.ds(pl.multiple_of(row, FFN_TILE), FFN_TILE)], zsem)

    def fills(e):
        return (pad_end_ref[e] - fill_from_ref[e]) // FFN_TILE

    def drain(s):
        n = pl.multiple_of(pend_ref[s], ROW_ALIGN)

        @pl.when(n > 0)
        def _():
            pltpu.make_async_copy(stage_ref.at[s, pl.ds(0, n)], xs_ref.at[pl.ds(0, n)], sems.at[s]).wait()
        pend_ref[s] = 0

    @pl.when(i == 0)
    def _():
        zero_ref[...] = jnp.zeros_like(zero_ref)
        pend_ref[0] = 0
        pend_ref[1] = 0

        def start_e(e, c):
            def start_j(j, c2):
                fill(fill_from_ref[e] + j * FFN_TILE).start()
                return c2
            lax.fori_loop(0, fills(e), start_j, 0)
            return c + fills(e)
        n_fill = lax.fori_loop(0, N_EXPERTS, start_e, 0)

        def wait_all(j, c):
            fill(0).wait()
            return c
        lax.fori_loop(0, n_fill, wait_all, 0)

    row = lax.broadcasted_iota(jnp.int32, (n_local, h_ref.shape[0]), 0)
    place = None
    row_w = None
    for k in range(TOP_K):
        hit = row == lpos_ref[0, k:k + 1, :]
        place = hit if place is None else place | hit
        wk = jnp.sum(jnp.where(hit, w_ref[0, k:k + 1, :], 0.0), axis=1, keepdims=True)
        row_w = wk if row_w is None else row_w + wk
    rows = _dot(place.astype(jnp.bfloat16), h_ref[...])

    drain(slot)
    stage_ref[slot, :, :d] = rows
    stage_ref[slot, :, d:] = jnp.broadcast_to(row_w, (n_local, width - d))

    def piece(e, total):
        loc, glob, n = _piece_table(tab_ref, e)

        @pl.when(n > 0)
        def _():
            pltpu.make_async_copy(stage_ref.at[slot, pl.ds(loc, n)], xs_ref.at[pl.ds(glob, n)], sems.at[slot]).start()
        return total + n
    pend_ref[slot] = lax.fori_loop(0, N_EXPERTS, piece, 0)

    tail0 = pad_end_ref[N_EXPERTS - 1]
    n_tail = (xs_ref.shape[0] - tail0) // FFN_TILE
    per_step = (n_tail + n_steps - 1) // n_steps

    def start_tail(j, c):
        t = i * per_step + j

        @pl.when(t < n_tail)
        def _():
            fill(tail0 + t * FFN_TILE).start()
        return c
    lax.fori_loop(0, per_step, start_tail, 0)

    @pl.when(i == n_steps - 1)
    def _():
        drain(0)
        drain(1)

        def wait_tail(j, c):
            fill(0).wait()
            return c
        lax.fori_loop(0, n_tail, wait_tail, 0)


def _dispatch(h2, tab_flat, lpos_t, w_t, fill_from, pad_end, *, n_rows):
    t, d = h2.shape
    tile = DISPATCH_TILE
    n_local = tile * TOP_K + N_EXPERTS * ROW_ALIGN
    width = d + LANES
    grid_spec = pltpu.PrefetchScalarGridSpec(
        num_scalar_prefetch=2,
        grid=(t // tile,),
        in_specs=[pl.BlockSpec((TAB_WIDTH,), lambda i, *_: (i,), memory_space=pltpu.SMEM),
                  pl.BlockSpec((tile, d), lambda i, *_: (i, 0)),
                  pl.BlockSpec((1, TOP_K, tile), lambda i, *_: (i, 0, 0)),
                  pl.BlockSpec((1, TOP_K, tile), lambda i, *_: (i, 0, 0))],
        out_specs=pl.BlockSpec(memory_space=pl.ANY),
        scratch_shapes=[pltpu.VMEM((2, n_local, width), jnp.float32),
                        pltpu.VMEM((FFN_TILE, width), jnp.float32),
                        pltpu.SMEM((2,), jnp.int32),
                        pltpu.SemaphoreType.DMA((2,)), pltpu.SemaphoreType.DMA(())],
    )
    return pl.pallas_call(
        _dispatch_kernel,
        grid_spec=grid_spec,
        out_shape=jax.ShapeDtypeStruct((n_rows, width), jnp.float32),
        compiler_params=_params("arbitrary"),
        name="dispatch",
    )(fill_from, pad_end, tab_flat, h2, lpos_t, w_t)


def _ffn_kernel(te_ref, rows_ref, n_used_ref, x_ref, w1_ref, b1_ref, w2_ref, b2_ref, y_ref,
                w1p_ref, w2b_ref, act_ref):
    i = pl.program_id(0)
    live = i < n_used_ref[0]
    n_pair = w1_ref.shape[2] // FFN_CHUNK
    half = FFN_CHUNK // 2
    d = y_ref.shape[1]
    n_sub = jnp.where(live, (rows_ref[i] + FFN_TILE - 1) // FFN_TILE, 0)

    @pl.when(live & ((i == 0) | (te_ref[i] != te_ref[jnp.maximum(i - 1, 0)])))
    def _():
        src = lax.broadcasted_iota(jnp.int32, (FFN_CHUNK, FFN_CHUNK), 0)
        dst = lax.broadcasted_iota(jnp.int32, (FFN_CHUNK, FFN_CHUNK), 1)
        unzip = (src == jnp.where(dst < half, 2 * dst, 2 * (dst - half) + 1)).astype(jnp.bfloat16)
        for c in range(n_pair):
            cs = slice(c * FFN_CHUNK, (c + 1) * FFN_CHUNK)
            w1p_ref[:, cs] = _dot(w1_ref[0, :, cs].astype(jnp.bfloat16), unzip).astype(jnp.bfloat16)
        w2b_ref[...] = w2_ref[0].astype(jnp.bfloat16)

    def tile(s, carry):
        rows = pl.ds(pl.multiple_of(s * FFN_TILE, FFN_TILE), FFN_TILE)
        x = x_ref[rows, :d].astype(jnp.bfloat16)
        for c in range(0, n_pair, 2):
            cs = slice(c * FFN_CHUNK, (c + 2) * FFN_CHUNK)
            h = _dot(x, w1p_ref[:, cs]) + b1_ref[0, :, cs]
            g = jnp.concatenate([h[:, :half], h[:, FFN_CHUNK:FFN_CHUNK + half]], axis=1)
            u = jnp.concatenate([h[:, half:FFN_CHUNK], h[:, FFN_CHUNK + half:]], axis=1)
            g = jnp.minimum(g, SWIGLU_LIMIT)
            u = jnp.clip(u, -SWIGLU_LIMIT, SWIGLU_LIMIT)
            act = g * jax.nn.sigmoid(SWIGLU_ALPHA * g) * (u + 1.0)
            act_ref[:, c * half:(c + 2) * half] = act.astype(act_ref.dtype)
        y_ref[rows, :] = (_dot(act_ref[...], w2b_ref[...]) + b2_ref[0]) * x_ref[rows, d:d + 1]
        return carry
    lax.fori_loop(0, n_sub, tile, 0)

    def blank(s, carry):
        y_ref[pl.ds(pl.multiple_of(s * FFN_TILE, FFN_TILE), FFN_TILE), :] = jnp.zeros((FFN_TILE, d), y_ref.dtype)
        return carry
    lax.fori_loop(n_sub, y_ref.shape[0] // FFN_TILE, blank, 0)


def _unzip_bias(b1):
    e, f2 = b1.shape
    half = FFN_CHUNK // 2
    return b1.reshape(e, f2 // FFN_CHUNK, half, 2).transpose(0, 1, 3, 2).reshape(e, 1, f2)


def _ffn(xs, step_expert, step_rows, n_used, w1, b1, w2, b2):
    rows, width = xs.shape
    d, f2 = w1.shape[1:]
    n_steps = rows // FFN_STEP
    live = lambda i, nu: jnp.minimum(i, nu[0] - 1)
    wspec = lambda shape: pl.BlockSpec((1,) + shape, lambda i, te, nr, nu: (te[live(i, nu)], 0, 0))
    grid_spec = pltpu.PrefetchScalarGridSpec(
        num_scalar_prefetch=3,
        grid=(n_steps,),
        in_specs=[pl.BlockSpec((FFN_STEP, width), lambda i, te, nr, nu: (live(i, nu), 0)),
                  wspec((d, f2)), wspec((1, f2)), wspec((f2 // 2, d)), wspec((1, d))],
        out_specs=pl.BlockSpec((FFN_STEP, d), lambda i, te, nr, nu: (i, 0)),
        scratch_shapes=[pltpu.VMEM((d, f2), jnp.bfloat16), pltpu.VMEM((f2 // 2, d), jnp.bfloat16),
                        pltpu.VMEM((FFN_TILE, f2 // 2), jnp.bfloat16)],
    )
    return pl.pallas_call(
        _ffn_kernel,
        grid_spec=grid_spec,
        out_shape=jax.ShapeDtypeStruct((rows, d), jnp.float32),
        compiler_params=_params("arbitrary"),
        name="expert_ffn",
    )(step_expert, step_rows, n_used, xs, w1, _unzip_bias(b1), w2, b2[:, None, :])


def _combine_kernel(tab_ref, next_tab_ref, lpos_ref, x1_ref, ga_ref, g_ref, y_ref, o_ref, buf_ref, sems):
    i = pl.program_id(0)
    n_steps = pl.num_programs(0)
    slot = i % 2
    tile = x1_ref.shape[0]
    n_local = buf_ref.shape[1]

    def fetch(tab, s):
        def piece(e, c):
            loc, glob, n = _piece_table(tab, e)

            @pl.when(n > 0)
            def _():
                pltpu.make_async_copy(y_ref.at[pl.ds(glob, n)], buf_ref.at[s, pl.ds(loc, n)], sems.at[s]).start()
            return c
        lax.fori_loop(0, N_EXPERTS, piece, 0)

    @pl.when(i == 0)
    def _():
        buf_ref[...] = jnp.zeros_like(buf_ref)
        fetch(tab_ref, 0)

    @pl.when(i + 1 < n_steps)
    def _():
        fetch(next_tab_ref, 1 - slot)

    n = pl.multiple_of(tab_ref[3 * N_EXPERTS], ROW_ALIGN)

    @pl.when(n > 0)
    def _():
        pltpu.make_async_copy(y_ref.at[pl.ds(0, n)], buf_ref.at[slot, pl.ds(0, n)], sems.at[slot]).wait()

    col = lax.broadcasted_iota(jnp.int32, (tile, n_local), 1)
    pos = lpos_ref[...]
    pick = col == pos[:, 0:1]
    for k in range(1, TOP_K):
        pick = pick | (col == pos[:, k:k + 1])
    pick = pick.astype(jnp.bfloat16)
    rows = buf_ref[slot]
    hi = rows.astype(jnp.bfloat16)
    lo = (rows - hi.astype(jnp.float32)).astype(jnp.bfloat16)
    y = _dot(pick, hi) + _dot(pick, lo)
    o_ref[...] = x1_ref[...] + ga_ref[0] * _rms(y, g_ref[...])


def _combine(ys, tab_flat, lpos, x1, mod3, g_post, *, seq):
    t, d = x1.shape
    tile = DISPATCH_TILE
    n_steps = t // tile
    per_seq = seq // tile
    n_local = tile * TOP_K + N_EXPERTS * ROW_ALIGN
    return pl.pallas_call(
        _combine_kernel,
        grid=(n_steps,),
        in_specs=[pl.BlockSpec((TAB_WIDTH,), lambda i: (i,), memory_space=pltpu.SMEM),
                  pl.BlockSpec((TAB_WIDTH,), lambda i: (jnp.minimum(i + 1, n_steps - 1),), memory_space=pltpu.SMEM),
                  pl.BlockSpec((tile, TOP_K), lambda i: (i, 0)),
                  pl.BlockSpec((tile, d), lambda i: (i, 0)),
                  pl.BlockSpec((1, 1, d), lambda i: (i // per_seq, 0, 5)),
                  pl.BlockSpec((1, d), lambda i: (0, 0)),
                  pl.BlockSpec(memory_space=pl.ANY)],
        out_specs=pl.BlockSpec((tile, d), lambda i: (i, 0)),
        out_shape=jax.ShapeDtypeStruct((t, d), jnp.float32),
        scratch_shapes=[pltpu.VMEM((2, n_local, d), ys.dtype), pltpu.SemaphoreType.DMA((2,))],
        compiler_params=_params("arbitrary"),
        name="combine",
    )(tab_flat, tab_flat, lpos, x1, mod3, g_post, ys)


def _rope_tables(seq):
    pos = np.arange(seq)
    n_freq = HEAD_DIM // 4
    freqs = ROPE_BASE ** (-jnp.arange(n_freq, dtype=jnp.float32) / n_freq)
    rows = jnp.asarray(pos // GRID_W, jnp.float32)[:, None] * freqs[None, :]
    cols = jnp.asarray(pos % GRID_W, jnp.float32)[:, None] * freqs[None, :]
    ang = jnp.concatenate([rows, rows, cols, cols], axis=1)
    sign = np.tile(np.repeat([-1.0, 1.0], n_freq), 2).astype(np.float32)
    cos = jnp.cos(ang)
    sin = jnp.sin(ang) * sign[None, :]
    reps = LANES // HEAD_DIM
    return jnp.tile(cos, (1, reps)), jnp.tile(sin, (1, reps))


def kernel(x, c, ctx, c_ctx, w_ada, b_ada, g_pre_mix, g_post_mix, g_pre_ffn, g_post_ffn, w_in, g_grp_a, g_grp_b,
           sink_a, rpb_b, w_out, w_router, b_router, w_mlp1, b_mlp1, w_mlp2, b_mlp2):
    batch, seq, d = x.shape
    n_ctx = ctx.shape[1]
    assert w_ada.shape[0] == 1, "single layer"
    assert seq % (ATT_BLOCKS * GRID_W * NB_Q_ROWS) == 0 and seq // GRID_W >= NB_K_ROWS
    assert seq % (ATT_BLOCKS * BLOCK_A) == 0 and seq >= BLOCK_A + 2 * WINDOW
    n_tok = batch * seq
    bf16 = jnp.bfloat16

    head_order = np.asarray([k * GQA_GROUP + j for j in range(GQA_GROUP) for k in range(N_KV_A)])
    col_order = (head_order[:, None] * HEAD_DIM + np.arange(HEAD_DIM)[None, :]).reshape(-1)

    mod_rows = -(-(batch + 1) // 8) * 8
    cc = jnp.zeros((mod_rows, d), jnp.float32).at[:batch].set(c).at[batch].set(c_ctx)
    mod3 = _ada(cc, w_ada[0], b_ada[0]).reshape(mod_rows, 1, 6 * d)

    w_in0 = w_in[0]
    w_lat = jnp.concatenate([w_in0[:, :WIDTH_A][:, col_order], w_in0[:, WIDTH_A:]], axis=1).astype(bf16)
    kv_cols = np.concatenate([np.arange(WIDTH_A, WIDTH_A + 2 * WIDTH_KV_A),
                              np.arange(WIDTH_A + 2 * WIDTH_KV_A + WIDTH_B, w_in0.shape[1])])
    w_ctx = w_in0[:, kv_cols].astype(bf16)
    g_pre = g_pre_mix[0].reshape(1, d)
    qa, ka, va, qb, kb, vb = _inproj(x.reshape(n_tok, d), mod3, g_pre, w_lat, _rope_tables(seq),
                                     seq=seq, mod_row0=0, latent=True)
    kac, vac, kbc, vbc = _inproj(ctx.reshape(batch * n_ctx, d), mod3, g_pre, w_ctx, None,
                                 seq=n_ctx, mod_row0=batch, latent=False)

    oa = _window_attention(qa, ka.reshape(batch, seq, -1), va.reshape(batch, seq, -1),
                           kac.reshape(batch, n_ctx, -1), vac.reshape(batch, n_ctx, -1),
                           sink_a[0].astype(jnp.float32) * LOG2E, batch=batch, seq=seq)
    tabs, cls_of_block = _nb_bias_tables(rpb_b[0], seq // GRID_W)
    ob = _neighbourhood_attention(qb, kb.reshape(batch, seq, -1), vb.reshape(batch, seq, -1),
                                  kbc.reshape(batch, n_ctx, -1), vbc.reshape(batch, n_ctx, -1),
                                  tabs, cls_of_block, batch=batch, seq=seq)

    w_out0 = w_out[0]
    w_o = jnp.concatenate([w_out0[:WIDTH_A][col_order], w_out0[WIDTH_A:]], axis=0).astype(bf16)
    x1, h2, idx, wts, rank, size, base, counts = _outproj(
        oa, ob, x.reshape(n_tok, d), mod3,
        g_grp_a[0][col_order].reshape(1, -1), g_grp_b[0].reshape(1, -1),
        g_post_mix[0].reshape(1, d), g_pre_ffn[0].reshape(1, d),
        w_o, w_router[0], b_router[0].reshape(1, -1), seq=seq)

    n_tok_tiles = n_tok // DISPATCH_TILE
    size = size.reshape(n_tok_tiles, N_EXPERTS)
    counts = counts.reshape(-1)
    padded = (counts + FFN_STEP - 1) // FFN_STEP * FFN_STEP
    pad_end = jnp.cumsum(padded).astype(jnp.int32)
    pad_start = pad_end - padded
    n_steps = (n_tok * TOP_K + n_tok_tiles * N_EXPERTS * (ROW_ALIGN - 1)) // FFN_STEP + N_EXPERTS
    n_used = (pad_end[-1:] // FFN_STEP).astype(jnp.int32)
    step_row0 = jnp.arange(n_steps, dtype=jnp.int32) * FFN_STEP
    step_expert = jnp.minimum(jnp.sum(step_row0[:, None] >= pad_end[None, :], axis=1),
                              N_EXPERTS - 1).astype(jnp.int32)
    own = step_expert[:, None] == jnp.arange(N_EXPERTS, dtype=jnp.int32)
    step_rows = jnp.clip(jnp.sum(jnp.where(own, (pad_start + counts)[None, :], 0), axis=1) - step_row0,
                         0, FFN_STEP).astype(jnp.int32)
    fill_from = (pad_start + jnp.maximum((counts + FFN_TILE - 1) // FFN_TILE * FFN_TILE - FFN_TILE, 0)
                 ).astype(jnp.int32)
    local0 = jnp.cumsum(size, axis=1) - size
    global0 = pad_start[None, :] + base.reshape(n_tok_tiles, N_EXPERTS)
    tab = jnp.concatenate([local0, global0, size, jnp.sum(size, axis=1, keepdims=True),
                           jnp.zeros((n_tok_tiles, TAB_WIDTH - 3 * N_EXPERTS - 1), jnp.int32)], axis=1)
    tab = tab.reshape(-1).astype(jnp.int32)
    hot = idx.reshape(n_tok_tiles, DISPATCH_TILE, TOP_K, 1) == jnp.arange(N_EXPERTS, dtype=jnp.int32)
    lpos = rank.reshape(n_tok_tiles, DISPATCH_TILE, TOP_K) + jnp.sum(
        jnp.where(hot, local0[:, None, None, :], 0), axis=-1)
    lpos_t = jnp.swapaxes(lpos, 1, 2)
    w_t = jnp.swapaxes(wts.reshape(n_tok_tiles, DISPATCH_TILE, TOP_K), 1, 2)

    xs = _dispatch(h2, tab, lpos_t, w_t, fill_from, pad_end, n_rows=n_steps * FFN_STEP)
    ys = _ffn(xs, step_expert, step_rows, n_used, w_mlp1[0], b_mlp1[0], w_mlp2[0], b_mlp2[0])
    out = _combine(ys, tab, lpos.reshape(n_tok, TOP_K), x1, mod3, g_post_ffn[0].reshape(1, d), seq=seq)
    return out.reshape(batch, seq, d)
```

```python
import functools

import numpy as np
import jax
import jax.numpy as jnp
from jax import lax
from jax.experimental import pallas as pl
from jax.experimental.pallas import tpu as pltpu

GRID_W = 64
HEAD_DIM = 64
N_HEADS_A = 8
N_KV_A = 2
GQA_GROUP = N_HEADS_A // N_KV_A
N_HEADS_B = 8
WIDTH_A = N_HEADS_A * HEAD_DIM
WIDTH_KV_A = N_KV_A * HEAD_DIM
WIDTH_B = N_HEADS_B * HEAD_DIM
WINDOW = 128
BLOCK_A = 128
NA_ROWS = 8
NA_COLS = 16
N_EXPERTS = 32
TOP_K = 4
SWIGLU_LIMIT = 7.0
SWIGLU_ALPHA = 1.702
ROPE_BASE = 10000.0
EPS = 1e-6
NEG_INF = -1e30

LANES = 128
VMEM_LIMIT = 56 * 1024 * 1024

LOG2E = 1.4426950408889634
ATT_BLOCKS = 2
NB_Q_ROWS = 2
NB_K_ROWS = NB_Q_ROWS + NA_ROWS
PROJ_TILE = 512
OUT_TILE = 256
FFN_TILE = 512
FFN_STEP = 2 * FFN_TILE
FFN_CHUNK = 256
DISPATCH_TILE = OUT_TILE
ROW_ALIGN = 8
TAB_WIDTH = 128


def _params(*sem):
    return pltpu.CompilerParams(dimension_semantics=sem, vmem_limit_bytes=VMEM_LIMIT)


def _rms(x, g):
    return x * lax.rsqrt(jnp.mean(x * x, axis=-1, keepdims=True) + EPS) * g


def _dot(a, b):
    return jnp.dot(a, b, preferred_element_type=jnp.float32)


def _dot_nt(a, b):
    return lax.dot_general(a, b, (((1,), (1,)), ((), ())), preferred_element_type=jnp.float32)


def _ada_kernel(c_ref, w_ref, b_ref, o_ref):
    c = c_ref[...]
    s = (c * jax.nn.sigmoid(c)).astype(jnp.bfloat16)
    o_ref[...] = _dot(s, w_ref[...].astype(jnp.bfloat16)) + b_ref[...]


def _ada(cc, w_ada, b_ada):
    rows, d = cc.shape
    n_out = w_ada.shape[1]
    return pl.pallas_call(
        _ada_kernel,
        grid=(n_out // d,),
        in_specs=[pl.BlockSpec((rows, d), lambda j: (0, 0)),
                  pl.BlockSpec((d, d), lambda j: (0, j)),
                  pl.BlockSpec((1, d), lambda j: (0, j))],
        out_specs=pl.BlockSpec((rows, d), lambda j: (0, j)),
        out_shape=jax.ShapeDtypeStruct((rows, n_out), jnp.float32),
        compiler_params=_params("arbitrary"),
        name="ada",
    )(cc, w_ada, b_ada.reshape(1, n_out))


def _rope(x, cos, sin):
    w = x.shape[1]
    reps = w // LANES
    if reps > 1:
        cos = jnp.concatenate([cos] * reps, axis=1)
        sin = jnp.concatenate([sin] * reps, axis=1)
    lane = lax.broadcasted_iota(jnp.int32, x.shape, 1)
    quarter = HEAD_DIM // 4
    partner = jnp.where(lane % (2 * quarter) < quarter,
                        pltpu.roll(x, w - quarter, 1), pltpu.roll(x, quarter, 1))
    return x * cos + partner * sin


def _inproj_kernel(x_ref, sh_ref, sc_ref, g_ref, w_ref, *rest, latent):
    x = x_ref[...]
    h = _rms(x, g_ref[...]) * (1.0 + sc_ref[0]) + sh_ref[0]
    p = _dot(h.astype(jnp.bfloat16), w_ref[...])
    if latent:
        cos_ref, sin_ref, qa_ref, ka_ref, va_ref, qb_ref, kb_ref, vb_ref = rest
        cos, sin = cos_ref[...], sin_ref[...]
        scale = HEAD_DIM ** -0.5 * LOG2E
        o = 0
        qa_ref[...] = (_rope(p[:, o:o + WIDTH_A], cos, sin) * scale).astype(qa_ref.dtype)
        o += WIDTH_A
        ka_ref[...] = _rope(p[:, o:o + WIDTH_KV_A], cos, sin).astype(ka_ref.dtype)
        o += WIDTH_KV_A
        va_ref[...] = p[:, o:o + WIDTH_KV_A].astype(va_ref.dtype)
        o += WIDTH_KV_A
        qb_ref[...] = (p[:, o:o + WIDTH_B] * scale).astype(qb_ref.dtype)
        o += WIDTH_B
    else:
        ka_ref, va_ref, kb_ref, vb_ref = rest
        o = 0
        ka_ref[...] = p[:, o:o + WIDTH_KV_A].astype(ka_ref.dtype)
        o += WIDTH_KV_A
        va_ref[...] = p[:, o:o + WIDTH_KV_A].astype(va_ref.dtype)
        o += WIDTH_KV_A
    kb_ref[...] = p[:, o:o + WIDTH_B].astype(kb_ref.dtype)
    o += WIDTH_B
    vb_ref[...] = p[:, o:o + WIDTH_B].astype(vb_ref.dtype)


def _inproj(x2d, mod3, g_pre, w, rope_tabs, *, seq, mod_row0, latent):
    t, d = x2d.shape
    tile = min(PROJ_TILE, seq)
    per_seq = seq // tile
    if latent:
        mod_row = lambda i: i // per_seq
    else:
        mod_row = lambda i: mod_row0
    in_specs = [pl.BlockSpec((tile, d), lambda i: (i, 0)),
                pl.BlockSpec((1, 1, d), lambda i: (mod_row(i), 0, 0)),
                pl.BlockSpec((1, 1, d), lambda i: (mod_row(i), 0, 1)),
                pl.BlockSpec((1, d), lambda i: (0, 0)),
                pl.BlockSpec(w.shape, lambda i: (0, 0))]
    args = [x2d, mod3, mod3, g_pre, w]
    widths = [WIDTH_KV_A, WIDTH_KV_A, WIDTH_B, WIDTH_B]
    if latent:
        in_specs += [pl.BlockSpec((tile, LANES), lambda i: (i % per_seq, 0))] * 2
        args += list(rope_tabs)
        widths = [WIDTH_A, WIDTH_KV_A, WIDTH_KV_A, WIDTH_B, WIDTH_B, WIDTH_B]
    return pl.pallas_call(
        functools.partial(_inproj_kernel, latent=latent),
        grid=(t // tile,),
        in_specs=in_specs,
        out_specs=[pl.BlockSpec((tile, wd), lambda i: (i, 0)) for wd in widths],
        out_shape=[jax.ShapeDtypeStruct((t, wd), jnp.bfloat16) for wd in widths],
        compiler_params=_params("parallel"),
        name="inproj_latent" if latent else "inproj_ctx",
    )(*args)


def _ones_beside(v, half):
    low = lax.broadcasted_iota(jnp.int32, v.shape, 1) < HEAD_DIM
    return jnp.where(low if half == 0 else ~low, v, jnp.ones_like(v))


def _pair_attention(q_ref, o_ref, rows, keys, values, sink_of, s_ref, p_ref, m_ref):
    tq = rows.stop - rows.start
    low = lax.broadcasted_iota(jnp.int32, (tq, LANES), 1) < HEAD_DIM
    n_pair = q_ref.shape[1] // LANES
    n_loc = None
    for j in range(n_pair):
        q = q_ref[rows, j * LANES:(j + 1) * LANES]
        for half in range(2):
            k_loc, k_ctx, bias = keys(j, half)
            n_loc = k_loc.shape[0]
            qm = jnp.where(low if half == 0 else ~low, q, jnp.zeros_like(q))
            s_ref[2 * j + half, :, :n_loc] = _dot_nt(qm, k_loc) + bias
            s_ref[2 * j + half, :, n_loc:] = _dot_nt(qm, k_ctx)
    for h in range(2 * n_pair):
        s = s_ref[h]
        m = jnp.max(s, axis=1, keepdims=True)
        sink = sink_of(h // 2, h % 2)
        if sink is not None:
            m = jnp.maximum(m, sink)
        p_ref[h] = jnp.exp2((s - m).astype(jnp.bfloat16))
        m_ref[h] = m
    for j in range(n_pair):
        acc = []
        for half in range(2):
            v_loc, v_ctx = values(j, half)
            h = 2 * j + half
            acc.append(_dot(p_ref[h, :, :n_loc], v_loc) + _dot(p_ref[h, :, n_loc:], v_ctx))
        num = jnp.where(low, acc[0], acc[1])
        den = pltpu.roll(jnp.where(low, acc[1], acc[0]), HEAD_DIM, 1)
        if sink_of(j, 0) is not None:
            den = den + jnp.where(low, jnp.exp2(sink_of(j, 0) - m_ref[2 * j]), jnp.exp2(sink_of(j, 1) - m_ref[2 * j + 1]))
        o_ref[rows, j * LANES:(j + 1) * LANES] = (num / den).astype(o_ref.dtype)


def _attention_scratch(tq, n_heads, n_keys):
    return [pltpu.VMEM((n_heads, tq, n_keys), jnp.float32), pltpu.VMEM((n_heads, tq, n_keys), jnp.bfloat16),
            pltpu.VMEM((n_heads, tq, 1), jnp.float32)]


def _win_kernel(sink_ref, q_ref, k_ref, v_ref, kc_ref, vc_ref, o_ref, *scratch, seq):
    span = BLOCK_A + 2 * WINDOW
    k_ctx = kc_ref[0]
    v_ctx = [_ones_beside(vc_ref[0], half) for half in range(2)]
    for sb in range(ATT_BLOCKS):
        i = pl.program_id(1) * ATT_BLOCKS + sb
        start = pl.multiple_of(jnp.clip(i * BLOCK_A - WINDOW, 0, seq - span), BLOCK_A)
        k_loc = k_ref[0, pl.ds(start, span), :]
        v_loc = [_ones_beside(v_ref[0, pl.ds(start, span), :], half) for half in range(2)]
        qpos = i * BLOCK_A + lax.broadcasted_iota(jnp.int32, (BLOCK_A, span), 0)
        kpos = start + lax.broadcasted_iota(jnp.int32, (BLOCK_A, span), 1)
        bias = jnp.where(jnp.abs(kpos - qpos) <= WINDOW, 0.0, NEG_INF).astype(jnp.float32)

        _pair_attention(q_ref, o_ref, slice(sb * BLOCK_A, (sb + 1) * BLOCK_A),
                        lambda j, half: (k_loc, k_ctx, bias),
                        lambda j, half: (v_loc[half], v_ctx[half]),
                        lambda j, half: sink_ref[half * GQA_GROUP + j], *scratch)


def _window_attention(qa, ka, va, kac, vac, sink, *, batch, seq):
    tq = ATT_BLOCKS * BLOCK_A
    nb = seq // tq
    n_ctx = kac.shape[1]
    return pl.pallas_call(
        functools.partial(_win_kernel, seq=seq),
        grid=(batch, nb),
        in_specs=[pl.BlockSpec(memory_space=pltpu.SMEM),
                  pl.BlockSpec((tq, WIDTH_A), lambda b, i: (b * nb + i, 0)),
                  pl.BlockSpec((1, seq, WIDTH_KV_A), lambda b, i: (b, 0, 0)),
                  pl.BlockSpec((1, seq, WIDTH_KV_A), lambda b, i: (b, 0, 0)),
                  pl.BlockSpec((1, n_ctx, WIDTH_KV_A), lambda b, i: (b, 0, 0)),
                  pl.BlockSpec((1, n_ctx, WIDTH_KV_A), lambda b, i: (b, 0, 0))],
        out_specs=pl.BlockSpec((tq, WIDTH_A), lambda b, i: (b * nb + i, 0)),
        out_shape=jax.ShapeDtypeStruct((batch * seq, WIDTH_A), jnp.bfloat16),
        scratch_shapes=_attention_scratch(BLOCK_A, N_HEADS_A, BLOCK_A + 2 * WINDOW + n_ctx),
        compiler_params=_params("parallel", "arbitrary"),
        name="window_attention",
    )(sink, qa, ka, va, kac, vac)


def _nb_kernel(q_ref, k_ref, v_ref, kc_ref, vc_ref, *rest, rows_n):
    tab_refs, o_ref, scratch = rest[:ATT_BLOCKS], rest[ATT_BLOCKS], rest[ATT_BLOCKS + 1:]
    n_keys = NB_K_ROWS * GRID_W
    tq = NB_Q_ROWS * GRID_W
    for sb in range(ATT_BLOCKS):
        m = pl.program_id(1) * ATT_BLOCKS + sb
        start_row = jnp.clip(NB_Q_ROWS * m - NA_ROWS // 2, 0, rows_n - NB_K_ROWS)
        start = pl.multiple_of(start_row * GRID_W, LANES)
        tab_ref = tab_refs[sb]

        def keys(j, half):
            cols = slice(j * LANES, (j + 1) * LANES)
            return (k_ref[0, pl.ds(start, n_keys), cols], kc_ref[0, :, cols],
                    tab_ref[0, 2 * j + half].astype(jnp.float32))

        def values(j, half):
            cols = slice(j * LANES, (j + 1) * LANES)
            return (_ones_beside(v_ref[0, pl.ds(start, n_keys), cols], half), _ones_beside(vc_ref[0, :, cols], half))

        _pair_attention(q_ref, o_ref, slice(sb * tq, (sb + 1) * tq), keys, values, lambda j, half: None, *scratch)


def _nb_classes(rows_n):
    n_blocks = rows_n // NB_Q_ROWS
    sig = {}
    cls_of_block = []
    reps = []
    for m in range(n_blocks):
        start_row = int(np.clip(NB_Q_ROWS * m - NA_ROWS // 2, 0, rows_n - NB_K_ROWS))
        key = tuple((start_row - r, int(np.clip(r - NA_ROWS // 2, 0, rows_n - NA_ROWS)) - r)
                    for r in range(NB_Q_ROWS * m, NB_Q_ROWS * (m + 1)))
        if key not in sig:
            sig[key] = len(reps)
            reps.append(m)
        cls_of_block.append(sig[key])
    return np.asarray(cls_of_block, np.int32), reps


def _nb_bias_tables(rpb, rows_n):
    cls_of_block, reps = _nb_classes(rows_n)
    n_heads = rpb.shape[0]
    cq = np.arange(GRID_W)[:, None]
    ck = np.arange(GRID_W)[None, :]
    cs = np.clip(cq - NA_COLS // 2, 0, GRID_W - NA_COLS)
    col_ok = (ck >= cs) & (ck < cs + NA_COLS)
    pick = ((ck - cq + NA_COLS - 1)[None] == np.arange(2 * NA_COLS - 1)[:, None, None]) & col_ok[None]
    tiles = jnp.einsum('hrd,dqk->hrqk', rpb.astype(jnp.float32), jnp.asarray(pick, jnp.float32),
                       precision=lax.Precision.HIGHEST)
    tiles = jnp.where(jnp.asarray(col_ok)[None, None], tiles * LOG2E, NEG_INF).astype(jnp.bfloat16)
    blocked = jnp.full((n_heads, GRID_W, GRID_W), NEG_INF, jnp.bfloat16)
    tabs = []
    for m in reps:
        start_row = int(np.clip(NB_Q_ROWS * m - NA_ROWS // 2, 0, rows_n - NB_K_ROWS))
        q_rows = []
        for r in range(NB_Q_ROWS * m, NB_Q_ROWS * (m + 1)):
            rs = int(np.clip(r - NA_ROWS // 2, 0, rows_n - NA_ROWS))
            q_rows.append(jnp.concatenate(
                [tiles[:, krow - r + NA_ROWS - 1] if rs <= krow < rs + NA_ROWS else blocked
                 for krow in range(start_row, start_row + NB_K_ROWS)], axis=2))
        tabs.append(jnp.concatenate(q_rows, axis=1))
    return jnp.stack(tabs), cls_of_block


def _neighbourhood_attention(qb, kb, vb, kbc, vbc, tabs, cls_of_block, *, batch, seq):
    rows_n = seq // GRID_W
    tq = ATT_BLOCKS * NB_Q_ROWS * GRID_W
    nstep = seq // tq
    n_ctx = kbc.shape[1]

    def tab_spec(sb):
        return pl.BlockSpec((1,) + tabs.shape[1:], lambda b, m, c: (c[m * ATT_BLOCKS + sb], 0, 0, 0))

    grid_spec = pltpu.PrefetchScalarGridSpec(
        num_scalar_prefetch=1,
        grid=(batch, nstep),
        in_specs=[pl.BlockSpec((tq, WIDTH_B), lambda b, m, c: (b * nstep + m, 0)),
                  pl.BlockSpec((1, seq, WIDTH_B), lambda b, m, c: (b, 0, 0)),
                  pl.BlockSpec((1, seq, WIDTH_B), lambda b, m, c: (b, 0, 0)),
                  pl.BlockSpec((1, n_ctx, WIDTH_B), lambda b, m, c: (b, 0, 0)),
                  pl.BlockSpec((1, n_ctx, WIDTH_B), lambda b, m, c: (b, 0, 0))]
                 + [tab_spec(sb) for sb in range(ATT_BLOCKS)],
        out_specs=pl.BlockSpec((tq, WIDTH_B), lambda b, m, c: (b * nstep + m, 0)),
        scratch_shapes=_attention_scratch(NB_Q_ROWS * GRID_W, N_HEADS_B, NB_K_ROWS * GRID_W + n_ctx),
    )

    def body(c_ref, *refs):
        _nb_kernel(*refs, rows_n=rows_n)

    return pl.pallas_call(
        body,
        grid_spec=grid_spec,
        out_shape=jax.ShapeDtypeStruct((batch * seq, WIDTH_B), jnp.bfloat16),
        compiler_params=_params("parallel", "arbitrary"),
        name="neighbourhood_attention",
    )(jnp.asarray(cls_of_block), qb, kb, vb, kbc, vbc, *([tabs] * ATT_BLOCKS))


def _outproj_kernel(oa_ref, ob_ref, x_ref, ga_ref, sh_ref, sc_ref, gga_ref, ggb_ref, gpost_ref, gpre_ref,
                    wo_ref, wr_hi_ref, wr_lo_ref, br_ref,
                    x1_ref, h2_ref, idx_ref, wts_ref, rank_ref, size_ref, base_ref, cnt_ref, run_ref):
    i = pl.program_id(0)

    @pl.when(i == 0)
    def _():
        run_ref[...] = jnp.zeros_like(run_ref)

    na = _rms(oa_ref[...].astype(jnp.float32), gga_ref[...]).astype(jnp.bfloat16)
    nb = _rms(ob_ref[...].astype(jnp.float32), ggb_ref[...]).astype(jnp.bfloat16)
    mix = _dot(na, wo_ref[:WIDTH_A, :]) + _dot(nb, wo_ref[WIDTH_A:, :])
    x1 = x_ref[...] + ga_ref[0] * _rms(mix, gpost_ref[...])
    x1_ref[...] = x1
    h2 = _rms(x1, gpre_ref[...]) * (1.0 + sc_ref[0]) + sh_ref[0]
    h2_ref[...] = h2.astype(h2_ref.dtype)

    h_hi = h2.astype(jnp.bfloat16)
    h_lo = (h2 - h_hi.astype(jnp.float32)).astype(jnp.bfloat16)
    logits = _dot(h_hi, wr_hi_ref[...]) + _dot(h_lo, wr_hi_ref[...]) + _dot(h_hi, wr_lo_ref[...])
    work = logits.T[:N_EXPERTS] + br_ref[...]
    e, t = work.shape
    row = lax.broadcasted_iota(jnp.int32, (e, t), 0)
    chosen = jnp.zeros((e, t), jnp.float32)
    vals, sels, hots = [], [], []
    for _k in range(TOP_K):
        mx = jnp.max(work, axis=0, keepdims=True)
        sel = jnp.min(jnp.where(work == mx, row, e), axis=0, keepdims=True)
        hot = row == sel
        vals.append(mx)
        sels.append(sel)
        hots.append(hot)
        work = jnp.where(hot, -jnp.inf, work)
        chosen = chosen + hot.astype(jnp.float32)
    ex = [jnp.exp(v - vals[0]) for v in vals]
    den = ex[0] + ex[1] + ex[2] + ex[3]

    earlier = (lax.broadcasted_iota(jnp.int32, (t, t), 0) < lax.broadcasted_iota(jnp.int32, (t, t), 1))
    prefix = _dot(chosen.astype(jnp.bfloat16), earlier.astype(jnp.bfloat16))
    ranks = [jnp.sum(jnp.where(h, prefix, 0.0), axis=0, keepdims=True) for h in hots]
    size = jnp.sum(chosen, axis=1, keepdims=True).astype(jnp.int32)
    size = (size + (ROW_ALIGN - 1)) // ROW_ALIGN * ROW_ALIGN
    size_ref[0] = size
    base_ref[0] = run_ref[...]
    run_ref[...] = run_ref[...] + size
    cnt_ref[...] = run_ref[...]

    idx_ref[0] = jnp.concatenate(sels, axis=0)
    wts_ref[0] = jnp.concatenate([v / den for v in ex], axis=0)
    rank_ref[0] = jnp.concatenate(ranks, axis=0).astype(jnp.int32)


def _outproj(oa, ob, x2d, mod3, gga, ggb, gpost, gpre, w_out, w_router, b_router, *, seq):
    t, d = x2d.shape
    tile = OUT_TILE
    per_seq = seq // tile
    row = lambda i: (i, 0)
    const = lambda i: (0, 0)
    modspec = lambda part: pl.BlockSpec((1, 1, d), lambda i: (i // per_seq, 0, part))
    per_tile = lambda shape: pl.BlockSpec((1,) + shape, lambda i: (i, 0, 0))
    n_steps = t // tile
    w_pad = jnp.zeros((d, LANES), jnp.float32).at[:, :N_EXPERTS].set(w_router)
    wr_hi = w_pad.astype(jnp.bfloat16)
    wr_lo = (w_pad - wr_hi.astype(jnp.float32)).astype(jnp.bfloat16)
    outs = pl.pallas_call(
        _outproj_kernel,
        grid=(n_steps,),
        in_specs=[pl.BlockSpec((tile, WIDTH_A), row), pl.BlockSpec((tile, WIDTH_B), row),
                  pl.BlockSpec((tile, d), row),
                  modspec(2), modspec(3), modspec(4),
                  pl.BlockSpec((1, WIDTH_A), const), pl.BlockSpec((1, WIDTH_B), const),
                  pl.BlockSpec((1, d), const), pl.BlockSpec((1, d), const),
                  pl.BlockSpec(w_out.shape, const), pl.BlockSpec((d, LANES), const), pl.BlockSpec((d, LANES), const),
                  pl.BlockSpec((N_EXPERTS, 1), const)],
        out_specs=[pl.BlockSpec((tile, d), row), pl.BlockSpec((tile, d), row),
                   per_tile((TOP_K, tile)), per_tile((TOP_K, tile)), per_tile((TOP_K, tile)),
                   per_tile((N_EXPERTS, 1)), per_tile((N_EXPERTS, 1)),
                   pl.BlockSpec((N_EXPERTS, 1), const)],
        out_shape=[jax.ShapeDtypeStruct((t, d), jnp.float32), jax.ShapeDtypeStruct((t, d), jnp.bfloat16),
                   jax.ShapeDtypeStruct((n_steps, TOP_K, tile), jnp.int32),
                   jax.ShapeDtypeStruct((n_steps, TOP_K, tile), jnp.float32),
                   jax.ShapeDtypeStruct((n_steps, TOP_K, tile), jnp.int32),
                   jax.ShapeDtypeStruct((n_steps, N_EXPERTS, 1), jnp.int32),
                   jax.ShapeDtypeStruct((n_steps, N_EXPERTS, 1), jnp.int32),
                   jax.ShapeDtypeStruct((N_EXPERTS, 1), jnp.int32)],
        scratch_shapes=[pltpu.VMEM((N_EXPERTS, 1), jnp.int32)],
        compiler_params=_params("arbitrary"),
        name="outproj_router",
    )(oa, ob, x2d, mod3, mod3, mod3, gga, ggb, gpost, gpre, w_out, wr_hi, wr_lo, b_router.reshape(N_EXPERTS, 1))
    return outs


def _piece_table(tab_ref, e):
    return (pl.multiple_of(tab_ref[e], ROW_ALIGN), pl.multiple_of(tab_ref[N_EXPERTS + e], ROW_ALIGN),
            pl.multiple_of(tab_ref[2 * N_EXPERTS + e], ROW_ALIGN))


def _dispatch_kernel(fill_from_ref, pad_end_ref, tab_ref, h_ref, lpos_ref, w_ref, xs_ref,
                     stage_ref, zero_ref, pend_ref, sems, zsem):
    i = pl.program_id(0)
    n_steps = pl.num_programs(0)
    slot = i % 2
    n_local, width = stage_ref.shape[1:]
    d = h_ref.shape[1]

    def fill(row):
        return pltpu.make_async_copy(zero_ref, xs_ref.at[pl.ds(pl.multiple_of(row, FFN_TILE), FFN_TILE)], zsem)

    def fills(e):
        return (pad_end_ref[e] - fill_from_ref[e]) // FFN_TILE

    def drain(s):
        n = pl.multiple_of(pend_ref[s], ROW_ALIGN)

        @pl.when(n > 0)
        def _():
            pltpu.make_async_copy(stage_ref.at[s, pl.ds(0, n)], xs_ref.at[pl.ds(0, n)], sems.at[s]).wait()
        pend_ref[s] = 0

    @pl.when(i == 0)
    def _():
        zero_ref[...] = jnp.zeros_like(zero_ref)
        pend_ref[0] = 0
        pend_ref[1] = 0

        def start_e(e, c):
            def start_j(j, c2):
                fill(fill_from_ref[e] + j * FFN_TILE).start()
                return c2
            lax.fori_loop(0, fills(e), start_j, 0)
            return c + fills(e)
        n_fill = lax.fori_loop(0, N_EXPERTS, start_e, 0)

        def wait_all(j, c):
            fill(0).wait()
            return c
        lax.fori_loop(0, n_fill, wait_all, 0)

    row = lax.broadcasted_iota(jnp.int32, (n_local, h_ref.shape[0]), 0)
    place = None
    row_w = None
    for k in range(TOP_K):
        hit = row == lpos_ref[0, k:k + 1, :]
        place = hit if place is None else place | hit
        wk = jnp.sum(jnp.where(hit, w_ref[0, k:k + 1, :], 0.0), axis=1, keepdims=True)
        row_w = wk if row_w is None else row_w + wk
    rows = _dot(place.astype(jnp.bfloat16), h_ref[...])

    drain(slot)
    stage_ref[slot, :, :d] = rows
    stage_ref[slot, :, d:] = jnp.broadcast_to(row_w, (n_local, width - d))

    def piece(e, total):
        loc, glob, n = _piece_table(tab_ref, e)

        @pl.when(n > 0)
        def _():
            pltpu.make_async_copy(stage_ref.at[slot, pl.ds(loc, n)], xs_ref.at[pl.ds(glob, n)], sems.at[slot]).start()
        return total + n
    pend_ref[slot] = lax.fori_loop(0, N_EXPERTS, piece, 0)

    tail0 = pad_end_ref[N_EXPERTS - 1]
    n_tail = (xs_ref.shape[0] - tail0) // FFN_TILE
    per_step = (n_tail + n_steps - 1) // n_steps

    def start_tail(j, c):
        t = i * per_step + j

        @pl.when(t < n_tail)
        def _():
            fill(tail0 + t * FFN_TILE).start()
        return c
    lax.fori_loop(0, per_step, start_tail, 0)

    @pl.when(i == n_steps - 1)
    def _():
        drain(0)
        drain(1)

        def wait_tail(j, c):
            fill(0).wait()
            return c
        lax.fori_loop(0, n_tail, wait_tail, 0)


def _dispatch(h2, tab_flat, lpos_t, w_t, fill_from, pad_end, *, n_rows):
    t, d = h2.shape
    tile = DISPATCH_TILE
    n_local = tile * TOP_K + N_EXPERTS * ROW_ALIGN
    width = d + LANES
    grid_spec = pltpu.PrefetchScalarGridSpec(
        num_scalar_prefetch=2,
        grid=(t // tile,),
        in_specs=[pl.BlockSpec((TAB_WIDTH,), lambda i, *_: (i,), memory_space=pltpu.SMEM),
                  pl.BlockSpec((tile, d), lambda i, *_: (i, 0)),
                  pl.BlockSpec((1, TOP_K, tile), lambda i, *_: (i, 0, 0)),
                  pl.BlockSpec((1, TOP_K, tile), lambda i, *_: (i, 0, 0))],
        out_specs=pl.BlockSpec(memory_space=pl.ANY),
        scratch_shapes=[pltpu.VMEM((2, n_local, width), jnp.float32),
                        pltpu.VMEM((FFN_TILE, width), jnp.float32),
                        pltpu.SMEM((2,), jnp.int32),
                        pltpu.SemaphoreType.DMA((2,)), pltpu.SemaphoreType.DMA(())],
    )
    return pl.pallas_call(
        _dispatch_kernel,
        grid_spec=grid_spec,
        out_shape=jax.ShapeDtypeStruct((n_rows, width), jnp.float32),
        compiler_params=_params("arbitrary"),
        name="dispatch",
    )(fill_from, pad_end, tab_flat, h2, lpos_t, w_t)


def _ffn_kernel(te_ref, rows_ref, n_used_ref, x_ref, w1_ref, b1_ref, w2_ref, b2_ref, y_ref,
                w1p_ref, w2b_ref, act_ref):
    i = pl.program_id(0)
    live = i < n_used_ref[0]
    n_pair = w1_ref.shape[2] // FFN_CHUNK
    half = FFN_CHUNK // 2
    d = y_ref.shape[1]
    n_sub = jnp.where(live, (rows_ref[i] + FFN_TILE - 1) // FFN_TILE, 0)

    @pl.when(live & ((i == 0) | (te_ref[i] != te_ref[jnp.maximum(i - 1, 0)])))
    def _():
        src = lax.broadcasted_iota(jnp.int32, (FFN_CHUNK, FFN_CHUNK), 0)
        dst = lax.broadcasted_iota(jnp.int32, (FFN_CHUNK, FFN_CHUNK), 1)
        unzip = (src == jnp.where(dst < half, 2 * dst, 2 * (dst - half) + 1)).astype(jnp.bfloat16)
        for c in range(n_pair):
            cs = slice(c * FFN_CHUNK, (c + 1) * FFN_CHUNK)
            w1p_ref[:, cs] = _dot(w1_ref[0, :, cs].astype(jnp.bfloat16), unzip).astype(jnp.bfloat16)
        w2b_ref[...] = w2_ref[0].astype(jnp.bfloat16)

    def tile(s, carry):
        rows = pl.ds(pl.multiple_of(s * FFN_TILE, FFN_TILE), FFN_TILE)
        x = x_ref[rows, :d].astype(jnp.bfloat16)
        for c in range(0, n_pair, 2):
            cs = slice(c * FFN_CHUNK, (c + 2) * FFN_CHUNK)
            h = _dot(x, w1p_ref[:, cs]) + b1_ref[0, :, cs]
            g = jnp.concatenate([h[:, :half], h[:, FFN_CHUNK:FFN_CHUNK + half]], axis=1)
            u = jnp.concatenate([h[:, half:FFN_CHUNK], h[:, FFN_CHUNK + half:]], axis=1)
            g = jnp.minimum(g, SWIGLU_LIMIT)
            u = jnp.clip(u, -SWIGLU_LIMIT, SWIGLU_LIMIT)
            act = g * jax.nn.sigmoid(SWIGLU_ALPHA * g) * (u + 1.0)
            act_ref[:, c * half:(c + 2) * half] = act.astype(act_ref.dtype)
        y_ref[rows, :] = (_dot(act_ref[...], w2b_ref[...]) + b2_ref[0]) * x_ref[rows, d:d + 1]
        return carry
    lax.fori_loop(0, n_sub, tile, 0)

    def blank(s, carry):
        y_ref[pl.ds(pl.multiple_of(s * FFN_TILE, FFN_TILE), FFN_TILE), :] = jnp.zeros((FFN_TILE, d), y_ref.dtype)
        return carry
    lax.fori_loop(n_sub, y_ref.shape[0] // FFN_TILE, blank, 0)


def _unzip_bias(b1):
    e, f2 = b1.shape
    half = FFN_CHUNK // 2
    return b1.reshape(e, f2 // FFN_CHUNK, half, 2).transpose(0, 1, 3, 2).reshape(e, 1, f2)


def _ffn(xs, step_expert, step_rows, n_used, w1, b1, w2, b2):
    rows, width = xs.shape
    d, f2 = w1.shape[1:]
    n_steps = rows // FFN_STEP
    live = lambda i, nu: jnp.minimum(i, nu[0] - 1)
    wspec = lambda shape: pl.BlockSpec((1,) + shape, lambda i, te, nr, nu: (te[live(i, nu)], 0, 0))
    grid_spec = pltpu.PrefetchScalarGridSpec(
        num_scalar_prefetch=3,
        grid=(n_steps,),
        in_specs=[pl.BlockSpec((FFN_STEP, width), lambda i, te, nr, nu: (live(i, nu), 0)),
                  wspec((d, f2)), wspec((1, f2)), wspec((f2 // 2, d)), wspec((1, d))],
        out_specs=pl.BlockSpec((FFN_STEP, d), lambda i, te, nr, nu: (i, 0)),
        scratch_shapes=[pltpu.VMEM((d, f2), jnp.bfloat16), pltpu.VMEM((f2 // 2, d), jnp.bfloat16),
                        pltpu.VMEM((FFN_TILE, f2 // 2), jnp.bfloat16)],
    )
    return pl.pallas_call(
        _ffn_kernel,
        grid_spec=grid_spec,
        out_shape=jax.ShapeDtypeStruct((rows, d), jnp.float32),
        compiler_params=_params("arbitrary"),
        name="expert_ffn",
    )(step_expert, step_rows, n_used, xs, w1, _unzip_bias(b1), w2, b2[:, None, :])


def _combine_kernel(tab_ref, next_tab_ref, lpos_ref, x1_ref, ga_ref, g_ref, y_ref, o_ref, buf_ref, sems):
    i = pl.program_id(0)
    n_steps = pl.num_programs(0)
    slot = i % 2
    tile = x1_ref.shape[0]
    n_local = buf_ref.shape[1]

    def fetch(tab, s):
        def piece(e, c):
            loc, glob, n = _piece_table(tab, e)

            @pl.when(n > 0)
            def _():
                pltpu.make_async_copy(y_ref.at[pl.ds(glob, n)], buf_ref.at[s, pl.ds(loc, n)], sems.at[s]).start()
            return c
        lax.fori_loop(0, N_EXPERTS, piece, 0)

    @pl.when(i == 0)
    def _():
        buf_ref[...] = jnp.zeros_like(buf_ref)
        fetch(tab_ref, 0)

    @pl.when(i + 1 < n_steps)
    def _():
        fetch(next_tab_ref, 1 - slot)

    n = pl.multiple_of(tab_ref[3 * N_EXPERTS], ROW_ALIGN)

    @pl.when(n > 0)
    def _():
        pltpu.make_async_copy(y_ref.at[pl.ds(0, n)], buf_ref.at[slot, pl.ds(0, n)], sems.at[slot]).wait()

    col = lax.broadcasted_iota(jnp.int32, (tile, n_local), 1)
    pos = lpos_ref[...]
    pick = col == pos[:, 0:1]
    for k in range(1, TOP_K):
        pick = pick | (col == pos[:, k:k + 1])
    pick = pick.astype(jnp.bfloat16)
    rows = buf_ref[slot]
    hi = rows.astype(jnp.bfloat16)
    lo = (rows - hi.astype(jnp.float32)).astype(jnp.bfloat16)
    y = _dot(pick, hi) + _dot(pick, lo)
    o_ref[...] = x1_ref[...] + ga_ref[0] * _rms(y, g_ref[...])


def _combine(ys, tab_flat, lpos, x1, mod3, g_post, *, seq):
    t, d = x1.shape
    tile = DISPATCH_TILE
    n_steps = t // tile
    per_seq = seq // tile
    n_local = tile * TOP_K + N_EXPERTS * ROW_ALIGN
    return pl.pallas_call(
        _combine_kernel,
        grid=(n_steps,),
        in_specs=[pl.BlockSpec((TAB_WIDTH,), lambda i: (i,), memory_space=pltpu.SMEM),
                  pl.BlockSpec((TAB_WIDTH,), lambda i: (jnp.minimum(i + 1, n_steps - 1),), memory_space=pltpu.SMEM),
                  pl.BlockSpec((tile, TOP_K), lambda i: (i, 0)),
                  pl.BlockSpec((tile, d), lambda i: (i, 0)),
                  pl.BlockSpec((1, 1, d), lambda i: (i // per_seq, 0, 5)),
                  pl.BlockSpec((1, d), lambda i: (0, 0)),
                  pl.BlockSpec(memory_space=pl.ANY)],
        out_specs=pl.BlockSpec((tile, d), lambda i: (i, 0)),
        out_shape=jax.ShapeDtypeStruct((t, d), jnp.float32),
        scratch_shapes=[pltpu.VMEM((2, n_local, d), ys.dtype), pltpu.SemaphoreType.DMA((2,))],
        compiler_params=_params("arbitrary"),
        name="combine",
    )(tab_flat, tab_flat, lpos, x1, mod3, g_post, ys)


def _rope_tables(seq):
    pos = np.arange(seq)
    n_freq = HEAD_DIM // 4
    freqs = ROPE_BASE ** (-jnp.arange(n_freq, dtype=jnp.float32) / n_freq)
    rows = jnp.asarray(pos // GRID_W, jnp.float32)[:, None] * freqs[None, :]
    cols = jnp.asarray(pos % GRID_W, jnp.float32)[:, None] * freqs[None, :]
    ang = jnp.concatenate([rows, rows, cols, cols], axis=1)
    sign = np.tile(np.repeat([-1.0, 1.0], n_freq), 2).astype(np.float32)
    cos = jnp.cos(ang)
    sin = jnp.sin(ang) * sign[None, :]
    reps = LANES // HEAD_DIM
    return jnp.tile(cos, (1, reps)), jnp.tile(sin, (1, reps))


def kernel(x, c, ctx, c_ctx, w_ada, b_ada, g_pre_mix, g_post_mix, g_pre_ffn, g_post_ffn, w_in, g_grp_a, g_grp_b,
           sink_a, rpb_b, w_out, w_router, b_router, w_mlp1, b_mlp1, w_mlp2, b_mlp2):
    batch, seq, d = x.shape
    n_ctx = ctx.shape[1]
    assert w_ada.shape[0] == 1, "single layer"
    assert seq % (ATT_BLOCKS * GRID_W * NB_Q_ROWS) == 0 and seq // GRID_W >= NB_K_ROWS
    assert seq % (ATT_BLOCKS * BLOCK_A) == 0 and seq >= BLOCK_A + 2 * WINDOW
    n_tok = batch * seq
    bf16 = jnp.bfloat16

    head_order = np.asarray([k * GQA_GROUP + j for j in range(GQA_GROUP) for k in range(N_KV_A)])
    col_order = (head_order[:, None] * HEAD_DIM + np.arange(HEAD_DIM)[None, :]).reshape(-1)

    mod_rows = -(-(batch + 1) // 8) * 8
    cc = jnp.zeros((mod_rows, d), jnp.float32).at[:batch].set(c).at[batch].set(c_ctx)
    mod3 = _ada(cc, w_ada[0], b_ada[0]).reshape(mod_rows, 1, 6 * d)

    w_in0 = w_in[0]
    w_lat = jnp.concatenate([w_in0[:, :WIDTH_A][:, col_order], w_in0[:, WIDTH_A:]], axis=1).astype(bf16)
    kv_cols = np.concatenate([np.arange(WIDTH_A, WIDTH_A + 2 * WIDTH_KV_A),
                              np.arange(WIDTH_A + 2 * WIDTH_KV_A + WIDTH_B, w_in0.shape[1])])
    w_ctx = w_in0[:, kv_cols].astype(bf16)
    g_pre = g_pre_mix[0].reshape(1, d)
    qa, ka, va, qb, kb, vb = _inproj(x.reshape(n_tok, d), mod3, g_pre, w_lat, _rope_tables(seq),
                                     seq=seq, mod_row0=0, latent=True)
    kac, vac, kbc, vbc = _inproj(ctx.reshape(batch * n_ctx, d), mod3, g_pre, w_ctx, None,
                                 seq=n_ctx, mod_row0=batch, latent=False)

    oa = _window_attention(qa, ka.reshape(batch, seq, -1), va.reshape(batch, seq, -1),
                           kac.reshape(batch, n_ctx, -1), vac.reshape(batch, n_ctx, -1),
                           sink_a[0].astype(jnp.float32) * LOG2E, batch=batch, seq=seq)
    tabs, cls_of_block = _nb_bias_tables(rpb_b[0], seq // GRID_W)
    ob = _neighbourhood_attention(qb, kb.reshape(batch, seq, -1), vb.reshape(batch, seq, -1),
                                  kbc.reshape(batch, n_ctx, -1), vbc.reshape(batch, n_ctx, -1),
                                  tabs, cls_of_block, batch=batch, seq=seq)

    w_out0 = w_out[0]
    w_o = jnp.concatenate([w_out0[:WIDTH_A][col_order], w_out0[WIDTH_A:]], axis=0).astype(bf16)
    x1, h2, idx, wts, rank, size, base, counts = _outproj(
        oa, ob, x.reshape(n_tok, d), mod3,
        g_grp_a[0][col_order].reshape(1, -1), g_grp_b[0].reshape(1, -1),
        g_post_mix[0].reshape(1, d), g_pre_ffn[0].reshape(1, d),
        w_o, w_router[0], b_router[0].reshape(1, -1), seq=seq)

    n_tok_tiles = n_tok // DISPATCH_TILE
    size = size.reshape(n_tok_tiles, N_EXPERTS)
    counts = counts.reshape(-1)
    padded = (counts + FFN_STEP - 1) // FFN_STEP * FFN_STEP
    pad_end = jnp.cumsum(padded).astype(jnp.int32)
    pad_start = pad_end - padded
    n_steps = (n_tok * TOP_K + n_tok_tiles * N_EXPERTS * (ROW_ALIGN - 1)) // FFN_STEP + N_EXPERTS
    n_used = (pad_end[-1:] // FFN_STEP).astype(jnp.int32)
    step_row0 = jnp.arange(n_steps, dtype=jnp.int32) * FFN_STEP
    step_expert = jnp.minimum(jnp.sum(step_row0[:, None] >= pad_end[None, :], axis=1),
                              N_EXPERTS - 1).astype(jnp.int32)
    own = step_expert[:, None] == jnp.arange(N_EXPERTS, dtype=jnp.int32)
    step_rows = jnp.clip(jnp.sum(jnp.where(own, (pad_start + counts)[None, :], 0), axis=1) - step_row0,
                         0, FFN_STEP).astype(jnp.int32)
    fill_from = (pad_start + jnp.maximum((counts + FFN_TILE - 1) // FFN_TILE * FFN_TILE - FFN_TILE, 0)
                 ).astype(jnp.int32)
    local0 = jnp.cumsum(size, axis=1) - size
    global0 = pad_start[None, :] + base.reshape(n_tok_tiles, N_EXPERTS)
    tab = jnp.concatenate([local0, global0, size, jnp.sum(size, axis=1, keepdims=True),
                           jnp.zeros((n_tok_tiles, TAB_WIDTH - 3 * N_EXPERTS - 1), jnp.int32)], axis=1)
    tab = tab.reshape(-1).astype(jnp.int32)
    hot = idx[:, :, None, :] == jnp.arange(N_EXPERTS, dtype=jnp.int32)[None, None, :, None]
    lpos_t = rank + jnp.sum(jnp.where(hot, local0[:, None, :, None], 0), axis=2)
    lpos = jnp.swapaxes(lpos_t, 1, 2).reshape(n_tok, TOP_K)

    xs = _dispatch(h2, tab, lpos_t, wts, fill_from, pad_end, n_rows=n_steps * FFN_STEP)
    ys = _ffn(xs, step_expert, step_rows, n_used, w_mlp1[0], b_mlp1[0], w_mlp2[0], b_mlp2[0])
    out = _combine(ys, tab, lpos, x1, mod3, g_post_ffn[0].reshape(1, d), seq=seq)
    return out.reshape(batch, seq, d)
```

```python
import functools

import numpy as np
import jax
import jax.numpy as jnp
from jax import lax
from jax.experimental import pallas as pl
from jax.experimental.pallas import tpu as pltpu

GRID_W = 64
HEAD_DIM = 64
N_HEADS_A = 8
N_KV_A = 2
GQA_GROUP = N_HEADS_A // N_KV_A
N_HEADS_B = 8
WIDTH_A = N_HEADS_A * HEAD_DIM
WIDTH_KV_A = N_KV_A * HEAD_DIM
WIDTH_B = N_HEADS_B * HEAD_DIM
WINDOW = 128
BLOCK_A = 128
NA_ROWS = 8
NA_COLS = 16
N_EXPERTS = 32
TOP_K = 4
SWIGLU_LIMIT = 7.0
SWIGLU_ALPHA = 1.702
ROPE_BASE = 10000.0
EPS = 1e-6
NEG_INF = -1e30

LANES = 128
VMEM_LIMIT = 56 * 1024 * 1024

LOG2E = 1.4426950408889634
ATT_BLOCKS = 2
NB_Q_ROWS = 2
NB_K_ROWS = NB_Q_ROWS + NA_ROWS
PROJ_TILE = 512
OUT_TILE = 256
FFN_TILE = 256
FFN_STEP = 4 * FFN_TILE
FFN_CHUNK = 256
DISPATCH_TILE = OUT_TILE
ROW_ALIGN = 8
TAB_WIDTH = 128


def _params(*sem):
    return pltpu.CompilerParams(dimension_semantics=sem, vmem_limit_bytes=VMEM_LIMIT)


def _rms(x, g):
    return x * lax.rsqrt(jnp.mean(x * x, axis=-1, keepdims=True) + EPS) * g


def _dot(a, b):
    return jnp.dot(a, b, preferred_element_type=jnp.float32)


def _dot_nt(a, b):
    return lax.dot_general(a, b, (((1,), (1,)), ((), ())), preferred_element_type=jnp.float32)


def _ada_kernel(c_ref, w_ref, b_ref, o_ref):
    c = c_ref[...]
    s = (c * jax.nn.sigmoid(c)).astype(jnp.bfloat16)
    o_ref[...] = _dot(s, w_ref[...].astype(jnp.bfloat16)) + b_ref[...]


def _ada(cc, w_ada, b_ada):
    rows, d = cc.shape
    n_out = w_ada.shape[1]
    return pl.pallas_call(
        _ada_kernel,
        grid=(n_out // d,),
        in_specs=[pl.BlockSpec((rows, d), lambda j: (0, 0)),
                  pl.BlockSpec((d, d), lambda j: (0, j)),
                  pl.BlockSpec((1, d), lambda j: (0, j))],
        out_specs=pl.BlockSpec((rows, d), lambda j: (0, j)),
        out_shape=jax.ShapeDtypeStruct((rows, n_out), jnp.float32),
        compiler_params=_params("arbitrary"),
        name="ada",
    )(cc, w_ada, b_ada.reshape(1, n_out))


def _rope(x, cos, sin):
    w = x.shape[1]
    reps = w // LANES
    if reps > 1:
        cos = jnp.concatenate([cos] * reps, axis=1)
        sin = jnp.concatenate([sin] * reps, axis=1)
    lane = lax.broadcasted_iota(jnp.int32, x.shape, 1)
    quarter = HEAD_DIM // 4
    partner = jnp.where(lane % (2 * quarter) < quarter,
                        pltpu.roll(x, w - quarter, 1), pltpu.roll(x, quarter, 1))
    return x * cos + partner * sin


def _inproj_kernel(x_ref, sh_ref, sc_ref, g_ref, w_ref, *rest, latent):
    x = x_ref[...]
    h = _rms(x, g_ref[...]) * (1.0 + sc_ref[0]) + sh_ref[0]
    p = _dot(h.astype(jnp.bfloat16), w_ref[...])
    if latent:
        cos_ref, sin_ref, qa_ref, ka_ref, va_ref, qb_ref, kb_ref, vb_ref = rest
        cos, sin = cos_ref[...], sin_ref[...]
        scale = HEAD_DIM ** -0.5 * LOG2E
        o = 0
        qa_ref[...] = (_rope(p[:, o:o + WIDTH_A], cos, sin) * scale).astype(qa_ref.dtype)
        o += WIDTH_A
        ka_ref[...] = _rope(p[:, o:o + WIDTH_KV_A], cos, sin).astype(ka_ref.dtype)
        o += WIDTH_KV_A
        va_ref[...] = p[:, o:o + WIDTH_KV_A].astype(va_ref.dtype)
        o += WIDTH_KV_A
        qb_ref[...] = (p[:, o:o + WIDTH_B] * scale).astype(qb_ref.dtype)
        o += WIDTH_B
    else:
        ka_ref, va_ref, kb_ref, vb_ref = rest
        o = 0
        ka_ref[...] = p[:, o:o + WIDTH_KV_A].astype(ka_ref.dtype)
        o += WIDTH_KV_A
        va_ref[...] = p[:, o:o + WIDTH_KV_A].astype(va_ref.dtype)
        o += WIDTH_KV_A
    kb_ref[...] = p[:, o:o + WIDTH_B].astype(kb_ref.dtype)
    o += WIDTH_B
    vb_ref[...] = p[:, o:o + WIDTH_B].astype(vb_ref.dtype)


def _inproj(x2d, mod3, g_pre, w, rope_tabs, *, seq, mod_row0, latent):
    t, d = x2d.shape
    tile = min(PROJ_TILE, seq)
    per_seq = seq // tile
    if latent:
        mod_row = lambda i: i // per_seq
    else:
        mod_row = lambda i: mod_row0
    in_specs = [pl.BlockSpec((tile, d), lambda i: (i, 0)),
                pl.BlockSpec((1, 1, d), lambda i: (mod_row(i), 0, 0)),
                pl.BlockSpec((1, 1, d), lambda i: (mod_row(i), 0, 1)),
                pl.BlockSpec((1, d), lambda i: (0, 0)),
                pl.BlockSpec(w.shape, lambda i: (0, 0))]
    args = [x2d, mod3, mod3, g_pre, w]
    widths = [WIDTH_KV_A, WIDTH_KV_A, WIDTH_B, WIDTH_B]
    if latent:
        in_specs += [pl.BlockSpec((tile, LANES), lambda i: (i % per_seq, 0))] * 2
        args += list(rope_tabs)
        widths = [WIDTH_A, WIDTH_KV_A, WIDTH_KV_A, WIDTH_B, WIDTH_B, WIDTH_B]
    return pl.pallas_call(
        functools.partial(_inproj_kernel, latent=latent),
        grid=(t // tile,),
        in_specs=in_specs,
        out_specs=[pl.BlockSpec((tile, wd), lambda i: (i, 0)) for wd in widths],
        out_shape=[jax.ShapeDtypeStruct((t, wd), jnp.bfloat16) for wd in widths],
        compiler_params=_params("parallel"),
        name="inproj_latent" if latent else "inproj_ctx",
    )(*args)


def _ones_beside(v, half):
    low = lax.broadcasted_iota(jnp.int32, v.shape, 1) < HEAD_DIM
    return jnp.where(low if half == 0 else ~low, v, jnp.ones_like(v))


def _pair_attention(q_ref, o_ref, rows, keys, values, sink_of, s_ref, p_ref, m_ref):
    tq = rows.stop - rows.start
    low = lax.broadcasted_iota(jnp.int32, (tq, LANES), 1) < HEAD_DIM
    n_pair = q_ref.shape[1] // LANES
    n_loc = None
    for j in range(n_pair):
        q = q_ref[rows, j * LANES:(j + 1) * LANES]
        for half in range(2):
            k_loc, k_ctx, bias = keys(j, half)
            n_loc = k_loc.shape[0]
            qm = jnp.where(low if half == 0 else ~low, q, jnp.zeros_like(q))
            s_ref[2 * j + half, :, :n_loc] = _dot_nt(qm, k_loc) + bias
            s_ref[2 * j + half, :, n_loc:] = _dot_nt(qm, k_ctx)
    for h in range(2 * n_pair):
        s = s_ref[h]
        m = jnp.max(s, axis=1, keepdims=True)
        sink = sink_of(h // 2, h % 2)
        if sink is not None:
            m = jnp.maximum(m, sink)
        p_ref[h] = jnp.exp2((s - m).astype(jnp.bfloat16))
        m_ref[h] = m
    for j in range(n_pair):
        acc = []
        for half in range(2):
            v_loc, v_ctx = values(j, half)
            h = 2 * j + half
            acc.append(_dot(p_ref[h, :, :n_loc], v_loc) + _dot(p_ref[h, :, n_loc:], v_ctx))
        num = jnp.where(low, acc[0], acc[1])
        den = pltpu.roll(jnp.where(low, acc[1], acc[0]), HEAD_DIM, 1)
        if sink_of(j, 0) is not None:
            den = den + jnp.where(low, jnp.exp2(sink_of(j, 0) - m_ref[2 * j]), jnp.exp2(sink_of(j, 1) - m_ref[2 * j + 1]))
        o_ref[rows, j * LANES:(j + 1) * LANES] = (num / den).astype(o_ref.dtype)


def _attention_scratch(tq, n_heads, n_keys):
    return [pltpu.VMEM((n_heads, tq, n_keys), jnp.float32), pltpu.VMEM((n_heads, tq, n_keys), jnp.bfloat16),
            pltpu.VMEM((n_heads, tq, 1), jnp.float32)]


def _win_kernel(sink_ref, q_ref, k_ref, v_ref, kc_ref, vc_ref, o_ref, *scratch, seq):
    span = BLOCK_A + 2 * WINDOW
    k_ctx = kc_ref[0]
    v_ctx = [_ones_beside(vc_ref[0], half) for half in range(2)]
    for sb in range(ATT_BLOCKS):
        i = pl.program_id(1) * ATT_BLOCKS + sb
        start = pl.multiple_of(jnp.clip(i * BLOCK_A - WINDOW, 0, seq - span), BLOCK_A)
        k_loc = k_ref[0, pl.ds(start, span), :]
        v_loc = [_ones_beside(v_ref[0, pl.ds(start, span), :], half) for half in range(2)]
        qpos = i * BLOCK_A + lax.broadcasted_iota(jnp.int32, (BLOCK_A, span), 0)
        kpos = start + lax.broadcasted_iota(jnp.int32, (BLOCK_A, span), 1)
        bias = jnp.where(jnp.abs(kpos - qpos) <= WINDOW, 0.0, NEG_INF).astype(jnp.float32)

        _pair_attention(q_ref, o_ref, slice(sb * BLOCK_A, (sb + 1) * BLOCK_A),
                        lambda j, half: (k_loc, k_ctx, bias),
                        lambda j, half: (v_loc[half], v_ctx[half]),
                        lambda j, half: sink_ref[half * GQA_GROUP + j], *scratch)


def _window_attention(qa, ka, va, kac, vac, sink, *, batch, seq):
    tq = ATT_BLOCKS * BLOCK_A
    nb = seq // tq
    n_ctx = kac.shape[1]
    return pl.pallas_call(
        functools.partial(_win_kernel, seq=seq),
        grid=(batch, nb),
        in_specs=[pl.BlockSpec(memory_space=pltpu.SMEM),
                  pl.BlockSpec((tq, WIDTH_A), lambda b, i: (b * nb + i, 0)),
                  pl.BlockSpec((1, seq, WIDTH_KV_A), lambda b, i: (b, 0, 0)),
                  pl.BlockSpec((1, seq, WIDTH_KV_A), lambda b, i: (b, 0, 0)),
                  pl.BlockSpec((1, n_ctx, WIDTH_KV_A), lambda b, i: (b, 0, 0)),
                  pl.BlockSpec((1, n_ctx, WIDTH_KV_A), lambda b, i: (b, 0, 0))],
        out_specs=pl.BlockSpec((tq, WIDTH_A), lambda b, i: (b * nb + i, 0)),
        out_shape=jax.ShapeDtypeStruct((batch * seq, WIDTH_A), jnp.bfloat16),
        scratch_shapes=_attention_scratch(BLOCK_A, N_HEADS_A, BLOCK_A + 2 * WINDOW + n_ctx),
        compiler_params=_params("parallel", "arbitrary"),
        name="window_attention",
    )(sink, qa, ka, va, kac, vac)


def _nb_kernel(q_ref, k_ref, v_ref, kc_ref, vc_ref, *rest, rows_n):
    tab_refs, o_ref, scratch = rest[:ATT_BLOCKS], rest[ATT_BLOCKS], rest[ATT_BLOCKS + 1:]
    n_keys = NB_K_ROWS * GRID_W
    tq = NB_Q_ROWS * GRID_W
    for sb in range(ATT_BLOCKS):
        m = pl.program_id(1) * ATT_BLOCKS + sb
        start_row = jnp.clip(NB_Q_ROWS * m - NA_ROWS // 2, 0, rows_n - NB_K_ROWS)
        start = pl.multiple_of(start_row * GRID_W, LANES)
        tab_ref = tab_refs[sb]

        def keys(j, half):
            cols = slice(j * LANES, (j + 1) * LANES)
            return (k_ref[0, pl.ds(start, n_keys), cols], kc_ref[0, :, cols],
                    tab_ref[0, 2 * j + half].astype(jnp.float32))

        def values(j, half):
            cols = slice(j * LANES, (j + 1) * LANES)
            return (_ones_beside(v_ref[0, pl.ds(start, n_keys), cols], half), _ones_beside(vc_ref[0, :, cols], half))

        _pair_attention(q_ref, o_ref, slice(sb * tq, (sb + 1) * tq), keys, values, lambda j, half: None, *scratch)


def _nb_classes(rows_n):
    n_blocks = rows_n // NB_Q_ROWS
    sig = {}
    cls_of_block = []
    reps = []
    for m in range(n_blocks):
        start_row = int(np.clip(NB_Q_ROWS * m - NA_ROWS // 2, 0, rows_n - NB_K_ROWS))
        key = tuple((start_row - r, int(np.clip(r - NA_ROWS // 2, 0, rows_n - NA_ROWS)) - r)
                    for r in range(NB_Q_ROWS * m, NB_Q_ROWS * (m + 1)))
        if key not in sig:
            sig[key] = len(reps)
            reps.append(m)
        cls_of_block.append(sig[key])
    return np.asarray(cls_of_block, np.int32), reps


def _nb_bias_tables(rpb, rows_n):
    cls_of_block, reps = _nb_classes(rows_n)
    n_heads = rpb.shape[0]
    cq = np.arange(GRID_W)[:, None]
    ck = np.arange(GRID_W)[None, :]
    cs = np.clip(cq - NA_COLS // 2, 0, GRID_W - NA_COLS)
    col_ok = (ck >= cs) & (ck < cs + NA_COLS)
    pick = ((ck - cq + NA_COLS - 1)[None] == np.arange(2 * NA_COLS - 1)[:, None, None]) & col_ok[None]
    tiles = jnp.einsum('hrd,dqk->hrqk', rpb.astype(jnp.float32), jnp.asarray(pick, jnp.float32),
                       precision=lax.Precision.HIGHEST)
    tiles = jnp.where(jnp.asarray(col_ok)[None, None], tiles * LOG2E, NEG_INF).astype(jnp.bfloat16)
    blocked = jnp.full((n_heads, GRID_W, GRID_W), NEG_INF, jnp.bfloat16)
    tabs = []
    for m in reps:
        start_row = int(np.clip(NB_Q_ROWS * m - NA_ROWS // 2, 0, rows_n - NB_K_ROWS))
        q_rows = []
        for r in range(NB_Q_ROWS * m, NB_Q_ROWS * (m + 1)):
            rs = int(np.clip(r - NA_ROWS // 2, 0, rows_n - NA_ROWS))
            q_rows.append(jnp.concatenate(
                [tiles[:, krow - r + NA_ROWS - 1] if rs <= krow < rs + NA_ROWS else blocked
                 for krow in range(start_row, start_row + NB_K_ROWS)], axis=2))
        tabs.append(jnp.concatenate(q_rows, axis=1))
    return jnp.stack(tabs), cls_of_block


def _neighbourhood_attention(qb, kb, vb, kbc, vbc, tabs, cls_of_block, *, batch, seq):
    rows_n = seq // GRID_W
    tq = ATT_BLOCKS * NB_Q_ROWS * GRID_W
    nstep = seq // tq
    n_ctx = kbc.shape[1]

    def tab_spec(sb):
        return pl.BlockSpec((1,) + tabs.shape[1:], lambda b, m, c: (c[m * ATT_BLOCKS + sb], 0, 0, 0))

    grid_spec = pltpu.PrefetchScalarGridSpec(
        num_scalar_prefetch=1,
        grid=(batch, nstep),
        in_specs=[pl.BlockSpec((tq, WIDTH_B), lambda b, m, c: (b * nstep + m, 0)),
                  pl.BlockSpec((1, seq, WIDTH_B), lambda b, m, c: (b, 0, 0)),
                  pl.BlockSpec((1, seq, WIDTH_B), lambda b, m, c: (b, 0, 0)),
                  pl.BlockSpec((1, n_ctx, WIDTH_B), lambda b, m, c: (b, 0, 0)),
                  pl.BlockSpec((1, n_ctx, WIDTH_B), lambda b, m, c: (b, 0, 0))]
                 + [tab_spec(sb) for sb in range(ATT_BLOCKS)],
        out_specs=pl.BlockSpec((tq, WIDTH_B), lambda b, m, c: (b * nstep + m, 0)),
        scratch_shapes=_attention_scratch(NB_Q_ROWS * GRID_W, N_HEADS_B, NB_K_ROWS * GRID_W + n_ctx),
    )

    def body(c_ref, *refs):
        _nb_kernel(*refs, rows_n=rows_n)

    return pl.pallas_call(
        body,
        grid_spec=grid_spec,
        out_shape=jax.ShapeDtypeStruct((batch * seq, WIDTH_B), jnp.bfloat16),
        compiler_params=_params("parallel", "arbitrary"),
        name="neighbourhood_attention",
    )(jnp.asarray(cls_of_block), qb, kb, vb, kbc, vbc, *([tabs] * ATT_BLOCKS))


def _outproj_kernel(oa_ref, ob_ref, x_ref, ga_ref, sh_ref, sc_ref, gga_ref, ggb_ref, gpost_ref, gpre_ref,
                    wo_ref, wr_hi_ref, wr_lo_ref, br_ref,
                    x1_ref, h2_ref, idx_ref, wts_ref, rank_ref, size_ref, base_ref, cnt_ref, run_ref):
    i = pl.program_id(0)

    @pl.when(i == 0)
    def _():
        run_ref[...] = jnp.zeros_like(run_ref)

    na = _rms(oa_ref[...].astype(jnp.float32), gga_ref[...]).astype(jnp.bfloat16)
    nb = _rms(ob_ref[...].astype(jnp.float32), ggb_ref[...]).astype(jnp.bfloat16)
    mix = _dot(na, wo_ref[:WIDTH_A, :]) + _dot(nb, wo_ref[WIDTH_A:, :])
    x1 = x_ref[...] + ga_ref[0] * _rms(mix, gpost_ref[...])
    x1_ref[...] = x1
    h2 = _rms(x1, gpre_ref[...]) * (1.0 + sc_ref[0]) + sh_ref[0]
    h2_ref[...] = h2.astype(h2_ref.dtype)

    h_hi = h2.astype(jnp.bfloat16)
    h_lo = (h2 - h_hi.astype(jnp.float32)).astype(jnp.bfloat16)
    logits = _dot(h_hi, wr_hi_ref[...]) + _dot(h_lo, wr_hi_ref[...]) + _dot(h_hi, wr_lo_ref[...])
    work = logits.T[:N_EXPERTS] + br_ref[...]
    e, t = work.shape
    row = lax.broadcasted_iota(jnp.int32, (e, t), 0)
    chosen = jnp.zeros((e, t), jnp.float32)
    vals, sels, hots = [], [], []
    for _k in range(TOP_K):
        mx = jnp.max(work, axis=0, keepdims=True)
        sel = jnp.min(jnp.where(work == mx, row, e), axis=0, keepdims=True)
        hot = row == sel
        vals.append(mx)
        sels.append(sel)
        hots.append(hot)
        work = jnp.where(hot, -jnp.inf, work)
        chosen = chosen + hot.astype(jnp.float32)
    ex = [jnp.exp(v - vals[0]) for v in vals]
    den = ex[0] + ex[1] + ex[2] + ex[3]

    earlier = (lax.broadcasted_iota(jnp.int32, (t, t), 0) < lax.broadcasted_iota(jnp.int32, (t, t), 1))
    prefix = _dot(chosen.astype(jnp.bfloat16), earlier.astype(jnp.bfloat16))
    ranks = [jnp.sum(jnp.where(h, prefix, 0.0), axis=0, keepdims=True) for h in hots]
    size = jnp.sum(chosen, axis=1, keepdims=True).astype(jnp.int32)
    size = (size + (ROW_ALIGN - 1)) // ROW_ALIGN * ROW_ALIGN
    size_ref[0] = size
    base_ref[0] = run_ref[...]
    run_ref[...] = run_ref[...] + size
    cnt_ref[...] = run_ref[...]

    idx_ref[0] = jnp.concatenate(sels, axis=0)
    wts_ref[0] = jnp.concatenate([v / den for v in ex], axis=0)
    rank_ref[0] = jnp.concatenate(ranks, axis=0).astype(jnp.int32)


def _outproj(oa, ob, x2d, mod3, gga, ggb, gpost, gpre, w_out, w_router, b_router, *, seq):
    t, d = x2d.shape
    tile = OUT_TILE
    per_seq = seq // tile
    row = lambda i: (i, 0)
    const = lambda i: (0, 0)
    modspec = lambda part: pl.BlockSpec((1, 1, d), lambda i: (i // per_seq, 0, part))
    per_tile = lambda shape: pl.BlockSpec((1,) + shape, lambda i: (i, 0, 0))
    n_steps = t // tile
    w_pad = jnp.zeros((d, LANES), jnp.float32).at[:, :N_EXPERTS].set(w_router)
    wr_hi = w_pad.astype(jnp.bfloat16)
    wr_lo = (w_pad - wr_hi.astype(jnp.float32)).astype(jnp.bfloat16)
    outs = pl.pallas_call(
        _outproj_kernel,
        grid=(n_steps,),
        in_specs=[pl.BlockSpec((tile, WIDTH_A), row), pl.BlockSpec((tile, WIDTH_B), row),
                  pl.BlockSpec((tile, d), row),
                  modspec(2), modspec(3), modspec(4),
                  pl.BlockSpec((1, WIDTH_A), const), pl.BlockSpec((1, WIDTH_B), const),
                  pl.BlockSpec((1, d), const), pl.BlockSpec((1, d), const),
                  pl.BlockSpec(w_out.shape, const), pl.BlockSpec((d, LANES), const), pl.BlockSpec((d, LANES), const),
                  pl.BlockSpec((N_EXPERTS, 1), const)],
        out_specs=[pl.BlockSpec((tile, d), row), pl.BlockSpec((tile, d), row),
                   per_tile((TOP_K, tile)), per_tile((TOP_K, tile)), per_tile((TOP_K, tile)),
                   per_tile((N_EXPERTS, 1)), per_tile((N_EXPERTS, 1)),
                   pl.BlockSpec((N_EXPERTS, 1), const)],
        out_shape=[jax.ShapeDtypeStruct((t, d), jnp.float32), jax.ShapeDtypeStruct((t, d), jnp.bfloat16),
                   jax.ShapeDtypeStruct((n_steps, TOP_K, tile), jnp.int32),
                   jax.ShapeDtypeStruct((n_steps, TOP_K, tile), jnp.float32),
                   jax.ShapeDtypeStruct((n_steps, TOP_K, tile), jnp.int32),
                   jax.ShapeDtypeStruct((n_steps, N_EXPERTS, 1), jnp.int32),
                   jax.ShapeDtypeStruct((n_steps, N_EXPERTS, 1), jnp.int32),
                   jax.ShapeDtypeStruct((N_EXPERTS, 1), jnp.int32)],
        scratch_shapes=[pltpu.VMEM((N_EXPERTS, 1), jnp.int32)],
        compiler_params=_params("arbitrary"),
        name="outproj_router",
    )(oa, ob, x2d, mod3, mod3, mod3, gga, ggb, gpost, gpre, w_out, wr_hi, wr_lo, b_router.reshape(N_EXPERTS, 1))
    return outs


def _pack_pairs(x):
    half = x.shape[1] // 2
    bits = pltpu.bitcast(x, jnp.uint32)
    return (bits[:, half:] & jnp.uint32(0xFFFF0000)) | (bits[:, :half] >> 16)


def _unpack_pairs(p):
    low = pltpu.bitcast(p << 16, jnp.float32).astype(jnp.bfloat16)
    high = pltpu.bitcast(p & jnp.uint32(0xFFFF0000), jnp.float32).astype(jnp.bfloat16)
    return jnp.concatenate([low, high], axis=1)


def _piece_table(tab_ref, e):
    return (pl.multiple_of(tab_ref[e], ROW_ALIGN), pl.multiple_of(tab_ref[N_EXPERTS + e], ROW_ALIGN),
            pl.multiple_of(tab_ref[2 * N_EXPERTS + e], ROW_ALIGN))


def _dispatch_kernel(fill_from_ref, pad_end_ref, tab_ref, h_ref, lpos_ref, w_ref, xs_ref,
                     stage_ref, zero_ref, pend_ref, sems, zsem):
    i = pl.program_id(0)
    n_steps = pl.num_programs(0)
    slot = i % 2
    n_local, width = stage_ref.shape[1:]
    d = h_ref.shape[1]

    def fill(row):
        return pltpu.make_async_copy(zero_ref, xs_ref.at[pl.ds(pl.multiple_of(row, FFN_TILE), FFN_TILE)], zsem)

    def fills(e):
        return (pad_end_ref[e] - fill_from_ref[e]) // FFN_TILE

    def drain(s):
        n = pl.multiple_of(pend_ref[s], ROW_ALIGN)

        @pl.when(n > 0)
        def _():
            pltpu.make_async_copy(stage_ref.at[s, pl.ds(0, n)], xs_ref.at[pl.ds(0, n)], sems.at[s]).wait()
        pend_ref[s] = 0

    @pl.when(i == 0)
    def _():
        zero_ref[...] = jnp.zeros_like(zero_ref)
        pend_ref[0] = 0
        pend_ref[1] = 0

        def start_e(e, c):
            def start_j(j, c2):
                fill(fill_from_ref[e] + j * FFN_TILE).start()
                return c2
            lax.fori_loop(0, fills(e), start_j, 0)
            return c + fills(e)
        n_fill = lax.fori_loop(0, N_EXPERTS, start_e, 0)

        def wait_all(j, c):
            fill(0).wait()
            return c
        lax.fori_loop(0, n_fill, wait_all, 0)

    row = lax.broadcasted_iota(jnp.int32, (n_local, h_ref.shape[0]), 0)
    place = None
    row_w = None
    for k in range(TOP_K):
        hit = row == lpos_ref[0, k:k + 1, :]
        place = hit if place is None else place | hit
        wk = jnp.sum(jnp.where(hit, w_ref[0, k:k + 1, :], 0.0), axis=1, keepdims=True)
        row_w = wk if row_w is None else row_w + wk
    rows = _dot(place.astype(jnp.bfloat16), h_ref[...])

    drain(slot)
    stage_ref[slot, :, :d // 2] = _pack_pairs(rows)
    stage_ref[slot, :, d // 2:] = pltpu.bitcast(jnp.broadcast_to(row_w, (n_local, width - d // 2)), jnp.uint32)

    def piece(e, total):
        loc, glob, n = _piece_table(tab_ref, e)

        @pl.when(n > 0)
        def _():
            pltpu.make_async_copy(stage_ref.at[slot, pl.ds(loc, n)], xs_ref.at[pl.ds(glob, n)], sems.at[slot]).start()
        return total + n
    pend_ref[slot] = lax.fori_loop(0, N_EXPERTS, piece, 0)

    tail0 = pad_end_ref[N_EXPERTS - 1]
    n_tail = (xs_ref.shape[0] - tail0) // FFN_TILE
    per_step = (n_tail + n_steps - 1) // n_steps

    def start_tail(j, c):
        t = i * per_step + j

        @pl.when(t < n_tail)
        def _():
            fill(tail0 + t * FFN_TILE).start()
        return c
    lax.fori_loop(0, per_step, start_tail, 0)

    @pl.when(i == n_steps - 1)
    def _():
        drain(0)
        drain(1)

        def wait_tail(j, c):
            fill(0).wait()
            return c
        lax.fori_loop(0, n_tail, wait_tail, 0)


def _dispatch(h2, tab_flat, lpos_t, w_t, fill_from, pad_end, *, n_rows):
    t, d = h2.shape
    tile = DISPATCH_TILE
    n_local = tile * TOP_K + N_EXPERTS * ROW_ALIGN
    width = d // 2 + LANES
    grid_spec = pltpu.PrefetchScalarGridSpec(
        num_scalar_prefetch=2,
        grid=(t // tile,),
        in_specs=[pl.BlockSpec((TAB_WIDTH,), lambda i, *_: (i,), memory_space=pltpu.SMEM),
                  pl.BlockSpec((tile, d), lambda i, *_: (i, 0)),
                  pl.BlockSpec((1, TOP_K, tile), lambda i, *_: (i, 0, 0)),
                  pl.BlockSpec((1, TOP_K, tile), lambda i, *_: (i, 0, 0))],
        out_specs=pl.BlockSpec(memory_space=pl.ANY),
        scratch_shapes=[pltpu.VMEM((2, n_local, width), jnp.uint32),
                        pltpu.VMEM((FFN_TILE, width), jnp.uint32),
                        pltpu.SMEM((2,), jnp.int32),
                        pltpu.SemaphoreType.DMA((2,)), pltpu.SemaphoreType.DMA(())],
    )
    return pl.pallas_call(
        _dispatch_kernel,
        grid_spec=grid_spec,
        out_shape=jax.ShapeDtypeStruct((n_rows, width), jnp.uint32),
        compiler_params=_params("arbitrary"),
        name="dispatch",
    )(fill_from, pad_end, tab_flat, h2, lpos_t, w_t)


def _ffn_kernel(te_ref, rows_ref, n_used_ref, x_ref, w1_ref, b1_ref, w2_ref, b2_ref, y_ref,
                w1p_ref, w2b_ref, act_ref):
    i = pl.program_id(0)
    live = i < n_used_ref[0]
    n_pair = w1_ref.shape[2] // FFN_CHUNK
    half = FFN_CHUNK // 2
    d = w2_ref.shape[2]
    n_sub = jnp.where(live, (rows_ref[i] + FFN_TILE - 1) // FFN_TILE, 0)

    @pl.when(live & ((i == 0) | (te_ref[i] != te_ref[jnp.maximum(i - 1, 0)])))
    def _():
        src = lax.broadcasted_iota(jnp.int32, (FFN_CHUNK, FFN_CHUNK), 0)
        dst = lax.broadcasted_iota(jnp.int32, (FFN_CHUNK, FFN_CHUNK), 1)
        unzip = (src == jnp.where(dst < half, 2 * dst, 2 * (dst - half) + 1)).astype(jnp.bfloat16)
        for c in range(n_pair):
            cs = slice(c * FFN_CHUNK, (c + 1) * FFN_CHUNK)
            w1p_ref[:, cs] = _dot(w1_ref[0, :, cs].astype(jnp.bfloat16), unzip).astype(jnp.bfloat16)
        w2b_ref[...] = w2_ref[0].astype(jnp.bfloat16)

    def compute(n_rows):
        x = _unpack_pairs(x_ref[:n_rows, :d // 2])
        for c in range(0, n_pair, 2):
            cs = slice(c * FFN_CHUNK, (c + 2) * FFN_CHUNK)
            h = _dot(x, w1p_ref[:, cs]) + b1_ref[0, :, cs]
            g = jnp.concatenate([h[:, :half], h[:, FFN_CHUNK:FFN_CHUNK + half]], axis=1)
            u = jnp.concatenate([h[:, half:FFN_CHUNK], h[:, FFN_CHUNK + half:]], axis=1)
            g = jnp.minimum(g, SWIGLU_LIMIT)
            u = jnp.clip(u, -SWIGLU_LIMIT, SWIGLU_LIMIT)
            act = g * jax.nn.sigmoid(SWIGLU_ALPHA * g) * (u + 1.0)
            act_ref[:n_rows, c * half:(c + 2) * half] = act.astype(act_ref.dtype)
        row_w = pltpu.bitcast(x_ref[:n_rows, d // 2:d // 2 + 1], jnp.float32)
        y = (_dot(act_ref[:n_rows, :], w2b_ref[...]) + b2_ref[0]) * row_w
        y_ref[:n_rows, :] = _pack_pairs(y.astype(jnp.bfloat16).astype(jnp.float32))
        if n_rows < y_ref.shape[0]:
            y_ref[n_rows:, :] = jnp.zeros((y_ref.shape[0] - n_rows, d // 2), y_ref.dtype)

    for tiles in range(1, y_ref.shape[0] // FFN_TILE + 1):
        pl.when(n_sub == tiles)(functools.partial(compute, tiles * FFN_TILE))

    @pl.when(n_sub == 0)
    def _():
        y_ref[...] = jnp.zeros_like(y_ref)


def _unzip_bias(b1):
    e, f2 = b1.shape
    half = FFN_CHUNK // 2
    return b1.reshape(e, f2 // FFN_CHUNK, half, 2).transpose(0, 1, 3, 2).reshape(e, 1, f2)


def _ffn(xs, step_expert, step_rows, n_used, w1, b1, w2, b2):
    rows, width = xs.shape
    d, f2 = w1.shape[1:]
    n_steps = rows // FFN_STEP
    live = lambda i, nu: jnp.minimum(i, nu[0] - 1)
    wspec = lambda shape: pl.BlockSpec((1,) + shape, lambda i, te, nr, nu: (te[live(i, nu)], 0, 0))
    grid_spec = pltpu.PrefetchScalarGridSpec(
        num_scalar_prefetch=3,
        grid=(n_steps,),
        in_specs=[pl.BlockSpec((FFN_STEP, width), lambda i, te, nr, nu: (live(i, nu), 0)),
                  wspec((d, f2)), wspec((1, f2)), wspec((f2 // 2, d)), wspec((1, d))],
        out_specs=pl.BlockSpec((FFN_STEP, d // 2), lambda i, te, nr, nu: (i, 0)),
        scratch_shapes=[pltpu.VMEM((d, f2), jnp.bfloat16), pltpu.VMEM((f2 // 2, d), jnp.bfloat16),
                        pltpu.VMEM((FFN_STEP, f2 // 2), jnp.bfloat16)],
    )
    return pl.pallas_call(
        _ffn_kernel,
        grid_spec=grid_spec,
        out_shape=jax.ShapeDtypeStruct((rows, d // 2), jnp.uint32),
        compiler_params=_params("arbitrary"),
        name="expert_ffn",
    )(step_expert, step_rows, n_used, xs, w1, _unzip_bias(b1), w2, b2[:, None, :])


def _combine_kernel(tab_ref, next_tab_ref, lpos_ref, x1_ref, ga_ref, g_ref, y_ref, o_ref, buf_ref, sems):
    i = pl.program_id(0)
    n_steps = pl.num_programs(0)
    slot = i % 2
    tile = x1_ref.shape[0]
    n_local = buf_ref.shape[1]

    def fetch(tab, s):
        def piece(e, c):
            loc, glob, n = _piece_table(tab, e)

            @pl.when(n > 0)
            def _():
                pltpu.make_async_copy(y_ref.at[pl.ds(glob, n)], buf_ref.at[s, pl.ds(loc, n)], sems.at[s]).start()
            return c
        lax.fori_loop(0, N_EXPERTS, piece, 0)

    @pl.when(i == 0)
    def _():
        buf_ref[...] = jnp.zeros_like(buf_ref)
        fetch(tab_ref, 0)

    @pl.when(i + 1 < n_steps)
    def _():
        fetch(next_tab_ref, 1 - slot)

    n = pl.multiple_of(tab_ref[3 * N_EXPERTS], ROW_ALIGN)

    @pl.when(n > 0)
    def _():
        pltpu.make_async_copy(y_ref.at[pl.ds(0, n)], buf_ref.at[slot, pl.ds(0, n)], sems.at[slot]).wait()

    col = lax.broadcasted_iota(jnp.int32, (tile, n_local), 1)
    pos = lpos_ref[...]
    pick = col == pos[:, 0:1]
    for k in range(1, TOP_K):
        pick = pick | (col == pos[:, k:k + 1])
    pick = pick.astype(jnp.bfloat16)
    y = _dot(pick, _unpack_pairs(buf_ref[slot]))
    o_ref[...] = x1_ref[...] + ga_ref[0] * _rms(y, g_ref[...])


def _combine(ys, tab_flat, lpos, x1, mod3, g_post, *, seq):
    t, d = x1.shape
    tile = DISPATCH_TILE
    n_steps = t // tile
    per_seq = seq // tile
    n_local = tile * TOP_K + N_EXPERTS * ROW_ALIGN
    return pl.pallas_call(
        _combine_kernel,
        grid=(n_steps,),
        in_specs=[pl.BlockSpec((TAB_WIDTH,), lambda i: (i,), memory_space=pltpu.SMEM),
                  pl.BlockSpec((TAB_WIDTH,), lambda i: (jnp.minimum(i + 1, n_steps - 1),), memory_space=pltpu.SMEM),
                  pl.BlockSpec((tile, TOP_K), lambda i: (i, 0)),
                  pl.BlockSpec((tile, d), lambda i: (i, 0)),
                  pl.BlockSpec((1, 1, d), lambda i: (i // per_seq, 0, 5)),
                  pl.BlockSpec((1, d), lambda i: (0, 0)),
                  pl.BlockSpec(memory_space=pl.ANY)],
        out_specs=pl.BlockSpec((tile, d), lambda i: (i, 0)),
        out_shape=jax.ShapeDtypeStruct((t, d), jnp.float32),
        scratch_shapes=[pltpu.VMEM((2, n_local, ys.shape[1]), ys.dtype), pltpu.SemaphoreType.DMA((2,))],
        compiler_params=_params("arbitrary"),
        name="combine",
    )(tab_flat, tab_flat, lpos, x1, mod3, g_post, ys)


def _rope_tables(seq):
    pos = np.arange(seq)
    n_freq = HEAD_DIM // 4
    freqs = ROPE_BASE ** (-jnp.arange(n_freq, dtype=jnp.float32) / n_freq)
    rows = jnp.asarray(pos // GRID_W, jnp.float32)[:, None] * freqs[None, :]
    cols = jnp.asarray(pos % GRID_W, jnp.float32)[:, None] * freqs[None, :]
    ang = jnp.concatenate([rows, rows, cols, cols], axis=1)
    sign = np.tile(np.repeat([-1.0, 1.0], n_freq), 2).astype(np.float32)
    cos = jnp.cos(ang)
    sin = jnp.sin(ang) * sign[None, :]
    reps = LANES // HEAD_DIM
    return jnp.tile(cos, (1, reps)), jnp.tile(sin, (1, reps))


def kernel(x, c, ctx, c_ctx, w_ada, b_ada, g_pre_mix, g_post_mix, g_pre_ffn, g_post_ffn, w_in, g_grp_a, g_grp_b,
           sink_a, rpb_b, w_out, w_router, b_router, w_mlp1, b_mlp1, w_mlp2, b_mlp2):
    batch, seq, d = x.shape
    n_ctx = ctx.shape[1]
    assert w_ada.shape[0] == 1, "single layer"
    assert seq % (ATT_BLOCKS * GRID_W * NB_Q_ROWS) == 0 and seq // GRID_W >= NB_K_ROWS
    assert seq % (ATT_BLOCKS * BLOCK_A) == 0 and seq >= BLOCK_A + 2 * WINDOW
    n_tok = batch * seq
    bf16 = jnp.bfloat16

    head_order = np.asarray([k * GQA_GROUP + j for j in range(GQA_GROUP) for k in range(N_KV_A)])
    col_order = (head_order[:, None] * HEAD_DIM + np.arange(HEAD_DIM)[None, :]).reshape(-1)

    mod_rows = -(-(batch + 1) // 8) * 8
    cc = jnp.zeros((mod_rows, d), jnp.float32).at[:batch].set(c).at[batch].set(c_ctx)
    mod3 = _ada(cc, w_ada[0], b_ada[0]).reshape(mod_rows, 1, 6 * d)

    w_in0 = w_in[0]
    w_lat = jnp.concatenate([w_in0[:, :WIDTH_A][:, col_order], w_in0[:, WIDTH_A:]], axis=1).astype(bf16)
    kv_cols = np.concatenate([np.arange(WIDTH_A, WIDTH_A + 2 * WIDTH_KV_A),
                              np.arange(WIDTH_A + 2 * WIDTH_KV_A + WIDTH_B, w_in0.shape[1])])
    w_ctx = w_in0[:, kv_cols].astype(bf16)
    g_pre = g_pre_mix[0].reshape(1, d)
    qa, ka, va, qb, kb, vb = _inproj(x.reshape(n_tok, d), mod3, g_pre, w_lat, _rope_tables(seq),
                                     seq=seq, mod_row0=0, latent=True)
    kac, vac, kbc, vbc = _inproj(ctx.reshape(batch * n_ctx, d), mod3, g_pre, w_ctx, None,
                                 seq=n_ctx, mod_row0=batch, latent=False)

    oa = _window_attention(qa, ka.reshape(batch, seq, -1), va.reshape(batch, seq, -1),
                           kac.reshape(batch, n_ctx, -1), vac.reshape(batch, n_ctx, -1),
                           sink_a[0].astype(jnp.float32) * LOG2E, batch=batch, seq=seq)
    tabs, cls_of_block = _nb_bias_tables(rpb_b[0], seq // GRID_W)
    ob = _neighbourhood_attention(qb, kb.reshape(batch, seq, -1), vb.reshape(batch, seq, -1),
                                  kbc.reshape(batch, n_ctx, -1), vbc.reshape(batch, n_ctx, -1),
                                  tabs, cls_of_block, batch=batch, seq=seq)

    w_out0 = w_out[0]
    w_o = jnp.concatenate([w_out0[:WIDTH_A][col_order], w_out0[WIDTH_A:]], axis=0).astype(bf16)
    x1, h2, idx, wts, rank, size, base, counts = _outproj(
        oa, ob, x.reshape(n_tok, d), mod3,
        g_grp_a[0][col_order].reshape(1, -1), g_grp_b[0].reshape(1, -1),
        g_post_mix[0].reshape(1, d), g_pre_ffn[0].reshape(1, d),
        w_o, w_router[0], b_router[0].reshape(1, -1), seq=seq)

    n_tok_tiles = n_tok // DISPATCH_TILE
    size = size.reshape(n_tok_tiles, N_EXPERTS)
    counts = counts.reshape(-1)
    padded = (counts + FFN_STEP - 1) // FFN_STEP * FFN_STEP
    pad_end = jnp.cumsum(padded).astype(jnp.int32)
    pad_start = pad_end - padded
    n_steps = (n_tok * TOP_K + n_tok_tiles * N_EXPERTS * (ROW_ALIGN - 1)) // FFN_STEP + N_EXPERTS
    n_used = (pad_end[-1:] // FFN_STEP).astype(jnp.int32)
    step_row0 = jnp.arange(n_steps, dtype=jnp.int32) * FFN_STEP
    step_expert = jnp.minimum(jnp.sum(step_row0[:, None] >= pad_end[None, :], axis=1),
                              N_EXPERTS - 1).astype(jnp.int32)
    own = step_expert[:, None] == jnp.arange(N_EXPERTS, dtype=jnp.int32)
    step_rows = jnp.clip(jnp.sum(jnp.where(own, (pad_start + counts)[None, :], 0), axis=1) - step_row0,
                         0, FFN_STEP).astype(jnp.int32)
    fill_from = (pad_start + jnp.maximum((counts + FFN_TILE - 1) // FFN_TILE * FFN_TILE - FFN_TILE, 0)
                 ).astype(jnp.int32)
    local0 = jnp.cumsum(size, axis=1) - size
    global0 = pad_start[None, :] + base.reshape(n_tok_tiles, N_EXPERTS)
    tab = jnp.concatenate([local0, global0, size, jnp.sum(size, axis=1, keepdims=True),
                           jnp.zeros((n_tok_tiles, TAB_WIDTH - 3 * N_EXPERTS - 1), jnp.int32)], axis=1)
    tab = tab.reshape(-1).astype(jnp.int32)
    hot = idx[:, :, None, :] == jnp.arange(N_EXPERTS, dtype=jnp.int32)[None, None, :, None]
    lpos_t = rank + jnp.sum(jnp.where(hot, local0[:, None, :, None], 0), axis=2)
    lpos = jnp.swapaxes(lpos_t, 1, 2).reshape(n_tok, TOP_K)

    xs = _dispatch(h2, tab, lpos_t, wts, fill_from, pad_end, n_rows=n_steps * FFN_STEP)
    ys = _ffn(xs, step_expert, step_rows, n_used, w_mlp1[0], b_mlp1[0], w_mlp2[0], b_mlp2[0])
    out = _combine(ys, tab, lpos, x1, mod3, g_post_ffn[0].reshape(1, d), seq=seq)
    return out.reshape(batch, seq, d)
```

```python
import functools

import numpy as np
import jax
import jax.numpy as jnp
from jax import lax
from jax.experimental import pallas as pl
from jax.experimental.pallas import tpu as pltpu

GRID_W = 64
HEAD_DIM = 64
N_HEADS_A = 8
N_KV_A = 2
GQA_GROUP = N_HEADS_A // N_KV_A
N_HEADS_B = 8
WIDTH_A = N_HEADS_A * HEAD_DIM
WIDTH_KV_A = N_KV_A * HEAD_DIM
WIDTH_B = N_HEADS_B * HEAD_DIM
WINDOW = 128
BLOCK_A = 128
NA_ROWS = 8
NA_COLS = 16
N_EXPERTS = 32
TOP_K = 4
SWIGLU_LIMIT = 7.0
SWIGLU_ALPHA = 1.702
ROPE_BASE = 10000.0
EPS = 1e-6
NEG_INF = -1e30

LANES = 128
VMEM_LIMIT = 56 * 1024 * 1024

LOG2E = 1.4426950408889634
ATT_BLOCKS = 2
NB_Q_ROWS = 2
NB_K_ROWS = NB_Q_ROWS + NA_ROWS
PROJ_TILE = 512
OUT_TILE = 256
FFN_TILE = 256
FFN_STEP = 4 * FFN_TILE
FFN_CHUNK = 256
DISPATCH_TILE = OUT_TILE
ROW_ALIGN = 8
TAB_WIDTH = 128


def _params(*sem):
    return pltpu.CompilerParams(dimension_semantics=sem, vmem_limit_bytes=VMEM_LIMIT)


def _rms(x, g):
    return x * lax.rsqrt(jnp.mean(x * x, axis=-1, keepdims=True) + EPS) * g


def _dot(a, b):
    return jnp.dot(a, b, preferred_element_type=jnp.float32)


def _dot_nt(a, b):
    return lax.dot_general(a, b, (((1,), (1,)), ((), ())), preferred_element_type=jnp.float32)


def _ada_kernel(c_ref, w_ref, b_ref, o_ref):
    c = c_ref[...]
    s = (c * jax.nn.sigmoid(c)).astype(jnp.bfloat16)
    o_ref[...] = _dot(s, w_ref[...].astype(jnp.bfloat16)) + b_ref[...]


def _ada(cc, w_ada, b_ada):
    rows, d = cc.shape
    n_out = w_ada.shape[1]
    return pl.pallas_call(
        _ada_kernel,
        grid=(n_out // d,),
        in_specs=[pl.BlockSpec((rows, d), lambda j: (0, 0)),
                  pl.BlockSpec((d, d), lambda j: (0, j)),
                  pl.BlockSpec((1, d), lambda j: (0, j))],
        out_specs=pl.BlockSpec((rows, d), lambda j: (0, j)),
        out_shape=jax.ShapeDtypeStruct((rows, n_out), jnp.float32),
        compiler_params=_params("arbitrary"),
        name="ada",
    )(cc, w_ada, b_ada.reshape(1, n_out))


def _rope(x, cos, sin):
    w = x.shape[1]
    reps = w // LANES
    if reps > 1:
        cos = jnp.concatenate([cos] * reps, axis=1)
        sin = jnp.concatenate([sin] * reps, axis=1)
    lane = lax.broadcasted_iota(jnp.int32, x.shape, 1)
    quarter = HEAD_DIM // 4
    partner = jnp.where(lane % (2 * quarter) < quarter,
                        pltpu.roll(x, w - quarter, 1), pltpu.roll(x, quarter, 1))
    return x * cos + partner * sin


def _inproj_kernel(x_ref, sh_ref, sc_ref, g_ref, w_ref, *rest, latent):
    x = x_ref[...]
    h = _rms(x, g_ref[...]) * (1.0 + sc_ref[0]) + sh_ref[0]
    p = _dot(h.astype(jnp.bfloat16), w_ref[...])
    if latent:
        cos_ref, sin_ref, qa_ref, ka_ref, va_ref, qb_ref, kb_ref, vb_ref = rest
        cos, sin = cos_ref[...], sin_ref[...]
        scale = HEAD_DIM ** -0.5 * LOG2E
        o = 0
        qa_ref[...] = (_rope(p[:, o:o + WIDTH_A], cos, sin) * scale).astype(qa_ref.dtype)
        o += WIDTH_A
        ka_ref[...] = _rope(p[:, o:o + WIDTH_KV_A], cos, sin).astype(ka_ref.dtype)
        o += WIDTH_KV_A
        va_ref[...] = p[:, o:o + WIDTH_KV_A].astype(va_ref.dtype)
        o += WIDTH_KV_A
        qb_ref[...] = (p[:, o:o + WIDTH_B] * scale).astype(qb_ref.dtype)
        o += WIDTH_B
    else:
        ka_ref, va_ref, kb_ref, vb_ref = rest
        o = 0
        ka_ref[...] = p[:, o:o + WIDTH_KV_A].astype(ka_ref.dtype)
        o += WIDTH_KV_A
        va_ref[...] = p[:, o:o + WIDTH_KV_A].astype(va_ref.dtype)
        o += WIDTH_KV_A
    kb_ref[...] = p[:, o:o + WIDTH_B].astype(kb_ref.dtype)
    o += WIDTH_B
    vb_ref[...] = p[:, o:o + WIDTH_B].astype(vb_ref.dtype)


def _inproj(x2d, mod3, g_pre, w, rope_tabs, *, seq, mod_row0, latent):
    t, d = x2d.shape
    tile = min(PROJ_TILE, seq)
    per_seq = seq // tile
    if latent:
        mod_row = lambda i: i // per_seq
    else:
        mod_row = lambda i: mod_row0
    in_specs = [pl.BlockSpec((tile, d), lambda i: (i, 0)),
                pl.BlockSpec((1, 1, d), lambda i: (mod_row(i), 0, 0)),
                pl.BlockSpec((1, 1, d), lambda i: (mod_row(i), 0, 1)),
                pl.BlockSpec((1, d), lambda i: (0, 0)),
                pl.BlockSpec(w.shape, lambda i: (0, 0))]
    args = [x2d, mod3, mod3, g_pre, w]
    widths = [WIDTH_KV_A, WIDTH_KV_A, WIDTH_B, WIDTH_B]
    if latent:
        in_specs += [pl.BlockSpec((tile, LANES), lambda i: (i % per_seq, 0))] * 2
        args += list(rope_tabs)
        widths = [WIDTH_A, WIDTH_KV_A, WIDTH_KV_A, WIDTH_B, WIDTH_B, WIDTH_B]
    return pl.pallas_call(
        functools.partial(_inproj_kernel, latent=latent),
        grid=(t // tile,),
        in_specs=in_specs,
        out_specs=[pl.BlockSpec((tile, wd), lambda i: (i, 0)) for wd in widths],
        out_shape=[jax.ShapeDtypeStruct((t, wd), jnp.bfloat16) for wd in widths],
        compiler_params=_params("parallel"),
        name="inproj_latent" if latent else "inproj_ctx",
    )(*args)


def _ones_beside(v, half):
    low = lax.broadcasted_iota(jnp.int32, v.shape, 1) < HEAD_DIM
    return jnp.where(low if half == 0 else ~low, v, jnp.ones_like(v))


def _pair_attention(q_ref, o_ref, rows, keys, values, sink_of, s_ref, p_ref, m_ref):
    tq = rows.stop - rows.start
    low = lax.broadcasted_iota(jnp.int32, (tq, LANES), 1) < HEAD_DIM
    n_pair = q_ref.shape[1] // LANES
    n_loc = None
    for j in range(n_pair):
        q = q_ref[rows, j * LANES:(j + 1) * LANES]
        for half in range(2):
            k_loc, k_ctx, bias = keys(j, half)
            n_loc = k_loc.shape[0]
            qm = jnp.where(low if half == 0 else ~low, q, jnp.zeros_like(q))
            s_ref[2 * j + half, :, :n_loc] = _dot_nt(qm, k_loc) + bias
            s_ref[2 * j + half, :, n_loc:] = _dot_nt(qm, k_ctx)
    for h in range(2 * n_pair):
        s = s_ref[h]
        m = jnp.max(s, axis=1, keepdims=True)
        sink = sink_of(h // 2, h % 2)
        if sink is not None:
            m = jnp.maximum(m, sink)
        p_ref[h] = jnp.exp2((s - m).astype(jnp.bfloat16))
        m_ref[h] = m
    for j in range(n_pair):
        acc = []
        for half in range(2):
            v_loc, v_ctx = values(j, half)
            h = 2 * j + half
            acc.append(_dot(p_ref[h, :, :n_loc], v_loc) + _dot(p_ref[h, :, n_loc:], v_ctx))
        num = jnp.where(low, acc[0], acc[1])
        den = pltpu.roll(jnp.where(low, acc[1], acc[0]), HEAD_DIM, 1)
        if sink_of(j, 0) is not None:
            den = den + jnp.where(low, jnp.exp2(sink_of(j, 0) - m_ref[2 * j]), jnp.exp2(sink_of(j, 1) - m_ref[2 * j + 1]))
        o_ref[rows, j * LANES:(j + 1) * LANES] = (num / den).astype(o_ref.dtype)


def _attention_scratch(tq, n_heads, n_keys):
    return [pltpu.VMEM((n_heads, tq, n_keys), jnp.float32), pltpu.VMEM((n_heads, tq, n_keys), jnp.bfloat16),
            pltpu.VMEM((n_heads, tq, 1), jnp.float32)]


def _win_kernel(sink_ref, q_ref, k_ref, v_ref, kc_ref, vc_ref, o_ref, *scratch, seq):
    span = BLOCK_A + 2 * WINDOW
    k_ctx = kc_ref[0]
    v_ctx = [_ones_beside(vc_ref[0], half) for half in range(2)]
    for sb in range(ATT_BLOCKS):
        i = pl.program_id(1) * ATT_BLOCKS + sb
        start = pl.multiple_of(jnp.clip(i * BLOCK_A - WINDOW, 0, seq - span), BLOCK_A)
        k_loc = k_ref[0, pl.ds(start, span), :]
        v_loc = [_ones_beside(v_ref[0, pl.ds(start, span), :], half) for half in range(2)]
        qpos = i * BLOCK_A + lax.broadcasted_iota(jnp.int32, (BLOCK_A, span), 0)
        kpos = start + lax.broadcasted_iota(jnp.int32, (BLOCK_A, span), 1)
        bias = jnp.where(jnp.abs(kpos - qpos) <= WINDOW, 0.0, NEG_INF).astype(jnp.float32)

        _pair_attention(q_ref, o_ref, slice(sb * BLOCK_A, (sb + 1) * BLOCK_A),
                        lambda j, half: (k_loc, k_ctx, bias),
                        lambda j, half: (v_loc[half], v_ctx[half]),
                        lambda j, half: sink_ref[half * GQA_GROUP + j], *scratch)


def _window_attention(qa, ka, va, kac, vac, sink, *, batch, seq):
    tq = ATT_BLOCKS * BLOCK_A
    nb = seq // tq
    n_ctx = kac.shape[1]
    return pl.pallas_call(
        functools.partial(_win_kernel, seq=seq),
        grid=(batch, nb),
        in_specs=[pl.BlockSpec(memory_space=pltpu.SMEM),
                  pl.BlockSpec((tq, WIDTH_A), lambda b, i: (b * nb + i, 0)),
                  pl.BlockSpec((1, seq, WIDTH_KV_A), lambda b, i: (b, 0, 0)),
                  pl.BlockSpec((1, seq, WIDTH_KV_A), lambda b, i: (b, 0, 0)),
                  pl.BlockSpec((1, n_ctx, WIDTH_KV_A), lambda b, i: (b, 0, 0)),
                  pl.BlockSpec((1, n_ctx, WIDTH_KV_A), lambda b, i: (b, 0, 0))],
        out_specs=pl.BlockSpec((tq, WIDTH_A), lambda b, i: (b * nb + i, 0)),
        out_shape=jax.ShapeDtypeStruct((batch * seq, WIDTH_A), jnp.bfloat16),
        scratch_shapes=_attention_scratch(BLOCK_A, N_HEADS_A, BLOCK_A + 2 * WINDOW + n_ctx),
        compiler_params=_params("parallel", "arbitrary"),
        name="window_attention",
    )(sink, qa, ka, va, kac, vac)


def _nb_kernel(q_ref, k_ref, v_ref, kc_ref, vc_ref, *rest, rows_n):
    tab_refs, o_ref, scratch = rest[:ATT_BLOCKS], rest[ATT_BLOCKS], rest[ATT_BLOCKS + 1:]
    n_keys = NB_K_ROWS * GRID_W
    tq = NB_Q_ROWS * GRID_W
    for sb in range(ATT_BLOCKS):
        m = pl.program_id(1) * ATT_BLOCKS + sb
        start_row = jnp.clip(NB_Q_ROWS * m - NA_ROWS // 2, 0, rows_n - NB_K_ROWS)
        start = pl.multiple_of(start_row * GRID_W, LANES)
        tab_ref = tab_refs[sb]

        def keys(j, half):
            cols = slice(j * LANES, (j + 1) * LANES)
            return (k_ref[0, pl.ds(start, n_keys), cols], kc_ref[0, :, cols],
                    tab_ref[0, 2 * j + half].astype(jnp.float32))

        def values(j, half):
            cols = slice(j * LANES, (j + 1) * LANES)
            return (_ones_beside(v_ref[0, pl.ds(start, n_keys), cols], half), _ones_beside(vc_ref[0, :, cols], half))

        _pair_attention(q_ref, o_ref, slice(sb * tq, (sb + 1) * tq), keys, values, lambda j, half: None, *scratch)


def _nb_classes(rows_n):
    n_blocks = rows_n // NB_Q_ROWS
    sig = {}
    cls_of_block = []
    reps = []
    for m in range(n_blocks):
        start_row = int(np.clip(NB_Q_ROWS * m - NA_ROWS // 2, 0, rows_n - NB_K_ROWS))
        key = tuple((start_row - r, int(np.clip(r - NA_ROWS // 2, 0, rows_n - NA_ROWS)) - r)
                    for r in range(NB_Q_ROWS * m, NB_Q_ROWS * (m + 1)))
        if key not in sig:
            sig[key] = len(reps)
            reps.append(m)
        cls_of_block.append(sig[key])
    return np.asarray(cls_of_block, np.int32), reps


def _nb_bias_tables(rpb, rows_n):
    cls_of_block, reps = _nb_classes(rows_n)
    n_heads = rpb.shape[0]
    cq = np.arange(GRID_W)[:, None]
    ck = np.arange(GRID_W)[None, :]
    cs = np.clip(cq - NA_COLS // 2, 0, GRID_W - NA_COLS)
    col_ok = (ck >= cs) & (ck < cs + NA_COLS)
    pick = ((ck - cq + NA_COLS - 1)[None] == np.arange(2 * NA_COLS - 1)[:, None, None]) & col_ok[None]
    tiles = jnp.einsum('hrd,dqk->hrqk', rpb.astype(jnp.float32), jnp.asarray(pick, jnp.float32),
                       precision=lax.Precision.HIGHEST)
    tiles = jnp.where(jnp.asarray(col_ok)[None, None], tiles * LOG2E, NEG_INF).astype(jnp.bfloat16)
    blocked = jnp.full((n_heads, GRID_W, GRID_W), NEG_INF, jnp.bfloat16)
    tabs = []
    for m in reps:
        start_row = int(np.clip(NB_Q_ROWS * m - NA_ROWS // 2, 0, rows_n - NB_K_ROWS))
        q_rows = []
        for r in range(NB_Q_ROWS * m, NB_Q_ROWS * (m + 1)):
            rs = int(np.clip(r - NA_ROWS // 2, 0, rows_n - NA_ROWS))
            q_rows.append(jnp.concatenate(
                [tiles[:, krow - r + NA_ROWS - 1] if rs <= krow < rs + NA_ROWS else blocked
                 for krow in range(start_row, start_row + NB_K_ROWS)], axis=2))
        tabs.append(jnp.concatenate(q_rows, axis=1))
    return jnp.stack(tabs), cls_of_block


def _neighbourhood_attention(qb, kb, vb, kbc, vbc, tabs, cls_of_block, *, batch, seq):
    rows_n = seq // GRID_W
    tq = ATT_BLOCKS * NB_Q_ROWS * GRID_W
    nstep = seq // tq
    n_ctx = kbc.shape[1]

    def tab_spec(sb):
        return pl.BlockSpec((1,) + tabs.shape[1:], lambda b, m, c: (c[m * ATT_BLOCKS + sb], 0, 0, 0))

    grid_spec = pltpu.PrefetchScalarGridSpec(
        num_scalar_prefetch=1,
        grid=(batch, nstep),
        in_specs=[pl.BlockSpec((tq, WIDTH_B), lambda b, m, c: (b * nstep + m, 0)),
                  pl.BlockSpec((1, seq, WIDTH_B), lambda b, m, c: (b, 0, 0)),
                  pl.BlockSpec((1, seq, WIDTH_B), lambda b, m, c: (b, 0, 0)),
                  pl.BlockSpec((1, n_ctx, WIDTH_B), lambda b, m, c: (b, 0, 0)),
                  pl.BlockSpec((1, n_ctx, WIDTH_B), lambda b, m, c: (b, 0, 0))]
                 + [tab_spec(sb) for sb in range(ATT_BLOCKS)],
        out_specs=pl.BlockSpec((tq, WIDTH_B), lambda b, m, c: (b * nstep + m, 0)),
        scratch_shapes=_attention_scratch(NB_Q_ROWS * GRID_W, N_HEADS_B, NB_K_ROWS * GRID_W + n_ctx),
    )

    def body(c_ref, *refs):
        _nb_kernel(*refs, rows_n=rows_n)

    return pl.pallas_call(
        body,
        grid_spec=grid_spec,
        out_shape=jax.ShapeDtypeStruct((batch * seq, WIDTH_B), jnp.bfloat16),
        compiler_params=_params("parallel", "arbitrary"),
        name="neighbourhood_attention",
    )(jnp.asarray(cls_of_block), qb, kb, vb, kbc, vbc, *([tabs] * ATT_BLOCKS))


def _outproj_kernel(oa_ref, ob_ref, x_ref, ga_ref, sh_ref, sc_ref, gga_ref, ggb_ref, gpost_ref, gpre_ref,
                    wo_ref, wr_hi_ref, wr_lo_ref, br_ref,
                    x1_ref, h2_ref, idx_ref, wts_ref, rank_ref, size_ref, base_ref, cnt_ref, run_ref):
    i = pl.program_id(0)

    @pl.when(i == 0)
    def _():
        run_ref[...] = jnp.zeros_like(run_ref)

    na = _rms(oa_ref[...].astype(jnp.float32), gga_ref[...]).astype(jnp.bfloat16)
    nb = _rms(ob_ref[...].astype(jnp.float32), ggb_ref[...]).astype(jnp.bfloat16)
    mix = _dot(na, wo_ref[:WIDTH_A, :]) + _dot(nb, wo_ref[WIDTH_A:, :])
    x1 = x_ref[...] + ga_ref[0] * _rms(mix, gpost_ref[...])
    x1_ref[...] = x1
    h2 = _rms(x1, gpre_ref[...]) * (1.0 + sc_ref[0]) + sh_ref[0]
    h2_ref[...] = h2.astype(h2_ref.dtype)

    h_hi = h2.astype(jnp.bfloat16)
    h_lo = (h2 - h_hi.astype(jnp.float32)).astype(jnp.bfloat16)
    logits = _dot(h_hi, wr_hi_ref[...]) + _dot(h_lo, wr_hi_ref[...]) + _dot(h_hi, wr_lo_ref[...])
    work = logits.T[:N_EXPERTS] + br_ref[...]
    e, t = work.shape
    row = lax.broadcasted_iota(jnp.int32, (e, t), 0)
    chosen = jnp.zeros((e, t), jnp.float32)
    vals, sels, hots = [], [], []
    for _k in range(TOP_K):
        mx = jnp.max(work, axis=0, keepdims=True)
        sel = jnp.min(jnp.where(work == mx, row, e), axis=0, keepdims=True)
        hot = row == sel
        vals.append(mx)
        sels.append(sel)
        hots.append(hot)
        work = jnp.where(hot, -jnp.inf, work)
        chosen = chosen + hot.astype(jnp.float32)
    ex = [jnp.exp(v - vals[0]) for v in vals]
    den = ex[0] + ex[1] + ex[2] + ex[3]

    earlier = (lax.broadcasted_iota(jnp.int32, (t, t), 0) < lax.broadcasted_iota(jnp.int32, (t, t), 1))
    prefix = _dot(chosen.astype(jnp.bfloat16), earlier.astype(jnp.bfloat16))
    ranks = [jnp.sum(jnp.where(h, prefix, 0.0), axis=0, keepdims=True) for h in hots]
    size = jnp.sum(chosen, axis=1, keepdims=True).astype(jnp.int32)
    size = (size + (ROW_ALIGN - 1)) // ROW_ALIGN * ROW_ALIGN
    size_ref[0] = size
    base_ref[0] = run_ref[...]
    run_ref[...] = run_ref[...] + size
    cnt_ref[...] = run_ref[...]

    idx_ref[0] = jnp.concatenate(sels, axis=0)
    wts_ref[0] = jnp.concatenate([v / den for v in ex], axis=0)
    rank_ref[0] = jnp.concatenate(ranks, axis=0).astype(jnp.int32)


def _outproj(oa, ob, x2d, mod3, gga, ggb, gpost, gpre, w_out, w_router, b_router, *, seq):
    t, d = x2d.shape
    tile = OUT_TILE
    per_seq = seq // tile
    row = lambda i: (i, 0)
    const = lambda i: (0, 0)
    modspec = lambda part: pl.BlockSpec((1, 1, d), lambda i: (i // per_seq, 0, part))
    per_tile = lambda shape: pl.BlockSpec((1,) + shape, lambda i: (i, 0, 0))
    n_steps = t // tile
    w_pad = jnp.zeros((d, LANES), jnp.float32).at[:, :N_EXPERTS].set(w_router)
    wr_hi = w_pad.astype(jnp.bfloat16)
    wr_lo = (w_pad - wr_hi.astype(jnp.float32)).astype(jnp.bfloat16)
    outs = pl.pallas_call(
        _outproj_kernel,
        grid=(n_steps,),
        in_specs=[pl.BlockSpec((tile, WIDTH_A), row), pl.BlockSpec((tile, WIDTH_B), row),
                  pl.BlockSpec((tile, d), row),
                  modspec(2), modspec(3), modspec(4),
                  pl.BlockSpec((1, WIDTH_A), const), pl.BlockSpec((1, WIDTH_B), const),
                  pl.BlockSpec((1, d), const), pl.BlockSpec((1, d), const),
                  pl.BlockSpec(w_out.shape, const), pl.BlockSpec((d, LANES), const), pl.BlockSpec((d, LANES), const),
                  pl.BlockSpec((N_EXPERTS, 1), const)],
        out_specs=[pl.BlockSpec((tile, d), row), pl.BlockSpec((tile, d), row),
                   per_tile((TOP_K, tile)), per_tile((TOP_K, tile)), per_tile((TOP_K, tile)),
                   per_tile((N_EXPERTS, 1)), per_tile((N_EXPERTS, 1)),
                   pl.BlockSpec((N_EXPERTS, 1), const)],
        out_shape=[jax.ShapeDtypeStruct((t, d), jnp.float32), jax.ShapeDtypeStruct((t, d), jnp.bfloat16),
                   jax.ShapeDtypeStruct((n_steps, TOP_K, tile), jnp.int32),
                   jax.ShapeDtypeStruct((n_steps, TOP_K, tile), jnp.float32),
                   jax.ShapeDtypeStruct((n_steps, TOP_K, tile), jnp.int32),
                   jax.ShapeDtypeStruct((n_steps, N_EXPERTS, 1), jnp.int32),
                   jax.ShapeDtypeStruct((n_steps, N_EXPERTS, 1), jnp.int32),
                   jax.ShapeDtypeStruct((N_EXPERTS, 1), jnp.int32)],
        scratch_shapes=[pltpu.VMEM((N_EXPERTS, 1), jnp.int32)],
        compiler_params=_params("arbitrary"),
        name="outproj_router",
    )(oa, ob, x2d, mod3, mod3, mod3, gga, ggb, gpost, gpre, w_out, wr_hi, wr_lo, b_router.reshape(N_EXPERTS, 1))
    return outs


def _piece_table(tab_ref, e):
    return (pl.multiple_of(tab_ref[e], ROW_ALIGN), pl.multiple_of(tab_ref[N_EXPERTS + e], ROW_ALIGN),
            pl.multiple_of(tab_ref[2 * N_EXPERTS + e], ROW_ALIGN))


def _dispatch_kernel(fill_from_ref, pad_end_ref, tab_ref, h_ref, lpos_ref, w_ref, xs_ref,
                     stage_ref, zero_ref, pend_ref, sems, zsem):
    i = pl.program_id(0)
    n_steps = pl.num_programs(0)
    slot = i % 2
    n_local, width = stage_ref.shape[1:]
    d = h_ref.shape[1]

    def fill(row):
        return pltpu.make_async_copy(zero_ref, xs_ref.at[pl.ds(pl.multiple_of(row, FFN_TILE), FFN_TILE)], zsem)

    def fills(e):
        return (pad_end_ref[e] - fill_from_ref[e]) // FFN_TILE

    def drain(s):
        n = pl.multiple_of(pend_ref[s], ROW_ALIGN)

        @pl.when(n > 0)
        def _():
            pltpu.make_async_copy(stage_ref.at[s, pl.ds(0, n)], xs_ref.at[pl.ds(0, n)], sems.at[s]).wait()
        pend_ref[s] = 0

    @pl.when(i == 0)
    def _():
        zero_ref[...] = jnp.zeros_like(zero_ref)
        pend_ref[0] = 0
        pend_ref[1] = 0

        def start_e(e, c):
            def start_j(j, c2):
                fill(fill_from_ref[e] + j * FFN_TILE).start()
                return c2
            lax.fori_loop(0, fills(e), start_j, 0)
            return c + fills(e)
        n_fill = lax.fori_loop(0, N_EXPERTS, start_e, 0)

        def wait_all(j, c):
            fill(0).wait()
            return c
        lax.fori_loop(0, n_fill, wait_all, 0)

    row = lax.broadcasted_iota(jnp.int32, (n_local, h_ref.shape[0]), 0)
    place = None
    row_w = None
    for k in range(TOP_K):
        hit = row == lpos_ref[0, k:k + 1, :]
        place = hit if place is None else place | hit
        wk = jnp.sum(jnp.where(hit, w_ref[0, k:k + 1, :], 0.0), axis=1, keepdims=True)
        row_w = wk if row_w is None else row_w + wk
    rows = _dot(place.astype(jnp.bfloat16), h_ref[...])

    drain(slot)
    stage_ref[slot, :, :d] = rows
    stage_ref[slot, :, d:] = jnp.broadcast_to(row_w, (n_local, width - d))

    def piece(e, total):
        loc, glob, n = _piece_table(tab_ref, e)

        @pl.when(n > 0)
        def _():
            pltpu.make_async_copy(stage_ref.at[slot, pl.ds(loc, n)], xs_ref.at[pl.ds(glob, n)], sems.at[slot]).start()
        return total + n
    pend_ref[slot] = lax.fori_loop(0, N_EXPERTS, piece, 0)

    tail0 = pad_end_ref[N_EXPERTS - 1]
    n_tail = (xs_ref.shape[0] - tail0) // FFN_TILE
    per_step = (n_tail + n_steps - 1) // n_steps

    def start_tail(j, c):
        t = i * per_step + j

        @pl.when(t < n_tail)
        def _():
            fill(tail0 + t * FFN_TILE).start()
        return c
    lax.fori_loop(0, per_step, start_tail, 0)

    @pl.when(i == n_steps - 1)
    def _():
        drain(0)
        drain(1)

        def wait_tail(j, c):
            fill(0).wait()
            return c
        lax.fori_loop(0, n_tail, wait_tail, 0)


def _dispatch(h2, tab_flat, lpos_t, w_t, fill_from, pad_end, *, n_rows):
    t, d = h2.shape
    tile = DISPATCH_TILE
    n_local = tile * TOP_K + N_EXPERTS * ROW_ALIGN
    width = d + LANES
    grid_spec = pltpu.PrefetchScalarGridSpec(
        num_scalar_prefetch=2,
        grid=(t // tile,),
        in_specs=[pl.BlockSpec((TAB_WIDTH,), lambda i, *_: (i,), memory_space=pltpu.SMEM),
                  pl.BlockSpec((tile, d), lambda i, *_: (i, 0)),
                  pl.BlockSpec((1, TOP_K, tile), lambda i, *_: (i, 0, 0)),
                  pl.BlockSpec((1, TOP_K, tile), lambda i, *_: (i, 0, 0))],
        out_specs=pl.BlockSpec(memory_space=pl.ANY),
        scratch_shapes=[pltpu.VMEM((2, n_local, width), jnp.float32),
                        pltpu.VMEM((FFN_TILE, width), jnp.float32),
                        pltpu.SMEM((2,), jnp.int32),
                        pltpu.SemaphoreType.DMA((2,)), pltpu.SemaphoreType.DMA(())],
    )
    return pl.pallas_call(
        _dispatch_kernel,
        grid_spec=grid_spec,
        out_shape=jax.ShapeDtypeStruct((n_rows, width), jnp.float32),
        compiler_params=_params("arbitrary"),
        name="dispatch",
    )(fill_from, pad_end, tab_flat, h2, lpos_t, w_t)


def _ffn_kernel(te_ref, rows_ref, n_used_ref, x_ref, w1_ref, b1_ref, w2_ref, b2_ref, y_ref,
                w1p_ref, w2b_ref, act_ref):
    i = pl.program_id(0)
    live = i < n_used_ref[0]
    n_pair = w1_ref.shape[2] // FFN_CHUNK
    half = FFN_CHUNK // 2
    d = w2_ref.shape[2]
    n_sub = jnp.where(live, (rows_ref[i] + FFN_TILE - 1) // FFN_TILE, 0)

    @pl.when(live & ((i == 0) | (te_ref[i] != te_ref[jnp.maximum(i - 1, 0)])))
    def _():
        src = lax.broadcasted_iota(jnp.int32, (FFN_CHUNK, FFN_CHUNK), 0)
        dst = lax.broadcasted_iota(jnp.int32, (FFN_CHUNK, FFN_CHUNK), 1)
        unzip = (src == jnp.where(dst < half, 2 * dst, 2 * (dst - half) + 1)).astype(jnp.bfloat16)
        for c in range(n_pair):
            cs = slice(c * FFN_CHUNK, (c + 1) * FFN_CHUNK)
            w1p_ref[:, cs] = _dot(w1_ref[0, :, cs].astype(jnp.bfloat16), unzip).astype(jnp.bfloat16)
        w2b_ref[...] = w2_ref[0].astype(jnp.bfloat16)

    def compute(n_rows):
        x = x_ref[:n_rows, :d].astype(jnp.bfloat16)
        for c in range(0, n_pair, 2):
            cs = slice(c * FFN_CHUNK, (c + 2) * FFN_CHUNK)
            h = _dot(x, w1p_ref[:, cs]) + b1_ref[0, :, cs]
            g = jnp.concatenate([h[:, :half], h[:, FFN_CHUNK:FFN_CHUNK + half]], axis=1)
            u = jnp.concatenate([h[:, half:FFN_CHUNK], h[:, FFN_CHUNK + half:]], axis=1)
            g = jnp.minimum(g, SWIGLU_LIMIT)
            u = jnp.clip(u, -SWIGLU_LIMIT, SWIGLU_LIMIT)
            act = g * jax.nn.sigmoid(SWIGLU_ALPHA * g) * (u + 1.0)
            act_ref[:n_rows, c * half:(c + 2) * half] = act.astype(act_ref.dtype)
        y_ref[:n_rows, :] = (_dot(act_ref[:n_rows, :], w2b_ref[...]) + b2_ref[0]) * x_ref[:n_rows, d:d + 1]
        if n_rows < y_ref.shape[0]:
            y_ref[n_rows:, :] = jnp.zeros((y_ref.shape[0] - n_rows, d), y_ref.dtype)

    for tiles in range(1, y_ref.shape[0] // FFN_TILE + 1):
        pl.when(n_sub == tiles)(functools.partial(compute, tiles * FFN_TILE))

    @pl.when(n_sub == 0)
    def _():
        y_ref[...] = jnp.zeros_like(y_ref)


def _unzip_bias(b1):
    e, f2 = b1.shape
    half = FFN_CHUNK // 2
    return b1.reshape(e, f2 // FFN_CHUNK, half, 2).transpose(0, 1, 3, 2).reshape(e, 1, f2)


def _ffn(xs, step_expert, step_rows, n_used, w1, b1, w2, b2):
    rows, width = xs.shape
    d, f2 = w1.shape[1:]
    n_steps = rows // FFN_STEP
    live = lambda i, nu: jnp.minimum(i, nu[0] - 1)
    wspec = lambda shape: pl.BlockSpec((1,) + shape, lambda i, te, nr, nu: (te[live(i, nu)], 0, 0))
    grid_spec = pltpu.PrefetchScalarGridSpec(
        num_scalar_prefetch=3,
        grid=(n_steps,),
        in_specs=[pl.BlockSpec((FFN_STEP, width), lambda i, te, nr, nu: (live(i, nu), 0)),
                  wspec((d, f2)), wspec((1, f2)), wspec((f2 // 2, d)), wspec((1, d))],
        out_specs=pl.BlockSpec((FFN_STEP, d), lambda i, te, nr, nu: (i, 0)),
        scratch_shapes=[pltpu.VMEM((d, f2), jnp.bfloat16), pltpu.VMEM((f2 // 2, d), jnp.bfloat16),
                        pltpu.VMEM((FFN_STEP, f2 // 2), jnp.bfloat16)],
    )
    return pl.pallas_call(
        _ffn_kernel,
        grid_spec=grid_spec,
        out_shape=jax.ShapeDtypeStruct((rows, d), jnp.float32),
        compiler_params=_params("arbitrary"),
        name="expert_ffn",
    )(step_expert, step_rows, n_used, xs, w1, _unzip_bias(b1), w2, b2[:, None, :])


def _combine_kernel(tab_ref, next_tab_ref, lpos_ref, x1_ref, ga_ref, g_ref, y_ref, o_ref, buf_ref, sems):
    i = pl.program_id(0)
    n_steps = pl.num_programs(0)
    slot = i % 2
    tile = x1_ref.shape[0]
    n_local = buf_ref.shape[1]

    def fetch(tab, s):
        def piece(e, c):
            loc, glob, n = _piece_table(tab, e)

            @pl.when(n > 0)
            def _():
                pltpu.make_async_copy(y_ref.at[pl.ds(glob, n)], buf_ref.at[s, pl.ds(loc, n)], sems.at[s]).start()
            return c
        lax.fori_loop(0, N_EXPERTS, piece, 0)

    @pl.when(i == 0)
    def _():
        buf_ref[...] = jnp.zeros_like(buf_ref)
        fetch(tab_ref, 0)

    @pl.when(i + 1 < n_steps)
    def _():
        fetch(next_tab_ref, 1 - slot)

    n = pl.multiple_of(tab_ref[3 * N_EXPERTS], ROW_ALIGN)

    @pl.when(n > 0)
    def _():
        pltpu.make_async_copy(y_ref.at[pl.ds(0, n)], buf_ref.at[slot, pl.ds(0, n)], sems.at[slot]).wait()

    col = lax.broadcasted_iota(jnp.int32, (tile, n_local), 1)
    pos = lpos_ref[...]
    pick = col == pos[:, 0:1]
    for k in range(1, TOP_K):
        pick = pick | (col == pos[:, k:k + 1])
    pick = pick.astype(jnp.bfloat16)
    y = _dot(pick, buf_ref[slot].astype(jnp.bfloat16))
    o_ref[...] = x1_ref[...] + ga_ref[0] * _rms(y, g_ref[...])


def _combine(ys, tab_flat, lpos, x1, mod3, g_post, *, seq):
    t, d = x1.shape
    tile = DISPATCH_TILE
    n_steps = t // tile
    per_seq = seq // tile
    n_local = tile * TOP_K + N_EXPERTS * ROW_ALIGN
    return pl.pallas_call(
        _combine_kernel,
        grid=(n_steps,),
        in_specs=[pl.BlockSpec((TAB_WIDTH,), lambda i: (i,), memory_space=pltpu.SMEM),
                  pl.BlockSpec((TAB_WIDTH,), lambda i: (jnp.minimum(i + 1, n_steps - 1),), memory_space=pltpu.SMEM),
                  pl.BlockSpec((tile, TOP_K), lambda i: (i, 0)),
                  pl.BlockSpec((tile, d), lambda i: (i, 0)),
                  pl.BlockSpec((1, 1, d), lambda i: (i // per_seq, 0, 5)),
                  pl.BlockSpec((1, d), lambda i: (0, 0)),
                  pl.BlockSpec(memory_space=pl.ANY)],
        out_specs=pl.BlockSpec((tile, d), lambda i: (i, 0)),
        out_shape=jax.ShapeDtypeStruct((t, d), jnp.float32),
        scratch_shapes=[pltpu.VMEM((2, n_local, ys.shape[1]), ys.dtype), pltpu.SemaphoreType.DMA((2,))],
        compiler_params=_params("arbitrary"),
        name="combine",
    )(tab_flat, tab_flat, lpos, x1, mod3, g_post, ys)


def _rope_tables(seq):
    pos = np.arange(seq)
    n_freq = HEAD_DIM // 4
    freqs = ROPE_BASE ** (-jnp.arange(n_freq, dtype=jnp.float32) / n_freq)
    rows = jnp.asarray(pos // GRID_W, jnp.float32)[:, None] * freqs[None, :]
    cols = jnp.asarray(pos % GRID_W, jnp.float32)[:, None] * freqs[None, :]
    ang = jnp.concatenate([rows, rows, cols, cols], axis=1)
    sign = np.tile(np.repeat([-1.0, 1.0], n_freq), 2).astype(np.float32)
    cos = jnp.cos(ang)
    sin = jnp.sin(ang) * sign[None, :]
    reps = LANES // HEAD_DIM
    return jnp.tile(cos, (1, reps)), jnp.tile(sin, (1, reps))


def kernel(x, c, ctx, c_ctx, w_ada, b_ada, g_pre_mix, g_post_mix, g_pre_ffn, g_post_ffn, w_in, g_grp_a, g_grp_b,
           sink_a, rpb_b, w_out, w_router, b_router, w_mlp1, b_mlp1, w_mlp2, b_mlp2):
    batch, seq, d = x.shape
    n_ctx = ctx.shape[1]
    assert w_ada.shape[0] == 1, "single layer"
    assert seq % (ATT_BLOCKS * GRID_W * NB_Q_ROWS) == 0 and seq // GRID_W >= NB_K_ROWS
    assert seq % (ATT_BLOCKS * BLOCK_A) == 0 and seq >= BLOCK_A + 2 * WINDOW
    n_tok = batch * seq
    bf16 = jnp.bfloat16

    head_order = np.asarray([k * GQA_GROUP + j for j in range(GQA_GROUP) for k in range(N_KV_A)])
    col_order = (head_order[:, None] * HEAD_DIM + np.arange(HEAD_DIM)[None, :]).reshape(-1)

    mod_rows = -(-(batch + 1) // 8) * 8
    cc = jnp.zeros((mod_rows, d), jnp.float32).at[:batch].set(c).at[batch].set(c_ctx)
    mod3 = _ada(cc, w_ada[0], b_ada[0]).reshape(mod_rows, 1, 6 * d)

    w_in0 = w_in[0]
    w_lat = jnp.concatenate([w_in0[:, :WIDTH_A][:, col_order], w_in0[:, WIDTH_A:]], axis=1).astype(bf16)
    kv_cols = np.concatenate([np.arange(WIDTH_A, WIDTH_A + 2 * WIDTH_KV_A),
                              np.arange(WIDTH_A + 2 * WIDTH_KV_A + WIDTH_B, w_in0.shape[1])])
    w_ctx = w_in0[:, kv_cols].astype(bf16)
    g_pre = g_pre_mix[0].reshape(1, d)
    qa, ka, va, qb, kb, vb = _inproj(x.reshape(n_tok, d), mod3, g_pre, w_lat, _rope_tables(seq),
                                     seq=seq, mod_row0=0, latent=True)
    kac, vac, kbc, vbc = _inproj(ctx.reshape(batch * n_ctx, d), mod3, g_pre, w_ctx, None,
                                 seq=n_ctx, mod_row0=batch, latent=False)

    oa = _window_attention(qa, ka.reshape(batch, seq, -1), va.reshape(batch, seq, -1),
                           kac.reshape(batch, n_ctx, -1), vac.reshape(batch, n_ctx, -1),
                           sink_a[0].astype(jnp.float32) * LOG2E, batch=batch, seq=seq)
    tabs, cls_of_block = _nb_bias_tables(rpb_b[0], seq // GRID_W)
    ob = _neighbourhood_attention(qb, kb.reshape(batch, seq, -1), vb.reshape(batch, seq, -1),
                                  kbc.reshape(batch, n_ctx, -1), vbc.reshape(batch, n_ctx, -1),
                                  tabs, cls_of_block, batch=batch, seq=seq)

    w_out0 = w_out[0]
    w_o = jnp.concatenate([w_out0[:WIDTH_A][col_order], w_out0[WIDTH_A:]], axis=0).astype(bf16)
    x1, h2, idx, wts, rank, size, base, counts = _outproj(
        oa, ob, x.reshape(n_tok, d), mod3,
        g_grp_a[0][col_order].reshape(1, -1), g_grp_b[0].reshape(1, -1),
        g_post_mix[0].reshape(1, d), g_pre_ffn[0].reshape(1, d),
        w_o, w_router[0], b_router[0].reshape(1, -1), seq=seq)

    n_tok_tiles = n_tok // DISPATCH_TILE
    size = size.reshape(n_tok_tiles, N_EXPERTS)
    counts = counts.reshape(-1)
    padded = (counts + FFN_STEP - 1) // FFN_STEP * FFN_STEP
    pad_end = jnp.cumsum(padded).astype(jnp.int32)
    pad_start = pad_end - padded
    n_steps = (n_tok * TOP_K + n_tok_tiles * N_EXPERTS * (ROW_ALIGN - 1)) // FFN_STEP + N_EXPERTS
    n_used = (pad_end[-1:] // FFN_STEP).astype(jnp.int32)
    step_row0 = jnp.arange(n_steps, dtype=jnp.int32) * FFN_STEP
    step_expert = jnp.minimum(jnp.sum(step_row0[:, None] >= pad_end[None, :], axis=1),
                              N_EXPERTS - 1).astype(jnp.int32)
    own = step_expert[:, None] == jnp.arange(N_EXPERTS, dtype=jnp.int32)
    step_rows = jnp.clip(jnp.sum(jnp.where(own, (pad_start + counts)[None, :], 0), axis=1) - step_row0,
                         0, FFN_STEP).astype(jnp.int32)
    fill_from = (pad_start + jnp.maximum((counts + FFN_TILE - 1) // FFN_TILE * FFN_TILE - FFN_TILE, 0)
                 ).astype(jnp.int32)
    local0 = jnp.cumsum(size, axis=1) - size
    global0 = pad_start[None, :] + base.reshape(n_tok_tiles, N_EXPERTS)
    tab = jnp.concatenate([local0, global0, size, jnp.sum(size, axis=1, keepdims=True),
                           jnp.zeros((n_tok_tiles, TAB_WIDTH - 3 * N_EXPERTS - 1), jnp.int32)], axis=1)
    tab = tab.reshape(-1).astype(jnp.int32)
    hot = idx[:, :, None, :] == jnp.arange(N_EXPERTS, dtype=jnp.int32)[None, None, :, None]
    lpos_t = rank + jnp.sum(jnp.where(hot, local0[:, None, :, None], 0), axis=2)
    lpos = jnp.swapaxes(lpos_t, 1, 2).reshape(n_tok, TOP_K)

    xs = _dispatch(h2, tab, lpos_t, wts, fill_from, pad_end, n_rows=n_steps * FFN_STEP)
    ys = _ffn(xs, step_expert, step_rows, n_used, w_mlp1[0], b_mlp1[0], w_mlp2[0], b_mlp2[0])
    out = _combine(ys, tab, lpos, x1, mod3, g_post_ffn[0].reshape(1, d), seq=seq)
    return out.reshape(batch, seq, d)
```

```python
import functools

import numpy as np
import jax
import jax.numpy as jnp
from jax import lax
from jax.experimental import pallas as pl
from jax.experimental.pallas import tpu as pltpu

GRID_W = 64
HEAD_DIM = 64
N_HEADS_A = 8
N_KV_A = 2
GQA_GROUP = N_HEADS_A // N_KV_A
N_HEADS_B = 8
WIDTH_A = N_HEADS_A * HEAD_DIM
WIDTH_KV_A = N_KV_A * HEAD_DIM
WIDTH_B = N_HEADS_B * HEAD_DIM
WINDOW = 128
BLOCK_A = 128
NA_ROWS = 8
NA_COLS = 16
N_EXPERTS = 32
TOP_K = 4
SWIGLU_LIMIT = 7.0
SWIGLU_ALPHA = 1.702
ROPE_BASE = 10000.0
EPS = 1e-6
NEG_INF = -1e30

LANES = 128
VMEM_LIMIT = 56 * 1024 * 1024

LOG2E = 1.4426950408889634
ATT_BLOCKS = 2
NB_Q_ROWS = 2
NB_K_ROWS = NB_Q_ROWS + NA_ROWS
PROJ_TILE = 512
OUT_TILE = 256
FFN_TILE = 256
FFN_STEP = 4 * FFN_TILE
FFN_CHUNK = 256
DISPATCH_TILE = OUT_TILE
ROW_ALIGN = 8
TAB_WIDTH = 128


def _params(*sem):
    return pltpu.CompilerParams(dimension_semantics=sem, vmem_limit_bytes=VMEM_LIMIT)


def _rms(x, g):
    return x * lax.rsqrt(jnp.mean(x * x, axis=-1, keepdims=True) + EPS) * g


def _dot(a, b):
    return jnp.dot(a, b, preferred_element_type=jnp.float32)


def _dot_nt(a, b):
    return lax.dot_general(a, b, (((1,), (1,)), ((), ())), preferred_element_type=jnp.float32)


def _ada_kernel(c_ref, w_ref, b_ref, o_ref):
    c = c_ref[...]
    s = (c * jax.nn.sigmoid(c)).astype(jnp.bfloat16)
    o_ref[...] = _dot(s, w_ref[...].astype(jnp.bfloat16)) + b_ref[...]


def _ada(cc, w_ada, b_ada):
    rows, d = cc.shape
    n_out = w_ada.shape[1]
    return pl.pallas_call(
        _ada_kernel,
        grid=(n_out // d,),
        in_specs=[pl.BlockSpec((rows, d), lambda j: (0, 0)),
                  pl.BlockSpec((d, d), lambda j: (0, j)),
                  pl.BlockSpec((1, d), lambda j: (0, j))],
        out_specs=pl.BlockSpec((rows, d), lambda j: (0, j)),
        out_shape=jax.ShapeDtypeStruct((rows, n_out), jnp.float32),
        compiler_params=_params("arbitrary"),
        name="ada",
    )(cc, w_ada, b_ada.reshape(1, n_out))


def _rope(x, cos, sin):
    w = x.shape[1]
    reps = w // LANES
    if reps > 1:
        cos = jnp.concatenate([cos] * reps, axis=1)
        sin = jnp.concatenate([sin] * reps, axis=1)
    lane = lax.broadcasted_iota(jnp.int32, x.shape, 1)
    quarter = HEAD_DIM // 4
    partner = jnp.where(lane % (2 * quarter) < quarter,
                        pltpu.roll(x, w - quarter, 1), pltpu.roll(x, quarter, 1))
    return x * cos + partner * sin


def _inproj_kernel(x_ref, sh_ref, sc_ref, g_ref, w_ref, *rest, latent):
    x = x_ref[...]
    h = _rms(x, g_ref[...]) * (1.0 + sc_ref[0]) + sh_ref[0]
    p = _dot(h.astype(jnp.bfloat16), w_ref[...])
    if latent:
        cos_ref, sin_ref, qa_ref, ka_ref, va_ref, qb_ref, kb_ref, vb_ref = rest
        cos, sin = cos_ref[...], sin_ref[...]
        scale = HEAD_DIM ** -0.5 * LOG2E
        o = 0
        qa_ref[...] = (_rope(p[:, o:o + WIDTH_A], cos, sin) * scale).astype(qa_ref.dtype)
        o += WIDTH_A
        ka_ref[...] = _rope(p[:, o:o + WIDTH_KV_A], cos, sin).astype(ka_ref.dtype)
        o += WIDTH_KV_A
        va_ref[...] = p[:, o:o + WIDTH_KV_A].astype(va_ref.dtype)
        o += WIDTH_KV_A
        qb_ref[...] = (p[:, o:o + WIDTH_B] * scale).astype(qb_ref.dtype)
        o += WIDTH_B
    else:
        ka_ref, va_ref, kb_ref, vb_ref = rest
        o = 0
        ka_ref[...] = p[:, o:o + WIDTH_KV_A].astype(ka_ref.dtype)
        o += WIDTH_KV_A
        va_ref[...] = p[:, o:o + WIDTH_KV_A].astype(va_ref.dtype)
        o += WIDTH_KV_A
    kb_ref[...] = p[:, o:o + WIDTH_B].astype(kb_ref.dtype)
    o += WIDTH_B
    vb_ref[...] = p[:, o:o + WIDTH_B].astype(vb_ref.dtype)


def _inproj(x2d, mod3, g_pre, w, rope_tabs, *, seq, mod_row0, latent):
    t, d = x2d.shape
    tile = min(PROJ_TILE, seq)
    per_seq = seq // tile
    if latent:
        mod_row = lambda i: i // per_seq
    else:
        mod_row = lambda i: mod_row0
    in_specs = [pl.BlockSpec((tile, d), lambda i: (i, 0)),
                pl.BlockSpec((1, 1, d), lambda i: (mod_row(i), 0, 0)),
                pl.BlockSpec((1, 1, d), lambda i: (mod_row(i), 0, 1)),
                pl.BlockSpec((1, d), lambda i: (0, 0)),
                pl.BlockSpec(w.shape, lambda i: (0, 0))]
    args = [x2d, mod3, mod3, g_pre, w]
    widths = [WIDTH_KV_A, WIDTH_KV_A, WIDTH_B, WIDTH_B]
    if latent:
        in_specs += [pl.BlockSpec((tile, LANES), lambda i: (i % per_seq, 0))] * 2
        args += list(rope_tabs)
        widths = [WIDTH_A, WIDTH_KV_A, WIDTH_KV_A, WIDTH_B, WIDTH_B, WIDTH_B]
    return pl.pallas_call(
        functools.partial(_inproj_kernel, latent=latent),
        grid=(t // tile,),
        in_specs=in_specs,
        out_specs=[pl.BlockSpec((tile, wd), lambda i: (i, 0)) for wd in widths],
        out_shape=[jax.ShapeDtypeStruct((t, wd), jnp.bfloat16) for wd in widths],
        compiler_params=_params("parallel"),
        name="inproj_latent" if latent else "inproj_ctx",
    )(*args)


def _ones_beside(v):
    return jnp.concatenate([v, jnp.ones_like(v)], axis=1)


def _group_attention(q_ref, o_ref, rows, groups, s_ref, p_ref, m_ref):
    tq = rows.stop - rows.start
    low = lax.broadcasted_iota(jnp.int32, (tq, LANES), 1) < HEAD_DIM
    layout = []
    base = 0
    for pairs, k_loc, k_ctx, _, _, _, _ in groups:
        heads = [(j, 0) for j in pairs] + [(j, 1) for j in pairs]
        n_loc = k_loc.shape[0]
        stacked = []
        for j, half in heads:
            q = q_ref[rows, j * LANES:(j + 1) * LANES]
            stacked.append(jnp.where(low if half == 0 else ~low, q, jnp.zeros_like(q)))
        qs = jnp.concatenate(stacked, axis=0)
        n = len(heads) * tq
        s_ref[base:base + n, :n_loc] = _dot_nt(qs, k_loc)
        s_ref[base:base + n, n_loc:] = _dot_nt(qs, k_ctx)
        layout.append((base, heads, n_loc))
        base += n
    for (base, heads, n_loc), group in zip(layout, groups):
        bias_of, sink_of = group[5], group[6]
        for g, (j, half) in enumerate(heads):
            r = slice(base + g * tq, base + (g + 1) * tq)
            s_loc = s_ref[r, :n_loc] + bias_of(j, half)
            s_ctx = s_ref[r, n_loc:]
            m = jnp.maximum(jnp.max(s_loc, axis=1, keepdims=True), jnp.max(s_ctx, axis=1, keepdims=True))
            sink = sink_of(j, half)
            if sink is not None:
                m = jnp.maximum(m, sink)
            p_ref[r, :n_loc] = jnp.exp2((s_loc - m).astype(jnp.bfloat16))
            p_ref[r, n_loc:] = jnp.exp2((s_ctx - m).astype(jnp.bfloat16))
            m_ref[r, :] = m
    for (base, heads, n_loc), group in zip(layout, groups):
        pairs, _, _, v_loc, v_ctx, _, sink_of = group
        n = len(heads) * tq
        both = (_dot(p_ref[base:base + n, :n_loc], _ones_beside(v_loc))
                + _dot(p_ref[base:base + n, n_loc:], _ones_beside(v_ctx)))
        for a, j in enumerate(pairs):
            outs = []
            for half in range(2):
                g = half * len(pairs) + a
                r = slice(g * tq, (g + 1) * tq)
                den = both[r, LANES:]
                sink = sink_of(j, half)
                if sink is not None:
                    den = den + jnp.exp2(sink - m_ref[base + g * tq:base + (g + 1) * tq, :])
                outs.append(both[r, :LANES] / den)
            o_ref[rows, j * LANES:(j + 1) * LANES] = jnp.where(low, outs[0], outs[1]).astype(o_ref.dtype)


def _attention_scratch(tq, n_heads, n_keys):
    return [pltpu.VMEM((n_heads * tq, n_keys), jnp.float32), pltpu.VMEM((n_heads * tq, n_keys), jnp.bfloat16),
            pltpu.VMEM((n_heads * tq, 1), jnp.float32)]


def _win_kernel(sink_ref, q_ref, k_ref, v_ref, kc_ref, vc_ref, o_ref, *scratch, seq):
    span = BLOCK_A + 2 * WINDOW
    pairs = list(range(GQA_GROUP))
    for sb in range(ATT_BLOCKS):
        i = pl.program_id(1) * ATT_BLOCKS + sb
        start = pl.multiple_of(jnp.clip(i * BLOCK_A - WINDOW, 0, seq - span), BLOCK_A)
        qpos = i * BLOCK_A + lax.broadcasted_iota(jnp.int32, (BLOCK_A, span), 0)
        kpos = start + lax.broadcasted_iota(jnp.int32, (BLOCK_A, span), 1)
        bias = jnp.where(jnp.abs(kpos - qpos) <= WINDOW, 0.0, NEG_INF).astype(jnp.float32)
        groups = [([j], k_ref[0, pl.ds(start, span), :], kc_ref[0], v_ref[0, pl.ds(start, span), :], vc_ref[0],
                   lambda j, half: bias, lambda j, half: sink_ref[half * GQA_GROUP + j]) for j in pairs]
        _group_attention(q_ref, o_ref, slice(sb * BLOCK_A, (sb + 1) * BLOCK_A), groups, *scratch)


def _window_attention(qa, ka, va, kac, vac, sink, *, batch, seq):
    tq = ATT_BLOCKS * BLOCK_A
    nb = seq // tq
    n_ctx = kac.shape[1]
    return pl.pallas_call(
        functools.partial(_win_kernel, seq=seq),
        grid=(batch, nb),
        in_specs=[pl.BlockSpec(memory_space=pltpu.SMEM),
                  pl.BlockSpec((tq, WIDTH_A), lambda b, i: (b * nb + i, 0)),
                  pl.BlockSpec((1, seq, WIDTH_KV_A), lambda b, i: (b, 0, 0)),
                  pl.BlockSpec((1, seq, WIDTH_KV_A), lambda b, i: (b, 0, 0)),
                  pl.BlockSpec((1, n_ctx, WIDTH_KV_A), lambda b, i: (b, 0, 0)),
                  pl.BlockSpec((1, n_ctx, WIDTH_KV_A), lambda b, i: (b, 0, 0))],
        out_specs=pl.BlockSpec((tq, WIDTH_A), lambda b, i: (b * nb + i, 0)),
        out_shape=jax.ShapeDtypeStruct((batch * seq, WIDTH_A), jnp.bfloat16),
        scratch_shapes=_attention_scratch(BLOCK_A, N_HEADS_A, BLOCK_A + 2 * WINDOW + n_ctx),
        compiler_params=_params("parallel", "arbitrary"),
        name="window_attention",
    )(sink, qa, ka, va, kac, vac)


def _nb_kernel(q_ref, k_ref, v_ref, kc_ref, vc_ref, *rest, rows_n):
    tab_refs, o_ref, scratch = rest[:ATT_BLOCKS], rest[ATT_BLOCKS], rest[ATT_BLOCKS + 1:]
    n_keys = NB_K_ROWS * GRID_W
    tq = NB_Q_ROWS * GRID_W
    for sb in range(ATT_BLOCKS):
        m = pl.program_id(1) * ATT_BLOCKS + sb
        start_row = jnp.clip(NB_Q_ROWS * m - NA_ROWS // 2, 0, rows_n - NB_K_ROWS)
        start = pl.multiple_of(start_row * GRID_W, LANES)
        tab_ref = tab_refs[sb]
        groups = []
        for j in range(q_ref.shape[1] // LANES):
            cols = slice(j * LANES, (j + 1) * LANES)
            groups.append(([j], k_ref[0, pl.ds(start, n_keys), cols], kc_ref[0, :, cols],
                           v_ref[0, pl.ds(start, n_keys), cols], vc_ref[0, :, cols],
                           lambda j, half: tab_ref[0, 2 * j + half].astype(jnp.float32), lambda j, half: None))
        _group_attention(q_ref, o_ref, slice(sb * tq, (sb + 1) * tq), groups, *scratch)


def _nb_classes(rows_n):
    n_blocks = rows_n // NB_Q_ROWS
    sig = {}
    cls_of_block = []
    reps = []
    for m in range(n_blocks):
        start_row = int(np.clip(NB_Q_ROWS * m - NA_ROWS // 2, 0, rows_n - NB_K_ROWS))
        key = tuple((start_row - r, int(np.clip(r - NA_ROWS // 2, 0, rows_n - NA_ROWS)) - r)
                    for r in range(NB_Q_ROWS * m, NB_Q_ROWS * (m + 1)))
        if key not in sig:
            sig[key] = len(reps)
            reps.append(m)
        cls_of_block.append(sig[key])
    return np.asarray(cls_of_block, np.int32), reps


def _nb_bias_tables(rpb, rows_n):
    cls_of_block, reps = _nb_classes(rows_n)
    n_heads = rpb.shape[0]
    cq = np.arange(GRID_W)[:, None]
    ck = np.arange(GRID_W)[None, :]
    cs = np.clip(cq - NA_COLS // 2, 0, GRID_W - NA_COLS)
    col_ok = (ck >= cs) & (ck < cs + NA_COLS)
    pick = ((ck - cq + NA_COLS - 1)[None] == np.arange(2 * NA_COLS - 1)[:, None, None]) & col_ok[None]
    tiles = jnp.einsum('hrd,dqk->hrqk', rpb.astype(jnp.float32), jnp.asarray(pick, jnp.float32),
                       precision=lax.Precision.HIGHEST)
    tiles = jnp.where(jnp.asarray(col_ok)[None, None], tiles * LOG2E, NEG_INF).astype(jnp.bfloat16)
    blocked = jnp.full((n_heads, GRID_W, GRID_W), NEG_INF, jnp.bfloat16)
    tabs = []
    for m in reps:
        start_row = int(np.clip(NB_Q_ROWS * m - NA_ROWS // 2, 0, rows_n - NB_K_ROWS))
        q_rows = []
        for r in range(NB_Q_ROWS * m, NB_Q_ROWS * (m + 1)):
            rs = int(np.clip(r - NA_ROWS // 2, 0, rows_n - NA_ROWS))
            q_rows.append(jnp.concatenate(
                [tiles[:, krow - r + NA_ROWS - 1] if rs <= krow < rs + NA_ROWS else blocked
                 for krow in range(start_row, start_row + NB_K_ROWS)], axis=2))
        tabs.append(jnp.concatenate(q_rows, axis=1))
    return jnp.stack(tabs), cls_of_block


def _neighbourhood_attention(qb, kb, vb, kbc, vbc, tabs, cls_of_block, *, batch, seq):
    rows_n = seq // GRID_W
    tq = ATT_BLOCKS * NB_Q_ROWS * GRID_W
    nstep = seq // tq
    n_ctx = kbc.shape[1]

    def tab_spec(sb):
        return pl.BlockSpec((1,) + tabs.shape[1:], lambda b, m, c: (c[m * ATT_BLOCKS + sb], 0, 0, 0))

    grid_spec = pltpu.PrefetchScalarGridSpec(
        num_scalar_prefetch=1,
        grid=(batch, nstep),
        in_specs=[pl.BlockSpec((tq, WIDTH_B), lambda b, m, c: (b * nstep + m, 0)),
                  pl.BlockSpec((1, seq, WIDTH_B), lambda b, m, c: (b, 0, 0)),
                  pl.BlockSpec((1, seq, WIDTH_B), lambda b, m, c: (b, 0, 0)),
                  pl.BlockSpec((1, n_ctx, WIDTH_B), lambda b, m, c: (b, 0, 0)),
                  pl.BlockSpec((1, n_ctx, WIDTH_B), lambda b, m, c: (b, 0, 0))]
                 + [tab_spec(sb) for sb in range(ATT_BLOCKS)],
        out_specs=pl.BlockSpec((tq, WIDTH_B), lambda b, m, c: (b * nstep + m, 0)),
        scratch_shapes=_attention_scratch(NB_Q_ROWS * GRID_W, N_HEADS_B, NB_K_ROWS * GRID_W + n_ctx),
    )

    def body(c_ref, *refs):
        _nb_kernel(*refs, rows_n=rows_n)

    return pl.pallas_call(
        body,
        grid_spec=grid_spec,
        out_shape=jax.ShapeDtypeStruct((batch * seq, WIDTH_B), jnp.bfloat16),
        compiler_params=_params("parallel", "arbitrary"),
        name="neighbourhood_attention",
    )(jnp.asarray(cls_of_block), qb, kb, vb, kbc, vbc, *([tabs] * ATT_BLOCKS))


def _outproj_kernel(oa_ref, ob_ref, x_ref, ga_ref, sh_ref, sc_ref, gga_ref, ggb_ref, gpost_ref, gpre_ref,
                    wo_ref, wr_hi_ref, wr_lo_ref, br_ref,
                    x1_ref, h2_ref, idx_ref, wts_ref, rank_ref, size_ref, base_ref, cnt_ref, run_ref):
    i = pl.program_id(0)

    @pl.when(i == 0)
    def _():
        run_ref[...] = jnp.zeros_like(run_ref)

    na = _rms(oa_ref[...].astype(jnp.float32), gga_ref[...]).astype(jnp.bfloat16)
    nb = _rms(ob_ref[...].astype(jnp.float32), ggb_ref[...]).astype(jnp.bfloat16)
    mix = _dot(na, wo_ref[:WIDTH_A, :]) + _dot(nb, wo_ref[WIDTH_A:, :])
    x1 = x_ref[...] + ga_ref[0] * _rms(mix, gpost_ref[...])
    x1_ref[...] = x1
    h2 = _rms(x1, gpre_ref[...]) * (1.0 + sc_ref[0]) + sh_ref[0]
    h2_ref[...] = h2.astype(h2_ref.dtype)

    h_hi = h2.astype(jnp.bfloat16)
    h_lo = (h2 - h_hi.astype(jnp.float32)).astype(jnp.bfloat16)
    logits = _dot(h_hi, wr_hi_ref[...]) + _dot(h_lo, wr_hi_ref[...]) + _dot(h_hi, wr_lo_ref[...])
    work = logits.T[:N_EXPERTS] + br_ref[...]
    e, t = work.shape
    row = lax.broadcasted_iota(jnp.int32, (e, t), 0)
    chosen = jnp.zeros((e, t), jnp.float32)
    vals, sels, hots = [], [], []
    for _k in range(TOP_K):
        mx = jnp.max(work, axis=0, keepdims=True)
        sel = jnp.min(jnp.where(work == mx, row, e), axis=0, keepdims=True)
        hot = row == sel
        vals.append(mx)
        sels.append(sel)
        hots.append(hot)
        work = jnp.where(hot, -jnp.inf, work)
        chosen = chosen + hot.astype(jnp.float32)
    ex = [jnp.exp(v - vals[0]) for v in vals]
    den = ex[0] + ex[1] + ex[2] + ex[3]

    earlier = (lax.broadcasted_iota(jnp.int32, (t, t), 0) < lax.broadcasted_iota(jnp.int32, (t, t), 1))
    prefix = _dot(chosen.astype(jnp.bfloat16), earlier.astype(jnp.bfloat16))
    ranks = [jnp.sum(jnp.where(h, prefix, 0.0), axis=0, keepdims=True) for h in hots]
    size = jnp.sum(chosen, axis=1, keepdims=True).astype(jnp.int32)
    size = (size + (ROW_ALIGN - 1)) // ROW_ALIGN * ROW_ALIGN
    size_ref[0] = size
    base_ref[0] = run_ref[...]
    run_ref[...] = run_ref[...] + size
    cnt_ref[...] = run_ref[...]

    idx_ref[0] = jnp.concatenate(sels, axis=0)
    wts_ref[0] = jnp.concatenate([v / den for v in ex], axis=0)
    rank_ref[0] = jnp.concatenate(ranks, axis=0).astype(jnp.int32)


def _outproj(oa, ob, x2d, mod3, gga, ggb, gpost, gpre, w_out, w_router, b_router, *, seq):
    t, d = x2d.shape
    tile = OUT_TILE
    per_seq = seq // tile
    row = lambda i: (i, 0)
    const = lambda i: (0, 0)
    modspec = lambda part: pl.BlockSpec((1, 1, d), lambda i: (i // per_seq, 0, part))
    per_tile = lambda shape: pl.BlockSpec((1,) + shape, lambda i: (i, 0, 0))
    n_steps = t // tile
    w_pad = jnp.zeros((d, LANES), jnp.float32).at[:, :N_EXPERTS].set(w_router)
    wr_hi = w_pad.astype(jnp.bfloat16)
    wr_lo = (w_pad - wr_hi.astype(jnp.float32)).astype(jnp.bfloat16)
    outs = pl.pallas_call(
        _outproj_kernel,
        grid=(n_steps,),
        in_specs=[pl.BlockSpec((tile, WIDTH_A), row), pl.BlockSpec((tile, WIDTH_B), row),
                  pl.BlockSpec((tile, d), row),
                  modspec(2), modspec(3), modspec(4),
                  pl.BlockSpec((1, WIDTH_A), const), pl.BlockSpec((1, WIDTH_B), const),
                  pl.BlockSpec((1, d), const), pl.BlockSpec((1, d), const),
                  pl.BlockSpec(w_out.shape, const), pl.BlockSpec((d, LANES), const), pl.BlockSpec((d, LANES), const),
                  pl.BlockSpec((N_EXPERTS, 1), const)],
        out_specs=[pl.BlockSpec((tile, d), row), pl.BlockSpec((tile, d), row),
                   per_tile((TOP_K, tile)), per_tile((TOP_K, tile)), per_tile((TOP_K, tile)),
                   per_tile((N_EXPERTS, 1)), per_tile((N_EXPERTS, 1)),
                   pl.BlockSpec((N_EXPERTS, 1), const)],
        out_shape=[jax.ShapeDtypeStruct((t, d), jnp.float32), jax.ShapeDtypeStruct((t, d), jnp.bfloat16),
                   jax.ShapeDtypeStruct((n_steps, TOP_K, tile), jnp.int32),
                   jax.ShapeDtypeStruct((n_steps, TOP_K, tile), jnp.float32),
                   jax.ShapeDtypeStruct((n_steps, TOP_K, tile), jnp.int32),
                   jax.ShapeDtypeStruct((n_steps, N_EXPERTS, 1), jnp.int32),
                   jax.ShapeDtypeStruct((n_steps, N_EXPERTS, 1), jnp.int32),
                   jax.ShapeDtypeStruct((N_EXPERTS, 1), jnp.int32)],
        scratch_shapes=[pltpu.VMEM((N_EXPERTS, 1), jnp.int32)],
        compiler_params=_params("arbitrary"),
        name="outproj_router",
    )(oa, ob, x2d, mod3, mod3, mod3, gga, ggb, gpost, gpre, w_out, wr_hi, wr_lo, b_router.reshape(N_EXPERTS, 1))
    return outs


def _piece_table(tab_ref, e):
    return (pl.multiple_of(tab_ref[e], ROW_ALIGN), pl.multiple_of(tab_ref[N_EXPERTS + e], ROW_ALIGN),
            pl.multiple_of(tab_ref[2 * N_EXPERTS + e], ROW_ALIGN))


def _dispatch_kernel(fill_from_ref, pad_end_ref, tab_ref, h_ref, lpos_ref, w_ref, xs_ref,
                     stage_ref, zero_ref, pend_ref, sems, zsem):
    i = pl.program_id(0)
    n_steps = pl.num_programs(0)
    slot = i % 2
    n_local, width = stage_ref.shape[1:]
    d = h_ref.shape[1]

    def fill(row):
        return pltpu.make_async_copy(zero_ref, xs_ref.at[pl.ds(pl.multiple_of(row, FFN_TILE), FFN_TILE)], zsem)

    def fills(e):
        return (pad_end_ref[e] - fill_from_ref[e]) // FFN_TILE

    def drain(s):
        n = pl.multiple_of(pend_ref[s], ROW_ALIGN)

        @pl.when(n > 0)
        def _():
            pltpu.make_async_copy(stage_ref.at[s, pl.ds(0, n)], xs_ref.at[pl.ds(0, n)], sems.at[s]).wait()
        pend_ref[s] = 0

    @pl.when(i == 0)
    def _():
        zero_ref[...] = jnp.zeros_like(zero_ref)
        pend_ref[0] = 0
        pend_ref[1] = 0

        def start_e(e, c):
            def start_j(j, c2):
                fill(fill_from_ref[e] + j * FFN_TILE).start()
                return c2
            lax.fori_loop(0, fills(e), start_j, 0)
            return c + fills(e)
        n_fill = lax.fori_loop(0, N_EXPERTS, start_e, 0)

        def wait_all(j, c):
            fill(0).wait()
            return c
        lax.fori_loop(0, n_fill, wait_all, 0)

    row = lax.broadcasted_iota(jnp.int32, (n_local, h_ref.shape[0]), 0)
    place = None
    row_w = None
    for k in range(TOP_K):
        hit = row == lpos_ref[0, k:k + 1, :]
        place = hit if place is None else place | hit
        wk = jnp.sum(jnp.where(hit, w_ref[0, k:k + 1, :], 0.0), axis=1, keepdims=True)
        row_w = wk if row_w is None else row_w + wk
    rows = _dot(place.astype(jnp.bfloat16), h_ref[...])

    drain(slot)
    stage_ref[slot, :, :d] = rows
    stage_ref[slot, :, d:] = jnp.broadcast_to(row_w, (n_local, width - d))

    def piece(e, total):
        loc, glob, n = _piece_table(tab_ref, e)

        @pl.when(n > 0)
        def _():
            pltpu.make_async_copy(stage_ref.at[slot, pl.ds(loc, n)], xs_ref.at[pl.ds(glob, n)], sems.at[slot]).start()
        return total + n
    pend_ref[slot] = lax.fori_loop(0, N_EXPERTS, piece, 0)

    tail0 = pad_end_ref[N_EXPERTS - 1]
    n_tail = (xs_ref.shape[0] - tail0) // FFN_TILE
    per_step = (n_tail + n_steps - 1) // n_steps

    def start_tail(j, c):
        t = i * per_step + j

        @pl.when(t < n_tail)
        def _():
            fill(tail0 + t * FFN_TILE).start()
        return c
    lax.fori_loop(0, per_step, start_tail, 0)

    @pl.when(i == n_steps - 1)
    def _():
        drain(0)
        drain(1)

        def wait_tail(j, c):
            fill(0).wait()
            return c
        lax.fori_loop(0, n_tail, wait_tail, 0)


def _dispatch(h2, tab_flat, lpos_t, w_t, fill_from, pad_end, *, n_rows):
    t, d = h2.shape
    tile = DISPATCH_TILE
    n_local = tile * TOP_K + N_EXPERTS * ROW_ALIGN
    width = d + LANES
    grid_spec = pltpu.PrefetchScalarGridSpec(
        num_scalar_prefetch=2,
        grid=(t // tile,),
        in_specs=[pl.BlockSpec((TAB_WIDTH,), lambda i, *_: (i,), memory_space=pltpu.SMEM),
                  pl.BlockSpec((tile, d), lambda i, *_: (i, 0)),
                  pl.BlockSpec((1, TOP_K, tile), lambda i, *_: (i, 0, 0)),
                  pl.BlockSpec((1, TOP_K, tile), lambda i, *_: (i, 0, 0))],
        out_specs=pl.BlockSpec(memory_space=pl.ANY),
        scratch_shapes=[pltpu.VMEM((2, n_local, width), jnp.float32),
                        pltpu.VMEM((FFN_TILE, width), jnp.float32),
                        pltpu.SMEM((2,), jnp.int32),
                        pltpu.SemaphoreType.DMA((2,)), pltpu.SemaphoreType.DMA(())],
    )
    return pl.pallas_call(
        _dispatch_kernel,
        grid_spec=grid_spec,
        out_shape=jax.ShapeDtypeStruct((n_rows, width), jnp.float32),
        compiler_params=_params("arbitrary"),
        name="dispatch",
    )(fill_from, pad_end, tab_flat, h2, lpos_t, w_t)


def _ffn_kernel(te_ref, rows_ref, n_used_ref, x_ref, w1_ref, b1_ref, w2_ref, b2_ref, y_ref,
                w1p_ref, w2b_ref, act_ref):
    i = pl.program_id(0)
    live = i < n_used_ref[0]
    n_pair = w1_ref.shape[2] // FFN_CHUNK
    half = FFN_CHUNK // 2
    d = w2_ref.shape[2]
    n_sub = jnp.where(live, (rows_ref[i] + FFN_TILE - 1) // FFN_TILE, 0)

    @pl.when(live & ((i == 0) | (te_ref[i] != te_ref[jnp.maximum(i - 1, 0)])))
    def _():
        src = lax.broadcasted_iota(jnp.int32, (FFN_CHUNK, FFN_CHUNK), 0)
        dst = lax.broadcasted_iota(jnp.int32, (FFN_CHUNK, FFN_CHUNK), 1)
        unzip = (src == jnp.where(dst < half, 2 * dst, 2 * (dst - half) + 1)).astype(jnp.bfloat16)
        for c in range(n_pair):
            cs = slice(c * FFN_CHUNK, (c + 1) * FFN_CHUNK)
            w1p_ref[:, cs] = _dot(w1_ref[0, :, cs].astype(jnp.bfloat16), unzip).astype(jnp.bfloat16)
        w2b_ref[...] = w2_ref[0].astype(jnp.bfloat16)

    def compute(n_rows):
        x = x_ref[:n_rows, :d].astype(jnp.bfloat16)
        for c in range(0, n_pair, 2):
            cs = slice(c * FFN_CHUNK, (c + 2) * FFN_CHUNK)
            h = _dot(x, w1p_ref[:, cs]) + b1_ref[0, :, cs]
            g = jnp.concatenate([h[:, :half], h[:, FFN_CHUNK:FFN_CHUNK + half]], axis=1)
            u = jnp.concatenate([h[:, half:FFN_CHUNK], h[:, FFN_CHUNK + half:]], axis=1)
            g = jnp.minimum(g, SWIGLU_LIMIT)
            u = jnp.clip(u, -SWIGLU_LIMIT, SWIGLU_LIMIT)
            act = g * jax.nn.sigmoid(SWIGLU_ALPHA * g) * (u + 1.0)
            act_ref[:n_rows, c * half:(c + 2) * half] = act.astype(act_ref.dtype)
        y_ref[:n_rows, :] = (_dot(act_ref[:n_rows, :], w2b_ref[...]) + b2_ref[0]) * x_ref[:n_rows, d:d + 1]
        if n_rows < y_ref.shape[0]:
            y_ref[n_rows:, :] = jnp.zeros((y_ref.shape[0] - n_rows, d), y_ref.dtype)

    for tiles in range(1, y_ref.shape[0] // FFN_TILE + 1):
        pl.when(n_sub == tiles)(functools.partial(compute, tiles * FFN_TILE))

    @pl.when(n_sub == 0)
    def _():
        y_ref[...] = jnp.zeros_like(y_ref)


def _unzip_bias(b1):
    e, f2 = b1.shape
    half = FFN_CHUNK // 2
    return b1.reshape(e, f2 // FFN_CHUNK, half, 2).transpose(0, 1, 3, 2).reshape(e, 1, f2)


def _ffn(xs, step_expert, step_rows, n_used, w1, b1, w2, b2):
    rows, width = xs.shape
    d, f2 = w1.shape[1:]
    n_steps = rows // FFN_STEP
    live = lambda i, nu: jnp.minimum(i, nu[0] - 1)
    wspec = lambda shape: pl.BlockSpec((1,) + shape, lambda i, te, nr, nu: (te[live(i, nu)], 0, 0))
    grid_spec = pltpu.PrefetchScalarGridSpec(
        num_scalar_prefetch=3,
        grid=(n_steps,),
        in_specs=[pl.BlockSpec((FFN_STEP, width), lambda i, te, nr, nu: (live(i, nu), 0)),
                  wspec((d, f2)), wspec((1, f2)), wspec((f2 // 2, d)), wspec((1, d))],
        out_specs=pl.BlockSpec((FFN_STEP, d), lambda i, te, nr, nu: (i, 0)),
        scratch_shapes=[pltpu.VMEM((d, f2), jnp.bfloat16), pltpu.VMEM((f2 // 2, d), jnp.bfloat16),
                        pltpu.VMEM((FFN_STEP, f2 // 2), jnp.bfloat16)],
    )
    return pl.pallas_call(
        _ffn_kernel,
        grid_spec=grid_spec,
        out_shape=jax.ShapeDtypeStruct((rows, d), jnp.float32),
        compiler_params=_params("arbitrary"),
        name="expert_ffn",
    )(step_expert, step_rows, n_used, xs, w1, _unzip_bias(b1), w2, b2[:, None, :])


def _combine_kernel(tab_ref, next_tab_ref, lpos_ref, x1_ref, ga_ref, g_ref, y_ref, o_ref, buf_ref, sems):
    i = pl.program_id(0)
    n_steps = pl.num_programs(0)
    slot = i % 2
    tile = x1_ref.shape[0]
    n_local = buf_ref.shape[1]

    def fetch(tab, s):
        def piece(e, c):
            loc, glob, n = _piece_table(tab, e)

            @pl.when(n > 0)
            def _():
                pltpu.make_async_copy(y_ref.at[pl.ds(glob, n)], buf_ref.at[s, pl.ds(loc, n)], sems.at[s]).start()
            return c
        lax.fori_loop(0, N_EXPERTS, piece, 0)

    @pl.when(i == 0)
    def _():
        buf_ref[...] = jnp.zeros_like(buf_ref)
        fetch(tab_ref, 0)

    @pl.when(i + 1 < n_steps)
    def _():
        fetch(next_tab_ref, 1 - slot)

    n = pl.multiple_of(tab_ref[3 * N_EXPERTS], ROW_ALIGN)

    @pl.when(n > 0)
    def _():
        pltpu.make_async_copy(y_ref.at[pl.ds(0, n)], buf_ref.at[slot, pl.ds(0, n)], sems.at[slot]).wait()

    col = lax.broadcasted_iota(jnp.int32, (tile, n_local), 1)
    pos = lpos_ref[...]
    pick = col == pos[:, 0:1]
    for k in range(1, TOP_K):
        pick = pick | (col == pos[:, k:k + 1])
    pick = pick.astype(jnp.bfloat16)
    y = _dot(pick, buf_ref[slot].astype(jnp.bfloat16))
    o_ref[...] = x1_ref[...] + ga_ref[0] * _rms(y, g_ref[...])


def _combine(ys, tab_flat, lpos, x1, mod3, g_post, *, seq):
    t, d = x1.shape
    tile = DISPATCH_TILE
    n_steps = t // tile
    per_seq = seq // tile
    n_local = tile * TOP_K + N_EXPERTS * ROW_ALIGN
    return pl.pallas_call(
        _combine_kernel,
        grid=(n_steps,),
        in_specs=[pl.BlockSpec((TAB_WIDTH,), lambda i: (i,), memory_space=pltpu.SMEM),
                  pl.BlockSpec((TAB_WIDTH,), lambda i: (jnp.minimum(i + 1, n_steps - 1),), memory_space=pltpu.SMEM),
                  pl.BlockSpec((tile, TOP_K), lambda i: (i, 0)),
                  pl.BlockSpec((tile, d), lambda i: (i, 0)),
                  pl.BlockSpec((1, 1, d), lambda i: (i // per_seq, 0, 5)),
                  pl.BlockSpec((1, d), lambda i: (0, 0)),
                  pl.BlockSpec(memory_space=pl.ANY)],
        out_specs=pl.BlockSpec((tile, d), lambda i: (i, 0)),
        out_shape=jax.ShapeDtypeStruct((t, d), jnp.float32),
        scratch_shapes=[pltpu.VMEM((2, n_local, ys.shape[1]), ys.dtype), pltpu.SemaphoreType.DMA((2,))],
        compiler_params=_params("arbitrary"),
        name="combine",
    )(tab_flat, tab_flat, lpos, x1, mod3, g_post, ys)


def _rope_tables(seq):
    pos = np.arange(seq)
    n_freq = HEAD_DIM // 4
    freqs = ROPE_BASE ** (-jnp.arange(n_freq, dtype=jnp.float32) / n_freq)
    rows = jnp.asarray(pos // GRID_W, jnp.float32)[:, None] * freqs[None, :]
    cols = jnp.asarray(pos % GRID_W, jnp.float32)[:, None] * freqs[None, :]
    ang = jnp.concatenate([rows, rows, cols, cols], axis=1)
    sign = np.tile(np.repeat([-1.0, 1.0], n_freq), 2).astype(np.float32)
    cos = jnp.cos(ang)
    sin = jnp.sin(ang) * sign[None, :]
    reps = LANES // HEAD_DIM
    return jnp.tile(cos, (1, reps)), jnp.tile(sin, (1, reps))


def kernel(x, c, ctx, c_ctx, w_ada, b_ada, g_pre_mix, g_post_mix, g_pre_ffn, g_post_ffn, w_in, g_grp_a, g_grp_b,
           sink_a, rpb_b, w_out, w_router, b_router, w_mlp1, b_mlp1, w_mlp2, b_mlp2):
    batch, seq, d = x.shape
    n_ctx = ctx.shape[1]
    assert w_ada.shape[0] == 1, "single layer"
    assert seq % (ATT_BLOCKS * GRID_W * NB_Q_ROWS) == 0 and seq // GRID_W >= NB_K_ROWS
    assert seq % (ATT_BLOCKS * BLOCK_A) == 0 and seq >= BLOCK_A + 2 * WINDOW
    n_tok = batch * seq
    bf16 = jnp.bfloat16

    head_order = np.asarray([k * GQA_GROUP + j for j in range(GQA_GROUP) for k in range(N_KV_A)])
    col_order = (head_order[:, None] * HEAD_DIM + np.arange(HEAD_DIM)[None, :]).reshape(-1)

    mod_rows = -(-(batch + 1) // 8) * 8
    cc = jnp.zeros((mod_rows, d), jnp.float32).at[:batch].set(c).at[batch].set(c_ctx)
    mod3 = _ada(cc, w_ada[0], b_ada[0]).reshape(mod_rows, 1, 6 * d)

    w_in0 = w_in[0]
    w_lat = jnp.concatenate([w_in0[:, :WIDTH_A][:, col_order], w_in0[:, WIDTH_A:]], axis=1).astype(bf16)
    kv_cols = np.concatenate([np.arange(WIDTH_A, WIDTH_A + 2 * WIDTH_KV_A),
                              np.arange(WIDTH_A + 2 * WIDTH_KV_A + WIDTH_B, w_in0.shape[1])])
    w_ctx = w_in0[:, kv_cols].astype(bf16)
    g_pre = g_pre_mix[0].reshape(1, d)
    qa, ka, va, qb, kb, vb = _inproj(x.reshape(n_tok, d), mod3, g_pre, w_lat, _rope_tables(seq),
                                     seq=seq, mod_row0=0, latent=True)
    kac, vac, kbc, vbc = _inproj(ctx.reshape(batch * n_ctx, d), mod3, g_pre, w_ctx, None,
                                 seq=n_ctx, mod_row0=batch, latent=False)

    oa = _window_attention(qa, ka.reshape(batch, seq, -1), va.reshape(batch, seq, -1),
                           kac.reshape(batch, n_ctx, -1), vac.reshape(batch, n_ctx, -1),
                           sink_a[0].astype(jnp.float32) * LOG2E, batch=batch, seq=seq)
    tabs, cls_of_block = _nb_bias_tables(rpb_b[0], seq // GRID_W)
    ob = _neighbourhood_attention(qb, kb.reshape(batch, seq, -1), vb.reshape(batch, seq, -1),
                                  kbc.reshape(batch, n_ctx, -1), vbc.reshape(batch, n_ctx, -1),
                                  tabs, cls_of_block, batch=batch, seq=seq)

    w_out0 = w_out[0]
    w_o = jnp.concatenate([w_out0[:WIDTH_A][col_order], w_out0[WIDTH_A:]], axis=0).astype(bf16)
    x1, h2, idx, wts, rank, size, base, counts = _outproj(
        oa, ob, x.reshape(n_tok, d), mod3,
        g_grp_a[0][col_order].reshape(1, -1), g_grp_b[0].reshape(1, -1),
        g_post_mix[0].reshape(1, d), g_pre_ffn[0].reshape(1, d),
        w_o, w_router[0], b_router[0].reshape(1, -1), seq=seq)

    n_tok_tiles = n_tok // DISPATCH_TILE
    size = size.reshape(n_tok_tiles, N_EXPERTS)
    counts = counts.reshape(-1)
    padded = (counts + FFN_STEP - 1) // FFN_STEP * FFN_STEP
    pad_end = jnp.cumsum(padded).astype(jnp.int32)
    pad_start = pad_end - padded
    n_steps = (n_tok * TOP_K + n_tok_tiles * N_EXPERTS * (ROW_ALIGN - 1)) // FFN_STEP + N_EXPERTS
    n_used = (pad_end[-1:] // FFN_STEP).astype(jnp.int32)
    step_row0 = jnp.arange(n_steps, dtype=jnp.int32) * FFN_STEP
    step_expert = jnp.minimum(jnp.sum(step_row0[:, None] >= pad_end[None, :], axis=1),
                              N_EXPERTS - 1).astype(jnp.int32)
    own = step_expert[:, None] == jnp.arange(N_EXPERTS, dtype=jnp.int32)
    step_rows = jnp.clip(jnp.sum(jnp.where(own, (pad_start + counts)[None, :], 0), axis=1) - step_row0,
                         0, FFN_STEP).astype(jnp.int32)
    fill_from = (pad_start + jnp.maximum((counts + FFN_TILE - 1) // FFN_TILE * FFN_TILE - FFN_TILE, 0)
                 ).astype(jnp.int32)
    local0 = jnp.cumsum(size, axis=1) - size
    global0 = pad_start[None, :] + base.reshape(n_tok_tiles, N_EXPERTS)
    tab = jnp.concatenate([local0, global0, size, jnp.sum(size, axis=1, keepdims=True),
                           jnp.zeros((n_tok_tiles, TAB_WIDTH - 3 * N_EXPERTS - 1), jnp.int32)], axis=1)
    tab = tab.reshape(-1).astype(jnp.int32)
    hot = idx[:, :, None, :] == jnp.arange(N_EXPERTS, dtype=jnp.int32)[None, None, :, None]
    lpos_t = rank + jnp.sum(jnp.where(hot, local0[:, None, :, None], 0), axis=2)
    lpos = jnp.swapaxes(lpos_t, 1, 2).reshape(n_tok, TOP_K)

    xs = _dispatch(h2, tab, lpos_t, wts, fill_from, pad_end, n_rows=n_steps * FFN_STEP)
    ys = _ffn(xs, step_expert, step_rows, n_used, w_mlp1[0], b_mlp1[0], w_mlp2[0], b_mlp2[0])
    out = _combine(ys, tab, lpos, x1, mod3, g_post_ffn[0].reshape(1, d), seq=seq)
    return out.reshape(batch, seq, d)
```

```python
import functools

import numpy as np
import jax
import jax.numpy as jnp
from jax import lax
from jax.experimental import pallas as pl
from jax.experimental.pallas import tpu as pltpu

GRID_W = 64
HEAD_DIM = 64
N_HEADS_A = 8
N_KV_A = 2
GQA_GROUP = N_HEADS_A // N_KV_A
N_HEADS_B = 8
WIDTH_A = N_HEADS_A * HEAD_DIM
WIDTH_KV_A = N_KV_A * HEAD_DIM
WIDTH_B = N_HEADS_B * HEAD_DIM
WINDOW = 128
BLOCK_A = 128
NA_ROWS = 8
NA_COLS = 16
N_EXPERTS = 32
TOP_K = 4
SWIGLU_LIMIT = 7.0
SWIGLU_ALPHA = 1.702
ROPE_BASE = 10000.0
EPS = 1e-6
NEG_INF = -1e30

LANES = 128
VMEM_LIMIT = 56 * 1024 * 1024

LOG2E = 1.4426950408889634
ATT_BLOCKS = 2
NB_Q_ROWS = 2
NB_K_ROWS = NB_Q_ROWS + NA_ROWS
PROJ_TILE = 512
OUT_TILE = 256
FFN_TILE = 256
FFN_STEP = 4 * FFN_TILE
FFN_CHUNK = 256
FFN_GROUP = 4
DISPATCH_TILE = OUT_TILE
ROW_ALIGN = 8
TAB_WIDTH = 128


def _params(*sem):
    return pltpu.CompilerParams(dimension_semantics=sem, vmem_limit_bytes=VMEM_LIMIT)


def _rms(x, g):
    return x * lax.rsqrt(jnp.mean(x * x, axis=-1, keepdims=True) + EPS) * g


def _dot(a, b):
    return jnp.dot(a, b, preferred_element_type=jnp.float32)


def _dot_nt(a, b):
    return lax.dot_general(a, b, (((1,), (1,)), ((), ())), preferred_element_type=jnp.float32)


def _ada_kernel(c_ref, w_ref, b_ref, o_ref):
    c = c_ref[...]
    s = (c * jax.nn.sigmoid(c)).astype(jnp.bfloat16)
    o_ref[...] = _dot(s, w_ref[...].astype(jnp.bfloat16)) + b_ref[...]


def _ada(cc, w_ada, b_ada):
    rows, d = cc.shape
    n_out = w_ada.shape[1]
    return pl.pallas_call(
        _ada_kernel,
        grid=(n_out // d,),
        in_specs=[pl.BlockSpec((rows, d), lambda j: (0, 0)),
                  pl.BlockSpec((d, d), lambda j: (0, j)),
                  pl.BlockSpec((1, d), lambda j: (0, j))],
        out_specs=pl.BlockSpec((rows, d), lambda j: (0, j)),
        out_shape=jax.ShapeDtypeStruct((rows, n_out), jnp.float32),
        compiler_params=_params("arbitrary"),
        name="ada",
    )(cc, w_ada, b_ada.reshape(1, n_out))


def _rope(x, cos, sin):
    w = x.shape[1]
    reps = w // LANES
    if reps > 1:
        cos = jnp.concatenate([cos] * reps, axis=1)
        sin = jnp.concatenate([sin] * reps, axis=1)
    lane = lax.broadcasted_iota(jnp.int32, x.shape, 1)
    quarter = HEAD_DIM // 4
    partner = jnp.where(lane % (2 * quarter) < quarter,
                        pltpu.roll(x, w - quarter, 1), pltpu.roll(x, quarter, 1))
    return x * cos + partner * sin


def _inproj_kernel(x_ref, sh_ref, sc_ref, g_ref, w_ref, *rest, latent):
    x = x_ref[...]
    h = _rms(x, g_ref[...]) * (1.0 + sc_ref[0]) + sh_ref[0]
    p = _dot(h.astype(jnp.bfloat16), w_ref[...])
    if latent:
        cos_ref, sin_ref, qa_ref, ka_ref, va_ref, qb_ref, kb_ref, vb_ref = rest
        cos, sin = cos_ref[...], sin_ref[...]
        scale = HEAD_DIM ** -0.5 * LOG2E
        o = 0
        qa_ref[...] = (_rope(p[:, o:o + WIDTH_A], cos, sin) * scale).astype(qa_ref.dtype)
        o += WIDTH_A
        ka_ref[...] = _rope(p[:, o:o + WIDTH_KV_A], cos, sin).astype(ka_ref.dtype)
        o += WIDTH_KV_A
        va_ref[...] = p[:, o:o + WIDTH_KV_A].astype(va_ref.dtype)
        o += WIDTH_KV_A
        qb_ref[...] = (p[:, o:o + WIDTH_B] * scale).astype(qb_ref.dtype)
        o += WIDTH_B
    else:
        ka_ref, va_ref, kb_ref, vb_ref = rest
        o = 0
        ka_ref[...] = p[:, o:o + WIDTH_KV_A].astype(ka_ref.dtype)
        o += WIDTH_KV_A
        va_ref[...] = p[:, o:o + WIDTH_KV_A].astype(va_ref.dtype)
        o += WIDTH_KV_A
    kb_ref[...] = p[:, o:o + WIDTH_B].astype(kb_ref.dtype)
    o += WIDTH_B
    vb_ref[...] = p[:, o:o + WIDTH_B].astype(vb_ref.dtype)


def _inproj(x2d, mod3, g_pre, w, rope_tabs, *, seq, mod_row0, latent):
    t, d = x2d.shape
    tile = min(PROJ_TILE, seq)
    per_seq = seq // tile
    if latent:
        mod_row = lambda i: i // per_seq
    else:
        mod_row = lambda i: mod_row0
    in_specs = [pl.BlockSpec((tile, d), lambda i: (i, 0)),
                pl.BlockSpec((1, 1, d), lambda i: (mod_row(i), 0, 0)),
                pl.BlockSpec((1, 1, d), lambda i: (mod_row(i), 0, 1)),
                pl.BlockSpec((1, d), lambda i: (0, 0)),
                pl.BlockSpec(w.shape, lambda i: (0, 0))]
    args = [x2d, mod3, mod3, g_pre, w]
    widths = [WIDTH_KV_A, WIDTH_KV_A, WIDTH_B, WIDTH_B]
    if latent:
        in_specs += [pl.BlockSpec((tile, LANES), lambda i: (i % per_seq, 0))] * 2
        args += list(rope_tabs)
        widths = [WIDTH_A, WIDTH_KV_A, WIDTH_KV_A, WIDTH_B, WIDTH_B, WIDTH_B]
    return pl.pallas_call(
        functools.partial(_inproj_kernel, latent=latent),
        grid=(t // tile,),
        in_specs=in_specs,
        out_specs=[pl.BlockSpec((tile, wd), lambda i: (i, 0)) for wd in widths],
        out_shape=[jax.ShapeDtypeStruct((t, wd), jnp.bfloat16) for wd in widths],
        compiler_params=_params("parallel"),
        name="inproj_latent" if latent else "inproj_ctx",
    )(*args)


def _ones_beside(v):
    return jnp.concatenate([v, jnp.ones_like(v)], axis=1)


def _group_attention(q_ref, o_ref, rows, groups, s_ref, p_ref, m_ref):
    tq = rows.stop - rows.start
    low = lax.broadcasted_iota(jnp.int32, (tq, LANES), 1) < HEAD_DIM
    layout = []
    base = 0
    for pairs, k_loc, k_ctx, _, _, _, _ in groups:
        heads = [(j, 0) for j in pairs] + [(j, 1) for j in pairs]
        n_loc = k_loc.shape[0]
        stacked = []
        for j, half in heads:
            q = q_ref[rows, j * LANES:(j + 1) * LANES]
            stacked.append(jnp.where(low if half == 0 else ~low, q, jnp.zeros_like(q)))
        qs = jnp.concatenate(stacked, axis=0)
        n = len(heads) * tq
        s_ref[base:base + n, :n_loc] = _dot_nt(qs, k_loc)
        s_ref[base:base + n, n_loc:] = _dot_nt(qs, k_ctx)
        layout.append((base, heads, n_loc))
        base += n
    for (base, heads, n_loc), group in zip(layout, groups):
        bias_of, sink_of = group[5], group[6]
        for g, (j, half) in enumerate(heads):
            r = slice(base + g * tq, base + (g + 1) * tq)
            s_loc = s_ref[r, :n_loc] + bias_of(j, half)
            s_ctx = s_ref[r, n_loc:]
            m = jnp.maximum(jnp.max(s_loc, axis=1, keepdims=True), jnp.max(s_ctx, axis=1, keepdims=True))
            sink = sink_of(j, half)
            if sink is not None:
                m = jnp.maximum(m, sink)
            p_ref[r, :n_loc] = jnp.exp2((s_loc - m).astype(jnp.bfloat16))
            p_ref[r, n_loc:] = jnp.exp2((s_ctx - m).astype(jnp.bfloat16))
            m_ref[r, :] = m
    for (base, heads, n_loc), group in zip(layout, groups):
        pairs, _, _, v_loc, v_ctx, _, sink_of = group
        n = len(heads) * tq
        both = (_dot(p_ref[base:base + n, :n_loc], _ones_beside(v_loc))
                + _dot(p_ref[base:base + n, n_loc:], _ones_beside(v_ctx)))
        for a, j in enumerate(pairs):
            outs = []
            for half in range(2):
                g = half * len(pairs) + a
                r = slice(g * tq, (g + 1) * tq)
                den = both[r, LANES:]
                sink = sink_of(j, half)
                if sink is not None:
                    den = den + jnp.exp2(sink - m_ref[base + g * tq:base + (g + 1) * tq, :])
                outs.append(both[r, :LANES] / den)
            o_ref[rows, j * LANES:(j + 1) * LANES] = jnp.where(low, outs[0], outs[1]).astype(o_ref.dtype)


def _attention_scratch(tq, n_heads, n_keys):
    return [pltpu.VMEM((n_heads * tq, n_keys), jnp.float32), pltpu.VMEM((n_heads * tq, n_keys), jnp.bfloat16),
            pltpu.VMEM((n_heads * tq, 1), jnp.float32)]


def _win_kernel(sink_ref, q_ref, k_ref, v_ref, kc_ref, vc_ref, o_ref, *scratch, seq):
    span = BLOCK_A + 2 * WINDOW
    pairs = list(range(GQA_GROUP))
    for sb in range(ATT_BLOCKS):
        i = pl.program_id(1) * ATT_BLOCKS + sb
        start = pl.multiple_of(jnp.clip(i * BLOCK_A - WINDOW, 0, seq - span), BLOCK_A)
        qpos = i * BLOCK_A + lax.broadcasted_iota(jnp.int32, (BLOCK_A, span), 0)
        kpos = start + lax.broadcasted_iota(jnp.int32, (BLOCK_A, span), 1)
        bias = jnp.where(jnp.abs(kpos - qpos) <= WINDOW, 0.0, NEG_INF).astype(jnp.float32)
        groups = [([j], k_ref[0, pl.ds(start, span), :], kc_ref[0], v_ref[0, pl.ds(start, span), :], vc_ref[0],
                   lambda j, half: bias, lambda j, half: sink_ref[half * GQA_GROUP + j]) for j in pairs]
        _group_attention(q_ref, o_ref, slice(sb * BLOCK_A, (sb + 1) * BLOCK_A), groups, *scratch)


def _window_attention(qa, ka, va, kac, vac, sink, *, batch, seq):
    tq = ATT_BLOCKS * BLOCK_A
    nb = seq // tq
    n_ctx = kac.shape[1]
    return pl.pallas_call(
        functools.partial(_win_kernel, seq=seq),
        grid=(batch, nb),
        in_specs=[pl.BlockSpec(memory_space=pltpu.SMEM),
                  pl.BlockSpec((tq, WIDTH_A), lambda b, i: (b * nb + i, 0)),
                  pl.BlockSpec((1, seq, WIDTH_KV_A), lambda b, i: (b, 0, 0)),
                  pl.BlockSpec((1, seq, WIDTH_KV_A), lambda b, i: (b, 0, 0)),
                  pl.BlockSpec((1, n_ctx, WIDTH_KV_A), lambda b, i: (b, 0, 0)),
                  pl.BlockSpec((1, n_ctx, WIDTH_KV_A), lambda b, i: (b, 0, 0))],
        out_specs=pl.BlockSpec((tq, WIDTH_A), lambda b, i: (b * nb + i, 0)),
        out_shape=jax.ShapeDtypeStruct((batch * seq, WIDTH_A), jnp.bfloat16),
        scratch_shapes=_attention_scratch(BLOCK_A, N_HEADS_A, BLOCK_A + 2 * WINDOW + n_ctx),
        compiler_params=_params("parallel", "arbitrary"),
        name="window_attention",
    )(sink, qa, ka, va, kac, vac)


def _nb_kernel(q_ref, k_ref, v_ref, kc_ref, vc_ref, *rest, rows_n):
    tab_refs, o_ref, scratch = rest[:ATT_BLOCKS], rest[ATT_BLOCKS], rest[ATT_BLOCKS + 1:]
    n_keys = NB_K_ROWS * GRID_W
    tq = NB_Q_ROWS * GRID_W
    for sb in range(ATT_BLOCKS):
        m = pl.program_id(1) * ATT_BLOCKS + sb
        start_row = jnp.clip(NB_Q_ROWS * m - NA_ROWS // 2, 0, rows_n - NB_K_ROWS)
        start = pl.multiple_of(start_row * GRID_W, LANES)
        tab_ref = tab_refs[sb]
        groups = []
        for j in range(q_ref.shape[1] // LANES):
            cols = slice(j * LANES, (j + 1) * LANES)
            groups.append(([j], k_ref[0, pl.ds(start, n_keys), cols], kc_ref[0, :, cols],
                           v_ref[0, pl.ds(start, n_keys), cols], vc_ref[0, :, cols],
                           lambda j, half: tab_ref[0, 2 * j + half].astype(jnp.float32), lambda j, half: None))
        _group_attention(q_ref, o_ref, slice(sb * tq, (sb + 1) * tq), groups, *scratch)


def _nb_classes(rows_n):
    n_blocks = rows_n // NB_Q_ROWS
    sig = {}
    cls_of_block = []
    reps = []
    for m in range(n_blocks):
        start_row = int(np.clip(NB_Q_ROWS * m - NA_ROWS // 2, 0, rows_n - NB_K_ROWS))
        key = tuple((start_row - r, int(np.clip(r - NA_ROWS // 2, 0, rows_n - NA_ROWS)) - r)
                    for r in range(NB_Q_ROWS * m, NB_Q_ROWS * (m + 1)))
        if key not in sig:
            sig[key] = len(reps)
            reps.append(m)
        cls_of_block.append(sig[key])
    return np.asarray(cls_of_block, np.int32), reps


def _nb_bias_tables(rpb, rows_n):
    cls_of_block, reps = _nb_classes(rows_n)
    n_heads = rpb.shape[0]
    cq = np.arange(GRID_W)[:, None]
    ck = np.arange(GRID_W)[None, :]
    cs = np.clip(cq - NA_COLS // 2, 0, GRID_W - NA_COLS)
    col_ok = (ck >= cs) & (ck < cs + NA_COLS)
    pick = ((ck - cq + NA_COLS - 1)[None] == np.arange(2 * NA_COLS - 1)[:, None, None]) & col_ok[None]
    tiles = jnp.einsum('hrd,dqk->hrqk', rpb.astype(jnp.float32), jnp.asarray(pick, jnp.float32),
                       precision=lax.Precision.HIGHEST)
    tiles = jnp.where(jnp.asarray(col_ok)[None, None], tiles * LOG2E, NEG_INF).astype(jnp.bfloat16)
    blocked = jnp.full((n_heads, GRID_W, GRID_W), NEG_INF, jnp.bfloat16)
    tabs = []
    for m in reps:
        start_row = int(np.clip(NB_Q_ROWS * m - NA_ROWS // 2, 0, rows_n - NB_K_ROWS))
        q_rows = []
        for r in range(NB_Q_ROWS * m, NB_Q_ROWS * (m + 1)):
            rs = int(np.clip(r - NA_ROWS // 2, 0, rows_n - NA_ROWS))
            q_rows.append(jnp.concatenate(
                [tiles[:, krow - r + NA_ROWS - 1] if rs <= krow < rs + NA_ROWS else blocked
                 for krow in range(start_row, start_row + NB_K_ROWS)], axis=2))
        tabs.append(jnp.concatenate(q_rows, axis=1))
    return jnp.stack(tabs), cls_of_block


def _neighbourhood_attention(qb, kb, vb, kbc, vbc, tabs, cls_of_block, *, batch, seq):
    rows_n = seq // GRID_W
    tq = ATT_BLOCKS * NB_Q_ROWS * GRID_W
    nstep = seq // tq
    n_ctx = kbc.shape[1]

    def tab_spec(sb):
        return pl.BlockSpec((1,) + tabs.shape[1:], lambda b, m, c: (c[m * ATT_BLOCKS + sb], 0, 0, 0))

    grid_spec = pltpu.PrefetchScalarGridSpec(
        num_scalar_prefetch=1,
        grid=(batch, nstep),
        in_specs=[pl.BlockSpec((tq, WIDTH_B), lambda b, m, c: (b * nstep + m, 0)),
                  pl.BlockSpec((1, seq, WIDTH_B), lambda b, m, c: (b, 0, 0)),
                  pl.BlockSpec((1, seq, WIDTH_B), lambda b, m, c: (b, 0, 0)),
                  pl.BlockSpec((1, n_ctx, WIDTH_B), lambda b, m, c: (b, 0, 0)),
                  pl.BlockSpec((1, n_ctx, WIDTH_B), lambda b, m, c: (b, 0, 0))]
                 + [tab_spec(sb) for sb in range(ATT_BLOCKS)],
        out_specs=pl.BlockSpec((tq, WIDTH_B), lambda b, m, c: (b * nstep + m, 0)),
        scratch_shapes=_attention_scratch(NB_Q_ROWS * GRID_W, N_HEADS_B, NB_K_ROWS * GRID_W + n_ctx),
    )

    def body(c_ref, *refs):
        _nb_kernel(*refs, rows_n=rows_n)

    return pl.pallas_call(
        body,
        grid_spec=grid_spec,
        out_shape=jax.ShapeDtypeStruct((batch * seq, WIDTH_B), jnp.bfloat16),
        compiler_params=_params("parallel", "arbitrary"),
        name="neighbourhood_attention",
    )(jnp.asarray(cls_of_block), qb, kb, vb, kbc, vbc, *([tabs] * ATT_BLOCKS))


def _outproj_kernel(oa_ref, ob_ref, x_ref, ga_ref, sh_ref, sc_ref, gga_ref, ggb_ref, gpost_ref, gpre_ref,
                    wo_ref, wr_hi_ref, wr_lo_ref, br_ref,
                    x1_ref, h2_ref, idx_ref, wts_ref, rank_ref, size_ref, base_ref, cnt_ref, run_ref):
    i = pl.program_id(0)

    @pl.when(i == 0)
    def _():
        run_ref[...] = jnp.zeros_like(run_ref)

    na = _rms(oa_ref[...].astype(jnp.float32), gga_ref[...]).astype(jnp.bfloat16)
    nb = _rms(ob_ref[...].astype(jnp.float32), ggb_ref[...]).astype(jnp.bfloat16)
    mix = _dot(na, wo_ref[:WIDTH_A, :]) + _dot(nb, wo_ref[WIDTH_A:, :])
    x1 = x_ref[...] + ga_ref[0] * _rms(mix, gpost_ref[...])
    x1_ref[...] = x1
    h2 = _rms(x1, gpre_ref[...]) * (1.0 + sc_ref[0]) + sh_ref[0]
    h2_ref[...] = h2.astype(h2_ref.dtype)

    h_hi = h2.astype(jnp.bfloat16)
    h_lo = (h2 - h_hi.astype(jnp.float32)).astype(jnp.bfloat16)
    logits = _dot(h_hi, wr_hi_ref[...]) + _dot(h_lo, wr_hi_ref[...]) + _dot(h_hi, wr_lo_ref[...])
    work = logits.T[:N_EXPERTS] + br_ref[...]
    e, t = work.shape
    row = lax.broadcasted_iota(jnp.int32, (e, t), 0)
    chosen = jnp.zeros((e, t), jnp.float32)
    vals, sels, hots = [], [], []
    for _k in range(TOP_K):
        mx = jnp.max(work, axis=0, keepdims=True)
        sel = jnp.min(jnp.where(work == mx, row, e), axis=0, keepdims=True)
        hot = row == sel
        vals.append(mx)
        sels.append(sel)
        hots.append(hot)
        work = jnp.where(hot, -jnp.inf, work)
        chosen = chosen + hot.astype(jnp.float32)
    ex = [jnp.exp(v - vals[0]) for v in vals]
    den = ex[0] + ex[1] + ex[2] + ex[3]

    earlier = (lax.broadcasted_iota(jnp.int32, (t, t), 0) < lax.broadcasted_iota(jnp.int32, (t, t), 1))
    prefix = _dot(chosen.astype(jnp.bfloat16), earlier.astype(jnp.bfloat16))
    ranks = [jnp.sum(jnp.where(h, prefix, 0.0), axis=0, keepdims=True) for h in hots]
    size = jnp.sum(chosen, axis=1, keepdims=True).astype(jnp.int32)
    size = (size + (ROW_ALIGN - 1)) // ROW_ALIGN * ROW_ALIGN
    size_ref[0] = size
    base_ref[0] = run_ref[...]
    run_ref[...] = run_ref[...] + size
    cnt_ref[...] = run_ref[...]

    idx_ref[0] = jnp.concatenate(sels, axis=0)
    wts_ref[0] = jnp.concatenate([v / den for v in ex], axis=0)
    rank_ref[0] = jnp.concatenate(ranks, axis=0).astype(jnp.int32)


def _outproj(oa, ob, x2d, mod3, gga, ggb, gpost, gpre, w_out, w_router, b_router, *, seq):
    t, d = x2d.shape
    tile = OUT_TILE
    per_seq = seq // tile
    row = lambda i: (i, 0)
    const = lambda i: (0, 0)
    modspec = lambda part: pl.BlockSpec((1, 1, d), lambda i: (i // per_seq, 0, part))
    per_tile = lambda shape: pl.BlockSpec((1,) + shape, lambda i: (i, 0, 0))
    n_steps = t // tile
    w_pad = jnp.zeros((d, LANES), jnp.float32).at[:, :N_EXPERTS].set(w_router)
    wr_hi = w_pad.astype(jnp.bfloat16)
    wr_lo = (w_pad - wr_hi.astype(jnp.float32)).astype(jnp.bfloat16)
    outs = pl.pallas_call(
        _outproj_kernel,
        grid=(n_steps,),
        in_specs=[pl.BlockSpec((tile, WIDTH_A), row), pl.BlockSpec((tile, WIDTH_B), row),
                  pl.BlockSpec((tile, d), row),
                  modspec(2), modspec(3), modspec(4),
                  pl.BlockSpec((1, WIDTH_A), const), pl.BlockSpec((1, WIDTH_B), const),
                  pl.BlockSpec((1, d), const), pl.BlockSpec((1, d), const),
                  pl.BlockSpec(w_out.shape, const), pl.BlockSpec((d, LANES), const), pl.BlockSpec((d, LANES), const),
                  pl.BlockSpec((N_EXPERTS, 1), const)],
        out_specs=[pl.BlockSpec((tile, d), row), pl.BlockSpec((tile, d), row),
                   per_tile((TOP_K, tile)), per_tile((TOP_K, tile)), per_tile((TOP_K, tile)),
                   per_tile((N_EXPERTS, 1)), per_tile((N_EXPERTS, 1)),
                   pl.BlockSpec((N_EXPERTS, 1), const)],
        out_shape=[jax.ShapeDtypeStruct((t, d), jnp.float32), jax.ShapeDtypeStruct((t, d), jnp.bfloat16),
                   jax.ShapeDtypeStruct((n_steps, TOP_K, tile), jnp.int32),
                   jax.ShapeDtypeStruct((n_steps, TOP_K, tile), jnp.float32),
                   jax.ShapeDtypeStruct((n_steps, TOP_K, tile), jnp.int32),
                   jax.ShapeDtypeStruct((n_steps, N_EXPERTS, 1), jnp.int32),
                   jax.ShapeDtypeStruct((n_steps, N_EXPERTS, 1), jnp.int32),
                   jax.ShapeDtypeStruct((N_EXPERTS, 1), jnp.int32)],
        scratch_shapes=[pltpu.VMEM((N_EXPERTS, 1), jnp.int32)],
        compiler_params=_params("arbitrary"),
        name="outproj_router",
    )(oa, ob, x2d, mod3, mod3, mod3, gga, ggb, gpost, gpre, w_out, wr_hi, wr_lo, b_router.reshape(N_EXPERTS, 1))
    return outs


def _piece_table(tab_ref, e):
    return (pl.multiple_of(tab_ref[e], ROW_ALIGN), pl.multiple_of(tab_ref[N_EXPERTS + e], ROW_ALIGN),
            pl.multiple_of(tab_ref[2 * N_EXPERTS + e], ROW_ALIGN))


def _dispatch_kernel(fill_from_ref, pad_end_ref, tab_ref, h_ref, lpos_ref, w_ref, xs_ref,
                     stage_ref, zero_ref, pend_ref, sems, zsem):
    i = pl.program_id(0)
    n_steps = pl.num_programs(0)
    slot = i % 2
    n_local, width = stage_ref.shape[1:]
    d = h_ref.shape[1]

    def fill(row):
        return pltpu.make_async_copy(zero_ref, xs_ref.at[pl.ds(pl.multiple_of(row, FFN_TILE), FFN_TILE)], zsem)

    def fills(e):
        return (pad_end_ref[e] - fill_from_ref[e]) // FFN_TILE

    def drain(s):
        n = pl.multiple_of(pend_ref[s], ROW_ALIGN)

        @pl.when(n > 0)
        def _():
            pltpu.make_async_copy(stage_ref.at[s, pl.ds(0, n)], xs_ref.at[pl.ds(0, n)], sems.at[s]).wait()
        pend_ref[s] = 0

    @pl.when(i == 0)
    def _():
        zero_ref[...] = jnp.zeros_like(zero_ref)
        pend_ref[0] = 0
        pend_ref[1] = 0

        def start_e(e, c):
            def start_j(j, c2):
                fill(fill_from_ref[e] + j * FFN_TILE).start()
                return c2
            lax.fori_loop(0, fills(e), start_j, 0)
            return c + fills(e)
        n_fill = lax.fori_loop(0, N_EXPERTS, start_e, 0)

        def wait_all(j, c):
            fill(0).wait()
            return c
        lax.fori_loop(0, n_fill, wait_all, 0)

    row = lax.broadcasted_iota(jnp.int32, (n_local, h_ref.shape[0]), 0)
    place = None
    row_w = None
    for k in range(TOP_K):
        hit = row == lpos_ref[0, k:k + 1, :]
        place = hit if place is None else place | hit
        wk = jnp.sum(jnp.where(hit, w_ref[0, k:k + 1, :], 0.0), axis=1, keepdims=True)
        row_w = wk if row_w is None else row_w + wk
    rows = _dot(place.astype(jnp.bfloat16), h_ref[...])

    drain(slot)
    stage_ref[slot, :, :d] = rows
    stage_ref[slot, :, d:] = jnp.broadcast_to(row_w, (n_local, width - d))

    def piece(e, total):
        loc, glob, n = _piece_table(tab_ref, e)

        @pl.when(n > 0)
        def _():
            pltpu.make_async_copy(stage_ref.at[slot, pl.ds(loc, n)], xs_ref.at[pl.ds(glob, n)], sems.at[slot]).start()
        return total + n
    pend_ref[slot] = lax.fori_loop(0, N_EXPERTS, piece, 0)

    tail0 = pad_end_ref[N_EXPERTS - 1]
    n_tail = (xs_ref.shape[0] - tail0) // FFN_TILE
    per_step = (n_tail + n_steps - 1) // n_steps

    def start_tail(j, c):
        t = i * per_step + j

        @pl.when(t < n_tail)
        def _():
            fill(tail0 + t * FFN_TILE).start()
        return c
    lax.fori_loop(0, per_step, start_tail, 0)

    @pl.when(i == n_steps - 1)
    def _():
        drain(0)
        drain(1)

        def wait_tail(j, c):
            fill(0).wait()
            return c
        lax.fori_loop(0, n_tail, wait_tail, 0)


def _dispatch(h2, tab_flat, lpos_t, w_t, fill_from, pad_end, *, n_rows):
    t, d = h2.shape
    tile = DISPATCH_TILE
    n_local = tile * TOP_K + N_EXPERTS * ROW_ALIGN
    width = d + LANES
    grid_spec = pltpu.PrefetchScalarGridSpec(
        num_scalar_prefetch=2,
        grid=(t // tile,),
        in_specs=[pl.BlockSpec((TAB_WIDTH,), lambda i, *_: (i,), memory_space=pltpu.SMEM),
                  pl.BlockSpec((tile, d), lambda i, *_: (i, 0)),
                  pl.BlockSpec((1, TOP_K, tile), lambda i, *_: (i, 0, 0)),
                  pl.BlockSpec((1, TOP_K, tile), lambda i, *_: (i, 0, 0))],
        out_specs=pl.BlockSpec(memory_space=pl.ANY),
        scratch_shapes=[pltpu.VMEM((2, n_local, width), jnp.float32),
                        pltpu.VMEM((FFN_TILE, width), jnp.float32),
                        pltpu.SMEM((2,), jnp.int32),
                        pltpu.SemaphoreType.DMA((2,)), pltpu.SemaphoreType.DMA(())],
    )
    return pl.pallas_call(
        _dispatch_kernel,
        grid_spec=grid_spec,
        out_shape=jax.ShapeDtypeStruct((n_rows, width), jnp.float32),
        compiler_params=_params("arbitrary"),
        name="dispatch",
    )(fill_from, pad_end, tab_flat, h2, lpos_t, w_t)


def _ffn_kernel(te_ref, rows_ref, n_used_ref, x_ref, w1_ref, b1_ref, w2_ref, b2_ref, y_ref,
                w1p_ref, w2b_ref, act_ref):
    i = pl.program_id(0)
    live = i < n_used_ref[0]
    n_pair = w1_ref.shape[2] // FFN_CHUNK
    half = FFN_CHUNK // 2
    d = w2_ref.shape[2]
    n_sub = jnp.where(live, (rows_ref[i] + FFN_TILE - 1) // FFN_TILE, 0)

    @pl.when(live & ((i == 0) | (te_ref[i] != te_ref[jnp.maximum(i - 1, 0)])))
    def _():
        src = lax.broadcasted_iota(jnp.int32, (FFN_CHUNK, FFN_CHUNK), 0)
        dst = lax.broadcasted_iota(jnp.int32, (FFN_CHUNK, FFN_CHUNK), 1)
        unzip = (src == jnp.where(dst < half, 2 * dst, 2 * (dst - half) + 1)).astype(jnp.bfloat16)
        for c in range(n_pair):
            cs = slice(c * FFN_CHUNK, (c + 1) * FFN_CHUNK)
            w1p_ref[:, cs] = _dot(w1_ref[0, :, cs].astype(jnp.bfloat16), unzip).astype(jnp.bfloat16)
        w2b_ref[...] = w2_ref[0].astype(jnp.bfloat16)

    def compute(n_rows):
        x = x_ref[:n_rows, :d].astype(jnp.bfloat16)
        for c in range(0, n_pair, FFN_GROUP):
            cs = slice(c * FFN_CHUNK, (c + FFN_GROUP) * FFN_CHUNK)
            h = _dot(x, w1p_ref[:, cs]) + b1_ref[0, :, cs]
            g = jnp.concatenate([h[:, k * FFN_CHUNK:k * FFN_CHUNK + half] for k in range(FFN_GROUP)], axis=1)
            u = jnp.concatenate([h[:, k * FFN_CHUNK + half:(k + 1) * FFN_CHUNK] for k in range(FFN_GROUP)], axis=1)
            g = jnp.minimum(g, SWIGLU_LIMIT)
            u = jnp.clip(u, -SWIGLU_LIMIT, SWIGLU_LIMIT)
            act = g * jax.nn.sigmoid(SWIGLU_ALPHA * g) * (u + 1.0)
            act_ref[:n_rows, c * half:(c + FFN_GROUP) * half] = act.astype(act_ref.dtype)
        y_ref[:n_rows, :] = (_dot(act_ref[:n_rows, :], w2b_ref[...]) + b2_ref[0]) * x_ref[:n_rows, d:d + 1]
        if n_rows < y_ref.shape[0]:
            y_ref[n_rows:, :] = jnp.zeros((y_ref.shape[0] - n_rows, d), y_ref.dtype)

    for tiles in range(1, y_ref.shape[0] // FFN_TILE + 1):
        pl.when(n_sub == tiles)(functools.partial(compute, tiles * FFN_TILE))

    @pl.when(n_sub == 0)
    def _():
        y_ref[...] = jnp.zeros_like(y_ref)


def _unzip_bias(b1):
    e, f2 = b1.shape
    half = FFN_CHUNK // 2
    return b1.reshape(e, f2 // FFN_CHUNK, half, 2).transpose(0, 1, 3, 2).reshape(e, 1, f2)


def _ffn(xs, step_expert, step_rows, n_used, w1, b1, w2, b2):
    rows, width = xs.shape
    d, f2 = w1.shape[1:]
    n_steps = rows // FFN_STEP
    live = lambda i, nu: jnp.minimum(i, nu[0] - 1)
    wspec = lambda shape: pl.BlockSpec((1,) + shape, lambda i, te, nr, nu: (te[live(i, nu)], 0, 0))
    grid_spec = pltpu.PrefetchScalarGridSpec(
        num_scalar_prefetch=3,
        grid=(n_steps,),
        in_specs=[pl.BlockSpec((FFN_STEP, width), lambda i, te, nr, nu: (live(i, nu), 0)),
                  wspec((d, f2)), wspec((1, f2)), wspec((f2 // 2, d)), wspec((1, d))],
        out_specs=pl.BlockSpec((FFN_STEP, d), lambda i, te, nr, nu: (i, 0)),
        scratch_shapes=[pltpu.VMEM((d, f2), jnp.bfloat16), pltpu.VMEM((f2 // 2, d), jnp.bfloat16),
                        pltpu.VMEM((FFN_STEP, f2 // 2), jnp.bfloat16)],
    )
    return pl.pallas_call(
        _ffn_kernel,
        grid_spec=grid_spec,
        out_shape=jax.ShapeDtypeStruct((rows, d), jnp.float32),
        compiler_params=_params("arbitrary"),
        name="expert_ffn",
    )(step_expert, step_rows, n_used, xs, w1, _unzip_bias(b1), w2, b2[:, None, :])


def _combine_kernel(tab_ref, next_tab_ref, lpos_ref, x1_ref, ga_ref, g_ref, y_ref, o_ref, buf_ref, sems):
    i = pl.program_id(0)
    n_steps = pl.num_programs(0)
    slot = i % 2
    tile = x1_ref.shape[0]
    n_local = buf_ref.shape[1]

    def fetch(tab, s):
        def piece(e, c):
            loc, glob, n = _piece_table(tab, e)

            @pl.when(n > 0)
            def _():
                pltpu.make_async_copy(y_ref.at[pl.ds(glob, n)], buf_ref.at[s, pl.ds(loc, n)], sems.at[s]).start()
            return c
        lax.fori_loop(0, N_EXPERTS, piece, 0)

    @pl.when(i == 0)
    def _():
        buf_ref[...] = jnp.zeros_like(buf_ref)
        fetch(tab_ref, 0)

    @pl.when(i + 1 < n_steps)
    def _():
        fetch(next_tab_ref, 1 - slot)

    n = pl.multiple_of(tab_ref[3 * N_EXPERTS], ROW_ALIGN)

    @pl.when(n > 0)
    def _():
        pltpu.make_async_copy(y_ref.at[pl.ds(0, n)], buf_ref.at[slot, pl.ds(0, n)], sems.at[slot]).wait()

    col = lax.broadcasted_iota(jnp.int32, (tile, n_local), 1)
    pos = lpos_ref[...]
    pick = col == pos[:, 0:1]
    for k in range(1, TOP_K):
        pick = pick | (col == pos[:, k:k + 1])
    pick = pick.astype(jnp.bfloat16)
    y = _dot(pick, buf_ref[slot].astype(jnp.bfloat16))
    o_ref[...] = x1_ref[...] + ga_ref[0] * _rms(y, g_ref[...])


def _combine(ys, tab_flat, lpos, x1, mod3, g_post, *, seq):
    t, d = x1.shape
    tile = DISPATCH_TILE
    n_steps = t // tile
    per_seq = seq // tile
    n_local = tile * TOP_K + N_EXPERTS * ROW_ALIGN
    return pl.pallas_call(
        _combine_kernel,
        grid=(n_steps,),
        in_specs=[pl.BlockSpec((TAB_WIDTH,), lambda i: (i,), memory_space=pltpu.SMEM),
                  pl.BlockSpec((TAB_WIDTH,), lambda i: (jnp.minimum(i + 1, n_steps - 1),), memory_space=pltpu.SMEM),
                  pl.BlockSpec((tile, TOP_K), lambda i: (i, 0)),
                  pl.BlockSpec((tile, d), lambda i: (i, 0)),
                  pl.BlockSpec((1, 1, d), lambda i: (i // per_seq, 0, 5)),
                  pl.BlockSpec((1, d), lambda i: (0, 0)),
                  pl.BlockSpec(memory_space=pl.ANY)],
        out_specs=pl.BlockSpec((tile, d), lambda i: (i, 0)),
        out_shape=jax.ShapeDtypeStruct((t, d), jnp.float32),
        scratch_shapes=[pltpu.VMEM((2, n_local, ys.shape[1]), ys.dtype), pltpu.SemaphoreType.DMA((2,))],
        compiler_params=_params("arbitrary"),
        name="combine",
    )(tab_flat, tab_flat, lpos, x1, mod3, g_post, ys)


def _rope_tables(seq):
    pos = np.arange(seq)
    n_freq = HEAD_DIM // 4
    freqs = ROPE_BASE ** (-jnp.arange(n_freq, dtype=jnp.float32) / n_freq)
    rows = jnp.asarray(pos // GRID_W, jnp.float32)[:, None] * freqs[None, :]
    cols = jnp.asarray(pos % GRID_W, jnp.float32)[:, None] * freqs[None, :]
    ang = jnp.concatenate([rows, rows, cols, cols], axis=1)
    sign = np.tile(np.repeat([-1.0, 1.0], n_freq), 2).astype(np.float32)
    cos = jnp.cos(ang)
    sin = jnp.sin(ang) * sign[None, :]
    reps = LANES // HEAD_DIM
    return jnp.tile(cos, (1, reps)), jnp.tile(sin, (1, reps))


def kernel(x, c, ctx, c_ctx, w_ada, b_ada, g_pre_mix, g_post_mix, g_pre_ffn, g_post_ffn, w_in, g_grp_a, g_grp_b,
           sink_a, rpb_b, w_out, w_router, b_router, w_mlp1, b_mlp1, w_mlp2, b_mlp2):
    batch, seq, d = x.shape
    n_ctx = ctx.shape[1]
    assert w_ada.shape[0] == 1, "single layer"
    assert seq % (ATT_BLOCKS * GRID_W * NB_Q_ROWS) == 0 and seq // GRID_W >= NB_K_ROWS
    assert seq % (ATT_BLOCKS * BLOCK_A) == 0 and seq >= BLOCK_A + 2 * WINDOW
    n_tok = batch * seq
    bf16 = jnp.bfloat16

    head_order = np.asarray([k * GQA_GROUP + j for j in range(GQA_GROUP) for k in range(N_KV_A)])
    col_order = (head_order[:, None] * HEAD_DIM + np.arange(HEAD_DIM)[None, :]).reshape(-1)

    mod_rows = -(-(batch + 1) // 8) * 8
    cc = jnp.zeros((mod_rows, d), jnp.float32).at[:batch].set(c).at[batch].set(c_ctx)
    mod3 = _ada(cc, w_ada[0], b_ada[0]).reshape(mod_rows, 1, 6 * d)

    w_in0 = w_in[0]
    w_lat = jnp.concatenate([w_in0[:, :WIDTH_A][:, col_order], w_in0[:, WIDTH_A:]], axis=1).astype(bf16)
    kv_cols = np.concatenate([np.arange(WIDTH_A, WIDTH_A + 2 * WIDTH_KV_A),
                              np.arange(WIDTH_A + 2 * WIDTH_KV_A + WIDTH_B, w_in0.shape[1])])
    w_ctx = w_in0[:, kv_cols].astype(bf16)
    g_pre = g_pre_mix[0].reshape(1, d)
    qa, ka, va, qb, kb, vb = _inproj(x.reshape(n_tok, d), mod3, g_pre, w_lat, _rope_tables(seq),
                                     seq=seq, mod_row0=0, latent=True)
    kac, vac, kbc, vbc = _inproj(ctx.reshape(batch * n_ctx, d), mod3, g_pre, w_ctx, None,
                                 seq=n_ctx, mod_row0=batch, latent=False)

    oa = _window_attention(qa, ka.reshape(batch, seq, -1), va.reshape(batch, seq, -1),
                           kac.reshape(batch, n_ctx, -1), vac.reshape(batch, n_ctx, -1),
                           sink_a[0].astype(jnp.float32) * LOG2E, batch=batch, seq=seq)
    tabs, cls_of_block = _nb_bias_tables(rpb_b[0], seq // GRID_W)
    ob = _neighbourhood_attention(qb, kb.reshape(batch, seq, -1), vb.reshape(batch, seq, -1),
                                  kbc.reshape(batch, n_ctx, -1), vbc.reshape(batch, n_ctx, -1),
                                  tabs, cls_of_block, batch=batch, seq=seq)

    w_out0 = w_out[0]
    w_o = jnp.concatenate([w_out0[:WIDTH_A][col_order], w_out0[WIDTH_A:]], axis=0).astype(bf16)
    x1, h2, idx, wts, rank, size, base, counts = _outproj(
        oa, ob, x.reshape(n_tok, d), mod3,
        g_grp_a[0][col_order].reshape(1, -1), g_grp_b[0].reshape(1, -1),
        g_post_mix[0].reshape(1, d), g_pre_ffn[0].reshape(1, d),
        w_o, w_router[0], b_router[0].reshape(1, -1), seq=seq)

    n_tok_tiles = n_tok // DISPATCH_TILE
    size = size.reshape(n_tok_tiles, N_EXPERTS)
    counts = counts.reshape(-1)
    padded = (counts + FFN_STEP - 1) // FFN_STEP * FFN_STEP
    pad_end = jnp.cumsum(padded).astype(jnp.int32)
    pad_start = pad_end - padded
    n_steps = (n_tok * TOP_K + n_tok_tiles * N_EXPERTS * (ROW_ALIGN - 1)) // FFN_STEP + N_EXPERTS
    n_used = (pad_end[-1:] // FFN_STEP).astype(jnp.int32)
    step_row0 = jnp.arange(n_steps, dtype=jnp.int32) * FFN_STEP
    step_expert = jnp.minimum(jnp.sum(step_row0[:, None] >= pad_end[None, :], axis=1),
                              N_EXPERTS - 1).astype(jnp.int32)
    own = step_expert[:, None] == jnp.arange(N_EXPERTS, dtype=jnp.int32)
    step_rows = jnp.clip(jnp.sum(jnp.where(own, (pad_start + counts)[None, :], 0), axis=1) - step_row0,
                         0, FFN_STEP).astype(jnp.int32)
    fill_from = (pad_start + jnp.maximum((counts + FFN_TILE - 1) // FFN_TILE * FFN_TILE - FFN_TILE, 0)
                 ).astype(jnp.int32)
    local0 = jnp.cumsum(size, axis=1) - size
    global0 = pad_start[None, :] + base.reshape(n_tok_tiles, N_EXPERTS)
    tab = jnp.concatenate([local0, global0, size, jnp.sum(size, axis=1, keepdims=True),
                           jnp.zeros((n_tok_tiles, TAB_WIDTH - 3 * N_EXPERTS - 1), jnp.int32)], axis=1)
    tab = tab.reshape(-1).astype(jnp.int32)
    hot = idx[:, :, None, :] == jnp.arange(N_EXPERTS, dtype=jnp.int32)[None, None, :, None]
    lpos_t = rank + jnp.sum(jnp.where(hot, local0[:, None, :, None], 0), axis=2)
    lpos = jnp.swapaxes(lpos_t, 1, 2).reshape(n_tok, TOP_K)

    xs = _dispatch(h2, tab, lpos_t, wts, fill_from, pad_end, n_rows=n_steps * FFN_STEP)
    ys = _ffn(xs, step_expert, step_rows, n_used, w_mlp1[0], b_mlp1[0], w_mlp2[0], b_mlp2[0])
    out = _combine(ys, tab, lpos, x1, mod3, g_post_ffn[0].reshape(1, d), seq=seq)
    return out.reshape(batch, seq, d)
```

```python
import functools

import numpy as np
import jax
import jax.numpy as jnp
from jax import lax
from jax.experimental import pallas as pl
from jax.experimental.pallas import tpu as pltpu

GRID_W = 64
HEAD_DIM = 64
N_HEADS_A = 8
N_KV_A = 2
GQA_GROUP = N_HEADS_A // N_KV_A
N_HEADS_B = 8
WIDTH_A = N_HEADS_A * HEAD_DIM
WIDTH_KV_A = N_KV_A * HEAD_DIM
WIDTH_B = N_HEADS_B * HEAD_DIM
WINDOW = 128
BLOCK_A = 128
NA_ROWS = 8
NA_COLS = 16
N_EXPERTS = 32
TOP_K = 4
SWIGLU_LIMIT = 7.0
SWIGLU_ALPHA = 1.702
ROPE_BASE = 10000.0
EPS = 1e-6
NEG_INF = -1e30

LANES = 128
VMEM_LIMIT = 56 * 1024 * 1024

LOG2E = 1.4426950408889634
ATT_BLOCKS = 2
NB_Q_ROWS = 2
NB_K_ROWS = NB_Q_ROWS + NA_ROWS
PROJ_TILE = 512
OUT_TILE = 256
FFN_TILE = 256
FFN_STEP = 4 * FFN_TILE
FFN_CHUNK = 256
FFN_GROUP = 4
DISPATCH_TILE = OUT_TILE
ROW_TILE = 8
TAB_WIDTH = 128


def _params(*sem):
    return pltpu.CompilerParams(dimension_semantics=sem, vmem_limit_bytes=VMEM_LIMIT)


def _rms(x, g):
    return x * lax.rsqrt(jnp.mean(x * x, axis=-1, keepdims=True) + EPS) * g


def _dot(a, b):
    return jnp.dot(a, b, preferred_element_type=jnp.float32)


def _dot_nt(a, b):
    return lax.dot_general(a, b, (((1,), (1,)), ((), ())), preferred_element_type=jnp.float32)


def _ada_kernel(c_ref, w_ref, b_ref, o_ref):
    c = c_ref[...]
    s = (c * jax.nn.sigmoid(c)).astype(jnp.bfloat16)
    o_ref[...] = _dot(s, w_ref[...].astype(jnp.bfloat16)) + b_ref[...]


def _ada(cc, w_ada, b_ada):
    rows, d = cc.shape
    n_out = w_ada.shape[1]
    return pl.pallas_call(
        _ada_kernel,
        grid=(n_out // d,),
        in_specs=[pl.BlockSpec((rows, d), lambda j: (0, 0)),
                  pl.BlockSpec((d, d), lambda j: (0, j)),
                  pl.BlockSpec((1, d), lambda j: (0, j))],
        out_specs=pl.BlockSpec((rows, d), lambda j: (0, j)),
        out_shape=jax.ShapeDtypeStruct((rows, n_out), jnp.float32),
        compiler_params=_params("arbitrary"),
        name="ada",
    )(cc, w_ada, b_ada.reshape(1, n_out))


def _rope(x, cos, sin):
    w = x.shape[1]
    reps = w // LANES
    if reps > 1:
        cos = jnp.concatenate([cos] * reps, axis=1)
        sin = jnp.concatenate([sin] * reps, axis=1)
    lane = lax.broadcasted_iota(jnp.int32, x.shape, 1)
    quarter = HEAD_DIM // 4
    partner = jnp.where(lane % (2 * quarter) < quarter,
                        pltpu.roll(x, w - quarter, 1), pltpu.roll(x, quarter, 1))
    return x * cos + partner * sin


def _inproj_kernel(x_ref, sh_ref, sc_ref, g_ref, w_ref, *rest, latent):
    x = x_ref[...]
    h = _rms(x, g_ref[...]) * (1.0 + sc_ref[0]) + sh_ref[0]
    p = _dot(h.astype(jnp.bfloat16), w_ref[...])
    if latent:
        cos_ref, sin_ref, qa_ref, ka_ref, va_ref, qb_ref, kb_ref, vb_ref = rest
        cos, sin = cos_ref[...], sin_ref[...]
        scale = HEAD_DIM ** -0.5 * LOG2E
        o = 0
        qa_ref[...] = (_rope(p[:, o:o + WIDTH_A], cos, sin) * scale).astype(qa_ref.dtype)
        o += WIDTH_A
        ka_ref[...] = _rope(p[:, o:o + WIDTH_KV_A], cos, sin).astype(ka_ref.dtype)
        o += WIDTH_KV_A
        va_ref[...] = p[:, o:o + WIDTH_KV_A].astype(va_ref.dtype)
        o += WIDTH_KV_A
        qb_ref[...] = (p[:, o:o + WIDTH_B] * scale).astype(qb_ref.dtype)
        o += WIDTH_B
    else:
        ka_ref, va_ref, kb_ref, vb_ref = rest
        o = 0
        ka_ref[...] = p[:, o:o + WIDTH_KV_A].astype(ka_ref.dtype)
        o += WIDTH_KV_A
        va_ref[...] = p[:, o:o + WIDTH_KV_A].astype(va_ref.dtype)
        o += WIDTH_KV_A
    kb_ref[...] = p[:, o:o + WIDTH_B].astype(kb_ref.dtype)
    o += WIDTH_B
    vb_ref[...] = p[:, o:o + WIDTH_B].astype(vb_ref.dtype)


def _inproj(x2d, mod3, g_pre, w, rope_tabs, *, seq, mod_row0, latent):
    t, d = x2d.shape
    tile = min(PROJ_TILE, seq)
    per_seq = seq // tile
    if latent:
        mod_row = lambda i: i // per_seq
    else:
        mod_row = lambda i: mod_row0
    in_specs = [pl.BlockSpec((tile, d), lambda i: (i, 0)),
                pl.BlockSpec((1, 1, d), lambda i: (mod_row(i), 0, 0)),
                pl.BlockSpec((1, 1, d), lambda i: (mod_row(i), 0, 1)),
                pl.BlockSpec((1, d), lambda i: (0, 0)),
                pl.BlockSpec(w.shape, lambda i: (0, 0))]
    args = [x2d, mod3, mod3, g_pre, w]
    widths = [WIDTH_KV_A, WIDTH_KV_A, WIDTH_B, WIDTH_B]
    if latent:
        in_specs += [pl.BlockSpec((tile, LANES), lambda i: (i % per_seq, 0))] * 2
        args += list(rope_tabs)
        widths = [WIDTH_A, WIDTH_KV_A, WIDTH_KV_A, WIDTH_B, WIDTH_B, WIDTH_B]
    return pl.pallas_call(
        functools.partial(_inproj_kernel, latent=latent),
        grid=(t // tile,),
        in_specs=in_specs,
        out_specs=[pl.BlockSpec((tile, wd), lambda i: (i, 0)) for wd in widths],
        out_shape=[jax.ShapeDtypeStruct((t, wd), jnp.bfloat16) for wd in widths],
        compiler_params=_params("parallel"),
        name="inproj_latent" if latent else "inproj_ctx",
    )(*args)


def _ones_beside(v):
    return jnp.concatenate([v, jnp.ones_like(v)], axis=1)


def _group_attention(q_ref, o_ref, rows, groups, s_ref, p_ref, m_ref):
    tq = rows.stop - rows.start
    low = lax.broadcasted_iota(jnp.int32, (tq, LANES), 1) < HEAD_DIM
    layout = []
    base = 0
    for pairs, k_loc, k_ctx, _, _, _, _ in groups:
        heads = [(j, 0) for j in pairs] + [(j, 1) for j in pairs]
        n_loc = k_loc.shape[0]
        stacked = []
        for j, half in heads:
            q = q_ref[rows, j * LANES:(j + 1) * LANES]
            stacked.append(jnp.where(low if half == 0 else ~low, q, jnp.zeros_like(q)))
        qs = jnp.concatenate(stacked, axis=0)
        n = len(heads) * tq
        s_ref[base:base + n, :n_loc] = _dot_nt(qs, k_loc)
        s_ref[base:base + n, n_loc:] = _dot_nt(qs, k_ctx)
        layout.append((base, heads, n_loc))
        base += n
    for (base, heads, n_loc), group in zip(layout, groups):
        bias_of, sink_of = group[5], group[6]
        for g, (j, half) in enumerate(heads):
            r = slice(base + g * tq, base + (g + 1) * tq)
            s_loc = s_ref[r, :n_loc] + bias_of(j, half)
            s_ctx = s_ref[r, n_loc:]
            m = jnp.maximum(jnp.max(s_loc, axis=1, keepdims=True), jnp.max(s_ctx, axis=1, keepdims=True))
            sink = sink_of(j, half)
            if sink is not None:
                m = jnp.maximum(m, sink)
            p_ref[r, :n_loc] = jnp.exp2((s_loc - m).astype(jnp.bfloat16))
            p_ref[r, n_loc:] = jnp.exp2((s_ctx - m).astype(jnp.bfloat16))
            m_ref[r, :] = m
    for (base, heads, n_loc), group in zip(layout, groups):
        pairs, _, _, v_loc, v_ctx, _, sink_of = group
        n = len(heads) * tq
        both = (_dot(p_ref[base:base + n, :n_loc], _ones_beside(v_loc))
                + _dot(p_ref[base:base + n, n_loc:], _ones_beside(v_ctx)))
        for a, j in enumerate(pairs):
            outs = []
            for half in range(2):
                g = half * len(pairs) + a
                r = slice(g * tq, (g + 1) * tq)
                den = both[r, LANES:]
                sink = sink_of(j, half)
                if sink is not None:
                    den = den + jnp.exp2(sink - m_ref[base + g * tq:base + (g + 1) * tq, :])
                outs.append(both[r, :LANES] / den)
            o_ref[rows, j * LANES:(j + 1) * LANES] = jnp.where(low, outs[0], outs[1]).astype(o_ref.dtype)


def _attention_scratch(tq, n_heads, n_keys):
    return [pltpu.VMEM((n_heads * tq, n_keys), jnp.float32), pltpu.VMEM((n_heads * tq, n_keys), jnp.bfloat16),
            pltpu.VMEM((n_heads * tq, 1), jnp.float32)]


def _win_kernel(sink_ref, q_ref, k_ref, v_ref, kc_ref, vc_ref, o_ref, *scratch, seq):
    span = BLOCK_A + 2 * WINDOW
    pairs = list(range(GQA_GROUP))
    for sb in range(ATT_BLOCKS):
        i = pl.program_id(1) * ATT_BLOCKS + sb
        start = pl.multiple_of(jnp.clip(i * BLOCK_A - WINDOW, 0, seq - span), BLOCK_A)
        qpos = i * BLOCK_A + lax.broadcasted_iota(jnp.int32, (BLOCK_A, span), 0)
        kpos = start + lax.broadcasted_iota(jnp.int32, (BLOCK_A, span), 1)
        bias = jnp.where(jnp.abs(kpos - qpos) <= WINDOW, 0.0, NEG_INF).astype(jnp.float32)
        groups = [([j], k_ref[0, pl.ds(start, span), :], kc_ref[0], v_ref[0, pl.ds(start, span), :], vc_ref[0],
                   lambda j, half: bias, lambda j, half: sink_ref[half * GQA_GROUP + j]) for j in pairs]
        _group_attention(q_ref, o_ref, slice(sb * BLOCK_A, (sb + 1) * BLOCK_A), groups, *scratch)


def _window_attention(qa, ka, va, kac, vac, sink, *, batch, seq):
    tq = ATT_BLOCKS * BLOCK_A
    nb = seq // tq
    n_ctx = kac.shape[1]
    return pl.pallas_call(
        functools.partial(_win_kernel, seq=seq),
        grid=(batch, nb),
        in_specs=[pl.BlockSpec(memory_space=pltpu.SMEM),
                  pl.BlockSpec((tq, WIDTH_A), lambda b, i: (b * nb + i, 0)),
                  pl.BlockSpec((1, seq, WIDTH_KV_A), lambda b, i: (b, 0, 0)),
                  pl.BlockSpec((1, seq, WIDTH_KV_A), lambda b, i: (b, 0, 0)),
                  pl.BlockSpec((1, n_ctx, WIDTH_KV_A), lambda b, i: (b, 0, 0)),
                  pl.BlockSpec((1, n_ctx, WIDTH_KV_A), lambda b, i: (b, 0, 0))],
        out_specs=pl.BlockSpec((tq, WIDTH_A), lambda b, i: (b * nb + i, 0)),
        out_shape=jax.ShapeDtypeStruct((batch * seq, WIDTH_A), jnp.bfloat16),
        scratch_shapes=_attention_scratch(BLOCK_A, N_HEADS_A, BLOCK_A + 2 * WINDOW + n_ctx),
        compiler_params=_params("parallel", "arbitrary"),
        name="window_attention",
    )(sink, qa, ka, va, kac, vac)


def _nb_kernel(q_ref, k_ref, v_ref, kc_ref, vc_ref, *rest, rows_n):
    tab_refs, o_ref, scratch = rest[:ATT_BLOCKS], rest[ATT_BLOCKS], rest[ATT_BLOCKS + 1:]
    n_keys = NB_K_ROWS * GRID_W
    tq = NB_Q_ROWS * GRID_W
    for sb in range(ATT_BLOCKS):
        m = pl.program_id(1) * ATT_BLOCKS + sb
        start_row = jnp.clip(NB_Q_ROWS * m - NA_ROWS // 2, 0, rows_n - NB_K_ROWS)
        start = pl.multiple_of(start_row * GRID_W, LANES)
        tab_ref = tab_refs[sb]
        groups = []
        for j in range(q_ref.shape[1] // LANES):
            cols = slice(j * LANES, (j + 1) * LANES)
            groups.append(([j], k_ref[0, pl.ds(start, n_keys), cols], kc_ref[0, :, cols],
                           v_ref[0, pl.ds(start, n_keys), cols], vc_ref[0, :, cols],
                           lambda j, half: tab_ref[0, 2 * j + half].astype(jnp.float32), lambda j, half: None))
        _group_attention(q_ref, o_ref, slice(sb * tq, (sb + 1) * tq), groups, *scratch)


def _nb_classes(rows_n):
    n_blocks = rows_n // NB_Q_ROWS
    sig = {}
    cls_of_block = []
    reps = []
    for m in range(n_blocks):
        start_row = int(np.clip(NB_Q_ROWS * m - NA_ROWS // 2, 0, rows_n - NB_K_ROWS))
        key = tuple((start_row - r, int(np.clip(r - NA_ROWS // 2, 0, rows_n - NA_ROWS)) - r)
                    for r in range(NB_Q_ROWS * m, NB_Q_ROWS * (m + 1)))
        if key not in sig:
            sig[key] = len(reps)
            reps.append(m)
        cls_of_block.append(sig[key])
    return np.asarray(cls_of_block, np.int32), reps


def _nb_bias_tables(rpb, rows_n):
    cls_of_block, reps = _nb_classes(rows_n)
    n_heads = rpb.shape[0]
    cq = np.arange(GRID_W)[:, None]
    ck = np.arange(GRID_W)[None, :]
    cs = np.clip(cq - NA_COLS // 2, 0, GRID_W - NA_COLS)
    col_ok = (ck >= cs) & (ck < cs + NA_COLS)
    pick = ((ck - cq + NA_COLS - 1)[None] == np.arange(2 * NA_COLS - 1)[:, None, None]) & col_ok[None]
    tiles = jnp.einsum('hrd,dqk->hrqk', rpb.astype(jnp.float32), jnp.asarray(pick, jnp.float32),
                       precision=lax.Precision.HIGHEST)
    tiles = jnp.where(jnp.asarray(col_ok)[None, None], tiles * LOG2E, NEG_INF).astype(jnp.bfloat16)
    blocked = jnp.full((n_heads, GRID_W, GRID_W), NEG_INF, jnp.bfloat16)
    tabs = []
    for m in reps:
        start_row = int(np.clip(NB_Q_ROWS * m - NA_ROWS // 2, 0, rows_n - NB_K_ROWS))
        q_rows = []
        for r in range(NB_Q_ROWS * m, NB_Q_ROWS * (m + 1)):
            rs = int(np.clip(r - NA_ROWS // 2, 0, rows_n - NA_ROWS))
            q_rows.append(jnp.concatenate(
                [tiles[:, krow - r + NA_ROWS - 1] if rs <= krow < rs + NA_ROWS else blocked
                 for krow in range(start_row, start_row + NB_K_ROWS)], axis=2))
        tabs.append(jnp.concatenate(q_rows, axis=1))
    return jnp.stack(tabs), cls_of_block


def _neighbourhood_attention(qb, kb, vb, kbc, vbc, tabs, cls_of_block, *, batch, seq):
    rows_n = seq // GRID_W
    tq = ATT_BLOCKS * NB_Q_ROWS * GRID_W
    nstep = seq // tq
    n_ctx = kbc.shape[1]

    def tab_spec(sb):
        return pl.BlockSpec((1,) + tabs.shape[1:], lambda b, m, c: (c[m * ATT_BLOCKS + sb], 0, 0, 0))

    grid_spec = pltpu.PrefetchScalarGridSpec(
        num_scalar_prefetch=1,
        grid=(batch, nstep),
        in_specs=[pl.BlockSpec((tq, WIDTH_B), lambda b, m, c: (b * nstep + m, 0)),
                  pl.BlockSpec((1, seq, WIDTH_B), lambda b, m, c: (b, 0, 0)),
                  pl.BlockSpec((1, seq, WIDTH_B), lambda b, m, c: (b, 0, 0)),
                  pl.BlockSpec((1, n_ctx, WIDTH_B), lambda b, m, c: (b, 0, 0)),
                  pl.BlockSpec((1, n_ctx, WIDTH_B), lambda b, m, c: (b, 0, 0))]
                 + [tab_spec(sb) for sb in range(ATT_BLOCKS)],
        out_specs=pl.BlockSpec((tq, WIDTH_B), lambda b, m, c: (b * nstep + m, 0)),
        scratch_shapes=_attention_scratch(NB_Q_ROWS * GRID_W, N_HEADS_B, NB_K_ROWS * GRID_W + n_ctx),
    )

    def body(c_ref, *refs):
        _nb_kernel(*refs, rows_n=rows_n)

    return pl.pallas_call(
        body,
        grid_spec=grid_spec,
        out_shape=jax.ShapeDtypeStruct((batch * seq, WIDTH_B), jnp.bfloat16),
        compiler_params=_params("parallel", "arbitrary"),
        name="neighbourhood_attention",
    )(jnp.asarray(cls_of_block), qb, kb, vb, kbc, vbc, *([tabs] * ATT_BLOCKS))


def _outproj_kernel(oa_ref, ob_ref, x_ref, ga_ref, sh_ref, sc_ref, gga_ref, ggb_ref, gpost_ref, gpre_ref,
                    wo_ref, wr_hi_ref, wr_lo_ref, br_ref,
                    x1_ref, h2_ref, idx_ref, wts_ref, rank_ref, size_ref, base_ref, cnt_ref, run_ref):
    i = pl.program_id(0)

    @pl.when(i == 0)
    def _():
        run_ref[...] = jnp.zeros_like(run_ref)

    na = _rms(oa_ref[...].astype(jnp.float32), gga_ref[...]).astype(jnp.bfloat16)
    nb = _rms(ob_ref[...].astype(jnp.float32), ggb_ref[...]).astype(jnp.bfloat16)
    mix = _dot(na, wo_ref[:WIDTH_A, :]) + _dot(nb, wo_ref[WIDTH_A:, :])
    x1 = x_ref[...] + ga_ref[0] * _rms(mix, gpost_ref[...])
    x1_ref[...] = x1
    h2 = _rms(x1, gpre_ref[...]) * (1.0 + sc_ref[0]) + sh_ref[0]
    h2_ref[...] = h2.astype(h2_ref.dtype)

    h_hi = h2.astype(jnp.bfloat16)
    h_lo = (h2 - h_hi.astype(jnp.float32)).astype(jnp.bfloat16)
    logits = _dot(h_hi, wr_hi_ref[...]) + _dot(h_lo, wr_hi_ref[...]) + _dot(h_hi, wr_lo_ref[...])
    work = logits.T[:N_EXPERTS] + br_ref[...]
    e, t = work.shape
    row = lax.broadcasted_iota(jnp.int32, (e, t), 0)
    chosen = jnp.zeros((e, t), jnp.float32)
    vals, sels, hots = [], [], []
    for _k in range(TOP_K):
        mx = jnp.max(work, axis=0, keepdims=True)
        sel = jnp.min(jnp.where(work == mx, row, e), axis=0, keepdims=True)
        hot = row == sel
        vals.append(mx)
        sels.append(sel)
        hots.append(hot)
        work = jnp.where(hot, -jnp.inf, work)
        chosen = chosen + hot.astype(jnp.float32)
    ex = [jnp.exp(v - vals[0]) for v in vals]
    den = ex[0] + ex[1] + ex[2] + ex[3]

    earlier = (lax.broadcasted_iota(jnp.int32, (t, t), 0) < lax.broadcasted_iota(jnp.int32, (t, t), 1))
    prefix = _dot(chosen.astype(jnp.bfloat16), earlier.astype(jnp.bfloat16))
    ranks = [jnp.sum(jnp.where(h, prefix, 0.0), axis=0, keepdims=True) for h in hots]
    size = jnp.sum(chosen, axis=1, keepdims=True).astype(jnp.int32)
    size_ref[0] = size
    base_ref[0] = run_ref[...]
    run_ref[...] = run_ref[...] + size
    cnt_ref[...] = run_ref[...]

    idx_ref[0] = jnp.concatenate(sels, axis=0)
    wts_ref[0] = jnp.concatenate([v / den for v in ex], axis=0)
    rank_ref[0] = jnp.concatenate(ranks, axis=0).astype(jnp.int32)


def _outproj(oa, ob, x2d, mod3, gga, ggb, gpost, gpre, w_out, w_router, b_router, *, seq):
    t, d = x2d.shape
    tile = OUT_TILE
    per_seq = seq // tile
    row = lambda i: (i, 0)
    const = lambda i: (0, 0)
    modspec = lambda part: pl.BlockSpec((1, 1, d), lambda i: (i // per_seq, 0, part))
    per_tile = lambda shape: pl.BlockSpec((1,) + shape, lambda i: (i, 0, 0))
    n_steps = t // tile
    w_pad = jnp.zeros((d, LANES), jnp.float32).at[:, :N_EXPERTS].set(w_router)
    wr_hi = w_pad.astype(jnp.bfloat16)
    wr_lo = (w_pad - wr_hi.astype(jnp.float32)).astype(jnp.bfloat16)
    outs = pl.pallas_call(
        _outproj_kernel,
        grid=(n_steps,),
        in_specs=[pl.BlockSpec((tile, WIDTH_A), row), pl.BlockSpec((tile, WIDTH_B), row),
                  pl.BlockSpec((tile, d), row),
                  modspec(2), modspec(3), modspec(4),
                  pl.BlockSpec((1, WIDTH_A), const), pl.BlockSpec((1, WIDTH_B), const),
                  pl.BlockSpec((1, d), const), pl.BlockSpec((1, d), const),
                  pl.BlockSpec(w_out.shape, const), pl.BlockSpec((d, LANES), const), pl.BlockSpec((d, LANES), const),
                  pl.BlockSpec((N_EXPERTS, 1), const)],
        out_specs=[pl.BlockSpec((tile, d), row), pl.BlockSpec((tile, d), row),
                   per_tile((TOP_K, tile)), per_tile((TOP_K, tile)), per_tile((TOP_K, tile)),
                   per_tile((N_EXPERTS, 1)), per_tile((N_EXPERTS, 1)),
                   pl.BlockSpec((N_EXPERTS, 1), const)],
        out_shape=[jax.ShapeDtypeStruct((t, d), jnp.float32), jax.ShapeDtypeStruct((t, d), jnp.bfloat16),
                   jax.ShapeDtypeStruct((n_steps, TOP_K, tile), jnp.int32),
                   jax.ShapeDtypeStruct((n_steps, TOP_K, tile), jnp.float32),
                   jax.ShapeDtypeStruct((n_steps, TOP_K, tile), jnp.int32),
                   jax.ShapeDtypeStruct((n_steps, N_EXPERTS, 1), jnp.int32),
                   jax.ShapeDtypeStruct((n_steps, N_EXPERTS, 1), jnp.int32),
                   jax.ShapeDtypeStruct((N_EXPERTS, 1), jnp.int32)],
        scratch_shapes=[pltpu.VMEM((N_EXPERTS, 1), jnp.int32)],
        compiler_params=_params("arbitrary"),
        name="outproj_router",
    )(oa, ob, x2d, mod3, mod3, mod3, gga, ggb, gpost, gpre, w_out, wr_hi, wr_lo, b_router.reshape(N_EXPERTS, 1))
    return outs


def _piece_table(tab_ref, e):
    return (pl.multiple_of(tab_ref[e], ROW_TILE), pl.multiple_of(tab_ref[N_EXPERTS + e], ROW_TILE),
            pl.multiple_of(tab_ref[2 * N_EXPERTS + e], ROW_TILE))


def _to_tiles(x, ref, index=()):
    for c in range(ROW_TILE):
        ref[index + (pl.ds(c, x.shape[0], stride=ROW_TILE), slice(None))] = x[:, c * LANES:(c + 1) * LANES]


def _from_tiles(ref, n_rows, index=()):
    return jnp.concatenate([ref[index + (pl.ds(c, n_rows, stride=ROW_TILE), slice(None))]
                            for c in range(ROW_TILE)], axis=1)


def _dispatch_kernel(fill_from_ref, pad_end_ref, tab_ref, h_ref, lpos_ref, xs_ref,
                     stage_ref, zero_ref, pend_ref, sems, zsem):
    i = pl.program_id(0)
    n_steps = pl.num_programs(0)
    slot = i % 2
    n_local = stage_ref.shape[1] // ROW_TILE
    fill_rows = zero_ref.shape[0]

    def fill(row):
        return pltpu.make_async_copy(zero_ref, xs_ref.at[pl.ds(pl.multiple_of(row, fill_rows), fill_rows)], zsem)

    def fills(e):
        return (pad_end_ref[e] - fill_from_ref[e]) // fill_rows

    def drain(s):
        n = pl.multiple_of(pend_ref[s], ROW_TILE)

        @pl.when(n > 0)
        def _():
            pltpu.make_async_copy(stage_ref.at[s, pl.ds(0, n)], xs_ref.at[pl.ds(0, n)], sems.at[s]).wait()
        pend_ref[s] = 0

    @pl.when(i == 0)
    def _():
        zero_ref[...] = jnp.zeros_like(zero_ref)
        pend_ref[0] = 0
        pend_ref[1] = 0

        def start_e(e, c):
            def start_j(j, c2):
                fill(fill_from_ref[e] + j * fill_rows).start()
                return c2
            lax.fori_loop(0, fills(e), start_j, 0)
            return c + fills(e)
        n_fill = lax.fori_loop(0, N_EXPERTS, start_e, 0)

        def wait_all(j, c):
            fill(0).wait()
            return c
        lax.fori_loop(0, n_fill, wait_all, 0)

    row = lax.broadcasted_iota(jnp.int32, (n_local, h_ref.shape[0]), 0)
    place = row == lpos_ref[0, 0:1, :]
    for k in range(1, TOP_K):
        place = place | (row == lpos_ref[0, k:k + 1, :])
    rows = _dot(place.astype(jnp.bfloat16), h_ref[...])

    drain(slot)
    _to_tiles(rows, stage_ref, (slot,))

    def piece(e, total):
        loc, glob, n = _piece_table(tab_ref, e)

        @pl.when(n > 0)
        def _():
            pltpu.make_async_copy(stage_ref.at[slot, pl.ds(loc, n)], xs_ref.at[pl.ds(glob, n)], sems.at[slot]).start()
        return total + n
    pend_ref[slot] = lax.fori_loop(0, N_EXPERTS, piece, 0)

    tail0 = pad_end_ref[N_EXPERTS - 1]
    n_tail = (xs_ref.shape[0] - tail0) // fill_rows
    per_step = (n_tail + n_steps - 1) // n_steps

    def start_tail(j, c):
        t = i * per_step + j

        @pl.when(t < n_tail)
        def _():
            fill(tail0 + t * fill_rows).start()
        return c
    lax.fori_loop(0, per_step, start_tail, 0)

    @pl.when(i == n_steps - 1)
    def _():
        drain(0)
        drain(1)

        def wait_tail(j, c):
            fill(0).wait()
            return c
        lax.fori_loop(0, n_tail, wait_tail, 0)


def _dispatch(h2, tab_flat, lpos_t, fill_from, pad_end, *, n_rows):
    t, d = h2.shape
    assert d == ROW_TILE * LANES
    tile = DISPATCH_TILE
    n_local = tile * TOP_K
    grid_spec = pltpu.PrefetchScalarGridSpec(
        num_scalar_prefetch=2,
        grid=(t // tile,),
        in_specs=[pl.BlockSpec((TAB_WIDTH,), lambda i, *_: (i,), memory_space=pltpu.SMEM),
                  pl.BlockSpec((tile, d), lambda i, *_: (i, 0)),
                  pl.BlockSpec((1, TOP_K, tile), lambda i, *_: (i, 0, 0))],
        out_specs=pl.BlockSpec(memory_space=pl.ANY),
        scratch_shapes=[pltpu.VMEM((2, n_local * ROW_TILE, LANES), jnp.float32),
                        pltpu.VMEM((FFN_TILE * ROW_TILE, LANES), jnp.float32),
                        pltpu.SMEM((2,), jnp.int32),
                        pltpu.SemaphoreType.DMA((2,)), pltpu.SemaphoreType.DMA(())],
    )
    return pl.pallas_call(
        _dispatch_kernel,
        grid_spec=grid_spec,
        out_shape=jax.ShapeDtypeStruct((n_rows * ROW_TILE, LANES), jnp.float32),
        compiler_params=_params("arbitrary"),
        name="dispatch",
    )(fill_from, pad_end, tab_flat, h2, lpos_t)


def _ffn_kernel(te_ref, rows_ref, n_used_ref, x_ref, w1_ref, b1_ref, w2_ref, b2_ref, y_ref,
                w1p_ref, w2b_ref, act_ref):
    i = pl.program_id(0)
    live = i < n_used_ref[0]
    n_pair = w1_ref.shape[2] // FFN_CHUNK
    half = FFN_CHUNK // 2
    d = w2_ref.shape[2]
    n_sub = jnp.where(live, (rows_ref[i] + FFN_TILE - 1) // FFN_TILE, 0)

    @pl.when(live & ((i == 0) | (te_ref[i] != te_ref[jnp.maximum(i - 1, 0)])))
    def _():
        src = lax.broadcasted_iota(jnp.int32, (FFN_CHUNK, FFN_CHUNK), 0)
        dst = lax.broadcasted_iota(jnp.int32, (FFN_CHUNK, FFN_CHUNK), 1)
        unzip = (src == jnp.where(dst < half, 2 * dst, 2 * (dst - half) + 1)).astype(jnp.bfloat16)
        for c in range(n_pair):
            cs = slice(c * FFN_CHUNK, (c + 1) * FFN_CHUNK)
            w1p_ref[:, cs] = _dot(w1_ref[0, :, cs].astype(jnp.bfloat16), unzip).astype(jnp.bfloat16)
        w2b_ref[...] = w2_ref[0].astype(jnp.bfloat16)

    def compute(n_rows):
        x = _from_tiles(x_ref, n_rows).astype(jnp.bfloat16)
        for c in range(0, n_pair, FFN_GROUP):
            cs = slice(c * FFN_CHUNK, (c + FFN_GROUP) * FFN_CHUNK)
            h = _dot(x, w1p_ref[:, cs]) + b1_ref[0, :, cs]
            g = jnp.concatenate([h[:, k * FFN_CHUNK:k * FFN_CHUNK + half] for k in range(FFN_GROUP)], axis=1)
            u = jnp.concatenate([h[:, k * FFN_CHUNK + half:(k + 1) * FFN_CHUNK] for k in range(FFN_GROUP)], axis=1)
            g = jnp.minimum(g, SWIGLU_LIMIT)
            u = jnp.clip(u, -SWIGLU_LIMIT, SWIGLU_LIMIT)
            act = g * jax.nn.sigmoid(SWIGLU_ALPHA * g) * (u + 1.0)
            act_ref[:n_rows, c * half:(c + FFN_GROUP) * half] = act.astype(act_ref.dtype)
        _to_tiles(_dot(act_ref[:n_rows, :], w2b_ref[...]) + b2_ref[0], y_ref)
        if n_rows * ROW_TILE < y_ref.shape[0]:
            y_ref[n_rows * ROW_TILE:, :] = jnp.zeros((y_ref.shape[0] - n_rows * ROW_TILE, LANES), y_ref.dtype)

    for tiles in range(1, FFN_STEP // FFN_TILE + 1):
        pl.when(n_sub == tiles)(functools.partial(compute, tiles * FFN_TILE))

    @pl.when(n_sub == 0)
    def _():
        y_ref[...] = jnp.zeros_like(y_ref)


def _unzip_bias(b1):
    e, f2 = b1.shape
    half = FFN_CHUNK // 2
    return b1.reshape(e, f2 // FFN_CHUNK, half, 2).transpose(0, 1, 3, 2).reshape(e, 1, f2)


def _ffn(xs, step_expert, step_rows, n_used, w1, b1, w2, b2):
    d, f2 = w1.shape[1:]
    n_steps = xs.shape[0] // (FFN_STEP * ROW_TILE)
    live = lambda i, nu: jnp.minimum(i, nu[0] - 1)
    wspec = lambda shape: pl.BlockSpec((1,) + shape, lambda i, te, nr, nu: (te[live(i, nu)], 0, 0))
    grid_spec = pltpu.PrefetchScalarGridSpec(
        num_scalar_prefetch=3,
        grid=(n_steps,),
        in_specs=[pl.BlockSpec((FFN_STEP * ROW_TILE, LANES), lambda i, te, nr, nu: (live(i, nu), 0)),
                  wspec((d, f2)), wspec((1, f2)), wspec((f2 // 2, d)), wspec((1, d))],
        out_specs=pl.BlockSpec((FFN_STEP * ROW_TILE, LANES), lambda i, te, nr, nu: (i, 0)),
        scratch_shapes=[pltpu.VMEM((d, f2), jnp.bfloat16), pltpu.VMEM((f2 // 2, d), jnp.bfloat16),
                        pltpu.VMEM((FFN_STEP, f2 // 2), jnp.bfloat16)],
    )
    return pl.pallas_call(
        _ffn_kernel,
        grid_spec=grid_spec,
        out_shape=jax.ShapeDtypeStruct(xs.shape, jnp.float32),
        compiler_params=_params("arbitrary"),
        name="expert_ffn",
    )(step_expert, step_rows, n_used, xs, w1, _unzip_bias(b1), w2, b2[:, None, :])


def _combine_kernel(tab_ref, next_tab_ref, lpos_ref, wts_ref, x1_ref, ga_ref, g_ref, y_ref, o_ref, buf_ref, sems):
    i = pl.program_id(0)
    n_steps = pl.num_programs(0)
    slot = i % 2
    tile = x1_ref.shape[0]
    n_local = buf_ref.shape[1] // ROW_TILE

    def fetch(tab, s):
        def piece(e, c):
            loc, glob, n = _piece_table(tab, e)

            @pl.when(n > 0)
            def _():
                pltpu.make_async_copy(y_ref.at[pl.ds(glob, n)], buf_ref.at[s, pl.ds(loc, n)], sems.at[s]).start()
            return c
        lax.fori_loop(0, N_EXPERTS, piece, 0)

    @pl.when(i == 0)
    def _():
        buf_ref[...] = jnp.zeros_like(buf_ref)
        fetch(tab_ref, 0)

    @pl.when(i + 1 < n_steps)
    def _():
        fetch(next_tab_ref, 1 - slot)

    n = pl.multiple_of(tab_ref[3 * N_EXPERTS], ROW_TILE)

    @pl.when(n > 0)
    def _():
        pltpu.make_async_copy(y_ref.at[pl.ds(0, n)], buf_ref.at[slot, pl.ds(0, n)], sems.at[slot]).wait()

    col = lax.broadcasted_iota(jnp.int32, (tile, n_local), 1)
    pos = lpos_ref[...]
    w = wts_ref[...]
    pick = jnp.where(col == pos[:, 0:1], w[:, 0:1], 0.0)
    for k in range(1, TOP_K):
        pick = pick + jnp.where(col == pos[:, k:k + 1], w[:, k:k + 1], 0.0)
    hi = pick.astype(jnp.bfloat16)
    lo = (pick - hi.astype(jnp.float32)).astype(jnp.bfloat16)
    rows = _from_tiles(buf_ref, n_local, (slot,)).astype(jnp.bfloat16)
    y = _dot(hi, rows) + _dot(lo, rows)
    o_ref[...] = x1_ref[...] + ga_ref[0] * _rms(y, g_ref[...])


def _combine(ys, tab_flat, lpos, wts, x1, mod3, g_post, *, seq):
    t, d = x1.shape
    tile = DISPATCH_TILE
    n_steps = t // tile
    per_seq = seq // tile
    n_local = tile * TOP_K
    return pl.pallas_call(
        _combine_kernel,
        grid=(n_steps,),
        in_specs=[pl.BlockSpec((TAB_WIDTH,), lambda i: (i,), memory_space=pltpu.SMEM),
                  pl.BlockSpec((TAB_WIDTH,), lambda i: (jnp.minimum(i + 1, n_steps - 1),), memory_space=pltpu.SMEM),
                  pl.BlockSpec((tile, TOP_K), lambda i: (i, 0)),
                  pl.BlockSpec((tile, TOP_K), lambda i: (i, 0)),
                  pl.BlockSpec((tile, d), lambda i: (i, 0)),
                  pl.BlockSpec((1, 1, d), lambda i: (i // per_seq, 0, 5)),
                  pl.BlockSpec((1, d), lambda i: (0, 0)),
                  pl.BlockSpec(memory_space=pl.ANY)],
        out_specs=pl.BlockSpec((tile, d), lambda i: (i, 0)),
        out_shape=jax.ShapeDtypeStruct((t, d), jnp.float32),
        scratch_shapes=[pltpu.VMEM((2, n_local * ROW_TILE, LANES), ys.dtype), pltpu.SemaphoreType.DMA((2,))],
        compiler_params=_params("arbitrary"),
        name="combine",
    )(tab_flat, tab_flat, lpos, wts, x1, mod3, g_post, ys)


def _rope_tables(seq):
    pos = np.arange(seq)
    n_freq = HEAD_DIM // 4
    freqs = ROPE_BASE ** (-jnp.arange(n_freq, dtype=jnp.float32) / n_freq)
    rows = jnp.asarray(pos // GRID_W, jnp.float32)[:, None] * freqs[None, :]
    cols = jnp.asarray(pos % GRID_W, jnp.float32)[:, None] * freqs[None, :]
    ang = jnp.concatenate([rows, rows, cols, cols], axis=1)
    sign = np.tile(np.repeat([-1.0, 1.0], n_freq), 2).astype(np.float32)
    cos = jnp.cos(ang)
    sin = jnp.sin(ang) * sign[None, :]
    reps = LANES // HEAD_DIM
    return jnp.tile(cos, (1, reps)), jnp.tile(sin, (1, reps))


def kernel(x, c, ctx, c_ctx, w_ada, b_ada, g_pre_mix, g_post_mix, g_pre_ffn, g_post_ffn, w_in, g_grp_a, g_grp_b,
           sink_a, rpb_b, w_out, w_router, b_router, w_mlp1, b_mlp1, w_mlp2, b_mlp2):
    batch, seq, d = x.shape
    n_ctx = ctx.shape[1]
    assert w_ada.shape[0] == 1, "single layer"
    assert seq % (ATT_BLOCKS * GRID_W * NB_Q_ROWS) == 0 and seq // GRID_W >= NB_K_ROWS
    assert seq % (ATT_BLOCKS * BLOCK_A) == 0 and seq >= BLOCK_A + 2 * WINDOW
    n_tok = batch * seq
    bf16 = jnp.bfloat16

    head_order = np.asarray([k * GQA_GROUP + j for j in range(GQA_GROUP) for k in range(N_KV_A)])
    col_order = (head_order[:, None] * HEAD_DIM + np.arange(HEAD_DIM)[None, :]).reshape(-1)

    mod_rows = -(-(batch + 1) // 8) * 8
    cc = jnp.zeros((mod_rows, d), jnp.float32).at[:batch].set(c).at[batch].set(c_ctx)
    mod3 = _ada(cc, w_ada[0], b_ada[0]).reshape(mod_rows, 1, 6 * d)

    w_in0 = w_in[0]
    w_lat = jnp.concatenate([w_in0[:, :WIDTH_A][:, col_order], w_in0[:, WIDTH_A:]], axis=1).astype(bf16)
    kv_cols = np.concatenate([np.arange(WIDTH_A, WIDTH_A + 2 * WIDTH_KV_A),
                              np.arange(WIDTH_A + 2 * WIDTH_KV_A + WIDTH_B, w_in0.shape[1])])
    w_ctx = w_in0[:, kv_cols].astype(bf16)
    g_pre = g_pre_mix[0].reshape(1, d)
    qa, ka, va, qb, kb, vb = _inproj(x.reshape(n_tok, d), mod3, g_pre, w_lat, _rope_tables(seq),
                                     seq=seq, mod_row0=0, latent=True)
    kac, vac, kbc, vbc = _inproj(ctx.reshape(batch * n_ctx, d), mod3, g_pre, w_ctx, None,
                                 seq=n_ctx, mod_row0=batch, latent=False)

    oa = _window_attention(qa, ka.reshape(batch, seq, -1), va.reshape(batch, seq, -1),
                           kac.reshape(batch, n_ctx, -1), vac.reshape(batch, n_ctx, -1),
                           sink_a[0].astype(jnp.float32) * LOG2E, batch=batch, seq=seq)
    tabs, cls_of_block = _nb_bias_tables(rpb_b[0], seq // GRID_W)
    ob = _neighbourhood_attention(qb, kb.reshape(batch, seq, -1), vb.reshape(batch, seq, -1),
                                  kbc.reshape(batch, n_ctx, -1), vbc.reshape(batch, n_ctx, -1),
                                  tabs, cls_of_block, batch=batch, seq=seq)

    w_out0 = w_out[0]
    w_o = jnp.concatenate([w_out0[:WIDTH_A][col_order], w_out0[WIDTH_A:]], axis=0).astype(bf16)
    x1, h2, idx, wts, rank, size, base, counts = _outproj(
        oa, ob, x.reshape(n_tok, d), mod3,
        g_grp_a[0][col_order].reshape(1, -1), g_grp_b[0].reshape(1, -1),
        g_post_mix[0].reshape(1, d), g_pre_ffn[0].reshape(1, d),
        w_o, w_router[0], b_router[0].reshape(1, -1), seq=seq)

    n_tok_tiles = n_tok // DISPATCH_TILE
    size = size.reshape(n_tok_tiles, N_EXPERTS)
    counts = counts.reshape(-1)
    padded = (counts + FFN_STEP - 1) // FFN_STEP * FFN_STEP
    pad_end = jnp.cumsum(padded).astype(jnp.int32)
    pad_start = pad_end - padded
    n_steps = n_tok * TOP_K // FFN_STEP + N_EXPERTS
    n_used = (pad_end[-1:] // FFN_STEP).astype(jnp.int32)
    step_row0 = jnp.arange(n_steps, dtype=jnp.int32) * FFN_STEP
    step_expert = jnp.minimum(jnp.sum(step_row0[:, None] >= pad_end[None, :], axis=1),
                              N_EXPERTS - 1).astype(jnp.int32)
    own = step_expert[:, None] == jnp.arange(N_EXPERTS, dtype=jnp.int32)
    step_rows = jnp.clip(jnp.sum(jnp.where(own, (pad_start + counts)[None, :], 0), axis=1) - step_row0,
                         0, FFN_STEP).astype(jnp.int32)
    fill_from = (pad_start + jnp.maximum((counts + FFN_TILE - 1) // FFN_TILE * FFN_TILE - FFN_TILE, 0)
                 ).astype(jnp.int32)
    local0 = jnp.cumsum(size, axis=1) - size
    global0 = pad_start[None, :] + base.reshape(n_tok_tiles, N_EXPERTS)
    tab = jnp.concatenate([local0, global0, size, jnp.sum(size, axis=1, keepdims=True),
                           jnp.zeros((n_tok_tiles, TAB_WIDTH - 3 * N_EXPERTS - 1), jnp.int32)], axis=1)
    tab = (tab * ROW_TILE).reshape(-1).astype(jnp.int32)
    hot = idx[:, :, None, :] == jnp.arange(N_EXPERTS, dtype=jnp.int32)[None, None, :, None]
    lpos_t = rank + jnp.sum(jnp.where(hot, local0[:, None, :, None], 0), axis=2)
    by_token = lambda a: jnp.swapaxes(a, 1, 2).reshape(n_tok, TOP_K)

    xs = _dispatch(h2, tab, lpos_t, fill_from * ROW_TILE, pad_end * ROW_TILE, n_rows=n_steps * FFN_STEP)
    ys = _ffn(xs, step_expert, step_rows, n_used, w_mlp1[0], b_mlp1[0], w_mlp2[0], b_mlp2[0])
    out = _combine(ys, tab, by_token(lpos_t), by_token(wts), x1, mod3, g_post_ffn[0].reshape(1, d), seq=seq)
    return out.reshape(batch, seq, d)
```

```python
import functools

import numpy as np
import jax
import jax.numpy as jnp
from jax import lax
from jax.experimental import pallas as pl
from jax.experimental.pallas import tpu as pltpu

GRID_W = 64
HEAD_DIM = 64
N_HEADS_A = 8
N_KV_A = 2
GQA_GROUP = N_HEADS_A // N_KV_A
N_HEADS_B = 8
WIDTH_A = N_HEADS_A * HEAD_DIM
WIDTH_KV_A = N_KV_A * HEAD_DIM
WIDTH_B = N_HEADS_B * HEAD_DIM
WINDOW = 128
BLOCK_A = 128
NA_ROWS = 8
NA_COLS = 16
N_EXPERTS = 32
TOP_K = 4
SWIGLU_LIMIT = 7.0
SWIGLU_ALPHA = 1.702
ROPE_BASE = 10000.0
EPS = 1e-6
NEG_INF = -1e30

LANES = 128
VMEM_LIMIT = 56 * 1024 * 1024

LOG2E = 1.4426950408889634
ATT_BLOCKS = 4
NB_Q_ROWS = 2
NB_K_ROWS = NB_Q_ROWS + NA_ROWS
PROJ_TILE = 1024
OUT_TILE = 256
FFN_TILE = 256
FFN_STEP = 4 * FFN_TILE
FFN_CHUNK = 256
FFN_GROUP = 4
DISPATCH_TILE = OUT_TILE
ROW_TILE = 8
TAB_WIDTH = 128


def _params(*sem):
    return pltpu.CompilerParams(dimension_semantics=sem, vmem_limit_bytes=VMEM_LIMIT)


def _rms(x, g):
    return x * lax.rsqrt(jnp.mean(x * x, axis=-1, keepdims=True) + EPS) * g


def _dot(a, b):
    return jnp.dot(a, b, preferred_element_type=jnp.float32)


def _dot_nt(a, b):
    return lax.dot_general(a, b, (((1,), (1,)), ((), ())), preferred_element_type=jnp.float32)


def _ada_kernel(c_ref, w_ref, b_ref, o_ref):
    c = c_ref[...]
    s = (c * jax.nn.sigmoid(c)).astype(jnp.bfloat16)
    o_ref[...] = _dot(s, w_ref[...].astype(jnp.bfloat16)) + b_ref[...]


def _ada(cc, w_ada, b_ada):
    rows, d = cc.shape
    n_out = w_ada.shape[1]
    return pl.pallas_call(
        _ada_kernel,
        grid=(n_out // d,),
        in_specs=[pl.BlockSpec((rows, d), lambda j: (0, 0)),
                  pl.BlockSpec((d, d), lambda j: (0, j)),
                  pl.BlockSpec((1, d), lambda j: (0, j))],
        out_specs=pl.BlockSpec((rows, d), lambda j: (0, j)),
        out_shape=jax.ShapeDtypeStruct((rows, n_out), jnp.float32),
        compiler_params=_params("arbitrary"),
        name="ada",
    )(cc, w_ada, b_ada.reshape(1, n_out))


def _rope(x, cos, sin):
    w = x.shape[1]
    reps = w // LANES
    if reps > 1:
        cos = jnp.concatenate([cos] * reps, axis=1)
        sin = jnp.concatenate([sin] * reps, axis=1)
    lane = lax.broadcasted_iota(jnp.int32, x.shape, 1)
    quarter = HEAD_DIM // 4
    partner = jnp.where(lane % (2 * quarter) < quarter,
                        pltpu.roll(x, w - quarter, 1), pltpu.roll(x, quarter, 1))
    return x * cos + partner * sin


def _inproj_kernel(x_ref, sh_ref, sc_ref, g_ref, w_ref, *rest, latent):
    x = x_ref[...]
    h = _rms(x, g_ref[...]) * (1.0 + sc_ref[0]) + sh_ref[0]
    p = _dot(h.astype(jnp.bfloat16), w_ref[...])
    if latent:
        cos_ref, sin_ref, qa_ref, ka_ref, va_ref, qb_ref, kb_ref, vb_ref = rest
        cos, sin = cos_ref[...], sin_ref[...]
        scale = HEAD_DIM ** -0.5 * LOG2E
        o = 0
        qa_ref[...] = (_rope(p[:, o:o + WIDTH_A], cos, sin) * scale).astype(qa_ref.dtype)
        o += WIDTH_A
        ka_ref[...] = _rope(p[:, o:o + WIDTH_KV_A], cos, sin).astype(ka_ref.dtype)
        o += WIDTH_KV_A
        va_ref[...] = p[:, o:o + WIDTH_KV_A].astype(va_ref.dtype)
        o += WIDTH_KV_A
        qb_ref[...] = (p[:, o:o + WIDTH_B] * scale).astype(qb_ref.dtype)
        o += WIDTH_B
    else:
        ka_ref, va_ref, kb_ref, vb_ref = rest
        o = 0
        ka_ref[...] = p[:, o:o + WIDTH_KV_A].astype(ka_ref.dtype)
        o += WIDTH_KV_A
        va_ref[...] = p[:, o:o + WIDTH_KV_A].astype(va_ref.dtype)
        o += WIDTH_KV_A
    kb_ref[...] = p[:, o:o + WIDTH_B].astype(kb_ref.dtype)
    o += WIDTH_B
    vb_ref[...] = p[:, o:o + WIDTH_B].astype(vb_ref.dtype)


def _inproj(x2d, mod3, g_pre, w, rope_tabs, *, seq, mod_row0, latent):
    t, d = x2d.shape
    tile = min(PROJ_TILE, seq)
    per_seq = seq // tile
    if latent:
        mod_row = lambda i: i // per_seq
    else:
        mod_row = lambda i: mod_row0
    in_specs = [pl.BlockSpec((tile, d), lambda i: (i, 0)),
                pl.BlockSpec((1, 1, d), lambda i: (mod_row(i), 0, 0)),
                pl.BlockSpec((1, 1, d), lambda i: (mod_row(i), 0, 1)),
                pl.BlockSpec((1, d), lambda i: (0, 0)),
                pl.BlockSpec(w.shape, lambda i: (0, 0))]
    args = [x2d, mod3, mod3, g_pre, w]
    widths = [WIDTH_KV_A, WIDTH_KV_A, WIDTH_B, WIDTH_B]
    if latent:
        in_specs += [pl.BlockSpec((tile, LANES), lambda i: (i % per_seq, 0))] * 2
        args += list(rope_tabs)
        widths = [WIDTH_A, WIDTH_KV_A, WIDTH_KV_A, WIDTH_B, WIDTH_B, WIDTH_B]
    return pl.pallas_call(
        functools.partial(_inproj_kernel, latent=latent),
        grid=(t // tile,),
        in_specs=in_specs,
        out_specs=[pl.BlockSpec((tile, wd), lambda i: (i, 0)) for wd in widths],
        out_shape=[jax.ShapeDtypeStruct((t, wd), jnp.bfloat16) for wd in widths],
        compiler_params=_params("parallel"),
        name="inproj_latent" if latent else "inproj_ctx",
    )(*args)


def _ones_beside(v):
    return jnp.concatenate([v, jnp.ones_like(v)], axis=1)


def _group_attention(q_ref, o_ref, rows, groups, s_ref, p_ref, m_ref):
    tq = rows.stop - rows.start
    low = lax.broadcasted_iota(jnp.int32, (tq, LANES), 1) < HEAD_DIM
    layout = []
    base = 0
    for pairs, k_loc, k_ctx, _, _, _, _ in groups:
        heads = [(j, 0) for j in pairs] + [(j, 1) for j in pairs]
        n_loc = k_loc.shape[0]
        stacked = []
        for j, half in heads:
            q = q_ref[rows, j * LANES:(j + 1) * LANES]
            stacked.append(jnp.where(low if half == 0 else ~low, q, jnp.zeros_like(q)))
        qs = jnp.concatenate(stacked, axis=0)
        n = len(heads) * tq
        s_ref[base:base + n, :n_loc] = _dot_nt(qs, k_loc)
        s_ref[base:base + n, n_loc:] = _dot_nt(qs, k_ctx)
        layout.append((base, heads, n_loc))
        base += n
    for (base, heads, n_loc), group in zip(layout, groups):
        bias_of, sink_of = group[5], group[6]
        for g, (j, half) in enumerate(heads):
            r = slice(base + g * tq, base + (g + 1) * tq)
            s_loc = s_ref[r, :n_loc] + bias_of(j, half)
            s_ctx = s_ref[r, n_loc:]
            m = jnp.maximum(jnp.max(s_loc, axis=1, keepdims=True), jnp.max(s_ctx, axis=1, keepdims=True))
            sink = sink_of(j, half)
            if sink is not None:
                m = jnp.maximum(m, sink)
            p_ref[r, :n_loc] = jnp.exp2((s_loc - m).astype(jnp.bfloat16))
            p_ref[r, n_loc:] = jnp.exp2((s_ctx - m).astype(jnp.bfloat16))
            m_ref[r, :] = m
    for (base, heads, n_loc), group in zip(layout, groups):
        pairs, _, _, v_loc, v_ctx, _, sink_of = group
        n = len(heads) * tq
        both = (_dot(p_ref[base:base + n, :n_loc], _ones_beside(v_loc))
                + _dot(p_ref[base:base + n, n_loc:], _ones_beside(v_ctx)))
        for a, j in enumerate(pairs):
            outs = []
            for half in range(2):
                g = half * len(pairs) + a
                r = slice(g * tq, (g + 1) * tq)
                den = both[r, LANES:]
                sink = sink_of(j, half)
                if sink is not None:
                    den = den + jnp.exp2(sink - m_ref[base + g * tq:base + (g + 1) * tq, :])
                outs.append(both[r, :LANES] / den)
            o_ref[rows, j * LANES:(j + 1) * LANES] = jnp.where(low, outs[0], outs[1]).astype(o_ref.dtype)


def _attention_scratch(tq, n_heads, n_keys):
    return [pltpu.VMEM((n_heads * tq, n_keys), jnp.float32), pltpu.VMEM((n_heads * tq, n_keys), jnp.bfloat16),
            pltpu.VMEM((n_heads * tq, 1), jnp.float32)]


def _win_kernel(sink_ref, q_ref, k_ref, v_ref, kc_ref, vc_ref, o_ref, *scratch, seq):
    span = BLOCK_A + 2 * WINDOW
    pairs = list(range(GQA_GROUP))
    for sb in range(ATT_BLOCKS):
        i = pl.program_id(1) * ATT_BLOCKS + sb
        start = pl.multiple_of(jnp.clip(i * BLOCK_A - WINDOW, 0, seq - span), BLOCK_A)
        qpos = i * BLOCK_A + lax.broadcasted_iota(jnp.int32, (BLOCK_A, span), 0)
        kpos = start + lax.broadcasted_iota(jnp.int32, (BLOCK_A, span), 1)
        bias = jnp.where(jnp.abs(kpos - qpos) <= WINDOW, 0.0, NEG_INF).astype(jnp.float32)
        groups = [([j], k_ref[0, pl.ds(start, span), :], kc_ref[0], v_ref[0, pl.ds(start, span), :], vc_ref[0],
                   lambda j, half: bias, lambda j, half: sink_ref[half * GQA_GROUP + j]) for j in pairs]
        _group_attention(q_ref, o_ref, slice(sb * BLOCK_A, (sb + 1) * BLOCK_A), groups, *scratch)


def _window_attention(qa, ka, va, kac, vac, sink, *, batch, seq):
    tq = ATT_BLOCKS * BLOCK_A
    nb = seq // tq
    n_ctx = kac.shape[1]
    return pl.pallas_call(
        functools.partial(_win_kernel, seq=seq),
        grid=(batch, nb),
        in_specs=[pl.BlockSpec(memory_space=pltpu.SMEM),
                  pl.BlockSpec((tq, WIDTH_A), lambda b, i: (b * nb + i, 0)),
                  pl.BlockSpec((1, seq, WIDTH_KV_A), lambda b, i: (b, 0, 0)),
                  pl.BlockSpec((1, seq, WIDTH_KV_A), lambda b, i: (b, 0, 0)),
                  pl.BlockSpec((1, n_ctx, WIDTH_KV_A), lambda b, i: (b, 0, 0)),
                  pl.BlockSpec((1, n_ctx, WIDTH_KV_A), lambda b, i: (b, 0, 0))],
        out_specs=pl.BlockSpec((tq, WIDTH_A), lambda b, i: (b * nb + i, 0)),
        out_shape=jax.ShapeDtypeStruct((batch * seq, WIDTH_A), jnp.bfloat16),
        scratch_shapes=_attention_scratch(BLOCK_A, N_HEADS_A, BLOCK_A + 2 * WINDOW + n_ctx),
        compiler_params=_params("parallel", "arbitrary"),
        name="window_attention",
    )(sink, qa, ka, va, kac, vac)


def _nb_kernel(q_ref, k_ref, v_ref, kc_ref, vc_ref, *rest, rows_n):
    tab_refs, o_ref, scratch = rest[:ATT_BLOCKS], rest[ATT_BLOCKS], rest[ATT_BLOCKS + 1:]
    n_keys = NB_K_ROWS * GRID_W
    tq = NB_Q_ROWS * GRID_W
    for sb in range(ATT_BLOCKS):
        m = pl.program_id(1) * ATT_BLOCKS + sb
        start_row = jnp.clip(NB_Q_ROWS * m - NA_ROWS // 2, 0, rows_n - NB_K_ROWS)
        start = pl.multiple_of(start_row * GRID_W, LANES)
        tab_ref = tab_refs[sb]
        groups = []
        for j in range(q_ref.shape[1] // LANES):
            cols = slice(j * LANES, (j + 1) * LANES)
            groups.append(([j], k_ref[0, pl.ds(start, n_keys), cols], kc_ref[0, :, cols],
                           v_ref[0, pl.ds(start, n_keys), cols], vc_ref[0, :, cols],
                           lambda j, half: tab_ref[0, 2 * j + half].astype(jnp.float32), lambda j, half: None))
        _group_attention(q_ref, o_ref, slice(sb * tq, (sb + 1) * tq), groups, *scratch)


def _nb_classes(rows_n):
    n_blocks = rows_n // NB_Q_ROWS
    sig = {}
    cls_of_block = []
    reps = []
    for m in range(n_blocks):
        start_row = int(np.clip(NB_Q_ROWS * m - NA_ROWS // 2, 0, rows_n - NB_K_ROWS))
        key = tuple((start_row - r, int(np.clip(r - NA_ROWS // 2, 0, rows_n - NA_ROWS)) - r)
                    for r in range(NB_Q_ROWS * m, NB_Q_ROWS * (m + 1)))
        if key not in sig:
            sig[key] = len(reps)
            reps.append(m)
        cls_of_block.append(sig[key])
    return np.asarray(cls_of_block, np.int32), reps


def _nb_bias_tables(rpb, rows_n):
    cls_of_block, reps = _nb_classes(rows_n)
    n_heads = rpb.shape[0]
    cq = np.arange(GRID_W)[:, None]
    ck = np.arange(GRID_W)[None, :]
    cs = np.clip(cq - NA_COLS // 2, 0, GRID_W - NA_COLS)
    col_ok = (ck >= cs) & (ck < cs + NA_COLS)
    pick = ((ck - cq + NA_COLS - 1)[None] == np.arange(2 * NA_COLS - 1)[:, None, None]) & col_ok[None]
    tiles = jnp.einsum('hrd,dqk->hrqk', rpb.astype(jnp.float32), jnp.asarray(pick, jnp.float32),
                       precision=lax.Precision.HIGHEST)
    tiles = jnp.where(jnp.asarray(col_ok)[None, None], tiles * LOG2E, NEG_INF).astype(jnp.bfloat16)
    blocked = jnp.full((n_heads, GRID_W, GRID_W), NEG_INF, jnp.bfloat16)
    tabs = []
    for m in reps:
        start_row = int(np.clip(NB_Q_ROWS * m - NA_ROWS // 2, 0, rows_n - NB_K_ROWS))
        q_rows = []
        for r in range(NB_Q_ROWS * m, NB_Q_ROWS * (m + 1)):
            rs = int(np.clip(r - NA_ROWS // 2, 0, rows_n - NA_ROWS))
            q_rows.append(jnp.concatenate(
                [tiles[:, krow - r + NA_ROWS - 1] if rs <= krow < rs + NA_ROWS else blocked
                 for krow in range(start_row, start_row + NB_K_ROWS)], axis=2))
        tabs.append(jnp.concatenate(q_rows, axis=1))
    return jnp.stack(tabs), cls_of_block


def _neighbourhood_attention(qb, kb, vb, kbc, vbc, tabs, cls_of_block, *, batch, seq):
    rows_n = seq // GRID_W
    tq = ATT_BLOCKS * NB_Q_ROWS * GRID_W
    nstep = seq // tq
    n_ctx = kbc.shape[1]

    def tab_spec(sb):
        return pl.BlockSpec((1,) + tabs.shape[1:], lambda b, m, c: (c[m * ATT_BLOCKS + sb], 0, 0, 0))

    grid_spec = pltpu.PrefetchScalarGridSpec(
        num_scalar_prefetch=1,
        grid=(batch, nstep),
        in_specs=[pl.BlockSpec((tq, WIDTH_B), lambda b, m, c: (b * nstep + m, 0)),
                  pl.BlockSpec((1, seq, WIDTH_B), lambda b, m, c: (b, 0, 0)),
                  pl.BlockSpec((1, seq, WIDTH_B), lambda b, m, c: (b, 0, 0)),
                  pl.BlockSpec((1, n_ctx, WIDTH_B), lambda b, m, c: (b, 0, 0)),
                  pl.BlockSpec((1, n_ctx, WIDTH_B), lambda b, m, c: (b, 0, 0))]
                 + [tab_spec(sb) for sb in range(ATT_BLOCKS)],
        out_specs=pl.BlockSpec((tq, WIDTH_B), lambda b, m, c: (b * nstep + m, 0)),
        scratch_shapes=_attention_scratch(NB_Q_ROWS * GRID_W, N_HEADS_B, NB_K_ROWS * GRID_W + n_ctx),
    )

    def body(c_ref, *refs):
        _nb_kernel(*refs, rows_n=rows_n)

    return pl.pallas_call(
        body,
        grid_spec=grid_spec,
        out_shape=jax.ShapeDtypeStruct((batch * seq, WIDTH_B), jnp.bfloat16),
        compiler_params=_params("parallel", "arbitrary"),
        name="neighbourhood_attention",
    )(jnp.asarray(cls_of_block), qb, kb, vb, kbc, vbc, *([tabs] * ATT_BLOCKS))


def _outproj_kernel(oa_ref, ob_ref, x_ref, ga_ref, sh_ref, sc_ref, gga_ref, ggb_ref, gpost_ref, gpre_ref,
                    wo_ref, wr_hi_ref, wr_lo_ref, br_ref,
                    x1_ref, h2_ref, idx_ref, wts_ref, rank_ref, size_ref, base_ref, cnt_ref, run_ref):
    i = pl.program_id(0)

    @pl.when(i == 0)
    def _():
        run_ref[...] = jnp.zeros_like(run_ref)

    na = _rms(oa_ref[...].astype(jnp.float32), gga_ref[...]).astype(jnp.bfloat16)
    nb = _rms(ob_ref[...].astype(jnp.float32), ggb_ref[...]).astype(jnp.bfloat16)
    mix = _dot(na, wo_ref[:WIDTH_A, :]) + _dot(nb, wo_ref[WIDTH_A:, :])
    x1 = x_ref[...] + ga_ref[0] * _rms(mix, gpost_ref[...])
    x1_ref[...] = x1
    h2 = _rms(x1, gpre_ref[...]) * (1.0 + sc_ref[0]) + sh_ref[0]
    h2_ref[...] = h2.astype(h2_ref.dtype)

    h_hi = h2.astype(jnp.bfloat16)
    h_lo = (h2 - h_hi.astype(jnp.float32)).astype(jnp.bfloat16)
    logits = _dot(h_hi, wr_hi_ref[...]) + _dot(h_lo, wr_hi_ref[...]) + _dot(h_hi, wr_lo_ref[...])
    work = logits.T[:N_EXPERTS] + br_ref[...]
    e, t = work.shape
    row = lax.broadcasted_iota(jnp.int32, (e, t), 0)
    chosen = jnp.zeros((e, t), jnp.float32)
    vals, sels, hots = [], [], []
    for _k in range(TOP_K):
        mx = jnp.max(work, axis=0, keepdims=True)
        sel = jnp.min(jnp.where(work == mx, row, e), axis=0, keepdims=True)
        hot = row == sel
        vals.append(mx)
        sels.append(sel)
        hots.append(hot)
        work = jnp.where(hot, -jnp.inf, work)
        chosen = chosen + hot.astype(jnp.float32)
    ex = [jnp.exp(v - vals[0]) for v in vals]
    den = ex[0] + ex[1] + ex[2] + ex[3]

    earlier = (lax.broadcasted_iota(jnp.int32, (t, t), 0) < lax.broadcasted_iota(jnp.int32, (t, t), 1))
    prefix = _dot(chosen.astype(jnp.bfloat16), earlier.astype(jnp.bfloat16))
    ranks = [jnp.sum(jnp.where(h, prefix, 0.0), axis=0, keepdims=True) for h in hots]
    size = jnp.sum(chosen, axis=1, keepdims=True).astype(jnp.int32)
    size_ref[0] = size
    base_ref[0] = run_ref[...]
    run_ref[...] = run_ref[...] + size
    cnt_ref[...] = run_ref[...]

    idx_ref[0] = jnp.concatenate(sels, axis=0)
    wts_ref[0] = jnp.concatenate([v / den for v in ex], axis=0)
    rank_ref[0] = jnp.concatenate(ranks, axis=0).astype(jnp.int32)


def _outproj(oa, ob, x2d, mod3, gga, ggb, gpost, gpre, w_out, w_router, b_router, *, seq):
    t, d = x2d.shape
    tile = OUT_TILE
    per_seq = seq // tile
    row = lambda i: (i, 0)
    const = lambda i: (0, 0)
    modspec = lambda part: pl.BlockSpec((1, 1, d), lambda i: (i // per_seq, 0, part))
    per_tile = lambda shape: pl.BlockSpec((1,) + shape, lambda i: (i, 0, 0))
    n_steps = t // tile
    w_pad = jnp.zeros((d, LANES), jnp.float32).at[:, :N_EXPERTS].set(w_router)
    wr_hi = w_pad.astype(jnp.bfloat16)
    wr_lo = (w_pad - wr_hi.astype(jnp.float32)).astype(jnp.bfloat16)
    outs = pl.pallas_call(
        _outproj_kernel,
        grid=(n_steps,),
        in_specs=[pl.BlockSpec((tile, WIDTH_A), row), pl.BlockSpec((tile, WIDTH_B), row),
                  pl.BlockSpec((tile, d), row),
                  modspec(2), modspec(3), modspec(4),
                  pl.BlockSpec((1, WIDTH_A), const), pl.BlockSpec((1, WIDTH_B), const),
                  pl.BlockSpec((1, d), const), pl.BlockSpec((1, d), const),
                  pl.BlockSpec(w_out.shape, const), pl.BlockSpec((d, LANES), const), pl.BlockSpec((d, LANES), const),
                  pl.BlockSpec((N_EXPERTS, 1), const)],
        out_specs=[pl.BlockSpec((tile, d), row), pl.BlockSpec((tile, d), row),
                   per_tile((TOP_K, tile)), per_tile((TOP_K, tile)), per_tile((TOP_K, tile)),
                   per_tile((N_EXPERTS, 1)), per_tile((N_EXPERTS, 1)),
                   pl.BlockSpec((N_EXPERTS, 1), const)],
        out_shape=[jax.ShapeDtypeStruct((t, d), jnp.float32), jax.ShapeDtypeStruct((t, d), jnp.bfloat16),
                   jax.ShapeDtypeStruct((n_steps, TOP_K, tile), jnp.int32),
                   jax.ShapeDtypeStruct((n_steps, TOP_K, tile), jnp.float32),
                   jax.ShapeDtypeStruct((n_steps, TOP_K, tile), jnp.int32),
                   jax.ShapeDtypeStruct((n_steps, N_EXPERTS, 1), jnp.int32),
                   jax.ShapeDtypeStruct((n_steps, N_EXPERTS, 1), jnp.int32),
                   jax.ShapeDtypeStruct((N_EXPERTS, 1), jnp.int32)],
        scratch_shapes=[pltpu.VMEM((N_EXPERTS, 1), jnp.int32)],
        compiler_params=_params("arbitrary"),
        name="outproj_router",
    )(oa, ob, x2d, mod3, mod3, mod3, gga, ggb, gpost, gpre, w_out, wr_hi, wr_lo, b_router.reshape(N_EXPERTS, 1))
    return outs


def _piece_table(tab_ref, e):
    return (pl.multiple_of(tab_ref[e], ROW_TILE), pl.multiple_of(tab_ref[N_EXPERTS + e], ROW_TILE),
            pl.multiple_of(tab_ref[2 * N_EXPERTS + e], ROW_TILE))


def _to_tiles(x, ref, index=()):
    for c in range(ROW_TILE):
        ref[index + (pl.ds(c, x.shape[0], stride=ROW_TILE), slice(None))] = x[:, c * LANES:(c + 1) * LANES]


def _from_tiles(ref, n_rows, index=()):
    return jnp.concatenate([ref[index + (pl.ds(c, n_rows, stride=ROW_TILE), slice(None))]
                            for c in range(ROW_TILE)], axis=1)


def _dispatch_kernel(fill_from_ref, pad_end_ref, tab_ref, h_ref, lpos_ref, xs_ref,
                     stage_ref, zero_ref, pend_ref, sems, zsem):
    i = pl.program_id(0)
    n_steps = pl.num_programs(0)
    slot = i % 2
    n_local = stage_ref.shape[1] // ROW_TILE
    fill_rows = zero_ref.shape[0]

    def fill(row):
        return pltpu.make_async_copy(zero_ref, xs_ref.at[pl.ds(pl.multiple_of(row, fill_rows), fill_rows)], zsem)

    def fills(e):
        return (pad_end_ref[e] - fill_from_ref[e]) // fill_rows

    def drain(s):
        n = pl.multiple_of(pend_ref[s], ROW_TILE)

        @pl.when(n > 0)
        def _():
            pltpu.make_async_copy(stage_ref.at[s, pl.ds(0, n)], xs_ref.at[pl.ds(0, n)], sems.at[s]).wait()
        pend_ref[s] = 0

    @pl.when(i == 0)
    def _():
        zero_ref[...] = jnp.zeros_like(zero_ref)
        pend_ref[0] = 0
        pend_ref[1] = 0

        def start_e(e, c):
            def start_j(j, c2):
                fill(fill_from_ref[e] + j * fill_rows).start()
                return c2
            lax.fori_loop(0, fills(e), start_j, 0)
            return c + fills(e)
        n_fill = lax.fori_loop(0, N_EXPERTS, start_e, 0)

        def wait_all(j, c):
            fill(0).wait()
            return c
        lax.fori_loop(0, n_fill, wait_all, 0)

    row = lax.broadcasted_iota(jnp.int32, (n_local, h_ref.shape[0]), 0)
    place = row == lpos_ref[0, 0:1, :]
    for k in range(1, TOP_K):
        place = place | (row == lpos_ref[0, k:k + 1, :])
    rows = _dot(place.astype(jnp.bfloat16), h_ref[...])

    drain(slot)
    _to_tiles(rows, stage_ref, (slot,))

    def piece(e, total):
        loc, glob, n = _piece_table(tab_ref, e)

        @pl.when(n > 0)
        def _():
            pltpu.make_async_copy(stage_ref.at[slot, pl.ds(loc, n)], xs_ref.at[pl.ds(glob, n)], sems.at[slot]).start()
        return total + n
    pend_ref[slot] = lax.fori_loop(0, N_EXPERTS, piece, 0)

    tail0 = pad_end_ref[N_EXPERTS - 1]
    n_tail = (xs_ref.shape[0] - tail0) // fill_rows
    per_step = (n_tail + n_steps - 1) // n_steps

    def start_tail(j, c):
        t = i * per_step + j

        @pl.when(t < n_tail)
        def _():
            fill(tail0 + t * fill_rows).start()
        return c
    lax.fori_loop(0, per_step, start_tail, 0)

    @pl.when(i == n_steps - 1)
    def _():
        drain(0)
        drain(1)

        def wait_tail(j, c):
            fill(0).wait()
            return c
        lax.fori_loop(0, n_tail, wait_tail, 0)


def _dispatch(h2, tab_flat, lpos_t, fill_from, pad_end, *, n_rows):
    t, d = h2.shape
    assert d == ROW_TILE * LANES
    tile = DISPATCH_TILE
    n_local = tile * TOP_K
    grid_spec = pltpu.PrefetchScalarGridSpec(
        num_scalar_prefetch=2,
        grid=(t // tile,),
        in_specs=[pl.BlockSpec((TAB_WIDTH,), lambda i, *_: (i,), memory_space=pltpu.SMEM),
                  pl.BlockSpec((tile, d), lambda i, *_: (i, 0)),
                  pl.BlockSpec((1, TOP_K, tile), lambda i, *_: (i, 0, 0))],
        out_specs=pl.BlockSpec(memory_space=pl.ANY),
        scratch_shapes=[pltpu.VMEM((2, n_local * ROW_TILE, LANES), jnp.float32),
                        pltpu.VMEM((FFN_TILE * ROW_TILE, LANES), jnp.float32),
                        pltpu.SMEM((2,), jnp.int32),
                        pltpu.SemaphoreType.DMA((2,)), pltpu.SemaphoreType.DMA(())],
    )
    return pl.pallas_call(
        _dispatch_kernel,
        grid_spec=grid_spec,
        out_shape=jax.ShapeDtypeStruct((n_rows * ROW_TILE, LANES), jnp.float32),
        compiler_params=_params("arbitrary"),
        name="dispatch",
    )(fill_from, pad_end, tab_flat, h2, lpos_t)


def _ffn_kernel(te_ref, rows_ref, n_used_ref, x_ref, w1_ref, b1_ref, w2_ref, b2_ref, y_ref,
                w1p_ref, w2b_ref, act_ref):
    i = pl.program_id(0)
    live = i < n_used_ref[0]
    n_pair = w1_ref.shape[2] // FFN_CHUNK
    half = FFN_CHUNK // 2
    d = w2_ref.shape[2]
    n_sub = jnp.where(live, (rows_ref[i] + FFN_TILE - 1) // FFN_TILE, 0)

    @pl.when(live & ((i == 0) | (te_ref[i] != te_ref[jnp.maximum(i - 1, 0)])))
    def _():
        src = lax.broadcasted_iota(jnp.int32, (FFN_CHUNK, FFN_CHUNK), 0)
        dst = lax.broadcasted_iota(jnp.int32, (FFN_CHUNK, FFN_CHUNK), 1)
        unzip = (src == jnp.where(dst < half, 2 * dst, 2 * (dst - half) + 1)).astype(jnp.bfloat16)
        for c in range(n_pair):
            cs = slice(c * FFN_CHUNK, (c + 1) * FFN_CHUNK)
            w1p_ref[:, cs] = _dot(w1_ref[0, :, cs].astype(jnp.bfloat16), unzip).astype(jnp.bfloat16)
        w2b_ref[...] = w2_ref[0].astype(jnp.bfloat16)

    def compute(n_rows):
        x = _from_tiles(x_ref, n_rows).astype(jnp.bfloat16)
        for c in range(0, n_pair, FFN_GROUP):
            cs = slice(c * FFN_CHUNK, (c + FFN_GROUP) * FFN_CHUNK)
            h = _dot(x, w1p_ref[:, cs]) + b1_ref[0, :, cs]
            g = jnp.concatenate([h[:, k * FFN_CHUNK:k * FFN_CHUNK + half] for k in range(FFN_GROUP)], axis=1)
            u = jnp.concatenate([h[:, k * FFN_CHUNK + half:(k + 1) * FFN_CHUNK] for k in range(FFN_GROUP)], axis=1)
            g = jnp.minimum(g, SWIGLU_LIMIT)
            u = jnp.clip(u, -SWIGLU_LIMIT, SWIGLU_LIMIT)
            act = g * jax.nn.sigmoid(SWIGLU_ALPHA * g) * (u + 1.0)
            act_ref[:n_rows, c * half:(c + FFN_GROUP) * half] = act.astype(act_ref.dtype)
        _to_tiles(_dot(act_ref[:n_rows, :], w2b_ref[...]) + b2_ref[0], y_ref)
        if n_rows * ROW_TILE < y_ref.shape[0]:
            y_ref[n_rows * ROW_TILE:, :] = jnp.zeros((y_ref.shape[0] - n_rows * ROW_TILE, LANES), y_ref.dtype)

    for tiles in range(1, FFN_STEP // FFN_TILE + 1):
        pl.when(n_sub == tiles)(functools.partial(compute, tiles * FFN_TILE))

    @pl.when(n_sub == 0)
    def _():
        y_ref[...] = jnp.zeros_like(y_ref)


def _unzip_bias(b1):
    e, f2 = b1.shape
    half = FFN_CHUNK // 2
    return b1.reshape(e, f2 // FFN_CHUNK, half, 2).transpose(0, 1, 3, 2).reshape(e, 1, f2)


def _ffn(xs, step_expert, step_rows, n_used, w1, b1, w2, b2):
    d, f2 = w1.shape[1:]
    n_steps = xs.shape[0] // (FFN_STEP * ROW_TILE)
    live = lambda i, nu: jnp.minimum(i, nu[0] - 1)
    wspec = lambda shape: pl.BlockSpec((1,) + shape, lambda i, te, nr, nu: (te[live(i, nu)], 0, 0))
    grid_spec = pltpu.PrefetchScalarGridSpec(
        num_scalar_prefetch=3,
        grid=(n_steps,),
        in_specs=[pl.BlockSpec((FFN_STEP * ROW_TILE, LANES), lambda i, te, nr, nu: (live(i, nu), 0)),
                  wspec((d, f2)), wspec((1, f2)), wspec((f2 // 2, d)), wspec((1, d))],
        out_specs=pl.BlockSpec((FFN_STEP * ROW_TILE, LANES), lambda i, te, nr, nu: (i, 0)),
        scratch_shapes=[pltpu.VMEM((d, f2), jnp.bfloat16), pltpu.VMEM((f2 // 2, d), jnp.bfloat16),
                        pltpu.VMEM((FFN_STEP, f2 // 2), jnp.bfloat16)],
    )
    return pl.pallas_call(
        _ffn_kernel,
        grid_spec=grid_spec,
        out_shape=jax.ShapeDtypeStruct(xs.shape, jnp.float32),
        compiler_params=_params("arbitrary"),
        name="expert_ffn",
    )(step_expert, step_rows, n_used, xs, w1, _unzip_bias(b1), w2, b2[:, None, :])


def _combine_kernel(tab_ref, next_tab_ref, lpos_ref, wts_ref, x1_ref, ga_ref, g_ref, y_ref, o_ref, buf_ref, sems):
    i = pl.program_id(0)
    n_steps = pl.num_programs(0)
    slot = i % 2
    tile = x1_ref.shape[0]
    n_local = buf_ref.shape[1] // ROW_TILE

    def fetch(tab, s):
        def piece(e, c):
            loc, glob, n = _piece_table(tab, e)

            @pl.when(n > 0)
            def _():
                pltpu.make_async_copy(y_ref.at[pl.ds(glob, n)], buf_ref.at[s, pl.ds(loc, n)], sems.at[s]).start()
            return c
        lax.fori_loop(0, N_EXPERTS, piece, 0)

    @pl.when(i == 0)
    def _():
        buf_ref[...] = jnp.zeros_like(buf_ref)
        fetch(tab_ref, 0)

    @pl.when(i + 1 < n_steps)
    def _():
        fetch(next_tab_ref, 1 - slot)

    n = pl.multiple_of(tab_ref[3 * N_EXPERTS], ROW_TILE)

    @pl.when(n > 0)
    def _():
        pltpu.make_async_copy(y_ref.at[pl.ds(0, n)], buf_ref.at[slot, pl.ds(0, n)], sems.at[slot]).wait()

    col = lax.broadcasted_iota(jnp.int32, (tile, n_local), 1)
    pos = lpos_ref[...]
    w = wts_ref[...]
    pick = jnp.where(col == pos[:, 0:1], w[:, 0:1], 0.0)
    for k in range(1, TOP_K):
        pick = pick + jnp.where(col == pos[:, k:k + 1], w[:, k:k + 1], 0.0)
    hi = pick.astype(jnp.bfloat16)
    lo = (pick - hi.astype(jnp.float32)).astype(jnp.bfloat16)
    rows = _from_tiles(buf_ref, n_local, (slot,)).astype(jnp.bfloat16)
    y = _dot(hi, rows) + _dot(lo, rows)
    o_ref[...] = x1_ref[...] + ga_ref[0] * _rms(y, g_ref[...])


def _combine(ys, tab_flat, lpos, wts, x1, mod3, g_post, *, seq):
    t, d = x1.shape
    tile = DISPATCH_TILE
    n_steps = t // tile
    per_seq = seq // tile
    n_local = tile * TOP_K
    return pl.pallas_call(
        _combine_kernel,
        grid=(n_steps,),
        in_specs=[pl.BlockSpec((TAB_WIDTH,), lambda i: (i,), memory_space=pltpu.SMEM),
                  pl.BlockSpec((TAB_WIDTH,), lambda i: (jnp.minimum(i + 1, n_steps - 1),), memory_space=pltpu.SMEM),
                  pl.BlockSpec((tile, TOP_K), lambda i: (i, 0)),
                  pl.BlockSpec((tile, TOP_K), lambda i: (i, 0)),
                  pl.BlockSpec((tile, d), lambda i: (i, 0)),
                  pl.BlockSpec((1, 1, d), lambda i: (i // per_seq, 0, 5)),
                  pl.BlockSpec((1, d), lambda i: (0, 0)),
                  pl.BlockSpec(memory_space=pl.ANY)],
        out_specs=pl.BlockSpec((tile, d), lambda i: (i, 0)),
        out_shape=jax.ShapeDtypeStruct((t, d), jnp.float32),
        scratch_shapes=[pltpu.VMEM((2, n_local * ROW_TILE, LANES), ys.dtype), pltpu.SemaphoreType.DMA((2,))],
        compiler_params=_params("arbitrary"),
        name="combine",
    )(tab_flat, tab_flat, lpos, wts, x1, mod3, g_post, ys)


def _rope_tables(seq):
    pos = np.arange(seq)
    n_freq = HEAD_DIM // 4
    freqs = ROPE_BASE ** (-jnp.arange(n_freq, dtype=jnp.float32) / n_freq)
    rows = jnp.asarray(pos // GRID_W, jnp.float32)[:, None] * freqs[None, :]
    cols = jnp.asarray(pos % GRID_W, jnp.float32)[:, None] * freqs[None, :]
    ang = jnp.concatenate([rows, rows, cols, cols], axis=1)
    sign = np.tile(np.repeat([-1.0, 1.0], n_freq), 2).astype(np.float32)
    cos = jnp.cos(ang)
    sin = jnp.sin(ang) * sign[None, :]
    reps = LANES // HEAD_DIM
    return jnp.tile(cos, (1, reps)), jnp.tile(sin, (1, reps))


def kernel(x, c, ctx, c_ctx, w_ada, b_ada, g_pre_mix, g_post_mix, g_pre_ffn, g_post_ffn, w_in, g_grp_a, g_grp_b,
           sink_a, rpb_b, w_out, w_router, b_router, w_mlp1, b_mlp1, w_mlp2, b_mlp2):
    batch, seq, d = x.shape
    n_ctx = ctx.shape[1]
    assert w_ada.shape[0] == 1, "single layer"
    assert seq % (ATT_BLOCKS * GRID_W * NB_Q_ROWS) == 0 and seq // GRID_W >= NB_K_ROWS
    assert seq % (ATT_BLOCKS * BLOCK_A) == 0 and seq >= BLOCK_A + 2 * WINDOW
    n_tok = batch * seq
    bf16 = jnp.bfloat16

    head_order = np.asarray([k * GQA_GROUP + j for j in range(GQA_GROUP) for k in range(N_KV_A)])
    col_order = (head_order[:, None] * HEAD_DIM + np.arange(HEAD_DIM)[None, :]).reshape(-1)

    mod_rows = -(-(batch + 1) // 8) * 8
    cc = jnp.zeros((mod_rows, d), jnp.float32).at[:batch].set(c).at[batch].set(c_ctx)
    mod3 = _ada(cc, w_ada[0], b_ada[0]).reshape(mod_rows, 1, 6 * d)

    w_in0 = w_in[0]
    w_lat = jnp.concatenate([w_in0[:, :WIDTH_A][:, col_order], w_in0[:, WIDTH_A:]], axis=1).astype(bf16)
    kv_cols = np.concatenate([np.arange(WIDTH_A, WIDTH_A + 2 * WIDTH_KV_A),
                              np.arange(WIDTH_A + 2 * WIDTH_KV_A + WIDTH_B, w_in0.shape[1])])
    w_ctx = w_in0[:, kv_cols].astype(bf16)
    g_pre = g_pre_mix[0].reshape(1, d)
    qa, ka, va, qb, kb, vb = _inproj(x.reshape(n_tok, d), mod3, g_pre, w_lat, _rope_tables(seq),
                                     seq=seq, mod_row0=0, latent=True)
    kac, vac, kbc, vbc = _inproj(ctx.reshape(batch * n_ctx, d), mod3, g_pre, w_ctx, None,
                                 seq=n_ctx, mod_row0=batch, latent=False)

    oa = _window_attention(qa, ka.reshape(batch, seq, -1), va.reshape(batch, seq, -1),
                           kac.reshape(batch, n_ctx, -1), vac.reshape(batch, n_ctx, -1),
                           sink_a[0].astype(jnp.float32) * LOG2E, batch=batch, seq=seq)
    tabs, cls_of_block = _nb_bias_tables(rpb_b[0], seq // GRID_W)
    ob = _neighbourhood_attention(qb, kb.reshape(batch, seq, -1), vb.reshape(batch, seq, -1),
                                  kbc.reshape(batch, n_ctx, -1), vbc.reshape(batch, n_ctx, -1),
                                  tabs, cls_of_block, batch=batch, seq=seq)

    w_out0 = w_out[0]
    w_o = jnp.concatenate([w_out0[:WIDTH_A][col_order], w_out0[WIDTH_A:]], axis=0).astype(bf16)
    x1, h2, idx, wts, rank, size, base, counts = _outproj(
        oa, ob, x.reshape(n_tok, d), mod3,
        g_grp_a[0][col_order].reshape(1, -1), g_grp_b[0].reshape(1, -1),
        g_post_mix[0].reshape(1, d), g_pre_ffn[0].reshape(1, d),
        w_o, w_router[0], b_router[0].reshape(1, -1), seq=seq)

    n_tok_tiles = n_tok // DISPATCH_TILE
    size = size.reshape(n_tok_tiles, N_EXPERTS)
    counts = counts.reshape(-1)
    padded = (counts + FFN_STEP - 1) // FFN_STEP * FFN_STEP
    pad_end = jnp.cumsum(padded).astype(jnp.int32)
    pad_start = pad_end - padded
    n_steps = n_tok * TOP_K // FFN_STEP + N_EXPERTS
    n_used = (pad_end[-1:] // FFN_STEP).astype(jnp.int32)
    step_row0 = jnp.arange(n_steps, dtype=jnp.int32) * FFN_STEP
    step_expert = jnp.minimum(jnp.sum(step_row0[:, None] >= pad_end[None, :], axis=1),
                              N_EXPERTS - 1).astype(jnp.int32)
    own = step_expert[:, None] == jnp.arange(N_EXPERTS, dtype=jnp.int32)
    step_rows = jnp.clip(jnp.sum(jnp.where(own, (pad_start + counts)[None, :], 0), axis=1) - step_row0,
                         0, FFN_STEP).astype(jnp.int32)
    fill_from = (pad_start + jnp.maximum((counts + FFN_TILE - 1) // FFN_TILE * FFN_TILE - FFN_TILE, 0)
                 ).astype(jnp.int32)
    local0 = jnp.cumsum(size, axis=1) - size
    global0 = pad_start[None, :] + base.reshape(n_tok_tiles, N_EXPERTS)
    tab = jnp.concatenate([local0, global0, size, jnp.sum(size, axis=1, keepdims=True),
                           jnp.zeros((n_tok_tiles, TAB_WIDTH - 3 * N_EXPERTS - 1), jnp.int32)], axis=1)
    tab = (tab * ROW_TILE).reshape(-1).astype(jnp.int32)
    hot = idx[:, :, None, :] == jnp.arange(N_EXPERTS, dtype=jnp.int32)[None, None, :, None]
    lpos_t = rank + jnp.sum(jnp.where(hot, local0[:, None, :, None], 0), axis=2)
    by_token = lambda a: jnp.swapaxes(a, 1, 2).reshape(n_tok, TOP_K)

    xs = _dispatch(h2, tab, lpos_t, fill_from * ROW_TILE, pad_end * ROW_TILE, n_rows=n_steps * FFN_STEP)
    ys = _ffn(xs, step_expert, step_rows, n_used, w_mlp1[0], b_mlp1[0], w_mlp2[0], b_mlp2[0])
    out = _combine(ys, tab, by_token(lpos_t), by_token(wts), x1, mod3, g_post_ffn[0].reshape(1, d), seq=seq)
    return out.reshape(batch, seq, d)
```

```python
import functools

import numpy as np
import jax
import jax.numpy as jnp
from jax import lax
from jax.experimental import pallas as pl
from jax.experimental.pallas import tpu as pltpu

GRID_W = 64
HEAD_DIM = 64
N_HEADS_A = 8
N_KV_A = 2
GQA_GROUP = N_HEADS_A // N_KV_A
N_HEADS_B = 8
WIDTH_A = N_HEADS_A * HEAD_DIM
WIDTH_KV_A = N_KV_A * HEAD_DIM
WIDTH_B = N_HEADS_B * HEAD_DIM
WINDOW = 128
BLOCK_A = 128
NA_ROWS = 8
NA_COLS = 16
N_EXPERTS = 32
TOP_K = 4
SWIGLU_LIMIT = 7.0
SWIGLU_ALPHA = 1.702
ROPE_BASE = 10000.0
EPS = 1e-6
NEG_INF = -1e30

LANES = 128
VMEM_LIMIT = 56 * 1024 * 1024

LOG2E = 1.4426950408889634
ATT_BLOCKS = 4
NB_Q_ROWS = 2
NB_K_ROWS = NB_Q_ROWS + NA_ROWS
PROJ_TILE = 1024
OUT_TILE = 256
FFN_TILE = 256
FFN_STEP = 4 * FFN_TILE
FFN_CHUNK = 256
FFN_GROUP = 4
DISPATCH_TILE = OUT_TILE
ROW_TILE = 8
TAB_WIDTH = 128


def _params(*sem):
    return pltpu.CompilerParams(dimension_semantics=sem, vmem_limit_bytes=VMEM_LIMIT)


def _rms(x, g):
    return x * lax.rsqrt(jnp.mean(x * x, axis=-1, keepdims=True) + EPS) * g


def _dot(a, b):
    return jnp.dot(a, b, preferred_element_type=jnp.float32)


def _dot_nt(a, b):
    return lax.dot_general(a, b, (((1,), (1,)), ((), ())), preferred_element_type=jnp.float32)


def _ada_kernel(c_ref, w_ref, b_ref, o_ref):
    c = c_ref[...]
    s = (c * jax.nn.sigmoid(c)).astype(jnp.bfloat16)
    o_ref[...] = _dot(s, w_ref[...].astype(jnp.bfloat16)) + b_ref[...]


def _ada(cc, w_ada, b_ada):
    rows, d = cc.shape
    n_out = w_ada.shape[1]
    return pl.pallas_call(
        _ada_kernel,
        grid=(n_out // d,),
        in_specs=[pl.BlockSpec((rows, d), lambda j: (0, 0)),
                  pl.BlockSpec((d, d), lambda j: (0, j)),
                  pl.BlockSpec((1, d), lambda j: (0, j))],
        out_specs=pl.BlockSpec((rows, d), lambda j: (0, j)),
        out_shape=jax.ShapeDtypeStruct((rows, n_out), jnp.float32),
        compiler_params=_params("arbitrary"),
        name="ada",
    )(cc, w_ada, b_ada.reshape(1, n_out))


def _rope(x, cos, sin):
    w = x.shape[1]
    reps = w // LANES
    if reps > 1:
        cos = jnp.concatenate([cos] * reps, axis=1)
        sin = jnp.concatenate([sin] * reps, axis=1)
    lane = lax.broadcasted_iota(jnp.int32, x.shape, 1)
    quarter = HEAD_DIM // 4
    partner = jnp.where(lane % (2 * quarter) < quarter,
                        pltpu.roll(x, w - quarter, 1), pltpu.roll(x, quarter, 1))
    return x * cos + partner * sin


def _inproj_kernel(x_ref, sh_ref, sc_ref, g_ref, w_ref, *rest, latent):
    x = x_ref[...]
    h = _rms(x, g_ref[...]) * (1.0 + sc_ref[0]) + sh_ref[0]
    p = _dot(h.astype(jnp.bfloat16), w_ref[...])
    if latent:
        cos_ref, sin_ref, qa_ref, ka_ref, va_ref, qb_ref, kb_ref, vb_ref = rest
        cos, sin = cos_ref[...], sin_ref[...]
        scale = HEAD_DIM ** -0.5 * LOG2E
        o = 0
        qa_ref[...] = (_rope(p[:, o:o + WIDTH_A], cos, sin) * scale).astype(qa_ref.dtype)
        o += WIDTH_A
        ka_ref[...] = _rope(p[:, o:o + WIDTH_KV_A], cos, sin).astype(ka_ref.dtype)
        o += WIDTH_KV_A
        va_ref[...] = p[:, o:o + WIDTH_KV_A].astype(va_ref.dtype)
        o += WIDTH_KV_A
        qb_ref[...] = (p[:, o:o + WIDTH_B] * scale).astype(qb_ref.dtype)
        o += WIDTH_B
    else:
        ka_ref, va_ref, kb_ref, vb_ref = rest
        o = 0
        ka_ref[...] = p[:, o:o + WIDTH_KV_A].astype(ka_ref.dtype)
        o += WIDTH_KV_A
        va_ref[...] = p[:, o:o + WIDTH_KV_A].astype(va_ref.dtype)
        o += WIDTH_KV_A
    kb_ref[...] = p[:, o:o + WIDTH_B].astype(kb_ref.dtype)
    o += WIDTH_B
    vb_ref[...] = p[:, o:o + WIDTH_B].astype(vb_ref.dtype)


def _inproj(x2d, mod3, g_pre, w, rope_tabs, *, seq, mod_row0, latent):
    t, d = x2d.shape
    tile = min(PROJ_TILE, seq)
    per_seq = seq // tile
    if latent:
        mod_row = lambda i: i // per_seq
    else:
        mod_row = lambda i: mod_row0
    in_specs = [pl.BlockSpec((tile, d), lambda i: (i, 0)),
                pl.BlockSpec((1, 1, d), lambda i: (mod_row(i), 0, 0)),
                pl.BlockSpec((1, 1, d), lambda i: (mod_row(i), 0, 1)),
                pl.BlockSpec((1, d), lambda i: (0, 0)),
                pl.BlockSpec(w.shape, lambda i: (0, 0))]
    args = [x2d, mod3, mod3, g_pre, w]
    widths = [WIDTH_KV_A, WIDTH_KV_A, WIDTH_B, WIDTH_B]
    if latent:
        in_specs += [pl.BlockSpec((tile, LANES), lambda i: (i % per_seq, 0))] * 2
        args += list(rope_tabs)
        widths = [WIDTH_A, WIDTH_KV_A, WIDTH_KV_A, WIDTH_B, WIDTH_B, WIDTH_B]
    return pl.pallas_call(
        functools.partial(_inproj_kernel, latent=latent),
        grid=(t // tile,),
        in_specs=in_specs,
        out_specs=[pl.BlockSpec((tile, wd), lambda i: (i, 0)) for wd in widths],
        out_shape=[jax.ShapeDtypeStruct((t, wd), jnp.bfloat16) for wd in widths],
        compiler_params=_params("parallel"),
        name="inproj_latent" if latent else "inproj_ctx",
    )(*args)


def _ones_beside(v):
    return jnp.concatenate([v, jnp.ones_like(v)], axis=1)


def _group_attention(q_ref, o_ref, rows, groups, s_ref, p_ref, m_ref):
    tq = rows.stop - rows.start
    low = lax.broadcasted_iota(jnp.int32, (tq, LANES), 1) < HEAD_DIM
    layout = []
    base = 0
    for pairs, k_loc, k_ctx, _, _, _, _ in groups:
        heads = [(j, 0) for j in pairs] + [(j, 1) for j in pairs]
        n_loc = k_loc.shape[0]
        stacked = []
        for j, half in heads:
            q = q_ref[rows, j * LANES:(j + 1) * LANES]
            stacked.append(jnp.where(low if half == 0 else ~low, q, jnp.zeros_like(q)))
        qs = jnp.concatenate(stacked, axis=0)
        n = len(heads) * tq
        s_ref[base:base + n, :n_loc] = _dot_nt(qs, k_loc)
        s_ref[base:base + n, n_loc:] = _dot_nt(qs, k_ctx)
        layout.append((base, heads, n_loc))
        base += n
    for (base, heads, n_loc), group in zip(layout, groups):
        bias_of, sink_of = group[5], group[6]
        for g, (j, half) in enumerate(heads):
            r = slice(base + g * tq, base + (g + 1) * tq)
            s_loc = s_ref[r, :n_loc] + bias_of(j, half)
            s_ctx = s_ref[r, n_loc:]
            m = jnp.maximum(jnp.max(s_loc, axis=1, keepdims=True), jnp.max(s_ctx, axis=1, keepdims=True))
            sink = sink_of(j, half)
            if sink is not None:
                m = jnp.maximum(m, sink)
            p_ref[r, :n_loc] = jnp.exp2((s_loc - m).astype(jnp.bfloat16))
            p_ref[r, n_loc:] = jnp.exp2((s_ctx - m).astype(jnp.bfloat16))
            m_ref[r, :] = m
    for (base, heads, n_loc), group in zip(layout, groups):
        pairs, _, _, v_loc, v_ctx, _, sink_of = group
        n = len(heads) * tq
        both = (_dot(p_ref[base:base + n, :n_loc], _ones_beside(v_loc))
                + _dot(p_ref[base:base + n, n_loc:], _ones_beside(v_ctx)))
        for a, j in enumerate(pairs):
            outs = []
            for half in range(2):
                g = half * len(pairs) + a
                r = slice(g * tq, (g + 1) * tq)
                den = both[r, LANES:]
                sink = sink_of(j, half)
                if sink is not None:
                    den = den + jnp.exp2(sink - m_ref[base + g * tq:base + (g + 1) * tq, :])
                outs.append(both[r, :LANES] / den)
            o_ref[rows, j * LANES:(j + 1) * LANES] = jnp.where(low, outs[0], outs[1]).astype(o_ref.dtype)


def _attention_scratch(tq, n_heads, n_keys):
    return [pltpu.VMEM((n_heads * tq, n_keys), jnp.float32), pltpu.VMEM((n_heads * tq, n_keys), jnp.bfloat16),
            pltpu.VMEM((n_heads * tq, 1), jnp.float32)]


def _win_kernel(sink_ref, q_ref, k_ref, v_ref, kc_ref, vc_ref, o_ref, *scratch, seq):
    span = BLOCK_A + 2 * WINDOW
    pairs = list(range(GQA_GROUP))
    for sb in range(ATT_BLOCKS):
        i = pl.program_id(1) * ATT_BLOCKS + sb
        start = pl.multiple_of(jnp.clip(i * BLOCK_A - WINDOW, 0, seq - span), BLOCK_A)
        qpos = i * BLOCK_A + lax.broadcasted_iota(jnp.int32, (BLOCK_A, span), 0)
        kpos = start + lax.broadcasted_iota(jnp.int32, (BLOCK_A, span), 1)
        bias = jnp.where(jnp.abs(kpos - qpos) <= WINDOW, 0.0, NEG_INF).astype(jnp.float32)
        groups = [([j], k_ref[0, pl.ds(start, span), :], kc_ref[0], v_ref[0, pl.ds(start, span), :], vc_ref[0],
                   lambda j, half: bias, lambda j, half: sink_ref[half * GQA_GROUP + j]) for j in pairs]
        _group_attention(q_ref, o_ref, slice(sb * BLOCK_A, (sb + 1) * BLOCK_A), groups, *scratch)


def _window_attention(qa, ka, va, kac, vac, sink, *, batch, seq):
    tq = ATT_BLOCKS * BLOCK_A
    nb = seq // tq
    n_ctx = kac.shape[1]
    return pl.pallas_call(
        functools.partial(_win_kernel, seq=seq),
        grid=(batch, nb),
        in_specs=[pl.BlockSpec(memory_space=pltpu.SMEM),
                  pl.BlockSpec((tq, WIDTH_A), lambda b, i: (b * nb + i, 0)),
                  pl.BlockSpec((1, seq, WIDTH_KV_A), lambda b, i: (b, 0, 0)),
                  pl.BlockSpec((1, seq, WIDTH_KV_A), lambda b, i: (b, 0, 0)),
                  pl.BlockSpec((1, n_ctx, WIDTH_KV_A), lambda b, i: (b, 0, 0)),
                  pl.BlockSpec((1, n_ctx, WIDTH_KV_A), lambda b, i: (b, 0, 0))],
        out_specs=pl.BlockSpec((tq, WIDTH_A), lambda b, i: (b * nb + i, 0)),
        out_shape=jax.ShapeDtypeStruct((batch * seq, WIDTH_A), jnp.bfloat16),
        scratch_shapes=_attention_scratch(BLOCK_A, N_HEADS_A, BLOCK_A + 2 * WINDOW + n_ctx),
        compiler_params=_params("parallel", "arbitrary"),
        name="window_attention",
    )(sink, qa, ka, va, kac, vac)


def _nb_kernel(q_ref, k_ref, v_ref, kc_ref, vc_ref, *rest, rows_n):
    tab_refs, o_ref, scratch = rest[:ATT_BLOCKS], rest[ATT_BLOCKS], rest[ATT_BLOCKS + 1:]
    n_keys = NB_K_ROWS * GRID_W
    tq = NB_Q_ROWS * GRID_W
    for sb in range(ATT_BLOCKS):
        m = pl.program_id(1) * ATT_BLOCKS + sb
        start_row = jnp.clip(NB_Q_ROWS * m - NA_ROWS // 2, 0, rows_n - NB_K_ROWS)
        start = pl.multiple_of(start_row * GRID_W, LANES)
        tab_ref = tab_refs[sb]
        groups = []
        for j in range(q_ref.shape[1] // LANES):
            cols = slice(j * LANES, (j + 1) * LANES)
            groups.append(([j], k_ref[0, pl.ds(start, n_keys), cols], kc_ref[0, :, cols],
                           v_ref[0, pl.ds(start, n_keys), cols], vc_ref[0, :, cols],
                           lambda j, half: tab_ref[0, 2 * j + half].astype(jnp.float32), lambda j, half: None))
        _group_attention(q_ref, o_ref, slice(sb * tq, (sb + 1) * tq), groups, *scratch)


def _nb_classes(rows_n):
    n_blocks = rows_n // NB_Q_ROWS
    sig = {}
    cls_of_block = []
    reps = []
    for m in range(n_blocks):
        start_row = int(np.clip(NB_Q_ROWS * m - NA_ROWS // 2, 0, rows_n - NB_K_ROWS))
        key = tuple((start_row - r, int(np.clip(r - NA_ROWS // 2, 0, rows_n - NA_ROWS)) - r)
                    for r in range(NB_Q_ROWS * m, NB_Q_ROWS * (m + 1)))
        if key not in sig:
            sig[key] = len(reps)
            reps.append(m)
        cls_of_block.append(sig[key])
    return np.asarray(cls_of_block, np.int32), reps


def _nb_bias_tables(rpb, rows_n):
    cls_of_block, reps = _nb_classes(rows_n)
    n_heads = rpb.shape[0]
    cq = np.arange(GRID_W)[:, None]
    ck = np.arange(GRID_W)[None, :]
    cs = np.clip(cq - NA_COLS // 2, 0, GRID_W - NA_COLS)
    col_ok = (ck >= cs) & (ck < cs + NA_COLS)
    pick = ((ck - cq + NA_COLS - 1)[None] == np.arange(2 * NA_COLS - 1)[:, None, None]) & col_ok[None]
    tiles = jnp.einsum('hrd,dqk->hrqk', rpb.astype(jnp.float32), jnp.asarray(pick, jnp.float32),
                       precision=lax.Precision.HIGHEST)
    tiles = jnp.where(jnp.asarray(col_ok)[None, None], tiles * LOG2E, NEG_INF).astype(jnp.bfloat16)
    blocked = jnp.full((n_heads, GRID_W, GRID_W), NEG_INF, jnp.bfloat16)
    tabs = []
    for m in reps:
        start_row = int(np.clip(NB_Q_ROWS * m - NA_ROWS // 2, 0, rows_n - NB_K_ROWS))
        q_rows = []
        for r in range(NB_Q_ROWS * m, NB_Q_ROWS * (m + 1)):
            rs = int(np.clip(r - NA_ROWS // 2, 0, rows_n - NA_ROWS))
            q_rows.append(jnp.concatenate(
                [tiles[:, krow - r + NA_ROWS - 1] if rs <= krow < rs + NA_ROWS else blocked
                 for krow in range(start_row, start_row + NB_K_ROWS)], axis=2))
        tabs.append(jnp.concatenate(q_rows, axis=1))
    return jnp.stack(tabs), cls_of_block


def _neighbourhood_attention(qb, kb, vb, kbc, vbc, tabs, cls_of_block, *, batch, seq):
    rows_n = seq // GRID_W
    tq = ATT_BLOCKS * NB_Q_ROWS * GRID_W
    nstep = seq // tq
    n_ctx = kbc.shape[1]

    def tab_spec(sb):
        return pl.BlockSpec((1,) + tabs.shape[1:], lambda b, m, c: (c[m * ATT_BLOCKS + sb], 0, 0, 0))

    grid_spec = pltpu.PrefetchScalarGridSpec(
        num_scalar_prefetch=1,
        grid=(batch, nstep),
        in_specs=[pl.BlockSpec((tq, WIDTH_B), lambda b, m, c: (b * nstep + m, 0)),
                  pl.BlockSpec((1, seq, WIDTH_B), lambda b, m, c: (b, 0, 0)),
                  pl.BlockSpec((1, seq, WIDTH_B), lambda b, m, c: (b, 0, 0)),
                  pl.BlockSpec((1, n_ctx, WIDTH_B), lambda b, m, c: (b, 0, 0)),
                  pl.BlockSpec((1, n_ctx, WIDTH_B), lambda b, m, c: (b, 0, 0))]
                 + [tab_spec(sb) for sb in range(ATT_BLOCKS)],
        out_specs=pl.BlockSpec((tq, WIDTH_B), lambda b, m, c: (b * nstep + m, 0)),
        scratch_shapes=_attention_scratch(NB_Q_ROWS * GRID_W, N_HEADS_B, NB_K_ROWS * GRID_W + n_ctx),
    )

    def body(c_ref, *refs):
        _nb_kernel(*refs, rows_n=rows_n)

    return pl.pallas_call(
        body,
        grid_spec=grid_spec,
        out_shape=jax.ShapeDtypeStruct((batch * seq, WIDTH_B), jnp.bfloat16),
        compiler_params=_params("parallel", "arbitrary"),
        name="neighbourhood_attention",
    )(jnp.asarray(cls_of_block), qb, kb, vb, kbc, vbc, *([tabs] * ATT_BLOCKS))


def _outproj_kernel(oa_ref, ob_ref, x_ref, ga_ref, sh_ref, sc_ref, gga_ref, ggb_ref, gpost_ref, gpre_ref,
                    wo_ref, wr_hi_ref, wr_lo_ref, br_ref,
                    x1_ref, h2_ref, idx_ref, wts_ref, rank_ref, size_ref, base_ref, cnt_ref, run_ref):
    i = pl.program_id(0)

    @pl.when(i == 0)
    def _():
        run_ref[...] = jnp.zeros_like(run_ref)

    na = _rms(oa_ref[...].astype(jnp.float32), gga_ref[...]).astype(jnp.bfloat16)
    nb = _rms(ob_ref[...].astype(jnp.float32), ggb_ref[...]).astype(jnp.bfloat16)
    mix = _dot(na, wo_ref[:WIDTH_A, :]) + _dot(nb, wo_ref[WIDTH_A:, :])
    x1 = x_ref[...] + ga_ref[0] * _rms(mix, gpost_ref[...])
    x1_ref[...] = x1
    h2 = _rms(x1, gpre_ref[...]) * (1.0 + sc_ref[0]) + sh_ref[0]
    h2_ref[...] = h2.astype(h2_ref.dtype)

    h_hi = h2.astype(jnp.bfloat16)
    h_lo = (h2 - h_hi.astype(jnp.float32)).astype(jnp.bfloat16)
    logits = _dot(h_hi, wr_hi_ref[...]) + _dot(h_lo, wr_hi_ref[...]) + _dot(h_hi, wr_lo_ref[...])
    work = logits.T[:N_EXPERTS] + br_ref[...]
    e, t = work.shape
    row = lax.broadcasted_iota(jnp.int32, (e, t), 0)
    chosen = jnp.zeros((e, t), jnp.float32)
    vals, sels, hots = [], [], []
    for _k in range(TOP_K):
        mx = jnp.max(work, axis=0, keepdims=True)
        sel = jnp.min(jnp.where(work == mx, row, e), axis=0, keepdims=True)
        hot = row == sel
        vals.append(mx)
        sels.append(sel)
        hots.append(hot)
        work = jnp.where(hot, -jnp.inf, work)
        chosen = chosen + hot.astype(jnp.float32)
    ex = [jnp.exp(v - vals[0]) for v in vals]
    den = ex[0] + ex[1] + ex[2] + ex[3]

    earlier = (lax.broadcasted_iota(jnp.int32, (t, t), 0) < lax.broadcasted_iota(jnp.int32, (t, t), 1))
    prefix = _dot(chosen.astype(jnp.bfloat16), earlier.astype(jnp.bfloat16))
    ranks = [jnp.sum(jnp.where(h, prefix, 0.0), axis=0, keepdims=True) for h in hots]
    size = jnp.sum(chosen, axis=1, keepdims=True).astype(jnp.int32)
    size_ref[0] = size
    base_ref[0] = run_ref[...]
    run_ref[...] = run_ref[...] + size
    cnt_ref[...] = run_ref[...]

    idx_ref[0] = jnp.concatenate(sels, axis=0)
    wts_ref[0] = jnp.concatenate([v / den for v in ex], axis=0)
    rank_ref[0] = jnp.concatenate(ranks, axis=0).astype(jnp.int32)


def _outproj(oa, ob, x2d, mod3, gga, ggb, gpost, gpre, w_out, w_router, b_router, *, seq):
    t, d = x2d.shape
    tile = OUT_TILE
    per_seq = seq // tile
    row = lambda i: (i, 0)
    const = lambda i: (0, 0)
    modspec = lambda part: pl.BlockSpec((1, 1, d), lambda i: (i // per_seq, 0, part))
    per_tile = lambda shape: pl.BlockSpec((1,) + shape, lambda i: (i, 0, 0))
    n_steps = t // tile
    w_pad = jnp.zeros((d, LANES), jnp.float32).at[:, :N_EXPERTS].set(w_router)
    wr_hi = w_pad.astype(jnp.bfloat16)
    wr_lo = (w_pad - wr_hi.astype(jnp.float32)).astype(jnp.bfloat16)
    outs = pl.pallas_call(
        _outproj_kernel,
        grid=(n_steps,),
        in_specs=[pl.BlockSpec((tile, WIDTH_A), row), pl.BlockSpec((tile, WIDTH_B), row),
                  pl.BlockSpec((tile, d), row),
                  modspec(2), modspec(3), modspec(4),
                  pl.BlockSpec((1, WIDTH_A), const), pl.BlockSpec((1, WIDTH_B), const),
                  pl.BlockSpec((1, d), const), pl.BlockSpec((1, d), const),
                  pl.BlockSpec(w_out.shape, const), pl.BlockSpec((d, LANES), const), pl.BlockSpec((d, LANES), const),
                  pl.BlockSpec((N_EXPERTS, 1), const)],
        out_specs=[pl.BlockSpec((tile, d), row), pl.BlockSpec((tile, d), row),
                   per_tile((TOP_K, tile)), per_tile((TOP_K, tile)), per_tile((TOP_K, tile)),
                   per_tile((N_EXPERTS, 1)), per_tile((N_EXPERTS, 1)),
                   pl.BlockSpec((N_EXPERTS, 1), const)],
        out_shape=[jax.ShapeDtypeStruct((t, d), jnp.float32), jax.ShapeDtypeStruct((t, d), jnp.bfloat16),
                   jax.ShapeDtypeStruct((n_steps, TOP_K, tile), jnp.int32),
                   jax.ShapeDtypeStruct((n_steps, TOP_K, tile), jnp.float32),
                   jax.ShapeDtypeStruct((n_steps, TOP_K, tile), jnp.int32),
                   jax.ShapeDtypeStruct((n_steps, N_EXPERTS, 1), jnp.int32),
                   jax.ShapeDtypeStruct((n_steps, N_EXPERTS, 1), jnp.int32),
                   jax.ShapeDtypeStruct((N_EXPERTS, 1), jnp.int32)],
        scratch_shapes=[pltpu.VMEM((N_EXPERTS, 1), jnp.int32)],
        compiler_params=_params("arbitrary"),
        name="outproj_router",
    )(oa, ob, x2d, mod3, mod3, mod3, gga, ggb, gpost, gpre, w_out, wr_hi, wr_lo, b_router.reshape(N_EXPERTS, 1))
    return outs


def _piece_table(tab_ref, e):
    return (pl.multiple_of(tab_ref[e], ROW_TILE), pl.multiple_of(tab_ref[N_EXPERTS + e], ROW_TILE),
            pl.multiple_of(tab_ref[2 * N_EXPERTS + e], ROW_TILE))


def _to_tiles(x, ref, index=()):
    for c in range(ROW_TILE):
        ref[index + (pl.ds(c, x.shape[0], stride=ROW_TILE), slice(None))] = x[:, c * LANES:(c + 1) * LANES]


def _from_tiles(ref, n_rows, index=()):
    return jnp.concatenate([ref[index + (pl.ds(c, n_rows, stride=ROW_TILE), slice(None))]
                            for c in range(ROW_TILE)], axis=1)


def _dispatch_kernel(fill_from_ref, pad_end_ref, tab_ref, h_ref, lpos_ref, xs_ref,
                     stage_ref, zero_ref, pend_ref, sems, zsem):
    i = pl.program_id(0)
    n_steps = pl.num_programs(0)
    slot = i % 2
    n_local = stage_ref.shape[1] // ROW_TILE
    fill_rows = zero_ref.shape[0]

    def fill(row):
        return pltpu.make_async_copy(zero_ref, xs_ref.at[pl.ds(pl.multiple_of(row, fill_rows), fill_rows)], zsem)

    def fills(e):
        return (pad_end_ref[e] - fill_from_ref[e]) // fill_rows

    def drain(s):
        n = pl.multiple_of(pend_ref[s], ROW_TILE)

        @pl.when(n > 0)
        def _():
            pltpu.make_async_copy(stage_ref.at[s, pl.ds(0, n)], xs_ref.at[pl.ds(0, n)], sems.at[s]).wait()
        pend_ref[s] = 0

    @pl.when(i == 0)
    def _():
        zero_ref[...] = jnp.zeros_like(zero_ref)
        pend_ref[0] = 0
        pend_ref[1] = 0

        def start_e(e, c):
            def start_j(j, c2):
                fill(fill_from_ref[e] + j * fill_rows).start()
                return c2
            lax.fori_loop(0, fills(e), start_j, 0)
            return c + fills(e)
        n_fill = lax.fori_loop(0, N_EXPERTS, start_e, 0)

        def wait_all(j, c):
            fill(0).wait()
            return c
        lax.fori_loop(0, n_fill, wait_all, 0)

    row = lax.broadcasted_iota(jnp.int32, (n_local, h_ref.shape[0]), 0)
    place = row == lpos_ref[0, 0:1, :]
    for k in range(1, TOP_K):
        place = place | (row == lpos_ref[0, k:k + 1, :])
    rows = _dot(place.astype(jnp.bfloat16), h_ref[...])

    drain(slot)
    _to_tiles(rows, stage_ref, (slot,))

    total = 0
    for e in range(N_EXPERTS):
        loc, glob, n = _piece_table(tab_ref, e)

        @pl.when(n > 0)
        def _(loc=loc, glob=glob, n=n, e=e):
            pltpu.make_async_copy(stage_ref.at[slot, pl.ds(loc, n)], xs_ref.at[pl.ds(glob, n)],
                                  sems.at[slot]).start(priority=e % 2)
        total = total + n
    pend_ref[slot] = total

    tail0 = pad_end_ref[N_EXPERTS - 1]
    n_tail = (xs_ref.shape[0] - tail0) // fill_rows
    per_step = (n_tail + n_steps - 1) // n_steps

    def start_tail(j, c):
        t = i * per_step + j

        @pl.when(t < n_tail)
        def _():
            fill(tail0 + t * fill_rows).start()
        return c
    lax.fori_loop(0, per_step, start_tail, 0)

    @pl.when(i == n_steps - 1)
    def _():
        drain(0)
        drain(1)

        def wait_tail(j, c):
            fill(0).wait()
            return c
        lax.fori_loop(0, n_tail, wait_tail, 0)


def _dispatch(h2, tab_flat, lpos_t, fill_from, pad_end, *, n_rows):
    t, d = h2.shape
    assert d == ROW_TILE * LANES
    tile = DISPATCH_TILE
    n_local = tile * TOP_K
    grid_spec = pltpu.PrefetchScalarGridSpec(
        num_scalar_prefetch=2,
        grid=(t // tile,),
        in_specs=[pl.BlockSpec((TAB_WIDTH,), lambda i, *_: (i,), memory_space=pltpu.SMEM),
                  pl.BlockSpec((tile, d), lambda i, *_: (i, 0)),
                  pl.BlockSpec((1, TOP_K, tile), lambda i, *_: (i, 0, 0))],
        out_specs=pl.BlockSpec(memory_space=pl.ANY),
        scratch_shapes=[pltpu.VMEM((2, n_local * ROW_TILE, LANES), jnp.float32),
                        pltpu.VMEM((FFN_TILE * ROW_TILE, LANES), jnp.float32),
                        pltpu.SMEM((2,), jnp.int32),
                        pltpu.SemaphoreType.DMA((2,)), pltpu.SemaphoreType.DMA(())],
    )
    return pl.pallas_call(
        _dispatch_kernel,
        grid_spec=grid_spec,
        out_shape=jax.ShapeDtypeStruct((n_rows * ROW_TILE, LANES), jnp.float32),
        compiler_params=_params("arbitrary"),
        name="dispatch",
    )(fill_from, pad_end, tab_flat, h2, lpos_t)


def _ffn_kernel(te_ref, rows_ref, n_used_ref, x_ref, w1_ref, b1_ref, w2_ref, b2_ref, y_ref,
                w1p_ref, w2b_ref, act_ref):
    i = pl.program_id(0)
    live = i < n_used_ref[0]
    n_pair = w1_ref.shape[2] // FFN_CHUNK
    half = FFN_CHUNK // 2
    d = w2_ref.shape[2]
    n_sub = jnp.where(live, (rows_ref[i] + FFN_TILE - 1) // FFN_TILE, 0)

    @pl.when(live & ((i == 0) | (te_ref[i] != te_ref[jnp.maximum(i - 1, 0)])))
    def _():
        src = lax.broadcasted_iota(jnp.int32, (FFN_CHUNK, FFN_CHUNK), 0)
        dst = lax.broadcasted_iota(jnp.int32, (FFN_CHUNK, FFN_CHUNK), 1)
        unzip = (src == jnp.where(dst < half, 2 * dst, 2 * (dst - half) + 1)).astype(jnp.bfloat16)
        for c in range(n_pair):
            cs = slice(c * FFN_CHUNK, (c + 1) * FFN_CHUNK)
            w1p_ref[:, cs] = _dot(w1_ref[0, :, cs].astype(jnp.bfloat16), unzip).astype(jnp.bfloat16)
        w2b_ref[...] = w2_ref[0].astype(jnp.bfloat16)

    def compute(n_rows):
        x = _from_tiles(x_ref, n_rows).astype(jnp.bfloat16)
        for c in range(0, n_pair, FFN_GROUP):
            cs = slice(c * FFN_CHUNK, (c + FFN_GROUP) * FFN_CHUNK)
            h = _dot(x, w1p_ref[:, cs]) + b1_ref[0, :, cs]
            g = jnp.concatenate([h[:, k * FFN_CHUNK:k * FFN_CHUNK + half] for k in range(FFN_GROUP)], axis=1)
            u = jnp.concatenate([h[:, k * FFN_CHUNK + half:(k + 1) * FFN_CHUNK] for k in range(FFN_GROUP)], axis=1)
            g = jnp.minimum(g, SWIGLU_LIMIT)
            u = jnp.clip(u, -SWIGLU_LIMIT, SWIGLU_LIMIT)
            act = g * jax.nn.sigmoid(SWIGLU_ALPHA * g) * (u + 1.0)
            act_ref[:n_rows, c * half:(c + FFN_GROUP) * half] = act.astype(act_ref.dtype)
        _to_tiles(_dot(act_ref[:n_rows, :], w2b_ref[...]) + b2_ref[0], y_ref)
        if n_rows * ROW_TILE < y_ref.shape[0]:
            y_ref[n_rows * ROW_TILE:, :] = jnp.zeros((y_ref.shape[0] - n_rows * ROW_TILE, LANES), y_ref.dtype)

    for tiles in range(1, FFN_STEP // FFN_TILE + 1):
        pl.when(n_sub == tiles)(functools.partial(compute, tiles * FFN_TILE))

    @pl.when(n_sub == 0)
    def _():
        y_ref[...] = jnp.zeros_like(y_ref)


def _unzip_bias(b1):
    e, f2 = b1.shape
    half = FFN_CHUNK // 2
    return b1.reshape(e, f2 // FFN_CHUNK, half, 2).transpose(0, 1, 3, 2).reshape(e, 1, f2)


def _ffn(xs, step_expert, step_rows, n_used, w1, b1, w2, b2):
    d, f2 = w1.shape[1:]
    n_steps = xs.shape[0] // (FFN_STEP * ROW_TILE)
    live = lambda i, nu: jnp.minimum(i, nu[0] - 1)
    wspec = lambda shape: pl.BlockSpec((1,) + shape, lambda i, te, nr, nu: (te[live(i, nu)], 0, 0))
    grid_spec = pltpu.PrefetchScalarGridSpec(
        num_scalar_prefetch=3,
        grid=(n_steps,),
        in_specs=[pl.BlockSpec((FFN_STEP * ROW_TILE, LANES), lambda i, te, nr, nu: (live(i, nu), 0)),
                  wspec((d, f2)), wspec((1, f2)), wspec((f2 // 2, d)), wspec((1, d))],
        out_specs=pl.BlockSpec((FFN_STEP * ROW_TILE, LANES), lambda i, te, nr, nu: (i, 0)),
        scratch_shapes=[pltpu.VMEM((d, f2), jnp.bfloat16), pltpu.VMEM((f2 // 2, d), jnp.bfloat16),
                        pltpu.VMEM((FFN_STEP, f2 // 2), jnp.bfloat16)],
    )
    return pl.pallas_call(
        _ffn_kernel,
        grid_spec=grid_spec,
        out_shape=jax.ShapeDtypeStruct(xs.shape, jnp.float32),
        compiler_params=_params("arbitrary"),
        name="expert_ffn",
    )(step_expert, step_rows, n_used, xs, w1, _unzip_bias(b1), w2, b2[:, None, :])


def _combine_kernel(tab_ref, next_tab_ref, lpos_ref, wts_ref, x1_ref, ga_ref, g_ref, y_ref, o_ref, buf_ref, sems):
    i = pl.program_id(0)
    n_steps = pl.num_programs(0)
    slot = i % 2
    tile = x1_ref.shape[0]
    n_local = buf_ref.shape[1] // ROW_TILE

    def fetch(tab, s):
        for e in range(N_EXPERTS):
            loc, glob, n = _piece_table(tab, e)

            @pl.when(n > 0)
            def _(loc=loc, glob=glob, n=n, e=e):
                pltpu.make_async_copy(y_ref.at[pl.ds(glob, n)], buf_ref.at[s, pl.ds(loc, n)],
                                      sems.at[s]).start(priority=e % 2)

    @pl.when(i == 0)
    def _():
        buf_ref[...] = jnp.zeros_like(buf_ref)
        fetch(tab_ref, 0)

    @pl.when(i + 1 < n_steps)
    def _():
        fetch(next_tab_ref, 1 - slot)

    n = pl.multiple_of(tab_ref[3 * N_EXPERTS], ROW_TILE)

    @pl.when(n > 0)
    def _():
        pltpu.make_async_copy(y_ref.at[pl.ds(0, n)], buf_ref.at[slot, pl.ds(0, n)], sems.at[slot]).wait()

    col = lax.broadcasted_iota(jnp.int32, (tile, n_local), 1)
    pos = lpos_ref[...]
    w = wts_ref[...]
    pick = jnp.where(col == pos[:, 0:1], w[:, 0:1], 0.0)
    for k in range(1, TOP_K):
        pick = pick + jnp.where(col == pos[:, k:k + 1], w[:, k:k + 1], 0.0)
    hi = pick.astype(jnp.bfloat16)
    lo = (pick - hi.astype(jnp.float32)).astype(jnp.bfloat16)
    rows = _from_tiles(buf_ref, n_local, (slot,)).astype(jnp.bfloat16)
    y = _dot(hi, rows) + _dot(lo, rows)
    o_ref[...] = x1_ref[...] + ga_ref[0] * _rms(y, g_ref[...])


def _combine(ys, tab_flat, lpos, wts, x1, mod3, g_post, *, seq):
    t, d = x1.shape
    tile = DISPATCH_TILE
    n_steps = t // tile
    per_seq = seq // tile
    n_local = tile * TOP_K
    return pl.pallas_call(
        _combine_kernel,
        grid=(n_steps,),
        in_specs=[pl.BlockSpec((TAB_WIDTH,), lambda i: (i,), memory_space=pltpu.SMEM),
                  pl.BlockSpec((TAB_WIDTH,), lambda i: (jnp.minimum(i + 1, n_steps - 1),), memory_space=pltpu.SMEM),
                  pl.BlockSpec((tile, TOP_K), lambda i: (i, 0)),
                  pl.BlockSpec((tile, TOP_K), lambda i: (i, 0)),
                  pl.BlockSpec((tile, d), lambda i: (i, 0)),
                  pl.BlockSpec((1, 1, d), lambda i: (i // per_seq, 0, 5)),
                  pl.BlockSpec((1, d), lambda i: (0, 0)),
                  pl.BlockSpec(memory_space=pl.ANY)],
        out_specs=pl.BlockSpec((tile, d), lambda i: (i, 0)),
        out_shape=jax.ShapeDtypeStruct((t, d), jnp.float32),
        scratch_shapes=[pltpu.VMEM((2, n_local * ROW_TILE, LANES), ys.dtype), pltpu.SemaphoreType.DMA((2,))],
        compiler_params=_params("arbitrary"),
        name="combine",
    )(tab_flat, tab_flat, lpos, wts, x1, mod3, g_post, ys)


def _rope_tables(seq):
    pos = np.arange(seq)
    n_freq = HEAD_DIM // 4
    freqs = ROPE_BASE ** (-jnp.arange(n_freq, dtype=jnp.float32) / n_freq)
    rows = jnp.asarray(pos // GRID_W, jnp.float32)[:, None] * freqs[None, :]
    cols = jnp.asarray(pos % GRID_W, jnp.float32)[:, None] * freqs[None, :]
    ang = jnp.concatenate([rows, rows, cols, cols], axis=1)
    sign = np.tile(np.repeat([-1.0, 1.0], n_freq), 2).astype(np.float32)
    cos = jnp.cos(ang)
    sin = jnp.sin(ang) * sign[None, :]
    reps = LANES // HEAD_DIM
    return jnp.tile(cos, (1, reps)), jnp.tile(sin, (1, reps))


def kernel(x, c, ctx, c_ctx, w_ada, b_ada, g_pre_mix, g_post_mix, g_pre_ffn, g_post_ffn, w_in, g_grp_a, g_grp_b,
           sink_a, rpb_b, w_out, w_router, b_router, w_mlp1, b_mlp1, w_mlp2, b_mlp2):
    batch, seq, d = x.shape
    n_ctx = ctx.shape[1]
    assert w_ada.shape[0] == 1, "single layer"
    assert seq % (ATT_BLOCKS * GRID_W * NB_Q_ROWS) == 0 and seq // GRID_W >= NB_K_ROWS
    assert seq % (ATT_BLOCKS * BLOCK_A) == 0 and seq >= BLOCK_A + 2 * WINDOW
    n_tok = batch * seq
    bf16 = jnp.bfloat16

    head_order = np.asarray([k * GQA_GROUP + j for j in range(GQA_GROUP) for k in range(N_KV_A)])
    col_order = (head_order[:, None] * HEAD_DIM + np.arange(HEAD_DIM)[None, :]).reshape(-1)

    mod_rows = -(-(batch + 1) // 8) * 8
    cc = jnp.zeros((mod_rows, d), jnp.float32).at[:batch].set(c).at[batch].set(c_ctx)
    mod3 = _ada(cc, w_ada[0], b_ada[0]).reshape(mod_rows, 1, 6 * d)

    w_in0 = w_in[0]
    w_lat = jnp.concatenate([w_in0[:, :WIDTH_A][:, col_order], w_in0[:, WIDTH_A:]], axis=1).astype(bf16)
    kv_cols = np.concatenate([np.arange(WIDTH_A, WIDTH_A + 2 * WIDTH_KV_A),
                              np.arange(WIDTH_A + 2 * WIDTH_KV_A + WIDTH_B, w_in0.shape[1])])
    w_ctx = w_in0[:, kv_cols].astype(bf16)
    g_pre = g_pre_mix[0].reshape(1, d)
    qa, ka, va, qb, kb, vb = _inproj(x.reshape(n_tok, d), mod3, g_pre, w_lat, _rope_tables(seq),
                                     seq=seq, mod_row0=0, latent=True)
    kac, vac, kbc, vbc = _inproj(ctx.reshape(batch * n_ctx, d), mod3, g_pre, w_ctx, None,
                                 seq=n_ctx, mod_row0=batch, latent=False)

    oa = _window_attention(qa, ka.reshape(batch, seq, -1), va.reshape(batch, seq, -1),
                           kac.reshape(batch, n_ctx, -1), vac.reshape(batch, n_ctx, -1),
                           sink_a[0].astype(jnp.float32) * LOG2E, batch=batch, seq=seq)
    tabs, cls_of_block = _nb_bias_tables(rpb_b[0], seq // GRID_W)
    ob = _neighbourhood_attention(qb, kb.reshape(batch, seq, -1), vb.reshape(batch, seq, -1),
                                  kbc.reshape(batch, n_ctx, -1), vbc.reshape(batch, n_ctx, -1),
                                  tabs, cls_of_block, batch=batch, seq=seq)

    w_out0 = w_out[0]
    w_o = jnp.concatenate([w_out0[:WIDTH_A][col_order], w_out0[WIDTH_A:]], axis=0).astype(bf16)
    x1, h2, idx, wts, rank, size, base, counts = _outproj(
        oa, ob, x.reshape(n_tok, d), mod3,
        g_grp_a[0][col_order].reshape(1, -1), g_grp_b[0].reshape(1, -1),
        g_post_mix[0].reshape(1, d), g_pre_ffn[0].reshape(1, d),
        w_o, w_router[0], b_router[0].reshape(1, -1), seq=seq)

    n_tok_tiles = n_tok // DISPATCH_TILE
    size = size.reshape(n_tok_tiles, N_EXPERTS)
    counts = counts.reshape(-1)
    padded = (counts + FFN_STEP - 1) // FFN_STEP * FFN_STEP
    pad_end = jnp.cumsum(padded).astype(jnp.int32)
    pad_start = pad_end - padded
    n_steps = n_tok * TOP_K // FFN_STEP + N_EXPERTS
    n_used = (pad_end[-1:] // FFN_STEP).astype(jnp.int32)
    step_row0 = jnp.arange(n_steps, dtype=jnp.int32) * FFN_STEP
    step_expert = jnp.minimum(jnp.sum(step_row0[:, None] >= pad_end[None, :], axis=1),
                              N_EXPERTS - 1).astype(jnp.int32)
    own = step_expert[:, None] == jnp.arange(N_EXPERTS, dtype=jnp.int32)
    step_rows = jnp.clip(jnp.sum(jnp.where(own, (pad_start + counts)[None, :], 0), axis=1) - step_row0,
                         0, FFN_STEP).astype(jnp.int32)
    fill_from = (pad_start + jnp.maximum((counts + FFN_TILE - 1) // FFN_TILE * FFN_TILE - FFN_TILE, 0)
                 ).astype(jnp.int32)
    local0 = jnp.cumsum(size, axis=1) - size
    global0 = pad_start[None, :] + base.reshape(n_tok_tiles, N_EXPERTS)
    tab = jnp.concatenate([local0, global0, size, jnp.sum(size, axis=1, keepdims=True),
                           jnp.zeros((n_tok_tiles, TAB_WIDTH - 3 * N_EXPERTS - 1), jnp.int32)], axis=1)
    tab = (tab * ROW_TILE).reshape(-1).astype(jnp.int32)
    hot = idx[:, :, None, :] == jnp.arange(N_EXPERTS, dtype=jnp.int32)[None, None, :, None]
    lpos_t = rank + jnp.sum(jnp.where(hot, local0[:, None, :, None], 0), axis=2)
    by_token = lambda a: jnp.swapaxes(a, 1, 2).reshape(n_tok, TOP_K)

    xs = _dispatch(h2, tab, lpos_t, fill_from * ROW_TILE, pad_end * ROW_TILE, n_rows=n_steps * FFN_STEP)
    ys = _ffn(xs, step_expert, step_rows, n_used, w_mlp1[0], b_mlp1[0], w_mlp2[0], b_mlp2[0])
    out = _combine(ys, tab, by_token(lpos_t), by_token(wts), x1, mod3, g_post_ffn[0].reshape(1, d), seq=seq)
    return out.reshape(batch, seq, d)
```

```python
import functools

import numpy as np
import jax
import jax.numpy as jnp
from jax import lax
from jax.experimental import pallas as pl
from jax.experimental.pallas import tpu as pltpu

GRID_W = 64
HEAD_DIM = 64
N_HEADS_A = 8
N_KV_A = 2
GQA_GROUP = N_HEADS_A // N_KV_A
N_HEADS_B = 8
WIDTH_A = N_HEADS_A * HEAD_DIM
WIDTH_KV_A = N_KV_A * HEAD_DIM
WIDTH_B = N_HEADS_B * HEAD_DIM
WINDOW = 128
BLOCK_A = 128
NA_ROWS = 8
NA_COLS = 16
N_EXPERTS = 32
TOP_K = 4
SWIGLU_LIMIT = 7.0
SWIGLU_ALPHA = 1.702
ROPE_BASE = 10000.0
EPS = 1e-6
NEG_INF = -1e30

LANES = 128
VMEM_LIMIT = 56 * 1024 * 1024

LOG2E = 1.4426950408889634
ATT_BLOCKS = 4
NB_Q_ROWS = 2
NB_K_ROWS = NB_Q_ROWS + NA_ROWS
PROJ_TILE = 1024
OUT_TILE = 256
OUT_PARTS = 2
FFN_TILE = 256
FFN_STEP = 4 * FFN_TILE
FFN_CHUNK = 256
FFN_GROUP = 4
DISPATCH_TILE = OUT_TILE
ROW_TILE = 8
TAB_WIDTH = 128


def _params(*sem):
    return pltpu.CompilerParams(dimension_semantics=sem, vmem_limit_bytes=VMEM_LIMIT)


def _rms(x, g):
    return x * lax.rsqrt(jnp.mean(x * x, axis=-1, keepdims=True) + EPS) * g


def _dot(a, b):
    return jnp.dot(a, b, preferred_element_type=jnp.float32)


def _dot_nt(a, b):
    return lax.dot_general(a, b, (((1,), (1,)), ((), ())), preferred_element_type=jnp.float32)


def _ada_kernel(c_ref, w_ref, b_ref, o_ref):
    c = c_ref[...]
    s = (c * jax.nn.sigmoid(c)).astype(jnp.bfloat16)
    o_ref[...] = _dot(s, w_ref[...].astype(jnp.bfloat16)) + b_ref[...]


def _ada(cc, w_ada, b_ada):
    rows, d = cc.shape
    n_out = w_ada.shape[1]
    return pl.pallas_call(
        _ada_kernel,
        grid=(n_out // d,),
        in_specs=[pl.BlockSpec((rows, d), lambda j: (0, 0)),
                  pl.BlockSpec((d, d), lambda j: (0, j)),
                  pl.BlockSpec((1, d), lambda j: (0, j))],
        out_specs=pl.BlockSpec((rows, d), lambda j: (0, j)),
        out_shape=jax.ShapeDtypeStruct((rows, n_out), jnp.float32),
        compiler_params=_params("arbitrary"),
        name="ada",
    )(cc, w_ada, b_ada.reshape(1, n_out))


def _rope(x, cos, sin):
    w = x.shape[1]
    reps = w // LANES
    if reps > 1:
        cos = jnp.concatenate([cos] * reps, axis=1)
        sin = jnp.concatenate([sin] * reps, axis=1)
    lane = lax.broadcasted_iota(jnp.int32, x.shape, 1)
    quarter = HEAD_DIM // 4
    partner = jnp.where(lane % (2 * quarter) < quarter,
                        pltpu.roll(x, w - quarter, 1), pltpu.roll(x, quarter, 1))
    return x * cos + partner * sin


def _inproj_kernel(x_ref, sh_ref, sc_ref, g_ref, w_ref, *rest, latent):
    x = x_ref[...]
    h = _rms(x, g_ref[...]) * (1.0 + sc_ref[0]) + sh_ref[0]
    p = _dot(h.astype(jnp.bfloat16), w_ref[...])
    if latent:
        cos_ref, sin_ref, qa_ref, ka_ref, va_ref, qb_ref, kb_ref, vb_ref = rest
        cos, sin = cos_ref[...], sin_ref[...]
        scale = HEAD_DIM ** -0.5 * LOG2E
        o = 0
        qa_ref[...] = (_rope(p[:, o:o + WIDTH_A], cos, sin) * scale).astype(qa_ref.dtype)
        o += WIDTH_A
        ka_ref[...] = _rope(p[:, o:o + WIDTH_KV_A], cos, sin).astype(ka_ref.dtype)
        o += WIDTH_KV_A
        va_ref[...] = p[:, o:o + WIDTH_KV_A].astype(va_ref.dtype)
        o += WIDTH_KV_A
        qb_ref[...] = (p[:, o:o + WIDTH_B] * scale).astype(qb_ref.dtype)
        o += WIDTH_B
    else:
        ka_ref, va_ref, kb_ref, vb_ref = rest
        o = 0
        ka_ref[...] = p[:, o:o + WIDTH_KV_A].astype(ka_ref.dtype)
        o += WIDTH_KV_A
        va_ref[...] = p[:, o:o + WIDTH_KV_A].astype(va_ref.dtype)
        o += WIDTH_KV_A
    kb_ref[...] = p[:, o:o + WIDTH_B].astype(kb_ref.dtype)
    o += WIDTH_B
    vb_ref[...] = p[:, o:o + WIDTH_B].astype(vb_ref.dtype)


def _inproj(x2d, mod3, g_pre, w, rope_tabs, *, seq, mod_row0, latent):
    t, d = x2d.shape
    tile = min(PROJ_TILE, seq)
    per_seq = seq // tile
    if latent:
        mod_row = lambda i: i // per_seq
    else:
        mod_row = lambda i: mod_row0
    in_specs = [pl.BlockSpec((tile, d), lambda i: (i, 0)),
                pl.BlockSpec((1, 1, d), lambda i: (mod_row(i), 0, 0)),
                pl.BlockSpec((1, 1, d), lambda i: (mod_row(i), 0, 1)),
                pl.BlockSpec((1, d), lambda i: (0, 0)),
                pl.BlockSpec(w.shape, lambda i: (0, 0))]
    args = [x2d, mod3, mod3, g_pre, w]
    widths = [WIDTH_KV_A, WIDTH_KV_A, WIDTH_B, WIDTH_B]
    if latent:
        in_specs += [pl.BlockSpec((tile, LANES), lambda i: (i % per_seq, 0))] * 2
        args += list(rope_tabs)
        widths = [WIDTH_A, WIDTH_KV_A, WIDTH_KV_A, WIDTH_B, WIDTH_B, WIDTH_B]
    return pl.pallas_call(
        functools.partial(_inproj_kernel, latent=latent),
        grid=(t // tile,),
        in_specs=in_specs,
        out_specs=[pl.BlockSpec((tile, wd), lambda i: (i, 0)) for wd in widths],
        out_shape=[jax.ShapeDtypeStruct((t, wd), jnp.bfloat16) for wd in widths],
        compiler_params=_params("parallel"),
        name="inproj_latent" if latent else "inproj_ctx",
    )(*args)


def _ones_beside(v):
    return jnp.concatenate([v, jnp.ones_like(v)], axis=1)


def _group_attention(q_ref, o_ref, rows, groups, s_ref, p_ref, m_ref):
    tq = rows.stop - rows.start
    low = lax.broadcasted_iota(jnp.int32, (tq, LANES), 1) < HEAD_DIM
    layout = []
    base = 0
    for pairs, k_loc, k_ctx, _, _, _, _ in groups:
        heads = [(j, 0) for j in pairs] + [(j, 1) for j in pairs]
        n_loc = k_loc.shape[0]
        stacked = []
        for j, half in heads:
            q = q_ref[rows, j * LANES:(j + 1) * LANES]
            stacked.append(jnp.where(low if half == 0 else ~low, q, jnp.zeros_like(q)))
        qs = jnp.concatenate(stacked, axis=0)
        n = len(heads) * tq
        s_ref[base:base + n, :n_loc] = _dot_nt(qs, k_loc)
        s_ref[base:base + n, n_loc:] = _dot_nt(qs, k_ctx)
        layout.append((base, heads, n_loc))
        base += n
    for (base, heads, n_loc), group in zip(layout, groups):
        bias_of, sink_of = group[5], group[6]
        for g, (j, half) in enumerate(heads):
            r = slice(base + g * tq, base + (g + 1) * tq)
            s_loc = s_ref[r, :n_loc] + bias_of(j, half)
            s_ctx = s_ref[r, n_loc:]
            m = jnp.maximum(jnp.max(s_loc, axis=1, keepdims=True), jnp.max(s_ctx, axis=1, keepdims=True))
            sink = sink_of(j, half)
            if sink is not None:
                m = jnp.maximum(m, sink)
            p_ref[r, :n_loc] = jnp.exp2((s_loc - m).astype(jnp.bfloat16))
            p_ref[r, n_loc:] = jnp.exp2((s_ctx - m).astype(jnp.bfloat16))
            m_ref[r, :] = m
    for (base, heads, n_loc), group in zip(layout, groups):
        pairs, _, _, v_loc, v_ctx, _, sink_of = group
        n = len(heads) * tq
        both = (_dot(p_ref[base:base + n, :n_loc], _ones_beside(v_loc))
                + _dot(p_ref[base:base + n, n_loc:], _ones_beside(v_ctx)))
        for a, j in enumerate(pairs):
            outs = []
            for half in range(2):
                g = half * len(pairs) + a
                r = slice(g * tq, (g + 1) * tq)
                den = both[r, LANES:]
                sink = sink_of(j, half)
                if sink is not None:
                    den = den + jnp.exp2(sink - m_ref[base + g * tq:base + (g + 1) * tq, :])
                outs.append(both[r, :LANES] / den)
            o_ref[rows, j * LANES:(j + 1) * LANES] = jnp.where(low, outs[0], outs[1]).astype(o_ref.dtype)


def _attention_scratch(tq, n_heads, n_keys):
    return [pltpu.VMEM((n_heads * tq, n_keys), jnp.float32), pltpu.VMEM((n_heads * tq, n_keys), jnp.bfloat16),
            pltpu.VMEM((n_heads * tq, 1), jnp.float32)]


def _win_kernel(sink_ref, q_ref, k_ref, v_ref, kc_ref, vc_ref, o_ref, *scratch, seq):
    span = BLOCK_A + 2 * WINDOW
    pairs = list(range(GQA_GROUP))
    for sb in range(ATT_BLOCKS):
        i = pl.program_id(1) * ATT_BLOCKS + sb
        start = pl.multiple_of(jnp.clip(i * BLOCK_A - WINDOW, 0, seq - span), BLOCK_A)
        qpos = i * BLOCK_A + lax.broadcasted_iota(jnp.int32, (BLOCK_A, span), 0)
        kpos = start + lax.broadcasted_iota(jnp.int32, (BLOCK_A, span), 1)
        bias = jnp.where(jnp.abs(kpos - qpos) <= WINDOW, 0.0, NEG_INF).astype(jnp.float32)
        groups = [([j], k_ref[0, pl.ds(start, span), :], kc_ref[0], v_ref[0, pl.ds(start, span), :], vc_ref[0],
                   lambda j, half: bias, lambda j, half: sink_ref[half * GQA_GROUP + j]) for j in pairs]
        _group_attention(q_ref, o_ref, slice(sb * BLOCK_A, (sb + 1) * BLOCK_A), groups, *scratch)


def _window_attention(qa, ka, va, kac, vac, sink, *, batch, seq):
    tq = ATT_BLOCKS * BLOCK_A
    nb = seq // tq
    n_ctx = kac.shape[1]
    return pl.pallas_call(
        functools.partial(_win_kernel, seq=seq),
        grid=(batch, nb),
        in_specs=[pl.BlockSpec(memory_space=pltpu.SMEM),
                  pl.BlockSpec((tq, WIDTH_A), lambda b, i: (b * nb + i, 0)),
                  pl.BlockSpec((1, seq, WIDTH_KV_A), lambda b, i: (b, 0, 0)),
                  pl.BlockSpec((1, seq, WIDTH_KV_A), lambda b, i: (b, 0, 0)),
                  pl.BlockSpec((1, n_ctx, WIDTH_KV_A), lambda b, i: (b, 0, 0)),
                  pl.BlockSpec((1, n_ctx, WIDTH_KV_A), lambda b, i: (b, 0, 0))],
        out_specs=pl.BlockSpec((tq, WIDTH_A), lambda b, i: (b * nb + i, 0)),
        out_shape=jax.ShapeDtypeStruct((batch * seq, WIDTH_A), jnp.bfloat16),
        scratch_shapes=_attention_scratch(BLOCK_A, N_HEADS_A, BLOCK_A + 2 * WINDOW + n_ctx),
        compiler_params=_params("parallel", "arbitrary"),
        name="window_attention",
    )(sink, qa, ka, va, kac, vac)


def _nb_kernel(q_ref, k_ref, v_ref, kc_ref, vc_ref, *rest, rows_n):
    tab_refs, o_ref, scratch = rest[:ATT_BLOCKS], rest[ATT_BLOCKS], rest[ATT_BLOCKS + 1:]
    n_keys = NB_K_ROWS * GRID_W
    tq = NB_Q_ROWS * GRID_W
    for sb in range(ATT_BLOCKS):
        m = pl.program_id(1) * ATT_BLOCKS + sb
        start_row = jnp.clip(NB_Q_ROWS * m - NA_ROWS // 2, 0, rows_n - NB_K_ROWS)
        start = pl.multiple_of(start_row * GRID_W, LANES)
        tab_ref = tab_refs[sb]
        groups = []
        for j in range(q_ref.shape[1] // LANES):
            cols = slice(j * LANES, (j + 1) * LANES)
            groups.append(([j], k_ref[0, pl.ds(start, n_keys), cols], kc_ref[0, :, cols],
                           v_ref[0, pl.ds(start, n_keys), cols], vc_ref[0, :, cols],
                           lambda j, half: tab_ref[0, 2 * j + half].astype(jnp.float32), lambda j, half: None))
        _group_attention(q_ref, o_ref, slice(sb * tq, (sb + 1) * tq), groups, *scratch)


def _nb_classes(rows_n):
    n_blocks = rows_n // NB_Q_ROWS
    sig = {}
    cls_of_block = []
    reps = []
    for m in range(n_blocks):
        start_row = int(np.clip(NB_Q_ROWS * m - NA_ROWS // 2, 0, rows_n - NB_K_ROWS))
        key = tuple((start_row - r, int(np.clip(r - NA_ROWS // 2, 0, rows_n - NA_ROWS)) - r)
                    for r in range(NB_Q_ROWS * m, NB_Q_ROWS * (m + 1)))
        if key not in sig:
            sig[key] = len(reps)
            reps.append(m)
        cls_of_block.append(sig[key])
    return np.asarray(cls_of_block, np.int32), reps


def _nb_bias_tables(rpb, rows_n):
    cls_of_block, reps = _nb_classes(rows_n)
    n_heads = rpb.shape[0]
    cq = np.arange(GRID_W)[:, None]
    ck = np.arange(GRID_W)[None, :]
    cs = np.clip(cq - NA_COLS // 2, 0, GRID_W - NA_COLS)
    col_ok = (ck >= cs) & (ck < cs + NA_COLS)
    pick = ((ck - cq + NA_COLS - 1)[None] == np.arange(2 * NA_COLS - 1)[:, None, None]) & col_ok[None]
    tiles = jnp.einsum('hrd,dqk->hrqk', rpb.astype(jnp.float32), jnp.asarray(pick, jnp.float32),
                       precision=lax.Precision.HIGHEST)
    tiles = jnp.where(jnp.asarray(col_ok)[None, None], tiles * LOG2E, NEG_INF).astype(jnp.bfloat16)
    blocked = jnp.full((n_heads, GRID_W, GRID_W), NEG_INF, jnp.bfloat16)
    tabs = []
    for m in reps:
        start_row = int(np.clip(NB_Q_ROWS * m - NA_ROWS // 2, 0, rows_n - NB_K_ROWS))
        q_rows = []
        for r in range(NB_Q_ROWS * m, NB_Q_ROWS * (m + 1)):
            rs = int(np.clip(r - NA_ROWS // 2, 0, rows_n - NA_ROWS))
            q_rows.append(jnp.concatenate(
                [tiles[:, krow - r + NA_ROWS - 1] if rs <= krow < rs + NA_ROWS else blocked
                 for krow in range(start_row, start_row + NB_K_ROWS)], axis=2))
        tabs.append(jnp.concatenate(q_rows, axis=1))
    return jnp.stack(tabs), cls_of_block


def _neighbourhood_attention(qb, kb, vb, kbc, vbc, tabs, cls_of_block, *, batch, seq):
    rows_n = seq // GRID_W
    tq = ATT_BLOCKS * NB_Q_ROWS * GRID_W
    nstep = seq // tq
    n_ctx = kbc.shape[1]

    def tab_spec(sb):
        return pl.BlockSpec((1,) + tabs.shape[1:], lambda b, m, c: (c[m * ATT_BLOCKS + sb], 0, 0, 0))

    grid_spec = pltpu.PrefetchScalarGridSpec(
        num_scalar_prefetch=1,
        grid=(batch, nstep),
        in_specs=[pl.BlockSpec((tq, WIDTH_B), lambda b, m, c: (b * nstep + m, 0)),
                  pl.BlockSpec((1, seq, WIDTH_B), lambda b, m, c: (b, 0, 0)),
                  pl.BlockSpec((1, seq, WIDTH_B), lambda b, m, c: (b, 0, 0)),
                  pl.BlockSpec((1, n_ctx, WIDTH_B), lambda b, m, c: (b, 0, 0)),
                  pl.BlockSpec((1, n_ctx, WIDTH_B), lambda b, m, c: (b, 0, 0))]
                 + [tab_spec(sb) for sb in range(ATT_BLOCKS)],
        out_specs=pl.BlockSpec((tq, WIDTH_B), lambda b, m, c: (b * nstep + m, 0)),
        scratch_shapes=_attention_scratch(NB_Q_ROWS * GRID_W, N_HEADS_B, NB_K_ROWS * GRID_W + n_ctx),
    )

    def body(c_ref, *refs):
        _nb_kernel(*refs, rows_n=rows_n)

    return pl.pallas_call(
        body,
        grid_spec=grid_spec,
        out_shape=jax.ShapeDtypeStruct((batch * seq, WIDTH_B), jnp.bfloat16),
        compiler_params=_params("parallel", "arbitrary"),
        name="neighbourhood_attention",
    )(jnp.asarray(cls_of_block), qb, kb, vb, kbc, vbc, *([tabs] * ATT_BLOCKS))


def _outproj_kernel(oa_ref, ob_ref, x_ref, ga_ref, sh_ref, sc_ref, gga_ref, ggb_ref, gpost_ref, gpre_ref,
                    wo_ref, wr_hi_ref, wr_lo_ref, br_ref,
                    x1_ref, h2_ref, idx_ref, wts_ref, rank_ref, size_ref, base_ref, cnt_ref, run_ref):
    i = pl.program_id(0)

    @pl.when(i == 0)
    def _():
        run_ref[...] = jnp.zeros_like(run_ref)

    n_rows = x_ref.shape[0]
    parts = [slice(p * n_rows // OUT_PARTS, (p + 1) * n_rows // OUT_PARTS) for p in range(OUT_PARTS)]
    na = [_rms(oa_ref[r, :].astype(jnp.float32), gga_ref[...]).astype(jnp.bfloat16) for r in parts]
    nb = [_rms(ob_ref[r, :].astype(jnp.float32), ggb_ref[...]).astype(jnp.bfloat16) for r in parts]
    mix = [_dot(a, wo_ref[:WIDTH_A, :]) + _dot(b, wo_ref[WIDTH_A:, :]) for a, b in zip(na, nb)]
    x1 = [x_ref[r, :] + ga_ref[0] * _rms(m, gpost_ref[...]) for r, m in zip(parts, mix)]
    for r, v in zip(parts, x1):
        x1_ref[r, :] = v
    h2 = [_rms(v, gpre_ref[...]) * (1.0 + sc_ref[0]) + sh_ref[0] for v in x1]
    for r, v in zip(parts, h2):
        h2_ref[r, :] = v.astype(h2_ref.dtype)

    h_hi = [v.astype(jnp.bfloat16) for v in h2]
    h_lo = [(v - hi.astype(jnp.float32)).astype(jnp.bfloat16) for v, hi in zip(h2, h_hi)]
    logits = jnp.concatenate([_dot(hi, wr_hi_ref[...]) + _dot(lo, wr_hi_ref[...]) + _dot(hi, wr_lo_ref[...])
                              for hi, lo in zip(h_hi, h_lo)], axis=0)
    work = logits.T[:N_EXPERTS] + br_ref[...]
    e, t = work.shape
    row = lax.broadcasted_iota(jnp.int32, (e, t), 0)
    chosen = jnp.zeros((e, t), jnp.float32)
    vals, sels, hots = [], [], []
    for _k in range(TOP_K):
        mx = jnp.max(work, axis=0, keepdims=True)
        sel = jnp.min(jnp.where(work == mx, row, e), axis=0, keepdims=True)
        hot = row == sel
        vals.append(mx)
        sels.append(sel)
        hots.append(hot)
        work = jnp.where(hot, -jnp.inf, work)
        chosen = chosen + hot.astype(jnp.float32)
    ex = [jnp.exp(v - vals[0]) for v in vals]
    den = ex[0] + ex[1] + ex[2] + ex[3]

    earlier = (lax.broadcasted_iota(jnp.int32, (t, t), 0) < lax.broadcasted_iota(jnp.int32, (t, t), 1))
    prefix = _dot(chosen.astype(jnp.bfloat16), earlier.astype(jnp.bfloat16))
    ranks = [jnp.sum(jnp.where(h, prefix, 0.0), axis=0, keepdims=True) for h in hots]
    size = jnp.sum(chosen, axis=1, keepdims=True).astype(jnp.int32)
    size_ref[0] = size
    base_ref[0] = run_ref[...]
    run_ref[...] = run_ref[...] + size
    cnt_ref[...] = run_ref[...]

    idx_ref[0] = jnp.concatenate(sels, axis=0)
    wts_ref[0] = jnp.concatenate([v / den for v in ex], axis=0)
    rank_ref[0] = jnp.concatenate(ranks, axis=0).astype(jnp.int32)


def _outproj(oa, ob, x2d, mod3, gga, ggb, gpost, gpre, w_out, w_router, b_router, *, seq):
    t, d = x2d.shape
    tile = OUT_TILE
    per_seq = seq // tile
    row = lambda i: (i, 0)
    const = lambda i: (0, 0)
    modspec = lambda part: pl.BlockSpec((1, 1, d), lambda i: (i // per_seq, 0, part))
    per_tile = lambda shape: pl.BlockSpec((1,) + shape, lambda i: (i, 0, 0))
    n_steps = t // tile
    w_pad = jnp.zeros((d, LANES), jnp.float32).at[:, :N_EXPERTS].set(w_router)
    wr_hi = w_pad.astype(jnp.bfloat16)
    wr_lo = (w_pad - wr_hi.astype(jnp.float32)).astype(jnp.bfloat16)
    outs = pl.pallas_call(
        _outproj_kernel,
        grid=(n_steps,),
        in_specs=[pl.BlockSpec((tile, WIDTH_A), row), pl.BlockSpec((tile, WIDTH_B), row),
                  pl.BlockSpec((tile, d), row),
                  modspec(2), modspec(3), modspec(4),
                  pl.BlockSpec((1, WIDTH_A), const), pl.BlockSpec((1, WIDTH_B), const),
                  pl.BlockSpec((1, d), const), pl.BlockSpec((1, d), const),
                  pl.BlockSpec(w_out.shape, const), pl.BlockSpec((d, LANES), const), pl.BlockSpec((d, LANES), const),
                  pl.BlockSpec((N_EXPERTS, 1), const)],
        out_specs=[pl.BlockSpec((tile, d), row), pl.BlockSpec((tile, d), row),
                   per_tile((TOP_K, tile)), per_tile((TOP_K, tile)), per_tile((TOP_K, tile)),
                   per_tile((N_EXPERTS, 1)), per_tile((N_EXPERTS, 1)),
                   pl.BlockSpec((N_EXPERTS, 1), const)],
        out_shape=[jax.ShapeDtypeStruct((t, d), jnp.float32), jax.ShapeDtypeStruct((t, d), jnp.bfloat16),
                   jax.ShapeDtypeStruct((n_steps, TOP_K, tile), jnp.int32),
                   jax.ShapeDtypeStruct((n_steps, TOP_K, tile), jnp.float32),
                   jax.ShapeDtypeStruct((n_steps, TOP_K, tile), jnp.int32),
                   jax.ShapeDtypeStruct((n_steps, N_EXPERTS, 1), jnp.int32),
                   jax.ShapeDtypeStruct((n_steps, N_EXPERTS, 1), jnp.int32),
                   jax.ShapeDtypeStruct((N_EXPERTS, 1), jnp.int32)],
        scratch_shapes=[pltpu.VMEM((N_EXPERTS, 1), jnp.int32)],
        compiler_params=_params("arbitrary"),
        name="outproj_router",
    )(oa, ob, x2d, mod3, mod3, mod3, gga, ggb, gpost, gpre, w_out, wr_hi, wr_lo, b_router.reshape(N_EXPERTS, 1))
    return outs


def _piece_table(tab_ref, e):
    return (pl.multiple_of(tab_ref[e], ROW_TILE), pl.multiple_of(tab_ref[N_EXPERTS + e], ROW_TILE),
            pl.multiple_of(tab_ref[2 * N_EXPERTS + e], ROW_TILE))


def _to_tiles(x, ref, index=()):
    for c in range(ROW_TILE):
        ref[index + (pl.ds(c, x.shape[0], stride=ROW_TILE), slice(None))] = x[:, c * LANES:(c + 1) * LANES]


def _from_tiles(ref, n_rows, index=()):
    return jnp.concatenate([ref[index + (pl.ds(c, n_rows, stride=ROW_TILE), slice(None))]
                            for c in range(ROW_TILE)], axis=1)


def _dispatch_kernel(fill_from_ref, pad_end_ref, tab_ref, h_ref, lpos_ref, xs_ref,
                     stage_ref, zero_ref, pend_ref, sems, zsem):
    i = pl.program_id(0)
    n_steps = pl.num_programs(0)
    slot = i % 2
    n_local = stage_ref.shape[1] // ROW_TILE
    fill_rows = zero_ref.shape[0]

    def fill(row):
        return pltpu.make_async_copy(zero_ref, xs_ref.at[pl.ds(pl.multiple_of(row, fill_rows), fill_rows)], zsem)

    def fills(e):
        return (pad_end_ref[e] - fill_from_ref[e]) // fill_rows

    def drain(s):
        n = pl.multiple_of(pend_ref[s], ROW_TILE)

        @pl.when(n > 0)
        def _():
            pltpu.make_async_copy(stage_ref.at[s, pl.ds(0, n)], xs_ref.at[pl.ds(0, n)], sems.at[s]).wait()
        pend_ref[s] = 0

    @pl.when(i == 0)
    def _():
        zero_ref[...] = jnp.zeros_like(zero_ref)
        pend_ref[0] = 0
        pend_ref[1] = 0

        def start_e(e, c):
            def start_j(j, c2):
                fill(fill_from_ref[e] + j * fill_rows).start()
                return c2
            lax.fori_loop(0, fills(e), start_j, 0)
            return c + fills(e)
        n_fill = lax.fori_loop(0, N_EXPERTS, start_e, 0)

        def wait_all(j, c):
            fill(0).wait()
            return c
        lax.fori_loop(0, n_fill, wait_all, 0)

    row = lax.broadcasted_iota(jnp.int32, (n_local, h_ref.shape[0]), 0)
    place = row == lpos_ref[0, 0:1, :]
    for k in range(1, TOP_K):
        place = place | (row == lpos_ref[0, k:k + 1, :])
    rows = _dot(place.astype(jnp.bfloat16), h_ref[...])

    drain(slot)
    _to_tiles(rows, stage_ref, (slot,))

    total = 0
    for e in range(N_EXPERTS):
        loc, glob, n = _piece_table(tab_ref, e)

        @pl.when(n > 0)
        def _(loc=loc, glob=glob, n=n, e=e):
            pltpu.make_async_copy(stage_ref.at[slot, pl.ds(loc, n)], xs_ref.at[pl.ds(glob, n)],
                                  sems.at[slot]).start(priority=e % 2)
        total = total + n
    pend_ref[slot] = total

    tail0 = pad_end_ref[N_EXPERTS - 1]
    n_tail = (xs_ref.shape[0] - tail0) // fill_rows
    per_step = (n_tail + n_steps - 1) // n_steps

    def start_tail(j, c):
        t = i * per_step + j

        @pl.when(t < n_tail)
        def _():
            fill(tail0 + t * fill_rows).start()
        return c
    lax.fori_loop(0, per_step, start_tail, 0)

    @pl.when(i == n_steps - 1)
    def _():
        drain(0)
        drain(1)

        def wait_tail(j, c):
            fill(0).wait()
            return c
        lax.fori_loop(0, n_tail, wait_tail, 0)


def _dispatch(h2, tab_flat, lpos_t, fill_from, pad_end, *, n_rows):
    t, d = h2.shape
    assert d == ROW_TILE * LANES
    tile = DISPATCH_TILE
    n_local = tile * TOP_K
    grid_spec = pltpu.PrefetchScalarGridSpec(
        num_scalar_prefetch=2,
        grid=(t // tile,),
        in_specs=[pl.BlockSpec((TAB_WIDTH,), lambda i, *_: (i,), memory_space=pltpu.SMEM),
                  pl.BlockSpec((tile, d), lambda i, *_: (i, 0)),
                  pl.BlockSpec((1, TOP_K, tile), lambda i, *_: (i, 0, 0))],
        out_specs=pl.BlockSpec(memory_space=pl.ANY),
        scratch_shapes=[pltpu.VMEM((2, n_local * ROW_TILE, LANES), jnp.float32),
                        pltpu.VMEM((FFN_TILE * ROW_TILE, LANES), jnp.float32),
                        pltpu.SMEM((2,), jnp.int32),
                        pltpu.SemaphoreType.DMA((2,)), pltpu.SemaphoreType.DMA(())],
    )
    return pl.pallas_call(
        _dispatch_kernel,
        grid_spec=grid_spec,
        out_shape=jax.ShapeDtypeStruct((n_rows * ROW_TILE, LANES), jnp.float32),
        compiler_params=_params("arbitrary"),
        name="dispatch",
    )(fill_from, pad_end, tab_flat, h2, lpos_t)


def _ffn_kernel(te_ref, rows_ref, n_used_ref, x_ref, w1_ref, b1_ref, w2_ref, b2_ref, y_ref,
                w1p_ref, w2b_ref, act_ref):
    i = pl.program_id(0)
    live = i < n_used_ref[0]
    n_pair = w1_ref.shape[2] // FFN_CHUNK
    half = FFN_CHUNK // 2
    d = w2_ref.shape[2]
    n_sub = jnp.where(live, (rows_ref[i] + FFN_TILE - 1) // FFN_TILE, 0)

    @pl.when(live & ((i == 0) | (te_ref[i] != te_ref[jnp.maximum(i - 1, 0)])))
    def _():
        src = lax.broadcasted_iota(jnp.int32, (FFN_CHUNK, FFN_CHUNK), 0)
        dst = lax.broadcasted_iota(jnp.int32, (FFN_CHUNK, FFN_CHUNK), 1)
        unzip = (src == jnp.where(dst < half, 2 * dst, 2 * (dst - half) + 1)).astype(jnp.bfloat16)
        for c in range(n_pair):
            cs = slice(c * FFN_CHUNK, (c + 1) * FFN_CHUNK)
            w1p_ref[:, cs] = _dot(w1_ref[0, :, cs].astype(jnp.bfloat16), unzip).astype(jnp.bfloat16)
        w2b_ref[...] = w2_ref[0].astype(jnp.bfloat16)

    def compute(n_rows):
        x = _from_tiles(x_ref, n_rows).astype(jnp.bfloat16)
        for c in range(0, n_pair, FFN_GROUP):
            cs = slice(c * FFN_CHUNK, (c + FFN_GROUP) * FFN_CHUNK)
            h = _dot(x, w1p_ref[:, cs]) + b1_ref[0, :, cs]
            g = jnp.concatenate([h[:, k * FFN_CHUNK:k * FFN_CHUNK + half] for k in range(FFN_GROUP)], axis=1)
            u = jnp.concatenate([h[:, k * FFN_CHUNK + half:(k + 1) * FFN_CHUNK] for k in range(FFN_GROUP)], axis=1)
            g = jnp.minimum(g, SWIGLU_LIMIT)
            u = jnp.clip(u, -SWIGLU_LIMIT, SWIGLU_LIMIT)
            act = g * jax.nn.sigmoid(SWIGLU_ALPHA * g) * (u + 1.0)
            act_ref[:n_rows, c * half:(c + FFN_GROUP) * half] = act.astype(act_ref.dtype)
        _to_tiles(_dot(act_ref[:n_rows, :], w2b_ref[...]) + b2_ref[0], y_ref)
        if n_rows * ROW_TILE < y_ref.shape[0]:
            y_ref[n_rows * ROW_TILE:, :] = jnp.zeros((y_ref.shape[0] - n_rows * ROW_TILE, LANES), y_ref.dtype)

    for tiles in range(1, FFN_STEP // FFN_TILE + 1):
        pl.when(n_sub == tiles)(functools.partial(compute, tiles * FFN_TILE))

    @pl.when(n_sub == 0)
    def _():
        y_ref[...] = jnp.zeros_like(y_ref)


def _unzip_bias(b1):
    e, f2 = b1.shape
    half = FFN_CHUNK // 2
    return b1.reshape(e, f2 // FFN_CHUNK, half, 2).transpose(0, 1, 3, 2).reshape(e, 1, f2)


def _ffn(xs, step_expert, step_rows, n_used, w1, b1, w2, b2):
    d, f2 = w1.shape[1:]
    n_steps = xs.shape[0] // (FFN_STEP * ROW_TILE)
    live = lambda i, nu: jnp.minimum(i, nu[0] - 1)
    wspec = lambda shape: pl.BlockSpec((1,) + shape, lambda i, te, nr, nu: (te[live(i, nu)], 0, 0))
    grid_spec = pltpu.PrefetchScalarGridSpec(
        num_scalar_prefetch=3,
        grid=(n_steps,),
        in_specs=[pl.BlockSpec((FFN_STEP * ROW_TILE, LANES), lambda i, te, nr, nu: (live(i, nu), 0)),
                  wspec((d, f2)), wspec((1, f2)), wspec((f2 // 2, d)), wspec((1, d))],
        out_specs=pl.BlockSpec((FFN_STEP * ROW_TILE, LANES), lambda i, te, nr, nu: (i, 0)),
        scratch_shapes=[pltpu.VMEM((d, f2), jnp.bfloat16), pltpu.VMEM((f2 // 2, d), jnp.bfloat16),
                        pltpu.VMEM((FFN_STEP, f2 // 2), jnp.bfloat16)],
    )
    return pl.pallas_call(
        _ffn_kernel,
        grid_spec=grid_spec,
        out_shape=jax.ShapeDtypeStruct(xs.shape, jnp.float32),
        compiler_params=_params("arbitrary"),
        name="expert_ffn",
    )(step_expert, step_rows, n_used, xs, w1, _unzip_bias(b1), w2, b2[:, None, :])


def _combine_kernel(tab_ref, next_tab_ref, lpos_ref, wts_ref, x1_ref, ga_ref, g_ref, y_ref, o_ref, buf_ref, sems):
    i = pl.program_id(0)
    n_steps = pl.num_programs(0)
    slot = i % 2
    tile = x1_ref.shape[0]
    n_local = buf_ref.shape[1] // ROW_TILE

    def fetch(tab, s):
        for e in range(N_EXPERTS):
            loc, glob, n = _piece_table(tab, e)

            @pl.when(n > 0)
            def _(loc=loc, glob=glob, n=n, e=e):
                pltpu.make_async_copy(y_ref.at[pl.ds(glob, n)], buf_ref.at[s, pl.ds(loc, n)],
                                      sems.at[s]).start(priority=e % 2)

    @pl.when(i == 0)
    def _():
        buf_ref[...] = jnp.zeros_like(buf_ref)
        fetch(tab_ref, 0)

    @pl.when(i + 1 < n_steps)
    def _():
        fetch(next_tab_ref, 1 - slot)

    n = pl.multiple_of(tab_ref[3 * N_EXPERTS], ROW_TILE)

    @pl.when(n > 0)
    def _():
        pltpu.make_async_copy(y_ref.at[pl.ds(0, n)], buf_ref.at[slot, pl.ds(0, n)], sems.at[slot]).wait()

    rows = _from_tiles(buf_ref, n_local, (slot,)).astype(jnp.bfloat16)
    part = tile // OUT_PARTS
    col = lax.broadcasted_iota(jnp.int32, (part, n_local), 1)
    for p in range(OUT_PARTS):
        r = slice(p * part, (p + 1) * part)
        pos = lpos_ref[r, :]
        w = wts_ref[r, :]
        pick = jnp.where(col == pos[:, 0:1], w[:, 0:1], 0.0)
        for k in range(1, TOP_K):
            pick = pick + jnp.where(col == pos[:, k:k + 1], w[:, k:k + 1], 0.0)
        hi = pick.astype(jnp.bfloat16)
        lo = (pick - hi.astype(jnp.float32)).astype(jnp.bfloat16)
        y = _dot(hi, rows) + _dot(lo, rows)
        o_ref[r, :] = x1_ref[r, :] + ga_ref[0] * _rms(y, g_ref[...])


def _combine(ys, tab_flat, lpos, wts, x1, mod3, g_post, *, seq):
    t, d = x1.shape
    tile = DISPATCH_TILE
    n_steps = t // tile
    per_seq = seq // tile
    n_local = tile * TOP_K
    return pl.pallas_call(
        _combine_kernel,
        grid=(n_steps,),
        in_specs=[pl.BlockSpec((TAB_WIDTH,), lambda i: (i,), memory_space=pltpu.SMEM),
                  pl.BlockSpec((TAB_WIDTH,), lambda i: (jnp.minimum(i + 1, n_steps - 1),), memory_space=pltpu.SMEM),
                  pl.BlockSpec((tile, TOP_K), lambda i: (i, 0)),
                  pl.BlockSpec((tile, TOP_K), lambda i: (i, 0)),
                  pl.BlockSpec((tile, d), lambda i: (i, 0)),
                  pl.BlockSpec((1, 1, d), lambda i: (i // per_seq, 0, 5)),
                  pl.BlockSpec((1, d), lambda i: (0, 0)),
                  pl.BlockSpec(memory_space=pl.ANY)],
        out_specs=pl.BlockSpec((tile, d), lambda i: (i, 0)),
        out_shape=jax.ShapeDtypeStruct((t, d), jnp.float32),
        scratch_shapes=[pltpu.VMEM((2, n_local * ROW_TILE, LANES), ys.dtype), pltpu.SemaphoreType.DMA((2,))],
        compiler_params=_params("arbitrary"),
        name="combine",
    )(tab_flat, tab_flat, lpos, wts, x1, mod3, g_post, ys)


def _rope_tables(seq):
    pos = np.arange(seq)
    n_freq = HEAD_DIM // 4
    freqs = ROPE_BASE ** (-jnp.arange(n_freq, dtype=jnp.float32) / n_freq)
    rows = jnp.asarray(pos // GRID_W, jnp.float32)[:, None] * freqs[None, :]
    cols = jnp.asarray(pos % GRID_W, jnp.float32)[:, None] * freqs[None, :]
    ang = jnp.concatenate([rows, rows, cols, cols], axis=1)
    sign = np.tile(np.repeat([-1.0, 1.0], n_freq), 2).astype(np.float32)
    cos = jnp.cos(ang)
    sin = jnp.sin(ang) * sign[None, :]
    reps = LANES // HEAD_DIM
    return jnp.tile(cos, (1, reps)), jnp.tile(sin, (1, reps))


def kernel(x, c, ctx, c_ctx, w_ada, b_ada, g_pre_mix, g_post_mix, g_pre_ffn, g_post_ffn, w_in, g_grp_a, g_grp_b,
           sink_a, rpb_b, w_out, w_router, b_router, w_mlp1, b_mlp1, w_mlp2, b_mlp2):
    batch, seq, d = x.shape
    n_ctx = ctx.shape[1]
    assert w_ada.shape[0] == 1, "single layer"
    assert seq % (ATT_BLOCKS * GRID_W * NB_Q_ROWS) == 0 and seq // GRID_W >= NB_K_ROWS
    assert seq % (ATT_BLOCKS * BLOCK_A) == 0 and seq >= BLOCK_A + 2 * WINDOW
    n_tok = batch * seq
    bf16 = jnp.bfloat16

    head_order = np.asarray([k * GQA_GROUP + j for j in range(GQA_GROUP) for k in range(N_KV_A)])
    col_order = (head_order[:, None] * HEAD_DIM + np.arange(HEAD_DIM)[None, :]).reshape(-1)

    mod_rows = -(-(batch + 1) // 8) * 8
    cc = jnp.zeros((mod_rows, d), jnp.float32).at[:batch].set(c).at[batch].set(c_ctx)
    mod3 = _ada(cc, w_ada[0], b_ada[0]).reshape(mod_rows, 1, 6 * d)

    w_in0 = w_in[0]
    w_lat = jnp.concatenate([w_in0[:, :WIDTH_A][:, col_order], w_in0[:, WIDTH_A:]], axis=1).astype(bf16)
    kv_cols = np.concatenate([np.arange(WIDTH_A, WIDTH_A + 2 * WIDTH_KV_A),
                              np.arange(WIDTH_A + 2 * WIDTH_KV_A + WIDTH_B, w_in0.shape[1])])
    w_ctx = w_in0[:, kv_cols].astype(bf16)
    g_pre = g_pre_mix[0].reshape(1, d)
    qa, ka, va, qb, kb, vb = _inproj(x.reshape(n_tok, d), mod3, g_pre, w_lat, _rope_tables(seq),
                                     seq=seq, mod_row0=0, latent=True)
    kac, vac, kbc, vbc = _inproj(ctx.reshape(batch * n_ctx, d), mod3, g_pre, w_ctx, None,
                                 seq=n_ctx, mod_row0=batch, latent=False)

    oa = _window_attention(qa, ka.reshape(batch, seq, -1), va.reshape(batch, seq, -1),
                           kac.reshape(batch, n_ctx, -1), vac.reshape(batch, n_ctx, -1),
                           sink_a[0].astype(jnp.float32) * LOG2E, batch=batch, seq=seq)
    tabs, cls_of_block = _nb_bias_tables(rpb_b[0], seq // GRID_W)
    ob = _neighbourhood_attention(qb, kb.reshape(batch, seq, -1), vb.reshape(batch, seq, -1),
                                  kbc.reshape(batch, n_ctx, -1), vbc.reshape(batch, n_ctx, -1),
                                  tabs, cls_of_block, batch=batch, seq=seq)

    w_out0 = w_out[0]
    w_o = jnp.concatenate([w_out0[:WIDTH_A][col_order], w_out0[WIDTH_A:]], axis=0).astype(bf16)
    x1, h2, idx, wts, rank, size, base, counts = _outproj(
        oa, ob, x.reshape(n_tok, d), mod3,
        g_grp_a[0][col_order].reshape(1, -1), g_grp_b[0].reshape(1, -1),
        g_post_mix[0].reshape(1, d), g_pre_ffn[0].reshape(1, d),
        w_o, w_router[0], b_router[0].reshape(1, -1), seq=seq)

    n_tok_tiles = n_tok // DISPATCH_TILE
    size = size.reshape(n_tok_tiles, N_EXPERTS)
    counts = counts.reshape(-1)
    padded = (counts + FFN_STEP - 1) // FFN_STEP * FFN_STEP
    pad_end = jnp.cumsum(padded).astype(jnp.int32)
    pad_start = pad_end - padded
    n_steps = n_tok * TOP_K // FFN_STEP + N_EXPERTS
    n_used = (pad_end[-1:] // FFN_STEP).astype(jnp.int32)
    step_row0 = jnp.arange(n_steps, dtype=jnp.int32) * FFN_STEP
    step_expert = jnp.minimum(jnp.sum(step_row0[:, None] >= pad_end[None, :], axis=1),
                              N_EXPERTS - 1).astype(jnp.int32)
    own = step_expert[:, None] == jnp.arange(N_EXPERTS, dtype=jnp.int32)
    step_rows = jnp.clip(jnp.sum(jnp.where(own, (pad_start + counts)[None, :], 0), axis=1) - step_row0,
                         0, FFN_STEP).astype(jnp.int32)
    fill_from = (pad_start + jnp.maximum((counts + FFN_TILE - 1) // FFN_TILE * FFN_TILE - FFN_TILE, 0)
                 ).astype(jnp.int32)
    local0 = jnp.cumsum(size, axis=1) - size
    global0 = pad_start[None, :] + base.reshape(n_tok_tiles, N_EXPERTS)
    tab = jnp.concatenate([local0, global0, size, jnp.sum(size, axis=1, keepdims=True),
                           jnp.zeros((n_tok_tiles, TAB_WIDTH - 3 * N_EXPERTS - 1), jnp.int32)], axis=1)
    tab = (tab * ROW_TILE).reshape(-1).astype(jnp.int32)
    hot = idx[:, :, None, :] == jnp.arange(N_EXPERTS, dtype=jnp.int32)[None, None, :, None]
    lpos_t = rank + jnp.sum(jnp.where(hot, local0[:, None, :, None], 0), axis=2)
    by_token = lambda a: jnp.swapaxes(a, 1, 2).reshape(n_tok, TOP_K)

    xs = _dispatch(h2, tab, lpos_t, fill_from * ROW_TILE, pad_end * ROW_TILE, n_rows=n_steps * FFN_STEP)
    ys = _ffn(xs, step_expert, step_rows, n_used, w_mlp1[0], b_mlp1[0], w_mlp2[0], b_mlp2[0])
    out = _combine(ys, tab, by_token(lpos_t), by_token(wts), x1, mod3, g_post_ffn[0].reshape(1, d), seq=seq)
    return out.reshape(batch, seq, d)
```

```python
import functools

import numpy as np
import jax
import jax.numpy as jnp
from jax import lax
from jax.experimental import pallas as pl
from jax.experimental.pallas import tpu as pltpu

GRID_W = 64
HEAD_DIM = 64
N_HEADS_A = 8
N_KV_A = 2
GQA_GROUP = N_HEADS_A // N_KV_A
N_HEADS_B = 8
WIDTH_A = N_HEADS_A * HEAD_DIM
WIDTH_KV_A = N_KV_A * HEAD_DIM
WIDTH_B = N_HEADS_B * HEAD_DIM
WINDOW = 128
BLOCK_A = 128
NA_ROWS = 8
NA_COLS = 16
N_EXPERTS = 32
TOP_K = 4
SWIGLU_LIMIT = 7.0
SWIGLU_ALPHA = 1.702
ROPE_BASE = 10000.0
EPS = 1e-6
NEG_INF = -1e30

LANES = 128
VMEM_LIMIT = 56 * 1024 * 1024

LOG2E = 1.4426950408889634
ATT_BLOCKS = 4
NB_Q_ROWS = 2
NB_K_ROWS = NB_Q_ROWS + NA_ROWS
PROJ_TILE = 1024
OUT_TILE = 256
OUT_PARTS = 2
STAGE_SLOTS = 3
FFN_TILE = 256
FFN_STEP = 4 * FFN_TILE
FFN_CHUNK = 256
FFN_GROUP = 4
DISPATCH_TILE = OUT_TILE
ROW_TILE = 8
TAB_WIDTH = 128


def _params(*sem):
    return pltpu.CompilerParams(dimension_semantics=sem, vmem_limit_bytes=VMEM_LIMIT)


def _rms(x, g):
    return x * lax.rsqrt(jnp.mean(x * x, axis=-1, keepdims=True) + EPS) * g


def _dot(a, b):
    return jnp.dot(a, b, preferred_element_type=jnp.float32)


def _dot_nt(a, b):
    return lax.dot_general(a, b, (((1,), (1,)), ((), ())), preferred_element_type=jnp.float32)


def _ada_kernel(c_ref, w_ref, b_ref, o_ref):
    c = c_ref[...]
    s = (c * jax.nn.sigmoid(c)).astype(jnp.bfloat16)
    o_ref[...] = _dot(s, w_ref[...].astype(jnp.bfloat16)) + b_ref[...]


def _ada(cc, w_ada, b_ada):
    rows, d = cc.shape
    n_out = w_ada.shape[1]
    return pl.pallas_call(
        _ada_kernel,
        grid=(n_out // d,),
        in_specs=[pl.BlockSpec((rows, d), lambda j: (0, 0)),
                  pl.BlockSpec((d, d), lambda j: (0, j)),
                  pl.BlockSpec((1, d), lambda j: (0, j))],
        out_specs=pl.BlockSpec((rows, d), lambda j: (0, j)),
        out_shape=jax.ShapeDtypeStruct((rows, n_out), jnp.float32),
        compiler_params=_params("arbitrary"),
        name="ada",
    )(cc, w_ada, b_ada.reshape(1, n_out))


def _rope(x, cos, sin):
    w = x.shape[1]
    reps = w // LANES
    if reps > 1:
        cos = jnp.concatenate([cos] * reps, axis=1)
        sin = jnp.concatenate([sin] * reps, axis=1)
    lane = lax.broadcasted_iota(jnp.int32, x.shape, 1)
    quarter = HEAD_DIM // 4
    partner = jnp.where(lane % (2 * quarter) < quarter,
                        pltpu.roll(x, w - quarter, 1), pltpu.roll(x, quarter, 1))
    return x * cos + partner * sin


def _inproj_kernel(x_ref, sh_ref, sc_ref, g_ref, w_ref, *rest, latent):
    x = x_ref[...]
    h = _rms(x, g_ref[...]) * (1.0 + sc_ref[0]) + sh_ref[0]
    p = _dot(h.astype(jnp.bfloat16), w_ref[...])
    if latent:
        cos_ref, sin_ref, qa_ref, ka_ref, va_ref, qb_ref, kb_ref, vb_ref = rest
        cos, sin = cos_ref[...], sin_ref[...]
        scale = HEAD_DIM ** -0.5 * LOG2E
        o = 0
        qa_ref[...] = (_rope(p[:, o:o + WIDTH_A], cos, sin) * scale).astype(qa_ref.dtype)
        o += WIDTH_A
        ka_ref[...] = _rope(p[:, o:o + WIDTH_KV_A], cos, sin).astype(ka_ref.dtype)
        o += WIDTH_KV_A
        va_ref[...] = p[:, o:o + WIDTH_KV_A].astype(va_ref.dtype)
        o += WIDTH_KV_A
        qb_ref[...] = (p[:, o:o + WIDTH_B] * scale).astype(qb_ref.dtype)
        o += WIDTH_B
    else:
        ka_ref, va_ref, kb_ref, vb_ref = rest
        o = 0
        ka_ref[...] = p[:, o:o + WIDTH_KV_A].astype(ka_ref.dtype)
        o += WIDTH_KV_A
        va_ref[...] = p[:, o:o + WIDTH_KV_A].astype(va_ref.dtype)
        o += WIDTH_KV_A
    kb_ref[...] = p[:, o:o + WIDTH_B].astype(kb_ref.dtype)
    o += WIDTH_B
    vb_ref[...] = p[:, o:o + WIDTH_B].astype(vb_ref.dtype)


def _inproj(x2d, mod3, g_pre, w, rope_tabs, *, seq, mod_row0, latent):
    t, d = x2d.shape
    tile = min(PROJ_TILE, seq)
    per_seq = seq // tile
    if latent:
        mod_row = lambda i: i // per_seq
    else:
        mod_row = lambda i: mod_row0
    in_specs = [pl.BlockSpec((tile, d), lambda i: (i, 0)),
                pl.BlockSpec((1, 1, d), lambda i: (mod_row(i), 0, 0)),
                pl.BlockSpec((1, 1, d), lambda i: (mod_row(i), 0, 1)),
                pl.BlockSpec((1, d), lambda i: (0, 0)),
                pl.BlockSpec(w.shape, lambda i: (0, 0))]
    args = [x2d, mod3, mod3, g_pre, w]
    widths = [WIDTH_KV_A, WIDTH_KV_A, WIDTH_B, WIDTH_B]
    if latent:
        in_specs += [pl.BlockSpec((tile, LANES), lambda i: (i % per_seq, 0))] * 2
        args += list(rope_tabs)
        widths = [WIDTH_A, WIDTH_KV_A, WIDTH_KV_A, WIDTH_B, WIDTH_B, WIDTH_B]
    return pl.pallas_call(
        functools.partial(_inproj_kernel, latent=latent),
        grid=(t // tile,),
        in_specs=in_specs,
        out_specs=[pl.BlockSpec((tile, wd), lambda i: (i, 0)) for wd in widths],
        out_shape=[jax.ShapeDtypeStruct((t, wd), jnp.bfloat16) for wd in widths],
        compiler_params=_params("parallel"),
        name="inproj_latent" if latent else "inproj_ctx",
    )(*args)


def _ones_beside(v):
    return jnp.concatenate([v, jnp.ones_like(v)], axis=1)


def _group_attention(q_ref, o_ref, rows, groups, s_ref, p_ref, m_ref):
    tq = rows.stop - rows.start
    low = lax.broadcasted_iota(jnp.int32, (tq, LANES), 1) < HEAD_DIM
    layout = []
    base = 0
    for pairs, k_loc, k_ctx, _, _, _, _ in groups:
        heads = [(j, 0) for j in pairs] + [(j, 1) for j in pairs]
        n_loc = k_loc.shape[0]
        stacked = []
        for j, half in heads:
            q = q_ref[rows, j * LANES:(j + 1) * LANES]
            stacked.append(jnp.where(low if half == 0 else ~low, q, jnp.zeros_like(q)))
        qs = jnp.concatenate(stacked, axis=0)
        n = len(heads) * tq
        s_ref[base:base + n, :n_loc] = _dot_nt(qs, k_loc)
        s_ref[base:base + n, n_loc:] = _dot_nt(qs, k_ctx)
        layout.append((base, heads, n_loc))
        base += n
    for (base, heads, n_loc), group in zip(layout, groups):
        bias_of, sink_of = group[5], group[6]
        for g, (j, half) in enumerate(heads):
            r = slice(base + g * tq, base + (g + 1) * tq)
            s_loc = s_ref[r, :n_loc] + bias_of(j, half)
            s_ctx = s_ref[r, n_loc:]
            m = jnp.maximum(jnp.max(s_loc, axis=1, keepdims=True), jnp.max(s_ctx, axis=1, keepdims=True))
            sink = sink_of(j, half)
            if sink is not None:
                m = jnp.maximum(m, sink)
            p_ref[r, :n_loc] = jnp.exp2((s_loc - m).astype(jnp.bfloat16))
            p_ref[r, n_loc:] = jnp.exp2((s_ctx - m).astype(jnp.bfloat16))
            m_ref[r, :] = m
    for (base, heads, n_loc), group in zip(layout, groups):
        pairs, _, _, v_loc, v_ctx, _, sink_of = group
        n = len(heads) * tq
        both = (_dot(p_ref[base:base + n, :n_loc], _ones_beside(v_loc))
                + _dot(p_ref[base:base + n, n_loc:], _ones_beside(v_ctx)))
        for a, j in enumerate(pairs):
            outs = []
            for half in range(2):
                g = half * len(pairs) + a
                r = slice(g * tq, (g + 1) * tq)
                den = both[r, LANES:]
                sink = sink_of(j, half)
                if sink is not None:
                    den = den + jnp.exp2(sink - m_ref[base + g * tq:base + (g + 1) * tq, :])
                outs.append(both[r, :LANES] / den)
            o_ref[rows, j * LANES:(j + 1) * LANES] = jnp.where(low, outs[0], outs[1]).astype(o_ref.dtype)


def _attention_scratch(tq, n_heads, n_keys):
    return [pltpu.VMEM((n_heads * tq, n_keys), jnp.float32), pltpu.VMEM((n_heads * tq, n_keys), jnp.bfloat16),
            pltpu.VMEM((n_heads * tq, 1), jnp.float32)]


def _win_kernel(sink_ref, q_ref, k_ref, v_ref, kc_ref, vc_ref, o_ref, *scratch, seq):
    span = BLOCK_A + 2 * WINDOW
    pairs = list(range(GQA_GROUP))
    for sb in range(ATT_BLOCKS):
        i = pl.program_id(1) * ATT_BLOCKS + sb
        start = pl.multiple_of(jnp.clip(i * BLOCK_A - WINDOW, 0, seq - span), BLOCK_A)
        qpos = i * BLOCK_A + lax.broadcasted_iota(jnp.int32, (BLOCK_A, span), 0)
        kpos = start + lax.broadcasted_iota(jnp.int32, (BLOCK_A, span), 1)
        bias = jnp.where(jnp.abs(kpos - qpos) <= WINDOW, 0.0, NEG_INF).astype(jnp.float32)
        groups = [([j], k_ref[0, pl.ds(start, span), :], kc_ref[0], v_ref[0, pl.ds(start, span), :], vc_ref[0],
                   lambda j, half: bias, lambda j, half: sink_ref[half * GQA_GROUP + j]) for j in pairs]
        _group_attention(q_ref, o_ref, slice(sb * BLOCK_A, (sb + 1) * BLOCK_A), groups, *scratch)


def _window_attention(qa, ka, va, kac, vac, sink, *, batch, seq):
    tq = ATT_BLOCKS * BLOCK_A
    nb = seq // tq
    n_ctx = kac.shape[1]
    return pl.pallas_call(
        functools.partial(_win_kernel, seq=seq),
        grid=(batch, nb),
        in_specs=[pl.BlockSpec(memory_space=pltpu.SMEM),
                  pl.BlockSpec((tq, WIDTH_A), lambda b, i: (b * nb + i, 0)),
                  pl.BlockSpec((1, seq, WIDTH_KV_A), lambda b, i: (b, 0, 0)),
                  pl.BlockSpec((1, seq, WIDTH_KV_A), lambda b, i: (b, 0, 0)),
                  pl.BlockSpec((1, n_ctx, WIDTH_KV_A), lambda b, i: (b, 0, 0)),
                  pl.BlockSpec((1, n_ctx, WIDTH_KV_A), lambda b, i: (b, 0, 0))],
        out_specs=pl.BlockSpec((tq, WIDTH_A), lambda b, i: (b * nb + i, 0)),
        out_shape=jax.ShapeDtypeStruct((batch * seq, WIDTH_A), jnp.bfloat16),
        scratch_shapes=_attention_scratch(BLOCK_A, N_HEADS_A, BLOCK_A + 2 * WINDOW + n_ctx),
        compiler_params=_params("parallel", "arbitrary"),
        name="window_attention",
    )(sink, qa, ka, va, kac, vac)


def _nb_kernel(q_ref, k_ref, v_ref, kc_ref, vc_ref, *rest, rows_n):
    tab_refs, o_ref, scratch = rest[:ATT_BLOCKS], rest[ATT_BLOCKS], rest[ATT_BLOCKS + 1:]
    n_keys = NB_K_ROWS * GRID_W
    tq = NB_Q_ROWS * GRID_W
    for sb in range(ATT_BLOCKS):
        m = pl.program_id(1) * ATT_BLOCKS + sb
        start_row = jnp.clip(NB_Q_ROWS * m - NA_ROWS // 2, 0, rows_n - NB_K_ROWS)
        start = pl.multiple_of(start_row * GRID_W, LANES)
        tab_ref = tab_refs[sb]
        groups = []
        for j in range(q_ref.shape[1] // LANES):
            cols = slice(j * LANES, (j + 1) * LANES)
            groups.append(([j], k_ref[0, pl.ds(start, n_keys), cols], kc_ref[0, :, cols],
                           v_ref[0, pl.ds(start, n_keys), cols], vc_ref[0, :, cols],
                           lambda j, half: tab_ref[0, 2 * j + half].astype(jnp.float32), lambda j, half: None))
        _group_attention(q_ref, o_ref, slice(sb * tq, (sb + 1) * tq), groups, *scratch)


def _nb_classes(rows_n):
    n_blocks = rows_n // NB_Q_ROWS
    sig = {}
    cls_of_block = []
    reps = []
    for m in range(n_blocks):
        start_row = int(np.clip(NB_Q_ROWS * m - NA_ROWS // 2, 0, rows_n - NB_K_ROWS))
        key = tuple((start_row - r, int(np.clip(r - NA_ROWS // 2, 0, rows_n - NA_ROWS)) - r)
                    for r in range(NB_Q_ROWS * m, NB_Q_ROWS * (m + 1)))
        if key not in sig:
            sig[key] = len(reps)
            reps.append(m)
        cls_of_block.append(sig[key])
    return np.asarray(cls_of_block, np.int32), reps


def _nb_bias_tables(rpb, rows_n):
    cls_of_block, reps = _nb_classes(rows_n)
    n_heads = rpb.shape[0]
    cq = np.arange(GRID_W)[:, None]
    ck = np.arange(GRID_W)[None, :]
    cs = np.clip(cq - NA_COLS // 2, 0, GRID_W - NA_COLS)
    col_ok = (ck >= cs) & (ck < cs + NA_COLS)
    pick = ((ck - cq + NA_COLS - 1)[None] == np.arange(2 * NA_COLS - 1)[:, None, None]) & col_ok[None]
    tiles = jnp.einsum('hrd,dqk->hrqk', rpb.astype(jnp.float32), jnp.asarray(pick, jnp.float32),
                       precision=lax.Precision.HIGHEST)
    tiles = jnp.where(jnp.asarray(col_ok)[None, None], tiles * LOG2E, NEG_INF).astype(jnp.bfloat16)
    blocked = jnp.full((n_heads, GRID_W, GRID_W), NEG_INF, jnp.bfloat16)
    tabs = []
    for m in reps:
        start_row = int(np.clip(NB_Q_ROWS * m - NA_ROWS // 2, 0, rows_n - NB_K_ROWS))
        q_rows = []
        for r in range(NB_Q_ROWS * m, NB_Q_ROWS * (m + 1)):
            rs = int(np.clip(r - NA_ROWS // 2, 0, rows_n - NA_ROWS))
            q_rows.append(jnp.concatenate(
                [tiles[:, krow - r + NA_ROWS - 1] if rs <= krow < rs + NA_ROWS else blocked
                 for krow in range(start_row, start_row + NB_K_ROWS)], axis=2))
        tabs.append(jnp.concatenate(q_rows, axis=1))
    return jnp.stack(tabs), cls_of_block


def _neighbourhood_attention(qb, kb, vb, kbc, vbc, tabs, cls_of_block, *, batch, seq):
    rows_n = seq // GRID_W
    tq = ATT_BLOCKS * NB_Q_ROWS * GRID_W
    nstep = seq // tq
    n_ctx = kbc.shape[1]

    def tab_spec(sb):
        return pl.BlockSpec((1,) + tabs.shape[1:], lambda b, m, c: (c[m * ATT_BLOCKS + sb], 0, 0, 0))

    grid_spec = pltpu.PrefetchScalarGridSpec(
        num_scalar_prefetch=1,
        grid=(batch, nstep),
        in_specs=[pl.BlockSpec((tq, WIDTH_B), lambda b, m, c: (b * nstep + m, 0)),
                  pl.BlockSpec((1, seq, WIDTH_B), lambda b, m, c: (b, 0, 0)),
                  pl.BlockSpec((1, seq, WIDTH_B), lambda b, m, c: (b, 0, 0)),
                  pl.BlockSpec((1, n_ctx, WIDTH_B), lambda b, m, c: (b, 0, 0)),
                  pl.BlockSpec((1, n_ctx, WIDTH_B), lambda b, m, c: (b, 0, 0))]
                 + [tab_spec(sb) for sb in range(ATT_BLOCKS)],
        out_specs=pl.BlockSpec((tq, WIDTH_B), lambda b, m, c: (b * nstep + m, 0)),
        scratch_shapes=_attention_scratch(NB_Q_ROWS * GRID_W, N_HEADS_B, NB_K_ROWS * GRID_W + n_ctx),
    )

    def body(c_ref, *refs):
        _nb_kernel(*refs, rows_n=rows_n)

    return pl.pallas_call(
        body,
        grid_spec=grid_spec,
        out_shape=jax.ShapeDtypeStruct((batch * seq, WIDTH_B), jnp.bfloat16),
        compiler_params=_params("parallel", "arbitrary"),
        name="neighbourhood_attention",
    )(jnp.asarray(cls_of_block), qb, kb, vb, kbc, vbc, *([tabs] * ATT_BLOCKS))


def _outproj_kernel(oa_ref, ob_ref, x_ref, ga_ref, sh_ref, sc_ref, gga_ref, ggb_ref, gpost_ref, gpre_ref,
                    wo_ref, wr_hi_ref, wr_lo_ref, br_ref,
                    x1_ref, h2_ref, idx_ref, wts_ref, rank_ref, size_ref, base_ref, cnt_ref, run_ref):
    i = pl.program_id(0)

    @pl.when(i == 0)
    def _():
        run_ref[...] = jnp.zeros_like(run_ref)

    n_rows = x_ref.shape[0]
    parts = [slice(p * n_rows // OUT_PARTS, (p + 1) * n_rows // OUT_PARTS) for p in range(OUT_PARTS)]
    na = [_rms(oa_ref[r, :].astype(jnp.float32), gga_ref[...]).astype(jnp.bfloat16) for r in parts]
    nb = [_rms(ob_ref[r, :].astype(jnp.float32), ggb_ref[...]).astype(jnp.bfloat16) for r in parts]
    mix = [_dot(a, wo_ref[:WIDTH_A, :]) + _dot(b, wo_ref[WIDTH_A:, :]) for a, b in zip(na, nb)]
    x1 = [x_ref[r, :] + ga_ref[0] * _rms(m, gpost_ref[...]) for r, m in zip(parts, mix)]
    for r, v in zip(parts, x1):
        x1_ref[r, :] = v
    h2 = [_rms(v, gpre_ref[...]) * (1.0 + sc_ref[0]) + sh_ref[0] for v in x1]
    for r, v in zip(parts, h2):
        h2_ref[r, :] = v.astype(h2_ref.dtype)

    h_hi = [v.astype(jnp.bfloat16) for v in h2]
    h_lo = [(v - hi.astype(jnp.float32)).astype(jnp.bfloat16) for v, hi in zip(h2, h_hi)]
    logits = jnp.concatenate([_dot(hi, wr_hi_ref[...]) + _dot(lo, wr_hi_ref[...]) + _dot(hi, wr_lo_ref[...])
                              for hi, lo in zip(h_hi, h_lo)], axis=0)
    work = logits.T[:N_EXPERTS] + br_ref[...]
    e, t = work.shape
    row = lax.broadcasted_iota(jnp.int32, (e, t), 0)
    chosen = jnp.zeros((e, t), jnp.float32)
    vals, sels, hots = [], [], []
    for _k in range(TOP_K):
        mx = jnp.max(work, axis=0, keepdims=True)
        sel = jnp.min(jnp.where(work == mx, row, e), axis=0, keepdims=True)
        hot = row == sel
        vals.append(mx)
        sels.append(sel)
        hots.append(hot)
        work = jnp.where(hot, -jnp.inf, work)
        chosen = chosen + hot.astype(jnp.float32)
    ex = [jnp.exp(v - vals[0]) for v in vals]
    den = ex[0] + ex[1] + ex[2] + ex[3]

    earlier = (lax.broadcasted_iota(jnp.int32, (t, t), 0) < lax.broadcasted_iota(jnp.int32, (t, t), 1))
    prefix = _dot(chosen.astype(jnp.bfloat16), earlier.astype(jnp.bfloat16))
    ranks = [jnp.sum(jnp.where(h, prefix, 0.0), axis=0, keepdims=True) for h in hots]
    size = jnp.sum(chosen, axis=1, keepdims=True).astype(jnp.int32)
    size_ref[0] = size
    base_ref[0] = run_ref[...]
    run_ref[...] = run_ref[...] + size
    cnt_ref[...] = run_ref[...]

    idx_ref[0] = jnp.concatenate(sels, axis=0)
    wts_ref[0] = jnp.concatenate([v / den for v in ex], axis=0)
    rank_ref[0] = jnp.concatenate(ranks, axis=0).astype(jnp.int32)


def _outproj(oa, ob, x2d, mod3, gga, ggb, gpost, gpre, w_out, w_router, b_router, *, seq):
    t, d = x2d.shape
    tile = OUT_TILE
    per_seq = seq // tile
    row = lambda i: (i, 0)
    const = lambda i: (0, 0)
    modspec = lambda part: pl.BlockSpec((1, 1, d), lambda i: (i // per_seq, 0, part))
    per_tile = lambda shape: pl.BlockSpec((1,) + shape, lambda i: (i, 0, 0))
    n_steps = t // tile
    w_pad = jnp.zeros((d, LANES), jnp.float32).at[:, :N_EXPERTS].set(w_router)
    wr_hi = w_pad.astype(jnp.bfloat16)
    wr_lo = (w_pad - wr_hi.astype(jnp.float32)).astype(jnp.bfloat16)
    outs = pl.pallas_call(
        _outproj_kernel,
        grid=(n_steps,),
        in_specs=[pl.BlockSpec((tile, WIDTH_A), row), pl.BlockSpec((tile, WIDTH_B), row),
                  pl.BlockSpec((tile, d), row),
                  modspec(2), modspec(3), modspec(4),
                  pl.BlockSpec((1, WIDTH_A), const), pl.BlockSpec((1, WIDTH_B), const),
                  pl.BlockSpec((1, d), const), pl.BlockSpec((1, d), const),
                  pl.BlockSpec(w_out.shape, const), pl.BlockSpec((d, LANES), const), pl.BlockSpec((d, LANES), const),
                  pl.BlockSpec((N_EXPERTS, 1), const)],
        out_specs=[pl.BlockSpec((tile, d), row), pl.BlockSpec((tile, d), row),
                   per_tile((TOP_K, tile)), per_tile((TOP_K, tile)), per_tile((TOP_K, tile)),
                   per_tile((N_EXPERTS, 1)), per_tile((N_EXPERTS, 1)),
                   pl.BlockSpec((N_EXPERTS, 1), const)],
        out_shape=[jax.ShapeDtypeStruct((t, d), jnp.float32), jax.ShapeDtypeStruct((t, d), jnp.bfloat16),
                   jax.ShapeDtypeStruct((n_steps, TOP_K, tile), jnp.int32),
                   jax.ShapeDtypeStruct((n_steps, TOP_K, tile), jnp.float32),
                   jax.ShapeDtypeStruct((n_steps, TOP_K, tile), jnp.int32),
                   jax.ShapeDtypeStruct((n_steps, N_EXPERTS, 1), jnp.int32),
                   jax.ShapeDtypeStruct((n_steps, N_EXPERTS, 1), jnp.int32),
                   jax.ShapeDtypeStruct((N_EXPERTS, 1), jnp.int32)],
        scratch_shapes=[pltpu.VMEM((N_EXPERTS, 1), jnp.int32)],
        compiler_params=_params("arbitrary"),
        name="outproj_router",
    )(oa, ob, x2d, mod3, mod3, mod3, gga, ggb, gpost, gpre, w_out, wr_hi, wr_lo, b_router.reshape(N_EXPERTS, 1))
    return outs


def _piece_table(tab_ref, e):
    return (pl.multiple_of(tab_ref[e], ROW_TILE), pl.multiple_of(tab_ref[N_EXPERTS + e], ROW_TILE),
            pl.multiple_of(tab_ref[2 * N_EXPERTS + e], ROW_TILE))


def _to_tiles(x, ref, index=()):
    for c in range(ROW_TILE):
        ref[index + (pl.ds(c, x.shape[0], stride=ROW_TILE), slice(None))] = x[:, c * LANES:(c + 1) * LANES]


def _from_tiles(ref, n_rows, index=()):
    return jnp.concatenate([ref[index + (pl.ds(c, n_rows, stride=ROW_TILE), slice(None))]
                            for c in range(ROW_TILE)], axis=1)


def _dispatch_kernel(fill_from_ref, pad_end_ref, tab_ref, h_ref, lpos_ref, xs_ref,
                     stage_ref, zero_ref, pend_ref, sems, zsem):
    i = pl.program_id(0)
    n_steps = pl.num_programs(0)
    n_slots = stage_ref.shape[0]
    slot = i % n_slots
    n_local = stage_ref.shape[1] // ROW_TILE
    fill_rows = zero_ref.shape[0]

    def fill(row):
        return pltpu.make_async_copy(zero_ref, xs_ref.at[pl.ds(pl.multiple_of(row, fill_rows), fill_rows)], zsem)

    def fills(e):
        return (pad_end_ref[e] - fill_from_ref[e]) // fill_rows

    def drain(s):
        n = pl.multiple_of(pend_ref[s], ROW_TILE)

        @pl.when(n > 0)
        def _():
            pltpu.make_async_copy(stage_ref.at[s, pl.ds(0, n)], xs_ref.at[pl.ds(0, n)], sems.at[s]).wait()
        pend_ref[s] = 0

    @pl.when(i == 0)
    def _():
        zero_ref[...] = jnp.zeros_like(zero_ref)
        for s in range(n_slots):
            pend_ref[s] = 0

        def start_e(e, c):
            def start_j(j, c2):
                fill(fill_from_ref[e] + j * fill_rows).start()
                return c2
            lax.fori_loop(0, fills(e), start_j, 0)
            return c + fills(e)
        n_fill = lax.fori_loop(0, N_EXPERTS, start_e, 0)

        def wait_all(j, c):
            fill(0).wait()
            return c
        lax.fori_loop(0, n_fill, wait_all, 0)

    row = lax.broadcasted_iota(jnp.int32, (n_local, h_ref.shape[0]), 0)
    place = row == lpos_ref[0, 0:1, :]
    for k in range(1, TOP_K):
        place = place | (row == lpos_ref[0, k:k + 1, :])
    rows = _dot(place.astype(jnp.bfloat16), h_ref[...])

    drain(slot)
    _to_tiles(rows, stage_ref, (slot,))

    total = 0
    for e in range(N_EXPERTS):
        loc, glob, n = _piece_table(tab_ref, e)

        @pl.when(n > 0)
        def _(loc=loc, glob=glob, n=n, e=e):
            pltpu.make_async_copy(stage_ref.at[slot, pl.ds(loc, n)], xs_ref.at[pl.ds(glob, n)],
                                  sems.at[slot]).start(priority=e % 2)
        total = total + n
    pend_ref[slot] = total

    tail0 = pad_end_ref[N_EXPERTS - 1]
    n_tail = (xs_ref.shape[0] - tail0) // fill_rows
    per_step = (n_tail + n_steps - 1) // n_steps

    def start_tail(j, c):
        t = i * per_step + j

        @pl.when(t < n_tail)
        def _():
            fill(tail0 + t * fill_rows).start()
        return c
    lax.fori_loop(0, per_step, start_tail, 0)

    @pl.when(i == n_steps - 1)
    def _():
        for s in range(n_slots):
            drain(s)

        def wait_tail(j, c):
            fill(0).wait()
            return c
        lax.fori_loop(0, n_tail, wait_tail, 0)


def _dispatch(h2, tab_flat, lpos_t, fill_from, pad_end, *, n_rows):
    t, d = h2.shape
    assert d == ROW_TILE * LANES
    tile = DISPATCH_TILE
    n_local = tile * TOP_K
    grid_spec = pltpu.PrefetchScalarGridSpec(
        num_scalar_prefetch=2,
        grid=(t // tile,),
        in_specs=[pl.BlockSpec((TAB_WIDTH,), lambda i, *_: (i,), memory_space=pltpu.SMEM),
                  pl.BlockSpec((tile, d), lambda i, *_: (i, 0)),
                  pl.BlockSpec((1, TOP_K, tile), lambda i, *_: (i, 0, 0))],
        out_specs=pl.BlockSpec(memory_space=pl.ANY),
        scratch_shapes=[pltpu.VMEM((STAGE_SLOTS, n_local * ROW_TILE, LANES), jnp.float32),
                        pltpu.VMEM((FFN_TILE * ROW_TILE, LANES), jnp.float32),
                        pltpu.SMEM((STAGE_SLOTS,), jnp.int32),
                        pltpu.SemaphoreType.DMA((STAGE_SLOTS,)), pltpu.SemaphoreType.DMA(())],
    )
    return pl.pallas_call(
        _dispatch_kernel,
        grid_spec=grid_spec,
        out_shape=jax.ShapeDtypeStruct((n_rows * ROW_TILE, LANES), jnp.float32),
        compiler_params=_params("arbitrary"),
        name="dispatch",
    )(fill_from, pad_end, tab_flat, h2, lpos_t)


def _ffn_kernel(te_ref, rows_ref, n_used_ref, x_ref, w1_ref, b1_ref, w2_ref, b2_ref, y_ref,
                w1p_ref, w2b_ref, act_ref):
    i = pl.program_id(0)
    live = i < n_used_ref[0]
    n_pair = w1_ref.shape[2] // FFN_CHUNK
    half = FFN_CHUNK // 2
    d = w2_ref.shape[2]
    n_sub = jnp.where(live, (rows_ref[i] + FFN_TILE - 1) // FFN_TILE, 0)

    @pl.when(live & ((i == 0) | (te_ref[i] != te_ref[jnp.maximum(i - 1, 0)])))
    def _():
        src = lax.broadcasted_iota(jnp.int32, (FFN_CHUNK, FFN_CHUNK), 0)
        dst = lax.broadcasted_iota(jnp.int32, (FFN_CHUNK, FFN_CHUNK), 1)
        unzip = (src == jnp.where(dst < half, 2 * dst, 2 * (dst - half) + 1)).astype(jnp.bfloat16)
        for c in range(n_pair):
            cs = slice(c * FFN_CHUNK, (c + 1) * FFN_CHUNK)
            w1p_ref[:, cs] = _dot(w1_ref[0, :, cs].astype(jnp.bfloat16), unzip).astype(jnp.bfloat16)
        w2b_ref[...] = w2_ref[0].astype(jnp.bfloat16)

    def compute(n_rows):
        x = _from_tiles(x_ref, n_rows).astype(jnp.bfloat16)
        for c in range(0, n_pair, FFN_GROUP):
            cs = slice(c * FFN_CHUNK, (c + FFN_GROUP) * FFN_CHUNK)
            h = _dot(x, w1p_ref[:, cs]) + b1_ref[0, :, cs]
            g = jnp.concatenate([h[:, k * FFN_CHUNK:k * FFN_CHUNK + half] for k in range(FFN_GROUP)], axis=1)
            u = jnp.concatenate([h[:, k * FFN_CHUNK + half:(k + 1) * FFN_CHUNK] for k in range(FFN_GROUP)], axis=1)
            g = jnp.minimum(g, SWIGLU_LIMIT)
            u = jnp.clip(u, -SWIGLU_LIMIT, SWIGLU_LIMIT)
            act = g * jax.nn.sigmoid(SWIGLU_ALPHA * g) * (u + 1.0)
            act_ref[:n_rows, c * half:(c + FFN_GROUP) * half] = act.astype(act_ref.dtype)
        _to_tiles(_dot(act_ref[:n_rows, :], w2b_ref[...]) + b2_ref[0], y_ref)
        if n_rows * ROW_TILE < y_ref.shape[0]:
            y_ref[n_rows * ROW_TILE:, :] = jnp.zeros((y_ref.shape[0] - n_rows * ROW_TILE, LANES), y_ref.dtype)

    for tiles in range(1, FFN_STEP // FFN_TILE + 1):
        pl.when(n_sub == tiles)(functools.partial(compute, tiles * FFN_TILE))

    @pl.when(n_sub == 0)
    def _():
        y_ref[...] = jnp.zeros_like(y_ref)


def _unzip_bias(b1):
    e, f2 = b1.shape
    half = FFN_CHUNK // 2
    return b1.reshape(e, f2 // FFN_CHUNK, half, 2).transpose(0, 1, 3, 2).reshape(e, 1, f2)


def _ffn(xs, step_expert, step_rows, n_used, w1, b1, w2, b2):
    d, f2 = w1.shape[1:]
    n_steps = xs.shape[0] // (FFN_STEP * ROW_TILE)
    live = lambda i, nu: jnp.minimum(i, nu[0] - 1)
    wspec = lambda shape: pl.BlockSpec((1,) + shape, lambda i, te, nr, nu: (te[live(i, nu)], 0, 0))
    grid_spec = pltpu.PrefetchScalarGridSpec(
        num_scalar_prefetch=3,
        grid=(n_steps,),
        in_specs=[pl.BlockSpec((FFN_STEP * ROW_TILE, LANES), lambda i, te, nr, nu: (live(i, nu), 0)),
                  wspec((d, f2)), wspec((1, f2)), wspec((f2 // 2, d)), wspec((1, d))],
        out_specs=pl.BlockSpec((FFN_STEP * ROW_TILE, LANES), lambda i, te, nr, nu: (i, 0)),
        scratch_shapes=[pltpu.VMEM((d, f2), jnp.bfloat16), pltpu.VMEM((f2 // 2, d), jnp.bfloat16),
                        pltpu.VMEM((FFN_STEP, f2 // 2), jnp.bfloat16)],
    )
    return pl.pallas_call(
        _ffn_kernel,
        grid_spec=grid_spec,
        out_shape=jax.ShapeDtypeStruct(xs.shape, jnp.float32),
        compiler_params=_params("arbitrary"),
        name="expert_ffn",
    )(step_expert, step_rows, n_used, xs, w1, _unzip_bias(b1), w2, b2[:, None, :])


def _combine_kernel(tab_ref, next_tab_ref, lpos_ref, wts_ref, x1_ref, ga_ref, g_ref, y_ref, o_ref, buf_ref, sems):
    i = pl.program_id(0)
    n_steps = pl.num_programs(0)
    slot = i % 2
    tile = x1_ref.shape[0]
    n_local = buf_ref.shape[1] // ROW_TILE

    def fetch(tab, s):
        for e in range(N_EXPERTS):
            loc, glob, n = _piece_table(tab, e)

            @pl.when(n > 0)
            def _(loc=loc, glob=glob, n=n, e=e):
                pltpu.make_async_copy(y_ref.at[pl.ds(glob, n)], buf_ref.at[s, pl.ds(loc, n)],
                                      sems.at[s]).start(priority=e % 2)

    @pl.when(i == 0)
    def _():
        buf_ref[...] = jnp.zeros_like(buf_ref)
        fetch(tab_ref, 0)

    @pl.when(i + 1 < n_steps)
    def _():
        fetch(next_tab_ref, 1 - slot)

    n = pl.multiple_of(tab_ref[3 * N_EXPERTS], ROW_TILE)

    @pl.when(n > 0)
    def _():
        pltpu.make_async_copy(y_ref.at[pl.ds(0, n)], buf_ref.at[slot, pl.ds(0, n)], sems.at[slot]).wait()

    rows = _from_tiles(buf_ref, n_local, (slot,)).astype(jnp.bfloat16)
    part = tile // OUT_PARTS
    col = lax.broadcasted_iota(jnp.int32, (part, n_local), 1)
    for p in range(OUT_PARTS):
        r = slice(p * part, (p + 1) * part)
        pos = lpos_ref[r, :]
        w = wts_ref[r, :]
        pick = jnp.where(col == pos[:, 0:1], w[:, 0:1], 0.0)
        for k in range(1, TOP_K):
            pick = pick + jnp.where(col == pos[:, k:k + 1], w[:, k:k + 1], 0.0)
        hi = pick.astype(jnp.bfloat16)
        lo = (pick - hi.astype(jnp.float32)).astype(jnp.bfloat16)
        y = _dot(hi, rows) + _dot(lo, rows)
        o_ref[r, :] = x1_ref[r, :] + ga_ref[0] * _rms(y, g_ref[...])


def _combine(ys, tab_flat, lpos, wts, x1, mod3, g_post, *, seq):
    t, d = x1.shape
    tile = DISPATCH_TILE
    n_steps = t // tile
    per_seq = seq // tile
    n_local = tile * TOP_K
    return pl.pallas_call(
        _combine_kernel,
        grid=(n_steps,),
        in_specs=[pl.BlockSpec((TAB_WIDTH,), lambda i: (i,), memory_space=pltpu.SMEM),
                  pl.BlockSpec((TAB_WIDTH,), lambda i: (jnp.minimum(i + 1, n_steps - 1),), memory_space=pltpu.SMEM),
                  pl.BlockSpec((tile, TOP_K), lambda i: (i, 0)),
                  pl.BlockSpec((tile, TOP_K), lambda i: (i, 0)),
                  pl.BlockSpec((tile, d), lambda i: (i, 0)),
                  pl.BlockSpec((1, 1, d), lambda i: (i // per_seq, 0, 5)),
                  pl.BlockSpec((1, d), lambda i: (0, 0)),
                  pl.BlockSpec(memory_space=pl.ANY)],
        out_specs=pl.BlockSpec((tile, d), lambda i: (i, 0)),
        out_shape=jax.ShapeDtypeStruct((t, d), jnp.float32),
        scratch_shapes=[pltpu.VMEM((2, n_local * ROW_TILE, LANES), ys.dtype), pltpu.SemaphoreType.DMA((2,))],
        compiler_params=_params("arbitrary"),
        name="combine",
    )(tab_flat, tab_flat, lpos, wts, x1, mod3, g_post, ys)


def _rope_tables(seq):
    pos = np.arange(seq)
    n_freq = HEAD_DIM // 4
    freqs = ROPE_BASE ** (-jnp.arange(n_freq, dtype=jnp.float32) / n_freq)
    rows = jnp.asarray(pos // GRID_W, jnp.float32)[:, None] * freqs[None, :]
    cols = jnp.asarray(pos % GRID_W, jnp.float32)[:, None] * freqs[None, :]
    ang = jnp.concatenate([rows, rows, cols, cols], axis=1)
    sign = np.tile(np.repeat([-1.0, 1.0], n_freq), 2).astype(np.float32)
    cos = jnp.cos(ang)
    sin = jnp.sin(ang) * sign[None, :]
    reps = LANES // HEAD_DIM
    return jnp.tile(cos, (1, reps)), jnp.tile(sin, (1, reps))


def kernel(x, c, ctx, c_ctx, w_ada, b_ada, g_pre_mix, g_post_mix, g_pre_ffn, g_post_ffn, w_in, g_grp_a, g_grp_b,
           sink_a, rpb_b, w_out, w_router, b_router, w_mlp1, b_mlp1, w_mlp2, b_mlp2):
    batch, seq, d = x.shape
    n_ctx = ctx.shape[1]
    assert w_ada.shape[0] == 1, "single layer"
    assert seq % (ATT_BLOCKS * GRID_W * NB_Q_ROWS) == 0 and seq // GRID_W >= NB_K_ROWS
    assert seq % (ATT_BLOCKS * BLOCK_A) == 0 and seq >= BLOCK_A + 2 * WINDOW
    n_tok = batch * seq
    bf16 = jnp.bfloat16

    def pair_heads(a, axis):
        shape = a.shape
        a = a.reshape(shape[:axis] + (N_KV_A, GQA_GROUP, HEAD_DIM) + shape[axis + 1:])
        return jnp.swapaxes(a, axis, axis + 1).reshape(shape)

    mod_rows = -(-(batch + 1) // 8) * 8
    cc = jnp.zeros((mod_rows, d), jnp.float32).at[:batch].set(c).at[batch].set(c_ctx)
    mod3 = _ada(cc, w_ada[0], b_ada[0]).reshape(mod_rows, 1, 6 * d)

    w_in0 = w_in[0]
    w_lat = jnp.concatenate([pair_heads(w_in0[:, :WIDTH_A], 1), w_in0[:, WIDTH_A:]], axis=1).astype(bf16)
    kv_a_end = WIDTH_A + 2 * WIDTH_KV_A
    w_ctx = jnp.concatenate([w_in0[:, WIDTH_A:kv_a_end], w_in0[:, kv_a_end + WIDTH_B:]], axis=1).astype(bf16)
    g_pre = g_pre_mix[0].reshape(1, d)
    qa, ka, va, qb, kb, vb = _inproj(x.reshape(n_tok, d), mod3, g_pre, w_lat, _rope_tables(seq),
                                     seq=seq, mod_row0=0, latent=True)
    kac, vac, kbc, vbc = _inproj(ctx.reshape(batch * n_ctx, d), mod3, g_pre, w_ctx, None,
                                 seq=n_ctx, mod_row0=batch, latent=False)

    oa = _window_attention(qa, ka.reshape(batch, seq, -1), va.reshape(batch, seq, -1),
                           kac.reshape(batch, n_ctx, -1), vac.reshape(batch, n_ctx, -1),
                           sink_a[0].astype(jnp.float32) * LOG2E, batch=batch, seq=seq)
    tabs, cls_of_block = _nb_bias_tables(rpb_b[0], seq // GRID_W)
    ob = _neighbourhood_attention(qb, kb.reshape(batch, seq, -1), vb.reshape(batch, seq, -1),
                                  kbc.reshape(batch, n_ctx, -1), vbc.reshape(batch, n_ctx, -1),
                                  tabs, cls_of_block, batch=batch, seq=seq)

    w_out0 = w_out[0]
    w_o = jnp.concatenate([pair_heads(w_out0[:WIDTH_A], 0), w_out0[WIDTH_A:]], axis=0).astype(bf16)
    x1, h2, idx, wts, rank, size, base, counts = _outproj(
        oa, ob, x.reshape(n_tok, d), mod3,
        pair_heads(g_grp_a[0], 0).reshape(1, -1), g_grp_b[0].reshape(1, -1),
        g_post_mix[0].reshape(1, d), g_pre_ffn[0].reshape(1, d),
        w_o, w_router[0], b_router[0].reshape(1, -1), seq=seq)

    n_tok_tiles = n_tok // DISPATCH_TILE
    size = size.reshape(n_tok_tiles, N_EXPERTS)
    counts = counts.reshape(-1)
    padded = (counts + FFN_STEP - 1) // FFN_STEP * FFN_STEP
    pad_end = jnp.cumsum(padded).astype(jnp.int32)
    pad_start = pad_end - padded
    n_steps = n_tok * TOP_K // FFN_STEP + N_EXPERTS
    n_used = (pad_end[-1:] // FFN_STEP).astype(jnp.int32)
    step_row0 = jnp.arange(n_steps, dtype=jnp.int32) * FFN_STEP
    step_expert = jnp.minimum(jnp.sum(step_row0[:, None] >= pad_end[None, :], axis=1),
                              N_EXPERTS - 1).astype(jnp.int32)
    own = step_expert[:, None] == jnp.arange(N_EXPERTS, dtype=jnp.int32)
    step_rows = jnp.clip(jnp.sum(jnp.where(own, (pad_start + counts)[None, :], 0), axis=1) - step_row0,
                         0, FFN_STEP).astype(jnp.int32)
    fill_from = (pad_start + jnp.maximum((counts + FFN_TILE - 1) // FFN_TILE * FFN_TILE - FFN_TILE, 0)
                 ).astype(jnp.int32)
    local0 = jnp.cumsum(size, axis=1) - size
    global0 = pad_start[None, :] + base.reshape(n_tok_tiles, N_EXPERTS)
    tab = jnp.concatenate([local0, global0, size, jnp.sum(size, axis=1, keepdims=True),
                           jnp.zeros((n_tok_tiles, TAB_WIDTH - 3 * N_EXPERTS - 1), jnp.int32)], axis=1)
    tab = (tab * ROW_TILE).reshape(-1).astype(jnp.int32)
    hot = idx[:, :, None, :] == jnp.arange(N_EXPERTS, dtype=jnp.int32)[None, None, :, None]
    lpos_t = rank + jnp.sum(jnp.where(hot, local0[:, None, :, None], 0), axis=2)
    by_token = lambda a: jnp.swapaxes(a, 1, 2).reshape(n_tok, TOP_K)

    xs = _dispatch(h2, tab, lpos_t, fill_from * ROW_TILE, pad_end * ROW_TILE, n_rows=n_steps * FFN_STEP)
    ys = _ffn(xs, step_expert, step_rows, n_used, w_mlp1[0], b_mlp1[0], w_mlp2[0], b_mlp2[0])
    out = _combine(ys, tab, by_token(lpos_t), by_token(wts), x1, mod3, g_post_ffn[0].reshape(1, d), seq=seq)
    return out.reshape(batch, seq, d)
```

```python
import functools

import numpy as np
import jax
import jax.numpy as jnp
from jax import lax
from jax.experimental import pallas as pl
from jax.experimental.pallas import tpu as pltpu

GRID_W = 64
HEAD_DIM = 64
N_HEADS_A = 8
N_KV_A = 2
GQA_GROUP = N_HEADS_A // N_KV_A
N_HEADS_B = 8
WIDTH_A = N_HEADS_A * HEAD_DIM
WIDTH_KV_A = N_KV_A * HEAD_DIM
WIDTH_B = N_HEADS_B * HEAD_DIM
WINDOW = 128
BLOCK_A = 128
NA_ROWS = 8
NA_COLS = 16
N_EXPERTS = 32
TOP_K = 4
SWIGLU_LIMIT = 7.0
SWIGLU_ALPHA = 1.702
ROPE_BASE = 10000.0
EPS = 1e-6
NEG_INF = -1e30

LANES = 128
VMEM_LIMIT = 56 * 1024 * 1024

LOG2E = 1.4426950408889634
ATT_BLOCKS = 4
NB_Q_ROWS = 2
NB_K_ROWS = NB_Q_ROWS + NA_ROWS
PROJ_TILE = 1024
OUT_TILE = 256
OUT_PARTS = 2
STAGE_SLOTS = 3
FFN_TILE = 256
FFN_STEP = 4 * FFN_TILE
FFN_CHUNK = 256
FFN_GROUP = 4
DISPATCH_TILE = OUT_TILE
ROW_TILE = 8
TAB_WIDTH = 128


def _params(*sem):
    return pltpu.CompilerParams(dimension_semantics=sem, vmem_limit_bytes=VMEM_LIMIT)


def _rms(x, g):
    return x * lax.rsqrt(jnp.mean(x * x, axis=-1, keepdims=True) + EPS) * g


def _dot(a, b):
    return jnp.dot(a, b, preferred_element_type=jnp.float32)


def _dot_nt(a, b):
    return lax.dot_general(a, b, (((1,), (1,)), ((), ())), preferred_element_type=jnp.float32)


def _ada_kernel(c_ref, w_ref, b_ref, o_ref):
    c = c_ref[...]
    s = (c * jax.nn.sigmoid(c)).astype(jnp.bfloat16)
    o_ref[...] = _dot(s, w_ref[...].astype(jnp.bfloat16)) + b_ref[...]


def _ada(cc, w_ada, b_ada):
    rows, d = cc.shape
    n_out = w_ada.shape[1]
    return pl.pallas_call(
        _ada_kernel,
        grid=(n_out // d,),
        in_specs=[pl.BlockSpec((rows, d), lambda j: (0, 0)),
                  pl.BlockSpec((d, d), lambda j: (0, j)),
                  pl.BlockSpec((1, d), lambda j: (0, j))],
        out_specs=pl.BlockSpec((rows, d), lambda j: (0, j)),
        out_shape=jax.ShapeDtypeStruct((rows, n_out), jnp.float32),
        compiler_params=_params("arbitrary"),
        name="ada",
    )(cc, w_ada, b_ada.reshape(1, n_out))


def _rope(x, cos, sin):
    w = x.shape[1]
    reps = w // LANES
    if reps > 1:
        cos = jnp.concatenate([cos] * reps, axis=1)
        sin = jnp.concatenate([sin] * reps, axis=1)
    lane = lax.broadcasted_iota(jnp.int32, x.shape, 1)
    quarter = HEAD_DIM // 4
    partner = jnp.where(lane % (2 * quarter) < quarter,
                        pltpu.roll(x, w - quarter, 1), pltpu.roll(x, quarter, 1))
    return x * cos + partner * sin


def _inproj_kernel(x_ref, sh_ref, sc_ref, g_ref, w_ref, *rest, latent):
    x = x_ref[...]
    h = _rms(x, g_ref[...]) * (1.0 + sc_ref[0]) + sh_ref[0]
    p = _dot(h.astype(jnp.bfloat16), w_ref[...])
    if latent:
        cos_ref, sin_ref, qa_ref, ka_ref, va_ref, qb_ref, kb_ref, vb_ref = rest
        cos, sin = cos_ref[...], sin_ref[...]
        scale = HEAD_DIM ** -0.5 * LOG2E
        o = 0
        qa_ref[...] = (_rope(p[:, o:o + WIDTH_A], cos, sin) * scale).astype(qa_ref.dtype)
        o += WIDTH_A
        ka_ref[...] = _rope(p[:, o:o + WIDTH_KV_A], cos, sin).astype(ka_ref.dtype)
        o += WIDTH_KV_A
        va_ref[...] = p[:, o:o + WIDTH_KV_A].astype(va_ref.dtype)
        o += WIDTH_KV_A
        qb_ref[...] = (p[:, o:o + WIDTH_B] * scale).astype(qb_ref.dtype)
        o += WIDTH_B
    else:
        ka_ref, va_ref, kb_ref, vb_ref = rest
        o = 0
        ka_ref[...] = p[:, o:o + WIDTH_KV_A].astype(ka_ref.dtype)
        o += WIDTH_KV_A
        va_ref[...] = p[:, o:o + WIDTH_KV_A].astype(va_ref.dtype)
        o += WIDTH_KV_A
    kb_ref[...] = p[:, o:o + WIDTH_B].astype(kb_ref.dtype)
    o += WIDTH_B
    vb_ref[...] = p[:, o:o + WIDTH_B].astype(vb_ref.dtype)


def _inproj(x2d, mod3, g_pre, w, rope_tabs, *, seq, mod_row0, latent):
    t, d = x2d.shape
    tile = min(PROJ_TILE, seq)
    per_seq = seq // tile
    if latent:
        mod_row = lambda i: i // per_seq
    else:
        mod_row = lambda i: mod_row0
    in_specs = [pl.BlockSpec((tile, d), lambda i: (i, 0)),
                pl.BlockSpec((1, 1, d), lambda i: (mod_row(i), 0, 0)),
                pl.BlockSpec((1, 1, d), lambda i: (mod_row(i), 0, 1)),
                pl.BlockSpec((1, d), lambda i: (0, 0)),
                pl.BlockSpec(w.shape, lambda i: (0, 0))]
    args = [x2d, mod3, mod3, g_pre, w]
    widths = [WIDTH_KV_A, WIDTH_KV_A, WIDTH_B, WIDTH_B]
    if latent:
        in_specs += [pl.BlockSpec((tile, LANES), lambda i: (i % per_seq, 0))] * 2
        args += list(rope_tabs)
        widths = [WIDTH_A, WIDTH_KV_A, WIDTH_KV_A, WIDTH_B, WIDTH_B, WIDTH_B]
    return pl.pallas_call(
        functools.partial(_inproj_kernel, latent=latent),
        grid=(t // tile,),
        in_specs=in_specs,
        out_specs=[pl.BlockSpec((tile, wd), lambda i: (i, 0)) for wd in widths],
        out_shape=[jax.ShapeDtypeStruct((t, wd), jnp.bfloat16) for wd in widths],
        compiler_params=_params("parallel"),
        name="inproj_latent" if latent else "inproj_ctx",
    )(*args)


def _ones_beside(v):
    return jnp.concatenate([v, jnp.ones_like(v)], axis=1)


def _group_attention(q_ref, o_ref, rows, groups, s_ref, p_ref, m_ref):
    tq = rows.stop - rows.start
    low = lax.broadcasted_iota(jnp.int32, (tq, LANES), 1) < HEAD_DIM
    layout = []
    base = 0
    for pairs, k_loc, k_ctx, _, _, _, _ in groups:
        heads = [(j, 0) for j in pairs] + [(j, 1) for j in pairs]
        n_loc = k_loc.shape[0]
        stacked = []
        for j, half in heads:
            q = q_ref[rows, j * LANES:(j + 1) * LANES]
            stacked.append(jnp.where(low if half == 0 else ~low, q, jnp.zeros_like(q)))
        qs = jnp.concatenate(stacked, axis=0)
        n = len(heads) * tq
        s_ref[base:base + n, :n_loc] = _dot_nt(qs, k_loc)
        s_ref[base:base + n, n_loc:] = _dot_nt(qs, k_ctx)
        layout.append((base, heads, n_loc))
        base += n
    for (base, heads, n_loc), group in zip(layout, groups):
        bias_of, sink_of = group[5], group[6]
        for g, (j, half) in enumerate(heads):
            r = slice(base + g * tq, base + (g + 1) * tq)
            s_loc = s_ref[r, :n_loc] + bias_of(j, half)
            s_ctx = s_ref[r, n_loc:]
            m = jnp.maximum(jnp.max(s_loc, axis=1, keepdims=True), jnp.max(s_ctx, axis=1, keepdims=True))
            sink = sink_of(j, half)
            if sink is not None:
                m = jnp.maximum(m, sink)
            p_ref[r, :n_loc] = jnp.exp2((s_loc - m).astype(jnp.bfloat16))
            p_ref[r, n_loc:] = jnp.exp2((s_ctx - m).astype(jnp.bfloat16))
            m_ref[r, :] = m
    for (base, heads, n_loc), group in zip(layout, groups):
        pairs, _, _, v_loc, v_ctx, _, sink_of = group
        n = len(heads) * tq
        both = (_dot(p_ref[base:base + n, :n_loc], _ones_beside(v_loc))
                + _dot(p_ref[base:base + n, n_loc:], _ones_beside(v_ctx)))
        for a, j in enumerate(pairs):
            outs = []
            for half in range(2):
                g = half * len(pairs) + a
                r = slice(g * tq, (g + 1) * tq)
                den = both[r, LANES:]
                sink = sink_of(j, half)
                if sink is not None:
                    den = den + jnp.exp2(sink - m_ref[base + g * tq:base + (g + 1) * tq, :])
                outs.append(both[r, :LANES] / den)
            o_ref[rows, j * LANES:(j + 1) * LANES] = jnp.where(low, outs[0], outs[1]).astype(o_ref.dtype)


def _attention_scratch(tq, n_heads, n_keys):
    return [pltpu.VMEM((n_heads * tq, n_keys), jnp.float32), pltpu.VMEM((n_heads * tq, n_keys), jnp.bfloat16),
            pltpu.VMEM((n_heads * tq, 1), jnp.float32)]


def _win_kernel(sink_ref, q_ref, k_ref, v_ref, kc_ref, vc_ref, o_ref, *scratch, seq):
    span = BLOCK_A + 2 * WINDOW
    pairs = list(range(GQA_GROUP))
    for sb in range(ATT_BLOCKS):
        i = pl.program_id(1) * ATT_BLOCKS + sb
        start = pl.multiple_of(jnp.clip(i * BLOCK_A - WINDOW, 0, seq - span), BLOCK_A)
        qpos = i * BLOCK_A + lax.broadcasted_iota(jnp.int32, (BLOCK_A, span), 0)
        kpos = start + lax.broadcasted_iota(jnp.int32, (BLOCK_A, span), 1)
        bias = jnp.where(jnp.abs(kpos - qpos) <= WINDOW, 0.0, NEG_INF).astype(jnp.float32)
        groups = [([j], k_ref[0, pl.ds(start, span), :], kc_ref[0], v_ref[0, pl.ds(start, span), :], vc_ref[0],
                   lambda j, half: bias, lambda j, half: sink_ref[half * GQA_GROUP + j]) for j in pairs]
        _group_attention(q_ref, o_ref, slice(sb * BLOCK_A, (sb + 1) * BLOCK_A), groups, *scratch)


def _window_attention(qa, ka, va, kac, vac, sink, *, batch, seq):
    tq = ATT_BLOCKS * BLOCK_A
    nb = seq // tq
    n_ctx = kac.shape[1]
    return pl.pallas_call(
        functools.partial(_win_kernel, seq=seq),
        grid=(batch, nb),
        in_specs=[pl.BlockSpec(memory_space=pltpu.SMEM),
                  pl.BlockSpec((tq, WIDTH_A), lambda b, i: (b * nb + i, 0)),
                  pl.BlockSpec((1, seq, WIDTH_KV_A), lambda b, i: (b, 0, 0)),
                  pl.BlockSpec((1, seq, WIDTH_KV_A), lambda b, i: (b, 0, 0)),
                  pl.BlockSpec((1, n_ctx, WIDTH_KV_A), lambda b, i: (b, 0, 0)),
                  pl.BlockSpec((1, n_ctx, WIDTH_KV_A), lambda b, i: (b, 0, 0))],
        out_specs=pl.BlockSpec((tq, WIDTH_A), lambda b, i: (b * nb + i, 0)),
        out_shape=jax.ShapeDtypeStruct((batch * seq, WIDTH_A), jnp.bfloat16),
        scratch_shapes=_attention_scratch(BLOCK_A, N_HEADS_A, BLOCK_A + 2 * WINDOW + n_ctx),
        compiler_params=_params("parallel", "arbitrary"),
        name="window_attention",
    )(sink, qa, ka, va, kac, vac)


def _nb_kernel(q_ref, k_ref, v_ref, kc_ref, vc_ref, *rest, rows_n):
    tab_refs, o_ref, scratch = rest[:ATT_BLOCKS], rest[ATT_BLOCKS], rest[ATT_BLOCKS + 1:]
    n_keys = NB_K_ROWS * GRID_W
    tq = NB_Q_ROWS * GRID_W
    for sb in range(ATT_BLOCKS):
        m = pl.program_id(1) * ATT_BLOCKS + sb
        start_row = jnp.clip(NB_Q_ROWS * m - NA_ROWS // 2, 0, rows_n - NB_K_ROWS)
        start = pl.multiple_of(start_row * GRID_W, LANES)
        tab_ref = tab_refs[sb]
        groups = []
        for j in range(q_ref.shape[1] // LANES):
            cols = slice(j * LANES, (j + 1) * LANES)
            groups.append(([j], k_ref[0, pl.ds(start, n_keys), cols], kc_ref[0, :, cols],
                           v_ref[0, pl.ds(start, n_keys), cols], vc_ref[0, :, cols],
                           lambda j, half: tab_ref[0, 2 * j + half].astype(jnp.float32), lambda j, half: None))
        _group_attention(q_ref, o_ref, slice(sb * tq, (sb + 1) * tq), groups, *scratch)


def _nb_classes(rows_n):
    n_blocks = rows_n // NB_Q_ROWS
    sig = {}
    cls_of_block = []
    reps = []
    for m in range(n_blocks):
        start_row = int(np.clip(NB_Q_ROWS * m - NA_ROWS // 2, 0, rows_n - NB_K_ROWS))
        key = tuple((start_row - r, int(np.clip(r - NA_ROWS // 2, 0, rows_n - NA_ROWS)) - r)
                    for r in range(NB_Q_ROWS * m, NB_Q_ROWS * (m + 1)))
        if key not in sig:
            sig[key] = len(reps)
            reps.append(m)
        cls_of_block.append(sig[key])
    return np.asarray(cls_of_block, np.int32), reps


def _nb_bias_tables(rpb, rows_n):
    cls_of_block, reps = _nb_classes(rows_n)
    n_heads = rpb.shape[0]
    cq = np.arange(GRID_W)[:, None]
    ck = np.arange(GRID_W)[None, :]
    cs = np.clip(cq - NA_COLS // 2, 0, GRID_W - NA_COLS)
    col_ok = (ck >= cs) & (ck < cs + NA_COLS)
    pick = ((ck - cq + NA_COLS - 1)[None] == np.arange(2 * NA_COLS - 1)[:, None, None]) & col_ok[None]
    tiles = jnp.einsum('hrd,dqk->hrqk', rpb.astype(jnp.float32), jnp.asarray(pick, jnp.float32),
                       precision=lax.Precision.HIGHEST)
    tiles = jnp.where(jnp.asarray(col_ok)[None, None], tiles * LOG2E, NEG_INF).astype(jnp.bfloat16)
    blocked = jnp.full((n_heads, GRID_W, GRID_W), NEG_INF, jnp.bfloat16)
    tabs = []
    for m in reps:
        start_row = int(np.clip(NB_Q_ROWS * m - NA_ROWS // 2, 0, rows_n - NB_K_ROWS))
        q_rows = []
        for r in range(NB_Q_ROWS * m, NB_Q_ROWS * (m + 1)):
            rs = int(np.clip(r - NA_ROWS // 2, 0, rows_n - NA_ROWS))
            q_rows.append(jnp.concatenate(
                [tiles[:, krow - r + NA_ROWS - 1] if rs <= krow < rs + NA_ROWS else blocked
                 for krow in range(start_row, start_row + NB_K_ROWS)], axis=2))
        tabs.append(jnp.concatenate(q_rows, axis=1))
    return jnp.stack(tabs), cls_of_block


def _neighbourhood_attention(qb, kb, vb, kbc, vbc, tabs, cls_of_block, *, batch, seq):
    rows_n = seq // GRID_W
    tq = ATT_BLOCKS * NB_Q_ROWS * GRID_W
    nstep = seq // tq
    n_ctx = kbc.shape[1]

    def tab_spec(sb):
        return pl.BlockSpec((1,) + tabs.shape[1:], lambda b, m, c: (c[m * ATT_BLOCKS + sb], 0, 0, 0))

    grid_spec = pltpu.PrefetchScalarGridSpec(
        num_scalar_prefetch=1,
        grid=(batch, nstep),
        in_specs=[pl.BlockSpec((tq, WIDTH_B), lambda b, m, c: (b * nstep + m, 0)),
                  pl.BlockSpec((1, seq, WIDTH_B), lambda b, m, c: (b, 0, 0)),
                  pl.BlockSpec((1, seq, WIDTH_B), lambda b, m, c: (b, 0, 0)),
                  pl.BlockSpec((1, n_ctx, WIDTH_B), lambda b, m, c: (b, 0, 0)),
                  pl.BlockSpec((1, n_ctx, WIDTH_B), lambda b, m, c: (b, 0, 0))]
                 + [tab_spec(sb) for sb in range(ATT_BLOCKS)],
        out_specs=pl.BlockSpec((tq, WIDTH_B), lambda b, m, c: (b * nstep + m, 0)),
        scratch_shapes=_attention_scratch(NB_Q_ROWS * GRID_W, N_HEADS_B, NB_K_ROWS * GRID_W + n_ctx),
    )

    def body(c_ref, *refs):
        _nb_kernel(*refs, rows_n=rows_n)

    return pl.pallas_call(
        body,
        grid_spec=grid_spec,
        out_shape=jax.ShapeDtypeStruct((batch * seq, WIDTH_B), jnp.bfloat16),
        compiler_params=_params("parallel", "arbitrary"),
        name="neighbourhood_attention",
    )(jnp.asarray(cls_of_block), qb, kb, vb, kbc, vbc, *([tabs] * ATT_BLOCKS))


def _outproj_kernel(oa_ref, ob_ref, x_ref, ga_ref, sh_ref, sc_ref, gga_ref, ggb_ref, gpost_ref, gpre_ref,
                    wo_ref, wr_hi_ref, wr_lo_ref, br_ref,
                    x1_ref, h2_ref, idx_ref, wts_ref, rank_ref, size_ref, base_ref, cnt_ref, run_ref):
    i = pl.program_id(0)

    @pl.when(i == 0)
    def _():
        run_ref[...] = jnp.zeros_like(run_ref)

    n_rows = x_ref.shape[0]
    parts = [slice(p * n_rows // OUT_PARTS, (p + 1) * n_rows // OUT_PARTS) for p in range(OUT_PARTS)]
    na = [_rms(oa_ref[r, :].astype(jnp.float32), gga_ref[...]).astype(jnp.bfloat16) for r in parts]
    nb = [_rms(ob_ref[r, :].astype(jnp.float32), ggb_ref[...]).astype(jnp.bfloat16) for r in parts]
    mix = [_dot(a, wo_ref[:WIDTH_A, :]) + _dot(b, wo_ref[WIDTH_A:, :]) for a, b in zip(na, nb)]
    x1 = [x_ref[r, :] + ga_ref[0] * _rms(m, gpost_ref[...]) for r, m in zip(parts, mix)]
    for r, v in zip(parts, x1):
        x1_ref[r, :] = v
    h2 = [_rms(v, gpre_ref[...]) * (1.0 + sc_ref[0]) + sh_ref[0] for v in x1]
    for r, v in zip(parts, h2):
        h2_ref[r, :] = v.astype(h2_ref.dtype)

    h_hi = [v.astype(jnp.bfloat16) for v in h2]
    h_lo = [(v - hi.astype(jnp.float32)).astype(jnp.bfloat16) for v, hi in zip(h2, h_hi)]
    logits = jnp.concatenate([_dot(hi, wr_hi_ref[...]) + _dot(lo, wr_hi_ref[...]) + _dot(hi, wr_lo_ref[...])
                              for hi, lo in zip(h_hi, h_lo)], axis=0)
    work = logits.T[:N_EXPERTS] + br_ref[...]
    e, t = work.shape
    row = lax.broadcasted_iota(jnp.int32, (e, t), 0)
    chosen = jnp.zeros((e, t), jnp.float32)
    vals, sels, hots = [], [], []
    for _k in range(TOP_K):
        mx = jnp.max(work, axis=0, keepdims=True)
        sel = jnp.min(jnp.where(work == mx, row, e), axis=0, keepdims=True)
        hot = row == sel
        vals.append(mx)
        sels.append(sel)
        hots.append(hot)
        work = jnp.where(hot, -jnp.inf, work)
        chosen = chosen + hot.astype(jnp.float32)
    ex = [jnp.exp(v - vals[0]) for v in vals]
    den = ex[0] + ex[1] + ex[2] + ex[3]

    earlier = (lax.broadcasted_iota(jnp.int32, (t, t), 0) < lax.broadcasted_iota(jnp.int32, (t, t), 1))
    prefix = _dot(chosen.astype(jnp.bfloat16), earlier.astype(jnp.bfloat16))
    ranks = [jnp.sum(jnp.where(h, prefix, 0.0), axis=0, keepdims=True) for h in hots]
    size = jnp.sum(chosen, axis=1, keepdims=True).astype(jnp.int32)
    size_ref[0] = size
    base_ref[0] = run_ref[...]
    run_ref[...] = run_ref[...] + size
    cnt_ref[...] = run_ref[...]

    idx_ref[0] = jnp.concatenate(sels, axis=0)
    wts_ref[0] = jnp.concatenate([v / den for v in ex], axis=0)
    rank_ref[0] = jnp.concatenate(ranks, axis=0).astype(jnp.int32)


def _outproj(oa, ob, x2d, mod3, gga, ggb, gpost, gpre, w_out, w_router, b_router, *, seq):
    t, d = x2d.shape
    tile = OUT_TILE
    per_seq = seq // tile
    row = lambda i: (i, 0)
    const = lambda i: (0, 0)
    modspec = lambda part: pl.BlockSpec((1, 1, d), lambda i: (i // per_seq, 0, part))
    per_tile = lambda shape: pl.BlockSpec((1,) + shape, lambda i: (i, 0, 0))
    n_steps = t // tile
    w_pad = jnp.zeros((d, LANES), jnp.float32).at[:, :N_EXPERTS].set(w_router)
    wr_hi = w_pad.astype(jnp.bfloat16)
    wr_lo = (w_pad - wr_hi.astype(jnp.float32)).astype(jnp.bfloat16)
    outs = pl.pallas_call(
        _outproj_kernel,
        grid=(n_steps,),
        in_specs=[pl.BlockSpec((tile, WIDTH_A), row), pl.BlockSpec((tile, WIDTH_B), row),
                  pl.BlockSpec((tile, d), row),
                  modspec(2), modspec(3), modspec(4),
                  pl.BlockSpec((1, WIDTH_A), const), pl.BlockSpec((1, WIDTH_B), const),
                  pl.BlockSpec((1, d), const), pl.BlockSpec((1, d), const),
                  pl.BlockSpec(w_out.shape, const), pl.BlockSpec((d, LANES), const), pl.BlockSpec((d, LANES), const),
                  pl.BlockSpec((N_EXPERTS, 1), const)],
        out_specs=[pl.BlockSpec((tile, d), row), pl.BlockSpec((tile, d), row),
                   per_tile((TOP_K, tile)), per_tile((TOP_K, tile)), per_tile((TOP_K, tile)),
                   per_tile((N_EXPERTS, 1)), per_tile((N_EXPERTS, 1)),
                   pl.BlockSpec((N_EXPERTS, 1), const)],
        out_shape=[jax.ShapeDtypeStruct((t, d), jnp.float32), jax.ShapeDtypeStruct((t, d), jnp.bfloat16),
                   jax.ShapeDtypeStruct((n_steps, TOP_K, tile), jnp.int32),
                   jax.ShapeDtypeStruct((n_steps, TOP_K, tile), jnp.float32),
                   jax.ShapeDtypeStruct((n_steps, TOP_K, tile), jnp.int32),
                   jax.ShapeDtypeStruct((n_steps, N_EXPERTS, 1), jnp.int32),
                   jax.ShapeDtypeStruct((n_steps, N_EXPERTS, 1), jnp.int32),
                   jax.ShapeDtypeStruct((N_EXPERTS, 1), jnp.int32)],
        scratch_shapes=[pltpu.VMEM((N_EXPERTS, 1), jnp.int32)],
        compiler_params=_params("arbitrary"),
        name="outproj_router",
    )(oa, ob, x2d, mod3, mod3, mod3, gga, ggb, gpost, gpre, w_out, wr_hi, wr_lo, b_router.reshape(N_EXPERTS, 1))
    return outs


def _piece_table(tab_ref, e):
    return (pl.multiple_of(tab_ref[e], ROW_TILE), pl.multiple_of(tab_ref[N_EXPERTS + e], ROW_TILE),
            pl.multiple_of(tab_ref[2 * N_EXPERTS + e], ROW_TILE))


def _to_tiles(x, ref, index=()):
    for c in range(ROW_TILE):
        ref[index + (pl.ds(c, x.shape[0], stride=ROW_TILE), slice(None))] = x[:, c * LANES:(c + 1) * LANES]


def _from_tiles(ref, n_rows, index=()):
    return jnp.concatenate([ref[index + (pl.ds(c, n_rows, stride=ROW_TILE), slice(None))]
                            for c in range(ROW_TILE)], axis=1)


def _dispatch_kernel(fill_from_ref, pad_end_ref, tab_ref, h_ref, lpos_ref, xs_ref,
                     stage_ref, zero_ref, pend_ref, sems, zsem, *, grid_steps):
    i = pl.program_id(0)
    n_steps = pl.num_programs(0)
    n_slots = stage_ref.shape[0]
    slot = i % n_slots
    n_local = stage_ref.shape[1] // ROW_TILE
    fill_rows = zero_ref.shape[0]

    def fill(row):
        return pltpu.make_async_copy(zero_ref.at[pl.ds(0, fill_rows)],
                                     xs_ref.at[pl.ds(pl.multiple_of(row, fill_rows), fill_rows)], zsem)

    def pad_fill(e):
        n = pl.multiple_of(pad_end_ref[e] - fill_from_ref[e], ROW_TILE)
        return n, pltpu.make_async_copy(zero_ref.at[pl.ds(0, n)],
                                        xs_ref.at[pl.ds(pl.multiple_of(fill_from_ref[e], ROW_TILE), n)], zsem)

    def drain(s):
        n = pl.multiple_of(pend_ref[s], ROW_TILE)

        @pl.when(n > 0)
        def _():
            pltpu.make_async_copy(stage_ref.at[s, pl.ds(0, n)], xs_ref.at[pl.ds(0, n)], sems.at[s]).wait()
        pend_ref[s] = 0

    @pl.when(i == 0)
    def _():
        zero_ref[...] = jnp.zeros_like(zero_ref)
        for s in range(n_slots):
            pend_ref[s] = 0

    for j in range(-(-N_EXPERTS // grid_steps)):
        e = i * -(-N_EXPERTS // grid_steps) + j

        @pl.when(e < N_EXPERTS)
        def _(e=e):
            n, copy = pad_fill(e)

            @pl.when(n > 0)
            def _():
                copy.start()

    row = lax.broadcasted_iota(jnp.int32, (n_local, h_ref.shape[0]), 0)
    place = row == lpos_ref[0, 0:1, :]
    for k in range(1, TOP_K):
        place = place | (row == lpos_ref[0, k:k + 1, :])
    rows = _dot(place.astype(jnp.bfloat16), h_ref[...])

    drain(slot)
    _to_tiles(rows, stage_ref, (slot,))

    total = 0
    for e in range(N_EXPERTS):
        loc, glob, n = _piece_table(tab_ref, e)

        @pl.when(n > 0)
        def _(loc=loc, glob=glob, n=n, e=e):
            pltpu.make_async_copy(stage_ref.at[slot, pl.ds(loc, n)], xs_ref.at[pl.ds(glob, n)],
                                  sems.at[slot]).start(priority=e % 2)
        total = total + n
    pend_ref[slot] = total

    tail0 = pad_end_ref[N_EXPERTS - 1]
    n_tail = (xs_ref.shape[0] - tail0) // fill_rows
    per_step = (n_tail + n_steps - 1) // n_steps

    def start_tail(j, c):
        t = i * per_step + j

        @pl.when(t < n_tail)
        def _():
            fill(tail0 + t * fill_rows).start()
        return c
    lax.fori_loop(0, per_step, start_tail, 0)

    @pl.when(i == n_steps - 1)
    def _():
        for s in range(n_slots):
            drain(s)

        def wait_tail(j, c):
            fill(0).wait()
            return c
        lax.fori_loop(0, n_tail, wait_tail, 0)

        def wait_pad(e, c):
            n, copy = pad_fill(e)

            @pl.when(n > 0)
            def _():
                copy.wait()
            return c
        lax.fori_loop(0, N_EXPERTS, wait_pad, 0)


def _dispatch(h2, tab_flat, lpos_t, fill_from, pad_end, *, n_rows):
    t, d = h2.shape
    tile = DISPATCH_TILE
    assert d == ROW_TILE * LANES
    n_local = tile * TOP_K
    grid_spec = pltpu.PrefetchScalarGridSpec(
        num_scalar_prefetch=2,
        grid=(t // tile,),
        in_specs=[pl.BlockSpec((TAB_WIDTH,), lambda i, *_: (i,), memory_space=pltpu.SMEM),
                  pl.BlockSpec((tile, d), lambda i, *_: (i, 0)),
                  pl.BlockSpec((1, TOP_K, tile), lambda i, *_: (i, 0, 0))],
        out_specs=pl.BlockSpec(memory_space=pl.ANY),
        scratch_shapes=[pltpu.VMEM((STAGE_SLOTS, n_local * ROW_TILE, LANES), jnp.float32),
                        pltpu.VMEM((FFN_STEP * ROW_TILE, LANES), jnp.float32),
                        pltpu.SMEM((STAGE_SLOTS,), jnp.int32),
                        pltpu.SemaphoreType.DMA((STAGE_SLOTS,)), pltpu.SemaphoreType.DMA(())],
    )
    return pl.pallas_call(
        functools.partial(_dispatch_kernel, grid_steps=t // tile),
        grid_spec=grid_spec,
        out_shape=jax.ShapeDtypeStruct((n_rows * ROW_TILE, LANES), jnp.float32),
        compiler_params=_params("arbitrary"),
        name="dispatch",
    )(fill_from, pad_end, tab_flat, h2, lpos_t)


def _ffn_kernel(te_ref, rows_ref, n_used_ref, x_ref, w1_ref, b1_ref, w2_ref, b2_ref, y_ref,
                w1p_ref, w2b_ref, act_ref):
    i = pl.program_id(0)
    live = i < n_used_ref[0]
    n_pair = w1_ref.shape[2] // FFN_CHUNK
    half = FFN_CHUNK // 2
    n_sub = jnp.where(live, (rows_ref[i] + FFN_TILE - 1) // FFN_TILE, 0)

    @pl.when(live & ((i == 0) | (te_ref[i] != te_ref[jnp.maximum(i - 1, 0)])))
    def _():
        src = lax.broadcasted_iota(jnp.int32, (FFN_CHUNK, FFN_CHUNK), 0)
        dst = lax.broadcasted_iota(jnp.int32, (FFN_CHUNK, FFN_CHUNK), 1)
        unzip = (src == jnp.where(dst < half, 2 * dst, 2 * (dst - half) + 1)).astype(jnp.bfloat16)
        for c in range(n_pair):
            cs = slice(c * FFN_CHUNK, (c + 1) * FFN_CHUNK)
            w1p_ref[:, cs] = _dot(w1_ref[0, :, cs].astype(jnp.bfloat16), unzip).astype(jnp.bfloat16)
        w2b_ref[...] = w2_ref[0].astype(jnp.bfloat16)

    def compute(n_rows):
        x = _from_tiles(x_ref, n_rows).astype(jnp.bfloat16)
        for c in range(0, n_pair, FFN_GROUP):
            cs = slice(c * FFN_CHUNK, (c + FFN_GROUP) * FFN_CHUNK)
            h = _dot(x, w1p_ref[:, cs]) + b1_ref[0, :, cs]
            g = jnp.concatenate([h[:, k * FFN_CHUNK:k * FFN_CHUNK + half] for k in range(FFN_GROUP)], axis=1)
            u = jnp.concatenate([h[:, k * FFN_CHUNK + half:(k + 1) * FFN_CHUNK] for k in range(FFN_GROUP)], axis=1)
            g = jnp.minimum(g, SWIGLU_LIMIT)
            u = jnp.clip(u, -SWIGLU_LIMIT, SWIGLU_LIMIT)
            act = g * jax.nn.sigmoid(SWIGLU_ALPHA * g) * (u + 1.0)
            act_ref[:n_rows, c * half:(c + FFN_GROUP) * half] = act.astype(act_ref.dtype)
        _to_tiles(_dot(act_ref[:n_rows, :], w2b_ref[...]) + b2_ref[0], y_ref)
        if n_rows * ROW_TILE < y_ref.shape[0]:
            y_ref[n_rows * ROW_TILE:, :] = jnp.zeros((y_ref.shape[0] - n_rows * ROW_TILE, LANES), y_ref.dtype)

    for tiles in range(1, FFN_STEP // FFN_TILE + 1):
        pl.when(n_sub == tiles)(functools.partial(compute, tiles * FFN_TILE))

    @pl.when(n_sub == 0)
    def _():
        y_ref[...] = jnp.zeros_like(y_ref)


def _unzip_bias(b1):
    e, f2 = b1.shape
    half = FFN_CHUNK // 2
    return b1.reshape(e, f2 // FFN_CHUNK, half, 2).transpose(0, 1, 3, 2).reshape(e, 1, f2)


def _ffn(xs, step_expert, step_rows, n_used, w1, b1, w2, b2):
    d, f2 = w1.shape[1:]
    n_steps = xs.shape[0] // (FFN_STEP * ROW_TILE)
    live = lambda i, nu: jnp.minimum(i, nu[0] - 1)
    wspec = lambda shape: pl.BlockSpec((1,) + shape, lambda i, te, nr, nu: (te[live(i, nu)], 0, 0))
    grid_spec = pltpu.PrefetchScalarGridSpec(
        num_scalar_prefetch=3,
        grid=(n_steps,),
        in_specs=[pl.BlockSpec((FFN_STEP * ROW_TILE, LANES), lambda i, te, nr, nu: (live(i, nu), 0)),
                  wspec((d, f2)), wspec((1, f2)), wspec((f2 // 2, d)), wspec((1, d))],
        out_specs=pl.BlockSpec((FFN_STEP * ROW_TILE, LANES), lambda i, te, nr, nu: (i, 0)),
        scratch_shapes=[pltpu.VMEM((d, f2), jnp.bfloat16), pltpu.VMEM((f2 // 2, d), jnp.bfloat16),
                        pltpu.VMEM((FFN_STEP, f2 // 2), jnp.bfloat16)],
    )
    return pl.pallas_call(
        _ffn_kernel,
        grid_spec=grid_spec,
        out_shape=jax.ShapeDtypeStruct(xs.shape, jnp.float32),
        compiler_params=_params("arbitrary"),
        name="expert_ffn",
    )(step_expert, step_rows, n_used, xs, w1, _unzip_bias(b1), w2, b2[:, None, :])


def _combine_kernel(tab_ref, next_tab_ref, lpos_ref, wts_ref, x1_ref, ga_ref, g_ref, y_ref, o_ref, buf_ref, sems):
    i = pl.program_id(0)
    n_steps = pl.num_programs(0)
    slot = i % 2
    tile = x1_ref.shape[0]
    n_local = buf_ref.shape[1] // ROW_TILE

    def fetch(tab, s):
        for e in range(N_EXPERTS):
            loc, glob, n = _piece_table(tab, e)

            @pl.when(n > 0)
            def _(loc=loc, glob=glob, n=n, e=e):
                pltpu.make_async_copy(y_ref.at[pl.ds(glob, n)], buf_ref.at[s, pl.ds(loc, n)],
                                      sems.at[s]).start(priority=e % 2)

    @pl.when(i == 0)
    def _():
        buf_ref[...] = jnp.zeros_like(buf_ref)
        fetch(tab_ref, 0)

    @pl.when(i + 1 < n_steps)
    def _():
        fetch(next_tab_ref, 1 - slot)

    n = pl.multiple_of(tab_ref[3 * N_EXPERTS], ROW_TILE)

    @pl.when(n > 0)
    def _():
        pltpu.make_async_copy(y_ref.at[pl.ds(0, n)], buf_ref.at[slot, pl.ds(0, n)], sems.at[slot]).wait()

    rows = _from_tiles(buf_ref, n_local, (slot,)).astype(jnp.bfloat16)
    part = tile // OUT_PARTS
    col = lax.broadcasted_iota(jnp.int32, (part, n_local), 1)
    for p in range(OUT_PARTS):
        r = slice(p * part, (p + 1) * part)
        pos = lpos_ref[r, :]
        w = wts_ref[r, :]
        pick = jnp.where(col == pos[:, 0:1], w[:, 0:1], 0.0)
        for k in range(1, TOP_K):
            pick = pick + jnp.where(col == pos[:, k:k + 1], w[:, k:k + 1], 0.0)
        hi = pick.astype(jnp.bfloat16)
        lo = (pick - hi.astype(jnp.float32)).astype(jnp.bfloat16)
        y = _dot(hi, rows) + _dot(lo, rows)
        o_ref[r, :] = x1_ref[r, :] + ga_ref[0] * _rms(y, g_ref[...])


def _combine(ys, tab_flat, lpos, wts, x1, mod3, g_post, *, seq):
    t, d = x1.shape
    tile = DISPATCH_TILE
    n_steps = t // tile
    per_seq = seq // tile
    n_local = tile * TOP_K
    return pl.pallas_call(
        _combine_kernel,
        grid=(n_steps,),
        in_specs=[pl.BlockSpec((TAB_WIDTH,), lambda i: (i,), memory_space=pltpu.SMEM),
                  pl.BlockSpec((TAB_WIDTH,), lambda i: (jnp.minimum(i + 1, n_steps - 1),), memory_space=pltpu.SMEM),
                  pl.BlockSpec((tile, TOP_K), lambda i: (i, 0)),
                  pl.BlockSpec((tile, TOP_K), lambda i: (i, 0)),
                  pl.BlockSpec((tile, d), lambda i: (i, 0)),
                  pl.BlockSpec((1, 1, d), lambda i: (i // per_seq, 0, 5)),
                  pl.BlockSpec((1, d), lambda i: (0, 0)),
                  pl.BlockSpec(memory_space=pl.ANY)],
        out_specs=pl.BlockSpec((tile, d), lambda i: (i, 0)),
        out_shape=jax.ShapeDtypeStruct((t, d), jnp.float32),
        scratch_shapes=[pltpu.VMEM((2, n_local * ROW_TILE, LANES), ys.dtype), pltpu.SemaphoreType.DMA((2,))],
        compiler_params=_params("arbitrary"),
        name="combine",
    )(tab_flat, tab_flat, lpos, wts, x1, mod3, g_post, ys)


def _rope_tables(seq):
    pos = np.arange(seq)
    n_freq = HEAD_DIM // 4
    freqs = ROPE_BASE ** (-jnp.arange(n_freq, dtype=jnp.float32) / n_freq)
    rows = jnp.asarray(pos // GRID_W, jnp.float32)[:, None] * freqs[None, :]
    cols = jnp.asarray(pos % GRID_W, jnp.float32)[:, None] * freqs[None, :]
    ang = jnp.concatenate([rows, rows, cols, cols], axis=1)
    sign = np.tile(np.repeat([-1.0, 1.0], n_freq), 2).astype(np.float32)
    cos = jnp.cos(ang)
    sin = jnp.sin(ang) * sign[None, :]
    reps = LANES // HEAD_DIM
    return jnp.tile(cos, (1, reps)), jnp.tile(sin, (1, reps))


def kernel(x, c, ctx, c_ctx, w_ada, b_ada, g_pre_mix, g_post_mix, g_pre_ffn, g_post_ffn, w_in, g_grp_a, g_grp_b,
           sink_a, rpb_b, w_out, w_router, b_router, w_mlp1, b_mlp1, w_mlp2, b_mlp2):
    batch, seq, d = x.shape
    n_ctx = ctx.shape[1]
    assert w_ada.shape[0] == 1, "single layer"
    assert seq % (ATT_BLOCKS * GRID_W * NB_Q_ROWS) == 0 and seq // GRID_W >= NB_K_ROWS
    assert seq % (ATT_BLOCKS * BLOCK_A) == 0 and seq >= BLOCK_A + 2 * WINDOW
    n_tok = batch * seq
    bf16 = jnp.bfloat16

    def pair_heads(a, axis):
        shape = a.shape
        a = a.reshape(shape[:axis] + (N_KV_A, GQA_GROUP, HEAD_DIM) + shape[axis + 1:])
        return jnp.swapaxes(a, axis, axis + 1).reshape(shape)

    mod_rows = -(-(batch + 1) // 8) * 8
    cc = jnp.zeros((mod_rows, d), jnp.float32).at[:batch].set(c).at[batch].set(c_ctx)
    mod3 = _ada(cc, w_ada[0], b_ada[0]).reshape(mod_rows, 1, 6 * d)

    w_in0 = w_in[0]
    w_lat = jnp.concatenate([pair_heads(w_in0[:, :WIDTH_A], 1), w_in0[:, WIDTH_A:]], axis=1).astype(bf16)
    kv_a_end = WIDTH_A + 2 * WIDTH_KV_A
    w_ctx = jnp.concatenate([w_in0[:, WIDTH_A:kv_a_end], w_in0[:, kv_a_end + WIDTH_B:]], axis=1).astype(bf16)
    g_pre = g_pre_mix[0].reshape(1, d)
    qa, ka, va, qb, kb, vb = _inproj(x.reshape(n_tok, d), mod3, g_pre, w_lat, _rope_tables(seq),
                                     seq=seq, mod_row0=0, latent=True)
    kac, vac, kbc, vbc = _inproj(ctx.reshape(batch * n_ctx, d), mod3, g_pre, w_ctx, None,
                                 seq=n_ctx, mod_row0=batch, latent=False)

    oa = _window_attention(qa, ka.reshape(batch, seq, -1), va.reshape(batch, seq, -1),
                           kac.reshape(batch, n_ctx, -1), vac.reshape(batch, n_ctx, -1),
                           sink_a[0].astype(jnp.float32) * LOG2E, batch=batch, seq=seq)
    tabs, cls_of_block = _nb_bias_tables(rpb_b[0], seq // GRID_W)
    ob = _neighbourhood_attention(qb, kb.reshape(batch, seq, -1), vb.reshape(batch, seq, -1),
                                  kbc.reshape(batch, n_ctx, -1), vbc.reshape(batch, n_ctx, -1),
                                  tabs, cls_of_block, batch=batch, seq=seq)

    w_out0 = w_out[0]
    w_o = jnp.concatenate([pair_heads(w_out0[:WIDTH_A], 0), w_out0[WIDTH_A:]], axis=0).astype(bf16)
    x1, h2, idx, wts, rank, size, base, counts = _outproj(
        oa, ob, x.reshape(n_tok, d), mod3,
        pair_heads(g_grp_a[0], 0).reshape(1, -1), g_grp_b[0].reshape(1, -1),
        g_post_mix[0].reshape(1, d), g_pre_ffn[0].reshape(1, d),
        w_o, w_router[0], b_router[0].reshape(1, -1), seq=seq)

    n_tok_tiles = n_tok // DISPATCH_TILE
    size = size.reshape(n_tok_tiles, N_EXPERTS)
    counts = counts.reshape(-1)
    padded = (counts + FFN_STEP - 1) // FFN_STEP * FFN_STEP
    pad_end = jnp.cumsum(padded).astype(jnp.int32)
    pad_start = pad_end - padded
    n_steps = n_tok * TOP_K // FFN_STEP + N_EXPERTS
    n_used = (pad_end[-1:] // FFN_STEP).astype(jnp.int32)
    step_row0 = jnp.arange(n_steps, dtype=jnp.int32) * FFN_STEP
    step_expert = jnp.minimum(jnp.sum(step_row0[:, None] >= pad_end[None, :], axis=1),
                              N_EXPERTS - 1).astype(jnp.int32)
    own = step_expert[:, None] == jnp.arange(N_EXPERTS, dtype=jnp.int32)
    step_rows = jnp.clip(jnp.sum(jnp.where(own, (pad_start + counts)[None, :], 0), axis=1) - step_row0,
                         0, FFN_STEP).astype(jnp.int32)
    fill_from = (pad_start + counts).astype(jnp.int32)
    local0 = jnp.cumsum(size, axis=1) - size
    global0 = pad_start[None, :] + base.reshape(n_tok_tiles, N_EXPERTS)
    tab = jnp.concatenate([local0, global0, size, jnp.sum(size, axis=1, keepdims=True),
                           jnp.zeros((n_tok_tiles, TAB_WIDTH - 3 * N_EXPERTS - 1), jnp.int32)], axis=1)
    tab = (tab * ROW_TILE).reshape(-1).astype(jnp.int32)
    hot = idx[:, :, None, :] == jnp.arange(N_EXPERTS, dtype=jnp.int32)[None, None, :, None]
    lpos_t = rank + jnp.sum(jnp.where(hot, local0[:, None, :, None], 0), axis=2)
    by_token = lambda a: jnp.swapaxes(a, 1, 2).reshape(n_tok, TOP_K)

    xs = _dispatch(h2, tab, lpos_t, fill_from * ROW_TILE, pad_end * ROW_TILE, n_rows=n_steps * FFN_STEP)
    ys = _ffn(xs, step_expert, step_rows, n_used, w_mlp1[0], b_mlp1[0], w_mlp2[0], b_mlp2[0])
    out = _combine(ys, tab, by_token(lpos_t), by_token(wts), x1, mod3, g_post_ffn[0].reshape(1, d), seq=seq)
    return out.reshape(batch, seq, d)
```

```python
import functools

import numpy as np
import jax
import jax.numpy as jnp
from jax import lax
from jax.experimental import pallas as pl
from jax.experimental.pallas import tpu as pltpu

GRID_W = 64
HEAD_DIM = 64
N_HEADS_A = 8
N_KV_A = 2
GQA_GROUP = N_HEADS_A // N_KV_A
N_HEADS_B = 8
WIDTH_A = N_HEADS_A * HEAD_DIM
WIDTH_KV_A = N_KV_A * HEAD_DIM
WIDTH_B = N_HEADS_B * HEAD_DIM
WINDOW = 128
BLOCK_A = 128
NA_ROWS = 8
NA_COLS = 16
N_EXPERTS = 32
TOP_K = 4
SWIGLU_LIMIT = 7.0
SWIGLU_ALPHA = 1.702
ROPE_BASE = 10000.0
EPS = 1e-6
NEG_INF = -1e30

LANES = 128
VMEM_LIMIT = 56 * 1024 * 1024

LOG2E = 1.4426950408889634
ATT_BLOCKS = 8
NB_Q_ROWS = 2
NB_K_ROWS = NB_Q_ROWS + NA_ROWS
PROJ_TILE = 1024
OUT_TILE = 256
OUT_PARTS = 2
STAGE_SLOTS = 3
FFN_TILE = 256
FFN_STEP = 4 * FFN_TILE
FFN_CHUNK = 256
FFN_GROUP = 4
DISPATCH_TILE = OUT_TILE
ROW_TILE = 8
TAB_WIDTH = 128


def _params(*sem):
    return pltpu.CompilerParams(dimension_semantics=sem, vmem_limit_bytes=VMEM_LIMIT)


def _rms(x, g):
    return x * lax.rsqrt(jnp.mean(x * x, axis=-1, keepdims=True) + EPS) * g


def _dot(a, b):
    return jnp.dot(a, b, preferred_element_type=jnp.float32)


def _dot_nt(a, b):
    return lax.dot_general(a, b, (((1,), (1,)), ((), ())), preferred_element_type=jnp.float32)


def _ada_kernel(c_ref, w_ref, b_ref, o_ref):
    c = c_ref[...]
    s = (c * jax.nn.sigmoid(c)).astype(jnp.bfloat16)
    o_ref[...] = _dot(s, w_ref[...].astype(jnp.bfloat16)) + b_ref[...]


def _ada(cc, w_ada, b_ada):
    rows, d = cc.shape
    n_out = w_ada.shape[1]
    return pl.pallas_call(
        _ada_kernel,
        grid=(n_out // d,),
        in_specs=[pl.BlockSpec((rows, d), lambda j: (0, 0)),
                  pl.BlockSpec((d, d), lambda j: (0, j)),
                  pl.BlockSpec((1, d), lambda j: (0, j))],
        out_specs=pl.BlockSpec((rows, d), lambda j: (0, j)),
        out_shape=jax.ShapeDtypeStruct((rows, n_out), jnp.float32),
        compiler_params=_params("arbitrary"),
        name="ada",
    )(cc, w_ada, b_ada.reshape(1, n_out))


def _rope(x, cos, sin):
    w = x.shape[1]
    reps = w // LANES
    if reps > 1:
        cos = jnp.concatenate([cos] * reps, axis=1)
        sin = jnp.concatenate([sin] * reps, axis=1)
    lane = lax.broadcasted_iota(jnp.int32, x.shape, 1)
    quarter = HEAD_DIM // 4
    partner = jnp.where(lane % (2 * quarter) < quarter,
                        pltpu.roll(x, w - quarter, 1), pltpu.roll(x, quarter, 1))
    return x * cos + partner * sin


def _inproj_kernel(x_ref, sh_ref, sc_ref, g_ref, w_ref, *rest, latent):
    x = x_ref[...]
    h = _rms(x, g_ref[...]) * (1.0 + sc_ref[0]) + sh_ref[0]
    p = _dot(h.astype(jnp.bfloat16), w_ref[...])
    if latent:
        cos_ref, sin_ref, qa_ref, ka_ref, va_ref, qb_ref, kb_ref, vb_ref = rest
        cos, sin = cos_ref[...], sin_ref[...]
        scale = HEAD_DIM ** -0.5 * LOG2E
        o = 0
        qa_ref[...] = (_rope(p[:, o:o + WIDTH_A], cos, sin) * scale).astype(qa_ref.dtype)
        o += WIDTH_A
        ka_ref[...] = _rope(p[:, o:o + WIDTH_KV_A], cos, sin).astype(ka_ref.dtype)
        o += WIDTH_KV_A
        va_ref[...] = p[:, o:o + WIDTH_KV_A].astype(va_ref.dtype)
        o += WIDTH_KV_A
        qb_ref[...] = (p[:, o:o + WIDTH_B] * scale).astype(qb_ref.dtype)
        o += WIDTH_B
    else:
        ka_ref, va_ref, kb_ref, vb_ref = rest
        o = 0
        ka_ref[...] = p[:, o:o + WIDTH_KV_A].astype(ka_ref.dtype)
        o += WIDTH_KV_A
        va_ref[...] = p[:, o:o + WIDTH_KV_A].astype(va_ref.dtype)
        o += WIDTH_KV_A
    kb_ref[...] = p[:, o:o + WIDTH_B].astype(kb_ref.dtype)
    o += WIDTH_B
    vb_ref[...] = p[:, o:o + WIDTH_B].astype(vb_ref.dtype)


def _inproj(x2d, mod3, g_pre, w, rope_tabs, *, seq, mod_row0, latent):
    t, d = x2d.shape
    tile = min(PROJ_TILE, seq)
    per_seq = seq // tile
    if latent:
        mod_row = lambda i: i // per_seq
    else:
        mod_row = lambda i: mod_row0
    in_specs = [pl.BlockSpec((tile, d), lambda i: (i, 0)),
                pl.BlockSpec((1, 1, d), lambda i: (mod_row(i), 0, 0)),
                pl.BlockSpec((1, 1, d), lambda i: (mod_row(i), 0, 1)),
                pl.BlockSpec((1, d), lambda i: (0, 0)),
                pl.BlockSpec(w.shape, lambda i: (0, 0))]
    args = [x2d, mod3, mod3, g_pre, w]
    widths = [WIDTH_KV_A, WIDTH_KV_A, WIDTH_B, WIDTH_B]
    if latent:
        in_specs += [pl.BlockSpec((tile, LANES), lambda i: (i % per_seq, 0))] * 2
        args += list(rope_tabs)
        widths = [WIDTH_A, WIDTH_KV_A, WIDTH_KV_A, WIDTH_B, WIDTH_B, WIDTH_B]
    return pl.pallas_call(
        functools.partial(_inproj_kernel, latent=latent),
        grid=(t // tile,),
        in_specs=in_specs,
        out_specs=[pl.BlockSpec((tile, wd), lambda i: (i, 0)) for wd in widths],
        out_shape=[jax.ShapeDtypeStruct((t, wd), jnp.bfloat16) for wd in widths],
        compiler_params=_params("parallel"),
        name="inproj_latent" if latent else "inproj_ctx",
    )(*args)


def _ones_beside(v):
    return jnp.concatenate([v, jnp.ones_like(v)], axis=1)


def _group_attention(q_ref, o_ref, rows, groups, s_ref, p_ref, m_ref):
    tq = rows.stop - rows.start
    low = lax.broadcasted_iota(jnp.int32, (tq, LANES), 1) < HEAD_DIM
    layout = []
    base = 0
    for pairs, k_loc, k_ctx, _, _, _, _ in groups:
        heads = [(j, 0) for j in pairs] + [(j, 1) for j in pairs]
        n_loc = k_loc.shape[0]
        stacked = []
        for j, half in heads:
            q = q_ref[rows, j * LANES:(j + 1) * LANES]
            stacked.append(jnp.where(low if half == 0 else ~low, q, jnp.zeros_like(q)))
        qs = jnp.concatenate(stacked, axis=0)
        n = len(heads) * tq
        s_ref[base:base + n, :n_loc] = _dot_nt(qs, k_loc)
        s_ref[base:base + n, n_loc:] = _dot_nt(qs, k_ctx)
        layout.append((base, heads, n_loc))
        base += n
    for (base, heads, n_loc), group in zip(layout, groups):
        bias_of, sink_of = group[5], group[6]
        for g, (j, half) in enumerate(heads):
            r = slice(base + g * tq, base + (g + 1) * tq)
            s_loc = s_ref[r, :n_loc] + bias_of(j, half)
            s_ctx = s_ref[r, n_loc:]
            m = jnp.maximum(jnp.max(s_loc, axis=1, keepdims=True), jnp.max(s_ctx, axis=1, keepdims=True))
            sink = sink_of(j, half)
            if sink is not None:
                m = jnp.maximum(m, sink)
            p_ref[r, :n_loc] = jnp.exp2((s_loc - m).astype(jnp.bfloat16))
            p_ref[r, n_loc:] = jnp.exp2((s_ctx - m).astype(jnp.bfloat16))
            m_ref[r, :] = m
    for (base, heads, n_loc), group in zip(layout, groups):
        pairs, _, _, v_loc, v_ctx, _, sink_of = group
        n = len(heads) * tq
        both = (_dot(p_ref[base:base + n, :n_loc], _ones_beside(v_loc))
                + _dot(p_ref[base:base + n, n_loc:], _ones_beside(v_ctx)))
        for a, j in enumerate(pairs):
            outs = []
            for half in range(2):
                g = half * len(pairs) + a
                r = slice(g * tq, (g + 1) * tq)
                den = both[r, LANES:]
                sink = sink_of(j, half)
                if sink is not None:
                    den = den + jnp.exp2(sink - m_ref[base + g * tq:base + (g + 1) * tq, :])
                outs.append(both[r, :LANES] / den)
            o_ref[rows, j * LANES:(j + 1) * LANES] = jnp.where(low, outs[0], outs[1]).astype(o_ref.dtype)


def _attention_scratch(tq, n_heads, n_keys):
    return [pltpu.VMEM((n_heads * tq, n_keys), jnp.float32), pltpu.VMEM((n_heads * tq, n_keys), jnp.bfloat16),
            pltpu.VMEM((n_heads * tq, 1), jnp.float32)]


def _win_kernel(sink_ref, q_ref, k_ref, v_ref, kc_ref, vc_ref, o_ref, *scratch, seq):
    span = BLOCK_A + 2 * WINDOW
    pairs = list(range(GQA_GROUP))
    for sb in range(ATT_BLOCKS):
        i = pl.program_id(1) * ATT_BLOCKS + sb
        start = pl.multiple_of(jnp.clip(i * BLOCK_A - WINDOW, 0, seq - span), BLOCK_A)
        qpos = i * BLOCK_A + lax.broadcasted_iota(jnp.int32, (BLOCK_A, span), 0)
        kpos = start + lax.broadcasted_iota(jnp.int32, (BLOCK_A, span), 1)
        bias = jnp.where(jnp.abs(kpos - qpos) <= WINDOW, 0.0, NEG_INF).astype(jnp.float32)
        groups = [([j], k_ref[0, pl.ds(start, span), :], kc_ref[0], v_ref[0, pl.ds(start, span), :], vc_ref[0],
                   lambda j, half: bias, lambda j, half: sink_ref[half * GQA_GROUP + j]) for j in pairs]
        _group_attention(q_ref, o_ref, slice(sb * BLOCK_A, (sb + 1) * BLOCK_A), groups, *scratch)


def _window_attention(qa, ka, va, kac, vac, sink, *, batch, seq):
    tq = ATT_BLOCKS * BLOCK_A
    nb = seq // tq
    n_ctx = kac.shape[1]
    return pl.pallas_call(
        functools.partial(_win_kernel, seq=seq),
        grid=(batch, nb),
        in_specs=[pl.BlockSpec(memory_space=pltpu.SMEM),
                  pl.BlockSpec((tq, WIDTH_A), lambda b, i: (b * nb + i, 0)),
                  pl.BlockSpec((1, seq, WIDTH_KV_A), lambda b, i: (b, 0, 0)),
                  pl.BlockSpec((1, seq, WIDTH_KV_A), lambda b, i: (b, 0, 0)),
                  pl.BlockSpec((1, n_ctx, WIDTH_KV_A), lambda b, i: (b, 0, 0)),
                  pl.BlockSpec((1, n_ctx, WIDTH_KV_A), lambda b, i: (b, 0, 0))],
        out_specs=pl.BlockSpec((tq, WIDTH_A), lambda b, i: (b * nb + i, 0)),
        out_shape=jax.ShapeDtypeStruct((batch * seq, WIDTH_A), jnp.bfloat16),
        scratch_shapes=_attention_scratch(BLOCK_A, N_HEADS_A, BLOCK_A + 2 * WINDOW + n_ctx),
        compiler_params=_params("parallel", "arbitrary"),
        name="window_attention",
    )(sink, qa, ka, va, kac, vac)


def _nb_kernel(q_ref, k_ref, v_ref, kc_ref, vc_ref, *rest, rows_n):
    tab_refs, o_ref, scratch = rest[:ATT_BLOCKS], rest[ATT_BLOCKS], rest[ATT_BLOCKS + 1:]
    n_keys = NB_K_ROWS * GRID_W
    tq = NB_Q_ROWS * GRID_W
    for sb in range(ATT_BLOCKS):
        m = pl.program_id(1) * ATT_BLOCKS + sb
        start_row = jnp.clip(NB_Q_ROWS * m - NA_ROWS // 2, 0, rows_n - NB_K_ROWS)
        start = pl.multiple_of(start_row * GRID_W, LANES)
        tab_ref = tab_refs[sb]
        groups = []
        for j in range(q_ref.shape[1] // LANES):
            cols = slice(j * LANES, (j + 1) * LANES)
            groups.append(([j], k_ref[0, pl.ds(start, n_keys), cols], kc_ref[0, :, cols],
                           v_ref[0, pl.ds(start, n_keys), cols], vc_ref[0, :, cols],
                           lambda j, half: tab_ref[0, 2 * j + half].astype(jnp.float32), lambda j, half: None))
        _group_attention(q_ref, o_ref, slice(sb * tq, (sb + 1) * tq), groups, *scratch)


def _nb_classes(rows_n):
    n_blocks = rows_n // NB_Q_ROWS
    sig = {}
    cls_of_block = []
    reps = []
    for m in range(n_blocks):
        start_row = int(np.clip(NB_Q_ROWS * m - NA_ROWS // 2, 0, rows_n - NB_K_ROWS))
        key = tuple((start_row - r, int(np.clip(r - NA_ROWS // 2, 0, rows_n - NA_ROWS)) - r)
                    for r in range(NB_Q_ROWS * m, NB_Q_ROWS * (m + 1)))
        if key not in sig:
            sig[key] = len(reps)
            reps.append(m)
        cls_of_block.append(sig[key])
    return np.asarray(cls_of_block, np.int32), reps


def _nb_bias_tables(rpb, rows_n):
    cls_of_block, reps = _nb_classes(rows_n)
    n_heads = rpb.shape[0]
    cq = np.arange(GRID_W)[:, None]
    ck = np.arange(GRID_W)[None, :]
    cs = np.clip(cq - NA_COLS // 2, 0, GRID_W - NA_COLS)
    col_ok = (ck >= cs) & (ck < cs + NA_COLS)
    pick = ((ck - cq + NA_COLS - 1)[None] == np.arange(2 * NA_COLS - 1)[:, None, None]) & col_ok[None]
    tiles = jnp.einsum('hrd,dqk->hrqk', rpb.astype(jnp.float32), jnp.asarray(pick, jnp.float32),
                       precision=lax.Precision.HIGHEST)
    tiles = jnp.where(jnp.asarray(col_ok)[None, None], tiles * LOG2E, NEG_INF).astype(jnp.bfloat16)
    blocked = jnp.full((n_heads, GRID_W, GRID_W), NEG_INF, jnp.bfloat16)
    tabs = []
    for m in reps:
        start_row = int(np.clip(NB_Q_ROWS * m - NA_ROWS // 2, 0, rows_n - NB_K_ROWS))
        q_rows = []
        for r in range(NB_Q_ROWS * m, NB_Q_ROWS * (m + 1)):
            rs = int(np.clip(r - NA_ROWS // 2, 0, rows_n - NA_ROWS))
            q_rows.append(jnp.concatenate(
                [tiles[:, krow - r + NA_ROWS - 1] if rs <= krow < rs + NA_ROWS else blocked
                 for krow in range(start_row, start_row + NB_K_ROWS)], axis=2))
        tabs.append(jnp.concatenate(q_rows, axis=1))
    return jnp.stack(tabs), cls_of_block


def _neighbourhood_attention(qb, kb, vb, kbc, vbc, tabs, cls_of_block, *, batch, seq):
    rows_n = seq // GRID_W
    tq = ATT_BLOCKS * NB_Q_ROWS * GRID_W
    nstep = seq // tq
    n_ctx = kbc.shape[1]

    def tab_spec(sb):
        return pl.BlockSpec((1,) + tabs.shape[1:], lambda b, m, c: (c[m * ATT_BLOCKS + sb], 0, 0, 0))

    grid_spec = pltpu.PrefetchScalarGridSpec(
        num_scalar_prefetch=1,
        grid=(batch, nstep),
        in_specs=[pl.BlockSpec((tq, WIDTH_B), lambda b, m, c: (b * nstep + m, 0)),
                  pl.BlockSpec((1, seq, WIDTH_B), lambda b, m, c: (b, 0, 0)),
                  pl.BlockSpec((1, seq, WIDTH_B), lambda b, m, c: (b, 0, 0)),
                  pl.BlockSpec((1, n_ctx, WIDTH_B), lambda b, m, c: (b, 0, 0)),
                  pl.BlockSpec((1, n_ctx, WIDTH_B), lambda b, m, c: (b, 0, 0))]
                 + [tab_spec(sb) for sb in range(ATT_BLOCKS)],
        out_specs=pl.BlockSpec((tq, WIDTH_B), lambda b, m, c: (b * nstep + m, 0)),
        scratch_shapes=_attention_scratch(NB_Q_ROWS * GRID_W, N_HEADS_B, NB_K_ROWS * GRID_W + n_ctx),
    )

    def body(c_ref, *refs):
        _nb_kernel(*refs, rows_n=rows_n)

    return pl.pallas_call(
        body,
        grid_spec=grid_spec,
        out_shape=jax.ShapeDtypeStruct((batch * seq, WIDTH_B), jnp.bfloat16),
        compiler_params=_params("parallel", "arbitrary"),
        name="neighbourhood_attention",
    )(jnp.asarray(cls_of_block), qb, kb, vb, kbc, vbc, *([tabs] * ATT_BLOCKS))


def _outproj_kernel(oa_ref, ob_ref, x_ref, ga_ref, sh_ref, sc_ref, gga_ref, ggb_ref, gpost_ref, gpre_ref,
                    wo_ref, wr_hi_ref, wr_lo_ref, br_ref,
                    x1_ref, h2_ref, idx_ref, wts_ref, rank_ref, size_ref, base_ref, cnt_ref, run_ref):
    i = pl.program_id(0)

    @pl.when(i == 0)
    def _():
        run_ref[...] = jnp.zeros_like(run_ref)

    n_rows = x_ref.shape[0]
    parts = [slice(p * n_rows // OUT_PARTS, (p + 1) * n_rows // OUT_PARTS) for p in range(OUT_PARTS)]
    na = [_rms(oa_ref[r, :].astype(jnp.float32), gga_ref[...]).astype(jnp.bfloat16) for r in parts]
    nb = [_rms(ob_ref[r, :].astype(jnp.float32), ggb_ref[...]).astype(jnp.bfloat16) for r in parts]
    mix = [_dot(a, wo_ref[:WIDTH_A, :]) + _dot(b, wo_ref[WIDTH_A:, :]) for a, b in zip(na, nb)]
    x1 = [x_ref[r, :] + ga_ref[0] * _rms(m, gpost_ref[...]) for r, m in zip(parts, mix)]
    for r, v in zip(parts, x1):
        x1_ref[r, :] = v
    h2 = [_rms(v, gpre_ref[...]) * (1.0 + sc_ref[0]) + sh_ref[0] for v in x1]
    for r, v in zip(parts, h2):
        h2_ref[r, :] = v.astype(h2_ref.dtype)

    h_hi = [v.astype(jnp.bfloat16) for v in h2]
    h_lo = [(v - hi.astype(jnp.float32)).astype(jnp.bfloat16) for v, hi in zip(h2, h_hi)]
    logits = jnp.concatenate([_dot(hi, wr_hi_ref[...]) + _dot(lo, wr_hi_ref[...]) + _dot(hi, wr_lo_ref[...])
                              for hi, lo in zip(h_hi, h_lo)], axis=0)
    work = logits.T[:N_EXPERTS] + br_ref[...]
    e, t = work.shape
    row = lax.broadcasted_iota(jnp.int32, (e, t), 0)
    chosen = jnp.zeros((e, t), jnp.float32)
    vals, sels, hots = [], [], []
    for _k in range(TOP_K):
        mx = jnp.max(work, axis=0, keepdims=True)
        sel = jnp.min(jnp.where(work == mx, row, e), axis=0, keepdims=True)
        hot = row == sel
        vals.append(mx)
        sels.append(sel)
        hots.append(hot)
        work = jnp.where(hot, -jnp.inf, work)
        chosen = chosen + hot.astype(jnp.float32)
    ex = [jnp.exp(v - vals[0]) for v in vals]
    den = ex[0] + ex[1] + ex[2] + ex[3]

    earlier = (lax.broadcasted_iota(jnp.int32, (t, t), 0) < lax.broadcasted_iota(jnp.int32, (t, t), 1))
    prefix = _dot(chosen.astype(jnp.bfloat16), earlier.astype(jnp.bfloat16))
    ranks = [jnp.sum(jnp.where(h, prefix, 0.0), axis=0, keepdims=True) for h in hots]
    size = jnp.sum(chosen, axis=1, keepdims=True).astype(jnp.int32)
    size_ref[0] = size
    base_ref[0] = run_ref[...]
    run_ref[...] = run_ref[...] + size
    cnt_ref[...] = run_ref[...]

    idx_ref[0] = jnp.concatenate(sels, axis=0)
    wts_ref[0] = jnp.concatenate([v / den for v in ex], axis=0)
    rank_ref[0] = jnp.concatenate(ranks, axis=0).astype(jnp.int32)


def _outproj(oa, ob, x2d, mod3, gga, ggb, gpost, gpre, w_out, w_router, b_router, *, seq):
    t, d = x2d.shape
    tile = OUT_TILE
    per_seq = seq // tile
    row = lambda i: (i, 0)
    const = lambda i: (0, 0)
    modspec = lambda part: pl.BlockSpec((1, 1, d), lambda i: (i // per_seq, 0, part))
    per_tile = lambda shape: pl.BlockSpec((1,) + shape, lambda i: (i, 0, 0))
    n_steps = t // tile
    w_pad = jnp.zeros((d, LANES), jnp.float32).at[:, :N_EXPERTS].set(w_router)
    wr_hi = w_pad.astype(jnp.bfloat16)
    wr_lo = (w_pad - wr_hi.astype(jnp.float32)).astype(jnp.bfloat16)
    outs = pl.pallas_call(
        _outproj_kernel,
        grid=(n_steps,),
        in_specs=[pl.BlockSpec((tile, WIDTH_A), row), pl.BlockSpec((tile, WIDTH_B), row),
                  pl.BlockSpec((tile, d), row),
                  modspec(2), modspec(3), modspec(4),
                  pl.BlockSpec((1, WIDTH_A), const), pl.BlockSpec((1, WIDTH_B), const),
                  pl.BlockSpec((1, d), const), pl.BlockSpec((1, d), const),
                  pl.BlockSpec(w_out.shape, const), pl.BlockSpec((d, LANES), const), pl.BlockSpec((d, LANES), const),
                  pl.BlockSpec((N_EXPERTS, 1), const)],
        out_specs=[pl.BlockSpec((tile, d), row), pl.BlockSpec((tile, d), row),
                   per_tile((TOP_K, tile)), per_tile((TOP_K, tile)), per_tile((TOP_K, tile)),
                   per_tile((N_EXPERTS, 1)), per_tile((N_EXPERTS, 1)),
                   pl.BlockSpec((N_EXPERTS, 1), const)],
        out_shape=[jax.ShapeDtypeStruct((t, d), jnp.float32), jax.ShapeDtypeStruct((t, d), jnp.bfloat16),
                   jax.ShapeDtypeStruct((n_steps, TOP_K, tile), jnp.int32),
                   jax.ShapeDtypeStruct((n_steps, TOP_K, tile), jnp.float32),
                   jax.ShapeDtypeStruct((n_steps, TOP_K, tile), jnp.int32),
                   jax.ShapeDtypeStruct((n_steps, N_EXPERTS, 1), jnp.int32),
                   jax.ShapeDtypeStruct((n_steps, N_EXPERTS, 1), jnp.int32),
                   jax.ShapeDtypeStruct((N_EXPERTS, 1), jnp.int32)],
        scratch_shapes=[pltpu.VMEM((N_EXPERTS, 1), jnp.int32)],
        compiler_params=_params("arbitrary"),
        name="outproj_router",
    )(oa, ob, x2d, mod3, mod3, mod3, gga, ggb, gpost, gpre, w_out, wr_hi, wr_lo, b_router.reshape(N_EXPERTS, 1))
    return outs


def _piece_table(tab_ref, e):
    return (pl.multiple_of(tab_ref[e], ROW_TILE), pl.multiple_of(tab_ref[N_EXPERTS + e], ROW_TILE),
            pl.multiple_of(tab_ref[2 * N_EXPERTS + e], ROW_TILE))


def _to_tiles(x, ref, index=()):
    for c in range(ROW_TILE):
        ref[index + (pl.ds(c, x.shape[0], stride=ROW_TILE), slice(None))] = x[:, c * LANES:(c + 1) * LANES]


def _from_tiles(ref, n_rows, index=()):
    return jnp.concatenate([ref[index + (pl.ds(c, n_rows, stride=ROW_TILE), slice(None))]
                            for c in range(ROW_TILE)], axis=1)


def _dispatch_kernel(fill_from_ref, pad_end_ref, tab_ref, h_ref, lpos_ref, xs_ref,
                     stage_ref, zero_ref, pend_ref, sems, zsem, *, grid_steps):
    i = pl.program_id(0)
    n_steps = pl.num_programs(0)
    n_slots = stage_ref.shape[0]
    slot = i % n_slots
    n_local = stage_ref.shape[1] // ROW_TILE
    fill_rows = zero_ref.shape[0]

    def fill(row):
        return pltpu.make_async_copy(zero_ref.at[pl.ds(0, fill_rows)],
                                     xs_ref.at[pl.ds(pl.multiple_of(row, fill_rows), fill_rows)], zsem)

    def pad_fill(e):
        n = pl.multiple_of(pad_end_ref[e] - fill_from_ref[e], ROW_TILE)
        return n, pltpu.make_async_copy(zero_ref.at[pl.ds(0, n)],
                                        xs_ref.at[pl.ds(pl.multiple_of(fill_from_ref[e], ROW_TILE), n)], zsem)

    def drain(s):
        n = pl.multiple_of(pend_ref[s], ROW_TILE)

        @pl.when(n > 0)
        def _():
            pltpu.make_async_copy(stage_ref.at[s, pl.ds(0, n)], xs_ref.at[pl.ds(0, n)], sems.at[s]).wait()
        pend_ref[s] = 0

    @pl.when(i == 0)
    def _():
        zero_ref[...] = jnp.zeros_like(zero_ref)
        for s in range(n_slots):
            pend_ref[s] = 0

    for j in range(-(-N_EXPERTS // grid_steps)):
        e = i * -(-N_EXPERTS // grid_steps) + j

        @pl.when(e < N_EXPERTS)
        def _(e=e):
            n, copy = pad_fill(e)

            @pl.when(n > 0)
            def _():
                copy.start()

    row = lax.broadcasted_iota(jnp.int32, (n_local, h_ref.shape[0]), 0)
    place = row == lpos_ref[0, 0:1, :]
    for k in range(1, TOP_K):
        place = place | (row == lpos_ref[0, k:k + 1, :])
    rows = _dot(place.astype(jnp.bfloat16), h_ref[...])

    drain(slot)
    _to_tiles(rows, stage_ref, (slot,))

    total = 0
    for e in range(N_EXPERTS):
        loc, glob, n = _piece_table(tab_ref, e)

        @pl.when(n > 0)
        def _(loc=loc, glob=glob, n=n, e=e):
            pltpu.make_async_copy(stage_ref.at[slot, pl.ds(loc, n)], xs_ref.at[pl.ds(glob, n)],
                                  sems.at[slot]).start(priority=e % 2)
        total = total + n
    pend_ref[slot] = total

    tail0 = pad_end_ref[N_EXPERTS - 1]
    n_tail = (xs_ref.shape[0] - tail0) // fill_rows
    per_step = (n_tail + n_steps - 1) // n_steps

    def start_tail(j, c):
        t = i * per_step + j

        @pl.when(t < n_tail)
        def _():
            fill(tail0 + t * fill_rows).start()
        return c
    lax.fori_loop(0, per_step, start_tail, 0)

    @pl.when(i == n_steps - 1)
    def _():
        for s in range(n_slots):
            drain(s)

        def wait_tail(j, c):
            fill(0).wait()
            return c
        lax.fori_loop(0, n_tail, wait_tail, 0)

        def wait_pad(e, c):
            n, copy = pad_fill(e)

            @pl.when(n > 0)
            def _():
                copy.wait()
            return c
        lax.fori_loop(0, N_EXPERTS, wait_pad, 0)


def _dispatch(h2, tab_flat, lpos_t, fill_from, pad_end, *, n_rows):
    t, d = h2.shape
    tile = DISPATCH_TILE
    assert d == ROW_TILE * LANES
    n_local = tile * TOP_K
    grid_spec = pltpu.PrefetchScalarGridSpec(
        num_scalar_prefetch=2,
        grid=(t // tile,),
        in_specs=[pl.BlockSpec((TAB_WIDTH,), lambda i, *_: (i,), memory_space=pltpu.SMEM),
                  pl.BlockSpec((tile, d), lambda i, *_: (i, 0)),
                  pl.BlockSpec((1, TOP_K, tile), lambda i, *_: (i, 0, 0))],
        out_specs=pl.BlockSpec(memory_space=pl.ANY),
        scratch_shapes=[pltpu.VMEM((STAGE_SLOTS, n_local * ROW_TILE, LANES), jnp.float32),
                        pltpu.VMEM((FFN_STEP * ROW_TILE, LANES), jnp.float32),
                        pltpu.SMEM((STAGE_SLOTS,), jnp.int32),
                        pltpu.SemaphoreType.DMA((STAGE_SLOTS,)), pltpu.SemaphoreType.DMA(())],
    )
    return pl.pallas_call(
        functools.partial(_dispatch_kernel, grid_steps=t // tile),
        grid_spec=grid_spec,
        out_shape=jax.ShapeDtypeStruct((n_rows * ROW_TILE, LANES), jnp.float32),
        compiler_params=_params("arbitrary"),
        name="dispatch",
    )(fill_from, pad_end, tab_flat, h2, lpos_t)


def _ffn_kernel(te_ref, rows_ref, n_used_ref, x_ref, w1_ref, b1_ref, w2_ref, b2_ref, y_ref,
                w1p_ref, w2b_ref, act_ref):
    i = pl.program_id(0)
    live = i < n_used_ref[0]
    n_pair = w1_ref.shape[2] // FFN_CHUNK
    half = FFN_CHUNK // 2
    n_sub = jnp.where(live, (rows_ref[i] + FFN_TILE - 1) // FFN_TILE, 0)

    @pl.when(live & ((i == 0) | (te_ref[i] != te_ref[jnp.maximum(i - 1, 0)])))
    def _():
        src = lax.broadcasted_iota(jnp.int32, (FFN_CHUNK, FFN_CHUNK), 0)
        dst = lax.broadcasted_iota(jnp.int32, (FFN_CHUNK, FFN_CHUNK), 1)
        unzip = (src == jnp.where(dst < half, 2 * dst, 2 * (dst - half) + 1)).astype(jnp.bfloat16)
        for c in range(n_pair):
            cs = slice(c * FFN_CHUNK, (c + 1) * FFN_CHUNK)
            w1p_ref[:, cs] = _dot(w1_ref[0, :, cs].astype(jnp.bfloat16), unzip).astype(jnp.bfloat16)
        w2b_ref[...] = w2_ref[0].astype(jnp.bfloat16)

    def compute(n_rows):
        x = _from_tiles(x_ref, n_rows).astype(jnp.bfloat16)
        for c in range(0, n_pair, FFN_GROUP):
            cs = slice(c * FFN_CHUNK, (c + FFN_GROUP) * FFN_CHUNK)
            h = _dot(x, w1p_ref[:, cs]) + b1_ref[0, :, cs]
            g = jnp.concatenate([h[:, k * FFN_CHUNK:k * FFN_CHUNK + half] for k in range(FFN_GROUP)], axis=1)
            u = jnp.concatenate([h[:, k * FFN_CHUNK + half:(k + 1) * FFN_CHUNK] for k in range(FFN_GROUP)], axis=1)
            g = jnp.minimum(g, SWIGLU_LIMIT)
            u = jnp.clip(u, -SWIGLU_LIMIT, SWIGLU_LIMIT)
            act = g * jax.nn.sigmoid(SWIGLU_ALPHA * g) * (u + 1.0)
            act_ref[:n_rows, c * half:(c + FFN_GROUP) * half] = act.astype(act_ref.dtype)
        _to_tiles(_dot(act_ref[:n_rows, :], w2b_ref[...]) + b2_ref[0], y_ref)
        if n_rows * ROW_TILE < y_ref.shape[0]:
            y_ref[n_rows * ROW_TILE:, :] = jnp.zeros((y_ref.shape[0] - n_rows * ROW_TILE, LANES), y_ref.dtype)

    for tiles in range(1, FFN_STEP // FFN_TILE + 1):
        pl.when(n_sub == tiles)(functools.partial(compute, tiles * FFN_TILE))

    @pl.when(n_sub == 0)
    def _():
        y_ref[...] = jnp.zeros_like(y_ref)


def _unzip_bias(b1):
    e, f2 = b1.shape
    half = FFN_CHUNK // 2
    return b1.reshape(e, f2 // FFN_CHUNK, half, 2).transpose(0, 1, 3, 2).reshape(e, 1, f2)


def _ffn(xs, step_expert, step_rows, n_used, w1, b1, w2, b2):
    d, f2 = w1.shape[1:]
    n_steps = xs.shape[0] // (FFN_STEP * ROW_TILE)
    live = lambda i, nu: jnp.minimum(i, nu[0] - 1)
    wspec = lambda shape: pl.BlockSpec((1,) + shape, lambda i, te, nr, nu: (te[live(i, nu)], 0, 0))
    grid_spec = pltpu.PrefetchScalarGridSpec(
        num_scalar_prefetch=3,
        grid=(n_steps,),
        in_specs=[pl.BlockSpec((FFN_STEP * ROW_TILE, LANES), lambda i, te, nr, nu: (live(i, nu), 0)),
                  wspec((d, f2)), wspec((1, f2)), wspec((f2 // 2, d)), wspec((1, d))],
        out_specs=pl.BlockSpec((FFN_STEP * ROW_TILE, LANES), lambda i, te, nr, nu: (i, 0)),
        scratch_shapes=[pltpu.VMEM((d, f2), jnp.bfloat16), pltpu.VMEM((f2 // 2, d), jnp.bfloat16),
                        pltpu.VMEM((FFN_STEP, f2 // 2), jnp.bfloat16)],
    )
    return pl.pallas_call(
        _ffn_kernel,
        grid_spec=grid_spec,
        out_shape=jax.ShapeDtypeStruct(xs.shape, jnp.float32),
        compiler_params=_params("arbitrary"),
        name="expert_ffn",
    )(step_expert, step_rows, n_used, xs, w1, _unzip_bias(b1), w2, b2[:, None, :])


def _combine_kernel(tab_ref, next_tab_ref, lpos_ref, wts_ref, x1_ref, ga_ref, g_ref, y_ref, o_ref, buf_ref, sems):
    i = pl.program_id(0)
    n_steps = pl.num_programs(0)
    slot = i % 2
    tile = x1_ref.shape[0]
    n_local = buf_ref.shape[1] // ROW_TILE

    def fetch(tab, s):
        for e in range(N_EXPERTS):
            loc, glob, n = _piece_table(tab, e)

            @pl.when(n > 0)
            def _(loc=loc, glob=glob, n=n, e=e):
                pltpu.make_async_copy(y_ref.at[pl.ds(glob, n)], buf_ref.at[s, pl.ds(loc, n)],
                                      sems.at[s]).start(priority=e % 2)

    @pl.when(i == 0)
    def _():
        buf_ref[...] = jnp.zeros_like(buf_ref)
        fetch(tab_ref, 0)

    @pl.when(i + 1 < n_steps)
    def _():
        fetch(next_tab_ref, 1 - slot)

    n = pl.multiple_of(tab_ref[3 * N_EXPERTS], ROW_TILE)

    @pl.when(n > 0)
    def _():
        pltpu.make_async_copy(y_ref.at[pl.ds(0, n)], buf_ref.at[slot, pl.ds(0, n)], sems.at[slot]).wait()

    rows = _from_tiles(buf_ref, n_local, (slot,)).astype(jnp.bfloat16)
    part = tile // OUT_PARTS
    col = lax.broadcasted_iota(jnp.int32, (part, n_local), 1)
    for p in range(OUT_PARTS):
        r = slice(p * part, (p + 1) * part)
        pos = lpos_ref[r, :]
        w = wts_ref[r, :]
        pick = jnp.where(col == pos[:, 0:1], w[:, 0:1], 0.0)
        for k in range(1, TOP_K):
            pick = pick + jnp.where(col == pos[:, k:k + 1], w[:, k:k + 1], 0.0)
        hi = pick.astype(jnp.bfloat16)
        lo = (pick - hi.astype(jnp.float32)).astype(jnp.bfloat16)
        y = _dot(hi, rows) + _dot(lo, rows)
        o_ref[r, :] = x1_ref[r, :] + ga_ref[0] * _rms(y, g_ref[...])


def _combine(ys, tab_flat, lpos, wts, x1, mod3, g_post, *, seq):
    t, d = x1.shape
    tile = DISPATCH_TILE
    n_steps = t // tile
    per_seq = seq // tile
    n_local = tile * TOP_K
    return pl.pallas_call(
        _combine_kernel,
        grid=(n_steps,),
        in_specs=[pl.BlockSpec((TAB_WIDTH,), lambda i: (i,), memory_space=pltpu.SMEM),
                  pl.BlockSpec((TAB_WIDTH,), lambda i: (jnp.minimum(i + 1, n_steps - 1),), memory_space=pltpu.SMEM),
                  pl.BlockSpec((tile, TOP_K), lambda i: (i, 0)),
                  pl.BlockSpec((tile, TOP_K), lambda i: (i, 0)),
                  pl.BlockSpec((tile, d), lambda i: (i, 0)),
                  pl.BlockSpec((1, 1, d), lambda i: (i // per_seq, 0, 5)),
                  pl.BlockSpec((1, d), lambda i: (0, 0)),
                  pl.BlockSpec(memory_space=pl.ANY)],
        out_specs=pl.BlockSpec((tile, d), lambda i: (i, 0)),
        out_shape=jax.ShapeDtypeStruct((t, d), jnp.float32),
        scratch_shapes=[pltpu.VMEM((2, n_local * ROW_TILE, LANES), ys.dtype), pltpu.SemaphoreType.DMA((2,))],
        compiler_params=_params("arbitrary"),
        name="combine",
    )(tab_flat, tab_flat, lpos, wts, x1, mod3, g_post, ys)


def _rope_tables(seq):
    pos = np.arange(seq)
    n_freq = HEAD_DIM // 4
    freqs = ROPE_BASE ** (-jnp.arange(n_freq, dtype=jnp.float32) / n_freq)
    rows = jnp.asarray(pos // GRID_W, jnp.float32)[:, None] * freqs[None, :]
    cols = jnp.asarray(pos % GRID_W, jnp.float32)[:, None] * freqs[None, :]
    ang = jnp.concatenate([rows, rows, cols, cols], axis=1)
    sign = np.tile(np.repeat([-1.0, 1.0], n_freq), 2).astype(np.float32)
    cos = jnp.cos(ang)
    sin = jnp.sin(ang) * sign[None, :]
    reps = LANES // HEAD_DIM
    return jnp.tile(cos, (1, reps)), jnp.tile(sin, (1, reps))


def kernel(x, c, ctx, c_ctx, w_ada, b_ada, g_pre_mix, g_post_mix, g_pre_ffn, g_post_ffn, w_in, g_grp_a, g_grp_b,
           sink_a, rpb_b, w_out, w_router, b_router, w_mlp1, b_mlp1, w_mlp2, b_mlp2):
    batch, seq, d = x.shape
    n_ctx = ctx.shape[1]
    assert w_ada.shape[0] == 1, "single layer"
    assert seq % (ATT_BLOCKS * GRID_W * NB_Q_ROWS) == 0 and seq // GRID_W >= NB_K_ROWS
    assert seq % (ATT_BLOCKS * BLOCK_A) == 0 and seq >= BLOCK_A + 2 * WINDOW
    n_tok = batch * seq
    bf16 = jnp.bfloat16

    def pair_heads(a, axis):
        shape = a.shape
        a = a.reshape(shape[:axis] + (N_KV_A, GQA_GROUP, HEAD_DIM) + shape[axis + 1:])
        return jnp.swapaxes(a, axis, axis + 1).reshape(shape)

    mod_rows = -(-(batch + 1) // 8) * 8
    cc = jnp.zeros((mod_rows, d), jnp.float32).at[:batch].set(c).at[batch].set(c_ctx)
    mod3 = _ada(cc, w_ada[0], b_ada[0]).reshape(mod_rows, 1, 6 * d)

    w_in0 = w_in[0]
    w_lat = jnp.concatenate([pair_heads(w_in0[:, :WIDTH_A], 1), w_in0[:, WIDTH_A:]], axis=1).astype(bf16)
    kv_a_end = WIDTH_A + 2 * WIDTH_KV_A
    w_ctx = jnp.concatenate([w_in0[:, WIDTH_A:kv_a_end], w_in0[:, kv_a_end + WIDTH_B:]], axis=1).astype(bf16)
    g_pre = g_pre_mix[0].reshape(1, d)
    qa, ka, va, qb, kb, vb = _inproj(x.reshape(n_tok, d), mod3, g_pre, w_lat, _rope_tables(seq),
                                     seq=seq, mod_row0=0, latent=True)
    kac, vac, kbc, vbc = _inproj(ctx.reshape(batch * n_ctx, d), mod3, g_pre, w_ctx, None,
                                 seq=n_ctx, mod_row0=batch, latent=False)

    oa = _window_attention(qa, ka.reshape(batch, seq, -1), va.reshape(batch, seq, -1),
                           kac.reshape(batch, n_ctx, -1), vac.reshape(batch, n_ctx, -1),
                           sink_a[0].astype(jnp.float32) * LOG2E, batch=batch, seq=seq)
    tabs, cls_of_block = _nb_bias_tables(rpb_b[0], seq // GRID_W)
    ob = _neighbourhood_attention(qb, kb.reshape(batch, seq, -1), vb.reshape(batch, seq, -1),
                                  kbc.reshape(batch, n_ctx, -1), vbc.reshape(batch, n_ctx, -1),
                                  tabs, cls_of_block, batch=batch, seq=seq)

    w_out0 = w_out[0]
    w_o = jnp.concatenate([pair_heads(w_out0[:WIDTH_A], 0), w_out0[WIDTH_A:]], axis=0).astype(bf16)
    x1, h2, idx, wts, rank, size, base, counts = _outproj(
        oa, ob, x.reshape(n_tok, d), mod3,
        pair_heads(g_grp_a[0], 0).reshape(1, -1), g_grp_b[0].reshape(1, -1),
        g_post_mix[0].reshape(1, d), g_pre_ffn[0].reshape(1, d),
        w_o, w_router[0], b_router[0].reshape(1, -1), seq=seq)

    n_tok_tiles = n_tok // DISPATCH_TILE
    size = size.reshape(n_tok_tiles, N_EXPERTS)
    counts = counts.reshape(-1)
    padded = (counts + FFN_STEP - 1) // FFN_STEP * FFN_STEP
    pad_end = jnp.cumsum(padded).astype(jnp.int32)
    pad_start = pad_end - padded
    n_steps = n_tok * TOP_K // FFN_STEP + N_EXPERTS
    n_used = (pad_end[-1:] // FFN_STEP).astype(jnp.int32)
    step_row0 = jnp.arange(n_steps, dtype=jnp.int32) * FFN_STEP
    step_expert = jnp.minimum(jnp.sum(step_row0[:, None] >= pad_end[None, :], axis=1),
                              N_EXPERTS - 1).astype(jnp.int32)
    own = step_expert[:, None] == jnp.arange(N_EXPERTS, dtype=jnp.int32)
    step_rows = jnp.clip(jnp.sum(jnp.where(own, (pad_start + counts)[None, :], 0), axis=1) - step_row0,
                         0, FFN_STEP).astype(jnp.int32)
    fill_from = (pad_start + counts).astype(jnp.int32)
    local0 = jnp.cumsum(size, axis=1) - size
    global0 = pad_start[None, :] + base.reshape(n_tok_tiles, N_EXPERTS)
    tab = jnp.concatenate([local0, global0, size, jnp.sum(size, axis=1, keepdims=True),
                           jnp.zeros((n_tok_tiles, TAB_WIDTH - 3 * N_EXPERTS - 1), jnp.int32)], axis=1)
    tab = (tab * ROW_TILE).reshape(-1).astype(jnp.int32)
    hot = idx[:, :, None, :] == jnp.arange(N_EXPERTS, dtype=jnp.int32)[None, None, :, None]
    lpos_t = rank + jnp.sum(jnp.where(hot, local0[:, None, :, None], 0), axis=2)
    by_token = lambda a: jnp.swapaxes(a, 1, 2).reshape(n_tok, TOP_K)

    xs = _dispatch(h2, tab, lpos_t, fill_from * ROW_TILE, pad_end * ROW_TILE, n_rows=n_steps * FFN_STEP)
    ys = _ffn(xs, step_expert, step_rows, n_used, w_mlp1[0], b_mlp1[0], w_mlp2[0], b_mlp2[0])
    out = _combine(ys, tab, by_token(lpos_t), by_token(wts), x1, mod3, g_post_ffn[0].reshape(1, d), seq=seq)
    return out.reshape(batch, seq, d)
```

```python
import functools

import numpy as np
import jax
import jax.numpy as jnp
from jax import lax
from jax.experimental import pallas as pl
from jax.experimental.pallas import tpu as pltpu

GRID_W = 64
HEAD_DIM = 64
N_HEADS_A = 8
N_KV_A = 2
GQA_GROUP = N_HEADS_A // N_KV_A
N_HEADS_B = 8
WIDTH_A = N_HEADS_A * HEAD_DIM
WIDTH_KV_A = N_KV_A * HEAD_DIM
WIDTH_B = N_HEADS_B * HEAD_DIM
WINDOW = 128
BLOCK_A = 128
NA_ROWS = 8
NA_COLS = 16
N_EXPERTS = 32
TOP_K = 4
SWIGLU_LIMIT = 7.0
SWIGLU_ALPHA = 1.702
ROPE_BASE = 10000.0
EPS = 1e-6
NEG_INF = -1e30

LANES = 128
VMEM_LIMIT = 56 * 1024 * 1024

LOG2E = 1.4426950408889634
ATT_BLOCKS = 8
NB_Q_ROWS = 2
NB_K_ROWS = NB_Q_ROWS + NA_ROWS
PROJ_TILE = 1024
OUT_TILE = 256
OUT_PARTS = 2
STAGE_SLOTS = 2
FFN_TILE = 256
FFN_STEP = 4 * FFN_TILE
FFN_CHUNK = 256
FFN_GROUP = 4
DISPATCH_TILE = OUT_TILE
ROW_TILE = 8
TAB_WIDTH = 128


def _params(*sem):
    return pltpu.CompilerParams(dimension_semantics=sem, vmem_limit_bytes=VMEM_LIMIT)


def _rms(x, g):
    return x * lax.rsqrt(jnp.mean(x * x, axis=-1, keepdims=True) + EPS) * g


def _dot(a, b):
    return jnp.dot(a, b, preferred_element_type=jnp.float32)


def _dot_nt(a, b):
    return lax.dot_general(a, b, (((1,), (1,)), ((), ())), preferred_element_type=jnp.float32)


def _ada_kernel(c_ref, w_ref, b_ref, o_ref):
    c = c_ref[...]
    s = (c * jax.nn.sigmoid(c)).astype(jnp.bfloat16)
    o_ref[...] = _dot(s, w_ref[...].astype(jnp.bfloat16)) + b_ref[...]


def _ada(cc, w_ada, b_ada):
    rows, d = cc.shape
    n_out = w_ada.shape[1]
    return pl.pallas_call(
        _ada_kernel,
        grid=(n_out // d,),
        in_specs=[pl.BlockSpec((rows, d), lambda j: (0, 0)),
                  pl.BlockSpec((d, d), lambda j: (0, j)),
                  pl.BlockSpec((1, d), lambda j: (0, j))],
        out_specs=pl.BlockSpec((rows, d), lambda j: (0, j)),
        out_shape=jax.ShapeDtypeStruct((rows, n_out), jnp.float32),
        compiler_params=_params("arbitrary"),
        name="ada",
    )(cc, w_ada, b_ada.reshape(1, n_out))


def _rope(x, cos, sin):
    w = x.shape[1]
    reps = w // LANES
    if reps > 1:
        cos = jnp.concatenate([cos] * reps, axis=1)
        sin = jnp.concatenate([sin] * reps, axis=1)
    lane = lax.broadcasted_iota(jnp.int32, x.shape, 1)
    quarter = HEAD_DIM // 4
    partner = jnp.where(lane % (2 * quarter) < quarter,
                        pltpu.roll(x, w - quarter, 1), pltpu.roll(x, quarter, 1))
    return x * cos + partner * sin


def _inproj_kernel(x_ref, sh_ref, sc_ref, g_ref, w_ref, *rest, latent):
    x = x_ref[...]
    h = _rms(x, g_ref[...]) * (1.0 + sc_ref[0]) + sh_ref[0]
    p = _dot(h.astype(jnp.bfloat16), w_ref[...])
    if latent:
        cos_ref, sin_ref, qa_ref, ka_ref, va_ref, qb_ref, kb_ref, vb_ref = rest
        cos, sin = cos_ref[...], sin_ref[...]
        scale = HEAD_DIM ** -0.5 * LOG2E
        o = 0
        qa_ref[...] = (_rope(p[:, o:o + WIDTH_A], cos, sin) * scale).astype(qa_ref.dtype)
        o += WIDTH_A
        ka_ref[...] = _rope(p[:, o:o + WIDTH_KV_A], cos, sin).astype(ka_ref.dtype)
        o += WIDTH_KV_A
        va_ref[...] = p[:, o:o + WIDTH_KV_A].astype(va_ref.dtype)
        o += WIDTH_KV_A
        qb_ref[...] = (p[:, o:o + WIDTH_B] * scale).astype(qb_ref.dtype)
        o += WIDTH_B
    else:
        ka_ref, va_ref, kb_ref, vb_ref = rest
        o = 0
        ka_ref[...] = p[:, o:o + WIDTH_KV_A].astype(ka_ref.dtype)
        o += WIDTH_KV_A
        va_ref[...] = p[:, o:o + WIDTH_KV_A].astype(va_ref.dtype)
        o += WIDTH_KV_A
    kb_ref[...] = p[:, o:o + WIDTH_B].astype(kb_ref.dtype)
    o += WIDTH_B
    vb_ref[...] = p[:, o:o + WIDTH_B].astype(vb_ref.dtype)


def _inproj(x2d, mod3, g_pre, w, rope_tabs, *, seq, mod_row0, latent):
    t, d = x2d.shape
    tile = min(PROJ_TILE, seq)
    per_seq = seq // tile
    if latent:
        mod_row = lambda i: i // per_seq
    else:
        mod_row = lambda i: mod_row0
    in_specs = [pl.BlockSpec((tile, d), lambda i: (i, 0)),
                pl.BlockSpec((1, 1, d), lambda i: (mod_row(i), 0, 0)),
                pl.BlockSpec((1, 1, d), lambda i: (mod_row(i), 0, 1)),
                pl.BlockSpec((1, d), lambda i: (0, 0)),
                pl.BlockSpec(w.shape, lambda i: (0, 0))]
    args = [x2d, mod3, mod3, g_pre, w]
    widths = [WIDTH_KV_A, WIDTH_KV_A, WIDTH_B, WIDTH_B]
    if latent:
        in_specs += [pl.BlockSpec((tile, LANES), lambda i: (i % per_seq, 0))] * 2
        args += list(rope_tabs)
        widths = [WIDTH_A, WIDTH_KV_A, WIDTH_KV_A, WIDTH_B, WIDTH_B, WIDTH_B]
    return pl.pallas_call(
        functools.partial(_inproj_kernel, latent=latent),
        grid=(t // tile,),
        in_specs=in_specs,
        out_specs=[pl.BlockSpec((tile, wd), lambda i: (i, 0)) for wd in widths],
        out_shape=[jax.ShapeDtypeStruct((t, wd), jnp.bfloat16) for wd in widths],
        compiler_params=_params("parallel"),
        name="inproj_latent" if latent else "inproj_ctx",
    )(*args)


def _ones_beside(v):
    return jnp.concatenate([v, jnp.ones_like(v)], axis=1)


def _group_attention(q_ref, o_ref, rows, groups, s_ref, p_ref, m_ref):
    tq = rows.stop - rows.start
    low = lax.broadcasted_iota(jnp.int32, (tq, LANES), 1) < HEAD_DIM
    layout = []
    base = 0
    for pairs, k_loc, k_ctx, _, _, _, _ in groups:
        heads = [(j, 0) for j in pairs] + [(j, 1) for j in pairs]
        n_loc = k_loc.shape[0]
        stacked = []
        for j, half in heads:
            q = q_ref[rows, j * LANES:(j + 1) * LANES]
            stacked.append(jnp.where(low if half == 0 else ~low, q, jnp.zeros_like(q)))
        qs = jnp.concatenate(stacked, axis=0)
        n = len(heads) * tq
        s_ref[base:base + n, :n_loc] = _dot_nt(qs, k_loc)
        s_ref[base:base + n, n_loc:] = _dot_nt(qs, k_ctx)
        layout.append((base, heads, n_loc))
        base += n
    for (base, heads, n_loc), group in zip(layout, groups):
        bias_of, sink_of = group[5], group[6]
        for g, (j, half) in enumerate(heads):
            r = slice(base + g * tq, base + (g + 1) * tq)
            s_loc = s_ref[r, :n_loc] + bias_of(j, half)
            s_ctx = s_ref[r, n_loc:]
            m = jnp.maximum(jnp.max(s_loc, axis=1, keepdims=True), jnp.max(s_ctx, axis=1, keepdims=True))
            sink = sink_of(j, half)
            if sink is not None:
                m = jnp.maximum(m, sink)
            p_ref[r, :n_loc] = jnp.exp2((s_loc - m).astype(jnp.bfloat16))
            p_ref[r, n_loc:] = jnp.exp2((s_ctx - m).astype(jnp.bfloat16))
            m_ref[r, :] = m
    for (base, heads, n_loc), group in zip(layout, groups):
        pairs, _, _, v_loc, v_ctx, _, sink_of = group
        n = len(heads) * tq
        both = (_dot(p_ref[base:base + n, :n_loc], _ones_beside(v_loc))
                + _dot(p_ref[base:base + n, n_loc:], _ones_beside(v_ctx)))
        for a, j in enumerate(pairs):
            outs = []
            for half in range(2):
                g = half * len(pairs) + a
                r = slice(g * tq, (g + 1) * tq)
                den = both[r, LANES:]
                sink = sink_of(j, half)
                if sink is not None:
                    den = den + jnp.exp2(sink - m_ref[base + g * tq:base + (g + 1) * tq, :])
                outs.append(both[r, :LANES] / den)
            o_ref[rows, j * LANES:(j + 1) * LANES] = jnp.where(low, outs[0], outs[1]).astype(o_ref.dtype)


def _attention_scratch(tq, n_heads, n_keys):
    return [pltpu.VMEM((n_heads * tq, n_keys), jnp.float32), pltpu.VMEM((n_heads * tq, n_keys), jnp.bfloat16),
            pltpu.VMEM((n_heads * tq, 1), jnp.float32)]


def _win_kernel(sink_ref, q_ref, k_ref, v_ref, kc_ref, vc_ref, o_ref, *scratch, seq):
    span = BLOCK_A + 2 * WINDOW
    pairs = list(range(GQA_GROUP))
    for sb in range(ATT_BLOCKS):
        i = pl.program_id(1) * ATT_BLOCKS + sb
        start = pl.multiple_of(jnp.clip(i * BLOCK_A - WINDOW, 0, seq - span), BLOCK_A)
        qpos = i * BLOCK_A + lax.broadcasted_iota(jnp.int32, (BLOCK_A, span), 0)
        kpos = start + lax.broadcasted_iota(jnp.int32, (BLOCK_A, span), 1)
        bias = jnp.where(jnp.abs(kpos - qpos) <= WINDOW, 0.0, NEG_INF).astype(jnp.float32)
        groups = [([j], k_ref[0, pl.ds(start, span), :], kc_ref[0], v_ref[0, pl.ds(start, span), :], vc_ref[0],
                   lambda j, half: bias, lambda j, half: sink_ref[half * GQA_GROUP + j]) for j in pairs]
        _group_attention(q_ref, o_ref, slice(sb * BLOCK_A, (sb + 1) * BLOCK_A), groups, *scratch)


def _window_attention(qa, ka, va, kac, vac, sink, *, batch, seq):
    tq = ATT_BLOCKS * BLOCK_A
    nb = seq // tq
    n_ctx = kac.shape[1]
    return pl.pallas_call(
        functools.partial(_win_kernel, seq=seq),
        grid=(batch, nb),
        in_specs=[pl.BlockSpec(memory_space=pltpu.SMEM),
                  pl.BlockSpec((tq, WIDTH_A), lambda b, i: (b * nb + i, 0)),
                  pl.BlockSpec((1, seq, WIDTH_KV_A), lambda b, i: (b, 0, 0)),
                  pl.BlockSpec((1, seq, WIDTH_KV_A), lambda b, i: (b, 0, 0)),
                  pl.BlockSpec((1, n_ctx, WIDTH_KV_A), lambda b, i: (b, 0, 0)),
                  pl.BlockSpec((1, n_ctx, WIDTH_KV_A), lambda b, i: (b, 0, 0))],
        out_specs=pl.BlockSpec((tq, WIDTH_A), lambda b, i: (b * nb + i, 0)),
        out_shape=jax.ShapeDtypeStruct((batch * seq, WIDTH_A), jnp.bfloat16),
        scratch_shapes=_attention_scratch(BLOCK_A, N_HEADS_A, BLOCK_A + 2 * WINDOW + n_ctx),
        compiler_params=_params("parallel", "arbitrary"),
        name="window_attention",
    )(sink, qa, ka, va, kac, vac)


def _nb_kernel(q_ref, k_ref, v_ref, kc_ref, vc_ref, *rest, rows_n):
    tab_refs, o_ref, scratch = rest[:ATT_BLOCKS], rest[ATT_BLOCKS], rest[ATT_BLOCKS + 1:]
    n_keys = NB_K_ROWS * GRID_W
    tq = NB_Q_ROWS * GRID_W
    for sb in range(ATT_BLOCKS):
        m = pl.program_id(1) * ATT_BLOCKS + sb
        start_row = jnp.clip(NB_Q_ROWS * m - NA_ROWS // 2, 0, rows_n - NB_K_ROWS)
        start = pl.multiple_of(start_row * GRID_W, LANES)
        tab_ref = tab_refs[sb]
        groups = []
        for j in range(q_ref.shape[1] // LANES):
            cols = slice(j * LANES, (j + 1) * LANES)
            groups.append(([j], k_ref[0, pl.ds(start, n_keys), cols], kc_ref[0, :, cols],
                           v_ref[0, pl.ds(start, n_keys), cols], vc_ref[0, :, cols],
                           lambda j, half: tab_ref[0, 2 * j + half].astype(jnp.float32), lambda j, half: None))
        _group_attention(q_ref, o_ref, slice(sb * tq, (sb + 1) * tq), groups, *scratch)


def _nb_classes(rows_n):
    n_blocks = rows_n // NB_Q_ROWS
    sig = {}
    cls_of_block = []
    reps = []
    for m in range(n_blocks):
        start_row = int(np.clip(NB_Q_ROWS * m - NA_ROWS // 2, 0, rows_n - NB_K_ROWS))
        key = tuple((start_row - r, int(np.clip(r - NA_ROWS // 2, 0, rows_n - NA_ROWS)) - r)
                    for r in range(NB_Q_ROWS * m, NB_Q_ROWS * (m + 1)))
        if key not in sig:
            sig[key] = len(reps)
            reps.append(m)
        cls_of_block.append(sig[key])
    return np.asarray(cls_of_block, np.int32), reps


def _nb_bias_tables(rpb, rows_n):
    cls_of_block, reps = _nb_classes(rows_n)
    n_heads = rpb.shape[0]
    cq = np.arange(GRID_W)[:, None]
    ck = np.arange(GRID_W)[None, :]
    cs = np.clip(cq - NA_COLS // 2, 0, GRID_W - NA_COLS)
    col_ok = (ck >= cs) & (ck < cs + NA_COLS)
    pick = ((ck - cq + NA_COLS - 1)[None] == np.arange(2 * NA_COLS - 1)[:, None, None]) & col_ok[None]
    tiles = jnp.einsum('hrd,dqk->hrqk', rpb.astype(jnp.float32), jnp.asarray(pick, jnp.float32),
                       precision=lax.Precision.HIGHEST)
    tiles = jnp.where(jnp.asarray(col_ok)[None, None], tiles * LOG2E, NEG_INF).astype(jnp.bfloat16)
    blocked = jnp.full((n_heads, GRID_W, GRID_W), NEG_INF, jnp.bfloat16)
    tabs = []
    for m in reps:
        start_row = int(np.clip(NB_Q_ROWS * m - NA_ROWS // 2, 0, rows_n - NB_K_ROWS))
        q_rows = []
        for r in range(NB_Q_ROWS * m, NB_Q_ROWS * (m + 1)):
            rs = int(np.clip(r - NA_ROWS // 2, 0, rows_n - NA_ROWS))
            q_rows.append(jnp.concatenate(
                [tiles[:, krow - r + NA_ROWS - 1] if rs <= krow < rs + NA_ROWS else blocked
                 for krow in range(start_row, start_row + NB_K_ROWS)], axis=2))
        tabs.append(jnp.concatenate(q_rows, axis=1))
    return jnp.stack(tabs), cls_of_block


def _neighbourhood_attention(qb, kb, vb, kbc, vbc, tabs, cls_of_block, *, batch, seq):
    rows_n = seq // GRID_W
    tq = ATT_BLOCKS * NB_Q_ROWS * GRID_W
    nstep = seq // tq
    n_ctx = kbc.shape[1]

    def tab_spec(sb):
        return pl.BlockSpec((1,) + tabs.shape[1:], lambda b, m, c: (c[m * ATT_BLOCKS + sb], 0, 0, 0))

    grid_spec = pltpu.PrefetchScalarGridSpec(
        num_scalar_prefetch=1,
        grid=(batch, nstep),
        in_specs=[pl.BlockSpec((tq, WIDTH_B), lambda b, m, c: (b * nstep + m, 0)),
                  pl.BlockSpec((1, seq, WIDTH_B), lambda b, m, c: (b, 0, 0)),
                  pl.BlockSpec((1, seq, WIDTH_B), lambda b, m, c: (b, 0, 0)),
                  pl.BlockSpec((1, n_ctx, WIDTH_B), lambda b, m, c: (b, 0, 0)),
                  pl.BlockSpec((1, n_ctx, WIDTH_B), lambda b, m, c: (b, 0, 0))]
                 + [tab_spec(sb) for sb in range(ATT_BLOCKS)],
        out_specs=pl.BlockSpec((tq, WIDTH_B), lambda b, m, c: (b * nstep + m, 0)),
        scratch_shapes=_attention_scratch(NB_Q_ROWS * GRID_W, N_HEADS_B, NB_K_ROWS * GRID_W + n_ctx),
    )

    def body(c_ref, *refs):
        _nb_kernel(*refs, rows_n=rows_n)

    return pl.pallas_call(
        body,
        grid_spec=grid_spec,
        out_shape=jax.ShapeDtypeStruct((batch * seq, WIDTH_B), jnp.bfloat16),
        compiler_params=_params("parallel", "arbitrary"),
        name="neighbourhood_attention",
    )(jnp.asarray(cls_of_block), qb, kb, vb, kbc, vbc, *([tabs] * ATT_BLOCKS))


def _outproj_kernel(oa_ref, ob_ref, x_ref, ga_ref, sh_ref, sc_ref, gga_ref, ggb_ref, gpost_ref, gpre_ref,
                    wo_ref, wr_hi_ref, wr_lo_ref, br_ref,
                    x1_ref, h2_ref, idx_ref, wts_ref, rank_ref, size_ref, base_ref, cnt_ref, run_ref):
    i = pl.program_id(0)

    @pl.when(i == 0)
    def _():
        run_ref[...] = jnp.zeros_like(run_ref)

    n_rows = x_ref.shape[0]
    parts = [slice(p * n_rows // OUT_PARTS, (p + 1) * n_rows // OUT_PARTS) for p in range(OUT_PARTS)]
    na = [_rms(oa_ref[r, :].astype(jnp.float32), gga_ref[...]).astype(jnp.bfloat16) for r in parts]
    nb = [_rms(ob_ref[r, :].astype(jnp.float32), ggb_ref[...]).astype(jnp.bfloat16) for r in parts]
    mix = [_dot(a, wo_ref[:WIDTH_A, :]) + _dot(b, wo_ref[WIDTH_A:, :]) for a, b in zip(na, nb)]
    x1 = [x_ref[r, :] + ga_ref[0] * _rms(m, gpost_ref[...]) for r, m in zip(parts, mix)]
    for r, v in zip(parts, x1):
        x1_ref[r, :] = v
    h2 = [_rms(v, gpre_ref[...]) * (1.0 + sc_ref[0]) + sh_ref[0] for v in x1]
    for r, v in zip(parts, h2):
        h2_ref[r, :] = v.astype(h2_ref.dtype)

    h_hi = [v.astype(jnp.bfloat16) for v in h2]
    h_lo = [(v - hi.astype(jnp.float32)).astype(jnp.bfloat16) for v, hi in zip(h2, h_hi)]
    logits = jnp.concatenate([_dot(hi, wr_hi_ref[...]) + _dot(lo, wr_hi_ref[...]) + _dot(hi, wr_lo_ref[...])
                              for hi, lo in zip(h_hi, h_lo)], axis=0)
    work = logits.T[:N_EXPERTS] + br_ref[...]
    e, t = work.shape
    row = lax.broadcasted_iota(jnp.int32, (e, t), 0)
    chosen = jnp.zeros((e, t), jnp.float32)
    vals, sels, hots = [], [], []
    for _k in range(TOP_K):
        mx = jnp.max(work, axis=0, keepdims=True)
        sel = jnp.min(jnp.where(work == mx, row, e), axis=0, keepdims=True)
        hot = row == sel
        vals.append(mx)
        sels.append(sel)
        hots.append(hot)
        work = jnp.where(hot, -jnp.inf, work)
        chosen = chosen + hot.astype(jnp.float32)
    ex = [jnp.exp(v - vals[0]) for v in vals]
    den = ex[0] + ex[1] + ex[2] + ex[3]

    earlier = (lax.broadcasted_iota(jnp.int32, (t, t), 0) < lax.broadcasted_iota(jnp.int32, (t, t), 1))
    prefix = _dot(chosen.astype(jnp.bfloat16), earlier.astype(jnp.bfloat16))
    ranks = [jnp.sum(jnp.where(h, prefix, 0.0), axis=0, keepdims=True) for h in hots]
    size = jnp.sum(chosen, axis=1, keepdims=True).astype(jnp.int32)
    size_ref[0] = size
    base_ref[0] = run_ref[...]
    run_ref[...] = run_ref[...] + size
    cnt_ref[...] = run_ref[...]

    idx_ref[0] = jnp.concatenate(sels, axis=0)
    wts_ref[0] = jnp.concatenate([v / den for v in ex], axis=0)
    rank_ref[0] = jnp.concatenate(ranks, axis=0).astype(jnp.int32)


def _outproj(oa, ob, x2d, mod3, gga, ggb, gpost, gpre, w_out, w_router, b_router, *, seq):
    t, d = x2d.shape
    tile = OUT_TILE
    per_seq = seq // tile
    row = lambda i: (i, 0)
    const = lambda i: (0, 0)
    modspec = lambda part: pl.BlockSpec((1, 1, d), lambda i: (i // per_seq, 0, part))
    per_tile = lambda shape: pl.BlockSpec((1,) + shape, lambda i: (i, 0, 0))
    n_steps = t // tile
    w_pad = jnp.zeros((d, LANES), jnp.float32).at[:, :N_EXPERTS].set(w_router)
    wr_hi = w_pad.astype(jnp.bfloat16)
    wr_lo = (w_pad - wr_hi.astype(jnp.float32)).astype(jnp.bfloat16)
    outs = pl.pallas_call(
        _outproj_kernel,
        grid=(n_steps,),
        in_specs=[pl.BlockSpec((tile, WIDTH_A), row), pl.BlockSpec((tile, WIDTH_B), row),
                  pl.BlockSpec((tile, d), row),
                  modspec(2), modspec(3), modspec(4),
                  pl.BlockSpec((1, WIDTH_A), const), pl.BlockSpec((1, WIDTH_B), const),
                  pl.BlockSpec((1, d), const), pl.BlockSpec((1, d), const),
                  pl.BlockSpec(w_out.shape, const), pl.BlockSpec((d, LANES), const), pl.BlockSpec((d, LANES), const),
                  pl.BlockSpec((N_EXPERTS, 1), const)],
        out_specs=[pl.BlockSpec((tile, d), row), pl.BlockSpec((tile, d), row),
                   per_tile((TOP_K, tile)), per_tile((TOP_K, tile)), per_tile((TOP_K, tile)),
                   per_tile((N_EXPERTS, 1)), per_tile((N_EXPERTS, 1)),
                   pl.BlockSpec((N_EXPERTS, 1), const)],
        out_shape=[jax.ShapeDtypeStruct((t, d), jnp.float32), jax.ShapeDtypeStruct((t, d), jnp.bfloat16),
                   jax.ShapeDtypeStruct((n_steps, TOP_K, tile), jnp.int32),
                   jax.ShapeDtypeStruct((n_steps, TOP_K, tile), jnp.float32),
                   jax.ShapeDtypeStruct((n_steps, TOP_K, tile), jnp.int32),
                   jax.ShapeDtypeStruct((n_steps, N_EXPERTS, 1), jnp.int32),
                   jax.ShapeDtypeStruct((n_steps, N_EXPERTS, 1), jnp.int32),
                   jax.ShapeDtypeStruct((N_EXPERTS, 1), jnp.int32)],
        scratch_shapes=[pltpu.VMEM((N_EXPERTS, 1), jnp.int32)],
        compiler_params=_params("arbitrary"),
        name="outproj_router",
    )(oa, ob, x2d, mod3, mod3, mod3, gga, ggb, gpost, gpre, w_out, wr_hi, wr_lo, b_router.reshape(N_EXPERTS, 1))
    return outs


def _piece_table(tab_ref, e):
    return (pl.multiple_of(tab_ref[e], ROW_TILE), pl.multiple_of(tab_ref[N_EXPERTS + e], ROW_TILE),
            pl.multiple_of(tab_ref[2 * N_EXPERTS + e], ROW_TILE))


def _to_tiles(x, ref, index=()):
    for c in range(ROW_TILE):
        ref[index + (pl.ds(c, x.shape[0], stride=ROW_TILE), slice(None))] = x[:, c * LANES:(c + 1) * LANES]


def _from_tiles(ref, n_rows, index=()):
    return jnp.concatenate([ref[index + (pl.ds(c, n_rows, stride=ROW_TILE), slice(None))]
                            for c in range(ROW_TILE)], axis=1)


def _dispatch_kernel(fill_from_ref, pad_end_ref, tab_ref, h_ref, lpos_ref, xs_ref,
                     stage_ref, zero_ref, pend_ref, sems, zsem, *, grid_steps):
    i = pl.program_id(0)
    n_steps = pl.num_programs(0)
    n_slots = stage_ref.shape[0]
    slot = i % n_slots
    n_local = stage_ref.shape[1] // ROW_TILE
    fill_rows = zero_ref.shape[0]

    def fill(row):
        return pltpu.make_async_copy(zero_ref.at[pl.ds(0, fill_rows)],
                                     xs_ref.at[pl.ds(pl.multiple_of(row, fill_rows), fill_rows)], zsem)

    def pad_fill(e):
        n = pl.multiple_of(pad_end_ref[e] - fill_from_ref[e], ROW_TILE)
        return n, pltpu.make_async_copy(zero_ref.at[pl.ds(0, n)],
                                        xs_ref.at[pl.ds(pl.multiple_of(fill_from_ref[e], ROW_TILE), n)], zsem)

    def drain(s):
        n = pl.multiple_of(pend_ref[s], ROW_TILE)

        @pl.when(n > 0)
        def _():
            pltpu.make_async_copy(stage_ref.at[s, pl.ds(0, n)], xs_ref.at[pl.ds(0, n)], sems.at[s]).wait()
        pend_ref[s] = 0

    @pl.when(i == 0)
    def _():
        zero_ref[...] = jnp.zeros_like(zero_ref)
        for s in range(n_slots):
            pend_ref[s] = 0

    for j in range(-(-N_EXPERTS // grid_steps)):
        e = i * -(-N_EXPERTS // grid_steps) + j

        @pl.when(e < N_EXPERTS)
        def _(e=e):
            n, copy = pad_fill(e)

            @pl.when(n > 0)
            def _():
                copy.start()

    row = lax.broadcasted_iota(jnp.int32, (n_local, h_ref.shape[0]), 0)
    place = row == lpos_ref[0, 0:1, :]
    for k in range(1, TOP_K):
        place = place | (row == lpos_ref[0, k:k + 1, :])
    rows = _dot(place.astype(jnp.bfloat16), h_ref[...])

    drain(slot)
    _to_tiles(rows, stage_ref, (slot,))

    total = 0
    for e in range(N_EXPERTS):
        loc, glob, n = _piece_table(tab_ref, e)

        @pl.when(n > 0)
        def _(loc=loc, glob=glob, n=n, e=e):
            pltpu.make_async_copy(stage_ref.at[slot, pl.ds(loc, n)], xs_ref.at[pl.ds(glob, n)],
                                  sems.at[slot]).start(priority=e % 2)
        total = total + n
    pend_ref[slot] = total

    tail0 = pad_end_ref[N_EXPERTS - 1]
    n_tail = (xs_ref.shape[0] - tail0) // fill_rows
    per_step = (n_tail + n_steps - 1) // n_steps

    def start_tail(j, c):
        t = i * per_step + j

        @pl.when(t < n_tail)
        def _():
            fill(tail0 + t * fill_rows).start()
        return c
    lax.fori_loop(0, per_step, start_tail, 0)

    @pl.when(i == n_steps - 1)
    def _():
        for s in range(n_slots):
            drain(s)

        def wait_tail(j, c):
            fill(0).wait()
            return c
        lax.fori_loop(0, n_tail, wait_tail, 0)

        def wait_pad(e, c):
            n, copy = pad_fill(e)

            @pl.when(n > 0)
            def _():
                copy.wait()
            return c
        lax.fori_loop(0, N_EXPERTS, wait_pad, 0)


def _dispatch(h2, tab_flat, lpos_t, fill_from, pad_end, *, n_rows):
    t, d = h2.shape
    tile = DISPATCH_TILE
    assert d == ROW_TILE * LANES
    n_local = tile * TOP_K
    grid_spec = pltpu.PrefetchScalarGridSpec(
        num_scalar_prefetch=2,
        grid=(t // tile,),
        in_specs=[pl.BlockSpec((TAB_WIDTH,), lambda i, *_: (i,), memory_space=pltpu.SMEM),
                  pl.BlockSpec((tile, d), lambda i, *_: (i, 0)),
                  pl.BlockSpec((1, TOP_K, tile), lambda i, *_: (i, 0, 0))],
        out_specs=pl.BlockSpec(memory_space=pl.ANY),
        scratch_shapes=[pltpu.VMEM((STAGE_SLOTS, n_local * ROW_TILE, LANES), jnp.float32),
                        pltpu.VMEM((FFN_STEP * ROW_TILE, LANES), jnp.float32),
                        pltpu.SMEM((STAGE_SLOTS,), jnp.int32),
                        pltpu.SemaphoreType.DMA((STAGE_SLOTS,)), pltpu.SemaphoreType.DMA(())],
    )
    return pl.pallas_call(
        functools.partial(_dispatch_kernel, grid_steps=t // tile),
        grid_spec=grid_spec,
        out_shape=jax.ShapeDtypeStruct((n_rows * ROW_TILE, LANES), jnp.float32),
        compiler_params=_params("arbitrary"),
        name="dispatch",
    )(fill_from, pad_end, tab_flat, h2, lpos_t)


def _ffn_kernel(te_ref, rows_ref, n_used_ref, x_ref, w1_ref, b1_ref, w2_ref, b2_ref, y_ref,
                w1p_ref, w2b_ref, act_ref):
    i = pl.program_id(0)
    live = i < n_used_ref[0]
    n_pair = w1_ref.shape[2] // FFN_CHUNK
    half = FFN_CHUNK // 2
    n_sub = jnp.where(live, (rows_ref[i] + FFN_TILE - 1) // FFN_TILE, 0)

    @pl.when(live & ((i == 0) | (te_ref[i] != te_ref[jnp.maximum(i - 1, 0)])))
    def _():
        src = lax.broadcasted_iota(jnp.int32, (FFN_CHUNK, FFN_CHUNK), 0)
        dst = lax.broadcasted_iota(jnp.int32, (FFN_CHUNK, FFN_CHUNK), 1)
        unzip = (src == jnp.where(dst < half, 2 * dst, 2 * (dst - half) + 1)).astype(jnp.bfloat16)
        for c in range(n_pair):
            cs = slice(c * FFN_CHUNK, (c + 1) * FFN_CHUNK)
            w1p_ref[:, cs] = _dot(w1_ref[0, :, cs].astype(jnp.bfloat16), unzip).astype(jnp.bfloat16)
        w2b_ref[...] = w2_ref[0].astype(jnp.bfloat16)

    def compute(n_rows):
        x = _from_tiles(x_ref, n_rows).astype(jnp.bfloat16)
        for c in range(0, n_pair, FFN_GROUP):
            cs = slice(c * FFN_CHUNK, (c + FFN_GROUP) * FFN_CHUNK)
            h = _dot(x, w1p_ref[:, cs]) + b1_ref[0, :, cs]
            g = jnp.concatenate([h[:, k * FFN_CHUNK:k * FFN_CHUNK + half] for k in range(FFN_GROUP)], axis=1)
            u = jnp.concatenate([h[:, k * FFN_CHUNK + half:(k + 1) * FFN_CHUNK] for k in range(FFN_GROUP)], axis=1)
            g = jnp.minimum(g, SWIGLU_LIMIT)
            u = jnp.clip(u, -SWIGLU_LIMIT, SWIGLU_LIMIT)
            act = g * jax.nn.sigmoid(SWIGLU_ALPHA * g) * (u + 1.0)
            act_ref[:n_rows, c * half:(c + FFN_GROUP) * half] = act.astype(act_ref.dtype)
        _to_tiles(_dot(act_ref[:n_rows, :], w2b_ref[...]) + b2_ref[0], y_ref)
        if n_rows * ROW_TILE < y_ref.shape[0]:
            y_ref[n_rows * ROW_TILE:, :] = jnp.zeros((y_ref.shape[0] - n_rows * ROW_TILE, LANES), y_ref.dtype)

    for tiles in range(1, FFN_STEP // FFN_TILE + 1):
        pl.when(n_sub == tiles)(functools.partial(compute, tiles * FFN_TILE))

    @pl.when(n_sub == 0)
    def _():
        y_ref[...] = jnp.zeros_like(y_ref)


def _unzip_bias(b1):
    e, f2 = b1.shape
    half = FFN_CHUNK // 2
    return b1.reshape(e, f2 // FFN_CHUNK, half, 2).transpose(0, 1, 3, 2).reshape(e, 1, f2)


def _ffn(xs, step_expert, step_rows, n_used, w1, b1, w2, b2):
    d, f2 = w1.shape[1:]
    n_steps = xs.shape[0] // (FFN_STEP * ROW_TILE)
    live = lambda i, nu: jnp.minimum(i, nu[0] - 1)
    wspec = lambda shape: pl.BlockSpec((1,) + shape, lambda i, te, nr, nu: (te[live(i, nu)], 0, 0))
    grid_spec = pltpu.PrefetchScalarGridSpec(
        num_scalar_prefetch=3,
        grid=(n_steps,),
        in_specs=[pl.BlockSpec((FFN_STEP * ROW_TILE, LANES), lambda i, te, nr, nu: (live(i, nu), 0)),
                  wspec((d, f2)), wspec((1, f2)), wspec((f2 // 2, d)), wspec((1, d))],
        out_specs=pl.BlockSpec((FFN_STEP * ROW_TILE, LANES), lambda i, te, nr, nu: (i, 0)),
        scratch_shapes=[pltpu.VMEM((d, f2), jnp.bfloat16), pltpu.VMEM((f2 // 2, d), jnp.bfloat16),
                        pltpu.VMEM((FFN_STEP, f2 // 2), jnp.bfloat16)],
    )
    return pl.pallas_call(
        _ffn_kernel,
        grid_spec=grid_spec,
        out_shape=jax.ShapeDtypeStruct(xs.shape, jnp.float32),
        compiler_params=_params("arbitrary"),
        name="expert_ffn",
    )(step_expert, step_rows, n_used, xs, w1, _unzip_bias(b1), w2, b2[:, None, :])


def _combine_kernel(tab_ref, next_tab_ref, lpos_ref, wts_ref, x1_ref, ga_ref, g_ref, y_ref, o_ref, buf_ref, sems):
    i = pl.program_id(0)
    n_steps = pl.num_programs(0)
    slot = i % 2
    tile = x1_ref.shape[0]
    n_local = buf_ref.shape[1] // ROW_TILE

    def fetch(tab, s):
        for e in range(N_EXPERTS):
            loc, glob, n = _piece_table(tab, e)

            @pl.when(n > 0)
            def _(loc=loc, glob=glob, n=n, e=e):
                pltpu.make_async_copy(y_ref.at[pl.ds(glob, n)], buf_ref.at[s, pl.ds(loc, n)],
                                      sems.at[s]).start(priority=e % 2)

    @pl.when(i == 0)
    def _():
        fetch(tab_ref, 0)

    @pl.when(i + 1 < n_steps)
    def _():
        fetch(next_tab_ref, 1 - slot)

    n = pl.multiple_of(tab_ref[3 * N_EXPERTS], ROW_TILE)

    @pl.when(n > 0)
    def _():
        pltpu.make_async_copy(y_ref.at[pl.ds(0, n)], buf_ref.at[slot, pl.ds(0, n)], sems.at[slot]).wait()

    rows = _from_tiles(buf_ref, n_local, (slot,)).astype(jnp.bfloat16)
    part = tile // OUT_PARTS
    col = lax.broadcasted_iota(jnp.int32, (part, n_local), 1)
    for p in range(OUT_PARTS):
        r = slice(p * part, (p + 1) * part)
        pos = lpos_ref[r, :]
        w = wts_ref[r, :]
        pick = jnp.where(col == pos[:, 0:1], w[:, 0:1], 0.0)
        for k in range(1, TOP_K):
            pick = pick + jnp.where(col == pos[:, k:k + 1], w[:, k:k + 1], 0.0)
        hi = pick.astype(jnp.bfloat16)
        lo = (pick - hi.astype(jnp.float32)).astype(jnp.bfloat16)
        y = _dot(hi, rows) + _dot(lo, rows)
        o_ref[r, :] = x1_ref[r, :] + ga_ref[0] * _rms(y, g_ref[...])


def _combine(ys, tab_flat, lpos, wts, x1, mod3, g_post, *, seq):
    t, d = x1.shape
    tile = DISPATCH_TILE
    n_steps = t // tile
    per_seq = seq // tile
    n_local = tile * TOP_K
    return pl.pallas_call(
        _combine_kernel,
        grid=(n_steps,),
        in_specs=[pl.BlockSpec((TAB_WIDTH,), lambda i: (i,), memory_space=pltpu.SMEM),
                  pl.BlockSpec((TAB_WIDTH,), lambda i: (jnp.minimum(i + 1, n_steps - 1),), memory_space=pltpu.SMEM),
                  pl.BlockSpec((tile, TOP_K), lambda i: (i, 0)),
                  pl.BlockSpec((tile, TOP_K), lambda i: (i, 0)),
                  pl.BlockSpec((tile, d), lambda i: (i, 0)),
                  pl.BlockSpec((1, 1, d), lambda i: (i // per_seq, 0, 5)),
                  pl.BlockSpec((1, d), lambda i: (0, 0)),
                  pl.BlockSpec(memory_space=pl.ANY)],
        out_specs=pl.BlockSpec((tile, d), lambda i: (i, 0)),
        out_shape=jax.ShapeDtypeStruct((t, d), jnp.float32),
        scratch_shapes=[pltpu.VMEM((2, n_local * ROW_TILE, LANES), ys.dtype), pltpu.SemaphoreType.DMA((2,))],
        compiler_params=_params("arbitrary"),
        name="combine",
    )(tab_flat, tab_flat, lpos, wts, x1, mod3, g_post, ys)


def _rope_tables(seq):
    pos = np.arange(seq)
    n_freq = HEAD_DIM // 4
    freqs = ROPE_BASE ** (-jnp.arange(n_freq, dtype=jnp.float32) / n_freq)
    rows = jnp.asarray(pos // GRID_W, jnp.float32)[:, None] * freqs[None, :]
    cols = jnp.asarray(pos % GRID_W, jnp.float32)[:, None] * freqs[None, :]
    ang = jnp.concatenate([rows, rows, cols, cols], axis=1)
    sign = np.tile(np.repeat([-1.0, 1.0], n_freq), 2).astype(np.float32)
    cos = jnp.cos(ang)
    sin = jnp.sin(ang) * sign[None, :]
    reps = LANES // HEAD_DIM
    return jnp.tile(cos, (1, reps)), jnp.tile(sin, (1, reps))


def kernel(x, c, ctx, c_ctx, w_ada, b_ada, g_pre_mix, g_post_mix, g_pre_ffn, g_post_ffn, w_in, g_grp_a, g_grp_b,
           sink_a, rpb_b, w_out, w_router, b_router, w_mlp1, b_mlp1, w_mlp2, b_mlp2):
    batch, seq, d = x.shape
    n_ctx = ctx.shape[1]
    assert w_ada.shape[0] == 1, "single layer"
    assert seq % (ATT_BLOCKS * GRID_W * NB_Q_ROWS) == 0 and seq // GRID_W >= NB_K_ROWS
    assert seq % (ATT_BLOCKS * BLOCK_A) == 0 and seq >= BLOCK_A + 2 * WINDOW
    n_tok = batch * seq
    bf16 = jnp.bfloat16

    def pair_heads(a, axis):
        shape = a.shape
        a = a.reshape(shape[:axis] + (N_KV_A, GQA_GROUP, HEAD_DIM) + shape[axis + 1:])
        return jnp.swapaxes(a, axis, axis + 1).reshape(shape)

    mod_rows = -(-(batch + 1) // 8) * 8
    cc = jnp.zeros((mod_rows, d), jnp.float32).at[:batch].set(c).at[batch].set(c_ctx)
    mod3 = _ada(cc, w_ada[0], b_ada[0]).reshape(mod_rows, 1, 6 * d)

    w_in0 = w_in[0]
    w_lat = jnp.concatenate([pair_heads(w_in0[:, :WIDTH_A], 1), w_in0[:, WIDTH_A:]], axis=1).astype(bf16)
    kv_a_end = WIDTH_A + 2 * WIDTH_KV_A
    w_ctx = jnp.concatenate([w_in0[:, WIDTH_A:kv_a_end], w_in0[:, kv_a_end + WIDTH_B:]], axis=1).astype(bf16)
    g_pre = g_pre_mix[0].reshape(1, d)
    qa, ka, va, qb, kb, vb = _inproj(x.reshape(n_tok, d), mod3, g_pre, w_lat, _rope_tables(seq),
                                     seq=seq, mod_row0=0, latent=True)
    kac, vac, kbc, vbc = _inproj(ctx.reshape(batch * n_ctx, d), mod3, g_pre, w_ctx, None,
                                 seq=n_ctx, mod_row0=batch, latent=False)

    oa = _window_attention(qa, ka.reshape(batch, seq, -1), va.reshape(batch, seq, -1),
                           kac.reshape(batch, n_ctx, -1), vac.reshape(batch, n_ctx, -1),
                           sink_a[0].astype(jnp.float32) * LOG2E, batch=batch, seq=seq)
    tabs, cls_of_block = _nb_bias_tables(rpb_b[0], seq // GRID_W)
    ob = _neighbourhood_attention(qb, kb.reshape(batch, seq, -1), vb.reshape(batch, seq, -1),
                                  kbc.reshape(batch, n_ctx, -1), vbc.reshape(batch, n_ctx, -1),
                                  tabs, cls_of_block, batch=batch, seq=seq)

    w_out0 = w_out[0]
    w_o = jnp.concatenate([pair_heads(w_out0[:WIDTH_A], 0), w_out0[WIDTH_A:]], axis=0).astype(bf16)
    x1, h2, idx, wts, rank, size, base, counts = _outproj(
        oa, ob, x.reshape(n_tok, d), mod3,
        pair_heads(g_grp_a[0], 0).reshape(1, -1), g_grp_b[0].reshape(1, -1),
        g_post_mix[0].reshape(1, d), g_pre_ffn[0].reshape(1, d),
        w_o, w_router[0], b_router[0].reshape(1, -1), seq=seq)

    n_tok_tiles = n_tok // DISPATCH_TILE
    size = size.reshape(n_tok_tiles, N_EXPERTS)
    counts = counts.reshape(-1)
    padded = (counts + FFN_STEP - 1) // FFN_STEP * FFN_STEP
    pad_end = jnp.cumsum(padded).astype(jnp.int32)
    pad_start = pad_end - padded
    n_steps = n_tok * TOP_K // FFN_STEP + N_EXPERTS
    n_used = (pad_end[-1:] // FFN_STEP).astype(jnp.int32)
    step_row0 = jnp.arange(n_steps, dtype=jnp.int32) * FFN_STEP
    step_expert = jnp.minimum(jnp.sum(step_row0[:, None] >= pad_end[None, :], axis=1),
                              N_EXPERTS - 1).astype(jnp.int32)
    own = step_expert[:, None] == jnp.arange(N_EXPERTS, dtype=jnp.int32)
    step_rows = jnp.clip(jnp.sum(jnp.where(own, (pad_start + counts)[None, :], 0), axis=1) - step_row0,
                         0, FFN_STEP).astype(jnp.int32)
    fill_from = (pad_start + counts).astype(jnp.int32)
    local0 = jnp.cumsum(size, axis=1) - size
    global0 = pad_start[None, :] + base.reshape(n_tok_tiles, N_EXPERTS)
    tab = jnp.concatenate([local0, global0, size, jnp.sum(size, axis=1, keepdims=True),
                           jnp.zeros((n_tok_tiles, TAB_WIDTH - 3 * N_EXPERTS - 1), jnp.int32)], axis=1)
    tab = (tab * ROW_TILE).reshape(-1).astype(jnp.int32)
    hot = idx[:, :, None, :] == jnp.arange(N_EXPERTS, dtype=jnp.int32)[None, None, :, None]
    lpos_t = rank + jnp.sum(jnp.where(hot, local0[:, None, :, None], 0), axis=2)
    by_token = lambda a: jnp.swapaxes(a, 1, 2).reshape(n_tok, TOP_K)

    xs = _dispatch(h2, tab, lpos_t, fill_from * ROW_TILE, pad_end * ROW_TILE, n_rows=n_steps * FFN_STEP)
    ys = _ffn(xs, step_expert, step_rows, n_used, w_mlp1[0], b_mlp1[0], w_mlp2[0], b_mlp2[0])
    out = _combine(ys, tab, by_token(lpos_t), by_token(wts), x1, mod3, g_post_ffn[0].reshape(1, d), seq=seq)
    return out.reshape(batch, seq, d)
```

```python
import functools

import numpy as np
import jax
import jax.numpy as jnp
from jax import lax
from jax.experimental import pallas as pl
from jax.experimental.pallas import tpu as pltpu

GRID_W = 64
HEAD_DIM = 64
N_HEADS_A = 8
N_KV_A = 2
GQA_GROUP = N_HEADS_A // N_KV_A
N_HEADS_B = 8
WIDTH_A = N_HEADS_A * HEAD_DIM
WIDTH_KV_A = N_KV_A * HEAD_DIM
WIDTH_B = N_HEADS_B * HEAD_DIM
WINDOW = 128
BLOCK_A = 128
NA_ROWS = 8
NA_COLS = 16
N_EXPERTS = 32
TOP_K = 4
SWIGLU_LIMIT = 7.0
SWIGLU_ALPHA = 1.702
ROPE_BASE = 10000.0
EPS = 1e-6
NEG_INF = -1e30

LANES = 128
VMEM_LIMIT = 56 * 1024 * 1024

LOG2E = 1.4426950408889634
ATT_BLOCKS = 8
NB_Q_ROWS = 2
NB_K_ROWS = NB_Q_ROWS + NA_ROWS
PROJ_TILE = 1024
OUT_TILE = 256
OUT_PARTS = 2
STAGE_SLOTS = 3
FFN_TILE = 256
FFN_STEP = 4 * FFN_TILE
FFN_CHUNK = 256
FFN_GROUP = 4
DISPATCH_TILE = OUT_TILE
ROW_TILE = 8
TAB_WIDTH = 128


def _params(*sem):
    return pltpu.CompilerParams(dimension_semantics=sem, vmem_limit_bytes=VMEM_LIMIT)


def _rms(x, g):
    return x * lax.rsqrt(jnp.mean(x * x, axis=-1, keepdims=True) + EPS) * g


def _dot(a, b):
    return jnp.dot(a, b, preferred_element_type=jnp.float32)


def _dot_nt(a, b):
    return lax.dot_general(a, b, (((1,), (1,)), ((), ())), preferred_element_type=jnp.float32)


def _ada_kernel(c_ref, w_ref, b_ref, o_ref):
    c = c_ref[...]
    s = (c * jax.nn.sigmoid(c)).astype(jnp.bfloat16)
    o_ref[...] = _dot(s, w_ref[...].astype(jnp.bfloat16)) + b_ref[...]


def _ada(cc, w_ada, b_ada):
    rows, d = cc.shape
    n_out = w_ada.shape[1]
    return pl.pallas_call(
        _ada_kernel,
        grid=(n_out // d,),
        in_specs=[pl.BlockSpec((rows, d), lambda j: (0, 0)),
                  pl.BlockSpec((d, d), lambda j: (0, j)),
                  pl.BlockSpec((1, d), lambda j: (0, j))],
        out_specs=pl.BlockSpec((rows, d), lambda j: (0, j)),
        out_shape=jax.ShapeDtypeStruct((rows, n_out), jnp.float32),
        compiler_params=_params("arbitrary"),
        name="ada",
    )(cc, w_ada, b_ada.reshape(1, n_out))


def _rope(x, cos, sin):
    w = x.shape[1]
    reps = w // LANES
    if reps > 1:
        cos = jnp.concatenate([cos] * reps, axis=1)
        sin = jnp.concatenate([sin] * reps, axis=1)
    lane = lax.broadcasted_iota(jnp.int32, x.shape, 1)
    quarter = HEAD_DIM // 4
    partner = jnp.where(lane % (2 * quarter) < quarter,
                        pltpu.roll(x, w - quarter, 1), pltpu.roll(x, quarter, 1))
    return x * cos + partner * sin


def _inproj_kernel(x_ref, sh_ref, sc_ref, g_ref, w_ref, *rest, latent):
    x = x_ref[...]
    h = _rms(x, g_ref[...]) * (1.0 + sc_ref[0]) + sh_ref[0]
    p = _dot(h.astype(jnp.bfloat16), w_ref[...])
    if latent:
        cos_ref, sin_ref, qa_ref, ka_ref, va_ref, qb_ref, kb_ref, vb_ref = rest
        cos, sin = cos_ref[...], sin_ref[...]
        scale = HEAD_DIM ** -0.5 * LOG2E
        o = 0
        qa_ref[...] = (_rope(p[:, o:o + WIDTH_A], cos, sin) * scale).astype(qa_ref.dtype)
        o += WIDTH_A
        ka_ref[...] = _rope(p[:, o:o + WIDTH_KV_A], cos, sin).astype(ka_ref.dtype)
        o += WIDTH_KV_A
        va_ref[...] = p[:, o:o + WIDTH_KV_A].astype(va_ref.dtype)
        o += WIDTH_KV_A
        qb_ref[...] = (p[:, o:o + WIDTH_B] * scale).astype(qb_ref.dtype)
        o += WIDTH_B
    else:
        ka_ref, va_ref, kb_ref, vb_ref = rest
        o = 0
        ka_ref[...] = p[:, o:o + WIDTH_KV_A].astype(ka_ref.dtype)
        o += WIDTH_KV_A
        va_ref[...] = p[:, o:o + WIDTH_KV_A].astype(va_ref.dtype)
        o += WIDTH_KV_A
    kb_ref[...] = p[:, o:o + WIDTH_B].astype(kb_ref.dtype)
    o += WIDTH_B
    vb_ref[...] = p[:, o:o + WIDTH_B].astype(vb_ref.dtype)


def _inproj(x2d, mod3, g_pre, w, rope_tabs, *, seq, mod_row0, latent):
    t, d = x2d.shape
    tile = min(PROJ_TILE, seq)
    per_seq = seq // tile
    if latent:
        mod_row = lambda i: i // per_seq
    else:
        mod_row = lambda i: mod_row0
    in_specs = [pl.BlockSpec((tile, d), lambda i: (i, 0)),
                pl.BlockSpec((1, 1, d), lambda i: (mod_row(i), 0, 0)),
                pl.BlockSpec((1, 1, d), lambda i: (mod_row(i), 0, 1)),
                pl.BlockSpec((1, d), lambda i: (0, 0)),
                pl.BlockSpec(w.shape, lambda i: (0, 0))]
    args = [x2d, mod3, mod3, g_pre, w]
    widths = [WIDTH_KV_A, WIDTH_KV_A, WIDTH_B, WIDTH_B]
    if latent:
        in_specs += [pl.BlockSpec((tile, LANES), lambda i: (i % per_seq, 0))] * 2
        args += list(rope_tabs)
        widths = [WIDTH_A, WIDTH_KV_A, WIDTH_KV_A, WIDTH_B, WIDTH_B, WIDTH_B]
    return pl.pallas_call(
        functools.partial(_inproj_kernel, latent=latent),
        grid=(t // tile,),
        in_specs=in_specs,
        out_specs=[pl.BlockSpec((tile, wd), lambda i: (i, 0)) for wd in widths],
        out_shape=[jax.ShapeDtypeStruct((t, wd), jnp.bfloat16) for wd in widths],
        compiler_params=_params("parallel"),
        name="inproj_latent" if latent else "inproj_ctx",
    )(*args)


def _ones_beside(v):
    return jnp.concatenate([v, jnp.ones_like(v)], axis=1)


def _group_attention(q_ref, o_ref, rows, groups, s_ref, p_ref, m_ref):
    tq = rows.stop - rows.start
    low = lax.broadcasted_iota(jnp.int32, (tq, LANES), 1) < HEAD_DIM
    layout = []
    base = 0
    for pairs, k_loc, k_ctx, _, _, _, _ in groups:
        heads = [(j, 0) for j in pairs] + [(j, 1) for j in pairs]
        n_loc = k_loc.shape[0]
        stacked = []
        for j, half in heads:
            q = q_ref[rows, j * LANES:(j + 1) * LANES]
            stacked.append(jnp.where(low if half == 0 else ~low, q, jnp.zeros_like(q)))
        qs = jnp.concatenate(stacked, axis=0)
        n = len(heads) * tq
        s_ref[base:base + n, :n_loc] = _dot_nt(qs, k_loc)
        s_ref[base:base + n, n_loc:] = _dot_nt(qs, k_ctx)
        layout.append((base, heads, n_loc))
        base += n
    for (base, heads, n_loc), group in zip(layout, groups):
        bias_of, sink_of = group[5], group[6]
        for g, (j, half) in enumerate(heads):
            r = slice(base + g * tq, base + (g + 1) * tq)
            s_loc = s_ref[r, :n_loc] + bias_of(j, half)
            s_ctx = s_ref[r, n_loc:]
            m = jnp.maximum(jnp.max(s_loc, axis=1, keepdims=True), jnp.max(s_ctx, axis=1, keepdims=True))
            sink = sink_of(j, half)
            if sink is not None:
                m = jnp.maximum(m, sink)
            p_ref[r, :n_loc] = jnp.exp2((s_loc - m).astype(jnp.bfloat16))
            p_ref[r, n_loc:] = jnp.exp2((s_ctx - m).astype(jnp.bfloat16))
            m_ref[r, :] = m
    for (base, heads, n_loc), group in zip(layout, groups):
        pairs, _, _, v_loc, v_ctx, _, sink_of = group
        n = len(heads) * tq
        both = (_dot(p_ref[base:base + n, :n_loc], _ones_beside(v_loc))
                + _dot(p_ref[base:base + n, n_loc:], _ones_beside(v_ctx)))
        for a, j in enumerate(pairs):
            outs = []
            for half in range(2):
                g = half * len(pairs) + a
                r = slice(g * tq, (g + 1) * tq)
                den = both[r, LANES:]
                sink = sink_of(j, half)
                if sink is not None:
                    den = den + jnp.exp2(sink - m_ref[base + g * tq:base + (g + 1) * tq, :])
                outs.append(both[r, :LANES] / den)
            o_ref[rows, j * LANES:(j + 1) * LANES] = jnp.where(low, outs[0], outs[1]).astype(o_ref.dtype)


def _attention_scratch(tq, n_heads, n_keys):
    return [pltpu.VMEM((n_heads * tq, n_keys), jnp.float32), pltpu.VMEM((n_heads * tq, n_keys), jnp.bfloat16),
            pltpu.VMEM((n_heads * tq, 1), jnp.float32)]


def _win_kernel(sink_ref, q_ref, k_ref, v_ref, kc_ref, vc_ref, o_ref, *scratch, seq):
    span = BLOCK_A + 2 * WINDOW
    pairs = list(range(GQA_GROUP))
    for sb in range(ATT_BLOCKS):
        i = pl.program_id(1) * ATT_BLOCKS + sb
        start = pl.multiple_of(jnp.clip(i * BLOCK_A - WINDOW, 0, seq - span), BLOCK_A)
        qpos = i * BLOCK_A + lax.broadcasted_iota(jnp.int32, (BLOCK_A, span), 0)
        kpos = start + lax.broadcasted_iota(jnp.int32, (BLOCK_A, span), 1)
        bias = jnp.where(jnp.abs(kpos - qpos) <= WINDOW, 0.0, NEG_INF).astype(jnp.float32)
        groups = [([j], k_ref[0, pl.ds(start, span), :], kc_ref[0], v_ref[0, pl.ds(start, span), :], vc_ref[0],
                   lambda j, half: bias, lambda j, half: sink_ref[half * GQA_GROUP + j]) for j in pairs]
        _group_attention(q_ref, o_ref, slice(sb * BLOCK_A, (sb + 1) * BLOCK_A), groups, *scratch)


def _window_attention(qa, ka, va, kac, vac, sink, *, batch, seq):
    tq = ATT_BLOCKS * BLOCK_A
    nb = seq // tq
    n_ctx = kac.shape[1]
    return pl.pallas_call(
        functools.partial(_win_kernel, seq=seq),
        grid=(batch, nb),
        in_specs=[pl.BlockSpec(memory_space=pltpu.SMEM),
                  pl.BlockSpec((tq, WIDTH_A), lambda b, i: (b * nb + i, 0)),
                  pl.BlockSpec((1, seq, WIDTH_KV_A), lambda b, i: (b, 0, 0)),
                  pl.BlockSpec((1, seq, WIDTH_KV_A), lambda b, i: (b, 0, 0)),
                  pl.BlockSpec((1, n_ctx, WIDTH_KV_A), lambda b, i: (b, 0, 0)),
                  pl.BlockSpec((1, n_ctx, WIDTH_KV_A), lambda b, i: (b, 0, 0))],
        out_specs=pl.BlockSpec((tq, WIDTH_A), lambda b, i: (b * nb + i, 0)),
        out_shape=jax.ShapeDtypeStruct((batch * seq, WIDTH_A), jnp.bfloat16),
        scratch_shapes=_attention_scratch(BLOCK_A, N_HEADS_A, BLOCK_A + 2 * WINDOW + n_ctx),
        compiler_params=_params("parallel", "arbitrary"),
        name="window_attention",
    )(sink, qa, ka, va, kac, vac)


def _nb_kernel(q_ref, k_ref, v_ref, kc_ref, vc_ref, *rest, rows_n):
    tab_refs, o_ref, scratch = rest[:ATT_BLOCKS], rest[ATT_BLOCKS], rest[ATT_BLOCKS + 1:]
    n_keys = NB_K_ROWS * GRID_W
    tq = NB_Q_ROWS * GRID_W
    for sb in range(ATT_BLOCKS):
        m = pl.program_id(1) * ATT_BLOCKS + sb
        start_row = jnp.clip(NB_Q_ROWS * m - NA_ROWS // 2, 0, rows_n - NB_K_ROWS)
        start = pl.multiple_of(start_row * GRID_W, LANES)
        tab_ref = tab_refs[sb]
        groups = []
        for j in range(q_ref.shape[1] // LANES):
            cols = slice(j * LANES, (j + 1) * LANES)
            groups.append(([j], k_ref[0, pl.ds(start, n_keys), cols], kc_ref[0, :, cols],
                           v_ref[0, pl.ds(start, n_keys), cols], vc_ref[0, :, cols],
                           lambda j, half: tab_ref[0, 2 * j + half].astype(jnp.float32), lambda j, half: None))
        _group_attention(q_ref, o_ref, slice(sb * tq, (sb + 1) * tq), groups, *scratch)


def _nb_classes(rows_n):
    n_blocks = rows_n // NB_Q_ROWS
    sig = {}
    cls_of_block = []
    reps = []
    for m in range(n_blocks):
        start_row = int(np.clip(NB_Q_ROWS * m - NA_ROWS // 2, 0, rows_n - NB_K_ROWS))
        key = tuple((start_row - r, int(np.clip(r - NA_ROWS // 2, 0, rows_n - NA_ROWS)) - r)
                    for r in range(NB_Q_ROWS * m, NB_Q_ROWS * (m + 1)))
        if key not in sig:
            sig[key] = len(reps)
            reps.append(m)
        cls_of_block.append(sig[key])
    return np.asarray(cls_of_block, np.int32), reps


def _nb_bias_tables(rpb, rows_n):
    cls_of_block, reps = _nb_classes(rows_n)
    n_heads = rpb.shape[0]
    cq = np.arange(GRID_W)[:, None]
    ck = np.arange(GRID_W)[None, :]
    cs = np.clip(cq - NA_COLS // 2, 0, GRID_W - NA_COLS)
    col_ok = (ck >= cs) & (ck < cs + NA_COLS)
    pick = ((ck - cq + NA_COLS - 1)[None] == np.arange(2 * NA_COLS - 1)[:, None, None]) & col_ok[None]
    tiles = jnp.einsum('hrd,dqk->hrqk', rpb.astype(jnp.float32), jnp.asarray(pick, jnp.float32),
                       precision=lax.Precision.HIGHEST)
    tiles = jnp.where(jnp.asarray(col_ok)[None, None], tiles * LOG2E, NEG_INF).astype(jnp.bfloat16)
    blocked = jnp.full((n_heads, GRID_W, GRID_W), NEG_INF, jnp.bfloat16)
    tabs = []
    for m in reps:
        start_row = int(np.clip(NB_Q_ROWS * m - NA_ROWS // 2, 0, rows_n - NB_K_ROWS))
        q_rows = []
        for r in range(NB_Q_ROWS * m, NB_Q_ROWS * (m + 1)):
            rs = int(np.clip(r - NA_ROWS // 2, 0, rows_n - NA_ROWS))
            q_rows.append(jnp.concatenate(
                [tiles[:, krow - r + NA_ROWS - 1] if rs <= krow < rs + NA_ROWS else blocked
                 for krow in range(start_row, start_row + NB_K_ROWS)], axis=2))
        tabs.append(jnp.concatenate(q_rows, axis=1))
    return jnp.stack(tabs), cls_of_block


def _neighbourhood_attention(qb, kb, vb, kbc, vbc, tabs, cls_of_block, *, batch, seq):
    rows_n = seq // GRID_W
    tq = ATT_BLOCKS * NB_Q_ROWS * GRID_W
    nstep = seq // tq
    n_ctx = kbc.shape[1]

    def tab_spec(sb):
        return pl.BlockSpec((1,) + tabs.shape[1:], lambda b, m, c: (c[m * ATT_BLOCKS + sb], 0, 0, 0))

    grid_spec = pltpu.PrefetchScalarGridSpec(
        num_scalar_prefetch=1,
        grid=(batch, nstep),
        in_specs=[pl.BlockSpec((tq, WIDTH_B), lambda b, m, c: (b * nstep + m, 0)),
                  pl.BlockSpec((1, seq, WIDTH_B), lambda b, m, c: (b, 0, 0)),
                  pl.BlockSpec((1, seq, WIDTH_B), lambda b, m, c: (b, 0, 0)),
                  pl.BlockSpec((1, n_ctx, WIDTH_B), lambda b, m, c: (b, 0, 0)),
                  pl.BlockSpec((1, n_ctx, WIDTH_B), lambda b, m, c: (b, 0, 0))]
                 + [tab_spec(sb) for sb in range(ATT_BLOCKS)],
        out_specs=pl.BlockSpec((tq, WIDTH_B), lambda b, m, c: (b * nstep + m, 0)),
        scratch_shapes=_attention_scratch(NB_Q_ROWS * GRID_W, N_HEADS_B, NB_K_ROWS * GRID_W + n_ctx),
    )

    def body(c_ref, *refs):
        _nb_kernel(*refs, rows_n=rows_n)

    return pl.pallas_call(
        body,
        grid_spec=grid_spec,
        out_shape=jax.ShapeDtypeStruct((batch * seq, WIDTH_B), jnp.bfloat16),
        compiler_params=_params("parallel", "arbitrary"),
        name="neighbourhood_attention",
    )(jnp.asarray(cls_of_block), qb, kb, vb, kbc, vbc, *([tabs] * ATT_BLOCKS))


def _outproj_kernel(oa_ref, ob_ref, x_ref, ga_ref, sh_ref, sc_ref, gga_ref, ggb_ref, gpost_ref, gpre_ref,
                    wo_ref, wr_hi_ref, wr_lo_ref, br_ref,
                    x1_ref, h2_ref, idx_ref, wts_ref, rank_ref, size_ref, base_ref, cnt_ref, run_ref):
    i = pl.program_id(0)

    @pl.when(i == 0)
    def _():
        run_ref[...] = jnp.zeros_like(run_ref)

    n_rows = x_ref.shape[0]
    parts = [slice(p * n_rows // OUT_PARTS, (p + 1) * n_rows // OUT_PARTS) for p in range(OUT_PARTS)]
    na = [_rms(oa_ref[r, :].astype(jnp.float32), gga_ref[...]).astype(jnp.bfloat16) for r in parts]
    nb = [_rms(ob_ref[r, :].astype(jnp.float32), ggb_ref[...]).astype(jnp.bfloat16) for r in parts]
    mix = [_dot(a, wo_ref[:WIDTH_A, :]) + _dot(b, wo_ref[WIDTH_A:, :]) for a, b in zip(na, nb)]
    x1 = [x_ref[r, :] + ga_ref[0] * _rms(m, gpost_ref[...]) for r, m in zip(parts, mix)]
    for r, v in zip(parts, x1):
        x1_ref[r, :] = v
    h2 = [_rms(v, gpre_ref[...]) * (1.0 + sc_ref[0]) + sh_ref[0] for v in x1]
    for r, v in zip(parts, h2):
        h2_ref[r, :] = v.astype(h2_ref.dtype)

    h_hi = [v.astype(jnp.bfloat16) for v in h2]
    h_lo = [(v - hi.astype(jnp.float32)).astype(jnp.bfloat16) for v, hi in zip(h2, h_hi)]
    logits = jnp.concatenate([_dot(hi, wr_hi_ref[...]) + _dot(lo, wr_hi_ref[...]) + _dot(hi, wr_lo_ref[...])
                              for hi, lo in zip(h_hi, h_lo)], axis=0)
    work = logits.T[:N_EXPERTS] + br_ref[...]
    e, t = work.shape
    row = lax.broadcasted_iota(jnp.int32, (e, t), 0)
    chosen = jnp.zeros((e, t), jnp.float32)
    vals, sels, hots = [], [], []
    for _k in range(TOP_K):
        mx = jnp.max(work, axis=0, keepdims=True)
        sel = jnp.min(jnp.where(work == mx, row, e), axis=0, keepdims=True)
        hot = row == sel
        vals.append(mx)
        sels.append(sel)
        hots.append(hot)
        work = jnp.where(hot, -jnp.inf, work)
        chosen = chosen + hot.astype(jnp.float32)
    ex = [jnp.exp(v - vals[0]) for v in vals]
    den = ex[0] + ex[1] + ex[2] + ex[3]

    earlier = (lax.broadcasted_iota(jnp.int32, (t, t), 0) < lax.broadcasted_iota(jnp.int32, (t, t), 1))
    prefix = _dot(chosen.astype(jnp.bfloat16), earlier.astype(jnp.bfloat16))
    ranks = [jnp.sum(jnp.where(h, prefix, 0.0), axis=0, keepdims=True) for h in hots]
    size = jnp.sum(chosen, axis=1, keepdims=True).astype(jnp.int32)
    size_ref[0] = size
    base_ref[0] = run_ref[...]
    run_ref[...] = run_ref[...] + size
    cnt_ref[...] = run_ref[...]

    idx_ref[0] = jnp.concatenate(sels, axis=0)
    wts_ref[0] = jnp.concatenate([v / den for v in ex], axis=0)
    rank_ref[0] = jnp.concatenate(ranks, axis=0).astype(jnp.int32)


def _outproj(oa, ob, x2d, mod3, gga, ggb, gpost, gpre, w_out, w_router, b_router, *, seq):
    t, d = x2d.shape
    tile = OUT_TILE
    per_seq = seq // tile
    row = lambda i: (i, 0)
    const = lambda i: (0, 0)
    modspec = lambda part: pl.BlockSpec((1, 1, d), lambda i: (i // per_seq, 0, part))
    per_tile = lambda shape: pl.BlockSpec((1,) + shape, lambda i: (i, 0, 0))
    n_steps = t // tile
    w_pad = jnp.zeros((d, LANES), jnp.float32).at[:, :N_EXPERTS].set(w_router)
    wr_hi = w_pad.astype(jnp.bfloat16)
    wr_lo = (w_pad - wr_hi.astype(jnp.float32)).astype(jnp.bfloat16)
    outs = pl.pallas_call(
        _outproj_kernel,
        grid=(n_steps,),
        in_specs=[pl.BlockSpec((tile, WIDTH_A), row), pl.BlockSpec((tile, WIDTH_B), row),
                  pl.BlockSpec((tile, d), row),
                  modspec(2), modspec(3), modspec(4),
                  pl.BlockSpec((1, WIDTH_A), const), pl.BlockSpec((1, WIDTH_B), const),
                  pl.BlockSpec((1, d), const), pl.BlockSpec((1, d), const),
                  pl.BlockSpec(w_out.shape, const), pl.BlockSpec((d, LANES), const), pl.BlockSpec((d, LANES), const),
                  pl.BlockSpec((N_EXPERTS, 1), const)],
        out_specs=[pl.BlockSpec((tile, d), row), pl.BlockSpec((tile, d), row),
                   per_tile((TOP_K, tile)), per_tile((TOP_K, tile)), per_tile((TOP_K, tile)),
                   per_tile((N_EXPERTS, 1)), per_tile((N_EXPERTS, 1)),
                   pl.BlockSpec((N_EXPERTS, 1), const)],
        out_shape=[jax.ShapeDtypeStruct((t, d), jnp.float32), jax.ShapeDtypeStruct((t, d), jnp.bfloat16),
                   jax.ShapeDtypeStruct((n_steps, TOP_K, tile), jnp.int32),
                   jax.ShapeDtypeStruct((n_steps, TOP_K, tile), jnp.float32),
                   jax.ShapeDtypeStruct((n_steps, TOP_K, tile), jnp.int32),
                   jax.ShapeDtypeStruct((n_steps, N_EXPERTS, 1), jnp.int32),
                   jax.ShapeDtypeStruct((n_steps, N_EXPERTS, 1), jnp.int32),
                   jax.ShapeDtypeStruct((N_EXPERTS, 1), jnp.int32)],
        scratch_shapes=[pltpu.VMEM((N_EXPERTS, 1), jnp.int32)],
        compiler_params=_params("arbitrary"),
        name="outproj_router",
    )(oa, ob, x2d, mod3, mod3, mod3, gga, ggb, gpost, gpre, w_out, wr_hi, wr_lo, b_router.reshape(N_EXPERTS, 1))
    return outs


def _piece_table(tab_ref, e):
    return (pl.multiple_of(tab_ref[e], ROW_TILE), pl.multiple_of(tab_ref[N_EXPERTS + e], ROW_TILE),
            pl.multiple_of(tab_ref[2 * N_EXPERTS + e], ROW_TILE))


def _to_tiles(x, ref, index=()):
    for c in range(ROW_TILE):
        ref[index + (pl.ds(c, x.shape[0], stride=ROW_TILE), slice(None))] = x[:, c * LANES:(c + 1) * LANES]


def _from_tiles(ref, n_rows, index=()):
    return jnp.concatenate([ref[index + (pl.ds(c, n_rows, stride=ROW_TILE), slice(None))]
                            for c in range(ROW_TILE)], axis=1)


def _dispatch_kernel(fill_from_ref, pad_end_ref, tab_ref, h_ref, lpos_ref, xs_ref,
                     stage_ref, zero_ref, pend_ref, sems, zsem, *, grid_steps):
    i = pl.program_id(0)
    n_steps = pl.num_programs(0)
    n_slots = stage_ref.shape[0]
    slot = i % n_slots
    n_local = stage_ref.shape[1] // ROW_TILE
    fill_rows = zero_ref.shape[0]

    def fill(row):
        return pltpu.make_async_copy(zero_ref.at[pl.ds(0, fill_rows)],
                                     xs_ref.at[pl.ds(pl.multiple_of(row, fill_rows), fill_rows)], zsem)

    def pad_fill(e):
        n = pl.multiple_of(pad_end_ref[e] - fill_from_ref[e], ROW_TILE)
        return n, pltpu.make_async_copy(zero_ref.at[pl.ds(0, n)],
                                        xs_ref.at[pl.ds(pl.multiple_of(fill_from_ref[e], ROW_TILE), n)], zsem)

    def drain(s):
        n = pl.multiple_of(pend_ref[s], ROW_TILE)

        @pl.when(n > 0)
        def _():
            pltpu.make_async_copy(stage_ref.at[s, pl.ds(0, n)], xs_ref.at[pl.ds(0, n)], sems.at[s]).wait()
        pend_ref[s] = 0

    @pl.when(i == 0)
    def _():
        zero_ref[...] = jnp.zeros_like(zero_ref)
        for s in range(n_slots):
            pend_ref[s] = 0

    for j in range(-(-N_EXPERTS // grid_steps)):
        e = i * -(-N_EXPERTS // grid_steps) + j

        @pl.when(e < N_EXPERTS)
        def _(e=e):
            n, copy = pad_fill(e)

            @pl.when(n > 0)
            def _():
                copy.start()

    row = lax.broadcasted_iota(jnp.int32, (n_local, h_ref.shape[0]), 0)
    place = row == lpos_ref[0, 0:1, :]
    for k in range(1, TOP_K):
        place = place | (row == lpos_ref[0, k:k + 1, :])
    rows = _dot(place.astype(jnp.bfloat16), h_ref[...])

    drain(slot)
    _to_tiles(rows, stage_ref, (slot,))

    total = 0
    for e in range(N_EXPERTS):
        loc, glob, n = _piece_table(tab_ref, e)

        @pl.when(n > 0)
        def _(loc=loc, glob=glob, n=n, e=e):
            pltpu.make_async_copy(stage_ref.at[slot, pl.ds(loc, n)], xs_ref.at[pl.ds(glob, n)],
                                  sems.at[slot]).start(priority=e % 2)
        total = total + n
    pend_ref[slot] = total

    tail0 = pad_end_ref[N_EXPERTS - 1]
    n_tail = (xs_ref.shape[0] - tail0) // fill_rows
    per_step = (n_tail + n_steps - 1) // n_steps

    def start_tail(j, c):
        t = i * per_step + j

        @pl.when(t < n_tail)
        def _():
            fill(tail0 + t * fill_rows).start()
        return c
    lax.fori_loop(0, per_step, start_tail, 0)

    @pl.when(i == n_steps - 1)
    def _():
        for s in range(n_slots):
            drain(s)

        def wait_tail(j, c):
            fill(0).wait()
            return c
        lax.fori_loop(0, n_tail, wait_tail, 0)

        def wait_pad(e, c):
            n, copy = pad_fill(e)

            @pl.when(n > 0)
            def _():
                copy.wait()
            return c
        lax.fori_loop(0, N_EXPERTS, wait_pad, 0)


def _dispatch(h2, tab_flat, lpos_t, fill_from, pad_end, *, n_rows):
    t, d = h2.shape
    tile = DISPATCH_TILE
    assert d == ROW_TILE * LANES
    n_local = tile * TOP_K
    grid_spec = pltpu.PrefetchScalarGridSpec(
        num_scalar_prefetch=2,
        grid=(t // tile,),
        in_specs=[pl.BlockSpec((TAB_WIDTH,), lambda i, *_: (i,), memory_space=pltpu.SMEM),
                  pl.BlockSpec((tile, d), lambda i, *_: (i, 0)),
                  pl.BlockSpec((1, TOP_K, tile), lambda i, *_: (i, 0, 0))],
        out_specs=pl.BlockSpec(memory_space=pl.ANY),
        scratch_shapes=[pltpu.VMEM((STAGE_SLOTS, n_local * ROW_TILE, LANES), jnp.float32),
                        pltpu.VMEM((FFN_STEP * ROW_TILE, LANES), jnp.float32),
                        pltpu.SMEM((STAGE_SLOTS,), jnp.int32),
                        pltpu.SemaphoreType.DMA((STAGE_SLOTS,)), pltpu.SemaphoreType.DMA(())],
    )
    return pl.pallas_call(
        functools.partial(_dispatch_kernel, grid_steps=t // tile),
        grid_spec=grid_spec,
        out_shape=jax.ShapeDtypeStruct((n_rows * ROW_TILE, LANES), jnp.float32),
        compiler_params=_params("arbitrary"),
        name="dispatch",
    )(fill_from, pad_end, tab_flat, h2, lpos_t)


def _ffn_kernel(te_ref, rows_ref, n_used_ref, x_ref, w1_ref, b1_ref, w2_ref, b2_ref, y_ref,
                w1p_ref, w2b_ref, act_ref):
    i = pl.program_id(0)
    live = i < n_used_ref[0]
    n_pair = w1_ref.shape[2] // FFN_CHUNK
    half = FFN_CHUNK // 2
    n_sub = jnp.where(live, (rows_ref[i] + FFN_TILE - 1) // FFN_TILE, 0)

    @pl.when(live & ((i == 0) | (te_ref[i] != te_ref[jnp.maximum(i - 1, 0)])))
    def _():
        src = lax.broadcasted_iota(jnp.int32, (FFN_CHUNK, FFN_CHUNK), 0)
        dst = lax.broadcasted_iota(jnp.int32, (FFN_CHUNK, FFN_CHUNK), 1)
        unzip = (src == jnp.where(dst < half, 2 * dst, 2 * (dst - half) + 1)).astype(jnp.bfloat16)
        for c in range(n_pair):
            cs = slice(c * FFN_CHUNK, (c + 1) * FFN_CHUNK)
            w1p_ref[:, cs] = _dot(w1_ref[0, :, cs].astype(jnp.bfloat16), unzip).astype(jnp.bfloat16)
        w2b_ref[...] = w2_ref[0].astype(jnp.bfloat16)

    def compute(n_rows):
        x = _from_tiles(x_ref, n_rows).astype(jnp.bfloat16)
        for c in range(0, n_pair, FFN_GROUP):
            cs = slice(c * FFN_CHUNK, (c + FFN_GROUP) * FFN_CHUNK)
            h = _dot(x, w1p_ref[:, cs]) + b1_ref[0, :, cs]
            g = jnp.concatenate([h[:, k * FFN_CHUNK:k * FFN_CHUNK + half] for k in range(FFN_GROUP)], axis=1)
            u = jnp.concatenate([h[:, k * FFN_CHUNK + half:(k + 1) * FFN_CHUNK] for k in range(FFN_GROUP)], axis=1)
            g = jnp.minimum(g, SWIGLU_LIMIT)
            u = jnp.clip(u, -SWIGLU_LIMIT, SWIGLU_LIMIT)
            act = g * jax.nn.sigmoid(SWIGLU_ALPHA * g) * (u + 1.0)
            act_ref[:n_rows, c * half:(c + FFN_GROUP) * half] = act.astype(act_ref.dtype)
        _to_tiles(_dot(act_ref[:n_rows, :], w2b_ref[...]) + b2_ref[0], y_ref)
        if n_rows * ROW_TILE < y_ref.shape[0]:
            y_ref[n_rows * ROW_TILE:, :] = jnp.zeros((y_ref.shape[0] - n_rows * ROW_TILE, LANES), y_ref.dtype)

    for tiles in range(1, FFN_STEP // FFN_TILE + 1):
        pl.when(n_sub == tiles)(functools.partial(compute, tiles * FFN_TILE))

    @pl.when(n_sub == 0)
    def _():
        y_ref[...] = jnp.zeros_like(y_ref)


def _unzip_bias(b1):
    e, f2 = b1.shape
    half = FFN_CHUNK // 2
    return b1.reshape(e, f2 // FFN_CHUNK, half, 2).transpose(0, 1, 3, 2).reshape(e, 1, f2)


def _ffn(xs, step_expert, step_rows, n_used, w1, b1, w2, b2):
    d, f2 = w1.shape[1:]
    n_steps = xs.shape[0] // (FFN_STEP * ROW_TILE)
    live = lambda i, nu: jnp.minimum(i, nu[0] - 1)
    wspec = lambda shape: pl.BlockSpec((1,) + shape, lambda i, te, nr, nu: (te[live(i, nu)], 0, 0))
    grid_spec = pltpu.PrefetchScalarGridSpec(
        num_scalar_prefetch=3,
        grid=(n_steps,),
        in_specs=[pl.BlockSpec((FFN_STEP * ROW_TILE, LANES), lambda i, te, nr, nu: (live(i, nu), 0)),
                  wspec((d, f2)), wspec((1, f2)), wspec((f2 // 2, d)), wspec((1, d))],
        out_specs=pl.BlockSpec((FFN_STEP * ROW_TILE, LANES), lambda i, te, nr, nu: (i, 0)),
        scratch_shapes=[pltpu.VMEM((d, f2), jnp.bfloat16), pltpu.VMEM((f2 // 2, d), jnp.bfloat16),
                        pltpu.VMEM((FFN_STEP, f2 // 2), jnp.bfloat16)],
    )
    return pl.pallas_call(
        _ffn_kernel,
        grid_spec=grid_spec,
        out_shape=jax.ShapeDtypeStruct(xs.shape, jnp.float32),
        compiler_params=_params("arbitrary"),
        name="expert_ffn",
    )(step_expert, step_rows, n_used, xs, w1, _unzip_bias(b1), w2, b2[:, None, :])


def _combine_kernel(tab_ref, next_tab_ref, ahead_tab_ref, lpos_ref, wts_ref, x1_ref, ga_ref, g_ref, y_ref, o_ref,
                    buf_ref, sems):
    i = pl.program_id(0)
    n_steps = pl.num_programs(0)
    n_slots = buf_ref.shape[0]
    slot = i % n_slots
    tile = x1_ref.shape[0]
    n_local = buf_ref.shape[1] // ROW_TILE

    def fetch(tab, s):
        for e in range(N_EXPERTS):
            loc, glob, n = _piece_table(tab, e)

            @pl.when(n > 0)
            def _(loc=loc, glob=glob, n=n, e=e):
                pltpu.make_async_copy(y_ref.at[pl.ds(glob, n)], buf_ref.at[s, pl.ds(loc, n)],
                                      sems.at[s]).start(priority=e % 2)

    @pl.when(i == 0)
    def _():
        fetch(tab_ref, 0)
        if n_slots > 2:
            pl.when(n_steps > 1)(lambda: fetch(next_tab_ref, 1))

    @pl.when(i + n_slots - 1 < n_steps)
    def _():
        fetch(ahead_tab_ref, (i + n_slots - 1) % n_slots)

    n = pl.multiple_of(tab_ref[3 * N_EXPERTS], ROW_TILE)

    @pl.when(n > 0)
    def _():
        pltpu.make_async_copy(y_ref.at[pl.ds(0, n)], buf_ref.at[slot, pl.ds(0, n)], sems.at[slot]).wait()

    rows = _from_tiles(buf_ref, n_local, (slot,)).astype(jnp.bfloat16)
    part = tile // OUT_PARTS
    col = lax.broadcasted_iota(jnp.int32, (part, n_local), 1)
    for p in range(OUT_PARTS):
        r = slice(p * part, (p + 1) * part)
        pos = lpos_ref[r, :]
        w = wts_ref[r, :]
        pick = jnp.where(col == pos[:, 0:1], w[:, 0:1], 0.0)
        for k in range(1, TOP_K):
            pick = pick + jnp.where(col == pos[:, k:k + 1], w[:, k:k + 1], 0.0)
        hi = pick.astype(jnp.bfloat16)
        lo = (pick - hi.astype(jnp.float32)).astype(jnp.bfloat16)
        y = _dot(hi, rows) + _dot(lo, rows)
        o_ref[r, :] = x1_ref[r, :] + ga_ref[0] * _rms(y, g_ref[...])


def _combine(ys, tab_flat, lpos, wts, x1, mod3, g_post, *, seq):
    t, d = x1.shape
    tile = DISPATCH_TILE
    n_steps = t // tile
    per_seq = seq // tile
    n_local = tile * TOP_K
    return pl.pallas_call(
        _combine_kernel,
        grid=(n_steps,),
        in_specs=[pl.BlockSpec((TAB_WIDTH,), lambda i: (i,), memory_space=pltpu.SMEM),
                  pl.BlockSpec((TAB_WIDTH,), lambda i: (jnp.minimum(i + 1, n_steps - 1),), memory_space=pltpu.SMEM),
                  pl.BlockSpec((TAB_WIDTH,), lambda i: (jnp.minimum(i + STAGE_SLOTS - 1, n_steps - 1),),
                               memory_space=pltpu.SMEM),
                  pl.BlockSpec((tile, TOP_K), lambda i: (i, 0)),
                  pl.BlockSpec((tile, TOP_K), lambda i: (i, 0)),
                  pl.BlockSpec((tile, d), lambda i: (i, 0)),
                  pl.BlockSpec((1, 1, d), lambda i: (i // per_seq, 0, 5)),
                  pl.BlockSpec((1, d), lambda i: (0, 0)),
                  pl.BlockSpec(memory_space=pl.ANY)],
        out_specs=pl.BlockSpec((tile, d), lambda i: (i, 0)),
        out_shape=jax.ShapeDtypeStruct((t, d), jnp.float32),
        scratch_shapes=[pltpu.VMEM((STAGE_SLOTS, n_local * ROW_TILE, LANES), ys.dtype),
                        pltpu.SemaphoreType.DMA((STAGE_SLOTS,))],
        compiler_params=_params("arbitrary"),
        name="combine",
    )(tab_flat, tab_flat, tab_flat, lpos, wts, x1, mod3, g_post, ys)


def _rope_tables(seq):
    pos = np.arange(seq)
    n_freq = HEAD_DIM // 4
    freqs = ROPE_BASE ** (-jnp.arange(n_freq, dtype=jnp.float32) / n_freq)
    rows = jnp.asarray(pos // GRID_W, jnp.float32)[:, None] * freqs[None, :]
    cols = jnp.asarray(pos % GRID_W, jnp.float32)[:, None] * freqs[None, :]
    ang = jnp.concatenate([rows, rows, cols, cols], axis=1)
    sign = np.tile(np.repeat([-1.0, 1.0], n_freq), 2).astype(np.float32)
    cos = jnp.cos(ang)
    sin = jnp.sin(ang) * sign[None, :]
    reps = LANES // HEAD_DIM
    return jnp.tile(cos, (1, reps)), jnp.tile(sin, (1, reps))


def kernel(x, c, ctx, c_ctx, w_ada, b_ada, g_pre_mix, g_post_mix, g_pre_ffn, g_post_ffn, w_in, g_grp_a, g_grp_b,
           sink_a, rpb_b, w_out, w_router, b_router, w_mlp1, b_mlp1, w_mlp2, b_mlp2):
    batch, seq, d = x.shape
    n_ctx = ctx.shape[1]
    assert w_ada.shape[0] == 1, "single layer"
    assert seq % (ATT_BLOCKS * GRID_W * NB_Q_ROWS) == 0 and seq // GRID_W >= NB_K_ROWS
    assert seq % (ATT_BLOCKS * BLOCK_A) == 0 and seq >= BLOCK_A + 2 * WINDOW
    n_tok = batch * seq
    bf16 = jnp.bfloat16

    def pair_heads(a, axis):
        shape = a.shape
        a = a.reshape(shape[:axis] + (N_KV_A, GQA_GROUP, HEAD_DIM) + shape[axis + 1:])
        return jnp.swapaxes(a, axis, axis + 1).reshape(shape)

    mod_rows = -(-(batch + 1) // 8) * 8
    cc = jnp.zeros((mod_rows, d), jnp.float32).at[:batch].set(c).at[batch].set(c_ctx)
    mod3 = _ada(cc, w_ada[0], b_ada[0]).reshape(mod_rows, 1, 6 * d)

    w_in0 = w_in[0]
    w_lat = jnp.concatenate([pair_heads(w_in0[:, :WIDTH_A], 1), w_in0[:, WIDTH_A:]], axis=1).astype(bf16)
    kv_a_end = WIDTH_A + 2 * WIDTH_KV_A
    w_ctx = jnp.concatenate([w_in0[:, WIDTH_A:kv_a_end], w_in0[:, kv_a_end + WIDTH_B:]], axis=1).astype(bf16)
    g_pre = g_pre_mix[0].reshape(1, d)
    qa, ka, va, qb, kb, vb = _inproj(x.reshape(n_tok, d), mod3, g_pre, w_lat, _rope_tables(seq),
                                     seq=seq, mod_row0=0, latent=True)
    kac, vac, kbc, vbc = _inproj(ctx.reshape(batch * n_ctx, d), mod3, g_pre, w_ctx, None,
                                 seq=n_ctx, mod_row0=batch, latent=False)

    oa = _window_attention(qa, ka.reshape(batch, seq, -1), va.reshape(batch, seq, -1),
                           kac.reshape(batch, n_ctx, -1), vac.reshape(batch, n_ctx, -1),
                           sink_a[0].astype(jnp.float32) * LOG2E, batch=batch, seq=seq)
    tabs, cls_of_block = _nb_bias_tables(rpb_b[0], seq // GRID_W)
    ob = _neighbourhood_attention(qb, kb.reshape(batch, seq, -1), vb.reshape(batch, seq, -1),
                                  kbc.reshape(batch, n_ctx, -1), vbc.reshape(batch, n_ctx, -1),
                                  tabs, cls_of_block, batch=batch, seq=seq)

    w_out0 = w_out[0]
    w_o = jnp.concatenate([pair_heads(w_out0[:WIDTH_A], 0), w_out0[WIDTH_A:]], axis=0).astype(bf16)
    x1, h2, idx, wts, rank, size, base, counts = _outproj(
        oa, ob, x.reshape(n_tok, d), mod3,
        pair_heads(g_grp_a[0], 0).reshape(1, -1), g_grp_b[0].reshape(1, -1),
        g_post_mix[0].reshape(1, d), g_pre_ffn[0].reshape(1, d),
        w_o, w_router[0], b_router[0].reshape(1, -1), seq=seq)

    n_tok_tiles = n_tok // DISPATCH_TILE
    size = size.reshape(n_tok_tiles, N_EXPERTS)
    counts = counts.reshape(-1)
    padded = (counts + FFN_STEP - 1) // FFN_STEP * FFN_STEP
    pad_end = jnp.cumsum(padded).astype(jnp.int32)
    pad_start = pad_end - padded
    n_steps = n_tok * TOP_K // FFN_STEP + N_EXPERTS
    n_used = (pad_end[-1:] // FFN_STEP).astype(jnp.int32)
    step_row0 = jnp.arange(n_steps, dtype=jnp.int32) * FFN_STEP
    step_expert = jnp.minimum(jnp.sum(step_row0[:, None] >= pad_end[None, :], axis=1),
                              N_EXPERTS - 1).astype(jnp.int32)
    own = step_expert[:, None] == jnp.arange(N_EXPERTS, dtype=jnp.int32)
    step_rows = jnp.clip(jnp.sum(jnp.where(own, (pad_start + counts)[None, :], 0), axis=1) - step_row0,
                         0, FFN_STEP).astype(jnp.int32)
    fill_from = (pad_start + counts).astype(jnp.int32)
    local0 = jnp.cumsum(size, axis=1) - size
    global0 = pad_start[None, :] + base.reshape(n_tok_tiles, N_EXPERTS)
    tab = jnp.concatenate([local0, global0, size, jnp.sum(size, axis=1, keepdims=True),
                           jnp.zeros((n_tok_tiles, TAB_WIDTH - 3 * N_EXPERTS - 1), jnp.int32)], axis=1)
    tab = (tab * ROW_TILE).reshape(-1).astype(jnp.int32)
    hot = idx[:, :, None, :] == jnp.arange(N_EXPERTS, dtype=jnp.int32)[None, None, :, None]
    lpos_t = rank + jnp.sum(jnp.where(hot, local0[:, None, :, None], 0), axis=2)
    by_token = lambda a: jnp.swapaxes(a, 1, 2).reshape(n_tok, TOP_K)

    xs = _dispatch(h2, tab, lpos_t, fill_from * ROW_TILE, pad_end * ROW_TILE, n_rows=n_steps * FFN_STEP)
    ys = _ffn(xs, step_expert, step_rows, n_used, w_mlp1[0], b_mlp1[0], w_mlp2[0], b_mlp2[0])
    out = _combine(ys, tab, by_token(lpos_t), by_token(wts), x1, mod3, g_post_ffn[0].reshape(1, d), seq=seq)
    return out.reshape(batch, seq, d)
```

```python
import functools

import numpy as np
import jax
import jax.numpy as jnp
from jax import lax
from jax.experimental import pallas as pl
from jax.experimental.pallas import tpu as pltpu

GRID_W = 64
HEAD_DIM = 64
N_HEADS_A = 8
N_KV_A = 2
GQA_GROUP = N_HEADS_A // N_KV_A
N_HEADS_B = 8
WIDTH_A = N_HEADS_A * HEAD_DIM
WIDTH_KV_A = N_KV_A * HEAD_DIM
WIDTH_B = N_HEADS_B * HEAD_DIM
WINDOW = 128
BLOCK_A = 128
NA_ROWS = 8
NA_COLS = 16
N_EXPERTS = 32
TOP_K = 4
SWIGLU_LIMIT = 7.0
SWIGLU_ALPHA = 1.702
ROPE_BASE = 10000.0
EPS = 1e-6
NEG_INF = -1e30

LANES = 128
VMEM_LIMIT = 56 * 1024 * 1024

LOG2E = 1.4426950408889634
ATT_BLOCKS = 8
NB_Q_ROWS = 2
NB_K_ROWS = NB_Q_ROWS + NA_ROWS
PROJ_TILE = 1024
OUT_TILE = 256
OUT_PARTS = 2
STAGE_SLOTS = 4
LAND_SLOTS = 4
FFN_TILE = 256
FFN_STEP = 4 * FFN_TILE
FFN_CHUNK = 256
FFN_GROUP = 4
DISPATCH_TILE = OUT_TILE
ROW_TILE = 8
TAB_WIDTH = 128


def _params(*sem):
    return pltpu.CompilerParams(dimension_semantics=sem, vmem_limit_bytes=VMEM_LIMIT)


def _rms(x, g):
    return x * lax.rsqrt(jnp.mean(x * x, axis=-1, keepdims=True) + EPS) * g


def _dot(a, b):
    return jnp.dot(a, b, preferred_element_type=jnp.float32)


def _dot_nt(a, b):
    return lax.dot_general(a, b, (((1,), (1,)), ((), ())), preferred_element_type=jnp.float32)


def _ada_kernel(c_ref, w_ref, b_ref, o_ref):
    c = c_ref[...]
    s = (c * jax.nn.sigmoid(c)).astype(jnp.bfloat16)
    o_ref[...] = _dot(s, w_ref[...].astype(jnp.bfloat16)) + b_ref[...]


def _ada(cc, w_ada, b_ada):
    rows, d = cc.shape
    n_out = w_ada.shape[1]
    return pl.pallas_call(
        _ada_kernel,
        grid=(n_out // d,),
        in_specs=[pl.BlockSpec((rows, d), lambda j: (0, 0)),
                  pl.BlockSpec((d, d), lambda j: (0, j)),
                  pl.BlockSpec((1, d), lambda j: (0, j))],
        out_specs=pl.BlockSpec((rows, d), lambda j: (0, j)),
        out_shape=jax.ShapeDtypeStruct((rows, n_out), jnp.float32),
        compiler_params=_params("arbitrary"),
        name="ada",
    )(cc, w_ada, b_ada.reshape(1, n_out))


def _rope(x, cos, sin):
    w = x.shape[1]
    reps = w // LANES
    if reps > 1:
        cos = jnp.concatenate([cos] * reps, axis=1)
        sin = jnp.concatenate([sin] * reps, axis=1)
    lane = lax.broadcasted_iota(jnp.int32, x.shape, 1)
    quarter = HEAD_DIM // 4
    partner = jnp.where(lane % (2 * quarter) < quarter,
                        pltpu.roll(x, w - quarter, 1), pltpu.roll(x, quarter, 1))
    return x * cos + partner * sin


def _inproj_kernel(x_ref, sh_ref, sc_ref, g_ref, w_ref, *rest, latent):
    x = x_ref[...]
    h = _rms(x, g_ref[...]) * (1.0 + sc_ref[0]) + sh_ref[0]
    p = _dot(h.astype(jnp.bfloat16), w_ref[...])
    if latent:
        cos_ref, sin_ref, qa_ref, ka_ref, va_ref, qb_ref, kb_ref, vb_ref = rest
        cos, sin = cos_ref[...], sin_ref[...]
        scale = HEAD_DIM ** -0.5 * LOG2E
        o = 0
        qa_ref[...] = (_rope(p[:, o:o + WIDTH_A], cos, sin) * scale).astype(qa_ref.dtype)
        o += WIDTH_A
        ka_ref[...] = _rope(p[:, o:o + WIDTH_KV_A], cos, sin).astype(ka_ref.dtype)
        o += WIDTH_KV_A
        va_ref[...] = p[:, o:o + WIDTH_KV_A].astype(va_ref.dtype)
        o += WIDTH_KV_A
        qb_ref[...] = (p[:, o:o + WIDTH_B] * scale).astype(qb_ref.dtype)
        o += WIDTH_B
    else:
        ka_ref, va_ref, kb_ref, vb_ref = rest
        o = 0
        ka_ref[...] = p[:, o:o + WIDTH_KV_A].astype(ka_ref.dtype)
        o += WIDTH_KV_A
        va_ref[...] = p[:, o:o + WIDTH_KV_A].astype(va_ref.dtype)
        o += WIDTH_KV_A
    kb_ref[...] = p[:, o:o + WIDTH_B].astype(kb_ref.dtype)
    o += WIDTH_B
    vb_ref[...] = p[:, o:o + WIDTH_B].astype(vb_ref.dtype)


def _inproj(x2d, mod3, g_pre, w, rope_tabs, *, seq, mod_row0, latent):
    t, d = x2d.shape
    tile = min(PROJ_TILE, seq)
    per_seq = seq // tile
    if latent:
        mod_row = lambda i: i // per_seq
    else:
        mod_row = lambda i: mod_row0
    in_specs = [pl.BlockSpec((tile, d), lambda i: (i, 0)),
                pl.BlockSpec((1, 1, d), lambda i: (mod_row(i), 0, 0)),
                pl.BlockSpec((1, 1, d), lambda i: (mod_row(i), 0, 1)),
                pl.BlockSpec((1, d), lambda i: (0, 0)),
                pl.BlockSpec(w.shape, lambda i: (0, 0))]
    args = [x2d, mod3, mod3, g_pre, w]
    widths = [WIDTH_KV_A, WIDTH_KV_A, WIDTH_B, WIDTH_B]
    if latent:
        in_specs += [pl.BlockSpec((tile, LANES), lambda i: (i % per_seq, 0))] * 2
        args += list(rope_tabs)
        widths = [WIDTH_A, WIDTH_KV_A, WIDTH_KV_A, WIDTH_B, WIDTH_B, WIDTH_B]
    return pl.pallas_call(
        functools.partial(_inproj_kernel, latent=latent),
        grid=(t // tile,),
        in_specs=in_specs,
        out_specs=[pl.BlockSpec((tile, wd), lambda i: (i, 0)) for wd in widths],
        out_shape=[jax.ShapeDtypeStruct((t, wd), jnp.bfloat16) for wd in widths],
        compiler_params=_params("parallel"),
        name="inproj_latent" if latent else "inproj_ctx",
    )(*args)


def _ones_beside(v):
    return jnp.concatenate([v, jnp.ones_like(v)], axis=1)


def _group_attention(q_ref, o_ref, rows, groups, s_ref, p_ref, m_ref):
    tq = rows.stop - rows.start
    low = lax.broadcasted_iota(jnp.int32, (tq, LANES), 1) < HEAD_DIM
    layout = []
    base = 0
    for pairs, k_loc, k_ctx, _, _, _, _ in groups:
        heads = [(j, 0) for j in pairs] + [(j, 1) for j in pairs]
        n_loc = k_loc.shape[0]
        stacked = []
        for j, half in heads:
            q = q_ref[rows, j * LANES:(j + 1) * LANES]
            stacked.append(jnp.where(low if half == 0 else ~low, q, jnp.zeros_like(q)))
        qs = jnp.concatenate(stacked, axis=0)
        n = len(heads) * tq
        s_ref[base:base + n, :n_loc] = _dot_nt(qs, k_loc)
        s_ref[base:base + n, n_loc:] = _dot_nt(qs, k_ctx)
        layout.append((base, heads, n_loc))
        base += n
    for (base, heads, n_loc), group in zip(layout, groups):
        bias_of, sink_of = group[5], group[6]
        for g, (j, half) in enumerate(heads):
            r = slice(base + g * tq, base + (g + 1) * tq)
            s_loc = s_ref[r, :n_loc] + bias_of(j, half)
            s_ctx = s_ref[r, n_loc:]
            m = jnp.maximum(jnp.max(s_loc, axis=1, keepdims=True), jnp.max(s_ctx, axis=1, keepdims=True))
            sink = sink_of(j, half)
            if sink is not None:
                m = jnp.maximum(m, sink)
            p_ref[r, :n_loc] = jnp.exp2((s_loc - m).astype(jnp.bfloat16))
            p_ref[r, n_loc:] = jnp.exp2((s_ctx - m).astype(jnp.bfloat16))
            m_ref[r, :] = m
    for (base, heads, n_loc), group in zip(layout, groups):
        pairs, _, _, v_loc, v_ctx, _, sink_of = group
        n = len(heads) * tq
        both = (_dot(p_ref[base:base + n, :n_loc], _ones_beside(v_loc))
                + _dot(p_ref[base:base + n, n_loc:], _ones_beside(v_ctx)))
        for a, j in enumerate(pairs):
            outs = []
            for half in range(2):
                g = half * len(pairs) + a
                r = slice(g * tq, (g + 1) * tq)
                den = both[r, LANES:]
                sink = sink_of(j, half)
                if sink is not None:
                    den = den + jnp.exp2(sink - m_ref[base + g * tq:base + (g + 1) * tq, :])
                outs.append(both[r, :LANES] / den)
            o_ref[rows, j * LANES:(j + 1) * LANES] = jnp.where(low, outs[0], outs[1]).astype(o_ref.dtype)


def _attention_scratch(tq, n_heads, n_keys):
    return [pltpu.VMEM((n_heads * tq, n_keys), jnp.float32), pltpu.VMEM((n_heads * tq, n_keys), jnp.bfloat16),
            pltpu.VMEM((n_heads * tq, 1), jnp.float32)]


def _win_kernel(sink_ref, q_ref, k_ref, v_ref, kc_ref, vc_ref, o_ref, *scratch, seq):
    span = BLOCK_A + 2 * WINDOW
    pairs = list(range(GQA_GROUP))
    for sb in range(ATT_BLOCKS):
        i = pl.program_id(1) * ATT_BLOCKS + sb
        start = pl.multiple_of(jnp.clip(i * BLOCK_A - WINDOW, 0, seq - span), BLOCK_A)
        qpos = i * BLOCK_A + lax.broadcasted_iota(jnp.int32, (BLOCK_A, span), 0)
        kpos = start + lax.broadcasted_iota(jnp.int32, (BLOCK_A, span), 1)
        bias = jnp.where(jnp.abs(kpos - qpos) <= WINDOW, 0.0, NEG_INF).astype(jnp.float32)
        groups = [([j], k_ref[0, pl.ds(start, span), :], kc_ref[0], v_ref[0, pl.ds(start, span), :], vc_ref[0],
                   lambda j, half: bias, lambda j, half: sink_ref[half * GQA_GROUP + j]) for j in pairs]
        _group_attention(q_ref, o_ref, slice(sb * BLOCK_A, (sb + 1) * BLOCK_A), groups, *scratch)


def _window_attention(qa, ka, va, kac, vac, sink, *, batch, seq):
    tq = ATT_BLOCKS * BLOCK_A
    nb = seq // tq
    n_ctx = kac.shape[1]
    return pl.pallas_call(
        functools.partial(_win_kernel, seq=seq),
        grid=(batch, nb),
        in_specs=[pl.BlockSpec(memory_space=pltpu.SMEM),
                  pl.BlockSpec((tq, WIDTH_A), lambda b, i: (b * nb + i, 0)),
                  pl.BlockSpec((1, seq, WIDTH_KV_A), lambda b, i: (b, 0, 0)),
                  pl.BlockSpec((1, seq, WIDTH_KV_A), lambda b, i: (b, 0, 0)),
                  pl.BlockSpec((1, n_ctx, WIDTH_KV_A), lambda b, i: (b, 0, 0)),
                  pl.BlockSpec((1, n_ctx, WIDTH_KV_A), lambda b, i: (b, 0, 0))],
        out_specs=pl.BlockSpec((tq, WIDTH_A), lambda b, i: (b * nb + i, 0)),
        out_shape=jax.ShapeDtypeStruct((batch * seq, WIDTH_A), jnp.bfloat16),
        scratch_shapes=_attention_scratch(BLOCK_A, N_HEADS_A, BLOCK_A + 2 * WINDOW + n_ctx),
        compiler_params=_params("parallel", "arbitrary"),
        name="window_attention",
    )(sink, qa, ka, va, kac, vac)


def _nb_kernel(q_ref, k_ref, v_ref, kc_ref, vc_ref, *rest, rows_n):
    tab_refs, o_ref, scratch = rest[:ATT_BLOCKS], rest[ATT_BLOCKS], rest[ATT_BLOCKS + 1:]
    n_keys = NB_K_ROWS * GRID_W
    tq = NB_Q_ROWS * GRID_W
    for sb in range(ATT_BLOCKS):
        m = pl.program_id(1) * ATT_BLOCKS + sb
        start_row = jnp.clip(NB_Q_ROWS * m - NA_ROWS // 2, 0, rows_n - NB_K_ROWS)
        start = pl.multiple_of(start_row * GRID_W, LANES)
        tab_ref = tab_refs[sb]
        groups = []
        for j in range(q_ref.shape[1] // LANES):
            cols = slice(j * LANES, (j + 1) * LANES)
            groups.append(([j], k_ref[0, pl.ds(start, n_keys), cols], kc_ref[0, :, cols],
                           v_ref[0, pl.ds(start, n_keys), cols], vc_ref[0, :, cols],
                           lambda j, half: tab_ref[0, 2 * j + half].astype(jnp.float32), lambda j, half: None))
        _group_attention(q_ref, o_ref, slice(sb * tq, (sb + 1) * tq), groups, *scratch)


def _nb_classes(rows_n):
    n_blocks = rows_n // NB_Q_ROWS
    sig = {}
    cls_of_block = []
    reps = []
    for m in range(n_blocks):
        start_row = int(np.clip(NB_Q_ROWS * m - NA_ROWS // 2, 0, rows_n - NB_K_ROWS))
        key = tuple((start_row - r, int(np.clip(r - NA_ROWS // 2, 0, rows_n - NA_ROWS)) - r)
                    for r in range(NB_Q_ROWS * m, NB_Q_ROWS * (m + 1)))
        if key not in sig:
            sig[key] = len(reps)
            reps.append(m)
        cls_of_block.append(sig[key])
    return np.asarray(cls_of_block, np.int32), reps


def _nb_bias_tables(rpb, rows_n):
    cls_of_block, reps = _nb_classes(rows_n)
    n_heads = rpb.shape[0]
    cq = np.arange(GRID_W)[:, None]
    ck = np.arange(GRID_W)[None, :]
    cs = np.clip(cq - NA_COLS // 2, 0, GRID_W - NA_COLS)
    col_ok = (ck >= cs) & (ck < cs + NA_COLS)
    pick = ((ck - cq + NA_COLS - 1)[None] == np.arange(2 * NA_COLS - 1)[:, None, None]) & col_ok[None]
    tiles = jnp.einsum('hrd,dqk->hrqk', rpb.astype(jnp.float32), jnp.asarray(pick, jnp.float32),
                       precision=lax.Precision.HIGHEST)
    tiles = jnp.where(jnp.asarray(col_ok)[None, None], tiles * LOG2E, NEG_INF).astype(jnp.bfloat16)
    blocked = jnp.full((n_heads, GRID_W, GRID_W), NEG_INF, jnp.bfloat16)
    tabs = []
    for m in reps:
        start_row = int(np.clip(NB_Q_ROWS * m - NA_ROWS // 2, 0, rows_n - NB_K_ROWS))
        q_rows = []
        for r in range(NB_Q_ROWS * m, NB_Q_ROWS * (m + 1)):
            rs = int(np.clip(r - NA_ROWS // 2, 0, rows_n - NA_ROWS))
            q_rows.append(jnp.concatenate(
                [tiles[:, krow - r + NA_ROWS - 1] if rs <= krow < rs + NA_ROWS else blocked
                 for krow in range(start_row, start_row + NB_K_ROWS)], axis=2))
        tabs.append(jnp.concatenate(q_rows, axis=1))
    return jnp.stack(tabs), cls_of_block


def _neighbourhood_attention(qb, kb, vb, kbc, vbc, tabs, cls_of_block, *, batch, seq):
    rows_n = seq // GRID_W
    tq = ATT_BLOCKS * NB_Q_ROWS * GRID_W
    nstep = seq // tq
    n_ctx = kbc.shape[1]

    def tab_spec(sb):
        return pl.BlockSpec((1,) + tabs.shape[1:], lambda b, m, c: (c[m * ATT_BLOCKS + sb], 0, 0, 0))

    grid_spec = pltpu.PrefetchScalarGridSpec(
        num_scalar_prefetch=1,
        grid=(batch, nstep),
        in_specs=[pl.BlockSpec((tq, WIDTH_B), lambda b, m, c: (b * nstep + m, 0)),
                  pl.BlockSpec((1, seq, WIDTH_B), lambda b, m, c: (b, 0, 0)),
                  pl.BlockSpec((1, seq, WIDTH_B), lambda b, m, c: (b, 0, 0)),
                  pl.BlockSpec((1, n_ctx, WIDTH_B), lambda b, m, c: (b, 0, 0)),
                  pl.BlockSpec((1, n_ctx, WIDTH_B), lambda b, m, c: (b, 0, 0))]
                 + [tab_spec(sb) for sb in range(ATT_BLOCKS)],
        out_specs=pl.BlockSpec((tq, WIDTH_B), lambda b, m, c: (b * nstep + m, 0)),
        scratch_shapes=_attention_scratch(NB_Q_ROWS * GRID_W, N_HEADS_B, NB_K_ROWS * GRID_W + n_ctx),
    )

    def body(c_ref, *refs):
        _nb_kernel(*refs, rows_n=rows_n)

    return pl.pallas_call(
        body,
        grid_spec=grid_spec,
        out_shape=jax.ShapeDtypeStruct((batch * seq, WIDTH_B), jnp.bfloat16),
        compiler_params=_params("parallel", "arbitrary"),
        name="neighbourhood_attention",
    )(jnp.asarray(cls_of_block), qb, kb, vb, kbc, vbc, *([tabs] * ATT_BLOCKS))


def _outproj_kernel(oa_ref, ob_ref, x_ref, ga_ref, sh_ref, sc_ref, gga_ref, ggb_ref, gpost_ref, gpre_ref,
                    wo_ref, wr_hi_ref, wr_lo_ref, br_ref,
                    x1_ref, h2_ref, idx_ref, wts_ref, rank_ref, size_ref, base_ref, cnt_ref, run_ref):
    i = pl.program_id(0)

    @pl.when(i == 0)
    def _():
        run_ref[...] = jnp.zeros_like(run_ref)

    n_rows = x_ref.shape[0]
    parts = [slice(p * n_rows // OUT_PARTS, (p + 1) * n_rows // OUT_PARTS) for p in range(OUT_PARTS)]
    na = [_rms(oa_ref[r, :].astype(jnp.float32), gga_ref[...]).astype(jnp.bfloat16) for r in parts]
    nb = [_rms(ob_ref[r, :].astype(jnp.float32), ggb_ref[...]).astype(jnp.bfloat16) for r in parts]
    mix = [_dot(a, wo_ref[:WIDTH_A, :]) + _dot(b, wo_ref[WIDTH_A:, :]) for a, b in zip(na, nb)]
    x1 = [x_ref[r, :] + ga_ref[0] * _rms(m, gpost_ref[...]) for r, m in zip(parts, mix)]
    for r, v in zip(parts, x1):
        x1_ref[r, :] = v
    h2 = [_rms(v, gpre_ref[...]) * (1.0 + sc_ref[0]) + sh_ref[0] for v in x1]
    for r, v in zip(parts, h2):
        h2_ref[r, :] = v.astype(h2_ref.dtype)

    h_hi = [v.astype(jnp.bfloat16) for v in h2]
    h_lo = [(v - hi.astype(jnp.float32)).astype(jnp.bfloat16) for v, hi in zip(h2, h_hi)]
    logits = jnp.concatenate([_dot(hi, wr_hi_ref[...]) + _dot(lo, wr_hi_ref[...]) + _dot(hi, wr_lo_ref[...])
                              for hi, lo in zip(h_hi, h_lo)], axis=0)
    work = logits.T[:N_EXPERTS] + br_ref[...]
    e, t = work.shape
    row = lax.broadcasted_iota(jnp.int32, (e, t), 0)
    chosen = jnp.zeros((e, t), jnp.float32)
    vals, sels, hots = [], [], []
    for _k in range(TOP_K):
        mx = jnp.max(work, axis=0, keepdims=True)
        sel = jnp.min(jnp.where(work == mx, row, e), axis=0, keepdims=True)
        hot = row == sel
        vals.append(mx)
        sels.append(sel)
        hots.append(hot)
        work = jnp.where(hot, -jnp.inf, work)
        chosen = chosen + hot.astype(jnp.float32)
    ex = [jnp.exp(v - vals[0]) for v in vals]
    den = ex[0] + ex[1] + ex[2] + ex[3]

    earlier = (lax.broadcasted_iota(jnp.int32, (t, t), 0) < lax.broadcasted_iota(jnp.int32, (t, t), 1))
    prefix = _dot(chosen.astype(jnp.bfloat16), earlier.astype(jnp.bfloat16))
    ranks = [jnp.sum(jnp.where(h, prefix, 0.0), axis=0, keepdims=True) for h in hots]
    size = jnp.sum(chosen, axis=1, keepdims=True).astype(jnp.int32)
    size_ref[0] = size
    base_ref[0] = run_ref[...]
    run_ref[...] = run_ref[...] + size
    cnt_ref[...] = run_ref[...]

    idx_ref[0] = jnp.concatenate(sels, axis=0)
    wts_ref[0] = jnp.concatenate([v / den for v in ex], axis=0)
    rank_ref[0] = jnp.concatenate(ranks, axis=0).astype(jnp.int32)


def _outproj(oa, ob, x2d, mod3, gga, ggb, gpost, gpre, w_out, w_router, b_router, *, seq):
    t, d = x2d.shape
    tile = OUT_TILE
    per_seq = seq // tile
    row = lambda i: (i, 0)
    const = lambda i: (0, 0)
    modspec = lambda part: pl.BlockSpec((1, 1, d), lambda i: (i // per_seq, 0, part))
    per_tile = lambda shape: pl.BlockSpec((1,) + shape, lambda i: (i, 0, 0))
    n_steps = t // tile
    w_pad = jnp.zeros((d, LANES), jnp.float32).at[:, :N_EXPERTS].set(w_router)
    wr_hi = w_pad.astype(jnp.bfloat16)
    wr_lo = (w_pad - wr_hi.astype(jnp.float32)).astype(jnp.bfloat16)
    outs = pl.pallas_call(
        _outproj_kernel,
        grid=(n_steps,),
        in_specs=[pl.BlockSpec((tile, WIDTH_A), row), pl.BlockSpec((tile, WIDTH_B), row),
                  pl.BlockSpec((tile, d), row),
                  modspec(2), modspec(3), modspec(4),
                  pl.BlockSpec((1, WIDTH_A), const), pl.BlockSpec((1, WIDTH_B), const),
                  pl.BlockSpec((1, d), const), pl.BlockSpec((1, d), const),
                  pl.BlockSpec(w_out.shape, const), pl.BlockSpec((d, LANES), const), pl.BlockSpec((d, LANES), const),
                  pl.BlockSpec((N_EXPERTS, 1), const)],
        out_specs=[pl.BlockSpec((tile, d), row), pl.BlockSpec((tile, d), row),
                   per_tile((TOP_K, tile)), per_tile((TOP_K, tile)), per_tile((TOP_K, tile)),
                   per_tile((N_EXPERTS, 1)), per_tile((N_EXPERTS, 1)),
                   pl.BlockSpec((N_EXPERTS, 1), const)],
        out_shape=[jax.ShapeDtypeStruct((t, d), jnp.float32), jax.ShapeDtypeStruct((t, d), jnp.bfloat16),
                   jax.ShapeDtypeStruct((n_steps, TOP_K, tile), jnp.int32),
                   jax.ShapeDtypeStruct((n_steps, TOP_K, tile), jnp.float32),
                   jax.ShapeDtypeStruct((n_steps, TOP_K, tile), jnp.int32),
                   jax.ShapeDtypeStruct((n_steps, N_EXPERTS, 1), jnp.int32),
                   jax.ShapeDtypeStruct((n_steps, N_EXPERTS, 1), jnp.int32),
                   jax.ShapeDtypeStruct((N_EXPERTS, 1), jnp.int32)],
        scratch_shapes=[pltpu.VMEM((N_EXPERTS, 1), jnp.int32)],
        compiler_params=_params("arbitrary"),
        name="outproj_router",
    )(oa, ob, x2d, mod3, mod3, mod3, gga, ggb, gpost, gpre, w_out, wr_hi, wr_lo, b_router.reshape(N_EXPERTS, 1))
    return outs


def _piece_table(tab_ref, e):
    return (pl.multiple_of(tab_ref[e], ROW_TILE), pl.multiple_of(tab_ref[N_EXPERTS + e], ROW_TILE),
            pl.multiple_of(tab_ref[2 * N_EXPERTS + e], ROW_TILE))


def _to_tiles(x, ref, index=()):
    for c in range(ROW_TILE):
        ref[index + (pl.ds(c, x.shape[0], stride=ROW_TILE), slice(None))] = x[:, c * LANES:(c + 1) * LANES]


def _from_tiles(ref, n_rows, index=()):
    return jnp.concatenate([ref[index + (pl.ds(c, n_rows, stride=ROW_TILE), slice(None))]
                            for c in range(ROW_TILE)], axis=1)


def _dispatch_kernel(fill_from_ref, pad_end_ref, tab_ref, h_ref, lpos_ref, xs_ref,
                     stage_ref, zero_ref, pend_ref, sems, zsem, *, grid_steps):
    i = pl.program_id(0)
    n_steps = pl.num_programs(0)
    n_slots = stage_ref.shape[0]
    slot = i % n_slots
    n_local = stage_ref.shape[1] // ROW_TILE
    fill_rows = zero_ref.shape[0]

    def fill(row):
        return pltpu.make_async_copy(zero_ref.at[pl.ds(0, fill_rows)],
                                     xs_ref.at[pl.ds(pl.multiple_of(row, fill_rows), fill_rows)], zsem)

    def pad_fill(e):
        n = pl.multiple_of(pad_end_ref[e] - fill_from_ref[e], ROW_TILE)
        return n, pltpu.make_async_copy(zero_ref.at[pl.ds(0, n)],
                                        xs_ref.at[pl.ds(pl.multiple_of(fill_from_ref[e], ROW_TILE), n)], zsem)

    def drain(s):
        n = pl.multiple_of(pend_ref[s], ROW_TILE)

        @pl.when(n > 0)
        def _():
            pltpu.make_async_copy(stage_ref.at[s, pl.ds(0, n)], xs_ref.at[pl.ds(0, n)], sems.at[s]).wait()
        pend_ref[s] = 0

    @pl.when(i == 0)
    def _():
        zero_ref[...] = jnp.zeros_like(zero_ref)
        for s in range(n_slots):
            pend_ref[s] = 0

    for j in range(-(-N_EXPERTS // grid_steps)):
        e = i * -(-N_EXPERTS // grid_steps) + j

        @pl.when(e < N_EXPERTS)
        def _(e=e):
            n, copy = pad_fill(e)

            @pl.when(n > 0)
            def _():
                copy.start()

    row = lax.broadcasted_iota(jnp.int32, (n_local, h_ref.shape[0]), 0)
    place = row == lpos_ref[0, 0:1, :]
    for k in range(1, TOP_K):
        place = place | (row == lpos_ref[0, k:k + 1, :])
    rows = _dot(place.astype(jnp.bfloat16), h_ref[...])

    drain(slot)
    _to_tiles(rows, stage_ref, (slot,))

    total = 0
    for e in range(N_EXPERTS):
        loc, glob, n = _piece_table(tab_ref, e)

        @pl.when(n > 0)
        def _(loc=loc, glob=glob, n=n, e=e):
            pltpu.make_async_copy(stage_ref.at[slot, pl.ds(loc, n)], xs_ref.at[pl.ds(glob, n)],
                                  sems.at[slot]).start(priority=e % 2)
        total = total + n
    pend_ref[slot] = total

    tail0 = pad_end_ref[N_EXPERTS - 1]
    n_tail = (xs_ref.shape[0] - tail0) // fill_rows
    per_step = (n_tail + n_steps - 1) // n_steps

    def start_tail(j, c):
        t = i * per_step + j

        @pl.when(t < n_tail)
        def _():
            fill(tail0 + t * fill_rows).start()
        return c
    lax.fori_loop(0, per_step, start_tail, 0)

    @pl.when(i == n_steps - 1)
    def _():
        for s in range(n_slots):
            drain(s)

        def wait_tail(j, c):
            fill(0).wait()
            return c
        lax.fori_loop(0, n_tail, wait_tail, 0)

        def wait_pad(e, c):
            n, copy = pad_fill(e)

            @pl.when(n > 0)
            def _():
                copy.wait()
            return c
        lax.fori_loop(0, N_EXPERTS, wait_pad, 0)


def _dispatch(h2, tab_flat, lpos_t, fill_from, pad_end, *, n_rows):
    t, d = h2.shape
    tile = DISPATCH_TILE
    assert d == ROW_TILE * LANES
    n_local = tile * TOP_K
    grid_spec = pltpu.PrefetchScalarGridSpec(
        num_scalar_prefetch=2,
        grid=(t // tile,),
        in_specs=[pl.BlockSpec((TAB_WIDTH,), lambda i, *_: (i,), memory_space=pltpu.SMEM),
                  pl.BlockSpec((tile, d), lambda i, *_: (i, 0)),
                  pl.BlockSpec((1, TOP_K, tile), lambda i, *_: (i, 0, 0))],
        out_specs=pl.BlockSpec(memory_space=pl.ANY),
        scratch_shapes=[pltpu.VMEM((STAGE_SLOTS, n_local * ROW_TILE, LANES), jnp.float32),
                        pltpu.VMEM((FFN_STEP * ROW_TILE, LANES), jnp.float32),
                        pltpu.SMEM((STAGE_SLOTS,), jnp.int32),
                        pltpu.SemaphoreType.DMA((STAGE_SLOTS,)), pltpu.SemaphoreType.DMA(())],
    )
    return pl.pallas_call(
        functools.partial(_dispatch_kernel, grid_steps=t // tile),
        grid_spec=grid_spec,
        out_shape=jax.ShapeDtypeStruct((n_rows * ROW_TILE, LANES), jnp.float32),
        compiler_params=_params("arbitrary"),
        name="dispatch",
    )(fill_from, pad_end, tab_flat, h2, lpos_t)


def _ffn_kernel(te_ref, rows_ref, n_used_ref, x_ref, w1_ref, b1_ref, w2_ref, b2_ref, y_ref,
                w1p_ref, w2b_ref, act_ref):
    i = pl.program_id(0)
    live = i < n_used_ref[0]
    n_pair = w1_ref.shape[2] // FFN_CHUNK
    half = FFN_CHUNK // 2
    n_sub = jnp.where(live, (rows_ref[i] + FFN_TILE - 1) // FFN_TILE, 0)

    @pl.when(live & ((i == 0) | (te_ref[i] != te_ref[jnp.maximum(i - 1, 0)])))
    def _():
        src = lax.broadcasted_iota(jnp.int32, (FFN_CHUNK, FFN_CHUNK), 0)
        dst = lax.broadcasted_iota(jnp.int32, (FFN_CHUNK, FFN_CHUNK), 1)
        unzip = (src == jnp.where(dst < half, 2 * dst, 2 * (dst - half) + 1)).astype(jnp.bfloat16)
        for c in range(n_pair):
            cs = slice(c * FFN_CHUNK, (c + 1) * FFN_CHUNK)
            w1p_ref[:, cs] = _dot(w1_ref[0, :, cs].astype(jnp.bfloat16), unzip).astype(jnp.bfloat16)
        w2b_ref[...] = w2_ref[0].astype(jnp.bfloat16)

    def compute(n_rows):
        x = _from_tiles(x_ref, n_rows).astype(jnp.bfloat16)
        for c in range(0, n_pair, FFN_GROUP):
            cs = slice(c * FFN_CHUNK, (c + FFN_GROUP) * FFN_CHUNK)
            h = _dot(x, w1p_ref[:, cs]) + b1_ref[0, :, cs]
            g = jnp.concatenate([h[:, k * FFN_CHUNK:k * FFN_CHUNK + half] for k in range(FFN_GROUP)], axis=1)
            u = jnp.concatenate([h[:, k * FFN_CHUNK + half:(k + 1) * FFN_CHUNK] for k in range(FFN_GROUP)], axis=1)
            g = jnp.minimum(g, SWIGLU_LIMIT)
            u = jnp.clip(u, -SWIGLU_LIMIT, SWIGLU_LIMIT)
            act = g * jax.nn.sigmoid(SWIGLU_ALPHA * g) * (u + 1.0)
            act_ref[:n_rows, c * half:(c + FFN_GROUP) * half] = act.astype(act_ref.dtype)
        _to_tiles(_dot(act_ref[:n_rows, :], w2b_ref[...]) + b2_ref[0], y_ref)
        if n_rows * ROW_TILE < y_ref.shape[0]:
            y_ref[n_rows * ROW_TILE:, :] = jnp.zeros((y_ref.shape[0] - n_rows * ROW_TILE, LANES), y_ref.dtype)

    for tiles in range(1, FFN_STEP // FFN_TILE + 1):
        pl.when(n_sub == tiles)(functools.partial(compute, tiles * FFN_TILE))

    @pl.when(n_sub == 0)
    def _():
        y_ref[...] = jnp.zeros_like(y_ref)


def _unzip_bias(b1):
    e, f2 = b1.shape
    half = FFN_CHUNK // 2
    return b1.reshape(e, f2 // FFN_CHUNK, half, 2).transpose(0, 1, 3, 2).reshape(e, 1, f2)


def _ffn(xs, step_expert, step_rows, n_used, w1, b1, w2, b2):
    d, f2 = w1.shape[1:]
    n_steps = xs.shape[0] // (FFN_STEP * ROW_TILE)
    live = lambda i, nu: jnp.minimum(i, nu[0] - 1)
    wspec = lambda shape: pl.BlockSpec((1,) + shape, lambda i, te, nr, nu: (te[live(i, nu)], 0, 0))
    grid_spec = pltpu.PrefetchScalarGridSpec(
        num_scalar_prefetch=3,
        grid=(n_steps,),
        in_specs=[pl.BlockSpec((FFN_STEP * ROW_TILE, LANES), lambda i, te, nr, nu: (live(i, nu), 0)),
                  wspec((d, f2)), wspec((1, f2)), wspec((f2 // 2, d)), wspec((1, d))],
        out_specs=pl.BlockSpec((FFN_STEP * ROW_TILE, LANES), lambda i, te, nr, nu: (i, 0)),
        scratch_shapes=[pltpu.VMEM((d, f2), jnp.bfloat16), pltpu.VMEM((f2 // 2, d), jnp.bfloat16),
                        pltpu.VMEM((FFN_STEP, f2 // 2), jnp.bfloat16)],
    )
    return pl.pallas_call(
        _ffn_kernel,
        grid_spec=grid_spec,
        out_shape=jax.ShapeDtypeStruct(xs.shape, jnp.float32),
        compiler_params=_params("arbitrary"),
        name="expert_ffn",
    )(step_expert, step_rows, n_used, xs, w1, _unzip_bias(b1), w2, b2[:, None, :])


def _combine_kernel(*refs):
    tab_refs = refs[:LAND_SLOTS]
    lpos_ref, wts_ref, x1_ref, ga_ref, g_ref, y_ref, o_ref, buf_ref, sems = refs[LAND_SLOTS:]
    tab_ref = tab_refs[0]
    i = pl.program_id(0)
    n_steps = pl.num_programs(0)
    n_slots = buf_ref.shape[0]
    slot = i % n_slots
    tile = x1_ref.shape[0]
    n_local = buf_ref.shape[1] // ROW_TILE

    def fetch(tab, s):
        for e in range(N_EXPERTS):
            loc, glob, n = _piece_table(tab, e)

            @pl.when(n > 0)
            def _(loc=loc, glob=glob, n=n, e=e):
                pltpu.make_async_copy(y_ref.at[pl.ds(glob, n)], buf_ref.at[s, pl.ds(loc, n)],
                                      sems.at[s]).start(priority=e % 2)

    @pl.when(i == 0)
    def _():
        for s in range(n_slots - 1):
            pl.when(s < n_steps)(functools.partial(fetch, tab_refs[s], s))

    @pl.when(i + n_slots - 1 < n_steps)
    def _():
        fetch(tab_refs[n_slots - 1], (i + n_slots - 1) % n_slots)

    n = pl.multiple_of(tab_ref[3 * N_EXPERTS], ROW_TILE)

    @pl.when(n > 0)
    def _():
        pltpu.make_async_copy(y_ref.at[pl.ds(0, n)], buf_ref.at[slot, pl.ds(0, n)], sems.at[slot]).wait()

    rows = _from_tiles(buf_ref, n_local, (slot,)).astype(jnp.bfloat16)
    part = tile // OUT_PARTS
    col = lax.broadcasted_iota(jnp.int32, (part, n_local), 1)
    for p in range(OUT_PARTS):
        r = slice(p * part, (p + 1) * part)
        pos = lpos_ref[r, :]
        w = wts_ref[r, :]
        pick = jnp.where(col == pos[:, 0:1], w[:, 0:1], 0.0)
        for k in range(1, TOP_K):
            pick = pick + jnp.where(col == pos[:, k:k + 1], w[:, k:k + 1], 0.0)
        hi = pick.astype(jnp.bfloat16)
        lo = (pick - hi.astype(jnp.float32)).astype(jnp.bfloat16)
        y = _dot(hi, rows) + _dot(lo, rows)
        o_ref[r, :] = x1_ref[r, :] + ga_ref[0] * _rms(y, g_ref[...])


def _combine(ys, tab_flat, lpos, wts, x1, mod3, g_post, *, seq):
    t, d = x1.shape
    tile = DISPATCH_TILE
    n_steps = t // tile
    per_seq = seq // tile
    n_local = tile * TOP_K
    return pl.pallas_call(
        _combine_kernel,
        grid=(n_steps,),
        in_specs=[pl.BlockSpec((TAB_WIDTH,), lambda i, s=s: (jnp.minimum(i + s, n_steps - 1),),
                               memory_space=pltpu.SMEM) for s in range(LAND_SLOTS)] + [
                  pl.BlockSpec((tile, TOP_K), lambda i: (i, 0)),
                  pl.BlockSpec((tile, TOP_K), lambda i: (i, 0)),
                  pl.BlockSpec((tile, d), lambda i: (i, 0)),
                  pl.BlockSpec((1, 1, d), lambda i: (i // per_seq, 0, 5)),
                  pl.BlockSpec((1, d), lambda i: (0, 0)),
                  pl.BlockSpec(memory_space=pl.ANY)],
        out_specs=pl.BlockSpec((tile, d), lambda i: (i, 0)),
        out_shape=jax.ShapeDtypeStruct((t, d), jnp.float32),
        scratch_shapes=[pltpu.VMEM((LAND_SLOTS, n_local * ROW_TILE, LANES), ys.dtype),
                        pltpu.SemaphoreType.DMA((LAND_SLOTS,))],
        compiler_params=_params("arbitrary"),
        name="combine",
    )(*([tab_flat] * LAND_SLOTS), lpos, wts, x1, mod3, g_post, ys)


def _rope_tables(seq):
    pos = np.arange(seq)
    n_freq = HEAD_DIM // 4
    freqs = ROPE_BASE ** (-jnp.arange(n_freq, dtype=jnp.float32) / n_freq)
    rows = jnp.asarray(pos // GRID_W, jnp.float32)[:, None] * freqs[None, :]
    cols = jnp.asarray(pos % GRID_W, jnp.float32)[:, None] * freqs[None, :]
    ang = jnp.concatenate([rows, rows, cols, cols], axis=1)
    sign = np.tile(np.repeat([-1.0, 1.0], n_freq), 2).astype(np.float32)
    cos = jnp.cos(ang)
    sin = jnp.sin(ang) * sign[None, :]
    reps = LANES // HEAD_DIM
    return jnp.tile(cos, (1, reps)), jnp.tile(sin, (1, reps))


def kernel(x, c, ctx, c_ctx, w_ada, b_ada, g_pre_mix, g_post_mix, g_pre_ffn, g_post_ffn, w_in, g_grp_a, g_grp_b,
           sink_a, rpb_b, w_out, w_router, b_router, w_mlp1, b_mlp1, w_mlp2, b_mlp2):
    batch, seq, d = x.shape
    n_ctx = ctx.shape[1]
    assert w_ada.shape[0] == 1, "single layer"
    assert seq % (ATT_BLOCKS * GRID_W * NB_Q_ROWS) == 0 and seq // GRID_W >= NB_K_ROWS
    assert seq % (ATT_BLOCKS * BLOCK_A) == 0 and seq >= BLOCK_A + 2 * WINDOW
    n_tok = batch * seq
    bf16 = jnp.bfloat16

    def pair_heads(a, axis):
        shape = a.shape
        a = a.reshape(shape[:axis] + (N_KV_A, GQA_GROUP, HEAD_DIM) + shape[axis + 1:])
        return jnp.swapaxes(a, axis, axis + 1).reshape(shape)

    mod_rows = -(-(batch + 1) // 8) * 8
    cc = jnp.zeros((mod_rows, d), jnp.float32).at[:batch].set(c).at[batch].set(c_ctx)
    mod3 = _ada(cc, w_ada[0], b_ada[0]).reshape(mod_rows, 1, 6 * d)

    w_in0 = w_in[0]
    w_lat = jnp.concatenate([pair_heads(w_in0[:, :WIDTH_A], 1), w_in0[:, WIDTH_A:]], axis=1).astype(bf16)
    kv_a_end = WIDTH_A + 2 * WIDTH_KV_A
    w_ctx = jnp.concatenate([w_in0[:, WIDTH_A:kv_a_end], w_in0[:, kv_a_end + WIDTH_B:]], axis=1).astype(bf16)
    g_pre = g_pre_mix[0].reshape(1, d)
    qa, ka, va, qb, kb, vb = _inproj(x.reshape(n_tok, d), mod3, g_pre, w_lat, _rope_tables(seq),
                                     seq=seq, mod_row0=0, latent=True)
    kac, vac, kbc, vbc = _inproj(ctx.reshape(batch * n_ctx, d), mod3, g_pre, w_ctx, None,
                                 seq=n_ctx, mod_row0=batch, latent=False)

    oa = _window_attention(qa, ka.reshape(batch, seq, -1), va.reshape(batch, seq, -1),
                           kac.reshape(batch, n_ctx, -1), vac.reshape(batch, n_ctx, -1),
                           sink_a[0].astype(jnp.float32) * LOG2E, batch=batch, seq=seq)
    tabs, cls_of_block = _nb_bias_tables(rpb_b[0], seq // GRID_W)
    ob = _neighbourhood_attention(qb, kb.reshape(batch, seq, -1), vb.reshape(batch, seq, -1),
                                  kbc.reshape(batch, n_ctx, -1), vbc.reshape(batch, n_ctx, -1),
                                  tabs, cls_of_block, batch=batch, seq=seq)

    w_out0 = w_out[0]
    w_o = jnp.concatenate([pair_heads(w_out0[:WIDTH_A], 0), w_out0[WIDTH_A:]], axis=0).astype(bf16)
    x1, h2, idx, wts, rank, size, base, counts = _outproj(
        oa, ob, x.reshape(n_tok, d), mod3,
        pair_heads(g_grp_a[0], 0).reshape(1, -1), g_grp_b[0].reshape(1, -1),
        g_post_mix[0].reshape(1, d), g_pre_ffn[0].reshape(1, d),
        w_o, w_router[0], b_router[0].reshape(1, -1), seq=seq)

    n_tok_tiles = n_tok // DISPATCH_TILE
    size = size.reshape(n_tok_tiles, N_EXPERTS)
    counts = counts.reshape(-1)
    padded = (counts + FFN_STEP - 1) // FFN_STEP * FFN_STEP
    pad_end = jnp.cumsum(padded).astype(jnp.int32)
    pad_start = pad_end - padded
    n_steps = n_tok * TOP_K // FFN_STEP + N_EXPERTS
    n_used = (pad_end[-1:] // FFN_STEP).astype(jnp.int32)
    step_row0 = jnp.arange(n_steps, dtype=jnp.int32) * FFN_STEP
    step_expert = jnp.minimum(jnp.sum(step_row0[:, None] >= pad_end[None, :], axis=1),
                              N_EXPERTS - 1).astype(jnp.int32)
    own = step_expert[:, None] == jnp.arange(N_EXPERTS, dtype=jnp.int32)
    step_rows = jnp.clip(jnp.sum(jnp.where(own, (pad_start + counts)[None, :], 0), axis=1) - step_row0,
                         0, FFN_STEP).astype(jnp.int32)
    fill_from = (pad_start + counts).astype(jnp.int32)
    local0 = jnp.cumsum(size, axis=1) - size
    global0 = pad_start[None, :] + base.reshape(n_tok_tiles, N_EXPERTS)
    tab = jnp.concatenate([local0, global0, size, jnp.sum(size, axis=1, keepdims=True),
                           jnp.zeros((n_tok_tiles, TAB_WIDTH - 3 * N_EXPERTS - 1), jnp.int32)], axis=1)
    tab = (tab * ROW_TILE).reshape(-1).astype(jnp.int32)
    hot = idx[:, :, None, :] == jnp.arange(N_EXPERTS, dtype=jnp.int32)[None, None, :, None]
    lpos_t = rank + jnp.sum(jnp.where(hot, local0[:, None, :, None], 0), axis=2)
    by_token = lambda a: jnp.swapaxes(a, 1, 2).reshape(n_tok, TOP_K)

    xs = _dispatch(h2, tab, lpos_t, fill_from * ROW_TILE, pad_end * ROW_TILE, n_rows=n_steps * FFN_STEP)
    ys = _ffn(xs, step_expert, step_rows, n_used, w_mlp1[0], b_mlp1[0], w_mlp2[0], b_mlp2[0])
    out = _combine(ys, tab, by_token(lpos_t), by_token(wts), x1, mod3, g_post_ffn[0].reshape(1, d), seq=seq)
    return out.reshape(batch, seq, d)
```

```python
import functools

import numpy as np
import jax
import jax.numpy as jnp
from jax import lax
from jax.experimental import pallas as pl
from jax.experimental.pallas import tpu as pltpu

GRID_W = 64
HEAD_DIM = 64
N_HEADS_A = 8
N_KV_A = 2
GQA_GROUP = N_HEADS_A // N_KV_A
N_HEADS_B = 8
WIDTH_A = N_HEADS_A * HEAD_DIM
WIDTH_KV_A = N_KV_A * HEAD_DIM
WIDTH_B = N_HEADS_B * HEAD_DIM
WINDOW = 128
BLOCK_A = 128
NA_ROWS = 8
NA_COLS = 16
N_EXPERTS = 32
TOP_K = 4
SWIGLU_LIMIT = 7.0
SWIGLU_ALPHA = 1.702
ROPE_BASE = 10000.0
EPS = 1e-6
NEG_INF = -1e30

LANES = 128
VMEM_LIMIT = 56 * 1024 * 1024

LOG2E = 1.4426950408889634
ATT_BLOCKS = 8
NB_Q_ROWS = 2
NB_K_ROWS = NB_Q_ROWS + NA_ROWS
PROJ_TILE = 1024
OUT_TILE = 256
OUT_PARTS = 2
STAGE_SLOTS = 4
LAND_SLOTS = 4
FFN_TILE = 256
FFN_STEP = 4 * FFN_TILE
FFN_CHUNK = 256
FFN_GROUP = 4
DISPATCH_TILE = OUT_TILE
ROW_TILE = 8
TAB_WIDTH = 128


def _params(*sem):
    return pltpu.CompilerParams(dimension_semantics=sem, vmem_limit_bytes=VMEM_LIMIT)


def _rms(x, g):
    return x * lax.rsqrt(jnp.mean(x * x, axis=-1, keepdims=True) + EPS) * g


def _dot(a, b):
    return jnp.dot(a, b, preferred_element_type=jnp.float32)


def _dot_nt(a, b):
    return lax.dot_general(a, b, (((1,), (1,)), ((), ())), preferred_element_type=jnp.float32)


def _ada_kernel(c_ref, w_ref, b_ref, o_ref):
    c = c_ref[...]
    s = (c * jax.nn.sigmoid(c)).astype(jnp.bfloat16)
    o_ref[...] = _dot(s, w_ref[...].astype(jnp.bfloat16)) + b_ref[...]


def _ada(cc, w_ada, b_ada):
    rows, d = cc.shape
    n_out = w_ada.shape[1]
    return pl.pallas_call(
        _ada_kernel,
        grid=(n_out // d,),
        in_specs=[pl.BlockSpec((rows, d), lambda j: (0, 0)),
                  pl.BlockSpec((d, d), lambda j: (0, j)),
                  pl.BlockSpec((1, d), lambda j: (0, j))],
        out_specs=pl.BlockSpec((rows, d), lambda j: (0, j)),
        out_shape=jax.ShapeDtypeStruct((rows, n_out), jnp.float32),
        compiler_params=_params("arbitrary"),
        name="ada",
    )(cc, w_ada, b_ada.reshape(1, n_out))


def _rope(x, cos, sin):
    w = x.shape[1]
    reps = w // LANES
    if reps > 1:
        cos = jnp.concatenate([cos] * reps, axis=1)
        sin = jnp.concatenate([sin] * reps, axis=1)
    lane = lax.broadcasted_iota(jnp.int32, x.shape, 1)
    quarter = HEAD_DIM // 4
    partner = jnp.where(lane % (2 * quarter) < quarter,
                        pltpu.roll(x, w - quarter, 1), pltpu.roll(x, quarter, 1))
    return x * cos + partner * sin


def _inproj_kernel(x_ref, sh_ref, sc_ref, g_ref, w_ref, *rest, latent):
    x = x_ref[...]
    h = _rms(x, g_ref[...]) * (1.0 + sc_ref[0]) + sh_ref[0]
    p = _dot(h.astype(jnp.bfloat16), w_ref[...])
    if latent:
        cos_ref, sin_ref, qa_ref, ka_ref, va_ref, qb_ref, kb_ref, vb_ref = rest
        cos, sin = cos_ref[...], sin_ref[...]
        scale = HEAD_DIM ** -0.5 * LOG2E
        o = 0
        qa_ref[...] = (_rope(p[:, o:o + WIDTH_A], cos, sin) * scale).astype(qa_ref.dtype)
        o += WIDTH_A
        ka_ref[...] = _rope(p[:, o:o + WIDTH_KV_A], cos, sin).astype(ka_ref.dtype)
        o += WIDTH_KV_A
        va_ref[...] = p[:, o:o + WIDTH_KV_A].astype(va_ref.dtype)
        o += WIDTH_KV_A
        qb_ref[...] = (p[:, o:o + WIDTH_B] * scale).astype(qb_ref.dtype)
        o += WIDTH_B
    else:
        ka_ref, va_ref, kb_ref, vb_ref = rest
        o = 0
        ka_ref[...] = p[:, o:o + WIDTH_KV_A].astype(ka_ref.dtype)
        o += WIDTH_KV_A
        va_ref[...] = p[:, o:o + WIDTH_KV_A].astype(va_ref.dtype)
        o += WIDTH_KV_A
    kb_ref[...] = p[:, o:o + WIDTH_B].astype(kb_ref.dtype)
    o += WIDTH_B
    vb_ref[...] = p[:, o:o + WIDTH_B].astype(vb_ref.dtype)


def _inproj(x2d, mod3, g_pre, w, rope_tabs, *, seq, mod_row0, latent):
    t, d = x2d.shape
    tile = min(PROJ_TILE, seq)
    per_seq = seq // tile
    if latent:
        mod_row = lambda i: i // per_seq
    else:
        mod_row = lambda i: mod_row0
    in_specs = [pl.BlockSpec((tile, d), lambda i: (i, 0)),
                pl.BlockSpec((1, 1, d), lambda i: (mod_row(i), 0, 0)),
                pl.BlockSpec((1, 1, d), lambda i: (mod_row(i), 0, 1)),
                pl.BlockSpec((1, d), lambda i: (0, 0)),
                pl.BlockSpec(w.shape, lambda i: (0, 0))]
    args = [x2d, mod3, mod3, g_pre, w]
    widths = [WIDTH_KV_A, WIDTH_KV_A, WIDTH_B, WIDTH_B]
    if latent:
        in_specs += [pl.BlockSpec((tile, LANES), lambda i: (i % per_seq, 0))] * 2
        args += list(rope_tabs)
        widths = [WIDTH_A, WIDTH_KV_A, WIDTH_KV_A, WIDTH_B, WIDTH_B, WIDTH_B]
    return pl.pallas_call(
        functools.partial(_inproj_kernel, latent=latent),
        grid=(t // tile,),
        in_specs=in_specs,
        out_specs=[pl.BlockSpec((tile, wd), lambda i: (i, 0)) for wd in widths],
        out_shape=[jax.ShapeDtypeStruct((t, wd), jnp.bfloat16) for wd in widths],
        compiler_params=_params("parallel"),
        name="inproj_latent" if latent else "inproj_ctx",
    )(*args)


def _ones_beside(v):
    return jnp.concatenate([v, jnp.ones_like(v)], axis=1)


def _group_attention(q_ref, o_ref, rows, groups, s_ref, p_ref, m_ref):
    tq = rows.stop - rows.start
    low = lax.broadcasted_iota(jnp.int32, (tq, LANES), 1) < HEAD_DIM
    layout = []
    base = 0
    for pairs, k_loc, k_ctx, _, _, _, _ in groups:
        heads = [(j, 0) for j in pairs] + [(j, 1) for j in pairs]
        n_loc = k_loc.shape[0]
        stacked = []
        for j, half in heads:
            q = q_ref[rows, j * LANES:(j + 1) * LANES]
            stacked.append(jnp.where(low if half == 0 else ~low, q, jnp.zeros_like(q)))
        qs = jnp.concatenate(stacked, axis=0)
        n = len(heads) * tq
        s_ref[base:base + n, :] = _dot_nt(qs, jnp.concatenate([k_loc, k_ctx], axis=0))
        layout.append((base, heads, n_loc))
        base += n
    for (base, heads, n_loc), group in zip(layout, groups):
        bias_of, sink_of = group[5], group[6]
        for g, (j, half) in enumerate(heads):
            r = slice(base + g * tq, base + (g + 1) * tq)
            s_loc = s_ref[r, :n_loc] + bias_of(j, half)
            s_ctx = s_ref[r, n_loc:]
            m = jnp.maximum(jnp.max(s_loc, axis=1, keepdims=True), jnp.max(s_ctx, axis=1, keepdims=True))
            sink = sink_of(j, half)
            if sink is not None:
                m = jnp.maximum(m, sink)
            p_ref[r, :n_loc] = jnp.exp2((s_loc - m).astype(jnp.bfloat16))
            p_ref[r, n_loc:] = jnp.exp2((s_ctx - m).astype(jnp.bfloat16))
            m_ref[r, :] = m
    for (base, heads, n_loc), group in zip(layout, groups):
        pairs, _, _, v_loc, v_ctx, _, sink_of = group
        n = len(heads) * tq
        both = _dot(p_ref[base:base + n, :], _ones_beside(jnp.concatenate([v_loc, v_ctx], axis=0)))
        for a, j in enumerate(pairs):
            outs = []
            for half in range(2):
                g = half * len(pairs) + a
                r = slice(g * tq, (g + 1) * tq)
                den = both[r, LANES:]
                sink = sink_of(j, half)
                if sink is not None:
                    den = den + jnp.exp2(sink - m_ref[base + g * tq:base + (g + 1) * tq, :])
                outs.append(both[r, :LANES] / den)
            o_ref[rows, j * LANES:(j + 1) * LANES] = jnp.where(low, outs[0], outs[1]).astype(o_ref.dtype)


def _attention_scratch(tq, n_heads, n_keys):
    return [pltpu.VMEM((n_heads * tq, n_keys), jnp.float32), pltpu.VMEM((n_heads * tq, n_keys), jnp.bfloat16),
            pltpu.VMEM((n_heads * tq, 1), jnp.float32)]


def _win_kernel(sink_ref, q_ref, k_ref, v_ref, kc_ref, vc_ref, o_ref, *scratch, seq):
    span = BLOCK_A + 2 * WINDOW
    pairs = list(range(GQA_GROUP))
    for sb in range(ATT_BLOCKS):
        i = pl.program_id(1) * ATT_BLOCKS + sb
        start = pl.multiple_of(jnp.clip(i * BLOCK_A - WINDOW, 0, seq - span), BLOCK_A)
        qpos = i * BLOCK_A + lax.broadcasted_iota(jnp.int32, (BLOCK_A, span), 0)
        kpos = start + lax.broadcasted_iota(jnp.int32, (BLOCK_A, span), 1)
        bias = jnp.where(jnp.abs(kpos - qpos) <= WINDOW, 0.0, NEG_INF).astype(jnp.float32)
        groups = [([j], k_ref[0, pl.ds(start, span), :], kc_ref[0], v_ref[0, pl.ds(start, span), :], vc_ref[0],
                   lambda j, half: bias, lambda j, half: sink_ref[half * GQA_GROUP + j]) for j in pairs]
        _group_attention(q_ref, o_ref, slice(sb * BLOCK_A, (sb + 1) * BLOCK_A), groups, *scratch)


def _window_attention(qa, ka, va, kac, vac, sink, *, batch, seq):
    tq = ATT_BLOCKS * BLOCK_A
    nb = seq // tq
    n_ctx = kac.shape[1]
    return pl.pallas_call(
        functools.partial(_win_kernel, seq=seq),
        grid=(batch, nb),
        in_specs=[pl.BlockSpec(memory_space=pltpu.SMEM),
                  pl.BlockSpec((tq, WIDTH_A), lambda b, i: (b * nb + i, 0)),
                  pl.BlockSpec((1, seq, WIDTH_KV_A), lambda b, i: (b, 0, 0)),
                  pl.BlockSpec((1, seq, WIDTH_KV_A), lambda b, i: (b, 0, 0)),
                  pl.BlockSpec((1, n_ctx, WIDTH_KV_A), lambda b, i: (b, 0, 0)),
                  pl.BlockSpec((1, n_ctx, WIDTH_KV_A), lambda b, i: (b, 0, 0))],
        out_specs=pl.BlockSpec((tq, WIDTH_A), lambda b, i: (b * nb + i, 0)),
        out_shape=jax.ShapeDtypeStruct((batch * seq, WIDTH_A), jnp.bfloat16),
        scratch_shapes=_attention_scratch(BLOCK_A, N_HEADS_A, BLOCK_A + 2 * WINDOW + n_ctx),
        compiler_params=_params("parallel", "arbitrary"),
        name="window_attention",
    )(sink, qa, ka, va, kac, vac)


def _nb_kernel(q_ref, k_ref, v_ref, kc_ref, vc_ref, *rest, rows_n):
    tab_refs, o_ref, scratch = rest[:ATT_BLOCKS], rest[ATT_BLOCKS], rest[ATT_BLOCKS + 1:]
    n_keys = NB_K_ROWS * GRID_W
    tq = NB_Q_ROWS * GRID_W
    for sb in range(ATT_BLOCKS):
        m = pl.program_id(1) * ATT_BLOCKS + sb
        start_row = jnp.clip(NB_Q_ROWS * m - NA_ROWS // 2, 0, rows_n - NB_K_ROWS)
        start = pl.multiple_of(start_row * GRID_W, LANES)
        tab_ref = tab_refs[sb]
        groups = []
        for j in range(q_ref.shape[1] // LANES):
            cols = slice(j * LANES, (j + 1) * LANES)
            groups.append(([j], k_ref[0, pl.ds(start, n_keys), cols], kc_ref[0, :, cols],
                           v_ref[0, pl.ds(start, n_keys), cols], vc_ref[0, :, cols],
                           lambda j, half: tab_ref[0, 2 * j + half].astype(jnp.float32), lambda j, half: None))
        _group_attention(q_ref, o_ref, slice(sb * tq, (sb + 1) * tq), groups, *scratch)


def _nb_classes(rows_n):
    n_blocks = rows_n // NB_Q_ROWS
    sig = {}
    cls_of_block = []
    reps = []
    for m in range(n_blocks):
        start_row = int(np.clip(NB_Q_ROWS * m - NA_ROWS // 2, 0, rows_n - NB_K_ROWS))
        key = tuple((start_row - r, int(np.clip(r - NA_ROWS // 2, 0, rows_n - NA_ROWS)) - r)
                    for r in range(NB_Q_ROWS * m, NB_Q_ROWS * (m + 1)))
        if key not in sig:
            sig[key] = len(reps)
            reps.append(m)
        cls_of_block.append(sig[key])
    return np.asarray(cls_of_block, np.int32), reps


def _nb_bias_tables(rpb, rows_n):
    cls_of_block, reps = _nb_classes(rows_n)
    n_heads = rpb.shape[0]
    cq = np.arange(GRID_W)[:, None]
    ck = np.arange(GRID_W)[None, :]
    cs = np.clip(cq - NA_COLS // 2, 0, GRID_W - NA_COLS)
    col_ok = (ck >= cs) & (ck < cs + NA_COLS)
    pick = ((ck - cq + NA_COLS - 1)[None] == np.arange(2 * NA_COLS - 1)[:, None, None]) & col_ok[None]
    tiles = jnp.einsum('hrd,dqk->hrqk', rpb.astype(jnp.float32), jnp.asarray(pick, jnp.float32),
                       precision=lax.Precision.HIGHEST)
    tiles = jnp.where(jnp.asarray(col_ok)[None, None], tiles * LOG2E, NEG_INF).astype(jnp.bfloat16)
    blocked = jnp.full((n_heads, GRID_W, GRID_W), NEG_INF, jnp.bfloat16)
    tabs = []
    for m in reps:
        start_row = int(np.clip(NB_Q_ROWS * m - NA_ROWS // 2, 0, rows_n - NB_K_ROWS))
        q_rows = []
        for r in range(NB_Q_ROWS * m, NB_Q_ROWS * (m + 1)):
            rs = int(np.clip(r - NA_ROWS // 2, 0, rows_n - NA_ROWS))
            q_rows.append(jnp.concatenate(
                [tiles[:, krow - r + NA_ROWS - 1] if rs <= krow < rs + NA_ROWS else blocked
                 for krow in range(start_row, start_row + NB_K_ROWS)], axis=2))
        tabs.append(jnp.concatenate(q_rows, axis=1))
    return jnp.stack(tabs), cls_of_block


def _neighbourhood_attention(qb, kb, vb, kbc, vbc, tabs, cls_of_block, *, batch, seq):
    rows_n = seq // GRID_W
    tq = ATT_BLOCKS * NB_Q_ROWS * GRID_W
    nstep = seq // tq
    n_ctx = kbc.shape[1]

    def tab_spec(sb):
        return pl.BlockSpec((1,) + tabs.shape[1:], lambda b, m, c: (c[m * ATT_BLOCKS + sb], 0, 0, 0))

    grid_spec = pltpu.PrefetchScalarGridSpec(
        num_scalar_prefetch=1,
        grid=(batch, nstep),
        in_specs=[pl.BlockSpec((tq, WIDTH_B), lambda b, m, c: (b * nstep + m, 0)),
                  pl.BlockSpec((1, seq, WIDTH_B), lambda b, m, c: (b, 0, 0)),
                  pl.BlockSpec((1, seq, WIDTH_B), lambda b, m, c: (b, 0, 0)),
                  pl.BlockSpec((1, n_ctx, WIDTH_B), lambda b, m, c: (b, 0, 0)),
                  pl.BlockSpec((1, n_ctx, WIDTH_B), lambda b, m, c: (b, 0, 0))]
                 + [tab_spec(sb) for sb in range(ATT_BLOCKS)],
        out_specs=pl.BlockSpec((tq, WIDTH_B), lambda b, m, c: (b * nstep + m, 0)),
        scratch_shapes=_attention_scratch(NB_Q_ROWS * GRID_W, N_HEADS_B, NB_K_ROWS * GRID_W + n_ctx),
    )

    def body(c_ref, *refs):
        _nb_kernel(*refs, rows_n=rows_n)

    return pl.pallas_call(
        body,
        grid_spec=grid_spec,
        out_shape=jax.ShapeDtypeStruct((batch * seq, WIDTH_B), jnp.bfloat16),
        compiler_params=_params("parallel", "arbitrary"),
        name="neighbourhood_attention",
    )(jnp.asarray(cls_of_block), qb, kb, vb, kbc, vbc, *([tabs] * ATT_BLOCKS))


def _outproj_kernel(oa_ref, ob_ref, x_ref, ga_ref, sh_ref, sc_ref, gga_ref, ggb_ref, gpost_ref, gpre_ref,
                    wo_ref, wr_hi_ref, wr_lo_ref, br_ref,
                    x1_ref, h2_ref, idx_ref, wts_ref, rank_ref, size_ref, base_ref, cnt_ref, run_ref):
    i = pl.program_id(0)

    @pl.when(i == 0)
    def _():
        run_ref[...] = jnp.zeros_like(run_ref)

    n_rows = x_ref.shape[0]
    parts = [slice(p * n_rows // OUT_PARTS, (p + 1) * n_rows // OUT_PARTS) for p in range(OUT_PARTS)]
    na = [_rms(oa_ref[r, :].astype(jnp.float32), gga_ref[...]).astype(jnp.bfloat16) for r in parts]
    nb = [_rms(ob_ref[r, :].astype(jnp.float32), ggb_ref[...]).astype(jnp.bfloat16) for r in parts]
    mix = [_dot(a, wo_ref[:WIDTH_A, :]) + _dot(b, wo_ref[WIDTH_A:, :]) for a, b in zip(na, nb)]
    x1 = [x_ref[r, :] + ga_ref[0] * _rms(m, gpost_ref[...]) for r, m in zip(parts, mix)]
    for r, v in zip(parts, x1):
        x1_ref[r, :] = v
    h2 = [_rms(v, gpre_ref[...]) * (1.0 + sc_ref[0]) + sh_ref[0] for v in x1]
    for r, v in zip(parts, h2):
        h2_ref[r, :] = v.astype(h2_ref.dtype)

    h_hi = [v.astype(jnp.bfloat16) for v in h2]
    h_lo = [(v - hi.astype(jnp.float32)).astype(jnp.bfloat16) for v, hi in zip(h2, h_hi)]
    logits = jnp.concatenate([_dot(hi, wr_hi_ref[...]) + _dot(lo, wr_hi_ref[...]) + _dot(hi, wr_lo_ref[...])
                              for hi, lo in zip(h_hi, h_lo)], axis=0)
    work = logits.T[:N_EXPERTS] + br_ref[...]
    e, t = work.shape
    row = lax.broadcasted_iota(jnp.int32, (e, t), 0)
    chosen = jnp.zeros((e, t), jnp.float32)
    vals, sels, hots = [], [], []
    for _k in range(TOP_K):
        mx = jnp.max(work, axis=0, keepdims=True)
        sel = jnp.min(jnp.where(work == mx, row, e), axis=0, keepdims=True)
        hot = row == sel
        vals.append(mx)
        sels.append(sel)
        hots.append(hot)
        work = jnp.where(hot, -jnp.inf, work)
        chosen = chosen + hot.astype(jnp.float32)
    ex = [jnp.exp(v - vals[0]) for v in vals]
    den = ex[0] + ex[1] + ex[2] + ex[3]

    earlier = (lax.broadcasted_iota(jnp.int32, (t, t), 0) < lax.broadcasted_iota(jnp.int32, (t, t), 1))
    prefix = _dot(chosen.astype(jnp.bfloat16), earlier.astype(jnp.bfloat16))
    ranks = [jnp.sum(jnp.where(h, prefix, 0.0), axis=0, keepdims=True) for h in hots]
    size = jnp.sum(chosen, axis=1, keepdims=True).astype(jnp.int32)
    size_ref[0] = size
    base_ref[0] = run_ref[...]
    run_ref[...] = run_ref[...] + size
    cnt_ref[...] = run_ref[...]

    idx_ref[0] = jnp.concatenate(sels, axis=0)
    wts_ref[0] = jnp.concatenate([v / den for v in ex], axis=0)
    rank_ref[0] = jnp.concatenate(ranks, axis=0).astype(jnp.int32)


def _outproj(oa, ob, x2d, mod3, gga, ggb, gpost, gpre, w_out, w_router, b_router, *, seq):
    t, d = x2d.shape
    tile = OUT_TILE
    per_seq = seq // tile
    row = lambda i: (i, 0)
    const = lambda i: (0, 0)
    modspec = lambda part: pl.BlockSpec((1, 1, d), lambda i: (i // per_seq, 0, part))
    per_tile = lambda shape: pl.BlockSpec((1,) + shape, lambda i: (i, 0, 0))
    n_steps = t // tile
    w_pad = jnp.zeros((d, LANES), jnp.float32).at[:, :N_EXPERTS].set(w_router)
    wr_hi = w_pad.astype(jnp.bfloat16)
    wr_lo = (w_pad - wr_hi.astype(jnp.float32)).astype(jnp.bfloat16)
    outs = pl.pallas_call(
        _outproj_kernel,
        grid=(n_steps,),
        in_specs=[pl.BlockSpec((tile, WIDTH_A), row), pl.BlockSpec((tile, WIDTH_B), row),
                  pl.BlockSpec((tile, d), row),
                  modspec(2), modspec(3), modspec(4),
                  pl.BlockSpec((1, WIDTH_A), const), pl.BlockSpec((1, WIDTH_B), const),
                  pl.BlockSpec((1, d), const), pl.BlockSpec((1, d), const),
                  pl.BlockSpec(w_out.shape, const), pl.BlockSpec((d, LANES), const), pl.BlockSpec((d, LANES), const),
                  pl.BlockSpec((N_EXPERTS, 1), const)],
        out_specs=[pl.BlockSpec((tile, d), row), pl.BlockSpec((tile, d), row),
                   per_tile((TOP_K, tile)), per_tile((TOP_K, tile)), per_tile((TOP_K, tile)),
                   per_tile((N_EXPERTS, 1)), per_tile((N_EXPERTS, 1)),
                   pl.BlockSpec((N_EXPERTS, 1), const)],
        out_shape=[jax.ShapeDtypeStruct((t, d), jnp.float32), jax.ShapeDtypeStruct((t, d), jnp.bfloat16),
                   jax.ShapeDtypeStruct((n_steps, TOP_K, tile), jnp.int32),
                   jax.ShapeDtypeStruct((n_steps, TOP_K, tile), jnp.float32),
                   jax.ShapeDtypeStruct((n_steps, TOP_K, tile), jnp.int32),
                   jax.ShapeDtypeStruct((n_steps, N_EXPERTS, 1), jnp.int32),
                   jax.ShapeDtypeStruct((n_steps, N_EXPERTS, 1), jnp.int32),
                   jax.ShapeDtypeStruct((N_EXPERTS, 1), jnp.int32)],
        scratch_shapes=[pltpu.VMEM((N_EXPERTS, 1), jnp.int32)],
        compiler_params=_params("arbitrary"),
        name="outproj_router",
    )(oa, ob, x2d, mod3, mod3, mod3, gga, ggb, gpost, gpre, w_out, wr_hi, wr_lo, b_router.reshape(N_EXPERTS, 1))
    return outs


def _piece_table(tab_ref, e):
    return (pl.multiple_of(tab_ref[e], ROW_TILE), pl.multiple_of(tab_ref[N_EXPERTS + e], ROW_TILE),
            pl.multiple_of(tab_ref[2 * N_EXPERTS + e], ROW_TILE))


def _to_tiles(x, ref, index=()):
    for c in range(ROW_TILE):
        ref[index + (pl.ds(c, x.shape[0], stride=ROW_TILE), slice(None))] = x[:, c * LANES:(c + 1) * LANES]


def _from_tiles(ref, n_rows, index=()):
    return jnp.concatenate([ref[index + (pl.ds(c, n_rows, stride=ROW_TILE), slice(None))]
                            for c in range(ROW_TILE)], axis=1)


def _dispatch_kernel(fill_from_ref, pad_end_ref, tab_ref, h_ref, lpos_ref, xs_ref,
                     stage_ref, zero_ref, pend_ref, sems, zsem, *, grid_steps):
    i = pl.program_id(0)
    n_steps = pl.num_programs(0)
    n_slots = stage_ref.shape[0]
    slot = i % n_slots
    n_local = stage_ref.shape[1] // ROW_TILE
    fill_rows = zero_ref.shape[0]

    def fill(row):
        return pltpu.make_async_copy(zero_ref.at[pl.ds(0, fill_rows)],
                                     xs_ref.at[pl.ds(pl.multiple_of(row, fill_rows), fill_rows)], zsem)

    def pad_fill(e):
        n = pl.multiple_of(pad_end_ref[e] - fill_from_ref[e], ROW_TILE)
        return n, pltpu.make_async_copy(zero_ref.at[pl.ds(0, n)],
                                        xs_ref.at[pl.ds(pl.multiple_of(fill_from_ref[e], ROW_TILE), n)], zsem)

    def drain(s):
        n = pl.multiple_of(pend_ref[s], ROW_TILE)

        @pl.when(n > 0)
        def _():
            pltpu.make_async_copy(stage_ref.at[s, pl.ds(0, n)], xs_ref.at[pl.ds(0, n)], sems.at[s]).wait()
        pend_ref[s] = 0

    @pl.when(i == 0)
    def _():
        zero_ref[...] = jnp.zeros_like(zero_ref)
        for s in range(n_slots):
            pend_ref[s] = 0

    for j in range(-(-N_EXPERTS // grid_steps)):
        e = i * -(-N_EXPERTS // grid_steps) + j

        @pl.when(e < N_EXPERTS)
        def _(e=e):
            n, copy = pad_fill(e)

            @pl.when(n > 0)
            def _():
                copy.start()

    row = lax.broadcasted_iota(jnp.int32, (n_local, h_ref.shape[0]), 0)
    place = row == lpos_ref[0, 0:1, :]
    for k in range(1, TOP_K):
        place = place | (row == lpos_ref[0, k:k + 1, :])
    rows = _dot(place.astype(jnp.bfloat16), h_ref[...])

    drain(slot)
    _to_tiles(rows, stage_ref, (slot,))

    total = 0
    for e in range(N_EXPERTS):
        loc, glob, n = _piece_table(tab_ref, e)

        @pl.when(n > 0)
        def _(loc=loc, glob=glob, n=n, e=e):
            pltpu.make_async_copy(stage_ref.at[slot, pl.ds(loc, n)], xs_ref.at[pl.ds(glob, n)],
                                  sems.at[slot]).start(priority=e % 2)
        total = total + n
    pend_ref[slot] = total

    tail0 = pad_end_ref[N_EXPERTS - 1]
    n_tail = (xs_ref.shape[0] - tail0) // fill_rows
    per_step = (n_tail + n_steps - 1) // n_steps

    def start_tail(j, c):
        t = i * per_step + j

        @pl.when(t < n_tail)
        def _():
            fill(tail0 + t * fill_rows).start()
        return c
    lax.fori_loop(0, per_step, start_tail, 0)

    @pl.when(i == n_steps - 1)
    def _():
        for s in range(n_slots):
            drain(s)

        def wait_tail(j, c):
            fill(0).wait()
            return c
        lax.fori_loop(0, n_tail, wait_tail, 0)

        def wait_pad(e, c):
            n, copy = pad_fill(e)

            @pl.when(n > 0)
            def _():
                copy.wait()
            return c
        lax.fori_loop(0, N_EXPERTS, wait_pad, 0)


def _dispatch(h2, tab_flat, lpos_t, fill_from, pad_end, *, n_rows):
    t, d = h2.shape
    tile = DISPATCH_TILE
    assert d == ROW_TILE * LANES
    n_local = tile * TOP_K
    grid_spec = pltpu.PrefetchScalarGridSpec(
        num_scalar_prefetch=2,
        grid=(t // tile,),
        in_specs=[pl.BlockSpec((TAB_WIDTH,), lambda i, *_: (i,), memory_space=pltpu.SMEM),
                  pl.BlockSpec((tile, d), lambda i, *_: (i, 0)),
                  pl.BlockSpec((1, TOP_K, tile), lambda i, *_: (i, 0, 0))],
        out_specs=pl.BlockSpec(memory_space=pl.ANY),
        scratch_shapes=[pltpu.VMEM((STAGE_SLOTS, n_local * ROW_TILE, LANES), jnp.float32),
                        pltpu.VMEM((FFN_STEP * ROW_TILE, LANES), jnp.float32),
                        pltpu.SMEM((STAGE_SLOTS,), jnp.int32),
                        pltpu.SemaphoreType.DMA((STAGE_SLOTS,)), pltpu.SemaphoreType.DMA(())],
    )
    return pl.pallas_call(
        functools.partial(_dispatch_kernel, grid_steps=t // tile),
        grid_spec=grid_spec,
        out_shape=jax.ShapeDtypeStruct((n_rows * ROW_TILE, LANES), jnp.float32),
        compiler_params=_params("arbitrary"),
        name="dispatch",
    )(fill_from, pad_end, tab_flat, h2, lpos_t)


def _ffn_kernel(te_ref, rows_ref, n_used_ref, x_ref, w1_ref, b1_ref, w2_ref, b2_ref, y_ref,
                w1p_ref, w2b_ref, act_ref):
    i = pl.program_id(0)
    live = i < n_used_ref[0]
    n_pair = w1_ref.shape[2] // FFN_CHUNK
    half = FFN_CHUNK // 2
    n_sub = jnp.where(live, (rows_ref[i] + FFN_TILE - 1) // FFN_TILE, 0)

    @pl.when(live & ((i == 0) | (te_ref[i] != te_ref[jnp.maximum(i - 1, 0)])))
    def _():
        src = lax.broadcasted_iota(jnp.int32, (FFN_CHUNK, FFN_CHUNK), 0)
        dst = lax.broadcasted_iota(jnp.int32, (FFN_CHUNK, FFN_CHUNK), 1)
        unzip = (src == jnp.where(dst < half, 2 * dst, 2 * (dst - half) + 1)).astype(jnp.bfloat16)
        for c in range(n_pair):
            cs = slice(c * FFN_CHUNK, (c + 1) * FFN_CHUNK)
            w1p_ref[:, cs] = _dot(w1_ref[0, :, cs].astype(jnp.bfloat16), unzip).astype(jnp.bfloat16)
        w2b_ref[...] = w2_ref[0].astype(jnp.bfloat16)

    def compute(n_rows):
        x = _from_tiles(x_ref, n_rows).astype(jnp.bfloat16)
        for c in range(0, n_pair, FFN_GROUP):
            cs = slice(c * FFN_CHUNK, (c + FFN_GROUP) * FFN_CHUNK)
            h = _dot(x, w1p_ref[:, cs]) + b1_ref[0, :, cs]
            g = jnp.concatenate([h[:, k * FFN_CHUNK:k * FFN_CHUNK + half] for k in range(FFN_GROUP)], axis=1)
            u = jnp.concatenate([h[:, k * FFN_CHUNK + half:(k + 1) * FFN_CHUNK] for k in range(FFN_GROUP)], axis=1)
            g = jnp.minimum(g, SWIGLU_LIMIT)
            u = jnp.clip(u, -SWIGLU_LIMIT, SWIGLU_LIMIT)
            act = g * jax.nn.sigmoid(SWIGLU_ALPHA * g) * (u + 1.0)
            act_ref[:n_rows, c * half:(c + FFN_GROUP) * half] = act.astype(act_ref.dtype)
        _to_tiles(_dot(act_ref[:n_rows, :], w2b_ref[...]) + b2_ref[0], y_ref)
        if n_rows * ROW_TILE < y_ref.shape[0]:
            y_ref[n_rows * ROW_TILE:, :] = jnp.zeros((y_ref.shape[0] - n_rows * ROW_TILE, LANES), y_ref.dtype)

    for tiles in range(1, FFN_STEP // FFN_TILE + 1):
        pl.when(n_sub == tiles)(functools.partial(compute, tiles * FFN_TILE))

    @pl.when(n_sub == 0)
    def _():
        y_ref[...] = jnp.zeros_like(y_ref)


def _unzip_bias(b1):
    e, f2 = b1.shape
    half = FFN_CHUNK // 2
    return b1.reshape(e, f2 // FFN_CHUNK, half, 2).transpose(0, 1, 3, 2).reshape(e, 1, f2)


def _ffn(xs, step_expert, step_rows, n_used, w1, b1, w2, b2):
    d, f2 = w1.shape[1:]
    n_steps = xs.shape[0] // (FFN_STEP * ROW_TILE)
    live = lambda i, nu: jnp.minimum(i, nu[0] - 1)
    wspec = lambda shape: pl.BlockSpec((1,) + shape, lambda i, te, nr, nu: (te[live(i, nu)], 0, 0))
    grid_spec = pltpu.PrefetchScalarGridSpec(
        num_scalar_prefetch=3,
        grid=(n_steps,),
        in_specs=[pl.BlockSpec((FFN_STEP * ROW_TILE, LANES), lambda i, te, nr, nu: (live(i, nu), 0)),
                  wspec((d, f2)), wspec((1, f2)), wspec((f2 // 2, d)), wspec((1, d))],
        out_specs=pl.BlockSpec((FFN_STEP * ROW_TILE, LANES), lambda i, te, nr, nu: (i, 0)),
        scratch_shapes=[pltpu.VMEM((d, f2), jnp.bfloat16), pltpu.VMEM((f2 // 2, d), jnp.bfloat16),
                        pltpu.VMEM((FFN_STEP, f2 // 2), jnp.bfloat16)],
    )
    return pl.pallas_call(
        _ffn_kernel,
        grid_spec=grid_spec,
        out_shape=jax.ShapeDtypeStruct(xs.shape, jnp.float32),
        compiler_params=_params("arbitrary"),
        name="expert_ffn",
    )(step_expert, step_rows, n_used, xs, w1, _unzip_bias(b1), w2, b2[:, None, :])


def _combine_kernel(*refs):
    tab_refs = refs[:LAND_SLOTS]
    lpos_ref, wts_ref, x1_ref, ga_ref, g_ref, y_ref, o_ref, buf_ref, sems = refs[LAND_SLOTS:]
    tab_ref = tab_refs[0]
    i = pl.program_id(0)
    n_steps = pl.num_programs(0)
    n_slots = buf_ref.shape[0]
    slot = i % n_slots
    tile = x1_ref.shape[0]
    n_local = buf_ref.shape[1] // ROW_TILE

    def fetch(tab, s):
        for e in range(N_EXPERTS):
            loc, glob, n = _piece_table(tab, e)

            @pl.when(n > 0)
            def _(loc=loc, glob=glob, n=n, e=e):
                pltpu.make_async_copy(y_ref.at[pl.ds(glob, n)], buf_ref.at[s, pl.ds(loc, n)],
                                      sems.at[s]).start(priority=e % 2)

    @pl.when(i == 0)
    def _():
        for s in range(n_slots - 1):
            pl.when(s < n_steps)(functools.partial(fetch, tab_refs[s], s))

    @pl.when(i + n_slots - 1 < n_steps)
    def _():
        fetch(tab_refs[n_slots - 1], (i + n_slots - 1) % n_slots)

    n = pl.multiple_of(tab_ref[3 * N_EXPERTS], ROW_TILE)

    @pl.when(n > 0)
    def _():
        pltpu.make_async_copy(y_ref.at[pl.ds(0, n)], buf_ref.at[slot, pl.ds(0, n)], sems.at[slot]).wait()

    rows = _from_tiles(buf_ref, n_local, (slot,)).astype(jnp.bfloat16)
    part = tile // OUT_PARTS
    col = lax.broadcasted_iota(jnp.int32, (part, n_local), 1)
    for p in range(OUT_PARTS):
        r = slice(p * part, (p + 1) * part)
        pos = lpos_ref[r, :]
        w = wts_ref[r, :]
        pick = jnp.where(col == pos[:, 0:1], w[:, 0:1], 0.0)
        for k in range(1, TOP_K):
            pick = pick + jnp.where(col == pos[:, k:k + 1], w[:, k:k + 1], 0.0)
        hi = pick.astype(jnp.bfloat16)
        lo = (pick - hi.astype(jnp.float32)).astype(jnp.bfloat16)
        y = _dot(hi, rows) + _dot(lo, rows)
        o_ref[r, :] = x1_ref[r, :] + ga_ref[0] * _rms(y, g_ref[...])


def _combine(ys, tab_flat, lpos, wts, x1, mod3, g_post, *, seq):
    t, d = x1.shape
    tile = DISPATCH_TILE
    n_steps = t // tile
    per_seq = seq // tile
    n_local = tile * TOP_K
    return pl.pallas_call(
        _combine_kernel,
        grid=(n_steps,),
        in_specs=[pl.BlockSpec((TAB_WIDTH,), lambda i, s=s: (jnp.minimum(i + s, n_steps - 1),),
                               memory_space=pltpu.SMEM) for s in range(LAND_SLOTS)] + [
                  pl.BlockSpec((tile, TOP_K), lambda i: (i, 0)),
                  pl.BlockSpec((tile, TOP_K), lambda i: (i, 0)),
                  pl.BlockSpec((tile, d), lambda i: (i, 0)),
                  pl.BlockSpec((1, 1, d), lambda i: (i // per_seq, 0, 5)),
                  pl.BlockSpec((1, d), lambda i: (0, 0)),
                  pl.BlockSpec(memory_space=pl.ANY)],
        out_specs=pl.BlockSpec((tile, d), lambda i: (i, 0)),
        out_shape=jax.ShapeDtypeStruct((t, d), jnp.float32),
        scratch_shapes=[pltpu.VMEM((LAND_SLOTS, n_local * ROW_TILE, LANES), ys.dtype),
                        pltpu.SemaphoreType.DMA((LAND_SLOTS,))],
        compiler_params=_params("arbitrary"),
        name="combine",
    )(*([tab_flat] * LAND_SLOTS), lpos, wts, x1, mod3, g_post, ys)


def _rope_tables(seq):
    pos = np.arange(seq)
    n_freq = HEAD_DIM // 4
    freqs = ROPE_BASE ** (-jnp.arange(n_freq, dtype=jnp.float32) / n_freq)
    rows = jnp.asarray(pos // GRID_W, jnp.float32)[:, None] * freqs[None, :]
    cols = jnp.asarray(pos % GRID_W, jnp.float32)[:, None] * freqs[None, :]
    ang = jnp.concatenate([rows, rows, cols, cols], axis=1)
    sign = np.tile(np.repeat([-1.0, 1.0], n_freq), 2).astype(np.float32)
    cos = jnp.cos(ang)
    sin = jnp.sin(ang) * sign[None, :]
    reps = LANES // HEAD_DIM
    return jnp.tile(cos, (1, reps)), jnp.tile(sin, (1, reps))


def kernel(x, c, ctx, c_ctx, w_ada, b_ada, g_pre_mix, g_post_mix, g_pre_ffn, g_post_ffn, w_in, g_grp_a, g_grp_b,
           sink_a, rpb_b, w_out, w_router, b_router, w_mlp1, b_mlp1, w_mlp2, b_mlp2):
    batch, seq, d = x.shape
    n_ctx = ctx.shape[1]
    assert w_ada.shape[0] == 1, "single layer"
    assert seq % (ATT_BLOCKS * GRID_W * NB_Q_ROWS) == 0 and seq // GRID_W >= NB_K_ROWS
    assert seq % (ATT_BLOCKS * BLOCK_A) == 0 and seq >= BLOCK_A + 2 * WINDOW
    n_tok = batch * seq
    bf16 = jnp.bfloat16

    def pair_heads(a, axis):
        shape = a.shape
        a = a.reshape(shape[:axis] + (N_KV_A, GQA_GROUP, HEAD_DIM) + shape[axis + 1:])
        return jnp.swapaxes(a, axis, axis + 1).reshape(shape)

    mod_rows = -(-(batch + 1) // 8) * 8
    cc = jnp.zeros((mod_rows, d), jnp.float32).at[:batch].set(c).at[batch].set(c_ctx)
    mod3 = _ada(cc, w_ada[0], b_ada[0]).reshape(mod_rows, 1, 6 * d)

    w_in0 = w_in[0]
    w_lat = jnp.concatenate([pair_heads(w_in0[:, :WIDTH_A], 1), w_in0[:, WIDTH_A:]], axis=1).astype(bf16)
    kv_a_end = WIDTH_A + 2 * WIDTH_KV_A
    w_ctx = jnp.concatenate([w_in0[:, WIDTH_A:kv_a_end], w_in0[:, kv_a_end + WIDTH_B:]], axis=1).astype(bf16)
    g_pre = g_pre_mix[0].reshape(1, d)
    qa, ka, va, qb, kb, vb = _inproj(x.reshape(n_tok, d), mod3, g_pre, w_lat, _rope_tables(seq),
                                     seq=seq, mod_row0=0, latent=True)
    kac, vac, kbc, vbc = _inproj(ctx.reshape(batch * n_ctx, d), mod3, g_pre, w_ctx, None,
                                 seq=n_ctx, mod_row0=batch, latent=False)

    oa = _window_attention(qa, ka.reshape(batch, seq, -1), va.reshape(batch, seq, -1),
                           kac.reshape(batch, n_ctx, -1), vac.reshape(batch, n_ctx, -1),
                           sink_a[0].astype(jnp.float32) * LOG2E, batch=batch, seq=seq)
    tabs, cls_of_block = _nb_bias_tables(rpb_b[0], seq // GRID_W)
    ob = _neighbourhood_attention(qb, kb.reshape(batch, seq, -1), vb.reshape(batch, seq, -1),
                                  kbc.reshape(batch, n_ctx, -1), vbc.reshape(batch, n_ctx, -1),
                                  tabs, cls_of_block, batch=batch, seq=seq)

    w_out0 = w_out[0]
    w_o = jnp.concatenate([pair_heads(w_out0[:WIDTH_A], 0), w_out0[WIDTH_A:]], axis=0).astype(bf16)
    x1, h2, idx, wts, rank, size, base, counts = _outproj(
        oa, ob, x.reshape(n_tok, d), mod3,
        pair_heads(g_grp_a[0], 0).reshape(1, -1), g_grp_b[0].reshape(1, -1),
        g_post_mix[0].reshape(1, d), g_pre_ffn[0].reshape(1, d),
        w_o, w_router[0], b_router[0].reshape(1, -1), seq=seq)

    n_tok_tiles = n_tok // DISPATCH_TILE
    size = size.reshape(n_tok_tiles, N_EXPERTS)
    counts = counts.reshape(-1)
    padded = (counts + FFN_STEP - 1) // FFN_STEP * FFN_STEP
    pad_end = jnp.cumsum(padded).astype(jnp.int32)
    pad_start = pad_end - padded
    n_steps = n_tok * TOP_K // FFN_STEP + N_EXPERTS
    n_used = (pad_end[-1:] // FFN_STEP).astype(jnp.int32)
    step_row0 = jnp.arange(n_steps, dtype=jnp.int32) * FFN_STEP
    step_expert = jnp.minimum(jnp.sum(step_row0[:, None] >= pad_end[None, :], axis=1),
                              N_EXPERTS - 1).astype(jnp.int32)
    own = step_expert[:, None] == jnp.arange(N_EXPERTS, dtype=jnp.int32)
    step_rows = jnp.clip(jnp.sum(jnp.where(own, (pad_start + counts)[None, :], 0), axis=1) - step_row0,
                         0, FFN_STEP).astype(jnp.int32)
    fill_from = (pad_start + counts).astype(jnp.int32)
    local0 = jnp.cumsum(size, axis=1) - size
    global0 = pad_start[None, :] + base.reshape(n_tok_tiles, N_EXPERTS)
    tab = jnp.concatenate([local0, global0, size, jnp.sum(size, axis=1, keepdims=True),
                           jnp.zeros((n_tok_tiles, TAB_WIDTH - 3 * N_EXPERTS - 1), jnp.int32)], axis=1)
    tab = (tab * ROW_TILE).reshape(-1).astype(jnp.int32)
    hot = idx[:, :, None, :] == jnp.arange(N_EXPERTS, dtype=jnp.int32)[None, None, :, None]
    lpos_t = rank + jnp.sum(jnp.where(hot, local0[:, None, :, None], 0), axis=2)
    by_token = lambda a: jnp.swapaxes(a, 1, 2).reshape(n_tok, TOP_K)

    xs = _dispatch(h2, tab, lpos_t, fill_from * ROW_TILE, pad_end * ROW_TILE, n_rows=n_steps * FFN_STEP)
    ys = _ffn(xs, step_expert, step_rows, n_used, w_mlp1[0], b_mlp1[0], w_mlp2[0], b_mlp2[0])
    out = _combine(ys, tab, by_token(lpos_t), by_token(wts), x1, mod3, g_post_ffn[0].reshape(1, d), seq=seq)
    return out.reshape(batch, seq, d)
```

```python
import functools

import numpy as np
import jax
import jax.numpy as jnp
from jax import lax
from jax.experimental import pallas as pl
from jax.experimental.pallas import tpu as pltpu

GRID_W = 64
HEAD_DIM = 64
N_HEADS_A = 8
N_KV_A = 2
GQA_GROUP = N_HEADS_A // N_KV_A
N_HEADS_B = 8
WIDTH_A = N_HEADS_A * HEAD_DIM
WIDTH_KV_A = N_KV_A * HEAD_DIM
WIDTH_B = N_HEADS_B * HEAD_DIM
WINDOW = 128
BLOCK_A = 128
NA_ROWS = 8
NA_COLS = 16
N_EXPERTS = 32
TOP_K = 4
SWIGLU_LIMIT = 7.0
SWIGLU_ALPHA = 1.702
ROPE_BASE = 10000.0
EPS = 1e-6
NEG_INF = -1e30

LANES = 128
VMEM_LIMIT = 56 * 1024 * 1024

LOG2E = 1.4426950408889634
ATT_BLOCKS = 8
NB_Q_ROWS = 2
NB_K_ROWS = NB_Q_ROWS + NA_ROWS
PROJ_TILE = 1024
OUT_TILE = 256
OUT_PARTS = 2
STAGE_SLOTS = 4
LAND_SLOTS = 4
FFN_TILE = 256
FFN_STEP = 4 * FFN_TILE
FFN_CHUNK = 256
FFN_GROUP = 4
DISPATCH_TILE = OUT_TILE
ROW_TILE = 8
TAB_WIDTH = 128


def _params(*sem):
    return pltpu.CompilerParams(dimension_semantics=sem, vmem_limit_bytes=VMEM_LIMIT)


def _rms(x, g):
    return x * lax.rsqrt(jnp.mean(x * x, axis=-1, keepdims=True) + EPS) * g


def _dot(a, b):
    return jnp.dot(a, b, preferred_element_type=jnp.float32)


def _dot_nt(a, b):
    return lax.dot_general(a, b, (((1,), (1,)), ((), ())), preferred_element_type=jnp.float32)


def _ada_kernel(c_ref, w_ref, b_ref, o_ref):
    c = c_ref[...]
    s = (c * jax.nn.sigmoid(c)).astype(jnp.bfloat16)
    o_ref[...] = _dot(s, w_ref[...].astype(jnp.bfloat16)) + b_ref[...]


def _ada(cc, w_ada, b_ada):
    rows, d = cc.shape
    n_out = w_ada.shape[1]
    return pl.pallas_call(
        _ada_kernel,
        grid=(n_out // d,),
        in_specs=[pl.BlockSpec((rows, d), lambda j: (0, 0)),
                  pl.BlockSpec((d, d), lambda j: (0, j)),
                  pl.BlockSpec((1, d), lambda j: (0, j))],
        out_specs=pl.BlockSpec((rows, d), lambda j: (0, j)),
        out_shape=jax.ShapeDtypeStruct((rows, n_out), jnp.float32),
        compiler_params=_params("arbitrary"),
        name="ada",
    )(cc, w_ada, b_ada.reshape(1, n_out))


def _rope(x, cos, sin):
    w = x.shape[1]
    reps = w // LANES
    if reps > 1:
        cos = jnp.concatenate([cos] * reps, axis=1)
        sin = jnp.concatenate([sin] * reps, axis=1)
    lane = lax.broadcasted_iota(jnp.int32, x.shape, 1)
    quarter = HEAD_DIM // 4
    partner = jnp.where(lane % (2 * quarter) < quarter,
                        pltpu.roll(x, w - quarter, 1), pltpu.roll(x, quarter, 1))
    return x * cos + partner * sin


def _inproj_kernel(x_ref, sh_ref, sc_ref, g_ref, w_ref, *rest, latent):
    x = x_ref[...]
    h = _rms(x, g_ref[...]) * (1.0 + sc_ref[0]) + sh_ref[0]
    p = _dot(h.astype(jnp.bfloat16), w_ref[...])
    if latent:
        cos_ref, sin_ref, qa_ref, ka_ref, va_ref, qb_ref, kb_ref, vb_ref = rest
        cos, sin = cos_ref[...], sin_ref[...]
        scale = HEAD_DIM ** -0.5 * LOG2E
        o = 0
        qa_ref[...] = (_rope(p[:, o:o + WIDTH_A], cos, sin) * scale).astype(qa_ref.dtype)
        o += WIDTH_A
        ka_ref[...] = _rope(p[:, o:o + WIDTH_KV_A], cos, sin).astype(ka_ref.dtype)
        o += WIDTH_KV_A
        va_ref[...] = p[:, o:o + WIDTH_KV_A].astype(va_ref.dtype)
        o += WIDTH_KV_A
        qb_ref[...] = (p[:, o:o + WIDTH_B] * scale).astype(qb_ref.dtype)
        o += WIDTH_B
    else:
        ka_ref, va_ref, kb_ref, vb_ref = rest
        o = 0
        ka_ref[...] = p[:, o:o + WIDTH_KV_A].astype(ka_ref.dtype)
        o += WIDTH_KV_A
        va_ref[...] = p[:, o:o + WIDTH_KV_A].astype(va_ref.dtype)
        o += WIDTH_KV_A
    kb_ref[...] = p[:, o:o + WIDTH_B].astype(kb_ref.dtype)
    o += WIDTH_B
    vb_ref[...] = p[:, o:o + WIDTH_B].astype(vb_ref.dtype)


def _inproj(x2d, mod3, g_pre, w, rope_tabs, *, seq, mod_row0, latent):
    t, d = x2d.shape
    tile = min(PROJ_TILE, seq)
    per_seq = seq // tile
    if latent:
        mod_row = lambda i: i // per_seq
    else:
        mod_row = lambda i: mod_row0
    in_specs = [pl.BlockSpec((tile, d), lambda i: (i, 0)),
                pl.BlockSpec((1, 1, d), lambda i: (mod_row(i), 0, 0)),
                pl.BlockSpec((1, 1, d), lambda i: (mod_row(i), 0, 1)),
                pl.BlockSpec((1, d), lambda i: (0, 0)),
                pl.BlockSpec(w.shape, lambda i: (0, 0))]
    args = [x2d, mod3, mod3, g_pre, w]
    widths = [WIDTH_KV_A, WIDTH_KV_A, WIDTH_B, WIDTH_B]
    if latent:
        in_specs += [pl.BlockSpec((tile, LANES), lambda i: (i % per_seq, 0))] * 2
        args += list(rope_tabs)
        widths = [WIDTH_A, WIDTH_KV_A, WIDTH_KV_A, WIDTH_B, WIDTH_B, WIDTH_B]
    return pl.pallas_call(
        functools.partial(_inproj_kernel, latent=latent),
        grid=(t // tile,),
        in_specs=in_specs,
        out_specs=[pl.BlockSpec((tile, wd), lambda i: (i, 0)) for wd in widths],
        out_shape=[jax.ShapeDtypeStruct((t, wd), jnp.bfloat16) for wd in widths],
        compiler_params=_params("parallel"),
        name="inproj_latent" if latent else "inproj_ctx",
    )(*args)


def _ones_beside(v):
    return jnp.concatenate([v, jnp.ones_like(v)], axis=1)


def _group_attention(q_ref, o_ref, rows, groups, s_ref, p_ref, m_ref):
    tq = rows.stop - rows.start
    low = lax.broadcasted_iota(jnp.int32, (tq, LANES), 1) < HEAD_DIM
    layout = []
    base = 0
    for pairs, k_loc, k_ctx, _, _, _, _ in groups:
        heads = [(j, 0) for j in pairs] + [(j, 1) for j in pairs]
        n_loc = k_loc.shape[0]
        stacked = []
        for j, half in heads:
            q = q_ref[rows, j * LANES:(j + 1) * LANES]
            stacked.append(jnp.where(low if half == 0 else ~low, q, jnp.zeros_like(q)))
        qs = jnp.concatenate(stacked, axis=0)
        n = len(heads) * tq
        s_ref[base:base + n, :] = _dot_nt(qs, jnp.concatenate([k_loc, k_ctx], axis=0))
        layout.append((base, heads, n_loc))
        base += n
    for (base, heads, n_loc), group in zip(layout, groups):
        bias_of, sink_of = group[5], group[6]
        for g, (j, half) in enumerate(heads):
            r = slice(base + g * tq, base + (g + 1) * tq)
            s_loc = s_ref[r, :n_loc] + bias_of(j, half)
            s_ctx = s_ref[r, n_loc:]
            m = jnp.maximum(jnp.max(s_loc, axis=1, keepdims=True), jnp.max(s_ctx, axis=1, keepdims=True))
            sink = sink_of(j, half)
            if sink is not None:
                m = jnp.maximum(m, sink)
            p_ref[r, :n_loc] = jnp.exp2((s_loc - m).astype(jnp.bfloat16))
            p_ref[r, n_loc:] = jnp.exp2((s_ctx - m).astype(jnp.bfloat16))
            m_ref[r, :] = m
    for (base, heads, n_loc), group in zip(layout, groups):
        pairs, _, _, v_loc, v_ctx, _, sink_of = group
        n = len(heads) * tq
        both = _dot(p_ref[base:base + n, :], _ones_beside(jnp.concatenate([v_loc, v_ctx], axis=0)))
        for a, j in enumerate(pairs):
            outs = []
            for half in range(2):
                g = half * len(pairs) + a
                r = slice(g * tq, (g + 1) * tq)
                den = both[r, LANES:]
                sink = sink_of(j, half)
                if sink is not None:
                    den = den + jnp.exp2(sink - m_ref[base + g * tq:base + (g + 1) * tq, :])
                outs.append(both[r, :LANES] / den)
            o_ref[rows, j * LANES:(j + 1) * LANES] = jnp.where(low, outs[0], outs[1]).astype(o_ref.dtype)


def _attention_scratch(tq, n_heads, n_keys):
    return [pltpu.VMEM((n_heads * tq, n_keys), jnp.float32), pltpu.VMEM((n_heads * tq, n_keys), jnp.bfloat16),
            pltpu.VMEM((n_heads * tq, 1), jnp.float32)]


def _win_kernel(sink_ref, q_ref, k_ref, v_ref, kc_ref, vc_ref, o_ref, *scratch, seq):
    span = BLOCK_A + 2 * WINDOW
    pairs = list(range(GQA_GROUP))
    for sb in range(ATT_BLOCKS):
        i = pl.program_id(1) * ATT_BLOCKS + sb
        start = pl.multiple_of(jnp.clip(i * BLOCK_A - WINDOW, 0, seq - span), BLOCK_A)
        qpos = i * BLOCK_A + lax.broadcasted_iota(jnp.int32, (BLOCK_A, span), 0)
        kpos = start + lax.broadcasted_iota(jnp.int32, (BLOCK_A, span), 1)
        bias = jnp.where(jnp.abs(kpos - qpos) <= WINDOW, 0.0, NEG_INF).astype(jnp.float32)
        groups = [([j], k_ref[0, pl.ds(start, span), :], kc_ref[0], v_ref[0, pl.ds(start, span), :], vc_ref[0],
                   lambda j, half: bias, lambda j, half: sink_ref[half * GQA_GROUP + j]) for j in pairs]
        _group_attention(q_ref, o_ref, slice(sb * BLOCK_A, (sb + 1) * BLOCK_A), groups, *scratch)


def _window_attention(qa, ka, va, kac, vac, sink, *, batch, seq):
    tq = ATT_BLOCKS * BLOCK_A
    nb = seq // tq
    n_ctx = kac.shape[1]
    return pl.pallas_call(
        functools.partial(_win_kernel, seq=seq),
        grid=(batch, nb),
        in_specs=[pl.BlockSpec(memory_space=pltpu.SMEM),
                  pl.BlockSpec((tq, WIDTH_A), lambda b, i: (b * nb + i, 0)),
                  pl.BlockSpec((1, seq, WIDTH_KV_A), lambda b, i: (b, 0, 0)),
                  pl.BlockSpec((1, seq, WIDTH_KV_A), lambda b, i: (b, 0, 0)),
                  pl.BlockSpec((1, n_ctx, WIDTH_KV_A), lambda b, i: (b, 0, 0)),
                  pl.BlockSpec((1, n_ctx, WIDTH_KV_A), lambda b, i: (b, 0, 0))],
        out_specs=pl.BlockSpec((tq, WIDTH_A), lambda b, i: (b * nb + i, 0)),
        out_shape=jax.ShapeDtypeStruct((batch * seq, WIDTH_A), jnp.bfloat16),
        scratch_shapes=_attention_scratch(BLOCK_A, N_HEADS_A, BLOCK_A + 2 * WINDOW + n_ctx),
        compiler_params=_params("parallel", "arbitrary"),
        name="window_attention",
    )(sink, qa, ka, va, kac, vac)


def _nb_kernel(q_ref, k_ref, v_ref, kc_ref, vc_ref, *rest, rows_n):
    tab_refs, o_ref, scratch = rest[:ATT_BLOCKS], rest[ATT_BLOCKS], rest[ATT_BLOCKS + 1:]
    n_keys = NB_K_ROWS * GRID_W
    tq = NB_Q_ROWS * GRID_W
    for sb in range(ATT_BLOCKS):
        m = pl.program_id(1) * ATT_BLOCKS + sb
        start_row = jnp.clip(NB_Q_ROWS * m - NA_ROWS // 2, 0, rows_n - NB_K_ROWS)
        start = pl.multiple_of(start_row * GRID_W, LANES)
        tab_ref = tab_refs[sb]
        groups = []
        for j in range(q_ref.shape[1] // LANES):
            cols = slice(j * LANES, (j + 1) * LANES)
            groups.append(([j], k_ref[0, pl.ds(start, n_keys), cols], kc_ref[0, :, cols],
                           v_ref[0, pl.ds(start, n_keys), cols], vc_ref[0, :, cols],
                           lambda j, half: tab_ref[0, 2 * j + half].astype(jnp.float32), lambda j, half: None))
        _group_attention(q_ref, o_ref, slice(sb * tq, (sb + 1) * tq), groups, *scratch)


def _nb_classes(rows_n):
    n_blocks = rows_n // NB_Q_ROWS
    sig = {}
    cls_of_block = []
    reps = []
    for m in range(n_blocks):
        start_row = int(np.clip(NB_Q_ROWS * m - NA_ROWS // 2, 0, rows_n - NB_K_ROWS))
        key = tuple((start_row - r, int(np.clip(r - NA_ROWS // 2, 0, rows_n - NA_ROWS)) - r)
                    for r in range(NB_Q_ROWS * m, NB_Q_ROWS * (m + 1)))
        if key not in sig:
            sig[key] = len(reps)
            reps.append(m)
        cls_of_block.append(sig[key])
    return np.asarray(cls_of_block, np.int32), reps


def _nb_bias_tables(rpb, rows_n):
    cls_of_block, reps = _nb_classes(rows_n)
    n_heads = rpb.shape[0]
    cq = np.arange(GRID_W)[:, None]
    ck = np.arange(GRID_W)[None, :]
    cs = np.clip(cq - NA_COLS // 2, 0, GRID_W - NA_COLS)
    col_ok = (ck >= cs) & (ck < cs + NA_COLS)
    pick = ((ck - cq + NA_COLS - 1)[None] == np.arange(2 * NA_COLS - 1)[:, None, None]) & col_ok[None]
    tiles = jnp.einsum('hrd,dqk->hrqk', rpb.astype(jnp.float32), jnp.asarray(pick, jnp.float32),
                       precision=lax.Precision.HIGHEST)
    tiles = jnp.where(jnp.asarray(col_ok)[None, None], tiles * LOG2E, NEG_INF).astype(jnp.bfloat16)
    blocked = jnp.full((n_heads, GRID_W, GRID_W), NEG_INF, jnp.bfloat16)
    tabs = []
    for m in reps:
        start_row = int(np.clip(NB_Q_ROWS * m - NA_ROWS // 2, 0, rows_n - NB_K_ROWS))
        q_rows = []
        for r in range(NB_Q_ROWS * m, NB_Q_ROWS * (m + 1)):
            rs = int(np.clip(r - NA_ROWS // 2, 0, rows_n - NA_ROWS))
            q_rows.append(jnp.concatenate(
                [tiles[:, krow - r + NA_ROWS - 1] if rs <= krow < rs + NA_ROWS else blocked
                 for krow in range(start_row, start_row + NB_K_ROWS)], axis=2))
        tabs.append(jnp.concatenate(q_rows, axis=1))
    return jnp.stack(tabs), cls_of_block


def _neighbourhood_attention(qb, kb, vb, kbc, vbc, tabs, cls_of_block, *, batch, seq):
    rows_n = seq // GRID_W
    tq = ATT_BLOCKS * NB_Q_ROWS * GRID_W
    nstep = seq // tq
    n_ctx = kbc.shape[1]

    def tab_spec(sb):
        return pl.BlockSpec((1,) + tabs.shape[1:], lambda b, m, c: (c[m * ATT_BLOCKS + sb], 0, 0, 0))

    grid_spec = pltpu.PrefetchScalarGridSpec(
        num_scalar_prefetch=1,
        grid=(batch, nstep),
        in_specs=[pl.BlockSpec((tq, WIDTH_B), lambda b, m, c: (b * nstep + m, 0)),
                  pl.BlockSpec((1, seq, WIDTH_B), lambda b, m, c: (b, 0, 0)),
                  pl.BlockSpec((1, seq, WIDTH_B), lambda b, m, c: (b, 0, 0)),
                  pl.BlockSpec((1, n_ctx, WIDTH_B), lambda b, m, c: (b, 0, 0)),
                  pl.BlockSpec((1, n_ctx, WIDTH_B), lambda b, m, c: (b, 0, 0))]
                 + [tab_spec(sb) for sb in range(ATT_BLOCKS)],
        out_specs=pl.BlockSpec((tq, WIDTH_B), lambda b, m, c: (b * nstep + m, 0)),
        scratch_shapes=_attention_scratch(NB_Q_ROWS * GRID_W, N_HEADS_B, NB_K_ROWS * GRID_W + n_ctx),
    )

    def body(c_ref, *refs):
        _nb_kernel(*refs, rows_n=rows_n)

    return pl.pallas_call(
        body,
        grid_spec=grid_spec,
        out_shape=jax.ShapeDtypeStruct((batch * seq, WIDTH_B), jnp.bfloat16),
        compiler_params=_params("parallel", "arbitrary"),
        name="neighbourhood_attention",
    )(jnp.asarray(cls_of_block), qb, kb, vb, kbc, vbc, *([tabs] * ATT_BLOCKS))


def _outproj_kernel(oa_ref, ob_ref, x_ref, ga_ref, sh_ref, sc_ref, gga_ref, ggb_ref, gpost_ref, gpre_ref,
                    wo_ref, wr_hi_ref, wr_lo_ref, br_ref,
                    x1_ref, h2_ref, idx_ref, wts_ref, rank_ref, size_ref, base_ref, cnt_ref, run_ref):
    i = pl.program_id(0)

    @pl.when(i == 0)
    def _():
        run_ref[...] = jnp.zeros_like(run_ref)

    n_rows = x_ref.shape[0]
    parts = [slice(p * n_rows // OUT_PARTS, (p + 1) * n_rows // OUT_PARTS) for p in range(OUT_PARTS)]
    na = [_rms(oa_ref[r, :].astype(jnp.float32), gga_ref[...]).astype(jnp.bfloat16) for r in parts]
    nb = [_rms(ob_ref[r, :].astype(jnp.float32), ggb_ref[...]).astype(jnp.bfloat16) for r in parts]
    mix = [_dot(a, wo_ref[:WIDTH_A, :]) + _dot(b, wo_ref[WIDTH_A:, :]) for a, b in zip(na, nb)]
    x1 = [x_ref[r, :] + ga_ref[0] * _rms(m, gpost_ref[...]) for r, m in zip(parts, mix)]
    for r, v in zip(parts, x1):
        x1_ref[r, :] = v
    h2 = [_rms(v, gpre_ref[...]) * (1.0 + sc_ref[0]) + sh_ref[0] for v in x1]
    for r, v in zip(parts, h2):
        h2_ref[r, :] = v.astype(h2_ref.dtype)

    h_hi = [v.astype(jnp.bfloat16) for v in h2]
    h_lo = [(v - hi.astype(jnp.float32)).astype(jnp.bfloat16) for v, hi in zip(h2, h_hi)]
    logits = jnp.concatenate([_dot(hi, wr_hi_ref[...]) + _dot(lo, wr_hi_ref[...]) + _dot(hi, wr_lo_ref[...])
                              for hi, lo in zip(h_hi, h_lo)], axis=0)
    work = logits.T[:N_EXPERTS] + br_ref[...]
    e, t = work.shape
    row = lax.broadcasted_iota(jnp.int32, (e, t), 0)
    chosen = jnp.zeros((e, t), jnp.float32)
    vals, sels, hots = [], [], []
    for _k in range(TOP_K):
        mx = jnp.max(work, axis=0, keepdims=True)
        sel = jnp.min(jnp.where(work == mx, row, e), axis=0, keepdims=True)
        hot = row == sel
        vals.append(mx)
        sels.append(sel)
        hots.append(hot)
        work = jnp.where(hot, -jnp.inf, work)
        chosen = chosen + hot.astype(jnp.float32)
    ex = [jnp.exp(v - vals[0]) for v in vals]
    den = ex[0] + ex[1] + ex[2] + ex[3]

    earlier = (lax.broadcasted_iota(jnp.int32, (t, t), 0) < lax.broadcasted_iota(jnp.int32, (t, t), 1))
    prefix = _dot(chosen.astype(jnp.bfloat16), earlier.astype(jnp.bfloat16))
    ranks = [jnp.sum(jnp.where(h, prefix, 0.0), axis=0, keepdims=True) for h in hots]
    size = jnp.sum(chosen, axis=1, keepdims=True).astype(jnp.int32)
    size_ref[0] = size
    base_ref[0] = run_ref[...]
    run_ref[...] = run_ref[...] + size
    cnt_ref[...] = run_ref[...]

    idx_ref[0] = jnp.concatenate(sels, axis=0)
    wts_ref[0] = jnp.concatenate([v / den for v in ex], axis=0)
    rank_ref[0] = jnp.concatenate(ranks, axis=0).astype(jnp.int32)


def _outproj(oa, ob, x2d, mod3, gga, ggb, gpost, gpre, w_out, w_router, b_router, *, seq):
    t, d = x2d.shape
    tile = OUT_TILE
    per_seq = seq // tile
    row = lambda i: (i, 0)
    const = lambda i: (0, 0)
    modspec = lambda part: pl.BlockSpec((1, 1, d), lambda i: (i // per_seq, 0, part))
    per_tile = lambda shape: pl.BlockSpec((1,) + shape, lambda i: (i, 0, 0))
    n_steps = t // tile
    w_pad = jnp.zeros((d, LANES), jnp.float32).at[:, :N_EXPERTS].set(w_router)
    wr_hi = w_pad.astype(jnp.bfloat16)
    wr_lo = (w_pad - wr_hi.astype(jnp.float32)).astype(jnp.bfloat16)
    outs = pl.pallas_call(
        _outproj_kernel,
        grid=(n_steps,),
        in_specs=[pl.BlockSpec((tile, WIDTH_A), row), pl.BlockSpec((tile, WIDTH_B), row),
                  pl.BlockSpec((tile, d), row),
                  modspec(2), modspec(3), modspec(4),
                  pl.BlockSpec((1, WIDTH_A), const), pl.BlockSpec((1, WIDTH_B), const),
                  pl.BlockSpec((1, d), const), pl.BlockSpec((1, d), const),
                  pl.BlockSpec(w_out.shape, const), pl.BlockSpec((d, LANES), const), pl.BlockSpec((d, LANES), const),
                  pl.BlockSpec((N_EXPERTS, 1), const)],
        out_specs=[pl.BlockSpec((tile, d), row), pl.BlockSpec((tile, d), row),
                   per_tile((TOP_K, tile)), per_tile((TOP_K, tile)), per_tile((TOP_K, tile)),
                   per_tile((N_EXPERTS, 1)), per_tile((N_EXPERTS, 1)),
                   pl.BlockSpec((N_EXPERTS, 1), const)],
        out_shape=[jax.ShapeDtypeStruct((t, d), jnp.float32), jax.ShapeDtypeStruct((t, d), jnp.bfloat16),
                   jax.ShapeDtypeStruct((n_steps, TOP_K, tile), jnp.int32),
                   jax.ShapeDtypeStruct((n_steps, TOP_K, tile), jnp.float32),
                   jax.ShapeDtypeStruct((n_steps, TOP_K, tile), jnp.int32),
                   jax.ShapeDtypeStruct((n_steps, N_EXPERTS, 1), jnp.int32),
                   jax.ShapeDtypeStruct((n_steps, N_EXPERTS, 1), jnp.int32),
                   jax.ShapeDtypeStruct((N_EXPERTS, 1), jnp.int32)],
        scratch_shapes=[pltpu.VMEM((N_EXPERTS, 1), jnp.int32)],
        compiler_params=_params("arbitrary"),
        name="outproj_router",
    )(oa, ob, x2d, mod3, mod3, mod3, gga, ggb, gpost, gpre, w_out, wr_hi, wr_lo, b_router.reshape(N_EXPERTS, 1))
    return outs


def _piece_table(tab_ref, e):
    return (pl.multiple_of(tab_ref[e], ROW_TILE), pl.multiple_of(tab_ref[N_EXPERTS + e], ROW_TILE),
            pl.multiple_of(tab_ref[2 * N_EXPERTS + e], ROW_TILE))


def _to_tiles(x, ref, index=()):
    for c in range(ROW_TILE):
        ref[index + (pl.ds(c, x.shape[0], stride=ROW_TILE), slice(None))] = x[:, c * LANES:(c + 1) * LANES]


def _from_tiles(ref, n_rows, index=()):
    return jnp.concatenate([ref[index + (pl.ds(c, n_rows, stride=ROW_TILE), slice(None))]
                            for c in range(ROW_TILE)], axis=1)


def _dispatch_kernel(fill_from_ref, pad_end_ref, tab_ref, h_ref, lpos_ref, xs_ref,
                     stage_ref, zero_ref, pend_ref, sems, zsem, *, grid_steps):
    i = pl.program_id(0)
    n_steps = pl.num_programs(0)
    n_slots = stage_ref.shape[0]
    slot = i % n_slots
    n_local = stage_ref.shape[1] // ROW_TILE
    fill_rows = zero_ref.shape[0]

    def fill(row):
        return pltpu.make_async_copy(zero_ref.at[pl.ds(0, fill_rows)],
                                     xs_ref.at[pl.ds(pl.multiple_of(row, fill_rows), fill_rows)], zsem)

    def pad_fill(e):
        n = pl.multiple_of(pad_end_ref[e] - fill_from_ref[e], ROW_TILE)
        return n, pltpu.make_async_copy(zero_ref.at[pl.ds(0, n)],
                                        xs_ref.at[pl.ds(pl.multiple_of(fill_from_ref[e], ROW_TILE), n)], zsem)

    def drain(s):
        n = pl.multiple_of(pend_ref[s], ROW_TILE)

        @pl.when(n > 0)
        def _():
            pltpu.make_async_copy(stage_ref.at[s, pl.ds(0, n)], xs_ref.at[pl.ds(0, n)], sems.at[s]).wait()
        pend_ref[s] = 0

    @pl.when(i == 0)
    def _():
        zero_ref[...] = jnp.zeros_like(zero_ref)
        for s in range(n_slots):
            pend_ref[s] = 0

    for j in range(-(-N_EXPERTS // grid_steps)):
        e = i * -(-N_EXPERTS // grid_steps) + j

        @pl.when(e < N_EXPERTS)
        def _(e=e):
            n, copy = pad_fill(e)

            @pl.when(n > 0)
            def _():
                copy.start()

    row = lax.broadcasted_iota(jnp.int32, (n_local, h_ref.shape[0]), 0)
    place = row == lpos_ref[0, 0:1, :]
    for k in range(1, TOP_K):
        place = place | (row == lpos_ref[0, k:k + 1, :])
    rows = _dot(place.astype(jnp.bfloat16), h_ref[...])

    drain(slot)
    _to_tiles(rows, stage_ref, (slot,))

    total = 0
    for e in range(N_EXPERTS):
        loc, glob, n = _piece_table(tab_ref, e)

        @pl.when(n > 0)
        def _(loc=loc, glob=glob, n=n, e=e):
            pltpu.make_async_copy(stage_ref.at[slot, pl.ds(loc, n)], xs_ref.at[pl.ds(glob, n)],
                                  sems.at[slot]).start(priority=e % 2)
        total = total + n
    pend_ref[slot] = total

    tail0 = pad_end_ref[N_EXPERTS - 1]
    n_tail = (xs_ref.shape[0] - tail0) // fill_rows
    per_step = (n_tail + n_steps - 1) // n_steps

    def start_tail(j, c):
        t = i * per_step + j

        @pl.when(t < n_tail)
        def _():
            fill(tail0 + t * fill_rows).start()
        return c
    lax.fori_loop(0, per_step, start_tail, 0)

    @pl.when(i == n_steps - 1)
    def _():
        for s in range(n_slots):
            drain(s)

        def wait_tail(j, c):
            fill(0).wait()
            return c
        lax.fori_loop(0, n_tail, wait_tail, 0)

        def wait_pad(e, c):
            n, copy = pad_fill(e)

            @pl.when(n > 0)
            def _():
                copy.wait()
            return c
        lax.fori_loop(0, N_EXPERTS, wait_pad, 0)


def _dispatch(h2, tab_flat, lpos_t, fill_from, pad_end, *, n_rows):
    t, d = h2.shape
    tile = DISPATCH_TILE
    assert d == ROW_TILE * LANES
    n_local = tile * TOP_K
    grid_spec = pltpu.PrefetchScalarGridSpec(
        num_scalar_prefetch=2,
        grid=(t // tile,),
        in_specs=[pl.BlockSpec((TAB_WIDTH,), lambda i, *_: (i,), memory_space=pltpu.SMEM),
                  pl.BlockSpec((tile, d), lambda i, *_: (i, 0)),
                  pl.BlockSpec((1, TOP_K, tile), lambda i, *_: (i, 0, 0))],
        out_specs=pl.BlockSpec(memory_space=pl.ANY),
        scratch_shapes=[pltpu.VMEM((STAGE_SLOTS, n_local * ROW_TILE, LANES), jnp.float32),
                        pltpu.VMEM((FFN_STEP * ROW_TILE, LANES), jnp.float32),
                        pltpu.SMEM((STAGE_SLOTS,), jnp.int32),
                        pltpu.SemaphoreType.DMA((STAGE_SLOTS,)), pltpu.SemaphoreType.DMA(())],
    )
    return pl.pallas_call(
        functools.partial(_dispatch_kernel, grid_steps=t // tile),
        grid_spec=grid_spec,
        out_shape=jax.ShapeDtypeStruct((n_rows * ROW_TILE, LANES), jnp.float32),
        compiler_params=_params("arbitrary"),
        name="dispatch",
    )(fill_from, pad_end, tab_flat, h2, lpos_t)


def _ffn_kernel(te_ref, rows_ref, n_used_ref, x_ref, w1_ref, b1_ref, w2_ref, b2_ref, y_ref,
                w1p_ref, w2b_ref, act_ref):
    i = pl.program_id(0)
    live = i < n_used_ref[0]
    n_pair = w1_ref.shape[2] // FFN_CHUNK
    half = FFN_CHUNK // 2
    n_sub = jnp.where(live, (rows_ref[i] + FFN_TILE - 1) // FFN_TILE, 0)

    @pl.when(live & ((i == 0) | (te_ref[i] != te_ref[jnp.maximum(i - 1, 0)])))
    def _():
        src = lax.broadcasted_iota(jnp.int32, (FFN_CHUNK, FFN_CHUNK), 0)
        dst = lax.broadcasted_iota(jnp.int32, (FFN_CHUNK, FFN_CHUNK), 1)
        unzip = (src == jnp.where(dst < half, 2 * dst, 2 * (dst - half) + 1)).astype(jnp.bfloat16)
        for c in range(n_pair):
            cs = slice(c * FFN_CHUNK, (c + 1) * FFN_CHUNK)
            w1p_ref[:, cs] = _dot(w1_ref[0, :, cs].astype(jnp.bfloat16), unzip).astype(jnp.bfloat16)
        w2b_ref[...] = w2_ref[0].astype(jnp.bfloat16)

    def compute(n_rows):
        x = _from_tiles(x_ref, n_rows).astype(jnp.bfloat16)
        for c in range(0, n_pair, FFN_GROUP):
            cs = slice(c * FFN_CHUNK, (c + FFN_GROUP) * FFN_CHUNK)
            h = _dot(x, w1p_ref[:, cs]) + b1_ref[0, :, cs]
            g = jnp.concatenate([h[:, k * FFN_CHUNK:k * FFN_CHUNK + half] for k in range(FFN_GROUP)], axis=1)
            u = jnp.concatenate([h[:, k * FFN_CHUNK + half:(k + 1) * FFN_CHUNK] for k in range(FFN_GROUP)], axis=1)
            g = jnp.minimum(g, SWIGLU_LIMIT)
            u = jnp.clip(u, -SWIGLU_LIMIT, SWIGLU_LIMIT)
            act = g * jax.nn.sigmoid(SWIGLU_ALPHA * g) * (u + 1.0)
            act_ref[:n_rows, c * half:(c + FFN_GROUP) * half] = act.astype(act_ref.dtype)
        _to_tiles(_dot(act_ref[:n_rows, :], w2b_ref[...]) + b2_ref[0], y_ref)
        if n_rows * ROW_TILE < y_ref.shape[0]:
            y_ref[n_rows * ROW_TILE:, :] = jnp.zeros((y_ref.shape[0] - n_rows * ROW_TILE, LANES), y_ref.dtype)

    for tiles in range(1, FFN_STEP // FFN_TILE + 1):
        pl.when(n_sub == tiles)(functools.partial(compute, tiles * FFN_TILE))

    @pl.when(n_sub == 0)
    def _():
        y_ref[...] = jnp.zeros_like(y_ref)


def _unzip_bias(b1):
    e, f2 = b1.shape
    half = FFN_CHUNK // 2
    return b1.reshape(e, f2 // FFN_CHUNK, half, 2).transpose(0, 1, 3, 2).reshape(e, 1, f2)


def _ffn(xs, step_expert, step_rows, n_used, w1, b1, w2, b2):
    d, f2 = w1.shape[1:]
    n_steps = xs.shape[0] // (FFN_STEP * ROW_TILE)
    live = lambda i, nu: jnp.minimum(i, nu[0] - 1)
    wspec = lambda shape: pl.BlockSpec((1,) + shape, lambda i, te, nr, nu: (te[live(i, nu)], 0, 0))
    grid_spec = pltpu.PrefetchScalarGridSpec(
        num_scalar_prefetch=3,
        grid=(n_steps,),
        in_specs=[pl.BlockSpec((FFN_STEP * ROW_TILE, LANES), lambda i, te, nr, nu: (live(i, nu), 0)),
                  wspec((d, f2)), wspec((1, f2)), wspec((f2 // 2, d)), wspec((1, d))],
        out_specs=pl.BlockSpec((FFN_STEP * ROW_TILE, LANES), lambda i, te, nr, nu: (i, 0)),
        scratch_shapes=[pltpu.VMEM((d, f2), jnp.bfloat16), pltpu.VMEM((f2 // 2, d), jnp.bfloat16),
                        pltpu.VMEM((FFN_STEP, f2 // 2), jnp.bfloat16)],
    )
    return pl.pallas_call(
        _ffn_kernel,
        grid_spec=grid_spec,
        out_shape=jax.ShapeDtypeStruct(xs.shape, jnp.float32),
        compiler_params=_params("arbitrary"),
        name="expert_ffn",
    )(step_expert, step_rows, n_used, xs, w1, _unzip_bias(b1), w2, b2[:, None, :])


def _combine_kernel(*refs):
    tab_refs = refs[:LAND_SLOTS]
    lpos_ref, wts_ref, x1_ref, ga_ref, g_ref, y_ref, o_ref, buf_ref, sems = refs[LAND_SLOTS:]
    tab_ref = tab_refs[0]
    i = pl.program_id(0)
    n_steps = pl.num_programs(0)
    n_slots = buf_ref.shape[0]
    slot = i % n_slots
    tile = x1_ref.shape[0]
    n_local = buf_ref.shape[1] // ROW_TILE

    def fetch(tab, s):
        for e in range(N_EXPERTS):
            loc, glob, n = _piece_table(tab, e)

            @pl.when(n > 0)
            def _(loc=loc, glob=glob, n=n, e=e):
                pltpu.make_async_copy(y_ref.at[pl.ds(glob, n)], buf_ref.at[s, pl.ds(loc, n)],
                                      sems.at[s]).start(priority=e % 2)

    @pl.when(i == 0)
    def _():
        for s in range(n_slots - 1):
            pl.when(s < n_steps)(functools.partial(fetch, tab_refs[s], s))

    @pl.when(i + n_slots - 1 < n_steps)
    def _():
        fetch(tab_refs[n_slots - 1], (i + n_slots - 1) % n_slots)

    n = pl.multiple_of(tab_ref[3 * N_EXPERTS], ROW_TILE)

    @pl.when(n > 0)
    def _():
        pltpu.make_async_copy(y_ref.at[pl.ds(0, n)], buf_ref.at[slot, pl.ds(0, n)], sems.at[slot]).wait()

    rows = _from_tiles(buf_ref, n_local, (slot,)).astype(jnp.bfloat16)
    part = tile // OUT_PARTS
    col = lax.broadcasted_iota(jnp.int32, (part, n_local), 1)
    for p in range(OUT_PARTS):
        r = slice(p * part, (p + 1) * part)
        pos = lpos_ref[r, :]
        w = wts_ref[r, :]
        pick = jnp.where(col == pos[:, 0:1], w[:, 0:1], 0.0)
        for k in range(1, TOP_K):
            pick = pick + jnp.where(col == pos[:, k:k + 1], w[:, k:k + 1], 0.0)
        y = _dot(pick.astype(jnp.bfloat16), rows)
        o_ref[r, :] = x1_ref[r, :] + ga_ref[0] * _rms(y, g_ref[...])


def _combine(ys, tab_flat, lpos, wts, x1, mod3, g_post, *, seq):
    t, d = x1.shape
    tile = DISPATCH_TILE
    n_steps = t // tile
    per_seq = seq // tile
    n_local = tile * TOP_K
    return pl.pallas_call(
        _combine_kernel,
        grid=(n_steps,),
        in_specs=[pl.BlockSpec((TAB_WIDTH,), lambda i, s=s: (jnp.minimum(i + s, n_steps - 1),),
                               memory_space=pltpu.SMEM) for s in range(LAND_SLOTS)] + [
                  pl.BlockSpec((tile, TOP_K), lambda i: (i, 0)),
                  pl.BlockSpec((tile, TOP_K), lambda i: (i, 0)),
                  pl.BlockSpec((tile, d), lambda i: (i, 0)),
                  pl.BlockSpec((1, 1, d), lambda i: (i // per_seq, 0, 5)),
                  pl.BlockSpec((1, d), lambda i: (0, 0)),
                  pl.BlockSpec(memory_space=pl.ANY)],
        out_specs=pl.BlockSpec((tile, d), lambda i: (i, 0)),
        out_shape=jax.ShapeDtypeStruct((t, d), jnp.float32),
        scratch_shapes=[pltpu.VMEM((LAND_SLOTS, n_local * ROW_TILE, LANES), ys.dtype),
                        pltpu.SemaphoreType.DMA((LAND_SLOTS,))],
        compiler_params=_params("arbitrary"),
        name="combine",
    )(*([tab_flat] * LAND_SLOTS), lpos, wts, x1, mod3, g_post, ys)


def _rope_tables(seq):
    pos = np.arange(seq)
    n_freq = HEAD_DIM // 4
    freqs = ROPE_BASE ** (-jnp.arange(n_freq, dtype=jnp.float32) / n_freq)
    rows = jnp.asarray(pos // GRID_W, jnp.float32)[:, None] * freqs[None, :]
    cols = jnp.asarray(pos % GRID_W, jnp.float32)[:, None] * freqs[None, :]
    ang = jnp.concatenate([rows, rows, cols, cols], axis=1)
    sign = np.tile(np.repeat([-1.0, 1.0], n_freq), 2).astype(np.float32)
    cos = jnp.cos(ang)
    sin = jnp.sin(ang) * sign[None, :]
    reps = LANES // HEAD_DIM
    return jnp.tile(cos, (1, reps)), jnp.tile(sin, (1, reps))


def kernel(x, c, ctx, c_ctx, w_ada, b_ada, g_pre_mix, g_post_mix, g_pre_ffn, g_post_ffn, w_in, g_grp_a, g_grp_b,
           sink_a, rpb_b, w_out, w_router, b_router, w_mlp1, b_mlp1, w_mlp2, b_mlp2):
    batch, seq, d = x.shape
    n_ctx = ctx.shape[1]
    assert w_ada.shape[0] == 1, "single layer"
    assert seq % (ATT_BLOCKS * GRID_W * NB_Q_ROWS) == 0 and seq // GRID_W >= NB_K_ROWS
    assert seq % (ATT_BLOCKS * BLOCK_A) == 0 and seq >= BLOCK_A + 2 * WINDOW
    n_tok = batch * seq
    bf16 = jnp.bfloat16

    def pair_heads(a, axis):
        shape = a.shape
        a = a.reshape(shape[:axis] + (N_KV_A, GQA_GROUP, HEAD_DIM) + shape[axis + 1:])
        return jnp.swapaxes(a, axis, axis + 1).reshape(shape)

    mod_rows = -(-(batch + 1) // 8) * 8
    cc = jnp.zeros((mod_rows, d), jnp.float32).at[:batch].set(c).at[batch].set(c_ctx)
    mod3 = _ada(cc, w_ada[0], b_ada[0]).reshape(mod_rows, 1, 6 * d)

    w_in0 = w_in[0]
    w_lat = jnp.concatenate([pair_heads(w_in0[:, :WIDTH_A], 1), w_in0[:, WIDTH_A:]], axis=1).astype(bf16)
    kv_a_end = WIDTH_A + 2 * WIDTH_KV_A
    w_ctx = jnp.concatenate([w_in0[:, WIDTH_A:kv_a_end], w_in0[:, kv_a_end + WIDTH_B:]], axis=1).astype(bf16)
    g_pre = g_pre_mix[0].reshape(1, d)
    qa, ka, va, qb, kb, vb = _inproj(x.reshape(n_tok, d), mod3, g_pre, w_lat, _rope_tables(seq),
                                     seq=seq, mod_row0=0, latent=True)
    kac, vac, kbc, vbc = _inproj(ctx.reshape(batch * n_ctx, d), mod3, g_pre, w_ctx, None,
                                 seq=n_ctx, mod_row0=batch, latent=False)

    oa = _window_attention(qa, ka.reshape(batch, seq, -1), va.reshape(batch, seq, -1),
                           kac.reshape(batch, n_ctx, -1), vac.reshape(batch, n_ctx, -1),
                           sink_a[0].astype(jnp.float32) * LOG2E, batch=batch, seq=seq)
    tabs, cls_of_block = _nb_bias_tables(rpb_b[0], seq // GRID_W)
    ob = _neighbourhood_attention(qb, kb.reshape(batch, seq, -1), vb.reshape(batch, seq, -1),
                                  kbc.reshape(batch, n_ctx, -1), vbc.reshape(batch, n_ctx, -1),
                                  tabs, cls_of_block, batch=batch, seq=seq)

    w_out0 = w_out[0]
    w_o = jnp.concatenate([pair_heads(w_out0[:WIDTH_A], 0), w_out0[WIDTH_A:]], axis=0).astype(bf16)
    x1, h2, idx, wts, rank, size, base, counts = _outproj(
        oa, ob, x.reshape(n_tok, d), mod3,
        pair_heads(g_grp_a[0], 0).reshape(1, -1), g_grp_b[0].reshape(1, -1),
        g_post_mix[0].reshape(1, d), g_pre_ffn[0].reshape(1, d),
        w_o, w_router[0], b_router[0].reshape(1, -1), seq=seq)

    n_tok_tiles = n_tok // DISPATCH_TILE
    size = size.reshape(n_tok_tiles, N_EXPERTS)
    counts = counts.reshape(-1)
    padded = (counts + FFN_STEP - 1) // FFN_STEP * FFN_STEP
    pad_end = jnp.cumsum(padded).astype(jnp.int32)
    pad_start = pad_end - padded
    n_steps = n_tok * TOP_K // FFN_STEP + N_EXPERTS
    n_used = (pad_end[-1:] // FFN_STEP).astype(jnp.int32)
    step_row0 = jnp.arange(n_steps, dtype=jnp.int32) * FFN_STEP
    step_expert = jnp.minimum(jnp.sum(step_row0[:, None] >= pad_end[None, :], axis=1),
                              N_EXPERTS - 1).astype(jnp.int32)
    own = step_expert[:, None] == jnp.arange(N_EXPERTS, dtype=jnp.int32)
    step_rows = jnp.clip(jnp.sum(jnp.where(own, (pad_start + counts)[None, :], 0), axis=1) - step_row0,
                         0, FFN_STEP).astype(jnp.int32)
    fill_from = (pad_start + counts).astype(jnp.int32)
    local0 = jnp.cumsum(size, axis=1) - size
    global0 = pad_start[None, :] + base.reshape(n_tok_tiles, N_EXPERTS)
    tab = jnp.concatenate([local0, global0, size, jnp.sum(size, axis=1, keepdims=True),
                           jnp.zeros((n_tok_tiles, TAB_WIDTH - 3 * N_EXPERTS - 1), jnp.int32)], axis=1)
    tab = (tab * ROW_TILE).reshape(-1).astype(jnp.int32)
    hot = idx[:, :, None, :] == jnp.arange(N_EXPERTS, dtype=jnp.int32)[None, None, :, None]
    lpos_t = rank + jnp.sum(jnp.where(hot, local0[:, None, :, None], 0), axis=2)
    by_token = lambda a: jnp.swapaxes(a, 1, 2).reshape(n_tok, TOP_K)

    xs = _dispatch(h2, tab, lpos_t, fill_from * ROW_TILE, pad_end * ROW_TILE, n_rows=n_steps * FFN_STEP)
    ys = _ffn(xs, step_expert, step_rows, n_used, w_mlp1[0], b_mlp1[0], w_mlp2[0], b_mlp2[0])
    out = _combine(ys, tab, by_token(lpos_t), by_token(wts), x1, mod3, g_post_ffn[0].reshape(1, d), seq=seq)
    return out.reshape(batch, seq, d)
```

```python
import functools

import numpy as np
import jax
import jax.numpy as jnp
from jax import lax
from jax.experimental import pallas as pl
from jax.experimental.pallas import tpu as pltpu

GRID_W = 64
HEAD_DIM = 64
N_HEADS_A = 8
N_KV_A = 2
GQA_GROUP = N_HEADS_A // N_KV_A
N_HEADS_B = 8
WIDTH_A = N_HEADS_A * HEAD_DIM
WIDTH_KV_A = N_KV_A * HEAD_DIM
WIDTH_B = N_HEADS_B * HEAD_DIM
WINDOW = 128
BLOCK_A = 128
NA_ROWS = 8
NA_COLS = 16
N_EXPERTS = 32
TOP_K = 4
SWIGLU_LIMIT = 7.0
SWIGLU_ALPHA = 1.702
ROPE_BASE = 10000.0
EPS = 1e-6
NEG_INF = -1e30

LANES = 128
VMEM_LIMIT = 56 * 1024 * 1024

LOG2E = 1.4426950408889634
ATT_BLOCKS = 8
NB_Q_ROWS = 2
NB_K_ROWS = NB_Q_ROWS + NA_ROWS
PROJ_TILE = 1024
OUT_TILE = 256
OUT_PARTS = 2
STAGE_SLOTS = 4
LAND_SLOTS = 4
FFN_TILE = 256
FFN_STEP = 4 * FFN_TILE
FFN_CHUNK = 256
FFN_GROUP = 4
DISPATCH_TILE = OUT_TILE
ROW_TILE = 8
TAB_WIDTH = 128


def _params(*sem):
    return pltpu.CompilerParams(dimension_semantics=sem, vmem_limit_bytes=VMEM_LIMIT)


def _rms(x, g):
    return x * lax.rsqrt(jnp.mean(x * x, axis=-1, keepdims=True) + EPS) * g


def _dot(a, b):
    return jnp.dot(a, b, preferred_element_type=jnp.float32)


def _dot_nt(a, b):
    return lax.dot_general(a, b, (((1,), (1,)), ((), ())), preferred_element_type=jnp.float32)


def _ada_kernel(c_ref, w_ref, b_ref, o_ref):
    c = c_ref[...]
    s = (c * jax.nn.sigmoid(c)).astype(jnp.bfloat16)
    o_ref[...] = _dot(s, w_ref[...].astype(jnp.bfloat16)) + b_ref[...]


def _ada(cc, w_ada, b_ada):
    rows, d = cc.shape
    n_out = w_ada.shape[1]
    return pl.pallas_call(
        _ada_kernel,
        grid=(n_out // d,),
        in_specs=[pl.BlockSpec((rows, d), lambda j: (0, 0)),
                  pl.BlockSpec((d, d), lambda j: (0, j)),
                  pl.BlockSpec((1, d), lambda j: (0, j))],
        out_specs=pl.BlockSpec((rows, d), lambda j: (0, j)),
        out_shape=jax.ShapeDtypeStruct((rows, n_out), jnp.float32),
        compiler_params=_params("arbitrary"),
        name="ada",
    )(cc, w_ada, b_ada.reshape(1, n_out))


def _rope(x, cos, sin):
    w = x.shape[1]
    reps = w // LANES
    if reps > 1:
        cos = jnp.concatenate([cos] * reps, axis=1)
        sin = jnp.concatenate([sin] * reps, axis=1)
    lane = lax.broadcasted_iota(jnp.int32, x.shape, 1)
    quarter = HEAD_DIM // 4
    partner = jnp.where(lane % (2 * quarter) < quarter,
                        pltpu.roll(x, w - quarter, 1), pltpu.roll(x, quarter, 1))
    return x * cos + partner * sin


def _inproj_kernel(x_ref, sh_ref, sc_ref, g_ref, w_ref, *rest, latent):
    x = x_ref[...]
    h = _rms(x, g_ref[...]) * (1.0 + sc_ref[0]) + sh_ref[0]
    p = _dot(h.astype(jnp.bfloat16), w_ref[...])
    if latent:
        cos_ref, sin_ref, qa_ref, ka_ref, va_ref, qb_ref, kb_ref, vb_ref = rest
        cos, sin = cos_ref[...], sin_ref[...]
        scale = HEAD_DIM ** -0.5 * LOG2E
        o = 0
        qa_ref[...] = (_rope(p[:, o:o + WIDTH_A], cos, sin) * scale).astype(qa_ref.dtype)
        o += WIDTH_A
        ka_ref[...] = _rope(p[:, o:o + WIDTH_KV_A], cos, sin).astype(ka_ref.dtype)
        o += WIDTH_KV_A
        va_ref[...] = p[:, o:o + WIDTH_KV_A].astype(va_ref.dtype)
        o += WIDTH_KV_A
        qb_ref[...] = (p[:, o:o + WIDTH_B] * scale).astype(qb_ref.dtype)
        o += WIDTH_B
    else:
        ka_ref, va_ref, kb_ref, vb_ref = rest
        o = 0
        ka_ref[...] = p[:, o:o + WIDTH_KV_A].astype(ka_ref.dtype)
        o += WIDTH_KV_A
        va_ref[...] = p[:, o:o + WIDTH_KV_A].astype(va_ref.dtype)
        o += WIDTH_KV_A
    kb_ref[...] = p[:, o:o + WIDTH_B].astype(kb_ref.dtype)
    o += WIDTH_B
    vb_ref[...] = p[:, o:o + WIDTH_B].astype(vb_ref.dtype)


def _inproj(x2d, mod3, g_pre, w, rope_tabs, *, seq, mod_row0, latent):
    t, d = x2d.shape
    tile = min(PROJ_TILE, seq)
    per_seq = seq // tile
    if latent:
        mod_row = lambda i: i // per_seq
    else:
        mod_row = lambda i: mod_row0
    in_specs = [pl.BlockSpec((tile, d), lambda i: (i, 0)),
                pl.BlockSpec((1, 1, d), lambda i: (mod_row(i), 0, 0)),
                pl.BlockSpec((1, 1, d), lambda i: (mod_row(i), 0, 1)),
                pl.BlockSpec((1, d), lambda i: (0, 0)),
                pl.BlockSpec(w.shape, lambda i: (0, 0))]
    args = [x2d, mod3, mod3, g_pre, w]
    widths = [WIDTH_KV_A, WIDTH_KV_A, WIDTH_B, WIDTH_B]
    if latent:
        in_specs += [pl.BlockSpec((tile, LANES), lambda i: (i % per_seq, 0))] * 2
        args += list(rope_tabs)
        widths = [WIDTH_A, WIDTH_KV_A, WIDTH_KV_A, WIDTH_B, WIDTH_B, WIDTH_B]
    return pl.pallas_call(
        functools.partial(_inproj_kernel, latent=latent),
        grid=(t // tile,),
        in_specs=in_specs,
        out_specs=[pl.BlockSpec((tile, wd), lambda i: (i, 0)) for wd in widths],
        out_shape=[jax.ShapeDtypeStruct((t, wd), jnp.bfloat16) for wd in widths],
        compiler_params=_params("parallel"),
        name="inproj_latent" if latent else "inproj_ctx",
    )(*args)


def _ones_beside(v):
    return jnp.concatenate([v, jnp.ones_like(v)], axis=1)


def _group_attention(q_ref, o_ref, rows, groups, s_ref, p_ref, m_ref):
    tq = rows.stop - rows.start
    low = lax.broadcasted_iota(jnp.int32, (tq, LANES), 1) < HEAD_DIM
    layout = []
    base = 0
    for pairs, k_loc, k_ctx, _, _, _, _ in groups:
        heads = [(j, 0) for j in pairs] + [(j, 1) for j in pairs]
        n_loc = k_loc.shape[0]
        stacked = []
        for j, half in heads:
            q = q_ref[rows, j * LANES:(j + 1) * LANES]
            stacked.append(jnp.where(low if half == 0 else ~low, q, jnp.zeros_like(q)))
        qs = jnp.concatenate(stacked, axis=0)
        n = len(heads) * tq
        s_ref[base:base + n, :] = _dot_nt(qs, jnp.concatenate([k_loc, k_ctx], axis=0))
        layout.append((base, heads, n_loc))
        base += n
    for (base, heads, n_loc), group in zip(layout, groups):
        bias_of, sink_of = group[5], group[6]
        for g, (j, half) in enumerate(heads):
            r = slice(base + g * tq, base + (g + 1) * tq)
            s_loc = s_ref[r, :n_loc] + bias_of(j, half)
            s_ctx = s_ref[r, n_loc:]
            m = jnp.maximum(jnp.max(s_loc, axis=1, keepdims=True), jnp.max(s_ctx, axis=1, keepdims=True))
            sink = sink_of(j, half)
            if sink is not None:
                m = jnp.maximum(m, sink)
            p_ref[r, :n_loc] = jnp.exp2((s_loc - m).astype(jnp.bfloat16))
            p_ref[r, n_loc:] = jnp.exp2((s_ctx - m).astype(jnp.bfloat16))
            m_ref[r, :] = m
    for (base, heads, n_loc), group in zip(layout, groups):
        pairs, _, _, v_loc, v_ctx, _, sink_of = group
        n = len(heads) * tq
        both = _dot(p_ref[base:base + n, :], _ones_beside(jnp.concatenate([v_loc, v_ctx], axis=0)))
        for a, j in enumerate(pairs):
            outs = []
            for half in range(2):
                g = half * len(pairs) + a
                r = slice(g * tq, (g + 1) * tq)
                den = both[r, LANES:]
                sink = sink_of(j, half)
                if sink is not None:
                    den = den + jnp.exp2(sink - m_ref[base + g * tq:base + (g + 1) * tq, :])
                outs.append(both[r, :LANES] / den)
            o_ref[rows, j * LANES:(j + 1) * LANES] = jnp.where(low, outs[0], outs[1]).astype(o_ref.dtype)


def _attention_scratch(tq, n_heads, n_keys):
    return [pltpu.VMEM((n_heads * tq, n_keys), jnp.float32), pltpu.VMEM((n_heads * tq, n_keys), jnp.bfloat16),
            pltpu.VMEM((n_heads * tq, 1), jnp.float32)]


def _win_kernel(sink_ref, q_ref, k_ref, v_ref, kc_ref, vc_ref, o_ref, *scratch, seq):
    span = BLOCK_A + 2 * WINDOW
    pairs = list(range(GQA_GROUP))
    for sb in range(ATT_BLOCKS):
        i = pl.program_id(1) * ATT_BLOCKS + sb
        start = pl.multiple_of(jnp.clip(i * BLOCK_A - WINDOW, 0, seq - span), BLOCK_A)
        qpos = i * BLOCK_A + lax.broadcasted_iota(jnp.int32, (BLOCK_A, span), 0)
        kpos = start + lax.broadcasted_iota(jnp.int32, (BLOCK_A, span), 1)
        bias = jnp.where(jnp.abs(kpos - qpos) <= WINDOW, 0.0, NEG_INF).astype(jnp.float32)
        groups = [([j], k_ref[0, pl.ds(start, span), :], kc_ref[0], v_ref[0, pl.ds(start, span), :], vc_ref[0],
                   lambda j, half: bias, lambda j, half: sink_ref[half * GQA_GROUP + j]) for j in pairs]
        _group_attention(q_ref, o_ref, slice(sb * BLOCK_A, (sb + 1) * BLOCK_A), groups, *scratch)


def _window_attention(qa, ka, va, kac, vac, sink, *, batch, seq):
    tq = ATT_BLOCKS * BLOCK_A
    nb = seq // tq
    n_ctx = kac.shape[1]
    return pl.pallas_call(
        functools.partial(_win_kernel, seq=seq),
        grid=(batch, nb),
        in_specs=[pl.BlockSpec(memory_space=pltpu.SMEM),
                  pl.BlockSpec((tq, WIDTH_A), lambda b, i: (b * nb + i, 0)),
                  pl.BlockSpec((1, seq, WIDTH_KV_A), lambda b, i: (b, 0, 0)),
                  pl.BlockSpec((1, seq, WIDTH_KV_A), lambda b, i: (b, 0, 0)),
                  pl.BlockSpec((1, n_ctx, WIDTH_KV_A), lambda b, i: (b, 0, 0)),
                  pl.BlockSpec((1, n_ctx, WIDTH_KV_A), lambda b, i: (b, 0, 0))],
        out_specs=pl.BlockSpec((tq, WIDTH_A), lambda b, i: (b * nb + i, 0)),
        out_shape=jax.ShapeDtypeStruct((batch * seq, WIDTH_A), jnp.bfloat16),
        scratch_shapes=_attention_scratch(BLOCK_A, N_HEADS_A, BLOCK_A + 2 * WINDOW + n_ctx),
        compiler_params=_params("parallel", "arbitrary"),
        name="window_attention",
    )(sink, qa, ka, va, kac, vac)


def _nb_kernel(q_ref, k_ref, v_ref, kc_ref, vc_ref, *rest, rows_n):
    tab_refs, o_ref, scratch = rest[:ATT_BLOCKS], rest[ATT_BLOCKS], rest[ATT_BLOCKS + 1:]
    n_keys = NB_K_ROWS * GRID_W
    tq = NB_Q_ROWS * GRID_W
    for sb in range(ATT_BLOCKS):
        m = pl.program_id(1) * ATT_BLOCKS + sb
        start_row = jnp.clip(NB_Q_ROWS * m - NA_ROWS // 2, 0, rows_n - NB_K_ROWS)
        start = pl.multiple_of(start_row * GRID_W, LANES)
        tab_ref = tab_refs[sb]
        groups = []
        for j in range(q_ref.shape[1] // LANES):
            cols = slice(j * LANES, (j + 1) * LANES)
            groups.append(([j], k_ref[0, pl.ds(start, n_keys), cols], kc_ref[0, :, cols],
                           v_ref[0, pl.ds(start, n_keys), cols], vc_ref[0, :, cols],
                           lambda j, half: tab_ref[0, 2 * j + half].astype(jnp.float32), lambda j, half: None))
        _group_attention(q_ref, o_ref, slice(sb * tq, (sb + 1) * tq), groups, *scratch)


def _nb_classes(rows_n):
    n_blocks = rows_n // NB_Q_ROWS
    sig = {}
    cls_of_block = []
    reps = []
    for m in range(n_blocks):
        start_row = int(np.clip(NB_Q_ROWS * m - NA_ROWS // 2, 0, rows_n - NB_K_ROWS))
        key = tuple((start_row - r, int(np.clip(r - NA_ROWS // 2, 0, rows_n - NA_ROWS)) - r)
                    for r in range(NB_Q_ROWS * m, NB_Q_ROWS * (m + 1)))
        if key not in sig:
            sig[key] = len(reps)
            reps.append(m)
        cls_of_block.append(sig[key])
    return np.asarray(cls_of_block, np.int32), reps


def _nb_bias_tables(rpb, rows_n):
    cls_of_block, reps = _nb_classes(rows_n)
    n_heads = rpb.shape[0]
    cq = np.arange(GRID_W)[:, None]
    ck = np.arange(GRID_W)[None, :]
    cs = np.clip(cq - NA_COLS // 2, 0, GRID_W - NA_COLS)
    col_ok = (ck >= cs) & (ck < cs + NA_COLS)
    pick = ((ck - cq + NA_COLS - 1)[None] == np.arange(2 * NA_COLS - 1)[:, None, None]) & col_ok[None]
    tiles = jnp.einsum('hrd,dqk->hrqk', rpb.astype(jnp.float32), jnp.asarray(pick, jnp.float32),
                       precision=lax.Precision.HIGHEST)
    tiles = jnp.where(jnp.asarray(col_ok)[None, None], tiles * LOG2E, NEG_INF).astype(jnp.bfloat16)
    blocked = jnp.full((n_heads, GRID_W, GRID_W), NEG_INF, jnp.bfloat16)
    tabs = []
    for m in reps:
        start_row = int(np.clip(NB_Q_ROWS * m - NA_ROWS // 2, 0, rows_n - NB_K_ROWS))
        q_rows = []
        for r in range(NB_Q_ROWS * m, NB_Q_ROWS * (m + 1)):
            rs = int(np.clip(r - NA_ROWS // 2, 0, rows_n - NA_ROWS))
            q_rows.append(jnp.concatenate(
                [tiles[:, krow - r + NA_ROWS - 1] if rs <= krow < rs + NA_ROWS else blocked
                 for krow in range(start_row, start_row + NB_K_ROWS)], axis=2))
        tabs.append(jnp.concatenate(q_rows, axis=1))
    return jnp.stack(tabs), cls_of_block


def _neighbourhood_attention(qb, kb, vb, kbc, vbc, tabs, cls_of_block, *, batch, seq):
    rows_n = seq // GRID_W
    tq = ATT_BLOCKS * NB_Q_ROWS * GRID_W
    nstep = seq // tq
    n_ctx = kbc.shape[1]

    def tab_spec(sb):
        return pl.BlockSpec((1,) + tabs.shape[1:], lambda b, m, c: (c[m * ATT_BLOCKS + sb], 0, 0, 0))

    grid_spec = pltpu.PrefetchScalarGridSpec(
        num_scalar_prefetch=1,
        grid=(batch, nstep),
        in_specs=[pl.BlockSpec((tq, WIDTH_B), lambda b, m, c: (b * nstep + m, 0)),
                  pl.BlockSpec((1, seq, WIDTH_B), lambda b, m, c: (b, 0, 0)),
                  pl.BlockSpec((1, seq, WIDTH_B), lambda b, m, c: (b, 0, 0)),
                  pl.BlockSpec((1, n_ctx, WIDTH_B), lambda b, m, c: (b, 0, 0)),
                  pl.BlockSpec((1, n_ctx, WIDTH_B), lambda b, m, c: (b, 0, 0))]
                 + [tab_spec(sb) for sb in range(ATT_BLOCKS)],
        out_specs=pl.BlockSpec((tq, WIDTH_B), lambda b, m, c: (b * nstep + m, 0)),
        scratch_shapes=_attention_scratch(NB_Q_ROWS * GRID_W, N_HEADS_B, NB_K_ROWS * GRID_W + n_ctx),
    )

    def body(c_ref, *refs):
        _nb_kernel(*refs, rows_n=rows_n)

    return pl.pallas_call(
        body,
        grid_spec=grid_spec,
        out_shape=jax.ShapeDtypeStruct((batch * seq, WIDTH_B), jnp.bfloat16),
        compiler_params=_params("parallel", "arbitrary"),
        name="neighbourhood_attention",
    )(jnp.asarray(cls_of_block), qb, kb, vb, kbc, vbc, *([tabs] * ATT_BLOCKS))


def _outproj_kernel(oa_ref, ob_ref, x_ref, ga_ref, sh_ref, sc_ref, gga_ref, ggb_ref, gpost_ref, gpre_ref,
                    wo_ref, wr_hi_ref, wr_lo_ref, br_ref,
                    x1_ref, h2_ref, idx_ref, wts_ref, rank_ref, size_ref, base_ref, cnt_ref, run_ref):
    i = pl.program_id(0)

    @pl.when(i == 0)
    def _():
        run_ref[...] = jnp.zeros_like(run_ref)

    n_rows = x_ref.shape[0]
    parts = [slice(p * n_rows // OUT_PARTS, (p + 1) * n_rows // OUT_PARTS) for p in range(OUT_PARTS)]
    na = [_rms(oa_ref[r, :].astype(jnp.float32), gga_ref[...]).astype(jnp.bfloat16) for r in parts]
    nb = [_rms(ob_ref[r, :].astype(jnp.float32), ggb_ref[...]).astype(jnp.bfloat16) for r in parts]
    mix = [_dot(a, wo_ref[:WIDTH_A, :]) + _dot(b, wo_ref[WIDTH_A:, :]) for a, b in zip(na, nb)]
    x1 = [x_ref[r, :] + ga_ref[0] * _rms(m, gpost_ref[...]) for r, m in zip(parts, mix)]
    for r, v in zip(parts, x1):
        x1_ref[r, :] = v
    h2 = [_rms(v, gpre_ref[...]) * (1.0 + sc_ref[0]) + sh_ref[0] for v in x1]
    for r, v in zip(parts, h2):
        h2_ref[r, :] = v.astype(h2_ref.dtype)

    h_hi = [v.astype(jnp.bfloat16) for v in h2]
    h_lo = [(v - hi.astype(jnp.float32)).astype(jnp.bfloat16) for v, hi in zip(h2, h_hi)]
    logits = jnp.concatenate([_dot(hi, wr_hi_ref[...]) + _dot(lo, wr_hi_ref[...]) + _dot(hi, wr_lo_ref[...])
                              for hi, lo in zip(h_hi, h_lo)], axis=0)
    work = logits.T[:N_EXPERTS] + br_ref[...]
    e, t = work.shape
    row = lax.broadcasted_iota(jnp.int32, (e, t), 0)
    chosen = jnp.zeros((e, t), jnp.float32)
    vals, sels, hots = [], [], []
    for _k in range(TOP_K):
        mx = jnp.max(work, axis=0, keepdims=True)
        sel = jnp.min(jnp.where(work == mx, row, e), axis=0, keepdims=True)
        hot = row == sel
        vals.append(mx)
        sels.append(sel)
        hots.append(hot)
        work = jnp.where(hot, -jnp.inf, work)
        chosen = chosen + hot.astype(jnp.float32)
    ex = [jnp.exp(v - vals[0]) for v in vals]
    den = ex[0] + ex[1] + ex[2] + ex[3]

    earlier = (lax.broadcasted_iota(jnp.int32, (t, t), 0) < lax.broadcasted_iota(jnp.int32, (t, t), 1))
    prefix = _dot(chosen.astype(jnp.bfloat16), earlier.astype(jnp.bfloat16))
    ranks = [jnp.sum(jnp.where(h, prefix, 0.0), axis=0, keepdims=True) for h in hots]
    size = jnp.sum(chosen, axis=1, keepdims=True).astype(jnp.int32)
    size_ref[0] = size
    base_ref[0] = run_ref[...]
    run_ref[...] = run_ref[...] + size
    cnt_ref[...] = run_ref[...]

    idx_ref[0] = jnp.concatenate(sels, axis=0)
    wts_ref[0] = jnp.concatenate([v / den for v in ex], axis=0)
    rank_ref[0] = jnp.concatenate(ranks, axis=0).astype(jnp.int32)


def _outproj(oa, ob, x2d, mod3, gga, ggb, gpost, gpre, w_out, w_router, b_router, *, seq):
    t, d = x2d.shape
    tile = OUT_TILE
    per_seq = seq // tile
    row = lambda i: (i, 0)
    const = lambda i: (0, 0)
    modspec = lambda part: pl.BlockSpec((1, 1, d), lambda i: (i // per_seq, 0, part))
    per_tile = lambda shape: pl.BlockSpec((1,) + shape, lambda i: (i, 0, 0))
    n_steps = t // tile
    w_pad = jnp.zeros((d, LANES), jnp.float32).at[:, :N_EXPERTS].set(w_router)
    wr_hi = w_pad.astype(jnp.bfloat16)
    wr_lo = (w_pad - wr_hi.astype(jnp.float32)).astype(jnp.bfloat16)
    outs = pl.pallas_call(
        _outproj_kernel,
        grid=(n_steps,),
        in_specs=[pl.BlockSpec((tile, WIDTH_A), row), pl.BlockSpec((tile, WIDTH_B), row),
                  pl.BlockSpec((tile, d), row),
                  modspec(2), modspec(3), modspec(4),
                  pl.BlockSpec((1, WIDTH_A), const), pl.BlockSpec((1, WIDTH_B), const),
                  pl.BlockSpec((1, d), const), pl.BlockSpec((1, d), const),
                  pl.BlockSpec(w_out.shape, const), pl.BlockSpec((d, LANES), const), pl.BlockSpec((d, LANES), const),
                  pl.BlockSpec((N_EXPERTS, 1), const)],
        out_specs=[pl.BlockSpec((tile, d), row), pl.BlockSpec((tile, d), row),
                   per_tile((TOP_K, tile)), per_tile((TOP_K, tile)), per_tile((TOP_K, tile)),
                   per_tile((N_EXPERTS, 1)), per_tile((N_EXPERTS, 1)),
                   pl.BlockSpec((N_EXPERTS, 1), const)],
        out_shape=[jax.ShapeDtypeStruct((t, d), jnp.float32), jax.ShapeDtypeStruct((t, d), jnp.bfloat16),
                   jax.ShapeDtypeStruct((n_steps, TOP_K, tile), jnp.int32),
                   jax.ShapeDtypeStruct((n_steps, TOP_K, tile), jnp.float32),
                   jax.ShapeDtypeStruct((n_steps, TOP_K, tile), jnp.int32),
                   jax.ShapeDtypeStruct((n_steps, N_EXPERTS, 1), jnp.int32),
                   jax.ShapeDtypeStruct((n_steps, N_EXPERTS, 1), jnp.int32),
                   jax.ShapeDtypeStruct((N_EXPERTS, 1), jnp.int32)],
        scratch_shapes=[pltpu.VMEM((N_EXPERTS, 1), jnp.int32)],
        compiler_params=_params("arbitrary"),
        name="outproj_router",
    )(oa, ob, x2d, mod3, mod3, mod3, gga, ggb, gpost, gpre, w_out, wr_hi, wr_lo, b_router.reshape(N_EXPERTS, 1))
    return outs


def _piece_table(tab_ref, e):
    return (pl.multiple_of(tab_ref[e], ROW_TILE), pl.multiple_of(tab_ref[N_EXPERTS + e], ROW_TILE),
            pl.multiple_of(tab_ref[2 * N_EXPERTS + e], ROW_TILE))


def _to_tiles(x, ref, index=()):
    for c in range(ROW_TILE):
        ref[index + (pl.ds(c, x.shape[0], stride=ROW_TILE), slice(None))] = x[:, c * LANES:(c + 1) * LANES]


def _from_tiles(ref, n_rows, index=()):
    return jnp.concatenate([ref[index + (pl.ds(c, n_rows, stride=ROW_TILE), slice(None))]
                            for c in range(ROW_TILE)], axis=1)


def _dispatch_kernel(fill_from_ref, pad_end_ref, tab_ref, h_ref, lpos_ref, xs_ref,
                     stage_ref, zero_ref, pend_ref, sems, zsem, *, grid_steps):
    i = pl.program_id(0)
    n_steps = pl.num_programs(0)
    n_slots = stage_ref.shape[0]
    slot = i % n_slots
    n_local = stage_ref.shape[1] // ROW_TILE
    fill_rows = zero_ref.shape[0]

    def fill(row):
        return pltpu.make_async_copy(zero_ref.at[pl.ds(0, fill_rows)],
                                     xs_ref.at[pl.ds(pl.multiple_of(row, fill_rows), fill_rows)], zsem)

    def pad_fill(e):
        n = pl.multiple_of(pad_end_ref[e] - fill_from_ref[e], ROW_TILE)
        return n, pltpu.make_async_copy(zero_ref.at[pl.ds(0, n)],
                                        xs_ref.at[pl.ds(pl.multiple_of(fill_from_ref[e], ROW_TILE), n)], zsem)

    def drain(s):
        n = pl.multiple_of(pend_ref[s], ROW_TILE)

        @pl.when(n > 0)
        def _():
            pltpu.make_async_copy(stage_ref.at[s, pl.ds(0, n)], xs_ref.at[pl.ds(0, n)], sems.at[s]).wait()
        pend_ref[s] = 0

    @pl.when(i == 0)
    def _():
        zero_ref[...] = jnp.zeros_like(zero_ref)
        for s in range(n_slots):
            pend_ref[s] = 0

    for j in range(-(-N_EXPERTS // grid_steps)):
        e = i * -(-N_EXPERTS // grid_steps) + j

        @pl.when(e < N_EXPERTS)
        def _(e=e):
            n, copy = pad_fill(e)

            @pl.when(n > 0)
            def _():
                copy.start()

    row = lax.broadcasted_iota(jnp.int32, (n_local, h_ref.shape[0]), 0)
    place = row == lpos_ref[0, 0:1, :]
    for k in range(1, TOP_K):
        place = place | (row == lpos_ref[0, k:k + 1, :])
    rows = _dot(place.astype(jnp.bfloat16), h_ref[...])

    drain(slot)
    _to_tiles(rows, stage_ref, (slot,))

    total = 0
    for e in range(N_EXPERTS):
        loc, glob, n = _piece_table(tab_ref, e)

        @pl.when(n > 0)
        def _(loc=loc, glob=glob, n=n, e=e):
            pltpu.make_async_copy(stage_ref.at[slot, pl.ds(loc, n)], xs_ref.at[pl.ds(glob, n)],
                                  sems.at[slot]).start(priority=e % 2)
        total = total + n
    pend_ref[slot] = total

    tail0 = pad_end_ref[N_EXPERTS - 1]
    n_tail = (xs_ref.shape[0] - tail0) // fill_rows
    per_step = (n_tail + n_steps - 1) // n_steps

    def start_tail(j, c):
        t = i * per_step + j

        @pl.when(t < n_tail)
        def _():
            fill(tail0 + t * fill_rows).start()
        return c
    lax.fori_loop(0, per_step, start_tail, 0)

    @pl.when(i == n_steps - 1)
    def _():
        for s in range(n_slots):
            drain(s)

        def wait_tail(j, c):
            fill(0).wait()
            return c
        lax.fori_loop(0, n_tail, wait_tail, 0)

        def wait_pad(e, c):
            n, copy = pad_fill(e)

            @pl.when(n > 0)
            def _():
                copy.wait()
            return c
        lax.fori_loop(0, N_EXPERTS, wait_pad, 0)


def _dispatch(h2, tab_flat, lpos_t, fill_from, pad_end, *, n_rows):
    t, d = h2.shape
    tile = DISPATCH_TILE
    assert d == ROW_TILE * LANES
    n_local = tile * TOP_K
    grid_spec = pltpu.PrefetchScalarGridSpec(
        num_scalar_prefetch=2,
        grid=(t // tile,),
        in_specs=[pl.BlockSpec((TAB_WIDTH,), lambda i, *_: (i,), memory_space=pltpu.SMEM),
                  pl.BlockSpec((tile, d), lambda i, *_: (i, 0)),
                  pl.BlockSpec((1, TOP_K, tile), lambda i, *_: (i, 0, 0))],
        out_specs=pl.BlockSpec(memory_space=pl.ANY),
        scratch_shapes=[pltpu.VMEM((STAGE_SLOTS, n_local * ROW_TILE, LANES), jnp.float32),
                        pltpu.VMEM((FFN_STEP * ROW_TILE, LANES), jnp.float32),
                        pltpu.SMEM((STAGE_SLOTS,), jnp.int32),
                        pltpu.SemaphoreType.DMA((STAGE_SLOTS,)), pltpu.SemaphoreType.DMA(())],
    )
    return pl.pallas_call(
        functools.partial(_dispatch_kernel, grid_steps=t // tile),
        grid_spec=grid_spec,
        out_shape=jax.ShapeDtypeStruct((n_rows * ROW_TILE, LANES), jnp.float32),
        compiler_params=_params("arbitrary"),
        name="dispatch",
    )(fill_from, pad_end, tab_flat, h2, lpos_t)


def _ffn_kernel(te_ref, rows_ref, n_used_ref, x_ref, w1_ref, b1_ref, w2_ref, b2_ref, y_ref,
                w1p_ref, w2b_ref, act_ref):
    i = pl.program_id(0)
    live = i < n_used_ref[0]
    n_pair = w1_ref.shape[2] // FFN_CHUNK
    half = FFN_CHUNK // 2
    n_sub = jnp.where(live, (rows_ref[i] + FFN_TILE - 1) // FFN_TILE, 0)

    @pl.when(live & ((i == 0) | (te_ref[i] != te_ref[jnp.maximum(i - 1, 0)])))
    def _():
        src = lax.broadcasted_iota(jnp.int32, (FFN_CHUNK, FFN_CHUNK), 0)
        dst = lax.broadcasted_iota(jnp.int32, (FFN_CHUNK, FFN_CHUNK), 1)
        unzip = (src == jnp.where(dst < half, 2 * dst, 2 * (dst - half) + 1)).astype(jnp.bfloat16)
        for c in range(n_pair):
            cs = slice(c * FFN_CHUNK, (c + 1) * FFN_CHUNK)
            w1p_ref[:, cs] = _dot(w1_ref[0, :, cs].astype(jnp.bfloat16), unzip).astype(jnp.bfloat16)
        w2b_ref[...] = w2_ref[0].astype(jnp.bfloat16)

    def compute(n_rows):
        x = _from_tiles(x_ref, n_rows).astype(jnp.bfloat16)
        for c in range(0, n_pair, FFN_GROUP):
            cs = slice(c * FFN_CHUNK, (c + FFN_GROUP) * FFN_CHUNK)
            h = _dot(x, w1p_ref[:, cs]) + b1_ref[0, :, cs]
            g = jnp.concatenate([h[:, k * FFN_CHUNK:k * FFN_CHUNK + half] for k in range(FFN_GROUP)], axis=1)
            u = jnp.concatenate([h[:, k * FFN_CHUNK + half:(k + 1) * FFN_CHUNK] for k in range(FFN_GROUP)], axis=1)
            g = jnp.minimum(g, SWIGLU_LIMIT)
            u = jnp.clip(u, -SWIGLU_LIMIT, SWIGLU_LIMIT)
            act = g * jax.nn.sigmoid(SWIGLU_ALPHA * g) * (u + 1.0)
            act_ref[:n_rows, c * half:(c + FFN_GROUP) * half] = act.astype(act_ref.dtype)
        _to_tiles(_dot(act_ref[:n_rows, :], w2b_ref[...]) + b2_ref[0], y_ref)
        if n_rows * ROW_TILE < y_ref.shape[0]:
            y_ref[n_rows * ROW_TILE:, :] = jnp.zeros((y_ref.shape[0] - n_rows * ROW_TILE, LANES), y_ref.dtype)

    for tiles in range(1, FFN_STEP // FFN_TILE + 1):
        pl.when(n_sub == tiles)(functools.partial(compute, tiles * FFN_TILE))

    @pl.when(n_sub == 0)
    def _():
        y_ref[...] = jnp.zeros_like(y_ref)


def _unzip_bias(b1):
    e, f2 = b1.shape
    half = FFN_CHUNK // 2
    return b1.reshape(e, f2 // FFN_CHUNK, half, 2).transpose(0, 1, 3, 2).reshape(e, 1, f2)


def _ffn(xs, step_expert, step_rows, n_used, w1, b1, w2, b2):
    d, f2 = w1.shape[1:]
    n_steps = xs.shape[0] // (FFN_STEP * ROW_TILE)
    live = lambda i, nu: jnp.minimum(i, nu[0] - 1)
    wspec = lambda shape: pl.BlockSpec((1,) + shape, lambda i, te, nr, nu: (te[live(i, nu)], 0, 0))
    grid_spec = pltpu.PrefetchScalarGridSpec(
        num_scalar_prefetch=3,
        grid=(n_steps,),
        in_specs=[pl.BlockSpec((FFN_STEP * ROW_TILE, LANES), lambda i, te, nr, nu: (live(i, nu), 0)),
                  wspec((d, f2)), wspec((1, f2)), wspec((f2 // 2, d)), wspec((1, d))],
        out_specs=pl.BlockSpec((FFN_STEP * ROW_TILE, LANES), lambda i, te, nr, nu: (i, 0)),
        scratch_shapes=[pltpu.VMEM((d, f2), jnp.bfloat16), pltpu.VMEM((f2 // 2, d), jnp.bfloat16),
                        pltpu.VMEM((FFN_STEP, f2 // 2), jnp.bfloat16)],
    )
    return pl.pallas_call(
        _ffn_kernel,
        grid_spec=grid_spec,
        out_shape=jax.ShapeDtypeStruct(xs.shape, jnp.float32),
        compiler_params=_params("arbitrary"),
        name="expert_ffn",
    )(step_expert, step_rows, n_used, xs, w1, _unzip_bias(b1), w2, b2[:, None, :])


def _combine_kernel(*refs):
    tab_refs = refs[:LAND_SLOTS]
    lpos_ref, wts_ref, x1_ref, ga_ref, g_ref, y_ref, o_ref, buf_ref, sems = refs[LAND_SLOTS:]
    tab_ref = tab_refs[0]
    i = pl.program_id(0)
    n_steps = pl.num_programs(0)
    n_slots = buf_ref.shape[0]
    slot = i % n_slots
    tile = x1_ref.shape[0]
    n_local = buf_ref.shape[1] // ROW_TILE

    def fetch(tab, s):
        for e in range(N_EXPERTS):
            loc, glob, n = _piece_table(tab, e)

            @pl.when(n > 0)
            def _(loc=loc, glob=glob, n=n, e=e):
                pltpu.make_async_copy(y_ref.at[pl.ds(glob, n)], buf_ref.at[s, pl.ds(loc, n)],
                                      sems.at[s]).start(priority=e % 2)

    @pl.when(i == 0)
    def _():
        for s in range(n_slots - 1):
            pl.when(s < n_steps)(functools.partial(fetch, tab_refs[s], s))

    @pl.when(i + n_slots - 1 < n_steps)
    def _():
        fetch(tab_refs[n_slots - 1], (i + n_slots - 1) % n_slots)

    n = pl.multiple_of(tab_ref[3 * N_EXPERTS], ROW_TILE)

    @pl.when(n > 0)
    def _():
        pltpu.make_async_copy(y_ref.at[pl.ds(0, n)], buf_ref.at[slot, pl.ds(0, n)], sems.at[slot]).wait()

    rows = _from_tiles(buf_ref, n_local, (slot,)).astype(jnp.bfloat16)
    col = lax.broadcasted_iota(jnp.int32, (tile, n_local), 1)
    pos = lpos_ref[...]
    w = wts_ref[...]
    pick = jnp.where(col == pos[:, 0:1], w[:, 0:1], 0.0)
    for k in range(1, TOP_K):
        pick = pick + jnp.where(col == pos[:, k:k + 1], w[:, k:k + 1], 0.0)
    y = _dot(pick.astype(jnp.bfloat16), rows)
    o_ref[...] = x1_ref[...] + ga_ref[0] * _rms(y, g_ref[...])


def _combine(ys, tab_flat, lpos, wts, x1, mod3, g_post, *, seq):
    t, d = x1.shape
    tile = DISPATCH_TILE
    n_steps = t // tile
    per_seq = seq // tile
    n_local = tile * TOP_K
    return pl.pallas_call(
        _combine_kernel,
        grid=(n_steps,),
        in_specs=[pl.BlockSpec((TAB_WIDTH,), lambda i, s=s: (jnp.minimum(i + s, n_steps - 1),),
                               memory_space=pltpu.SMEM) for s in range(LAND_SLOTS)] + [
                  pl.BlockSpec((tile, TOP_K), lambda i: (i, 0)),
                  pl.BlockSpec((tile, TOP_K), lambda i: (i, 0)),
                  pl.BlockSpec((tile, d), lambda i: (i, 0)),
                  pl.BlockSpec((1, 1, d), lambda i: (i // per_seq, 0, 5)),
                  pl.BlockSpec((1, d), lambda i: (0, 0)),
                  pl.BlockSpec(memory_space=pl.ANY)],
        out_specs=pl.BlockSpec((tile, d), lambda i: (i, 0)),
        out_shape=jax.ShapeDtypeStruct((t, d), jnp.float32),
        scratch_shapes=[pltpu.VMEM((LAND_SLOTS, n_local * ROW_TILE, LANES), ys.dtype),
                        pltpu.SemaphoreType.DMA((LAND_SLOTS,))],
        compiler_params=_params("arbitrary"),
        name="combine",
    )(*([tab_flat] * LAND_SLOTS), lpos, wts, x1, mod3, g_post, ys)


def _rope_tables(seq):
    pos = np.arange(seq)
    n_freq = HEAD_DIM // 4
    freqs = ROPE_BASE ** (-jnp.arange(n_freq, dtype=jnp.float32) / n_freq)
    rows = jnp.asarray(pos // GRID_W, jnp.float32)[:, None] * freqs[None, :]
    cols = jnp.asarray(pos % GRID_W, jnp.float32)[:, None] * freqs[None, :]
    ang = jnp.concatenate([rows, rows, cols, cols], axis=1)
    sign = np.tile(np.repeat([-1.0, 1.0], n_freq), 2).astype(np.float32)
    cos = jnp.cos(ang)
    sin = jnp.sin(ang) * sign[None, :]
    reps = LANES // HEAD_DIM
    return jnp.tile(cos, (1, reps)), jnp.tile(sin, (1, reps))


def kernel(x, c, ctx, c_ctx, w_ada, b_ada, g_pre_mix, g_post_mix, g_pre_ffn, g_post_ffn, w_in, g_grp_a, g_grp_b,
           sink_a, rpb_b, w_out, w_router, b_router, w_mlp1, b_mlp1, w_mlp2, b_mlp2):
    batch, seq, d = x.shape
    n_ctx = ctx.shape[1]
    assert w_ada.shape[0] == 1, "single layer"
    assert seq % (ATT_BLOCKS * GRID_W * NB_Q_ROWS) == 0 and seq // GRID_W >= NB_K_ROWS
    assert seq % (ATT_BLOCKS * BLOCK_A) == 0 and seq >= BLOCK_A + 2 * WINDOW
    n_tok = batch * seq
    bf16 = jnp.bfloat16

    def pair_heads(a, axis):
        shape = a.shape
        a = a.reshape(shape[:axis] + (N_KV_A, GQA_GROUP, HEAD_DIM) + shape[axis + 1:])
        return jnp.swapaxes(a, axis, axis + 1).reshape(shape)

    mod_rows = -(-(batch + 1) // 8) * 8
    cc = jnp.zeros((mod_rows, d), jnp.float32).at[:batch].set(c).at[batch].set(c_ctx)
    mod3 = _ada(cc, w_ada[0], b_ada[0]).reshape(mod_rows, 1, 6 * d)

    w_in0 = w_in[0]
    w_lat = jnp.concatenate([pair_heads(w_in0[:, :WIDTH_A], 1), w_in0[:, WIDTH_A:]], axis=1).astype(bf16)
    kv_a_end = WIDTH_A + 2 * WIDTH_KV_A
    w_ctx = jnp.concatenate([w_in0[:, WIDTH_A:kv_a_end], w_in0[:, kv_a_end + WIDTH_B:]], axis=1).astype(bf16)
    g_pre = g_pre_mix[0].reshape(1, d)
    qa, ka, va, qb, kb, vb = _inproj(x.reshape(n_tok, d), mod3, g_pre, w_lat, _rope_tables(seq),
                                     seq=seq, mod_row0=0, latent=True)
    kac, vac, kbc, vbc = _inproj(ctx.reshape(batch * n_ctx, d), mod3, g_pre, w_ctx, None,
                                 seq=n_ctx, mod_row0=batch, latent=False)

    oa = _window_attention(qa, ka.reshape(batch, seq, -1), va.reshape(batch, seq, -1),
                           kac.reshape(batch, n_ctx, -1), vac.reshape(batch, n_ctx, -1),
                           sink_a[0].astype(jnp.float32) * LOG2E, batch=batch, seq=seq)
    tabs, cls_of_block = _nb_bias_tables(rpb_b[0], seq // GRID_W)
    ob = _neighbourhood_attention(qb, kb.reshape(batch, seq, -1), vb.reshape(batch, seq, -1),
                                  kbc.reshape(batch, n_ctx, -1), vbc.reshape(batch, n_ctx, -1),
                                  tabs, cls_of_block, batch=batch, seq=seq)

    w_out0 = w_out[0]
    w_o = jnp.concatenate([pair_heads(w_out0[:WIDTH_A], 0), w_out0[WIDTH_A:]], axis=0).astype(bf16)
    x1, h2, idx, wts, rank, size, base, counts = _outproj(
        oa, ob, x.reshape(n_tok, d), mod3,
        pair_heads(g_grp_a[0], 0).reshape(1, -1), g_grp_b[0].reshape(1, -1),
        g_post_mix[0].reshape(1, d), g_pre_ffn[0].reshape(1, d),
        w_o, w_router[0], b_router[0].reshape(1, -1), seq=seq)

    n_tok_tiles = n_tok // DISPATCH_TILE
    size = size.reshape(n_tok_tiles, N_EXPERTS)
    counts = counts.reshape(-1)
    padded = (counts + FFN_STEP - 1) // FFN_STEP * FFN_STEP
    pad_end = jnp.cumsum(padded).astype(jnp.int32)
    pad_start = pad_end - padded
    n_steps = n_tok * TOP_K // FFN_STEP + N_EXPERTS
    n_used = (pad_end[-1:] // FFN_STEP).astype(jnp.int32)
    step_row0 = jnp.arange(n_steps, dtype=jnp.int32) * FFN_STEP
    step_expert = jnp.minimum(jnp.sum(step_row0[:, None] >= pad_end[None, :], axis=1),
                              N_EXPERTS - 1).astype(jnp.int32)
    own = step_expert[:, None] == jnp.arange(N_EXPERTS, dtype=jnp.int32)
    step_rows = jnp.clip(jnp.sum(jnp.where(own, (pad_start + counts)[None, :], 0), axis=1) - step_row0,
                         0, FFN_STEP).astype(jnp.int32)
    fill_from = (pad_start + counts).astype(jnp.int32)
    local0 = jnp.cumsum(size, axis=1) - size
    global0 = pad_start[None, :] + base.reshape(n_tok_tiles, N_EXPERTS)
    tab = jnp.concatenate([local0, global0, size, jnp.sum(size, axis=1, keepdims=True),
                           jnp.zeros((n_tok_tiles, TAB_WIDTH - 3 * N_EXPERTS - 1), jnp.int32)], axis=1)
    tab = (tab * ROW_TILE).reshape(-1).astype(jnp.int32)
    hot = idx[:, :, None, :] == jnp.arange(N_EXPERTS, dtype=jnp.int32)[None, None, :, None]
    lpos_t = rank + jnp.sum(jnp.where(hot, local0[:, None, :, None], 0), axis=2)
    by_token = lambda a: jnp.swapaxes(a, 1, 2).reshape(n_tok, TOP_K)

    xs = _dispatch(h2, tab, lpos_t, fill_from * ROW_TILE, pad_end * ROW_TILE, n_rows=n_steps * FFN_STEP)
    ys = _ffn(xs, step_expert, step_rows, n_used, w_mlp1[0], b_mlp1[0], w_mlp2[0], b_mlp2[0])
    out = _combine(ys, tab, by_token(lpos_t), by_token(wts), x1, mod3, g_post_ffn[0].reshape(1, d), seq=seq)
    return out.reshape(batch, seq, d)
```
